```python
import jax
import jax.numpy as jnp
from jax import lax
import numpy as np

D_MODEL = 2048
BATCH = 1
SEQ = 16384
DEPTH = 1

HEAD_DIM = 128
NSA_HEADS = 8
NSA_GROUPS = 2
NSA_HPG = NSA_HEADS // NSA_GROUPS
SB_HEADS = 4
MEM_HEADS = 4
MEM_LEN = 256
CMP_LEN = 32
CMP_STRIDE = 16
CMP_HIDDEN = 2 * HEAD_DIM
SEL_BLOCK = 64
SEL_TOPK = 16
WINDOW = 512
Q_BLOCK = 128
ROPE_THETA = 10000.0
NORM_EPS = 1e-6
NEG_BIG = -1e30
N_BRANCH = 3
FFN_HIDDEN = ((8 * D_MODEL + 3 * 256 - 1) // (3 * 256)) * 256

NSA_Q_W = NSA_HEADS * HEAD_DIM
NSA_KV_W = NSA_GROUPS * HEAD_DIM
SB_W = SB_HEADS * HEAD_DIM
MEM_W = MEM_HEADS * HEAD_DIM
IN_SPLITS = (NSA_Q_W, 6 * NSA_KV_W, 3 * NSA_HEADS, 3 * SB_W, MEM_W, N_BRANCH * D_MODEL)
IN_WIDTH = sum(IN_SPLITS)

kernel_name = 'hybrid_nsa_stickbreak_memxattn_swiglu'


def _split_cols(a):
    outs, start = [], 0
    for w in IN_SPLITS:
        outs.append(a[..., start:start + w])
        start += w
    return outs


def rms_norm(x, g):
    xf = x.astype(jnp.float32)
    y = xf * lax.rsqrt(jnp.mean(xf * xf, axis=-1, keepdims=True) + NORM_EPS)
    return (y * g.astype(jnp.float32)).astype(x.dtype)


def rope_tables(pos):
    inv = 1.0 / (ROPE_THETA ** (jnp.arange(0, HEAD_DIM, 2, dtype=jnp.float32) / HEAD_DIM))
    ang = pos.astype(jnp.float32)[..., None] * inv
    return jnp.cos(ang), jnp.sin(ang)


def apply_rope(x, cos, sin):
    x1, x2 = jnp.split(x.astype(jnp.float32), 2, axis=-1)
    c, s = cos[:, :, None, :], sin[:, :, None, :]
    return jnp.concatenate([x1 * c - x2 * s, x2 * c + x1 * s], axis=-1).astype(x.dtype)


def compress_blocks(t, pe, w1, w2):
    B, T, G, dh = t.shape
    chunks = t.reshape(B, T // CMP_STRIDE, CMP_STRIDE, G, dh)
    blocks = jnp.concatenate([chunks[:, :-1], chunks[:, 1:]], axis=2) + pe[:, None, :]
    hid = jax.nn.gelu(jnp.einsum('bnlgd,ldf->bngf', blocks, w1))
    return jnp.einsum('bngf,fd->bngd', hid, w2)


def masked_softmax(s, mask):
    return jax.nn.softmax(jnp.where(mask, s, NEG_BIG), axis=-1)


def nsa_mixer(q_flat, kv_flat, gate_flat, positions, cos, sin, q_norm, kc_norm, ks_norm, kw_norm,
              ck_pe, ck_w1, ck_w2, cv_pe, cv_w1, cv_w2):
    B, T, _ = q_flat.shape
    G, Hg, dh = NSA_GROUPS, NSA_HPG, HEAD_DIM
    n_cmp = T // CMP_STRIDE - 1
    n_sel = T // SEL_BLOCK
    top_k = min(SEL_TOPK, n_sel)
    scale = HEAD_DIM ** -0.5
    dtype = q_flat.dtype
    f32 = jnp.float32

    q = apply_rope(rms_norm(q_flat.reshape(B, T, NSA_HEADS, dh), q_norm), cos, sin)
    qg = q.reshape(B, T, G, Hg, dh).transpose(0, 2, 3, 1, 4)
    gg = jax.nn.sigmoid(gate_flat.reshape(B, T, G, Hg, 3)).transpose(0, 2, 3, 1, 4)
    kc, vc, ks, vs, kw, vw = [a.reshape(B, T, G, dh) for a in jnp.split(kv_flat, 6, axis=-1)]

    cos_c, sin_c = rope_tables(positions[:, CMP_LEN - 1::CMP_STRIDE])
    k_c = apply_rope(rms_norm(compress_blocks(kc, ck_pe, ck_w1, ck_w2), kc_norm), cos_c, sin_c)
    k_c = k_c.transpose(0, 2, 1, 3)
    v_c = compress_blocks(vc, cv_pe, cv_w1, cv_w2).transpose(0, 2, 1, 3)
    k_s = apply_rope(rms_norm(ks, ks_norm), cos, sin).transpose(0, 2, 1, 3).reshape(B, G, n_sel, SEL_BLOCK, dh)
    v_s = vs.transpose(0, 2, 1, 3).reshape(B, G, n_sel, SEL_BLOCK, dh)
    pad = ((0, 0), (0, 0), (WINDOW, 0), (0, 0))
    k_w = jnp.pad(apply_rope(rms_norm(kw, kw_norm), cos, sin).transpose(0, 2, 1, 3), pad)
    v_w = jnp.pad(vw.transpose(0, 2, 1, 3), pad)

    cmp_start = jnp.arange(n_cmp) * CMP_STRIDE
    cmp_end = cmp_start + (CMP_LEN - 1)
    sel_start = jnp.arange(n_sel) * SEL_BLOCK
    overlap = ((cmp_start[:, None] < sel_start[None, :] + SEL_BLOCK)
               & (cmp_end[:, None] >= sel_start[None, :])).astype(f32)
    blk_ids = jnp.arange(n_sel)
    b_ix = jnp.arange(B)[:, None, None, None]
    g_ix = jnp.arange(G)[None, :, None, None]
    sel_off = jnp.arange(SEL_BLOCK)
    win_off = jnp.arange(WINDOW + Q_BLOCK) - WINDOW

    def block(i):
        q0 = i * Q_BLOCK
        qb = lax.dynamic_slice_in_dim(qg, q0, Q_BLOCK, axis=3)
        gb = lax.dynamic_slice_in_dim(gg, q0, Q_BLOCK, axis=3)
        t = q0 + jnp.arange(Q_BLOCK)
        s_c = jnp.einsum('bghqd,bgnd->bghqn', qb, k_c, preferred_element_type=f32) * scale
        valid_c = cmp_end[None, :] <= t[:, None]
        p_c = masked_softmax(s_c, valid_c) * valid_c
        o_c = jnp.einsum('bghqn,bgnd->bghqd', p_c.astype(dtype), v_c)
        imp = jnp.einsum('bgqn,ns->bgqs', p_c.sum(axis=2), overlap)
        cur = t // SEL_BLOCK
        forced = ((blk_ids[None, :] == 0) | (blk_ids[None, :] == cur[:, None])
                  | (blk_ids[None, :] == cur[:, None] - 1))
        future = sel_start[None, :] > t[:, None]
        imp = jnp.where(forced, jnp.inf, jnp.where(future, -jnp.inf, imp))
        _, idx = lax.top_k(imp, top_k)
        k_sel = k_s[b_ix, g_ix, idx].reshape(B, G, Q_BLOCK, top_k * SEL_BLOCK, dh)
        v_sel = v_s[b_ix, g_ix, idx].reshape(B, G, Q_BLOCK, top_k * SEL_BLOCK, dh)
        tok = (idx[..., None] * SEL_BLOCK + sel_off).reshape(B, G, Q_BLOCK, top_k * SEL_BLOCK)
        valid_s = (tok <= t[:, None])[:, :, None]
        s_s = jnp.einsum('bghqd,bgqkd->bghqk', qb, k_sel, preferred_element_type=f32) * scale
        o_s = jnp.einsum('bghqk,bgqkd->bghqd', masked_softmax(s_s, valid_s).astype(dtype), v_sel)
        k_wb = lax.dynamic_slice_in_dim(k_w, q0, WINDOW + Q_BLOCK, axis=2)
        v_wb = lax.dynamic_slice_in_dim(v_w, q0, WINDOW + Q_BLOCK, axis=2)
        kpos = q0 + win_off
        valid_w = ((kpos[None, :] <= t[:, None]) & (kpos[None, :] > t[:, None] - WINDOW)
                   & (kpos[None, :] >= 0))
        s_w = jnp.einsum('bghqd,bgkd->bghqk', qb, k_wb, preferred_element_type=f32) * scale
        o_w = jnp.einsum('bghqk,bgkd->bghqd', masked_softmax(s_w, valid_w).astype(dtype), v_wb)
        return gb[..., 0:1] * o_c + gb[..., 1:2] * o_s + gb[..., 2:3] * o_w

    out = lax.map(block, jnp.arange(T // Q_BLOCK))
    return out.transpose(1, 0, 4, 2, 3, 5).reshape(B, T, NSA_Q_W)


def stick_breaking_mixer(qkv_flat):
    B, T, _ = qkv_flat.shape
    q, k, v = [a.reshape(B, T, SB_HEADS, HEAD_DIM).transpose(0, 2, 1, 3)
               for a in jnp.split(qkv_flat, 3, axis=-1)]
    scale = HEAD_DIM ** -0.5
    key_idx = jnp.arange(T)

    def block(i):
        q0 = i * Q_BLOCK
        qb = lax.dynamic_slice_in_dim(q, q0, Q_BLOCK, axis=2)
        t = q0 + jnp.arange(Q_BLOCK)
        z = jnp.einsum('bhqd,bhkd->bhqk', qb, k, preferred_element_type=jnp.float32) * scale
        past = key_idx[None, :] < t[:, None]
        log_keep = jnp.where(past, jax.nn.log_sigmoid(-z), 0.0)
        log_between = lax.cumsum(log_keep, axis=3, reverse=True) - log_keep
        weight = jnp.where(past, jnp.exp(jax.nn.log_sigmoid(z) + log_between), 0.0)
        return jnp.einsum('bhqk,bhkd->bhqd', weight.astype(v.dtype), v)

    out = lax.map(block, jnp.arange(T // Q_BLOCK))
    return out.transpose(1, 0, 3, 2, 4).reshape(B, T, SB_W)


def memory_cross_attention(q_flat, mem, mem_norm, w_mem_kv, q_norm, k_norm):
    B, T, _ = q_flat.shape
    M = mem.shape[1]
    kv = rms_norm(mem, mem_norm) @ w_mem_kv
    k, v = [a.reshape(B, M, MEM_HEADS, HEAD_DIM) for a in jnp.split(kv, 2, axis=-1)]
    k = rms_norm(k, k_norm)
    q = rms_norm(q_flat.reshape(B, T, MEM_HEADS, HEAD_DIM), q_norm)
    s = jnp.einsum('bthd,bmhd->bhtm', q, k, preferred_element_type=jnp.float32) * (HEAD_DIM ** -0.5)
    p = jax.nn.softmax(s, axis=-1)
    return jnp.einsum('bhtm,bmhd->bthd', p.astype(v.dtype), v).reshape(B, T, MEM_W)


def setup_inputs(seed: int = 0) -> dict:
    key = jax.random.key(seed)
    ks = jax.random.split(key, 27)
    f32 = jnp.float32

    def dense(k, shape, fan_in):
        return jax.random.normal(k, (DEPTH,) + shape, f32) * (fan_in ** -0.5)

    def gain(k, n):
        return 1.0 + 0.01 * jax.random.normal(k, (DEPTH, n), f32)

    x = jax.random.normal(ks[0], (BATCH, SEQ, D_MODEL), f32)
    mem = jax.random.normal(ks[1], (BATCH, MEM_LEN, D_MODEL), f32)
    positions = (jax.random.randint(ks[2], (BATCH, 1), 0, 1024, dtype=jnp.int32)
                 + jnp.arange(SEQ, dtype=jnp.int32)[None, :])
    return {
        'x': x,
        'mem': mem,
        'positions': positions,
        'attn_norm': gain(ks[3], D_MODEL),
        'w_in': dense(ks[4], (D_MODEL, IN_WIDTH), D_MODEL),
        'nsa_q_norm': gain(ks[5], HEAD_DIM),
        'nsa_kc_norm': gain(ks[6], HEAD_DIM),
        'nsa_ks_norm': gain(ks[7], HEAD_DIM),
        'nsa_kw_norm': gain(ks[8], HEAD_DIM),
        'cmp_k_pe': 0.02 * jax.random.normal(ks[9], (DEPTH, CMP_LEN, HEAD_DIM), f32),
        'cmp_k_w1': dense(ks[10], (CMP_LEN, HEAD_DIM, CMP_HIDDEN), CMP_LEN * HEAD_DIM),
        'cmp_k_w2': dense(ks[11], (CMP_HIDDEN, HEAD_DIM), CMP_HIDDEN),
        'cmp_v_pe': 0.02 * jax.random.normal(ks[12], (DEPTH, CMP_LEN, HEAD_DIM), f32),
        'cmp_v_w1': dense(ks[13], (CMP_LEN, HEAD_DIM, CMP_HIDDEN), CMP_LEN * HEAD_DIM),
        'cmp_v_w2': dense(ks[14], (CMP_HIDDEN, HEAD_DIM), CMP_HIDDEN),
        'mem_norm': gain(ks[15], D_MODEL),
        'w_mem_kv': dense(ks[16], (D_MODEL, 2 * MEM_W), D_MODEL),
        'mem_q_norm': gain(ks[17], HEAD_DIM),
        'mem_k_norm': gain(ks[18], HEAD_DIM),
        'w_o_nsa': dense(ks[19], (NSA_Q_W, D_MODEL), NSA_Q_W),
        'w_o_sb': dense(ks[20], (SB_W, D_MODEL), SB_W),
        'w_o_mem': dense(ks[21], (MEM_W, D_MODEL), MEM_W),
        'w_out': dense(ks[22], (D_MODEL, D_MODEL), D_MODEL),
        'ffn_norm': gain(ks[23], D_MODEL),
        'w_ffn_gate': dense(ks[24], (D_MODEL, FFN_HIDDEN), D_MODEL),
        'w_ffn_up': dense(ks[25], (D_MODEL, FFN_HIDDEN), D_MODEL),
        'w_ffn_down': dense(ks[26], (FFN_HIDDEN, D_MODEL), FFN_HIDDEN),
    }


def reference(x, mem, positions, attn_norm, w_in, nsa_q_norm, nsa_kc_norm, nsa_ks_norm, nsa_kw_norm,
              cmp_k_pe, cmp_k_w1, cmp_k_w2, cmp_v_pe, cmp_v_w1, cmp_v_w2, mem_norm, w_mem_kv,
              mem_q_norm, mem_k_norm, w_o_nsa, w_o_sb, w_o_mem, w_out, ffn_norm,
              w_ffn_gate, w_ffn_up, w_ffn_down):
    B, T, D = x.shape
    cos, sin = rope_tables(positions)
    for l in range(DEPTH):
        h = rms_norm(x, attn_norm[l])
        q_nsa, kv_nsa, g_nsa, qkv_sb, q_mem, g_merge = _split_cols(h @ w_in[l])
        y_nsa = nsa_mixer(q_nsa, kv_nsa, g_nsa, positions, cos, sin, nsa_q_norm[l], nsa_kc_norm[l],
                          nsa_ks_norm[l], nsa_kw_norm[l], cmp_k_pe[l], cmp_k_w1[l], cmp_k_w2[l],
                          cmp_v_pe[l], cmp_v_w1[l], cmp_v_w2[l])
        y_sb = stick_breaking_mixer(qkv_sb)
        y_mem = memory_cross_attention(q_mem, mem, mem_norm[l], w_mem_kv[l], mem_q_norm[l], mem_k_norm[l])
        g = jax.nn.sigmoid(g_merge).reshape(B, T, N_BRANCH, D)
        mixed = (g[:, :, 0] * (y_nsa @ w_o_nsa[l]) + g[:, :, 1] * (y_sb @ w_o_sb[l])
                 + g[:, :, 2] * (y_mem @ w_o_mem[l]))
        x = x + mixed @ w_out[l]
        h = rms_norm(x, ffn_norm[l])
        x = x + (jax.nn.silu(h @ w_ffn_gate[l]) * (h @ w_ffn_up[l])) @ w_ffn_down[l]
    return x
```

```python
import functools

import numpy as np
import jax
import jax.numpy as jnp
from jax import lax
from jax.experimental import pallas as pl
from jax.experimental.pallas import tpu as pltpu

HEAD_DIM = 128
NSA_HEADS = 8
NSA_GROUPS = 2
NSA_HPG = NSA_HEADS // NSA_GROUPS
SB_HEADS = 4
MEM_HEADS = 4
CMP_LEN = 32
CMP_STRIDE = 16
CMP_HIDDEN = 2 * HEAD_DIM
SEL_BLOCK = 64
SEL_TOPK = 16
WINDOW = 512
ROPE_THETA = 10000.0
NORM_EPS = 1e-6
NEG_BIG = -1e30
N_BRANCH = 3
SCALE = HEAD_DIM ** -0.5

LANE = 128
MIB = 1 << 20
BF16 = jnp.bfloat16
F32 = jnp.float32

_NT = (((1,), (1,)), ((), ()))


def _tile(n, pref):
    t = min(n, pref)
    assert n % t == 0, (n, pref)
    return t


def _params(sem, vmem_mib):
    return pltpu.CompilerParams(dimension_semantics=sem, vmem_limit_bytes=vmem_mib * MIB)


def _rms(x, gain):
    return x * lax.rsqrt(jnp.mean(x * x, axis=-1, keepdims=True) + NORM_EPS) * gain


def _rope_tables(pos, inv2, sgn):
    ang = pos * inv2
    return jnp.cos(ang), jnp.sin(ang) * sgn


def _rope(x, c, s):
    return x * c + pltpu.roll(x, HEAD_DIM // 2, 1) * s


def _proj_kernel(x_ref, g_ref, w_ref, o_ref, hn_ref):
    @pl.when(pl.program_id(1) == 0)
    def _():
        hn_ref[...] = _rms(x_ref[...], g_ref[...]).astype(BF16)

    o_ref[...] = jnp.dot(hn_ref[...], w_ref[...], preferred_element_type=F32)


def _in_proj(x, gain, w_bf16):
    T, D = x.shape
    N = w_bf16.shape[1]
    tm = _tile(T, 1024)
    tn = 640
    assert N % tn == 0
    return pl.pallas_call(
        _proj_kernel,
        grid=(T // tm, N // tn),
        in_specs=[
            pl.BlockSpec((tm, D), lambda i, j: (i, 0)),
            pl.BlockSpec((1, D), lambda i, j: (0, 0)),
            pl.BlockSpec((D, tn), lambda i, j: (0, j)),
        ],
        out_specs=pl.BlockSpec((tm, tn), lambda i, j: (i, j)),
        out_shape=jax.ShapeDtypeStruct((T, N), F32),
        scratch_shapes=[pltpu.VMEM((tm, D), BF16)],
        compiler_params=_params(("parallel", "arbitrary"), 48),
        name="in_proj",
    )(x, gain, w_bf16)


def _prep_kernel(pos_ref, inv_ref, sgn_ref, gq_ref, gks_ref, gkw_ref, gmq_ref,
                 q_ref, kc_ref, vc_ref, ks_ref, vs_ref, kw_ref, vw_ref,
                 sq_ref, sk_ref, sv_ref, mq_ref, gn_ref,
                 qn_o, kc_o, vc_o, ks_o, kw_o, vsT_o, vwT_o, sq_o, sk_o, svT_o, mq_o, gT_o):
    c, s = _rope_tables(pos_ref[...], inv_ref[...], sgn_ref[...])
    hd = HEAD_DIM
    for h in range(NSA_HEADS):
        sl = slice(h * hd, (h + 1) * hd)
        qn_o[:, sl] = (_rope(_rms(q_ref[:, sl], gq_ref[...]), c, s) * SCALE).astype(BF16)
    for g in range(NSA_GROUPS):
        sl = slice(g * hd, (g + 1) * hd)
        kc_o[g] = kc_ref[:, sl].astype(BF16)
        vc_o[g] = vc_ref[:, sl].astype(BF16)
        ks_o[:, sl] = _rope(_rms(ks_ref[:, sl], gks_ref[...]), c, s).astype(BF16)
        kw_o[:, sl] = _rope(_rms(kw_ref[:, sl], gkw_ref[...]), c, s).astype(BF16)
        vsT_o[sl, :] = vs_ref[:, sl].T.astype(BF16)
        vwT_o[sl, :] = vw_ref[:, sl].T.astype(BF16)
    for h in range(SB_HEADS):
        sl = slice(h * hd, (h + 1) * hd)
        sq_o[:, sl] = (sq_ref[:, sl] * SCALE).astype(BF16)
        svT_o[sl, :] = sv_ref[:, sl].T.astype(BF16)
    sk_o[...] = sk_ref[...].astype(BF16)
    for h in range(MEM_HEADS):
        sl = slice(h * hd, (h + 1) * hd)
        mq_o[:, sl] = (_rms(mq_ref[:, sl], gmq_ref[...]) * SCALE).astype(BF16)
    gT_o[...] = jax.nn.sigmoid(gn_ref[...]).T


def _prep(P, pos_col, inv2, sgn, gq, gks, gkw, gmq):
    T = P.shape[0]
    tp = _tile(T, 512)
    hd = HEAD_DIM
    row = lambda w, c: pl.BlockSpec((tp, w), lambda i, c=c: (i, c))
    const = lambda: pl.BlockSpec((1, hd), lambda i: (0, 0))
    in_specs = [pl.BlockSpec((tp, 1), lambda i: (i, 0)), const(), const(), const(), const(), const(), const(),
                row(8 * hd, 0),
                row(2 * hd, 4), row(2 * hd, 5),
                row(2 * hd, 6), row(2 * hd, 7),
                row(2 * hd, 8), row(2 * hd, 9),
                row(4 * hd, 5), row(4 * hd, 6), row(4 * hd, 7),
                row(4 * hd, 8),
                row(hd, 84)]
    colT = lambda w: pl.BlockSpec((w, tp), lambda i: (0, i))
    out_specs = [row(8 * hd, 0),
                 pl.BlockSpec((NSA_GROUPS, tp, hd), lambda i: (0, i, 0)),
                 pl.BlockSpec((NSA_GROUPS, tp, hd), lambda i: (0, i, 0)),
                 row(2 * hd, 0), row(2 * hd, 0), colT(2 * hd), colT(2 * hd),
                 row(4 * hd, 0), row(4 * hd, 0), colT(4 * hd), row(4 * hd, 0), colT(hd)]
    sds = jax.ShapeDtypeStruct
    out_shape = [sds((T, 8 * hd), BF16),
                 sds((NSA_GROUPS, T, hd), BF16), sds((NSA_GROUPS, T, hd), BF16),
                 sds((T, 2 * hd), BF16), sds((T, 2 * hd), BF16), sds((2 * hd, T), BF16), sds((2 * hd, T), BF16),
                 sds((T, 4 * hd), BF16), sds((T, 4 * hd), BF16), sds((4 * hd, T), BF16), sds((T, 4 * hd), BF16),
                 sds((hd, T), F32)]
    return pl.pallas_call(
        _prep_kernel, grid=(T // tp,), in_specs=in_specs, out_specs=out_specs, out_shape=out_shape,
        compiler_params=_params(("parallel",), 48), name="prep",
    )(pos_col, inv2, sgn, gq, gks, gkw, gmq, *([P] * 12))


def _gelu_tanh(x):
    return 0.5 * x * (1.0 + jnp.tanh(0.7978845608028654 * (x + 0.044715 * (x * x * x))))


def _compress_one(x, w1, pe, w2):
    n = x.shape[0]
    ab = jnp.dot(x, w1, preferred_element_type=F32)
    pr = jnp.dot(pe, w1, preferred_element_type=F32)
    pec = pr[0:1, :CMP_HIDDEN] + pr[1:2, CMP_HIDDEN:]
    hid = ab[:, :CMP_HIDDEN] + pltpu.roll(ab[:, CMP_HIDDEN:], n - 1, 0) + pec
    return jnp.dot(_gelu_tanh(hid).astype(BF16), w2, preferred_element_type=F32)


def _compress_kernel(xk_ref, xv_ref, w1k_ref, w1v_ref, pek_ref, pev_ref, w2k_ref, w2v_ref,
                     gk_ref, pos_ref, inv_ref, sgn_ref, kc_o, vcT_o):
    c, s = _rope_tables(pos_ref[...], inv_ref[...], sgn_ref[...])
    k = _compress_one(xk_ref[0], w1k_ref[...], pek_ref[...], w2k_ref[...])
    kc_o[0] = _rope(_rms(k, gk_ref[...]), c, s).astype(BF16)
    v = _compress_one(xv_ref[0], w1v_ref[...], pev_ref[...], w2v_ref[...])
    vcT_o[0] = v.T.astype(BF16)


def _compress(xk, xv, w1k, w1v, pek, pev, w2k, w2v, gk, posc, inv2, sgn):
    G, n, W = xk.shape
    full = lambda a: pl.BlockSpec(a.shape, lambda g: (0,) * a.ndim)
    grp = pl.BlockSpec((1, n, W), lambda g: (g, 0, 0))
    return pl.pallas_call(
        _compress_kernel, grid=(G,),
        in_specs=[grp, grp, full(w1k), full(w1v), full(pek), full(pev), full(w2k), full(w2v),
                  full(gk), full(posc), full(inv2), full(sgn)],
        out_specs=[pl.BlockSpec((1, n, HEAD_DIM), lambda g: (g, 0, 0)),
                   pl.BlockSpec((1, HEAD_DIM, n), lambda g: (g, 0, 0))],
        out_shape=[jax.ShapeDtypeStruct((G, n, HEAD_DIM), BF16), jax.ShapeDtypeStruct((G, HEAD_DIM, n), BF16)],
        compiler_params=_params(("parallel",), 48), name="compress",
    )(xk, xv, w1k, w1v, pek, pev, w2k, w2v, gk, posc, inv2, sgn)


def _memkv_kernel(mem_ref, gm_ref, w_ref, gk_ref, k_o, vT_o):
    hn = _rms(mem_ref[...], gm_ref[...]).astype(BF16)
    kv = jnp.dot(hn, w_ref[...], preferred_element_type=F32)
    mw = MEM_HEADS * HEAD_DIM
    for h in range(MEM_HEADS):
        sl = slice(h * HEAD_DIM, (h + 1) * HEAD_DIM)
        k_o[:, sl] = _rms(kv[:, sl], gk_ref[...]).astype(BF16)
        vT_o[sl, :] = kv[:, mw + h * HEAD_DIM: mw + (h + 1) * HEAD_DIM].T.astype(BF16)


def _memkv(mem, gm, w, gk):
    M = mem.shape[0]
    mw = MEM_HEADS * HEAD_DIM
    return pl.pallas_call(
        _memkv_kernel,
        out_shape=[jax.ShapeDtypeStruct((M, mw), BF16), jax.ShapeDtypeStruct((mw, M), BF16)],
        compiler_params=pltpu.CompilerParams(vmem_limit_bytes=32 * MIB), name="mem_kv",
    )(mem, gm, w, gk)


def _cmp_kernel(q_ref, kc_ref, vcT_ref, ovT_ref, g_ref, y_ref, b_ref, *, tq, n_pad, n_sel, top_k):
    i = pl.program_id(1)
    t = i * tq + lax.broadcasted_iota(jnp.int32, (1, tq), 1)
    n_end = lax.broadcasted_iota(jnp.int32, (n_pad, 1), 0) * CMP_STRIDE + (CMP_LEN - 1)
    valid = n_end <= t
    kc = kc_ref[0]
    vcT = vcT_ref[0]
    psum = jnp.zeros((n_pad, tq), F32)
    for h in range(NSA_HPG):
        q_h = q_ref[:, h * HEAD_DIM:(h + 1) * HEAD_DIM]
        sm = jnp.where(valid, lax.dot_general(kc, q_h, _NT, preferred_element_type=F32), NEG_BIG)
        m = jnp.max(sm, axis=0, keepdims=True)
        e = jnp.where(valid, jnp.exp(sm - m), 0.0)
        l = jnp.sum(e, axis=0, keepdims=True)
        p = e * (1.0 / jnp.where(l > 0.0, l, 1.0))
        oT = jnp.dot(vcT, p.astype(BF16), preferred_element_type=F32)
        y_ref[:, h * HEAD_DIM:(h + 1) * HEAD_DIM] = (oT * g_ref[0, 0, h:h + 1, :]).T.astype(BF16)
        psum = psum + p
    hi = psum.astype(BF16)
    r1 = psum - hi.astype(F32)
    mid = r1.astype(BF16)
    lo = (r1 - mid.astype(F32)).astype(BF16)
    parts = jnp.dot(ovT_ref[...], jnp.concatenate([hi, mid, lo], axis=1), preferred_element_type=F32)
    imp = parts[:, :tq] + parts[:, tq:2 * tq] + parts[:, 2 * tq:]
    s_i = lax.broadcasted_iota(jnp.int32, (n_sel, 1), 0)
    cur = lax.shift_right_logical(t, 6)
    forced = (s_i == 0) | (s_i == cur) | (s_i == cur - 1)
    future = s_i * SEL_BLOCK > t
    w = jnp.where(forced, jnp.inf, jnp.where(future, -jnp.inf, imp))
    s_f = jnp.broadcast_to(s_i.astype(F32), (n_sel, tq))
    sel = jnp.zeros((n_sel, tq), F32)
    for _ in range(top_k):
        m = jnp.max(w, axis=0, keepdims=True)
        idx = jnp.min(jnp.where(w == m, s_f, float(n_sel)), axis=0, keepdims=True)
        pick = s_f == idx
        sel = jnp.where(pick, 1.0, sel)
        w = jnp.where(pick, -jnp.inf, w)
    b_ref[0] = jnp.where(sel > 0.0, 0.0, NEG_BIG)


def _cmp_select(qn, kc, vcT, ovT, gates):
    T = qn.shape[0]
    G, n_pad, _ = kc.shape
    n_sel = T // SEL_BLOCK
    tq = _tile(T, 256)
    gw = NSA_HPG * HEAD_DIM
    kern = functools.partial(_cmp_kernel, tq=tq, n_pad=n_pad, n_sel=n_sel, top_k=min(SEL_TOPK, n_sel))
    return pl.pallas_call(
        kern, grid=(G, T // tq),
        in_specs=[pl.BlockSpec((tq, gw), lambda g, i: (i, g)),
                  pl.BlockSpec((1, n_pad, HEAD_DIM), lambda g, i: (g, 0, 0)),
                  pl.BlockSpec((1, HEAD_DIM, n_pad), lambda g, i: (g, 0, 0)),
                  pl.BlockSpec((n_sel, n_pad), lambda g, i: (0, 0)),
                  pl.BlockSpec((1, 1, 8, tq), lambda g, i: (0, g, 0, i))],
        out_specs=[pl.BlockSpec((tq, gw), lambda g, i: (i, g)),
                   pl.BlockSpec((1, n_sel, tq), lambda g, i: (g, 0, i))],
        out_shape=[jax.ShapeDtypeStruct((T, NSA_HEADS * HEAD_DIM), BF16),
                   jax.ShapeDtypeStruct((G, n_sel, T), F32)],
        compiler_params=_params(("parallel", "parallel"), 48), name="cmp_select",
    )(qn, kc, vcT, ovT, gates)


def _flash_kernel(qi_ref, kj_ref, fl_ref, q_ref, k_ref, vT_ref, g_ref, *rest, mode, tq, tk):
    if mode == "sel":
        bias_ref, o_ref, m_sc, l_sc, acc_sc = rest
    else:
        o_ref, m_sc, l_sc, acc_sc = rest
    p = pl.program_id(1)
    i = qi_ref[p]
    j = kj_ref[p]
    fl = fl_ref[p]

    @pl.when((fl & 1) != 0)
    def _():
        m_sc[...] = jnp.full(m_sc.shape, NEG_BIG, F32)
        l_sc[...] = jnp.zeros(l_sc.shape, F32)
        acc_sc[...] = jnp.zeros(acc_sc.shape, F32)

    t = i * tq + lax.broadcasted_iota(jnp.int32, (1, tq), 1)
    kpos = j * tk + lax.broadcasted_iota(jnp.int32, (tk, 1), 0)
    if mode == "sel":
        nb = tk // SEL_BLOCK
        base = jnp.concatenate(
            [jnp.broadcast_to(bias_ref[0, b:b + 1, :], (SEL_BLOCK, tq)) for b in range(nb)], axis=0)
        mask_add = jnp.where(kpos <= t, base, NEG_BIG)
    else:
        mask_add = jnp.where((kpos <= t) & (kpos > t - WINDOW), 0.0, NEG_BIG)
    k = k_ref[...]
    vT = vT_ref[...]
    for h in range(NSA_HPG):
        q_h = q_ref[:, h * HEAD_DIM:(h + 1) * HEAD_DIM]
        sT = lax.dot_general(k, q_h, _NT, preferred_element_type=F32) + mask_add
        m_old = m_sc[h]
        m_new = jnp.maximum(m_old, jnp.max(sT, axis=0, keepdims=True))
        alpha = jnp.exp(m_old - m_new)
        pT = jnp.exp(sT - m_new)
        l_sc[h] = alpha * l_sc[h] + jnp.sum(pT, axis=0, keepdims=True)
        acc_sc[h] = alpha * acc_sc[h] + jnp.dot(vT, pT.astype(BF16), preferred_element_type=F32)
        m_sc[h] = m_new

    @pl.when((fl & 2) != 0)
    def _():
        for h in range(NSA_HPG):
            o = acc_sc[h] * ((1.0 / l_sc[h]) * g_ref[0, 0, h:h + 1, :])
            o_ref[:, h * HEAD_DIM:(h + 1) * HEAD_DIM] = o.T.astype(BF16)


def _steps(nq, lo_fn, hi_fn, reverse=False):
    qi, kj, fl = [], [], []
    for i in range(nq):
        js = list(range(lo_fn(i), hi_fn(i) + 1))
        if reverse:
            js = js[::-1]
        for n, j in enumerate(js):
            qi.append(i)
            kj.append(j)
            fl.append((1 if n == 0 else 0) | (2 if n == len(js) - 1 else 0))
    return (jnp.asarray(np.array(qi, np.int32)), jnp.asarray(np.array(kj, np.int32)),
            jnp.asarray(np.array(fl, np.int32)))


def _nsa_flash(mode, qn, k, vT, gates, bias=None):
    T = qn.shape[0]
    tq = _tile(T, 512)
    tk = _tile(T, 512)
    nq = T // tq
    gw = NSA_HPG * HEAD_DIM
    hi = lambda i: ((i + 1) * tq - 1) // tk
    if mode == "sel":
        lo = lambda i: 0
        br = 1
    else:
        lo = lambda i: max(0, (i * tq - (WINDOW - 1)) // tk)
        br = 2
    qi, kj, fl = _steps(nq, lo, hi)
    in_specs = [pl.BlockSpec((tq, gw), lambda g, p, qi, kj, fl: (qi[p], g)),
                pl.BlockSpec((tk, HEAD_DIM), lambda g, p, qi, kj, fl: (kj[p], g)),
                pl.BlockSpec((HEAD_DIM, tk), lambda g, p, qi, kj, fl: (g, kj[p])),
                pl.BlockSpec((1, 1, 8, tq), lambda g, p, qi, kj, fl, br=br: (br, g, 0, qi[p]))]
    args = [qn, k, vT, gates]
    if mode == "sel":
        in_specs.append(pl.BlockSpec((1, tk // SEL_BLOCK, tq), lambda g, p, qi, kj, fl: (g, kj[p], qi[p])))
        args.append(bias)
    kern = functools.partial(_flash_kernel, mode=mode, tq=tq, tk=tk)
    return pl.pallas_call(
        kern,
        grid_spec=pltpu.PrefetchScalarGridSpec(
            num_scalar_prefetch=3, grid=(NSA_GROUPS, int(qi.shape[0])),
            in_specs=in_specs,
            out_specs=pl.BlockSpec((tq, gw), lambda g, p, qi, kj, fl: (qi[p], g)),
            scratch_shapes=[pltpu.VMEM((NSA_HPG, 1, tq), F32), pltpu.VMEM((NSA_HPG, 1, tq), F32),
                            pltpu.VMEM((NSA_HPG, HEAD_DIM, tq), F32)]),
        out_shape=jax.ShapeDtypeStruct((T, NSA_HEADS * HEAD_DIM), BF16),
        compiler_params=_params(("parallel", "arbitrary"), 48), name="nsa_" + mode,
    )(qi, kj, fl, *args)


def _sb_kernel(qi_ref, kj_ref, fl_ref, q_ref, k_ref, vT_ref, lm_ref, o_ref, carry_sc, acc_sc, *, tq, tk):
    p = pl.program_id(0)
    i = qi_ref[p]
    j = kj_ref[p]
    fl = fl_ref[p]

    @pl.when((fl & 1) != 0)
    def _():
        carry_sc[...] = jnp.zeros(carry_sc.shape, F32)
        acc_sc[...] = jnp.zeros(acc_sc.shape, F32)

    t = i * tq + lax.broadcasted_iota(jnp.int32, (1, tq), 1)
    kpos = j * tk + lax.broadcasted_iota(jnp.int32, (tk, 1), 0)
    past = kpos < t
    lm = lm_ref[...]
    for h in range(SB_HEADS):
        sl = slice(h * HEAD_DIM, (h + 1) * HEAD_DIM)
        z = lax.dot_general(k_ref[:, sl], q_ref[:, sl], _NT, preferred_element_type=F32)
        sp = jnp.maximum(z, 0.0) + jnp.log(1.0 + jnp.exp(-jnp.abs(z)))
        lk = jnp.where(past, -sp, 0.0)
        hi = lk.astype(BF16)
        lo = (lk - hi.astype(F32)).astype(BF16)
        cs = jnp.dot(lm, jnp.concatenate([hi, lo], axis=1), preferred_element_type=F32)
        between = cs[:, :tq] + cs[:, tq:] + carry_sc[h]
        wgt = jnp.where(past, jnp.exp(z - sp + between), 0.0)
        acc_sc[h] = acc_sc[h] + jnp.dot(vT_ref[sl, :], wgt.astype(BF16), preferred_element_type=F32)
        carry_sc[h] = carry_sc[h] + jnp.sum(lk, axis=0, keepdims=True)

    @pl.when((fl & 2) != 0)
    def _():
        for h in range(SB_HEADS):
            o_ref[:, h * HEAD_DIM:(h + 1) * HEAD_DIM] = acc_sc[h].T.astype(BF16)


def _stick_breaking(q, k, vT):
    T, W = q.shape
    tq = _tile(T, 512)
    tk = _tile(T, 256)
    nq = T // tq
    qi, kj, fl = _steps(nq, lambda i: 0, lambda i: ((i + 1) * tq - 2) // tk, reverse=True)
    lmat = jnp.asarray(np.triu(np.ones((tk, tk), np.float32), 1), BF16)
    kern = functools.partial(_sb_kernel, tq=tq, tk=tk)
    return pl.pallas_call(
        kern,
        grid_spec=pltpu.PrefetchScalarGridSpec(
            num_scalar_prefetch=3, grid=(int(qi.shape[0]),),
            in_specs=[pl.BlockSpec((tq, W), lambda p, qi, kj, fl: (qi[p], 0)),
                      pl.BlockSpec((tk, W), lambda p, qi, kj, fl: (kj[p], 0)),
                      pl.BlockSpec((W, tk), lambda p, qi, kj, fl: (0, kj[p])),
                      pl.BlockSpec((tk, tk), lambda p, qi, kj, fl: (0, 0))],
            out_specs=pl.BlockSpec((tq, W), lambda p, qi, kj, fl: (qi[p], 0)),
            scratch_shapes=[pltpu.VMEM((SB_HEADS, 1, tq), F32), pltpu.VMEM((SB_HEADS, HEAD_DIM, tq), F32)]),
        out_shape=jax.ShapeDtypeStruct((T, W), BF16),
        compiler_params=_params(("arbitrary",), 48), name="stick_breaking",
    )(qi, kj, fl, q, k, vT, lmat)


def _memattn_kernel(q_ref, k_ref, vT_ref, o_ref):
    for h in range(MEM_HEADS):
        sl = slice(h * HEAD_DIM, (h + 1) * HEAD_DIM)
        sT = lax.dot_general(k_ref[:, sl], q_ref[:, sl], _NT, preferred_element_type=F32)
        e = jnp.exp(sT - jnp.max(sT, axis=0, keepdims=True))
        l = jnp.sum(e, axis=0, keepdims=True)
        oT = jnp.dot(vT_ref[sl, :], e.astype(BF16), preferred_element_type=F32) * (1.0 / l)
        o_ref[:, sl] = oT.T.astype(BF16)


def _mem_attention(q, k, vT):
    T, W = q.shape
    M = k.shape[0]
    tq = _tile(T, 512)
    return pl.pallas_call(
        _memattn_kernel, grid=(T // tq,),
        in_specs=[pl.BlockSpec((tq, W), lambda i: (i, 0)),
                  pl.BlockSpec((M, W), lambda i: (0, 0)),
                  pl.BlockSpec((W, M), lambda i: (0, 0))],
        out_specs=pl.BlockSpec((tq, W), lambda i: (i, 0)),
        out_shape=jax.ShapeDtypeStruct((T, W), BF16),
        compiler_params=_params(("parallel",), 32), name="mem_attention",
    )(q, k, vT)


def _merge_kernel(yc_ref, ys_ref, yw_ref, ysb_ref, ym_ref, g0_ref, g1_ref, g2_ref,
                  wn_ref, wsb_ref, wm_ref, o_ref, yn_sc):
    @pl.when(pl.program_id(1) == 0)
    def _():
        yn_sc[...] = (yc_ref[...].astype(F32) + ys_ref[...].astype(F32) + yw_ref[...].astype(F32)).astype(BF16)

    a = jnp.dot(yn_sc[...], wn_ref[...], preferred_element_type=F32)
    b = jnp.dot(ysb_ref[...], wsb_ref[...], preferred_element_type=F32)
    c = jnp.dot(ym_ref[...], wm_ref[...], preferred_element_type=F32)
    o = jax.nn.sigmoid(g0_ref[...]) * a + jax.nn.sigmoid(g1_ref[...]) * b + jax.nn.sigmoid(g2_ref[...]) * c
    o_ref[...] = o.astype(BF16)


def _merge(yc, ys, yw, ysb, ym, P, wn, wsb, wm, gate_col0):
    T = yc.shape[0]
    D = wn.shape[1]
    tm = _tile(T, 512)
    tn = 512
    assert D % tn == 0 and gate_col0 % tn == 0
    gb = gate_col0 // tn
    nb = D // tn
    rowi = lambda w: pl.BlockSpec((tm, w), lambda i, j: (i, 0))
    gate = lambda b: pl.BlockSpec((tm, tn), lambda i, j, b=b: (i, gb + b * nb + j))
    wcol = lambda k: pl.BlockSpec((k, tn), lambda i, j: (0, j))
    return pl.pallas_call(
        _merge_kernel, grid=(T // tm, nb),
        in_specs=[rowi(yc.shape[1]), rowi(ys.shape[1]), rowi(yw.shape[1]), rowi(ysb.shape[1]), rowi(ym.shape[1]),
                  gate(0), gate(1), gate(2), wcol(wn.shape[0]), wcol(wsb.shape[0]), wcol(wm.shape[0])],
        out_specs=pl.BlockSpec((tm, tn), lambda i, j: (i, j)),
        out_shape=jax.ShapeDtypeStruct((T, D), BF16),
        scratch_shapes=[pltpu.VMEM((tm, yc.shape[1]), BF16)],
        compiler_params=_params(("parallel", "arbitrary"), 48), name="merge",
    )(yc, ys, yw, ysb, ym, P, P, P, wn, wsb, wm)


def _out_kernel(mix_ref, w_ref, x_ref, g_ref, x2_o, h2_o):
    x2 = x_ref[...] + jnp.dot(mix_ref[...], w_ref[...], preferred_element_type=F32)
    x2_o[...] = x2
    h2_o[...] = _rms(x2, g_ref[...]).astype(BF16)


def _out_proj(mixed, w, x, gain):
    T, D = x.shape
    tm = _tile(T, 256)
    return pl.pallas_call(
        _out_kernel, grid=(T // tm,),
        in_specs=[pl.BlockSpec((tm, D), lambda i: (i, 0)),
                  pl.BlockSpec((D, D), lambda i: (0, 0)),
                  pl.BlockSpec((tm, D), lambda i: (i, 0)),
                  pl.BlockSpec((1, D), lambda i: (0, 0))],
        out_specs=[pl.BlockSpec((tm, D), lambda i: (i, 0)), pl.BlockSpec((tm, D), lambda i: (i, 0))],
        out_shape=[jax.ShapeDtypeStruct((T, D), F32), jax.ShapeDtypeStruct((T, D), BF16)],
        compiler_params=_params(("parallel",), 48), name="out_proj",
    )(mixed, w, x, gain)


def _ffn_kernel(h_ref, wg_ref, wu_ref, wd_ref, x_ref, o_ref):
    a = jnp.dot(h_ref[...], wg_ref[...], preferred_element_type=F32)
    b = jnp.dot(h_ref[...], wu_ref[...], preferred_element_type=F32)
    z = (a * jax.nn.sigmoid(a) * b).astype(BF16)
    c = jnp.dot(z, wd_ref[...], preferred_element_type=F32)

    @pl.when(pl.program_id(1) == 0)
    def _():
        o_ref[...] = x_ref[...] + c

    @pl.when(pl.program_id(1) != 0)
    def _():
        o_ref[...] += c


def _ffn(h2, wg, wu, wd, x2):
    T, D = x2.shape
    F = wg.shape[1]
    tm = _tile(T, 512)
    tf = 512
    assert F % tf == 0
    return pl.pallas_call(
        _ffn_kernel, grid=(T // tm, F // tf),
        in_specs=[pl.BlockSpec((tm, D), lambda i, f: (i, 0)),
                  pl.BlockSpec((D, tf), lambda i, f: (0, f)),
                  pl.BlockSpec((D, tf), lambda i, f: (0, f)),
                  pl.BlockSpec((tf, D), lambda i, f: (f, 0)),
                  pl.BlockSpec((tm, D), lambda i, f: (i, 0))],
        out_specs=pl.BlockSpec((tm, D), lambda i, f: (i, 0)),
        out_shape=jax.ShapeDtypeStruct((T, D), F32),
        compiler_params=_params(("parallel", "arbitrary"), 48), name="ffn",
    )(h2, wg, wu, wd, x2)


def _layer(x, mem, pos_col, posc_col, consts, attn_norm, w_in, nsa_q_norm, nsa_kc_norm, nsa_ks_norm, nsa_kw_norm,
           cmp_k_pe, cmp_k_w1, cmp_k_w2, cmp_v_pe, cmp_v_w1, cmp_v_w2, mem_norm, w_mem_kv,
           mem_q_norm, mem_k_norm, w_o_nsa, w_o_sb, w_o_mem, w_out, ffn_norm,
           w_ffn_gate, w_ffn_up, w_ffn_down):
    T, D = x.shape
    inv2, sgn, ovT = consts
    hd = HEAD_DIM
    row = lambda g: g.reshape(1, -1)

    q_w, kv_w, gn_w = NSA_HEADS * hd, 6 * NSA_GROUPS * hd, 3 * NSA_HEADS
    sb_w, mq_w, gm_w = 3 * SB_HEADS * hd, MEM_HEADS * hd, N_BRANCH * D
    o_gn = q_w + kv_w
    o_sb = o_gn + gn_w
    o_mq = o_sb + sb_w
    o_gm = o_mq + mq_w
    assert w_in.shape[1] == o_gm + gm_w
    w_re = jnp.concatenate(
        [w_in[:, :o_gn], w_in[:, o_sb:], w_in[:, o_gn:o_sb], jnp.zeros((D, LANE - gn_w), w_in.dtype)], axis=1)
    gate_col0 = q_w + kv_w + sb_w + mq_w
    P = _in_proj(x, row(attn_norm), w_re.astype(BF16))

    (qn, kc_raw, vc_raw, ksn, kwn, vsT, vwT, sbq, sbk, sbvT, memq, gT) = _prep(
        P, pos_col, inv2, sgn, row(nsa_q_norm), row(nsa_ks_norm), row(nsa_kw_norm), row(mem_q_norm))

    gates = gT[:gn_w].reshape(NSA_GROUPS, NSA_HPG, 3, T).transpose(2, 0, 1, 3)
    gates = jnp.pad(gates, ((0, 0), (0, 0), (0, 8 - NSA_HPG), (0, 0)))

    n_pad = T // CMP_STRIDE
    half = CMP_LEN // 2

    def w1_pack(w1):
        return jnp.concatenate([w1[:half].reshape(half * hd, -1), w1[half:].reshape(half * hd, -1)], axis=1).astype(BF16)

    def pe_pack(pe):
        return jnp.pad(pe.reshape(2, half * hd), ((0, 6), (0, 0))).astype(BF16)

    kc, vcT = _compress(
        kc_raw.reshape(NSA_GROUPS, n_pad, CMP_STRIDE * hd), vc_raw.reshape(NSA_GROUPS, n_pad, CMP_STRIDE * hd),
        w1_pack(cmp_k_w1), w1_pack(cmp_v_w1), pe_pack(cmp_k_pe), pe_pack(cmp_v_pe),
        cmp_k_w2.astype(BF16), cmp_v_w2.astype(BF16), row(nsa_kc_norm), posc_col, inv2, sgn)

    y_cmp, bias = _cmp_select(qn, kc, vcT, ovT, gates)
    y_sel = _nsa_flash("sel", qn, ksn, vsT, gates, bias)
    y_win = _nsa_flash("win", qn, kwn, vwT, gates)
    y_sb = _stick_breaking(sbq, sbk, sbvT)

    mk, mvT = _memkv(mem, row(mem_norm), w_mem_kv.astype(BF16), row(mem_k_norm))
    y_mem = _mem_attention(memq, mk, mvT)

    mixed = _merge(y_cmp, y_sel, y_win, y_sb, y_mem, P,
                   w_o_nsa.astype(BF16), w_o_sb.astype(BF16), w_o_mem.astype(BF16), gate_col0)
    x2, h2 = _out_proj(mixed, w_out.astype(BF16), x, row(ffn_norm))
    return _ffn(h2, w_ffn_gate.astype(BF16), w_ffn_up.astype(BF16), w_ffn_down.astype(BF16), x2)


def kernel(x, mem, positions, attn_norm, w_in, nsa_q_norm, nsa_kc_norm, nsa_ks_norm, nsa_kw_norm, cmp_k_pe, cmp_k_w1, cmp_k_w2, cmp_v_pe, cmp_v_w1, cmp_v_w2, mem_norm, w_mem_kv, mem_q_norm, mem_k_norm, w_o_nsa, w_o_sb, w_o_mem, w_out, ffn_norm, w_ffn_gate, w_ffn_up, w_ffn_down):
    B, T, D = x.shape
    assert T % (4 * LANE) == 0 and T // SEL_BLOCK >= 8
    n_pad = T // CMP_STRIDE
    n_sel = T // SEL_BLOCK
    inv = 1.0 / (ROPE_THETA ** (jnp.arange(0, HEAD_DIM, 2, dtype=F32) / HEAD_DIM))
    inv2 = jnp.concatenate([inv, inv]).reshape(1, HEAD_DIM)
    sgn = jnp.concatenate([-jnp.ones((HEAD_DIM // 2,), F32), jnp.ones((HEAD_DIM // 2,), F32)]).reshape(1, HEAD_DIM)
    cs = np.arange(n_pad)[None, :] * CMP_STRIDE
    ss = np.arange(n_sel)[:, None] * SEL_BLOCK
    ovT = jnp.asarray(((cs < ss + SEL_BLOCK) & (cs + CMP_LEN - 1 >= ss)).astype(np.float32), BF16)
    consts = (inv2, sgn, ovT)
    depth = w_in.shape[0]
    outs = []
    for b in range(B):
        xb = x[b]
        posf = positions[b].astype(F32)
        pos_col = posf.reshape(T, 1)
        posc = jnp.concatenate([posf[CMP_LEN - 1::CMP_STRIDE], posf[-1:]]).reshape(n_pad, 1)
        for l in range(depth):
            xb = _layer(xb, mem[b], pos_col, posc, consts, attn_norm[l], w_in[l], nsa_q_norm[l], nsa_kc_norm[l],
                        nsa_ks_norm[l], nsa_kw_norm[l], cmp_k_pe[l], cmp_k_w1[l], cmp_k_w2[l], cmp_v_pe[l],
                        cmp_v_w1[l], cmp_v_w2[l], mem_norm[l], w_mem_kv[l], mem_q_norm[l], mem_k_norm[l],
                        w_o_nsa[l], w_o_sb[l], w_o_mem[l], w_out[l], ffn_norm[l],
                        w_ffn_gate[l], w_ffn_up[l], w_ffn_down[l])
        outs.append(xb)
    return jnp.stack(outs, axis=0)
```

```python
import functools

import numpy as np
import jax
import jax.numpy as jnp
from jax import lax
from jax.experimental import pallas as pl
from jax.experimental.pallas import tpu as pltpu

HEAD_DIM = 128
NSA_HEADS = 8
NSA_GROUPS = 2
NSA_HPG = NSA_HEADS // NSA_GROUPS
SB_HEADS = 4
MEM_HEADS = 4
CMP_LEN = 32
CMP_STRIDE = 16
CMP_HIDDEN = 2 * HEAD_DIM
SEL_BLOCK = 64
SEL_TOPK = 16
WINDOW = 512
ROPE_THETA = 10000.0
NORM_EPS = 1e-6
NEG_BIG = -1e30
N_BRANCH = 3
SCALE = HEAD_DIM ** -0.5
LOG2E = 1.4426950408889634
QSCALE = SCALE * LOG2E
SB_DEAD_LOG2 = -160.0
VAUG = HEAD_DIM + 16

LANE = 128
MIB = 1 << 20
BF16 = jnp.bfloat16
F32 = jnp.float32

_NT = (((1,), (1,)), ((), ()))


def _tile(n, pref):
    t = min(n, pref)
    assert n % t == 0, (n, pref)
    return t


def _params(sem, vmem_mib):
    return pltpu.CompilerParams(dimension_semantics=sem, vmem_limit_bytes=vmem_mib * MIB)


def _rms(x, gain):
    return x * lax.rsqrt(jnp.mean(x * x, axis=-1, keepdims=True) + NORM_EPS) * gain


def _rope_tables(pos, inv2, sgn):
    ang = pos * inv2
    return jnp.cos(ang), jnp.sin(ang) * sgn


def _rope(x, c, s):
    return x * c + pltpu.roll(x, HEAD_DIM // 2, 1) * s


def _proj_kernel(x_ref, g_ref, w_ref, o_ref, hn_ref):
    @pl.when(pl.program_id(1) == 0)
    def _():
        hn_ref[...] = _rms(x_ref[...], g_ref[...]).astype(BF16)

    o_ref[...] = jnp.dot(hn_ref[...], w_ref[...], preferred_element_type=F32)


def _in_proj(x, gain, w_bf16):
    T, D = x.shape
    N = w_bf16.shape[1]
    tm = _tile(T, 1024)
    tn = 640
    assert N % tn == 0
    return pl.pallas_call(
        _proj_kernel,
        grid=(T // tm, N // tn),
        in_specs=[
            pl.BlockSpec((tm, D), lambda i, j: (i, 0)),
            pl.BlockSpec((1, D), lambda i, j: (0, 0)),
            pl.BlockSpec((D, tn), lambda i, j: (0, j)),
        ],
        out_specs=pl.BlockSpec((tm, tn), lambda i, j: (i, j)),
        out_shape=jax.ShapeDtypeStruct((T, N), F32),
        scratch_shapes=[pltpu.VMEM((tm, D), BF16)],
        compiler_params=_params(("parallel", "arbitrary"), 48),
        name="in_proj",
    )(x, gain, w_bf16)


def _prep_kernel(pos_ref, inv_ref, sgn_ref, gq_ref, gks_ref, gkw_ref, gmq_ref,
                 q_ref, kc_ref, vc_ref, ks_ref, vs_ref, kw_ref, vw_ref,
                 sq_ref, sk_ref, sv_ref, mq_ref, gn_ref,
                 qn_o, kc_o, vc_o, ks_o, kw_o, vsT_o, vwT_o, sq_o, sk_o, svT_o, mq_o, gT_o):
    c, s = _rope_tables(pos_ref[...], inv_ref[...], sgn_ref[...])
    hd = HEAD_DIM
    for h in range(NSA_HEADS):
        sl = slice(h * hd, (h + 1) * hd)
        qn_o[:, sl] = (_rope(_rms(q_ref[:, sl], gq_ref[...]), c, s) * QSCALE).astype(BF16)
    tp = pos_ref.shape[0]
    ones_rows = (lax.broadcasted_iota(jnp.int32, (VAUG - hd, tp), 0) == 0).astype(F32).astype(BF16)
    for g in range(NSA_GROUPS):
        sl = slice(g * hd, (g + 1) * hd)
        kc_o[g] = kc_ref[:, sl].astype(BF16)
        vc_o[g] = vc_ref[:, sl].astype(BF16)
        ks_o[:, sl] = _rope(_rms(ks_ref[:, sl], gks_ref[...]), c, s).astype(BF16)
        kw_o[:, sl] = _rope(_rms(kw_ref[:, sl], gkw_ref[...]), c, s).astype(BF16)
        vsT_o[g * VAUG:g * VAUG + hd, :] = vs_ref[:, sl].T.astype(BF16)
        vsT_o[g * VAUG + hd:(g + 1) * VAUG, :] = ones_rows
        vwT_o[g * VAUG:g * VAUG + hd, :] = vw_ref[:, sl].T.astype(BF16)
        vwT_o[g * VAUG + hd:(g + 1) * VAUG, :] = ones_rows
    for h in range(SB_HEADS):
        sl = slice(h * hd, (h + 1) * hd)
        sq_o[:, sl] = (sq_ref[:, sl] * QSCALE).astype(BF16)
        svT_o[sl, :] = sv_ref[:, sl].T.astype(BF16)
    sk_o[...] = sk_ref[...].astype(BF16)
    for h in range(MEM_HEADS):
        sl = slice(h * hd, (h + 1) * hd)
        mq_o[:, sl] = (_rms(mq_ref[:, sl], gmq_ref[...]) * QSCALE).astype(BF16)
    gT_o[...] = jax.nn.sigmoid(gn_ref[...]).T


def _prep(P, pos_col, inv2, sgn, gq, gks, gkw, gmq):
    T = P.shape[0]
    tp = _tile(T, 512)
    hd = HEAD_DIM
    row = lambda w, c: pl.BlockSpec((tp, w), lambda i, c=c: (i, c))
    const = lambda: pl.BlockSpec((1, hd), lambda i: (0, 0))
    in_specs = [pl.BlockSpec((tp, 1), lambda i: (i, 0)), const(), const(), const(), const(), const(), const(),
                row(8 * hd, 0),
                row(2 * hd, 4), row(2 * hd, 5),
                row(2 * hd, 6), row(2 * hd, 7),
                row(2 * hd, 8), row(2 * hd, 9),
                row(4 * hd, 5), row(4 * hd, 6), row(4 * hd, 7),
                row(4 * hd, 8),
                row(hd, 84)]
    colT = lambda w: pl.BlockSpec((w, tp), lambda i: (0, i))
    out_specs = [row(8 * hd, 0),
                 pl.BlockSpec((NSA_GROUPS, tp, hd), lambda i: (0, i, 0)),
                 pl.BlockSpec((NSA_GROUPS, tp, hd), lambda i: (0, i, 0)),
                 row(2 * hd, 0), row(2 * hd, 0), colT(NSA_GROUPS * VAUG), colT(NSA_GROUPS * VAUG),
                 row(4 * hd, 0), row(4 * hd, 0), colT(4 * hd), row(4 * hd, 0), colT(hd)]
    sds = jax.ShapeDtypeStruct
    out_shape = [sds((T, 8 * hd), BF16),
                 sds((NSA_GROUPS, T, hd), BF16), sds((NSA_GROUPS, T, hd), BF16),
                 sds((T, 2 * hd), BF16), sds((T, 2 * hd), BF16),
                 sds((NSA_GROUPS * VAUG, T), BF16), sds((NSA_GROUPS * VAUG, T), BF16),
                 sds((T, 4 * hd), BF16), sds((T, 4 * hd), BF16), sds((4 * hd, T), BF16), sds((T, 4 * hd), BF16),
                 sds((hd, T), F32)]
    return pl.pallas_call(
        _prep_kernel, grid=(T // tp,), in_specs=in_specs, out_specs=out_specs, out_shape=out_shape,
        compiler_params=_params(("parallel",), 48), name="prep",
    )(pos_col, inv2, sgn, gq, gks, gkw, gmq, *([P] * 12))


def _gelu_tanh(x):
    return 0.5 * x * (1.0 + jnp.tanh(0.7978845608028654 * (x + 0.044715 * (x * x * x))))


def _compress_one(x, w1, pe, w2):
    n = x.shape[0]
    ab = jnp.dot(x, w1, preferred_element_type=F32)
    pr = jnp.dot(pe, w1, preferred_element_type=F32)
    pec = pr[0:1, :CMP_HIDDEN] + pr[1:2, CMP_HIDDEN:]
    hid = ab[:, :CMP_HIDDEN] + pltpu.roll(ab[:, CMP_HIDDEN:], n - 1, 0) + pec
    return jnp.dot(_gelu_tanh(hid).astype(BF16), w2, preferred_element_type=F32)


def _compress_kernel(xk_ref, xv_ref, w1k_ref, w1v_ref, pek_ref, pev_ref, w2k_ref, w2v_ref,
                     gk_ref, pos_ref, inv_ref, sgn_ref, kc_o, vcT_o):
    c, s = _rope_tables(pos_ref[...], inv_ref[...], sgn_ref[...])
    k = _compress_one(xk_ref[0], w1k_ref[...], pek_ref[...], w2k_ref[...])
    kc_o[0] = _rope(_rms(k, gk_ref[...]), c, s).astype(BF16)
    v = _compress_one(xv_ref[0], w1v_ref[...], pev_ref[...], w2v_ref[...])
    vcT_o[0] = v.T.astype(BF16)


def _compress(xk, xv, w1k, w1v, pek, pev, w2k, w2v, gk, posc, inv2, sgn):
    G, n, W = xk.shape
    full = lambda a: pl.BlockSpec(a.shape, lambda g: (0,) * a.ndim)
    grp = pl.BlockSpec((1, n, W), lambda g: (g, 0, 0))
    return pl.pallas_call(
        _compress_kernel, grid=(G,),
        in_specs=[grp, grp, full(w1k), full(w1v), full(pek), full(pev), full(w2k), full(w2v),
                  full(gk), full(posc), full(inv2), full(sgn)],
        out_specs=[pl.BlockSpec((1, n, HEAD_DIM), lambda g: (g, 0, 0)),
                   pl.BlockSpec((1, HEAD_DIM, n), lambda g: (g, 0, 0))],
        out_shape=[jax.ShapeDtypeStruct((G, n, HEAD_DIM), BF16), jax.ShapeDtypeStruct((G, HEAD_DIM, n), BF16)],
        compiler_params=_params(("parallel",), 48), name="compress",
    )(xk, xv, w1k, w1v, pek, pev, w2k, w2v, gk, posc, inv2, sgn)


def _memkv_kernel(mem_ref, gm_ref, w_ref, gk_ref, k_o, vT_o):
    hn = _rms(mem_ref[...], gm_ref[...]).astype(BF16)
    kv = jnp.dot(hn, w_ref[...], preferred_element_type=F32)
    mw = MEM_HEADS * HEAD_DIM
    for h in range(MEM_HEADS):
        sl = slice(h * HEAD_DIM, (h + 1) * HEAD_DIM)
        k_o[:, sl] = _rms(kv[:, sl], gk_ref[...]).astype(BF16)
        vT_o[sl, :] = kv[:, mw + h * HEAD_DIM: mw + (h + 1) * HEAD_DIM].T.astype(BF16)


def _memkv(mem, gm, w, gk):
    M = mem.shape[0]
    mw = MEM_HEADS * HEAD_DIM
    return pl.pallas_call(
        _memkv_kernel,
        out_shape=[jax.ShapeDtypeStruct((M, mw), BF16), jax.ShapeDtypeStruct((mw, M), BF16)],
        compiler_params=pltpu.CompilerParams(vmem_limit_bytes=32 * MIB), name="mem_kv",
    )(mem, gm, w, gk)


def _cmp_kernel(q_ref, kc_ref, vcT_ref, ovT_ref, g_ref, y_ref, b_ref, *, tq, n_pad, n_sel, top_k):
    i = pl.program_id(1)
    t = i * tq + lax.broadcasted_iota(jnp.int32, (1, tq), 1)
    n_end = lax.broadcasted_iota(jnp.int32, (n_pad, 1), 0) * CMP_STRIDE + (CMP_LEN - 1)
    valid = n_end <= t
    kc = kc_ref[0]
    vcT = vcT_ref[0]
    has_valid = (t >= CMP_LEN - 1).astype(F32)
    sms = []
    for h in range(NSA_HPG):
        q_h = q_ref[:, h * HEAD_DIM:(h + 1) * HEAD_DIM]
        sms.append(jnp.where(valid, lax.dot_general(kc, q_h, _NT, preferred_element_type=F32), NEG_BIG))
    psum = jnp.zeros((n_pad, tq), F32)
    for h in range(NSA_HPG):
        e = jnp.exp2(sms[h] - jnp.max(sms[h], axis=0, keepdims=True))
        p = e * (has_valid / jnp.sum(e, axis=0, keepdims=True))
        oT = jnp.dot(vcT, p.astype(BF16), preferred_element_type=F32)
        y_ref[:, h * HEAD_DIM:(h + 1) * HEAD_DIM] = (oT * g_ref[0, 0, h:h + 1, :]).T.astype(BF16)
        psum = psum + p
    hi = psum.astype(BF16)
    r1 = psum - hi.astype(F32)
    mid = r1.astype(BF16)
    lo = (r1 - mid.astype(F32)).astype(BF16)
    parts = jnp.dot(ovT_ref[...], jnp.concatenate([hi, mid, lo], axis=1), preferred_element_type=F32)
    imp = parts[:, :tq] + parts[:, tq:2 * tq] + parts[:, 2 * tq:]
    s_i = lax.broadcasted_iota(jnp.int32, (n_sel, 1), 0)
    cur = lax.shift_right_logical(t, 6)
    forced = (s_i == 0) | (s_i == cur) | (s_i == cur - 1)
    future = s_i * SEL_BLOCK > t
    w = jnp.where(forced, jnp.inf, jnp.where(future, -jnp.inf, imp))
    s_f = jnp.broadcast_to(s_i.astype(F32), (n_sel, tq))
    for _ in range(top_k):
        m = jnp.max(w, axis=0, keepdims=True)
        idx = jnp.min(jnp.where(w == m, s_f, float(n_sel)), axis=0, keepdims=True)
        w = jnp.where(s_f == idx, -jnp.inf, w)
    b_ref[0] = jnp.where(future, NEG_BIG, jnp.where(w == -jnp.inf, 0.0, NEG_BIG))


def _cmp_select(qn, kc, vcT, ovT, gates):
    T = qn.shape[0]
    G, n_pad, _ = kc.shape
    n_sel = T // SEL_BLOCK
    tq = _tile(T, 256)
    gw = NSA_HPG * HEAD_DIM
    kern = functools.partial(_cmp_kernel, tq=tq, n_pad=n_pad, n_sel=n_sel, top_k=min(SEL_TOPK, n_sel))
    return pl.pallas_call(
        kern, grid=(G, T // tq),
        in_specs=[pl.BlockSpec((tq, gw), lambda g, i: (i, g)),
                  pl.BlockSpec((1, n_pad, HEAD_DIM), lambda g, i: (g, 0, 0)),
                  pl.BlockSpec((1, HEAD_DIM, n_pad), lambda g, i: (g, 0, 0)),
                  pl.BlockSpec((n_sel, n_pad), lambda g, i: (0, 0)),
                  pl.BlockSpec((1, 1, 8, tq), lambda g, i: (0, g, 0, i))],
        out_specs=[pl.BlockSpec((tq, gw), lambda g, i: (i, g)),
                   pl.BlockSpec((1, n_sel, tq), lambda g, i: (g, 0, i))],
        out_shape=[jax.ShapeDtypeStruct((T, NSA_HEADS * HEAD_DIM), BF16),
                   jax.ShapeDtypeStruct((G, n_sel, T), F32)],
        compiler_params=_params(("parallel", "parallel"), 48), name="cmp_select",
    )(qn, kc, vcT, ovT, gates)


def _flash_kernel(qi_ref, kj_ref, fl_ref, q_ref, k_ref, vT_ref, g_ref, *rest, mode, tq, tk):
    if mode == "sel":
        bias_ref, o_ref, m_sc, acc_sc = rest
    else:
        o_ref, m_sc, acc_sc = rest
    p = pl.program_id(1)
    i = qi_ref[p]
    j = kj_ref[p]
    fl = fl_ref[p]

    @pl.when((fl & 1) != 0)
    def _():
        m_sc[...] = jnp.full(m_sc.shape, NEG_BIG, F32)
        acc_sc[...] = jnp.zeros(acc_sc.shape, F32)

    t = i * tq + lax.broadcasted_iota(jnp.int32, (1, tq), 1)
    kpos = j * tk + lax.broadcasted_iota(jnp.int32, (tk, 1), 0)
    if mode == "sel":
        nb = tk // SEL_BLOCK
        base = jnp.concatenate(
            [jnp.broadcast_to(bias_ref[0, b:b + 1, :], (SEL_BLOCK, tq)) for b in range(nb)], axis=0)
        mask_add = jnp.where(kpos <= t, base, NEG_BIG)
    else:
        mask_add = jnp.where((kpos <= t) & (kpos > t - WINDOW), 0.0, NEG_BIG)
    k = k_ref[...]
    vT = vT_ref[...]
    sTs = []
    for h in range(NSA_HPG):
        q_h = q_ref[:, h * HEAD_DIM:(h + 1) * HEAD_DIM]
        sTs.append(lax.dot_general(k, q_h, _NT, preferred_element_type=F32) + mask_add)
    pTs, alphas = [], []
    for h in range(NSA_HPG):
        sT = sTs[h]
        m_old = m_sc[h]
        m_new = jnp.maximum(m_old, jnp.max(sT, axis=0, keepdims=True))
        alphas.append(jnp.exp2(m_old - m_new))
        pTs.append(jnp.exp2((sT - m_new).astype(BF16)))
        m_sc[h] = m_new
    for h in range(NSA_HPG):
        acc_sc[h] = alphas[h] * acc_sc[h] + jnp.dot(vT, pTs[h], preferred_element_type=F32)

    @pl.when((fl & 2) != 0)
    def _():
        for h in range(NSA_HPG):
            acc = acc_sc[h]
            l = acc[HEAD_DIM:HEAD_DIM + 1, :]
            o = acc[:HEAD_DIM, :] * ((1.0 / l) * g_ref[0, 0, h:h + 1, :])
            o_ref[:, h * HEAD_DIM:(h + 1) * HEAD_DIM] = o.T.astype(BF16)


def _steps(nq, lo_fn, hi_fn, reverse=False):
    qi, kj, fl = [], [], []
    for i in range(nq):
        js = list(range(lo_fn(i), hi_fn(i) + 1))
        if reverse:
            js = js[::-1]
        for n, j in enumerate(js):
            qi.append(i)
            kj.append(j)
            fl.append((1 if n == 0 else 0) | (2 if n == len(js) - 1 else 0))
    return (jnp.asarray(np.array(qi, np.int32)), jnp.asarray(np.array(kj, np.int32)),
            jnp.asarray(np.array(fl, np.int32)))


def _nsa_flash(mode, qn, k, vT, gates, bias=None):
    T = qn.shape[0]
    tq = _tile(T, 1024)
    tk = _tile(T, 512)
    nq = T // tq
    gw = NSA_HPG * HEAD_DIM
    hi = lambda i: ((i + 1) * tq - 1) // tk
    if mode == "sel":
        lo = lambda i: 0
        br = 1
    else:
        lo = lambda i: max(0, (i * tq - (WINDOW - 1)) // tk)
        br = 2
    qi, kj, fl = _steps(nq, lo, hi)
    in_specs = [pl.BlockSpec((tq, gw), lambda g, p, qi, kj, fl: (qi[p], g)),
                pl.BlockSpec((tk, HEAD_DIM), lambda g, p, qi, kj, fl: (kj[p], g)),
                pl.BlockSpec((VAUG, tk), lambda g, p, qi, kj, fl: (g, kj[p])),
                pl.BlockSpec((1, 1, 8, tq), lambda g, p, qi, kj, fl, br=br: (br, g, 0, qi[p]))]
    args = [qn, k, vT, gates]
    if mode == "sel":
        in_specs.append(pl.BlockSpec((1, tk // SEL_BLOCK, tq), lambda g, p, qi, kj, fl: (g, kj[p], qi[p])))
        args.append(bias)
    kern = functools.partial(_flash_kernel, mode=mode, tq=tq, tk=tk)
    return pl.pallas_call(
        kern,
        grid_spec=pltpu.PrefetchScalarGridSpec(
            num_scalar_prefetch=3, grid=(NSA_GROUPS, int(qi.shape[0])),
            in_specs=in_specs,
            out_specs=pl.BlockSpec((tq, gw), lambda g, p, qi, kj, fl: (qi[p], g)),
            scratch_shapes=[pltpu.VMEM((NSA_HPG, 1, tq), F32), pltpu.VMEM((NSA_HPG, VAUG, tq), F32)]),
        out_shape=jax.ShapeDtypeStruct((T, NSA_HEADS * HEAD_DIM), BF16),
        compiler_params=_params(("parallel", "arbitrary"), 48), name="nsa_" + mode,
    )(qi, kj, fl, *args)


def _sb_kernel(qi_ref, kj_ref, fl_ref, q_ref, k_ref, vT_ref, lm_ref, o_ref, carry_sc, acc_sc, *, tq, tk):
    p = pl.program_id(0)
    i = qi_ref[p]
    j = kj_ref[p]
    fl = fl_ref[p]

    @pl.when((fl & 1) != 0)
    def _():
        carry_sc[...] = jnp.zeros(carry_sc.shape, F32)
        acc_sc[...] = jnp.zeros(acc_sc.shape, F32)

    @pl.when(jnp.max(carry_sc[...]) > SB_DEAD_LOG2)
    def _():
        t = i * tq + lax.broadcasted_iota(jnp.int32, (1, tq), 1)
        kpos = j * tk + lax.broadcasted_iota(jnp.int32, (tk, 1), 0)
        past = kpos < t
        lm = lm_ref[...]
        hs = [slice(h * HEAD_DIM, (h + 1) * HEAD_DIM) for h in range(SB_HEADS)]
        zs = [lax.dot_general(k_ref[:, sl], q_ref[:, sl], _NT, preferred_element_type=F32) for sl in hs]
        lgs, css = [], []
        for h in range(SB_HEADS):
            z = zs[h]
            sp = jnp.maximum(z, 0.0) + jnp.log2(1.0 + jnp.exp2(-jnp.abs(z)))
            lk = jnp.where(past, -sp, 0.0)
            hi = lk.astype(BF16)
            lo = (lk - hi.astype(F32)).astype(BF16)
            css.append(jnp.dot(lm, jnp.concatenate([hi, lo], axis=1), preferred_element_type=F32))
            lgs.append(z - sp)
            carry_old = carry_sc[h]
            carry_sc[h] = carry_old + jnp.sum(lk, axis=0, keepdims=True)
            css[h] = css[h][:, :tq] + css[h][:, tq:] + carry_old
        for h in range(SB_HEADS):
            wgt = jnp.where(past, jnp.exp2(lgs[h] + css[h]), 0.0)
            acc_sc[h] = acc_sc[h] + jnp.dot(vT_ref[hs[h], :], wgt.astype(BF16), preferred_element_type=F32)

    @pl.when((fl & 2) != 0)
    def _():
        for h in range(SB_HEADS):
            o_ref[:, h * HEAD_DIM:(h + 1) * HEAD_DIM] = acc_sc[h].T.astype(BF16)


def _stick_breaking(q, k, vT):
    T, W = q.shape
    tq = _tile(T, 512)
    tk = _tile(T, 512)
    nq = T // tq
    qi, kj, fl = _steps(nq, lambda i: 0, lambda i: ((i + 1) * tq - 2) // tk, reverse=True)
    lmat = jnp.asarray(np.triu(np.ones((tk, tk), np.float32), 1), BF16)
    kern = functools.partial(_sb_kernel, tq=tq, tk=tk)
    return pl.pallas_call(
        kern,
        grid_spec=pltpu.PrefetchScalarGridSpec(
            num_scalar_prefetch=3, grid=(int(qi.shape[0]),),
            in_specs=[pl.BlockSpec((tq, W), lambda p, qi, kj, fl: (qi[p], 0)),
                      pl.BlockSpec((tk, W), lambda p, qi, kj, fl: (kj[p], 0)),
                      pl.BlockSpec((W, tk), lambda p, qi, kj, fl: (0, kj[p])),
                      pl.BlockSpec((tk, tk), lambda p, qi, kj, fl: (0, 0))],
            out_specs=pl.BlockSpec((tq, W), lambda p, qi, kj, fl: (qi[p], 0)),
            scratch_shapes=[pltpu.VMEM((SB_HEADS, 1, tq), F32), pltpu.VMEM((SB_HEADS, HEAD_DIM, tq), F32)]),
        out_shape=jax.ShapeDtypeStruct((T, W), BF16),
        compiler_params=_params(("arbitrary",), 48), name="stick_breaking",
    )(qi, kj, fl, q, k, vT, lmat)


def _memattn_kernel(q_ref, k_ref, vT_ref, o_ref):
    for h in range(MEM_HEADS):
        sl = slice(h * HEAD_DIM, (h + 1) * HEAD_DIM)
        sT = lax.dot_general(k_ref[:, sl], q_ref[:, sl], _NT, preferred_element_type=F32)
        e = jnp.exp2(sT - jnp.max(sT, axis=0, keepdims=True))
        l = jnp.sum(e, axis=0, keepdims=True)
        oT = jnp.dot(vT_ref[sl, :], e.astype(BF16), preferred_element_type=F32) * (1.0 / l)
        o_ref[:, sl] = oT.T.astype(BF16)


def _mem_attention(q, k, vT):
    T, W = q.shape
    M = k.shape[0]
    tq = _tile(T, 512)
    return pl.pallas_call(
        _memattn_kernel, grid=(T // tq,),
        in_specs=[pl.BlockSpec((tq, W), lambda i: (i, 0)),
                  pl.BlockSpec((M, W), lambda i: (0, 0)),
                  pl.BlockSpec((W, M), lambda i: (0, 0))],
        out_specs=pl.BlockSpec((tq, W), lambda i: (i, 0)),
        out_shape=jax.ShapeDtypeStruct((T, W), BF16),
        compiler_params=_params(("parallel",), 32), name="mem_attention",
    )(q, k, vT)


def _merge_kernel(yc_ref, ys_ref, yw_ref, ysb_ref, ym_ref, g0_ref, g1_ref, g2_ref,
                  wn_ref, wsb_ref, wm_ref, o_ref, yn_sc):
    @pl.when(pl.program_id(1) == 0)
    def _():
        yn_sc[...] = (yc_ref[...].astype(F32) + ys_ref[...].astype(F32) + yw_ref[...].astype(F32)).astype(BF16)

    a = jnp.dot(yn_sc[...], wn_ref[...], preferred_element_type=F32)
    b = jnp.dot(ysb_ref[...], wsb_ref[...], preferred_element_type=F32)
    c = jnp.dot(ym_ref[...], wm_ref[...], preferred_element_type=F32)
    o = jax.nn.sigmoid(g0_ref[...]) * a + jax.nn.sigmoid(g1_ref[...]) * b + jax.nn.sigmoid(g2_ref[...]) * c
    o_ref[...] = o.astype(BF16)


def _merge(yc, ys, yw, ysb, ym, P, wn, wsb, wm, gate_col0):
    T = yc.shape[0]
    D = wn.shape[1]
    tm = _tile(T, 512)
    tn = 512
    assert D % tn == 0 and gate_col0 % tn == 0
    gb = gate_col0 // tn
    nb = D // tn
    rowi = lambda w: pl.BlockSpec((tm, w), lambda i, j: (i, 0))
    gate = lambda b: pl.BlockSpec((tm, tn), lambda i, j, b=b: (i, gb + b * nb + j))
    wcol = lambda k: pl.BlockSpec((k, tn), lambda i, j: (0, j))
    return pl.pallas_call(
        _merge_kernel, grid=(T // tm, nb),
        in_specs=[rowi(yc.shape[1]), rowi(ys.shape[1]), rowi(yw.shape[1]), rowi(ysb.shape[1]), rowi(ym.shape[1]),
                  gate(0), gate(1), gate(2), wcol(wn.shape[0]), wcol(wsb.shape[0]), wcol(wm.shape[0])],
        out_specs=pl.BlockSpec((tm, tn), lambda i, j: (i, j)),
        out_shape=jax.ShapeDtypeStruct((T, D), BF16),
        scratch_shapes=[pltpu.VMEM((tm, yc.shape[1]), BF16)],
        compiler_params=_params(("parallel", "arbitrary"), 48), name="merge",
    )(yc, ys, yw, ysb, ym, P, P, P, wn, wsb, wm)


def _out_kernel(mix_ref, w_ref, x_ref, g_ref, x2_o, h2_o):
    x2 = x_ref[...] + jnp.dot(mix_ref[...], w_ref[...], preferred_element_type=F32)
    x2_o[...] = x2
    h2_o[...] = _rms(x2, g_ref[...]).astype(BF16)


def _out_proj(mixed, w, x, gain):
    T, D = x.shape
    tm = _tile(T, 256)
    return pl.pallas_call(
        _out_kernel, grid=(T // tm,),
        in_specs=[pl.BlockSpec((tm, D), lambda i: (i, 0)),
                  pl.BlockSpec((D, D), lambda i: (0, 0)),
                  pl.BlockSpec((tm, D), lambda i: (i, 0)),
                  pl.BlockSpec((1, D), lambda i: (0, 0))],
        out_specs=[pl.BlockSpec((tm, D), lambda i: (i, 0)), pl.BlockSpec((tm, D), lambda i: (i, 0))],
        out_shape=[jax.ShapeDtypeStruct((T, D), F32), jax.ShapeDtypeStruct((T, D), BF16)],
        compiler_params=_params(("parallel",), 48), name="out_proj",
    )(mixed, w, x, gain)


def _ffn_kernel(h_ref, wg_ref, wu_ref, wd_ref, x_ref, o_ref):
    a = jnp.dot(h_ref[...], wg_ref[...], preferred_element_type=F32)
    b = jnp.dot(h_ref[...], wu_ref[...], preferred_element_type=F32)
    z = (a * jax.nn.sigmoid(a) * b).astype(BF16)
    c = jnp.dot(z, wd_ref[...], preferred_element_type=F32)

    @pl.when(pl.program_id(1) == 0)
    def _():
        o_ref[...] = x_ref[...] + c

    @pl.when(pl.program_id(1) != 0)
    def _():
        o_ref[...] += c


def _ffn(h2, wg, wu, wd, x2):
    T, D = x2.shape
    F = wg.shape[1]
    tm = _tile(T, 512)
    tf = 512
    assert F % tf == 0
    return pl.pallas_call(
        _ffn_kernel, grid=(T // tm, F // tf),
        in_specs=[pl.BlockSpec((tm, D), lambda i, f: (i, 0)),
                  pl.BlockSpec((D, tf), lambda i, f: (0, f)),
                  pl.BlockSpec((D, tf), lambda i, f: (0, f)),
                  pl.BlockSpec((tf, D), lambda i, f: (f, 0)),
                  pl.BlockSpec((tm, D), lambda i, f: (i, 0))],
        out_specs=pl.BlockSpec((tm, D), lambda i, f: (i, 0)),
        out_shape=jax.ShapeDtypeStruct((T, D), F32),
        compiler_params=_params(("parallel", "arbitrary"), 48), name="ffn",
    )(h2, wg, wu, wd, x2)


def _layer(x, mem, pos_col, posc_col, consts, attn_norm, w_in, nsa_q_norm, nsa_kc_norm, nsa_ks_norm, nsa_kw_norm,
           cmp_k_pe, cmp_k_w1, cmp_k_w2, cmp_v_pe, cmp_v_w1, cmp_v_w2, mem_norm, w_mem_kv,
           mem_q_norm, mem_k_norm, w_o_nsa, w_o_sb, w_o_mem, w_out, ffn_norm,
           w_ffn_gate, w_ffn_up, w_ffn_down):
    T, D = x.shape
    inv2, sgn, ovT = consts
    hd = HEAD_DIM
    row = lambda g: g.reshape(1, -1)

    q_w, kv_w, gn_w = NSA_HEADS * hd, 6 * NSA_GROUPS * hd, 3 * NSA_HEADS
    sb_w, mq_w, gm_w = 3 * SB_HEADS * hd, MEM_HEADS * hd, N_BRANCH * D
    o_gn = q_w + kv_w
    o_sb = o_gn + gn_w
    o_mq = o_sb + sb_w
    o_gm = o_mq + mq_w
    assert w_in.shape[1] == o_gm + gm_w
    w_re = jnp.concatenate(
        [w_in[:, :o_gn], w_in[:, o_sb:], w_in[:, o_gn:o_sb], jnp.zeros((D, LANE - gn_w), w_in.dtype)], axis=1)
    gate_col0 = q_w + kv_w + sb_w + mq_w
    P = _in_proj(x, row(attn_norm), w_re.astype(BF16))

    (qn, kc_raw, vc_raw, ksn, kwn, vsT, vwT, sbq, sbk, sbvT, memq, gT) = _prep(
        P, pos_col, inv2, sgn, row(nsa_q_norm), row(nsa_ks_norm), row(nsa_kw_norm), row(mem_q_norm))

    gates = gT[:gn_w].reshape(NSA_GROUPS, NSA_HPG, 3, T).transpose(2, 0, 1, 3)
    gates = jnp.pad(gates, ((0, 0), (0, 0), (0, 8 - NSA_HPG), (0, 0)))

    n_pad = T // CMP_STRIDE
    half = CMP_LEN // 2

    def w1_pack(w1):
        return jnp.concatenate([w1[:half].reshape(half * hd, -1), w1[half:].reshape(half * hd, -1)], axis=1).astype(BF16)

    def pe_pack(pe):
        return jnp.pad(pe.reshape(2, half * hd), ((0, 6), (0, 0))).astype(BF16)

    kc, vcT = _compress(
        kc_raw.reshape(NSA_GROUPS, n_pad, CMP_STRIDE * hd), vc_raw.reshape(NSA_GROUPS, n_pad, CMP_STRIDE * hd),
        w1_pack(cmp_k_w1), w1_pack(cmp_v_w1), pe_pack(cmp_k_pe), pe_pack(cmp_v_pe),
        cmp_k_w2.astype(BF16), cmp_v_w2.astype(BF16), row(nsa_kc_norm), posc_col, inv2, sgn)

    y_cmp, bias = _cmp_select(qn, kc, vcT, ovT, gates)
    y_sel = _nsa_flash("sel", qn, ksn, vsT, gates, bias)
    y_win = _nsa_flash("win", qn, kwn, vwT, gates)
    y_sb = _stick_breaking(sbq, sbk, sbvT)

    mk, mvT = _memkv(mem, row(mem_norm), w_mem_kv.astype(BF16), row(mem_k_norm))
    y_mem = _mem_attention(memq, mk, mvT)

    mixed = _merge(y_cmp, y_sel, y_win, y_sb, y_mem, P,
                   w_o_nsa.astype(BF16), w_o_sb.astype(BF16), w_o_mem.astype(BF16), gate_col0)
    x2, h2 = _out_proj(mixed, w_out.astype(BF16), x, row(ffn_norm))
    return _ffn(h2, w_ffn_gate.astype(BF16), w_ffn_up.astype(BF16), w_ffn_down.astype(BF16), x2)


def kernel(x, mem, positions, attn_norm, w_in, nsa_q_norm, nsa_kc_norm, nsa_ks_norm, nsa_kw_norm, cmp_k_pe, cmp_k_w1, cmp_k_w2, cmp_v_pe, cmp_v_w1, cmp_v_w2, mem_norm, w_mem_kv, mem_q_norm, mem_k_norm, w_o_nsa, w_o_sb, w_o_mem, w_out, ffn_norm, w_ffn_gate, w_ffn_up, w_ffn_down):
    B, T, D = x.shape
    assert T % (4 * LANE) == 0 and T // SEL_BLOCK >= 8
    n_pad = T // CMP_STRIDE
    n_sel = T // SEL_BLOCK
    inv = 1.0 / (ROPE_THETA ** (jnp.arange(0, HEAD_DIM, 2, dtype=F32) / HEAD_DIM))
    inv2 = jnp.concatenate([inv, inv]).reshape(1, HEAD_DIM)
    sgn = jnp.concatenate([-jnp.ones((HEAD_DIM // 2,), F32), jnp.ones((HEAD_DIM // 2,), F32)]).reshape(1, HEAD_DIM)
    cs = np.arange(n_pad)[None, :] * CMP_STRIDE
    ss = np.arange(n_sel)[:, None] * SEL_BLOCK
    ovT = jnp.asarray(((cs < ss + SEL_BLOCK) & (cs + CMP_LEN - 1 >= ss)).astype(np.float32), BF16)
    consts = (inv2, sgn, ovT)
    depth = w_in.shape[0]
    outs = []
    for b in range(B):
        xb = x[b]
        posf = positions[b].astype(F32)
        pos_col = posf.reshape(T, 1)
        posc = jnp.concatenate([posf[CMP_LEN - 1::CMP_STRIDE], posf[-1:]]).reshape(n_pad, 1)
        for l in range(depth):
            xb = _layer(xb, mem[b], pos_col, posc, consts, attn_norm[l], w_in[l], nsa_q_norm[l], nsa_kc_norm[l],
                        nsa_ks_norm[l], nsa_kw_norm[l], cmp_k_pe[l], cmp_k_w1[l], cmp_k_w2[l], cmp_v_pe[l],
                        cmp_v_w1[l], cmp_v_w2[l], mem_norm[l], w_mem_kv[l], mem_q_norm[l], mem_k_norm[l],
                        w_o_nsa[l], w_o_sb[l], w_o_mem[l], w_out[l], ffn_norm[l],
                        w_ffn_gate[l], w_ffn_up[l], w_ffn_down[l])
        outs.append(xb)
    return jnp.stack(outs, axis=0)
```

```python
import functools

import numpy as np
import jax
import jax.numpy as jnp
from jax import lax
from jax.experimental import pallas as pl
from jax.experimental.pallas import tpu as pltpu

HEAD_DIM = 128
NSA_HEADS = 8
NSA_GROUPS = 2
NSA_HPG = NSA_HEADS // NSA_GROUPS
SB_HEADS = 4
MEM_HEADS = 4
CMP_LEN = 32
CMP_STRIDE = 16
CMP_HIDDEN = 2 * HEAD_DIM
SEL_BLOCK = 64
SEL_TOPK = 16
WINDOW = 512
ROPE_THETA = 10000.0
NORM_EPS = 1e-6
NEG_BIG = -1e30
N_BRANCH = 3
SCALE = HEAD_DIM ** -0.5
LOG2E = 1.4426950408889634
QSCALE = SCALE * LOG2E
SEL_TK = 512
SEL_TILE_BLOCKS = SEL_TK // SEL_BLOCK
SB_DEAD_LOG2 = -160.0
VAUG = HEAD_DIM + 16

LANE = 128
MIB = 1 << 20
BF16 = jnp.bfloat16
F32 = jnp.float32

_NT = (((1,), (1,)), ((), ()))


def _tile(n, pref):
    t = min(n, pref)
    assert n % t == 0, (n, pref)
    return t


def _params(sem, vmem_mib):
    return pltpu.CompilerParams(dimension_semantics=sem, vmem_limit_bytes=vmem_mib * MIB)


def _rms(x, gain):
    return x * lax.rsqrt(jnp.mean(x * x, axis=-1, keepdims=True) + NORM_EPS) * gain


def _rope_tables(pos, inv2, sgn):
    ang = pos * inv2
    return jnp.cos(ang), jnp.sin(ang) * sgn


def _rope(x, c, s):
    return x * c + pltpu.roll(x, HEAD_DIM // 2, 1) * s


def _proj_kernel(x_ref, g_ref, w_ref, wg_ref, o_ref, og_ref, hn_ref):
    @pl.when(pl.program_id(1) == 0)
    def _():
        hn_ref[...] = _rms(x_ref[...], g_ref[...]).astype(BF16)
        og_ref[...] = jnp.dot(hn_ref[...], wg_ref[...], preferred_element_type=F32)

    o_ref[...] = jnp.dot(hn_ref[...], w_ref[...], preferred_element_type=F32).astype(BF16)


def _in_proj(x, gain, w_main, w_gate):
    T, D = x.shape
    N = w_main.shape[1]
    tm = _tile(T, 1024)
    tn = 1536
    assert N % tn == 0
    return pl.pallas_call(
        _proj_kernel,
        grid=(T // tm, N // tn),
        in_specs=[
            pl.BlockSpec((tm, D), lambda i, j: (i, 0)),
            pl.BlockSpec((1, D), lambda i, j: (0, 0)),
            pl.BlockSpec((D, tn), lambda i, j: (0, j)),
            pl.BlockSpec((D, LANE), lambda i, j: (0, 0)),
        ],
        out_specs=[pl.BlockSpec((tm, tn), lambda i, j: (i, j)), pl.BlockSpec((tm, LANE), lambda i, j: (i, 0))],
        out_shape=[jax.ShapeDtypeStruct((T, N), BF16), jax.ShapeDtypeStruct((T, LANE), F32)],
        scratch_shapes=[pltpu.VMEM((tm, D), BF16)],
        compiler_params=_params(("parallel", "arbitrary"), 52),
        name="in_proj",
    )(x, gain, w_main, w_gate)


def _prep_kernel(pos_ref, inv_ref, sgn_ref, gq_ref, gks_ref, gkw_ref, gmq_ref,
                 q_ref, kc_ref, vc_ref, ks_ref, vs_ref, kw_ref, vw_ref,
                 sq_ref, sk_ref, sv_ref, mq_ref, gn_ref,
                 qn_o, qT_o, kc_o, vc_o, ksa_o, kw_o, vsT_o, vwT_o, sq_o, sk_o, svT_o, mq_o, gT_o):
    c, s = _rope_tables(pos_ref[...], inv_ref[...], sgn_ref[...])
    hd = HEAD_DIM
    tp = pos_ref.shape[0]
    f32 = lambda ref, sl: ref[:, sl].astype(F32)
    for h in range(NSA_HEADS):
        sl = slice(h * hd, (h + 1) * hd)
        qh = _rope(_rms(f32(q_ref, sl), gq_ref[...]), c, s) * QSCALE
        qn_o[:, sl] = qh.astype(BF16)
        qT_o[sl, :] = qh.T.astype(BF16)
    ones_rows = (lax.broadcasted_iota(jnp.int32, (VAUG - hd, tp), 0) == 0).astype(F32).astype(BF16)
    row = pl.program_id(0) * tp + lax.broadcasted_iota(jnp.int32, (tp, hd), 0)
    blk = lax.shift_right_logical(row, 6) & (SEL_TILE_BLOCKS - 1)
    onehot = (lax.broadcasted_iota(jnp.int32, (tp, hd), 1) == blk).astype(F32).astype(BF16)
    for g in range(NSA_GROUPS):
        sl = slice(g * hd, (g + 1) * hd)
        kc_o[g] = kc_ref[:, sl]
        vc_o[g] = vc_ref[:, sl]
        ksa_o[g, :, :hd] = _rope(_rms(f32(ks_ref, sl), gks_ref[...]), c, s).astype(BF16)
        ksa_o[g, :, hd:] = onehot
        kw_o[:, sl] = _rope(_rms(f32(kw_ref, sl), gkw_ref[...]), c, s).astype(BF16)
        vsT_o[g * VAUG:g * VAUG + hd, :] = f32(vs_ref, sl).T.astype(BF16)
        vsT_o[g * VAUG + hd:(g + 1) * VAUG, :] = ones_rows
        vwT_o[g * VAUG:g * VAUG + hd, :] = f32(vw_ref, sl).T.astype(BF16)
        vwT_o[g * VAUG + hd:(g + 1) * VAUG, :] = ones_rows
    for h in range(SB_HEADS):
        sl = slice(h * hd, (h + 1) * hd)
        sq_o[:, sl] = (f32(sq_ref, sl) * QSCALE).astype(BF16)
        svT_o[sl, :] = f32(sv_ref, sl).T.astype(BF16)
    sk_o[...] = sk_ref[...]
    for h in range(MEM_HEADS):
        sl = slice(h * hd, (h + 1) * hd)
        mq_o[:, sl] = (_rms(f32(mq_ref, sl), gmq_ref[...]) * QSCALE).astype(BF16)
    gT_o[...] = jax.nn.sigmoid(gn_ref[...]).T


def _prep(P, Pg, pos_col, inv2, sgn, gq, gks, gkw, gmq):
    T = P.shape[0]
    tp = _tile(T, 512)
    hd = HEAD_DIM
    row = lambda w, c: pl.BlockSpec((tp, w), lambda i, c=c: (i, c))
    const = lambda: pl.BlockSpec((1, hd), lambda i: (0, 0))
    in_specs = [pl.BlockSpec((tp, 1), lambda i: (i, 0)), const(), const(), const(), const(), const(), const(),
                row(8 * hd, 0),
                row(2 * hd, 4), row(2 * hd, 5),
                row(2 * hd, 6), row(2 * hd, 7),
                row(2 * hd, 8), row(2 * hd, 9),
                row(4 * hd, 5), row(4 * hd, 6), row(4 * hd, 7),
                row(4 * hd, 8),
                row(hd, 0)]
    colT = lambda w: pl.BlockSpec((w, tp), lambda i: (0, i))
    grp = lambda w: pl.BlockSpec((NSA_GROUPS, tp, w), lambda i: (0, i, 0))
    out_specs = [row(8 * hd, 0), colT(8 * hd), grp(hd), grp(hd), grp(2 * hd),
                 row(2 * hd, 0), colT(NSA_GROUPS * VAUG), colT(NSA_GROUPS * VAUG),
                 row(4 * hd, 0), row(4 * hd, 0), colT(4 * hd), row(4 * hd, 0), colT(hd)]
    sds = jax.ShapeDtypeStruct
    out_shape = [sds((T, 8 * hd), BF16), sds((8 * hd, T), BF16),
                 sds((NSA_GROUPS, T, hd), BF16), sds((NSA_GROUPS, T, hd), BF16), sds((NSA_GROUPS, T, 2 * hd), BF16),
                 sds((T, 2 * hd), BF16),
                 sds((NSA_GROUPS * VAUG, T), BF16), sds((NSA_GROUPS * VAUG, T), BF16),
                 sds((T, 4 * hd), BF16), sds((T, 4 * hd), BF16), sds((4 * hd, T), BF16), sds((T, 4 * hd), BF16),
                 sds((hd, T), F32)]
    return pl.pallas_call(
        _prep_kernel, grid=(T // tp,), in_specs=in_specs, out_specs=out_specs, out_shape=out_shape,
        compiler_params=_params(("parallel",), 48), name="prep",
    )(pos_col, inv2, sgn, gq, gks, gkw, gmq, *([P] * 11), Pg)


def _gelu_tanh(x):
    return 0.5 * x * (1.0 + jnp.tanh(0.7978845608028654 * (x + 0.044715 * (x * x * x))))


def _compress_one(x, w1, pe, w2):
    n = x.shape[0]
    ab = jnp.dot(x, w1, preferred_element_type=F32)
    pr = jnp.dot(pe, w1, preferred_element_type=F32)
    pec = pr[0:1, :CMP_HIDDEN] + pr[1:2, CMP_HIDDEN:]
    hid = ab[:, :CMP_HIDDEN] + pltpu.roll(ab[:, CMP_HIDDEN:], n - 1, 0) + pec
    return jnp.dot(_gelu_tanh(hid).astype(BF16), w2, preferred_element_type=F32)


def _compress_kernel(xk_ref, xv_ref, w1k_ref, w1v_ref, pek_ref, pev_ref, w2k_ref, w2v_ref,
                     gk_ref, pos_ref, inv_ref, sgn_ref, kc_o, vcT_o):
    c, s = _rope_tables(pos_ref[...], inv_ref[...], sgn_ref[...])
    k = _compress_one(xk_ref[0], w1k_ref[...], pek_ref[...], w2k_ref[...])
    kc_o[0] = _rope(_rms(k, gk_ref[...]), c, s).astype(BF16)
    v = _compress_one(xv_ref[0], w1v_ref[...], pev_ref[...], w2v_ref[...])
    vcT_o[0] = v.T.astype(BF16)


def _compress(xk, xv, w1k, w1v, pek, pev, w2k, w2v, gk, posc, inv2, sgn):
    G, n, W = xk.shape
    full = lambda a: pl.BlockSpec(a.shape, lambda g: (0,) * a.ndim)
    grp = pl.BlockSpec((1, n, W), lambda g: (g, 0, 0))
    return pl.pallas_call(
        _compress_kernel, grid=(G,),
        in_specs=[grp, grp, full(w1k), full(w1v), full(pek), full(pev), full(w2k), full(w2v),
                  full(gk), full(posc), full(inv2), full(sgn)],
        out_specs=[pl.BlockSpec((1, n, HEAD_DIM), lambda g: (g, 0, 0)),
                   pl.BlockSpec((1, HEAD_DIM, n), lambda g: (g, 0, 0))],
        out_shape=[jax.ShapeDtypeStruct((G, n, HEAD_DIM), BF16), jax.ShapeDtypeStruct((G, HEAD_DIM, n), BF16)],
        compiler_params=_params(("parallel",), 48), name="compress",
    )(xk, xv, w1k, w1v, pek, pev, w2k, w2v, gk, posc, inv2, sgn)


def _memkv_kernel(mem_ref, gm_ref, w_ref, gk_ref, k_o, vT_o):
    hn = _rms(mem_ref[...], gm_ref[...]).astype(BF16)
    kv = jnp.dot(hn, w_ref[...], preferred_element_type=F32)
    mw = MEM_HEADS * HEAD_DIM
    for h in range(MEM_HEADS):
        sl = slice(h * HEAD_DIM, (h + 1) * HEAD_DIM)
        k_o[:, sl] = _rms(kv[:, sl], gk_ref[...]).astype(BF16)
        vT_o[sl, :] = kv[:, mw + h * HEAD_DIM: mw + (h + 1) * HEAD_DIM].T.astype(BF16)


def _memkv(mem, gm, w, gk):
    M = mem.shape[0]
    mw = MEM_HEADS * HEAD_DIM
    return pl.pallas_call(
        _memkv_kernel,
        out_shape=[jax.ShapeDtypeStruct((M, mw), BF16), jax.ShapeDtypeStruct((mw, M), BF16)],
        compiler_params=pltpu.CompilerParams(vmem_limit_bytes=32 * MIB), name="mem_kv",
    )(mem, gm, w, gk)


def _cmp_kernel(q_ref, kc_ref, vcT_ref, ovT_ref, g_ref, y_ref, b_ref, *, tq, n_pad, n_sel, top_k):
    i = pl.program_id(1)
    t = i * tq + lax.broadcasted_iota(jnp.int32, (1, tq), 1)
    n_end = lax.broadcasted_iota(jnp.int32, (n_pad, 1), 0) * CMP_STRIDE + (CMP_LEN - 1)
    valid = n_end <= t
    kc = kc_ref[0]
    vcT = vcT_ref[0]
    has_valid = (t >= CMP_LEN - 1).astype(F32)
    sms = []
    for h in range(NSA_HPG):
        q_h = q_ref[:, h * HEAD_DIM:(h + 1) * HEAD_DIM]
        sms.append(jnp.where(valid, lax.dot_general(kc, q_h, _NT, preferred_element_type=F32), NEG_BIG))
    psum = jnp.zeros((n_pad, tq), F32)
    for h in range(NSA_HPG):
        e = jnp.exp2(sms[h] - jnp.max(sms[h], axis=0, keepdims=True))
        p = e * (has_valid / jnp.sum(e, axis=0, keepdims=True))
        oT = jnp.dot(vcT, p.astype(BF16), preferred_element_type=F32)
        y_ref[:, h * HEAD_DIM:(h + 1) * HEAD_DIM] = (oT * g_ref[0, 0, h:h + 1, :]).T.astype(BF16)
        psum = psum + p
    hi = psum.astype(BF16)
    r1 = psum - hi.astype(F32)
    mid = r1.astype(BF16)
    lo = (r1 - mid.astype(F32)).astype(BF16)
    parts = jnp.dot(ovT_ref[...], jnp.concatenate([hi, mid, lo], axis=1), preferred_element_type=F32)
    imp = parts[:, :tq] + parts[:, tq:2 * tq] + parts[:, 2 * tq:]
    s_i = lax.broadcasted_iota(jnp.int32, (n_sel, 1), 0)
    cur = lax.shift_right_logical(t, 6)
    forced = (s_i == 0) | (s_i == cur) | (s_i == cur - 1)
    future = s_i * SEL_BLOCK > t
    w = jnp.where(forced, jnp.inf, jnp.where(future, -jnp.inf, imp))
    s_f = jnp.broadcast_to(s_i.astype(F32), (n_sel, tq))
    for _ in range(top_k):
        m = jnp.max(w, axis=0, keepdims=True)
        idx = jnp.min(jnp.where(w == m, s_f, float(n_sel)), axis=0, keepdims=True)
        w = jnp.where(s_f == idx, -jnp.inf, w)
    b_ref[0] = jnp.where(future, NEG_BIG, jnp.where(w == -jnp.inf, 0.0, NEG_BIG))


def _cmp_select(qn, kc, vcT, ovT, gates):
    T = qn.shape[0]
    G, n_pad, _ = kc.shape
    n_sel = T // SEL_BLOCK
    tq = _tile(T, 256)
    gw = NSA_HPG * HEAD_DIM
    kern = functools.partial(_cmp_kernel, tq=tq, n_pad=n_pad, n_sel=n_sel, top_k=min(SEL_TOPK, n_sel))
    return pl.pallas_call(
        kern, grid=(G, T // tq),
        in_specs=[pl.BlockSpec((tq, gw), lambda g, i: (i, g)),
                  pl.BlockSpec((1, n_pad, HEAD_DIM), lambda g, i: (g, 0, 0)),
                  pl.BlockSpec((1, HEAD_DIM, n_pad), lambda g, i: (g, 0, 0)),
                  pl.BlockSpec((n_sel, n_pad), lambda g, i: (0, 0)),
                  pl.BlockSpec((1, 1, 8, tq), lambda g, i: (0, g, 0, i))],
        out_specs=[pl.BlockSpec((tq, gw), lambda g, i: (i, g)),
                   pl.BlockSpec((1, n_sel, tq), lambda g, i: (g, 0, i))],
        out_shape=[jax.ShapeDtypeStruct((T, NSA_HEADS * HEAD_DIM), BF16),
                   jax.ShapeDtypeStruct((G, n_sel, T), F32)],
        compiler_params=_params(("parallel", "parallel"), 48), name="cmp_select",
    )(qn, kc, vcT, ovT, gates)


def _flash_kernel(qi_ref, kj_ref, fl_ref, q_ref, k_ref, vT_ref, g_ref, *rest, mode, tq, tk):
    if mode == "sel":
        bias_ref, o_ref, m_sc, acc_sc = rest
    else:
        o_ref, m_sc, acc_sc = rest
    p = pl.program_id(1)
    i = qi_ref[p]
    j = kj_ref[p]
    fl = fl_ref[p]

    @pl.when((fl & 1) != 0)
    def _():
        m_sc[...] = jnp.full(m_sc.shape, NEG_BIG, F32)
        acc_sc[...] = jnp.zeros(acc_sc.shape, F32)

    t = i * tq + lax.broadcasted_iota(jnp.int32, (1, tq), 1)
    kpos = j * tk + lax.broadcasted_iota(jnp.int32, (tk, 1), 0)
    if mode == "sel":
        nb = tk // SEL_BLOCK
        base = jnp.concatenate(
            [jnp.broadcast_to(bias_ref[0, b:b + 1, :], (SEL_BLOCK, tq)) for b in range(nb)], axis=0)
        mask_add = jnp.where(kpos <= t, base, NEG_BIG)
    else:
        mask_add = jnp.where((kpos <= t) & (kpos > t - WINDOW), 0.0, NEG_BIG)
    k = k_ref[...]
    vT = vT_ref[...]
    sTs = []
    for h in range(NSA_HPG):
        q_h = q_ref[:, h * HEAD_DIM:(h + 1) * HEAD_DIM]
        sTs.append(lax.dot_general(k, q_h, _NT, preferred_element_type=F32) + mask_add)
    pTs, alphas = [], []
    for h in range(NSA_HPG):
        sT = sTs[h]
        m_old = m_sc[h]
        m_new = jnp.maximum(m_old, jnp.max(sT, axis=0, keepdims=True))
        alphas.append(jnp.exp2(m_old - m_new))
        pTs.append(jnp.exp2((sT - m_new).astype(BF16)))
        m_sc[h] = m_new
    for h in range(NSA_HPG):
        acc_sc[h] = alphas[h] * acc_sc[h] + jnp.dot(vT, pTs[h], preferred_element_type=F32)

    @pl.when((fl & 2) != 0)
    def _():
        for h in range(NSA_HPG):
            acc = acc_sc[h]
            l = acc[HEAD_DIM:HEAD_DIM + 1, :]
            o = acc[:HEAD_DIM, :] * ((1.0 / l) * g_ref[0, 0, h:h + 1, :])
            o_ref[:, h * HEAD_DIM:(h + 1) * HEAD_DIM] = o.T.astype(BF16)


def _steps(nq, lo_fn, hi_fn, reverse=False):
    qi, kj, fl = [], [], []
    for i in range(nq):
        js = list(range(lo_fn(i), hi_fn(i) + 1))
        if reverse:
            js = js[::-1]
        for n, j in enumerate(js):
            qi.append(i)
            kj.append(j)
            fl.append((1 if n == 0 else 0) | (2 if n == len(js) - 1 else 0))
    return (jnp.asarray(np.array(qi, np.int32)), jnp.asarray(np.array(kj, np.int32)),
            jnp.asarray(np.array(fl, np.int32)))


def _nsa_flash(mode, qn, k, vT, gates, bias=None):
    T = qn.shape[0]
    tq = _tile(T, 1024)
    tk = _tile(T, 512)
    nq = T // tq
    gw = NSA_HPG * HEAD_DIM
    hi = lambda i: ((i + 1) * tq - 1) // tk
    if mode == "sel":
        lo = lambda i: 0
        br = 1
    else:
        lo = lambda i: max(0, (i * tq - (WINDOW - 1)) // tk)
        br = 2
    qi, kj, fl = _steps(nq, lo, hi)
    in_specs = [pl.BlockSpec((tq, gw), lambda g, p, qi, kj, fl: (qi[p], g)),
                pl.BlockSpec((tk, HEAD_DIM), lambda g, p, qi, kj, fl: (kj[p], g)),
                pl.BlockSpec((VAUG, tk), lambda g, p, qi, kj, fl: (g, kj[p])),
                pl.BlockSpec((1, 1, 8, tq), lambda g, p, qi, kj, fl, br=br: (br, g, 0, qi[p]))]
    args = [qn, k, vT, gates]
    if mode == "sel":
        in_specs.append(pl.BlockSpec((1, tk // SEL_BLOCK, tq), lambda g, p, qi, kj, fl: (g, kj[p], qi[p])))
        args.append(bias)
    kern = functools.partial(_flash_kernel, mode=mode, tq=tq, tk=tk)
    return pl.pallas_call(
        kern,
        grid_spec=pltpu.PrefetchScalarGridSpec(
            num_scalar_prefetch=3, grid=(NSA_GROUPS, int(qi.shape[0])),
            in_specs=in_specs,
            out_specs=pl.BlockSpec((tq, gw), lambda g, p, qi, kj, fl: (qi[p], g)),
            scratch_shapes=[pltpu.VMEM((NSA_HPG, 1, tq), F32), pltpu.VMEM((NSA_HPG, VAUG, tq), F32)]),
        out_shape=jax.ShapeDtypeStruct((T, NSA_HEADS * HEAD_DIM), BF16),
        compiler_params=_params(("parallel", "arbitrary"), 48), name="nsa_" + mode,
    )(qi, kj, fl, *args)


def _sel_kernel(qi_ref, kj_ref, fl_ref, qT_ref, ka_ref, vT_ref, g_ref, bias_ref, o_ref, m_sc, acc_sc, qa_sc,
                *, tq, tk):
    p = pl.program_id(1)
    i = qi_ref[p]
    j = kj_ref[p]
    fl = fl_ref[p]
    hd = HEAD_DIM
    nb = tk // SEL_BLOCK

    @pl.when((fl & 1) != 0)
    def _():
        m_sc[...] = jnp.full(m_sc.shape, NEG_BIG, F32)
        acc_sc[...] = jnp.zeros(acc_sc.shape, F32)
        for h in range(NSA_HPG):
            qa_sc[h, :hd, :] = qT_ref[h * hd:(h + 1) * hd, :]
            qa_sc[h, hd:, :] = jnp.zeros((hd, tq), BF16)

    mask_rows = jnp.concatenate([bias_ref[0], jnp.zeros((16 - nb, tq), F32)], axis=0).astype(BF16)
    for h in range(NSA_HPG):
        qa_sc[h, hd:hd + 16, :] = mask_rows
    ka = ka_ref[0]
    vT = vT_ref[...]

    def body(diag):
        if diag:
            t = i * tq + lax.broadcasted_iota(jnp.int32, (1, tq), 1)
            kpos = j * tk + lax.broadcasted_iota(jnp.int32, (tk, 1), 0)
            ok = kpos <= t

        def scores(h):
            s = jnp.dot(ka, qa_sc[h], preferred_element_type=F32)
            return jnp.where(ok, s, NEG_BIG) if diag else s

        def softmax(h, s):
            m_old = m_sc[h]
            m_new = jnp.maximum(m_old, jnp.max(s, axis=0, keepdims=True))
            m_sc[h] = m_new
            return jnp.exp2(m_old - m_new), jnp.exp2((s - m_new).astype(BF16))

        def accumulate(h, alpha, pT):
            acc_sc[h] = alpha * acc_sc[h] + jnp.dot(vT, pT, preferred_element_type=F32)

        nh = NSA_HPG
        s = {0: scores(0), 1: scores(1)}
        sm = {}
        for h in range(nh):
            sm[h] = softmax(h, s.pop(h))
            if h + 2 < nh:
                s[h + 2] = scores(h + 2)
            if h >= 1:
                accumulate(h - 1, *sm.pop(h - 1))
        accumulate(nh - 1, *sm.pop(nh - 1))

    pl.when((fl & 4) != 0)(lambda: body(True))
    pl.when((fl & 4) == 0)(lambda: body(False))

    @pl.when((fl & 2) != 0)
    def _():
        for h in range(NSA_HPG):
            acc = acc_sc[h]
            l = acc[hd:hd + 1, :]
            o = acc[:hd, :] * ((1.0 / l) * g_ref[0, 0, h:h + 1, :])
            o_ref[:, h * hd:(h + 1) * hd] = o.T.astype(BF16)


def _nsa_sel(qT, ka, vT, gates, bias):
    T = qT.shape[1]
    tq = _tile(T, 1024)
    tk = SEL_TK
    nq = T // tq
    gw = NSA_HPG * HEAD_DIM
    qi, kj, fl = _steps(nq, lambda i: 0, lambda i: ((i + 1) * tq - 1) // tk)
    diag = ((kj + 1) * tk - 1 > qi * tq).astype(jnp.int32) * 4
    fl = fl | diag
    return pl.pallas_call(
        functools.partial(_sel_kernel, tq=tq, tk=tk),
        grid_spec=pltpu.PrefetchScalarGridSpec(
            num_scalar_prefetch=3, grid=(NSA_GROUPS, int(qi.shape[0])),
            in_specs=[pl.BlockSpec((gw, tq), lambda g, p, qi, kj, fl: (g, qi[p])),
                      pl.BlockSpec((1, tk, 2 * HEAD_DIM), lambda g, p, qi, kj, fl: (g, kj[p], 0)),
                      pl.BlockSpec((VAUG, tk), lambda g, p, qi, kj, fl: (g, kj[p])),
                      pl.BlockSpec((1, 1, 8, tq), lambda g, p, qi, kj, fl: (1, g, 0, qi[p])),
                      pl.BlockSpec((1, tk // SEL_BLOCK, tq), lambda g, p, qi, kj, fl: (g, kj[p], qi[p]))],
            out_specs=pl.BlockSpec((tq, gw), lambda g, p, qi, kj, fl: (qi[p], g)),
            scratch_shapes=[pltpu.VMEM((NSA_HPG, 1, tq), F32), pltpu.VMEM((NSA_HPG, VAUG, tq), F32),
                            pltpu.VMEM((NSA_HPG, 2 * HEAD_DIM, tq), BF16)]),
        out_shape=jax.ShapeDtypeStruct((T, NSA_HEADS * HEAD_DIM), BF16),
        compiler_params=_params(("parallel", "arbitrary"), 48), name="nsa_sel",
    )(qi, kj, fl, qT, ka, vT, gates, bias)


def _sb_kernel(qi_ref, kj_ref, fl_ref, q_ref, k_ref, vT_ref, lm_ref, o_ref, carry_sc, acc_sc, *, tq, tk):
    p = pl.program_id(0)
    i = qi_ref[p]
    j = kj_ref[p]
    fl = fl_ref[p]

    @pl.when((fl & 1) != 0)
    def _():
        carry_sc[...] = jnp.zeros(carry_sc.shape, F32)
        acc_sc[...] = jnp.zeros(acc_sc.shape, F32)

    @pl.when(jnp.max(carry_sc[...]) > SB_DEAD_LOG2)
    def _():
        t = i * tq + lax.broadcasted_iota(jnp.int32, (1, tq), 1)
        kpos = j * tk + lax.broadcasted_iota(jnp.int32, (tk, 1), 0)
        past = kpos < t
        lm = lm_ref[...]
        hs = [slice(h * HEAD_DIM, (h + 1) * HEAD_DIM) for h in range(SB_HEADS)]
        zs = [lax.dot_general(k_ref[:, sl], q_ref[:, sl], _NT, preferred_element_type=F32) for sl in hs]
        lgs, css = [], []
        for h in range(SB_HEADS):
            z = zs[h]
            sp = jnp.maximum(z, 0.0) + jnp.log2(1.0 + jnp.exp2(-jnp.abs(z)))
            lk = jnp.where(past, -sp, 0.0)
            hi = lk.astype(BF16)
            lo = (lk - hi.astype(F32)).astype(BF16)
            css.append(jnp.dot(lm, jnp.concatenate([hi, lo], axis=1), preferred_element_type=F32))
            lgs.append(z - sp)
            carry_old = carry_sc[h]
            carry_sc[h] = carry_old + jnp.sum(lk, axis=0, keepdims=True)
            css[h] = css[h][:, :tq] + css[h][:, tq:] + carry_old
        for h in range(SB_HEADS):
            wgt = jnp.where(past, jnp.exp2(lgs[h] + css[h]), 0.0)
            acc_sc[h] = acc_sc[h] + jnp.dot(vT_ref[hs[h], :], wgt.astype(BF16), preferred_element_type=F32)

    @pl.when((fl & 2) != 0)
    def _():
        for h in range(SB_HEADS):
            o_ref[:, h * HEAD_DIM:(h + 1) * HEAD_DIM] = acc_sc[h].T.astype(BF16)


def _stick_breaking(q, k, vT):
    T, W = q.shape
    tq = _tile(T, 512)
    tk = _tile(T, 512)
    nq = T // tq
    qi, kj, fl = _steps(nq, lambda i: 0, lambda i: ((i + 1) * tq - 2) // tk, reverse=True)
    lmat = jnp.asarray(np.triu(np.ones((tk, tk), np.float32), 1), BF16)
    kern = functools.partial(_sb_kernel, tq=tq, tk=tk)
    return pl.pallas_call(
        kern,
        grid_spec=pltpu.PrefetchScalarGridSpec(
            num_scalar_prefetch=3, grid=(int(qi.shape[0]),),
            in_specs=[pl.BlockSpec((tq, W), lambda p, qi, kj, fl: (qi[p], 0)),
                      pl.BlockSpec((tk, W), lambda p, qi, kj, fl: (kj[p], 0)),
                      pl.BlockSpec((W, tk), lambda p, qi, kj, fl: (0, kj[p])),
                      pl.BlockSpec((tk, tk), lambda p, qi, kj, fl: (0, 0))],
            out_specs=pl.BlockSpec((tq, W), lambda p, qi, kj, fl: (qi[p], 0)),
            scratch_shapes=[pltpu.VMEM((SB_HEADS, 1, tq), F32), pltpu.VMEM((SB_HEADS, HEAD_DIM, tq), F32)]),
        out_shape=jax.ShapeDtypeStruct((T, W), BF16),
        compiler_params=_params(("arbitrary",), 48), name="stick_breaking",
    )(qi, kj, fl, q, k, vT, lmat)


def _memattn_kernel(q_ref, k_ref, vT_ref, o_ref):
    for h in range(MEM_HEADS):
        sl = slice(h * HEAD_DIM, (h + 1) * HEAD_DIM)
        sT = lax.dot_general(k_ref[:, sl], q_ref[:, sl], _NT, preferred_element_type=F32)
        e = jnp.exp2(sT - jnp.max(sT, axis=0, keepdims=True))
        l = jnp.sum(e, axis=0, keepdims=True)
        oT = jnp.dot(vT_ref[sl, :], e.astype(BF16), preferred_element_type=F32) * (1.0 / l)
        o_ref[:, sl] = oT.T.astype(BF16)


def _mem_attention(q, k, vT):
    T, W = q.shape
    M = k.shape[0]
    tq = _tile(T, 512)
    return pl.pallas_call(
        _memattn_kernel, grid=(T // tq,),
        in_specs=[pl.BlockSpec((tq, W), lambda i: (i, 0)),
                  pl.BlockSpec((M, W), lambda i: (0, 0)),
                  pl.BlockSpec((W, M), lambda i: (0, 0))],
        out_specs=pl.BlockSpec((tq, W), lambda i: (i, 0)),
        out_shape=jax.ShapeDtypeStruct((T, W), BF16),
        compiler_params=_params(("parallel",), 32), name="mem_attention",
    )(q, k, vT)


def _merge_kernel(yc_ref, ys_ref, yw_ref, ysb_ref, ym_ref, g0_ref, g1_ref, g2_ref,
                  wn_ref, wsb_ref, wm_ref, o_ref, yn_sc):
    @pl.when(pl.program_id(1) == 0)
    def _():
        yn_sc[...] = (yc_ref[...].astype(F32) + ys_ref[...].astype(F32) + yw_ref[...].astype(F32)).astype(BF16)

    a = jnp.dot(yn_sc[...], wn_ref[...], preferred_element_type=F32)
    b = jnp.dot(ysb_ref[...], wsb_ref[...], preferred_element_type=F32)
    c = jnp.dot(ym_ref[...], wm_ref[...], preferred_element_type=F32)
    sig = lambda ref: jax.nn.sigmoid(ref[...].astype(F32))
    o = sig(g0_ref) * a + sig(g1_ref) * b + sig(g2_ref) * c
    o_ref[...] = o.astype(BF16)


def _merge(yc, ys, yw, ysb, ym, P, wn, wsb, wm, gate_col0):
    T = yc.shape[0]
    D = wn.shape[1]
    tm = _tile(T, 512)
    tn = 512
    assert D % tn == 0 and gate_col0 % tn == 0
    gb = gate_col0 // tn
    nb = D // tn
    rowi = lambda w: pl.BlockSpec((tm, w), lambda i, j: (i, 0))
    gate = lambda b: pl.BlockSpec((tm, tn), lambda i, j, b=b: (i, gb + b * nb + j))
    wcol = lambda k: pl.BlockSpec((k, tn), lambda i, j: (0, j))
    return pl.pallas_call(
        _merge_kernel, grid=(T // tm, nb),
        in_specs=[rowi(yc.shape[1]), rowi(ys.shape[1]), rowi(yw.shape[1]), rowi(ysb.shape[1]), rowi(ym.shape[1]),
                  gate(0), gate(1), gate(2), wcol(wn.shape[0]), wcol(wsb.shape[0]), wcol(wm.shape[0])],
        out_specs=pl.BlockSpec((tm, tn), lambda i, j: (i, j)),
        out_shape=jax.ShapeDtypeStruct((T, D), BF16),
        scratch_shapes=[pltpu.VMEM((tm, yc.shape[1]), BF16)],
        compiler_params=_params(("parallel", "arbitrary"), 48), name="merge",
    )(yc, ys, yw, ysb, ym, P, P, P, wn, wsb, wm)


def _out_kernel(mix_ref, w_ref, x_ref, g_ref, x2_o, h2_o):
    x2 = x_ref[...] + jnp.dot(mix_ref[...], w_ref[...], preferred_element_type=F32)
    x2_o[...] = x2
    h2_o[...] = _rms(x2, g_ref[...]).astype(BF16)


def _out_proj(mixed, w, x, gain):
    T, D = x.shape
    tm = _tile(T, 256)
    return pl.pallas_call(
        _out_kernel, grid=(T // tm,),
        in_specs=[pl.BlockSpec((tm, D), lambda i: (i, 0)),
                  pl.BlockSpec((D, D), lambda i: (0, 0)),
                  pl.BlockSpec((tm, D), lambda i: (i, 0)),
                  pl.BlockSpec((1, D), lambda i: (0, 0))],
        out_specs=[pl.BlockSpec((tm, D), lambda i: (i, 0)), pl.BlockSpec((tm, D), lambda i: (i, 0))],
        out_shape=[jax.ShapeDtypeStruct((T, D), F32), jax.ShapeDtypeStruct((T, D), BF16)],
        compiler_params=_params(("parallel",), 48), name="out_proj",
    )(mixed, w, x, gain)


def _ffn_kernel(h_ref, wg_ref, wu_ref, wd_ref, x_ref, o_ref):
    a = jnp.dot(h_ref[...], wg_ref[...], preferred_element_type=F32)
    b = jnp.dot(h_ref[...], wu_ref[...], preferred_element_type=F32)
    z = (a * jax.nn.sigmoid(a) * b).astype(BF16)
    c = jnp.dot(z, wd_ref[...], preferred_element_type=F32)

    @pl.when(pl.program_id(1) == 0)
    def _():
        o_ref[...] = x_ref[...] + c

    @pl.when(pl.program_id(1) != 0)
    def _():
        o_ref[...] += c


def _ffn(h2, wg, wu, wd, x2):
    T, D = x2.shape
    F = wg.shape[1]
    tm = _tile(T, 1024)
    tf = 512
    assert F % tf == 0
    once = pl.Buffered(1)
    return pl.pallas_call(
        _ffn_kernel, grid=(T // tm, F // tf),
        in_specs=[pl.BlockSpec((tm, D), lambda i, f: (i, 0), pipeline_mode=once),
                  pl.BlockSpec((D, tf), lambda i, f: (0, f)),
                  pl.BlockSpec((D, tf), lambda i, f: (0, f)),
                  pl.BlockSpec((tf, D), lambda i, f: (f, 0)),
                  pl.BlockSpec((tm, D), lambda i, f: (i, 0), pipeline_mode=once)],
        out_specs=pl.BlockSpec((tm, D), lambda i, f: (i, 0)),
        out_shape=jax.ShapeDtypeStruct((T, D), F32),
        compiler_params=_params(("parallel", "arbitrary"), 52), name="ffn",
    )(h2, wg, wu, wd, x2)


def _layer(x, mem, pos_col, posc_col, consts, attn_norm, w_in, nsa_q_norm, nsa_kc_norm, nsa_ks_norm, nsa_kw_norm,
           cmp_k_pe, cmp_k_w1, cmp_k_w2, cmp_v_pe, cmp_v_w1, cmp_v_w2, mem_norm, w_mem_kv,
           mem_q_norm, mem_k_norm, w_o_nsa, w_o_sb, w_o_mem, w_out, ffn_norm,
           w_ffn_gate, w_ffn_up, w_ffn_down):
    T, D = x.shape
    inv2, sgn, ovT = consts
    hd = HEAD_DIM
    row = lambda g: g.reshape(1, -1)

    q_w, kv_w, gn_w = NSA_HEADS * hd, 6 * NSA_GROUPS * hd, 3 * NSA_HEADS
    sb_w, mq_w, gm_w = 3 * SB_HEADS * hd, MEM_HEADS * hd, N_BRANCH * D
    o_gn = q_w + kv_w
    o_sb = o_gn + gn_w
    o_mq = o_sb + sb_w
    o_gm = o_mq + mq_w
    assert w_in.shape[1] == o_gm + gm_w
    w_main = jnp.concatenate([w_in[:, :o_gn].astype(BF16), w_in[:, o_sb:].astype(BF16)], axis=1)
    w_gate = jnp.pad(w_in[:, o_gn:o_sb], ((0, 0), (0, LANE - gn_w))).astype(BF16)
    gate_col0 = q_w + kv_w + sb_w + mq_w
    P, Pg = _in_proj(x, row(attn_norm), w_main, w_gate)

    (qn, qT, kc_raw, vc_raw, ksa, kwn, vsT, vwT, sbq, sbk, sbvT, memq, gT) = _prep(
        P, Pg, pos_col, inv2, sgn, row(nsa_q_norm), row(nsa_ks_norm), row(nsa_kw_norm), row(mem_q_norm))

    gates = gT[:gn_w].reshape(NSA_GROUPS, NSA_HPG, 3, T).transpose(2, 0, 1, 3)
    gates = jnp.pad(gates, ((0, 0), (0, 0), (0, 8 - NSA_HPG), (0, 0)))

    n_pad = T // CMP_STRIDE
    half = CMP_LEN // 2

    def w1_pack(w1):
        return jnp.concatenate([w1[:half].reshape(half * hd, -1), w1[half:].reshape(half * hd, -1)], axis=1).astype(BF16)

    def pe_pack(pe):
        return jnp.pad(pe.reshape(2, half * hd), ((0, 6), (0, 0))).astype(BF16)

    kc, vcT = _compress(
        kc_raw.reshape(NSA_GROUPS, n_pad, CMP_STRIDE * hd), vc_raw.reshape(NSA_GROUPS, n_pad, CMP_STRIDE * hd),
        w1_pack(cmp_k_w1), w1_pack(cmp_v_w1), pe_pack(cmp_k_pe), pe_pack(cmp_v_pe),
        cmp_k_w2.astype(BF16), cmp_v_w2.astype(BF16), row(nsa_kc_norm), posc_col, inv2, sgn)

    y_cmp, bias = _cmp_select(qn, kc, vcT, ovT, gates)
    y_sel = _nsa_sel(qT, ksa, vsT, gates, bias)
    y_win = _nsa_flash("win", qn, kwn, vwT, gates)
    y_sb = _stick_breaking(sbq, sbk, sbvT)

    mk, mvT = _memkv(mem, row(mem_norm), w_mem_kv.astype(BF16), row(mem_k_norm))
    y_mem = _mem_attention(memq, mk, mvT)

    mixed = _merge(y_cmp, y_sel, y_win, y_sb, y_mem, P,
                   w_o_nsa.astype(BF16), w_o_sb.astype(BF16), w_o_mem.astype(BF16), gate_col0)
    x2, h2 = _out_proj(mixed, w_out.astype(BF16), x, row(ffn_norm))
    return _ffn(h2, w_ffn_gate.astype(BF16), w_ffn_up.astype(BF16), w_ffn_down.astype(BF16), x2)


def kernel(x, mem, positions, attn_norm, w_in, nsa_q_norm, nsa_kc_norm, nsa_ks_norm, nsa_kw_norm, cmp_k_pe, cmp_k_w1, cmp_k_w2, cmp_v_pe, cmp_v_w1, cmp_v_w2, mem_norm, w_mem_kv, mem_q_norm, mem_k_norm, w_o_nsa, w_o_sb, w_o_mem, w_out, ffn_norm, w_ffn_gate, w_ffn_up, w_ffn_down):
    B, T, D = x.shape
    assert T % (4 * LANE) == 0 and T // SEL_BLOCK >= 8
    n_pad = T // CMP_STRIDE
    n_sel = T // SEL_BLOCK
    inv = 1.0 / (ROPE_THETA ** (jnp.arange(0, HEAD_DIM, 2, dtype=F32) / HEAD_DIM))
    inv2 = jnp.concatenate([inv, inv]).reshape(1, HEAD_DIM)
    sgn = jnp.concatenate([-jnp.ones((HEAD_DIM // 2,), F32), jnp.ones((HEAD_DIM // 2,), F32)]).reshape(1, HEAD_DIM)
    cs = np.arange(n_pad)[None, :] * CMP_STRIDE
    ss = np.arange(n_sel)[:, None] * SEL_BLOCK
    ovT = jnp.asarray(((cs < ss + SEL_BLOCK) & (cs + CMP_LEN - 1 >= ss)).astype(np.float32), BF16)
    consts = (inv2, sgn, ovT)
    depth = w_in.shape[0]
    outs = []
    for b in range(B):
        xb = x[b]
        posf = positions[b].astype(F32)
        pos_col = posf.reshape(T, 1)
        posc = jnp.concatenate([posf[CMP_LEN - 1::CMP_STRIDE], posf[-1:]]).reshape(n_pad, 1)
        for l in range(depth):
            xb = _layer(xb, mem[b], pos_col, posc, consts, attn_norm[l], w_in[l], nsa_q_norm[l], nsa_kc_norm[l],
                        nsa_ks_norm[l], nsa_kw_norm[l], cmp_k_pe[l], cmp_k_w1[l], cmp_k_w2[l], cmp_v_pe[l],
                        cmp_v_w1[l], cmp_v_w2[l], mem_norm[l], w_mem_kv[l], mem_q_norm[l], mem_k_norm[l],
                        w_o_nsa[l], w_o_sb[l], w_o_mem[l], w_out[l], ffn_norm[l],
                        w_ffn_gate[l], w_ffn_up[l], w_ffn_down[l])
        outs.append(xb)
    return jnp.stack(outs, axis=0)
```

```python
import functools

import numpy as np
import jax
import jax.numpy as jnp
from jax import lax
from jax.experimental import pallas as pl
from jax.experimental.pallas import tpu as pltpu

HEAD_DIM = 128
NSA_HEADS = 8
NSA_GROUPS = 2
NSA_HPG = NSA_HEADS // NSA_GROUPS
SB_HEADS = 4
MEM_HEADS = 4
CMP_LEN = 32
CMP_STRIDE = 16
CMP_HIDDEN = 2 * HEAD_DIM
SEL_BLOCK = 64
SEL_TOPK = 16
WINDOW = 512
ROPE_THETA = 10000.0
NORM_EPS = 1e-6
NEG_BIG = -1e30
N_BRANCH = 3
SCALE = HEAD_DIM ** -0.5
LOG2E = 1.4426950408889634
QSCALE = SCALE * LOG2E
SB_NEAR_TILES = 3
SB_DEAD_LOG2 = -160.0
VAUG = HEAD_DIM + 16

LANE = 128
MIB = 1 << 20
BF16 = jnp.bfloat16
F32 = jnp.float32

_NT = (((1,), (1,)), ((), ()))


def _tile(n, pref):
    t = min(n, pref)
    assert n % t == 0, (n, pref)
    return t


def _params(sem, vmem_mib):
    return pltpu.CompilerParams(dimension_semantics=sem, vmem_limit_bytes=vmem_mib * MIB)


def _rms(x, gain):
    return x * lax.rsqrt(jnp.mean(x * x, axis=-1, keepdims=True) + NORM_EPS) * gain


def _rope_tables(pos, inv2, sgn):
    ang = pos * inv2
    return jnp.cos(ang), jnp.sin(ang) * sgn


def _rope(x, c, s):
    return x * c + pltpu.roll(x, HEAD_DIM // 2, 1) * s


def _proj_kernel(x_ref, g_ref, w_ref, wg_ref, o_ref, og_ref, hn_ref):
    @pl.when(pl.program_id(1) == 0)
    def _():
        hn_ref[...] = _rms(x_ref[...], g_ref[...]).astype(BF16)
        og_ref[...] = jnp.dot(hn_ref[...], wg_ref[...], preferred_element_type=F32)

    o_ref[...] = jnp.dot(hn_ref[...], w_ref[...], preferred_element_type=F32).astype(BF16)


def _in_proj(x, gain, w_main, w_gate):
    T, D = x.shape
    N = w_main.shape[1]
    tm = _tile(T, 1024)
    tn = 1536
    assert N % tn == 0
    return pl.pallas_call(
        _proj_kernel,
        grid=(T // tm, N // tn),
        in_specs=[
            pl.BlockSpec((tm, D), lambda i, j: (i, 0)),
            pl.BlockSpec((1, D), lambda i, j: (0, 0)),
            pl.BlockSpec((D, tn), lambda i, j: (0, j)),
            pl.BlockSpec((D, LANE), lambda i, j: (0, 0)),
        ],
        out_specs=[pl.BlockSpec((tm, tn), lambda i, j: (i, j)), pl.BlockSpec((tm, LANE), lambda i, j: (i, 0))],
        out_shape=[jax.ShapeDtypeStruct((T, N), BF16), jax.ShapeDtypeStruct((T, LANE), F32)],
        scratch_shapes=[pltpu.VMEM((tm, D), BF16)],
        compiler_params=_params(("parallel", "arbitrary"), 52),
        name="in_proj",
    )(x, gain, w_main, w_gate)


def _prep_kernel(pos_ref, inv_ref, sgn_ref, gq_ref, gks_ref, gkw_ref, gmq_ref,
                 q_ref, kc_ref, vc_ref, ks_ref, vs_ref, kw_ref, vw_ref,
                 sq_ref, sk_ref, sv_ref, mq_ref, gn_ref,
                 qn_o, kc_o, vc_o, ks_o, kw_o, vsT_o, vwT_o, sq_o, sk_o, svT_o, mq_o, gT_o):
    c, s = _rope_tables(pos_ref[...], inv_ref[...], sgn_ref[...])
    hd = HEAD_DIM
    tp = pos_ref.shape[0]
    f32 = lambda ref, sl: ref[:, sl].astype(F32)
    for h in range(NSA_HEADS):
        sl = slice(h * hd, (h + 1) * hd)
        qn_o[:, sl] = (_rope(_rms(f32(q_ref, sl), gq_ref[...]), c, s) * QSCALE).astype(BF16)
    ones_rows = (lax.broadcasted_iota(jnp.int32, (VAUG - hd, tp), 0) == 0).astype(F32).astype(BF16)
    for g in range(NSA_GROUPS):
        sl = slice(g * hd, (g + 1) * hd)
        kc_o[g] = kc_ref[:, sl]
        vc_o[g] = vc_ref[:, sl]
        ks_o[:, sl] = _rope(_rms(f32(ks_ref, sl), gks_ref[...]), c, s).astype(BF16)
        kw_o[:, sl] = _rope(_rms(f32(kw_ref, sl), gkw_ref[...]), c, s).astype(BF16)
        vsT_o[g * VAUG:g * VAUG + hd, :] = f32(vs_ref, sl).T.astype(BF16)
        vsT_o[g * VAUG + hd:(g + 1) * VAUG, :] = ones_rows
        vwT_o[g * VAUG:g * VAUG + hd, :] = f32(vw_ref, sl).T.astype(BF16)
        vwT_o[g * VAUG + hd:(g + 1) * VAUG, :] = ones_rows
    for h in range(SB_HEADS):
        sl = slice(h * hd, (h + 1) * hd)
        sq_o[:, sl] = (f32(sq_ref, sl) * QSCALE).astype(BF16)
        svT_o[sl, :] = f32(sv_ref, sl).T.astype(BF16)
    sk_o[...] = sk_ref[...]
    for h in range(MEM_HEADS):
        sl = slice(h * hd, (h + 1) * hd)
        mq_o[:, sl] = (_rms(f32(mq_ref, sl), gmq_ref[...]) * QSCALE).astype(BF16)
    gT_o[...] = jax.nn.sigmoid(gn_ref[...]).T


def _prep(P, Pg, pos_col, inv2, sgn, gq, gks, gkw, gmq):
    T = P.shape[0]
    tp = _tile(T, 512)
    hd = HEAD_DIM
    row = lambda w, c: pl.BlockSpec((tp, w), lambda i, c=c: (i, c))
    const = lambda: pl.BlockSpec((1, hd), lambda i: (0, 0))
    in_specs = [pl.BlockSpec((tp, 1), lambda i: (i, 0)), const(), const(), const(), const(), const(), const(),
                row(8 * hd, 0),
                row(2 * hd, 4), row(2 * hd, 5),
                row(2 * hd, 6), row(2 * hd, 7),
                row(2 * hd, 8), row(2 * hd, 9),
                row(4 * hd, 5), row(4 * hd, 6), row(4 * hd, 7),
                row(4 * hd, 8),
                row(hd, 0)]
    colT = lambda w: pl.BlockSpec((w, tp), lambda i: (0, i))
    grp = lambda w: pl.BlockSpec((NSA_GROUPS, tp, w), lambda i: (0, i, 0))
    out_specs = [row(8 * hd, 0), grp(hd), grp(hd), row(2 * hd, 0),
                 row(2 * hd, 0), colT(NSA_GROUPS * VAUG), colT(NSA_GROUPS * VAUG),
                 row(4 * hd, 0), row(4 * hd, 0), colT(4 * hd), row(4 * hd, 0), colT(hd)]
    sds = jax.ShapeDtypeStruct
    out_shape = [sds((T, 8 * hd), BF16),
                 sds((NSA_GROUPS, T, hd), BF16), sds((NSA_GROUPS, T, hd), BF16), sds((T, 2 * hd), BF16),
                 sds((T, 2 * hd), BF16),
                 sds((NSA_GROUPS * VAUG, T), BF16), sds((NSA_GROUPS * VAUG, T), BF16),
                 sds((T, 4 * hd), BF16), sds((T, 4 * hd), BF16), sds((4 * hd, T), BF16), sds((T, 4 * hd), BF16),
                 sds((hd, T), F32)]
    return pl.pallas_call(
        _prep_kernel, grid=(T // tp,), in_specs=in_specs, out_specs=out_specs, out_shape=out_shape,
        compiler_params=_params(("parallel",), 48), name="prep",
    )(pos_col, inv2, sgn, gq, gks, gkw, gmq, *([P] * 11), Pg)


def _gelu_tanh(x):
    return 0.5 * x * (1.0 + jnp.tanh(0.7978845608028654 * (x + 0.044715 * (x * x * x))))


def _compress_one(x, w1, pe, w2):
    n = x.shape[0]
    ab = jnp.dot(x, w1, preferred_element_type=F32)
    pr = jnp.dot(pe, w1, preferred_element_type=F32)
    pec = pr[0:1, :CMP_HIDDEN] + pr[1:2, CMP_HIDDEN:]
    hid = ab[:, :CMP_HIDDEN] + pltpu.roll(ab[:, CMP_HIDDEN:], n - 1, 0) + pec
    return jnp.dot(_gelu_tanh(hid).astype(BF16), w2, preferred_element_type=F32)


def _compress_kernel(xk_ref, xv_ref, w1k_ref, w1v_ref, pek_ref, pev_ref, w2k_ref, w2v_ref,
                     gk_ref, pos_ref, inv_ref, sgn_ref, kc_o, vcT_o):
    c, s = _rope_tables(pos_ref[...], inv_ref[...], sgn_ref[...])
    k = _compress_one(xk_ref[0], w1k_ref[...], pek_ref[...], w2k_ref[...])
    kc_o[0] = _rope(_rms(k, gk_ref[...]), c, s).astype(BF16)
    v = _compress_one(xv_ref[0], w1v_ref[...], pev_ref[...], w2v_ref[...])
    vcT_o[0] = v.T.astype(BF16)


def _compress(xk, xv, w1k, w1v, pek, pev, w2k, w2v, gk, posc, inv2, sgn):
    G, n, W = xk.shape
    full = lambda a: pl.BlockSpec(a.shape, lambda g: (0,) * a.ndim)
    grp = pl.BlockSpec((1, n, W), lambda g: (g, 0, 0))
    return pl.pallas_call(
        _compress_kernel, grid=(G,),
        in_specs=[grp, grp, full(w1k), full(w1v), full(pek), full(pev), full(w2k), full(w2v),
                  full(gk), full(posc), full(inv2), full(sgn)],
        out_specs=[pl.BlockSpec((1, n, HEAD_DIM), lambda g: (g, 0, 0)),
                   pl.BlockSpec((1, HEAD_DIM, n), lambda g: (g, 0, 0))],
        out_shape=[jax.ShapeDtypeStruct((G, n, HEAD_DIM), BF16), jax.ShapeDtypeStruct((G, HEAD_DIM, n), BF16)],
        compiler_params=_params(("parallel",), 48), name="compress",
    )(xk, xv, w1k, w1v, pek, pev, w2k, w2v, gk, posc, inv2, sgn)


def _memkv_kernel(mem_ref, gm_ref, w_ref, gk_ref, k_o, vT_o):
    hn = _rms(mem_ref[...], gm_ref[...]).astype(BF16)
    kv = jnp.dot(hn, w_ref[...], preferred_element_type=F32)
    mw = MEM_HEADS * HEAD_DIM
    for h in range(MEM_HEADS):
        sl = slice(h * HEAD_DIM, (h + 1) * HEAD_DIM)
        k_o[:, sl] = _rms(kv[:, sl], gk_ref[...]).astype(BF16)
        vT_o[sl, :] = kv[:, mw + h * HEAD_DIM: mw + (h + 1) * HEAD_DIM].T.astype(BF16)


def _memkv(mem, gm, w, gk):
    M = mem.shape[0]
    mw = MEM_HEADS * HEAD_DIM
    return pl.pallas_call(
        _memkv_kernel,
        out_shape=[jax.ShapeDtypeStruct((M, mw), BF16), jax.ShapeDtypeStruct((mw, M), BF16)],
        compiler_params=pltpu.CompilerParams(vmem_limit_bytes=32 * MIB), name="mem_kv",
    )(mem, gm, w, gk)


def _cmp_kernel(q_ref, kc_ref, vcT_ref, ovT_ref, g_ref, y_ref, b_ref, *, tq, n_pad, n_sel, top_k):
    i = pl.program_id(1)
    t = i * tq + lax.broadcasted_iota(jnp.int32, (1, tq), 1)
    n_end = lax.broadcasted_iota(jnp.int32, (n_pad, 1), 0) * CMP_STRIDE + (CMP_LEN - 1)
    valid = n_end <= t
    kc = kc_ref[0]
    vcT = vcT_ref[0]
    has_valid = (t >= CMP_LEN - 1).astype(F32)
    sms = []
    for h in range(NSA_HPG):
        q_h = q_ref[:, h * HEAD_DIM:(h + 1) * HEAD_DIM]
        sms.append(jnp.where(valid, lax.dot_general(kc, q_h, _NT, preferred_element_type=F32), NEG_BIG))
    psum = jnp.zeros((n_pad, tq), F32)
    for h in range(NSA_HPG):
        e = jnp.exp2(sms[h] - jnp.max(sms[h], axis=0, keepdims=True))
        p = e * (has_valid / jnp.sum(e, axis=0, keepdims=True))
        oT = jnp.dot(vcT, p.astype(BF16), preferred_element_type=F32)
        y_ref[:, h * HEAD_DIM:(h + 1) * HEAD_DIM] = (oT * g_ref[0, 0, h:h + 1, :]).T.astype(BF16)
        psum = psum + p
    hi = psum.astype(BF16)
    r1 = psum - hi.astype(F32)
    mid = r1.astype(BF16)
    lo = (r1 - mid.astype(F32)).astype(BF16)
    parts = jnp.dot(ovT_ref[...], jnp.concatenate([hi, mid, lo], axis=1), preferred_element_type=F32)
    imp = parts[:, :tq] + parts[:, tq:2 * tq] + parts[:, 2 * tq:]
    s_i = lax.broadcasted_iota(jnp.int32, (n_sel, 1), 0)
    cur = lax.shift_right_logical(t, 6)
    forced = (s_i == 0) | (s_i == cur) | (s_i == cur - 1)
    future = s_i * SEL_BLOCK > t
    w = jnp.where(forced, jnp.inf, jnp.where(future, -jnp.inf, imp))
    s_f = jnp.broadcast_to(s_i.astype(F32), (n_sel, tq))
    for _ in range(top_k):
        m = jnp.max(w, axis=0, keepdims=True)
        idx = jnp.min(jnp.where(w == m, s_f, float(n_sel)), axis=0, keepdims=True)
        w = jnp.where(s_f == idx, -jnp.inf, w)
    b_ref[0] = jnp.where(future, NEG_BIG, jnp.where(w == -jnp.inf, 0.0, NEG_BIG))


def _cmp_select(qn, kc, vcT, ovT, gates):
    T = qn.shape[0]
    G, n_pad, _ = kc.shape
    n_sel = T // SEL_BLOCK
    tq = _tile(T, 256)
    gw = NSA_HPG * HEAD_DIM
    kern = functools.partial(_cmp_kernel, tq=tq, n_pad=n_pad, n_sel=n_sel, top_k=min(SEL_TOPK, n_sel))
    return pl.pallas_call(
        kern, grid=(G, T // tq),
        in_specs=[pl.BlockSpec((tq, gw), lambda g, i: (i, g)),
                  pl.BlockSpec((1, n_pad, HEAD_DIM), lambda g, i: (g, 0, 0)),
                  pl.BlockSpec((1, HEAD_DIM, n_pad), lambda g, i: (g, 0, 0)),
                  pl.BlockSpec((n_sel, n_pad), lambda g, i: (0, 0)),
                  pl.BlockSpec((1, 1, 8, tq), lambda g, i: (0, g, 0, i))],
        out_specs=[pl.BlockSpec((tq, gw), lambda g, i: (i, g)),
                   pl.BlockSpec((1, n_sel, tq), lambda g, i: (g, 0, i))],
        out_shape=[jax.ShapeDtypeStruct((T, NSA_HEADS * HEAD_DIM), BF16),
                   jax.ShapeDtypeStruct((G, n_sel, T), F32)],
        compiler_params=_params(("parallel", "parallel"), 48), name="cmp_select",
    )(qn, kc, vcT, ovT, gates)


def _flash_kernel(qi_ref, kj_ref, fl_ref, q_ref, k_ref, vT_ref, g_ref, *rest, mode, tq, tk):
    if mode == "sel":
        bias_ref, o_ref, m_sc, acc_sc = rest
    else:
        o_ref, m_sc, acc_sc = rest
    p = pl.program_id(1)
    i = qi_ref[p]
    j = kj_ref[p]
    fl = fl_ref[p]

    @pl.when((fl & 1) != 0)
    def _():
        m_sc[...] = jnp.full(m_sc.shape, NEG_BIG, F32)
        acc_sc[...] = jnp.zeros(acc_sc.shape, F32)

    t = i * tq + lax.broadcasted_iota(jnp.int32, (1, tq), 1)
    kpos = j * tk + lax.broadcasted_iota(jnp.int32, (tk, 1), 0)
    if mode == "sel":
        nb = tk // SEL_BLOCK
        base = jnp.concatenate(
            [jnp.broadcast_to(bias_ref[0, b:b + 1, :], (SEL_BLOCK, tq)) for b in range(nb)], axis=0)
        mask_add = jnp.where(kpos <= t, base, NEG_BIG)
    else:
        mask_add = jnp.where((kpos <= t) & (kpos > t - WINDOW), 0.0, NEG_BIG)
    k = k_ref[...]
    vT = vT_ref[...]
    sTs = []
    for h in range(NSA_HPG):
        q_h = q_ref[:, h * HEAD_DIM:(h + 1) * HEAD_DIM]
        sTs.append(lax.dot_general(k, q_h, _NT, preferred_element_type=F32) + mask_add)
    pTs, alphas = [], []
    for h in range(NSA_HPG):
        sT = sTs[h]
        m_old = m_sc[h]
        m_new = jnp.maximum(m_old, jnp.max(sT, axis=0, keepdims=True))
        alphas.append(jnp.exp2(m_old - m_new))
        pTs.append(jnp.exp2((sT - m_new).astype(BF16)))
        m_sc[h] = m_new
    for h in range(NSA_HPG):
        acc_sc[h] = alphas[h] * acc_sc[h] + jnp.dot(vT, pTs[h], preferred_element_type=F32)

    @pl.when((fl & 2) != 0)
    def _():
        for h in range(NSA_HPG):
            acc = acc_sc[h]
            l = acc[HEAD_DIM:HEAD_DIM + 1, :]
            o = acc[:HEAD_DIM, :] * ((1.0 / l) * g_ref[0, 0, h:h + 1, :])
            o_ref[:, h * HEAD_DIM:(h + 1) * HEAD_DIM] = o.T.astype(BF16)


def _steps(nq, lo_fn, hi_fn, reverse=False):
    qi, kj, fl = [], [], []
    for i in range(nq):
        js = list(range(lo_fn(i), hi_fn(i) + 1))
        if reverse:
            js = js[::-1]
        for n, j in enumerate(js):
            qi.append(i)
            kj.append(j)
            fl.append((1 if n == 0 else 0) | (2 if n == len(js) - 1 else 0))
    return (jnp.asarray(np.array(qi, np.int32)), jnp.asarray(np.array(kj, np.int32)),
            jnp.asarray(np.array(fl, np.int32)))


def _nsa_flash(mode, qn, k, vT, gates, bias=None):
    T = qn.shape[0]
    tq = _tile(T, 1024)
    tk = _tile(T, 512)
    nq = T // tq
    gw = NSA_HPG * HEAD_DIM
    hi = lambda i: ((i + 1) * tq - 1) // tk
    if mode == "sel":
        lo = lambda i: 0
        br = 1
    else:
        lo = lambda i: max(0, (i * tq - (WINDOW - 1)) // tk)
        br = 2
    qi, kj, fl = _steps(nq, lo, hi)
    in_specs = [pl.BlockSpec((tq, gw), lambda g, p, qi, kj, fl: (qi[p], g)),
                pl.BlockSpec((tk, HEAD_DIM), lambda g, p, qi, kj, fl: (kj[p], g)),
                pl.BlockSpec((VAUG, tk), lambda g, p, qi, kj, fl: (g, kj[p])),
                pl.BlockSpec((1, 1, 8, tq), lambda g, p, qi, kj, fl, br=br: (br, g, 0, qi[p]))]
    args = [qn, k, vT, gates]
    if mode == "sel":
        in_specs.append(pl.BlockSpec((1, tk // SEL_BLOCK, tq), lambda g, p, qi, kj, fl: (g, kj[p], qi[p])))
        args.append(bias)
    kern = functools.partial(_flash_kernel, mode=mode, tq=tq, tk=tk)
    return pl.pallas_call(
        kern,
        grid_spec=pltpu.PrefetchScalarGridSpec(
            num_scalar_prefetch=3, grid=(NSA_GROUPS, int(qi.shape[0])),
            in_specs=in_specs,
            out_specs=pl.BlockSpec((tq, gw), lambda g, p, qi, kj, fl: (qi[p], g)),
            scratch_shapes=[pltpu.VMEM((NSA_HPG, 1, tq), F32), pltpu.VMEM((NSA_HPG, VAUG, tq), F32)]),
        out_shape=jax.ShapeDtypeStruct((T, NSA_HEADS * HEAD_DIM), BF16),
        compiler_params=_params(("parallel", "arbitrary"), 48), name="nsa_" + mode,
    )(qi, kj, fl, *args)


def _sb_kernel(qi_ref, kj_ref, fl_ref, q_ref, k_ref, vT_ref, lm_ref, *rest, tq, tk, resume):
    if resume:
        acc_in, carry_in, o_ref, carry_sc, acc_sc = rest
    else:
        o_ref, acc_o, carry_o, carry_sc, acc_sc = rest
    p = pl.program_id(0)
    i = qi_ref[p]
    j = kj_ref[p]
    fl = fl_ref[p]

    @pl.when((fl & 1) != 0)
    def _():
        if resume:
            for h in range(SB_HEADS):
                carry_sc[h] = carry_in[h:h + 1, :]
                acc_sc[h] = acc_in[h * HEAD_DIM:(h + 1) * HEAD_DIM, :]
        else:
            carry_sc[...] = jnp.zeros(carry_sc.shape, F32)
            acc_sc[...] = jnp.zeros(acc_sc.shape, F32)

    @pl.when((jnp.max(carry_sc[...]) > SB_DEAD_LOG2) & ((fl & 4) == 0))
    def _():
        t = i * tq + lax.broadcasted_iota(jnp.int32, (1, tq), 1)
        kpos = j * tk + lax.broadcasted_iota(jnp.int32, (tk, 1), 0)
        past = kpos < t
        lm = lm_ref[...]
        hs = [slice(h * HEAD_DIM, (h + 1) * HEAD_DIM) for h in range(SB_HEADS)]
        zs = [lax.dot_general(k_ref[:, sl], q_ref[:, sl], _NT, preferred_element_type=F32) for sl in hs]
        lgs, css = [], []
        for h in range(SB_HEADS):
            z = zs[h]
            sp = jnp.maximum(z, 0.0) + jnp.log2(1.0 + jnp.exp2(-jnp.abs(z)))
            lk = jnp.where(past, -sp, 0.0)
            hi = lk.astype(BF16)
            lo = (lk - hi.astype(F32)).astype(BF16)
            css.append(jnp.dot(lm, jnp.concatenate([hi, lo], axis=1), preferred_element_type=F32))
            lgs.append(z - sp)
            carry_old = carry_sc[h]
            carry_sc[h] = carry_old + jnp.sum(lk, axis=0, keepdims=True)
            css[h] = css[h][:, :tq] + css[h][:, tq:] + carry_old
        for h in range(SB_HEADS):
            wgt = jnp.where(past, jnp.exp2(lgs[h] + css[h]), 0.0)
            acc_sc[h] = acc_sc[h] + jnp.dot(vT_ref[hs[h], :], wgt.astype(BF16), preferred_element_type=F32)

    @pl.when((fl & 2) != 0)
    def _():
        for h in range(SB_HEADS):
            o_ref[:, h * HEAD_DIM:(h + 1) * HEAD_DIM] = acc_sc[h].T.astype(BF16)
            if not resume:
                acc_o[h * HEAD_DIM:(h + 1) * HEAD_DIM, :] = acc_sc[h]
                carry_o[h:h + 1, :] = carry_sc[h]
        if not resume:
            carry_o[SB_HEADS:, :] = jnp.zeros((8 - SB_HEADS, tq), F32)


def _sb_call(q, k, vT, steps, tq, tk, state=None):
    T, W = q.shape
    qi, kj, fl = steps
    lmat = jnp.asarray(np.triu(np.ones((tk, tk), np.float32), 1), BF16)
    resume = state is not None
    qtile = lambda shape: pl.BlockSpec(shape, lambda p, qi, kj, fl: (qi[p], 0))
    qtileT = lambda rows: pl.BlockSpec((rows, tq), lambda p, qi, kj, fl: (0, qi[p]))
    in_specs = [qtile((tq, W)),
                pl.BlockSpec((tk, W), lambda p, qi, kj, fl: (kj[p], 0)),
                pl.BlockSpec((W, tk), lambda p, qi, kj, fl: (0, kj[p])),
                pl.BlockSpec((tk, tk), lambda p, qi, kj, fl: (0, 0))]
    y_sds = jax.ShapeDtypeStruct((T, W), BF16)
    if resume:
        in_specs += [qtileT(W), qtileT(8)]
        out_specs, out_shape = qtile((tq, W)), y_sds
    else:
        out_specs = [qtile((tq, W)), qtileT(W), qtileT(8)]
        out_shape = [y_sds, jax.ShapeDtypeStruct((W, T), F32), jax.ShapeDtypeStruct((8, T), F32)]
    return pl.pallas_call(
        functools.partial(_sb_kernel, tq=tq, tk=tk, resume=resume),
        grid_spec=pltpu.PrefetchScalarGridSpec(
            num_scalar_prefetch=3, grid=(int(qi.shape[0]),),
            in_specs=in_specs, out_specs=out_specs,
            scratch_shapes=[pltpu.VMEM((SB_HEADS, 1, tq), F32), pltpu.VMEM((SB_HEADS, HEAD_DIM, tq), F32)]),
        out_shape=out_shape,
        compiler_params=_params(("arbitrary",), 48), name="sb_far" if resume else "sb_near",
    )(qi, kj, fl, q, k, vT, lmat, *(state or ()))


def _stick_breaking(q, k, vT):
    T, W = q.shape
    tq = _tile(T, 512)
    tk = _tile(T, 256)
    nq = T // tq
    hi = lambda i: ((i + 1) * tq - 2) // tk
    lo_near = lambda i: max(0, hi(i) - SB_NEAR_TILES + 1)
    near = _steps(nq, lo_near, hi, reverse=True)
    y_near, acc, carry = _sb_call(q, k, vT, near, tq, tk)
    qi, kj, fl = [], [], []
    for i in range(nq):
        js = list(range(lo_near(i) - 1, -1, -1))
        for n, j in enumerate(js or [0]):
            qi.append(i)
            kj.append(j)
            fl.append((1 if n == 0 else 0) | (2 if n == max(len(js), 1) - 1 else 0) | (0 if js else 4))
    far = tuple(jnp.asarray(np.array(a, np.int32)) for a in (qi, kj, fl))
    alive = jnp.max(carry[:SB_HEADS]) > SB_DEAD_LOG2
    return lax.cond(alive, lambda: _sb_call(q, k, vT, far, tq, tk, state=(acc, carry)), lambda: y_near)


def _memattn_kernel(q_ref, k_ref, vT_ref, o_ref):
    for h in range(MEM_HEADS):
        sl = slice(h * HEAD_DIM, (h + 1) * HEAD_DIM)
        sT = lax.dot_general(k_ref[:, sl], q_ref[:, sl], _NT, preferred_element_type=F32)
        e = jnp.exp2(sT - jnp.max(sT, axis=0, keepdims=True))
        l = jnp.sum(e, axis=0, keepdims=True)
        oT = jnp.dot(vT_ref[sl, :], e.astype(BF16), preferred_element_type=F32) * (1.0 / l)
        o_ref[:, sl] = oT.T.astype(BF16)


def _mem_attention(q, k, vT):
    T, W = q.shape
    M = k.shape[0]
    tq = _tile(T, 512)
    return pl.pallas_call(
        _memattn_kernel, grid=(T // tq,),
        in_specs=[pl.BlockSpec((tq, W), lambda i: (i, 0)),
                  pl.BlockSpec((M, W), lambda i: (0, 0)),
                  pl.BlockSpec((W, M), lambda i: (0, 0))],
        out_specs=pl.BlockSpec((tq, W), lambda i: (i, 0)),
        out_shape=jax.ShapeDtypeStruct((T, W), BF16),
        compiler_params=_params(("parallel",), 32), name="mem_attention",
    )(q, k, vT)


def _merge_kernel(yc_ref, ys_ref, yw_ref, ysb_ref, ym_ref, g0_ref, g1_ref, g2_ref,
                  wn_ref, wsb_ref, wm_ref, o_ref, yn_sc):
    @pl.when(pl.program_id(1) == 0)
    def _():
        yn_sc[...] = (yc_ref[...].astype(F32) + ys_ref[...].astype(F32) + yw_ref[...].astype(F32)).astype(BF16)

    a = jnp.dot(yn_sc[...], wn_ref[...], preferred_element_type=F32)
    b = jnp.dot(ysb_ref[...], wsb_ref[...], preferred_element_type=F32)
    c = jnp.dot(ym_ref[...], wm_ref[...], preferred_element_type=F32)
    sig = lambda ref: jax.nn.sigmoid(ref[...].astype(F32))
    o = sig(g0_ref) * a + sig(g1_ref) * b + sig(g2_ref) * c
    o_ref[...] = o.astype(BF16)


def _merge(yc, ys, yw, ysb, ym, P, wn, wsb, wm, gate_col0):
    T = yc.shape[0]
    D = wn.shape[1]
    tm = _tile(T, 512)
    tn = 512
    assert D % tn == 0 and gate_col0 % tn == 0
    gb = gate_col0 // tn
    nb = D // tn
    rowi = lambda w: pl.BlockSpec((tm, w), lambda i, j: (i, 0))
    gate = lambda b: pl.BlockSpec((tm, tn), lambda i, j, b=b: (i, gb + b * nb + j))
    wcol = lambda k: pl.BlockSpec((k, tn), lambda i, j: (0, j))
    return pl.pallas_call(
        _merge_kernel, grid=(T // tm, nb),
        in_specs=[rowi(yc.shape[1]), rowi(ys.shape[1]), rowi(yw.shape[1]), rowi(ysb.shape[1]), rowi(ym.shape[1]),
                  gate(0), gate(1), gate(2), wcol(wn.shape[0]), wcol(wsb.shape[0]), wcol(wm.shape[0])],
        out_specs=pl.BlockSpec((tm, tn), lambda i, j: (i, j)),
        out_shape=jax.ShapeDtypeStruct((T, D), BF16),
        scratch_shapes=[pltpu.VMEM((tm, yc.shape[1]), BF16)],
        compiler_params=_params(("parallel", "arbitrary"), 48), name="merge",
    )(yc, ys, yw, ysb, ym, P, P, P, wn, wsb, wm)


def _out_kernel(mix_ref, w_ref, x_ref, g_ref, x2_o, h2_o):
    x2 = x_ref[...] + jnp.dot(mix_ref[...], w_ref[...], preferred_element_type=F32)
    x2_o[...] = x2
    h2_o[...] = _rms(x2, g_ref[...]).astype(BF16)


def _out_proj(mixed, w, x, gain):
    T, D = x.shape
    tm = _tile(T, 256)
    return pl.pallas_call(
        _out_kernel, grid=(T // tm,),
        in_specs=[pl.BlockSpec((tm, D), lambda i: (i, 0)),
                  pl.BlockSpec((D, D), lambda i: (0, 0)),
                  pl.BlockSpec((tm, D), lambda i: (i, 0)),
                  pl.BlockSpec((1, D), lambda i: (0, 0))],
        out_specs=[pl.BlockSpec((tm, D), lambda i: (i, 0)), pl.BlockSpec((tm, D), lambda i: (i, 0))],
        out_shape=[jax.ShapeDtypeStruct((T, D), F32), jax.ShapeDtypeStruct((T, D), BF16)],
        compiler_params=_params(("parallel",), 48), name="out_proj",
    )(mixed, w, x, gain)


def _ffn_kernel(h_ref, wg_ref, wu_ref, wd_ref, x_ref, o_ref, z_sc, *, nf):
    f = pl.program_id(1)

    def up():
        a = jnp.dot(h_ref[...], wg_ref[...], preferred_element_type=F32)
        b = jnp.dot(h_ref[...], wu_ref[...], preferred_element_type=F32)
        return (a * jax.nn.sigmoid(a) * b).astype(BF16)

    def down():
        return jnp.dot(z_sc[...], wd_ref[...], preferred_element_type=F32)

    @pl.when(f == 0)
    def _():
        z_sc[...] = up()

    @pl.when((f > 0) & (f < nf))
    def _():
        c = down()
        z_new = up()
        o_ref[...] = jnp.where(f == 1, x_ref[...], o_ref[...]) + c
        z_sc[...] = z_new

    @pl.when(f == nf)
    def _():
        o_ref[...] += down()


def _ffn(h2, wg, wu, wd, x2):
    T, D = x2.shape
    F = wg.shape[1]
    tm = _tile(T, 512)
    tf = 512
    assert F % tf == 0
    nf = F // tf
    assert nf >= 2
    return pl.pallas_call(
        functools.partial(_ffn_kernel, nf=nf), grid=(T // tm, nf + 1),
        in_specs=[pl.BlockSpec((tm, D), lambda i, f: (i, 0)),
                  pl.BlockSpec((D, tf), lambda i, f: (0, jnp.minimum(f, nf - 1))),
                  pl.BlockSpec((D, tf), lambda i, f: (0, jnp.minimum(f, nf - 1))),
                  pl.BlockSpec((tf, D), lambda i, f: (jnp.maximum(f - 1, 0), 0)),
                  pl.BlockSpec((tm, D), lambda i, f: (i, 0))],
        out_specs=pl.BlockSpec((tm, D), lambda i, f: (i, 0)),
        out_shape=jax.ShapeDtypeStruct((T, D), F32),
        scratch_shapes=[pltpu.VMEM((tm, tf), BF16)],
        compiler_params=_params(("parallel", "arbitrary"), 48), name="ffn",
    )(h2, wg, wu, wd, x2)


def _layer(x, mem, pos_col, posc_col, consts, attn_norm, w_in, nsa_q_norm, nsa_kc_norm, nsa_ks_norm, nsa_kw_norm,
           cmp_k_pe, cmp_k_w1, cmp_k_w2, cmp_v_pe, cmp_v_w1, cmp_v_w2, mem_norm, w_mem_kv,
           mem_q_norm, mem_k_norm, w_o_nsa, w_o_sb, w_o_mem, w_out, ffn_norm,
           w_ffn_gate, w_ffn_up, w_ffn_down):
    T, D = x.shape
    inv2, sgn, ovT = consts
    hd = HEAD_DIM
    row = lambda g: g.reshape(1, -1)

    q_w, kv_w, gn_w = NSA_HEADS * hd, 6 * NSA_GROUPS * hd, 3 * NSA_HEADS
    sb_w, mq_w, gm_w = 3 * SB_HEADS * hd, MEM_HEADS * hd, N_BRANCH * D
    o_gn = q_w + kv_w
    o_sb = o_gn + gn_w
    o_mq = o_sb + sb_w
    o_gm = o_mq + mq_w
    assert w_in.shape[1] == o_gm + gm_w
    w_main = jnp.concatenate([w_in[:, :o_gn].astype(BF16), w_in[:, o_sb:].astype(BF16)], axis=1)
    w_gate = jnp.pad(w_in[:, o_gn:o_sb], ((0, 0), (0, LANE - gn_w))).astype(BF16)
    gate_col0 = q_w + kv_w + sb_w + mq_w
    P, Pg = _in_proj(x, row(attn_norm), w_main, w_gate)

    (qn, kc_raw, vc_raw, ksn, kwn, vsT, vwT, sbq, sbk, sbvT, memq, gT) = _prep(
        P, Pg, pos_col, inv2, sgn, row(nsa_q_norm), row(nsa_ks_norm), row(nsa_kw_norm), row(mem_q_norm))

    gates = gT[:gn_w].reshape(NSA_GROUPS, NSA_HPG, 3, T).transpose(2, 0, 1, 3)
    gates = jnp.pad(gates, ((0, 0), (0, 0), (0, 8 - NSA_HPG), (0, 0)))

    n_pad = T // CMP_STRIDE
    half = CMP_LEN // 2

    def w1_pack(w1):
        return jnp.concatenate([w1[:half].reshape(half * hd, -1), w1[half:].reshape(half * hd, -1)], axis=1).astype(BF16)

    def pe_pack(pe):
        return jnp.pad(pe.reshape(2, half * hd), ((0, 6), (0, 0))).astype(BF16)

    kc, vcT = _compress(
        kc_raw.reshape(NSA_GROUPS, n_pad, CMP_STRIDE * hd), vc_raw.reshape(NSA_GROUPS, n_pad, CMP_STRIDE * hd),
        w1_pack(cmp_k_w1), w1_pack(cmp_v_w1), pe_pack(cmp_k_pe), pe_pack(cmp_v_pe),
        cmp_k_w2.astype(BF16), cmp_v_w2.astype(BF16), row(nsa_kc_norm), posc_col, inv2, sgn)

    y_cmp, bias = _cmp_select(qn, kc, vcT, ovT, gates)
    y_sel = _nsa_flash("sel", qn, ksn, vsT, gates, bias)
    y_win = _nsa_flash("win", qn, kwn, vwT, gates)
    y_sb = _stick_breaking(sbq, sbk, sbvT)

    mk, mvT = _memkv(mem, row(mem_norm), w_mem_kv.astype(BF16), row(mem_k_norm))
    y_mem = _mem_attention(memq, mk, mvT)

    mixed = _merge(y_cmp, y_sel, y_win, y_sb, y_mem, P,
                   w_o_nsa.astype(BF16), w_o_sb.astype(BF16), w_o_mem.astype(BF16), gate_col0)
    x2, h2 = _out_proj(mixed, w_out.astype(BF16), x, row(ffn_norm))
    return _ffn(h2, w_ffn_gate.astype(BF16), w_ffn_up.astype(BF16), w_ffn_down.astype(BF16), x2)


def kernel(x, mem, positions, attn_norm, w_in, nsa_q_norm, nsa_kc_norm, nsa_ks_norm, nsa_kw_norm, cmp_k_pe, cmp_k_w1, cmp_k_w2, cmp_v_pe, cmp_v_w1, cmp_v_w2, mem_norm, w_mem_kv, mem_q_norm, mem_k_norm, w_o_nsa, w_o_sb, w_o_mem, w_out, ffn_norm, w_ffn_gate, w_ffn_up, w_ffn_down):
    B, T, D = x.shape
    assert T % (4 * LANE) == 0 and T // SEL_BLOCK >= 8
    n_pad = T // CMP_STRIDE
    n_sel = T // SEL_BLOCK
    inv = 1.0 / (ROPE_THETA ** (jnp.arange(0, HEAD_DIM, 2, dtype=F32) / HEAD_DIM))
    inv2 = jnp.concatenate([inv, inv]).reshape(1, HEAD_DIM)
    sgn = jnp.concatenate([-jnp.ones((HEAD_DIM // 2,), F32), jnp.ones((HEAD_DIM // 2,), F32)]).reshape(1, HEAD_DIM)
    cs = np.arange(n_pad)[None, :] * CMP_STRIDE
    ss = np.arange(n_sel)[:, None] * SEL_BLOCK
    ovT = jnp.asarray(((cs < ss + SEL_BLOCK) & (cs + CMP_LEN - 1 >= ss)).astype(np.float32), BF16)
    consts = (inv2, sgn, ovT)
    depth = w_in.shape[0]
    outs = []
    for b in range(B):
        xb = x[b]
        posf = positions[b].astype(F32)
        pos_col = posf.reshape(T, 1)
        posc = jnp.concatenate([posf[CMP_LEN - 1::CMP_STRIDE], posf[-1:]]).reshape(n_pad, 1)
        for l in range(depth):
            xb = _layer(xb, mem[b], pos_col, posc, consts, attn_norm[l], w_in[l], nsa_q_norm[l], nsa_kc_norm[l],
                        nsa_ks_norm[l], nsa_kw_norm[l], cmp_k_pe[l], cmp_k_w1[l], cmp_k_w2[l], cmp_v_pe[l],
                        cmp_v_w1[l], cmp_v_w2[l], mem_norm[l], w_mem_kv[l], mem_q_norm[l], mem_k_norm[l],
                        w_o_nsa[l], w_o_sb[l], w_o_mem[l], w_out[l], ffn_norm[l],
                        w_ffn_gate[l], w_ffn_up[l], w_ffn_down[l])
        outs.append(xb)
    return outs[0][None] if B == 1 else jnp.stack(outs, axis=0)
```

```python
import functools

import numpy as np
import jax
import jax.numpy as jnp
from jax import lax
from jax.experimental import pallas as pl
from jax.experimental.pallas import tpu as pltpu

HEAD_DIM = 128
NSA_HEADS = 8
NSA_GROUPS = 2
NSA_HPG = NSA_HEADS // NSA_GROUPS
SB_HEADS = 4
MEM_HEADS = 4
CMP_LEN = 32
CMP_STRIDE = 16
CMP_HIDDEN = 2 * HEAD_DIM
SEL_BLOCK = 64
SEL_TOPK = 16
WINDOW = 512
ROPE_THETA = 10000.0
NORM_EPS = 1e-6
NEG_BIG = -1e30
N_BRANCH = 3
SCALE = HEAD_DIM ** -0.5
LOG2E = 1.4426950408889634
QSCALE = SCALE * LOG2E
SB_NEAR_TILES = 3
SB_DEAD_LOG2 = -160.0
VAUG = HEAD_DIM + 16

LANE = 128
MIB = 1 << 20
BF16 = jnp.bfloat16
F32 = jnp.float32

_NT = (((1,), (1,)), ((), ()))


def _tile(n, pref):
    t = min(n, pref)
    assert n % t == 0, (n, pref)
    return t


def _params(sem, vmem_mib):
    return pltpu.CompilerParams(dimension_semantics=sem, vmem_limit_bytes=vmem_mib * MIB)


def _rms(x, gain):
    return x * lax.rsqrt(jnp.mean(x * x, axis=-1, keepdims=True) + NORM_EPS) * gain


def _rope_tables(pos, inv2, sgn):
    ang = pos * inv2
    return jnp.cos(ang), jnp.sin(ang) * sgn


def _rope(x, c, s):
    return x * c + pltpu.roll(x, HEAD_DIM // 2, 1) * s


def _proj_kernel(x_ref, g_ref, w_ref, wg_ref, o_ref, og_ref, hn_ref):
    @pl.when(pl.program_id(1) == 0)
    def _():
        hn_ref[...] = _rms(x_ref[...], g_ref[...]).astype(BF16)
        og_ref[...] = jnp.dot(hn_ref[...], wg_ref[...], preferred_element_type=F32)

    o_ref[...] = jnp.dot(hn_ref[...], w_ref[...], preferred_element_type=F32).astype(BF16)


def _in_proj(x, gain, w_main, w_gate):
    T, D = x.shape
    N = w_main.shape[1]
    tm = _tile(T, 1024)
    tn = 1536
    assert N % tn == 0
    return pl.pallas_call(
        _proj_kernel,
        grid=(T // tm, N // tn),
        in_specs=[
            pl.BlockSpec((tm, D), lambda i, j: (i, 0)),
            pl.BlockSpec((1, D), lambda i, j: (0, 0)),
            pl.BlockSpec((D, tn), lambda i, j: (0, j)),
            pl.BlockSpec((D, LANE), lambda i, j: (0, 0)),
        ],
        out_specs=[pl.BlockSpec((tm, tn), lambda i, j: (i, j)), pl.BlockSpec((tm, LANE), lambda i, j: (i, 0))],
        out_shape=[jax.ShapeDtypeStruct((T, N), BF16), jax.ShapeDtypeStruct((T, LANE), F32)],
        scratch_shapes=[pltpu.VMEM((tm, D), BF16)],
        compiler_params=_params(("parallel", "arbitrary"), 52),
        name="in_proj",
    )(x, gain, w_main, w_gate)


def _prep_kernel(pos_ref, inv_ref, sgn_ref, gq_ref, gks_ref, gkw_ref, gmq_ref,
                 q_ref, kc_ref, vc_ref, ks_ref, vs_ref, kw_ref, vw_ref,
                 sq_ref, sk_ref, sv_ref, mq_ref, gn_ref,
                 qn_o, kc_o, vc_o, ks_o, kw_o, vsT_o, vwT_o, sq_o, sk_o, svT_o, mq_o, gT_o):
    c, s = _rope_tables(pos_ref[...], inv_ref[...], sgn_ref[...])
    hd = HEAD_DIM
    tp = pos_ref.shape[0]
    f32 = lambda ref, sl: ref[:, sl].astype(F32)
    for h in range(NSA_HEADS):
        sl = slice(h * hd, (h + 1) * hd)
        qn_o[:, sl] = (_rope(_rms(f32(q_ref, sl), gq_ref[...]), c, s) * QSCALE).astype(BF16)
    ones_rows = (lax.broadcasted_iota(jnp.int32, (VAUG - hd, tp), 0) == 0).astype(F32).astype(BF16)
    for g in range(NSA_GROUPS):
        sl = slice(g * hd, (g + 1) * hd)
        kc_o[g] = kc_ref[:, sl]
        vc_o[g] = vc_ref[:, sl]
        ks_o[:, sl] = _rope(_rms(f32(ks_ref, sl), gks_ref[...]), c, s).astype(BF16)
        kw_o[:, sl] = _rope(_rms(f32(kw_ref, sl), gkw_ref[...]), c, s).astype(BF16)
        vsT_o[g * VAUG:g * VAUG + hd, :] = f32(vs_ref, sl).T.astype(BF16)
        vsT_o[g * VAUG + hd:(g + 1) * VAUG, :] = ones_rows
        vwT_o[g * VAUG:g * VAUG + hd, :] = f32(vw_ref, sl).T.astype(BF16)
        vwT_o[g * VAUG + hd:(g + 1) * VAUG, :] = ones_rows
    for h in range(SB_HEADS):
        sl = slice(h * hd, (h + 1) * hd)
        sq_o[:, sl] = (f32(sq_ref, sl) * QSCALE).astype(BF16)
        svT_o[sl, :] = f32(sv_ref, sl).T.astype(BF16)
    sk_o[...] = sk_ref[...]
    for h in range(MEM_HEADS):
        sl = slice(h * hd, (h + 1) * hd)
        mq_o[:, sl] = (_rms(f32(mq_ref, sl), gmq_ref[...]) * QSCALE).astype(BF16)
    gT_o[...] = jax.nn.sigmoid(gn_ref[...]).T


def _prep(P, Pg, pos_col, inv2, sgn, gq, gks, gkw, gmq):
    T = P.shape[0]
    tp = _tile(T, 512)
    hd = HEAD_DIM
    row = lambda w, c: pl.BlockSpec((tp, w), lambda i, c=c: (i, c))
    const = lambda: pl.BlockSpec((1, hd), lambda i: (0, 0))
    in_specs = [pl.BlockSpec((tp, 1), lambda i: (i, 0)), const(), const(), const(), const(), const(), const(),
                row(8 * hd, 0),
                row(2 * hd, 4), row(2 * hd, 5),
                row(2 * hd, 6), row(2 * hd, 7),
                row(2 * hd, 8), row(2 * hd, 9),
                row(4 * hd, 5), row(4 * hd, 6), row(4 * hd, 7),
                row(4 * hd, 8),
                row(hd, 0)]
    colT = lambda w: pl.BlockSpec((w, tp), lambda i: (0, i))
    grp = lambda w: pl.BlockSpec((NSA_GROUPS, tp, w), lambda i: (0, i, 0))
    out_specs = [row(8 * hd, 0), grp(hd), grp(hd), row(2 * hd, 0),
                 row(2 * hd, 0), colT(NSA_GROUPS * VAUG), colT(NSA_GROUPS * VAUG),
                 row(4 * hd, 0), row(4 * hd, 0), colT(4 * hd), row(4 * hd, 0), colT(hd)]
    sds = jax.ShapeDtypeStruct
    out_shape = [sds((T, 8 * hd), BF16),
                 sds((NSA_GROUPS, T, hd), BF16), sds((NSA_GROUPS, T, hd), BF16), sds((T, 2 * hd), BF16),
                 sds((T, 2 * hd), BF16),
                 sds((NSA_GROUPS * VAUG, T), BF16), sds((NSA_GROUPS * VAUG, T), BF16),
                 sds((T, 4 * hd), BF16), sds((T, 4 * hd), BF16), sds((4 * hd, T), BF16), sds((T, 4 * hd), BF16),
                 sds((hd, T), F32)]
    return pl.pallas_call(
        _prep_kernel, grid=(T // tp,), in_specs=in_specs, out_specs=out_specs, out_shape=out_shape,
        compiler_params=_params(("parallel",), 48), name="prep",
    )(pos_col, inv2, sgn, gq, gks, gkw, gmq, *([P] * 11), Pg)


def _gelu_tanh(x):
    return 0.5 * x * (1.0 + jnp.tanh(0.7978845608028654 * (x + 0.044715 * (x * x * x))))


def _compress_one(x, w1, pe, w2):
    n = x.shape[0]
    ab = jnp.dot(x, w1, preferred_element_type=F32)
    pr = jnp.dot(pe, w1, preferred_element_type=F32)
    pec = pr[0:1, :CMP_HIDDEN] + pr[1:2, CMP_HIDDEN:]
    hid = ab[:, :CMP_HIDDEN] + pltpu.roll(ab[:, CMP_HIDDEN:], n - 1, 0) + pec
    return jnp.dot(_gelu_tanh(hid).astype(BF16), w2, preferred_element_type=F32)


def _compress_kernel(xk_ref, xv_ref, w1k_ref, w1v_ref, pek_ref, pev_ref, w2k_ref, w2v_ref,
                     gk_ref, pos_ref, inv_ref, sgn_ref, kc_o, vcT_o):
    c, s = _rope_tables(pos_ref[...], inv_ref[...], sgn_ref[...])
    k = _compress_one(xk_ref[0], w1k_ref[...], pek_ref[...], w2k_ref[...])
    kc_o[0] = _rope(_rms(k, gk_ref[...]), c, s).astype(BF16)
    v = _compress_one(xv_ref[0], w1v_ref[...], pev_ref[...], w2v_ref[...])
    vcT_o[0] = v.T.astype(BF16)


def _compress(xk, xv, w1k, w1v, pek, pev, w2k, w2v, gk, posc, inv2, sgn):
    G, n, W = xk.shape
    full = lambda a: pl.BlockSpec(a.shape, lambda g: (0,) * a.ndim)
    grp = pl.BlockSpec((1, n, W), lambda g: (g, 0, 0))
    return pl.pallas_call(
        _compress_kernel, grid=(G,),
        in_specs=[grp, grp, full(w1k), full(w1v), full(pek), full(pev), full(w2k), full(w2v),
                  full(gk), full(posc), full(inv2), full(sgn)],
        out_specs=[pl.BlockSpec((1, n, HEAD_DIM), lambda g: (g, 0, 0)),
                   pl.BlockSpec((1, HEAD_DIM, n), lambda g: (g, 0, 0))],
        out_shape=[jax.ShapeDtypeStruct((G, n, HEAD_DIM), BF16), jax.ShapeDtypeStruct((G, HEAD_DIM, n), BF16)],
        compiler_params=_params(("parallel",), 48), name="compress",
    )(xk, xv, w1k, w1v, pek, pev, w2k, w2v, gk, posc, inv2, sgn)


def _memkv_kernel(mem_ref, gm_ref, w_ref, gk_ref, k_o, vT_o):
    hn = _rms(mem_ref[...], gm_ref[...]).astype(BF16)
    kv = jnp.dot(hn, w_ref[...], preferred_element_type=F32)
    mw = MEM_HEADS * HEAD_DIM
    for h in range(MEM_HEADS):
        sl = slice(h * HEAD_DIM, (h + 1) * HEAD_DIM)
        k_o[:, sl] = _rms(kv[:, sl], gk_ref[...]).astype(BF16)
        vT_o[sl, :] = kv[:, mw + h * HEAD_DIM: mw + (h + 1) * HEAD_DIM].T.astype(BF16)


def _memkv(mem, gm, w, gk):
    M = mem.shape[0]
    mw = MEM_HEADS * HEAD_DIM
    return pl.pallas_call(
        _memkv_kernel,
        out_shape=[jax.ShapeDtypeStruct((M, mw), BF16), jax.ShapeDtypeStruct((mw, M), BF16)],
        compiler_params=pltpu.CompilerParams(vmem_limit_bytes=32 * MIB), name="mem_kv",
    )(mem, gm, w, gk)


def _cmp_kernel(q_ref, kc_ref, vcT_ref, ovT_ref, g_ref, y_ref, b_ref, *, tq, n_pad, n_sel, top_k):
    i = pl.program_id(1)
    t = i * tq + lax.broadcasted_iota(jnp.int32, (1, tq), 1)
    n_end = lax.broadcasted_iota(jnp.int32, (n_pad, 1), 0) * CMP_STRIDE + (CMP_LEN - 1)
    valid = n_end <= t
    kc = kc_ref[0]
    vcT = vcT_ref[0]
    has_valid = (t >= CMP_LEN - 1).astype(F32)
    sms = []
    for h in range(NSA_HPG):
        q_h = q_ref[:, h * HEAD_DIM:(h + 1) * HEAD_DIM]
        sms.append(jnp.where(valid, lax.dot_general(kc, q_h, _NT, preferred_element_type=F32), NEG_BIG))
    psum = jnp.zeros((n_pad, tq), F32)
    for h in range(NSA_HPG):
        e = jnp.exp2(sms[h] - jnp.max(sms[h], axis=0, keepdims=True))
        p = e * (has_valid / jnp.sum(e, axis=0, keepdims=True))
        oT = jnp.dot(vcT, p.astype(BF16), preferred_element_type=F32)
        y_ref[:, h * HEAD_DIM:(h + 1) * HEAD_DIM] = (oT * g_ref[0, 0, h:h + 1, :]).T.astype(BF16)
        psum = psum + p
    hi = psum.astype(BF16)
    r1 = psum - hi.astype(F32)
    mid = r1.astype(BF16)
    lo = (r1 - mid.astype(F32)).astype(BF16)
    parts = jnp.dot(ovT_ref[...], jnp.concatenate([hi, mid, lo], axis=1), preferred_element_type=F32)
    imp = parts[:, :tq] + parts[:, tq:2 * tq] + parts[:, 2 * tq:]
    s_i = lax.broadcasted_iota(jnp.int32, (n_sel, 1), 0)
    cur = lax.shift_right_logical(t, 6)
    forced = (s_i == 0) | (s_i == cur) | (s_i == cur - 1)
    future = s_i * SEL_BLOCK > t
    w = jnp.where(forced, jnp.inf, jnp.where(future, -jnp.inf, imp))
    s_f = jnp.broadcast_to(s_i.astype(F32), (n_sel, tq))
    for _ in range(top_k):
        m = jnp.max(w, axis=0, keepdims=True)
        idx = jnp.min(jnp.where(w == m, s_f, float(n_sel)), axis=0, keepdims=True)
        w = jnp.where(s_f == idx, -jnp.inf, w)
    b_ref[0] = jnp.where(future, NEG_BIG, jnp.where(w == -jnp.inf, 0.0, NEG_BIG))


def _cmp_select(qn, kc, vcT, ovT, gates):
    T = qn.shape[0]
    G, n_pad, _ = kc.shape
    n_sel = T // SEL_BLOCK
    tq = _tile(T, 256)
    gw = NSA_HPG * HEAD_DIM
    kern = functools.partial(_cmp_kernel, tq=tq, n_pad=n_pad, n_sel=n_sel, top_k=min(SEL_TOPK, n_sel))
    return pl.pallas_call(
        kern, grid=(G, T // tq),
        in_specs=[pl.BlockSpec((tq, gw), lambda g, i: (i, g)),
                  pl.BlockSpec((1, n_pad, HEAD_DIM), lambda g, i: (g, 0, 0)),
                  pl.BlockSpec((1, HEAD_DIM, n_pad), lambda g, i: (g, 0, 0)),
                  pl.BlockSpec((n_sel, n_pad), lambda g, i: (0, 0)),
                  pl.BlockSpec((1, 1, 8, tq), lambda g, i: (0, g, 0, i))],
        out_specs=[pl.BlockSpec((tq, gw), lambda g, i: (i, g)),
                   pl.BlockSpec((1, n_sel, tq), lambda g, i: (g, 0, i))],
        out_shape=[jax.ShapeDtypeStruct((T, NSA_HEADS * HEAD_DIM), BF16),
                   jax.ShapeDtypeStruct((G, n_sel, T), F32)],
        compiler_params=_params(("parallel", "parallel"), 48), name="cmp_select",
    )(qn, kc, vcT, ovT, gates)


def _flash_kernel(qi_ref, kj_ref, fl_ref, q_ref, k_ref, vT_ref, g_ref, *rest, mode, tq, tk):
    if mode == "sel":
        bias_ref, o_ref, m_sc, acc_sc = rest
    else:
        o_ref, m_sc, acc_sc = rest
    p = pl.program_id(1)
    i = qi_ref[p]
    j = kj_ref[p]
    fl = fl_ref[p]

    @pl.when((fl & 1) != 0)
    def _():
        m_sc[...] = jnp.full(m_sc.shape, NEG_BIG, F32)
        acc_sc[...] = jnp.zeros(acc_sc.shape, F32)

    t = i * tq + lax.broadcasted_iota(jnp.int32, (1, tq), 1)
    kpos = j * tk + lax.broadcasted_iota(jnp.int32, (tk, 1), 0)
    if mode == "sel":
        nb = tk // SEL_BLOCK
        base = jnp.concatenate(
            [jnp.broadcast_to(bias_ref[0, b:b + 1, :], (SEL_BLOCK, tq)) for b in range(nb)], axis=0)
        mask_add = jnp.where(kpos <= t, base, NEG_BIG)
    else:
        mask_add = jnp.where((kpos <= t) & (kpos > t - WINDOW), 0.0, NEG_BIG)
    k = k_ref[...]
    vT = vT_ref[...]
    sTs = []
    for h in range(NSA_HPG):
        q_h = q_ref[:, h * HEAD_DIM:(h + 1) * HEAD_DIM]
        sTs.append(lax.dot_general(k, q_h, _NT, preferred_element_type=F32) + mask_add)
    pTs, alphas = [], []
    for h in range(NSA_HPG):
        sT = sTs[h]
        m_old = m_sc[h]
        m_new = jnp.maximum(m_old, jnp.max(sT, axis=0, keepdims=True))
        alphas.append(jnp.exp2(m_old - m_new))
        pTs.append(jnp.exp2((sT - m_new).astype(BF16)))
        m_sc[h] = m_new
    for h in range(NSA_HPG):
        acc_sc[h] = alphas[h] * acc_sc[h] + jnp.dot(vT, pTs[h], preferred_element_type=F32)

    @pl.when((fl & 2) != 0)
    def _():
        for h in range(NSA_HPG):
            acc = acc_sc[h]
            l = acc[HEAD_DIM:HEAD_DIM + 1, :]
            o = acc[:HEAD_DIM, :] * ((1.0 / l) * g_ref[0, 0, h:h + 1, :])
            o_ref[:, h * HEAD_DIM:(h + 1) * HEAD_DIM] = o.T.astype(BF16)


def _steps(nq, lo_fn, hi_fn, reverse=False):
    qi, kj, fl = [], [], []
    for i in range(nq):
        js = list(range(lo_fn(i), hi_fn(i) + 1))
        if reverse:
            js = js[::-1]
        for n, j in enumerate(js):
            qi.append(i)
            kj.append(j)
            fl.append((1 if n == 0 else 0) | (2 if n == len(js) - 1 else 0))
    return (jnp.asarray(np.array(qi, np.int32)), jnp.asarray(np.array(kj, np.int32)),
            jnp.asarray(np.array(fl, np.int32)))


def _nsa_flash(mode, qn, k, vT, gates, bias=None):
    T = qn.shape[0]
    tq = _tile(T, 1024)
    tk = _tile(T, 512)
    nq = T // tq
    gw = NSA_HPG * HEAD_DIM
    hi = lambda i: ((i + 1) * tq - 1) // tk
    if mode == "sel":
        lo = lambda i: 0
        br = 1
    else:
        lo = lambda i: max(0, (i * tq - (WINDOW - 1)) // tk)
        br = 2
    qi, kj, fl = _steps(nq, lo, hi)
    in_specs = [pl.BlockSpec((tq, gw), lambda g, p, qi, kj, fl: (qi[p], g)),
                pl.BlockSpec((tk, HEAD_DIM), lambda g, p, qi, kj, fl: (kj[p], g)),
                pl.BlockSpec((VAUG, tk), lambda g, p, qi, kj, fl: (g, kj[p])),
                pl.BlockSpec((1, 1, 8, tq), lambda g, p, qi, kj, fl, br=br: (br, g, 0, qi[p]))]
    args = [qn, k, vT, gates]
    if mode == "sel":
        in_specs.append(pl.BlockSpec((1, tk // SEL_BLOCK, tq), lambda g, p, qi, kj, fl: (g, kj[p], qi[p])))
        args.append(bias)
    kern = functools.partial(_flash_kernel, mode=mode, tq=tq, tk=tk)
    return pl.pallas_call(
        kern,
        grid_spec=pltpu.PrefetchScalarGridSpec(
            num_scalar_prefetch=3, grid=(NSA_GROUPS, int(qi.shape[0])),
            in_specs=in_specs,
            out_specs=pl.BlockSpec((tq, gw), lambda g, p, qi, kj, fl: (qi[p], g)),
            scratch_shapes=[pltpu.VMEM((NSA_HPG, 1, tq), F32), pltpu.VMEM((NSA_HPG, VAUG, tq), F32)]),
        out_shape=jax.ShapeDtypeStruct((T, NSA_HEADS * HEAD_DIM), BF16),
        compiler_params=_params(("parallel", "arbitrary"), 48), name="nsa_" + mode,
    )(qi, kj, fl, *args)


def _sb_kernel(qi_ref, kj_ref, fl_ref, q_ref, k_ref, vT_ref, lm_ref, *rest, tq, tk, resume):
    if resume:
        acc_in, carry_in, o_ref, carry_sc, acc_sc = rest
    else:
        o_ref, acc_o, carry_o, carry_sc, acc_sc = rest
    p = pl.program_id(0)
    i = qi_ref[p]
    j = kj_ref[p]
    fl = fl_ref[p]

    @pl.when((fl & 1) != 0)
    def _():
        if resume:
            for h in range(SB_HEADS):
                carry_sc[h] = carry_in[h:h + 1, :]
                acc_sc[h] = acc_in[h * HEAD_DIM:(h + 1) * HEAD_DIM, :]
        else:
            carry_sc[...] = jnp.zeros(carry_sc.shape, F32)
            acc_sc[...] = jnp.zeros(acc_sc.shape, F32)

    @pl.when((jnp.max(carry_sc[...]) > SB_DEAD_LOG2) & ((fl & 4) == 0))
    def _():
        t = i * tq + lax.broadcasted_iota(jnp.int32, (1, tq), 1)
        kpos = j * tk + lax.broadcasted_iota(jnp.int32, (tk, 1), 0)
        past = kpos < t
        lm = lm_ref[...]
        hs = [slice(h * HEAD_DIM, (h + 1) * HEAD_DIM) for h in range(SB_HEADS)]
        zs = [lax.dot_general(k_ref[:, sl], q_ref[:, sl], _NT, preferred_element_type=F32) for sl in hs]
        lgs, css = [], []
        for h in range(SB_HEADS):
            z = zs[h]
            sp = jnp.maximum(z, 0.0) + jnp.log2(1.0 + jnp.exp2(-jnp.abs(z)))
            lk = jnp.where(past, -sp, 0.0)
            hi = lk.astype(BF16)
            lo = (lk - hi.astype(F32)).astype(BF16)
            css.append(jnp.dot(lm, jnp.concatenate([hi, lo], axis=1), preferred_element_type=F32))
            lgs.append(z - sp)
            carry_old = carry_sc[h]
            carry_sc[h] = carry_old + jnp.sum(lk, axis=0, keepdims=True)
            css[h] = css[h][:, :tq] + css[h][:, tq:] + carry_old
        for h in range(SB_HEADS):
            wgt = jnp.where(past, jnp.exp2(lgs[h] + css[h]), 0.0)
            acc_sc[h] = acc_sc[h] + jnp.dot(vT_ref[hs[h], :], wgt.astype(BF16), preferred_element_type=F32)

    @pl.when((fl & 2) != 0)
    def _():
        for h in range(SB_HEADS):
            o_ref[:, h * HEAD_DIM:(h + 1) * HEAD_DIM] = acc_sc[h].T.astype(BF16)
            if not resume:
                acc_o[h * HEAD_DIM:(h + 1) * HEAD_DIM, :] = acc_sc[h]
                carry_o[h:h + 1, :] = carry_sc[h]
        if not resume:
            carry_o[SB_HEADS:, :] = jnp.zeros((8 - SB_HEADS, tq), F32)


def _sb_call(q, k, vT, steps, tq, tk, state=None):
    T, W = q.shape
    qi, kj, fl = steps
    lmat = jnp.asarray(np.triu(np.ones((tk, tk), np.float32), 1), BF16)
    resume = state is not None
    qtile = lambda shape: pl.BlockSpec(shape, lambda p, qi, kj, fl: (qi[p], 0))
    qtileT = lambda rows: pl.BlockSpec((rows, tq), lambda p, qi, kj, fl: (0, qi[p]))
    in_specs = [qtile((tq, W)),
                pl.BlockSpec((tk, W), lambda p, qi, kj, fl: (kj[p], 0)),
                pl.BlockSpec((W, tk), lambda p, qi, kj, fl: (0, kj[p])),
                pl.BlockSpec((tk, tk), lambda p, qi, kj, fl: (0, 0))]
    y_sds = jax.ShapeDtypeStruct((T, W), BF16)
    if resume:
        in_specs += [qtileT(W), qtileT(8)]
        out_specs, out_shape = qtile((tq, W)), y_sds
    else:
        out_specs = [qtile((tq, W)), qtileT(W), qtileT(8)]
        out_shape = [y_sds, jax.ShapeDtypeStruct((W, T), F32), jax.ShapeDtypeStruct((8, T), F32)]
    return pl.pallas_call(
        functools.partial(_sb_kernel, tq=tq, tk=tk, resume=resume),
        grid_spec=pltpu.PrefetchScalarGridSpec(
            num_scalar_prefetch=3, grid=(int(qi.shape[0]),),
            in_specs=in_specs, out_specs=out_specs,
            scratch_shapes=[pltpu.VMEM((SB_HEADS, 1, tq), F32), pltpu.VMEM((SB_HEADS, HEAD_DIM, tq), F32)]),
        out_shape=out_shape,
        compiler_params=_params(("arbitrary",), 48), name="sb_far" if resume else "sb_near",
    )(qi, kj, fl, q, k, vT, lmat, *(state or ()))


def _stick_breaking(q, k, vT):
    T, W = q.shape
    tq = _tile(T, 512)
    tk = _tile(T, 256)
    nq = T // tq
    hi = lambda i: ((i + 1) * tq - 2) // tk
    lo_near = lambda i: max(0, hi(i) - SB_NEAR_TILES + 1)
    near = _steps(nq, lo_near, hi, reverse=True)
    y_near, acc, carry = _sb_call(q, k, vT, near, tq, tk)
    qi, kj, fl = [], [], []
    for i in range(nq):
        js = list(range(lo_near(i) - 1, -1, -1))
        for n, j in enumerate(js or [0]):
            qi.append(i)
            kj.append(j)
            fl.append((1 if n == 0 else 0) | (2 if n == max(len(js), 1) - 1 else 0) | (0 if js else 4))
    far = tuple(jnp.asarray(np.array(a, np.int32)) for a in (qi, kj, fl))
    with_far = [i for i in range(nq) if lo_near(i) > 0]
    if not with_far:
        return y_near
    alive = jnp.max(carry[:SB_HEADS, with_far[0] * tq:]) > SB_DEAD_LOG2
    return lax.cond(alive, lambda: _sb_call(q, k, vT, far, tq, tk, state=(acc, carry)), lambda: y_near)


def _memattn_kernel(q_ref, k_ref, vT_ref, o_ref):
    for h in range(MEM_HEADS):
        sl = slice(h * HEAD_DIM, (h + 1) * HEAD_DIM)
        sT = lax.dot_general(k_ref[:, sl], q_ref[:, sl], _NT, preferred_element_type=F32)
        e = jnp.exp2(sT - jnp.max(sT, axis=0, keepdims=True))
        l = jnp.sum(e, axis=0, keepdims=True)
        oT = jnp.dot(vT_ref[sl, :], e.astype(BF16), preferred_element_type=F32) * (1.0 / l)
        o_ref[:, sl] = oT.T.astype(BF16)


def _mem_attention(q, k, vT):
    T, W = q.shape
    M = k.shape[0]
    tq = _tile(T, 512)
    return pl.pallas_call(
        _memattn_kernel, grid=(T // tq,),
        in_specs=[pl.BlockSpec((tq, W), lambda i: (i, 0)),
                  pl.BlockSpec((M, W), lambda i: (0, 0)),
                  pl.BlockSpec((W, M), lambda i: (0, 0))],
        out_specs=pl.BlockSpec((tq, W), lambda i: (i, 0)),
        out_shape=jax.ShapeDtypeStruct((T, W), BF16),
        compiler_params=_params(("parallel",), 32), name="mem_attention",
    )(q, k, vT)


def _merge_kernel(yc_ref, ys_ref, yw_ref, ysb_ref, ym_ref, g0_ref, g1_ref, g2_ref,
                  wn_ref, wsb_ref, wm_ref, o_ref, yn_sc):
    @pl.when(pl.program_id(1) == 0)
    def _():
        yn_sc[...] = (yc_ref[...].astype(F32) + ys_ref[...].astype(F32) + yw_ref[...].astype(F32)).astype(BF16)

    a = jnp.dot(yn_sc[...], wn_ref[...], preferred_element_type=F32)
    b = jnp.dot(ysb_ref[...], wsb_ref[...], preferred_element_type=F32)
    c = jnp.dot(ym_ref[...], wm_ref[...], preferred_element_type=F32)
    sig = lambda ref: jax.nn.sigmoid(ref[...].astype(F32))
    o = sig(g0_ref) * a + sig(g1_ref) * b + sig(g2_ref) * c
    o_ref[...] = o.astype(BF16)


def _merge(yc, ys, yw, ysb, ym, P, wn, wsb, wm, gate_col0):
    T = yc.shape[0]
    D = wn.shape[1]
    tm = _tile(T, 512)
    tn = 512
    assert D % tn == 0 and gate_col0 % tn == 0
    gb = gate_col0 // tn
    nb = D // tn
    rowi = lambda w: pl.BlockSpec((tm, w), lambda i, j: (i, 0))
    gate = lambda b: pl.BlockSpec((tm, tn), lambda i, j, b=b: (i, gb + b * nb + j))
    wcol = lambda k: pl.BlockSpec((k, tn), lambda i, j: (0, j))
    return pl.pallas_call(
        _merge_kernel, grid=(T // tm, nb),
        in_specs=[rowi(yc.shape[1]), rowi(ys.shape[1]), rowi(yw.shape[1]), rowi(ysb.shape[1]), rowi(ym.shape[1]),
                  gate(0), gate(1), gate(2), wcol(wn.shape[0]), wcol(wsb.shape[0]), wcol(wm.shape[0])],
        out_specs=pl.BlockSpec((tm, tn), lambda i, j: (i, j)),
        out_shape=jax.ShapeDtypeStruct((T, D), BF16),
        scratch_shapes=[pltpu.VMEM((tm, yc.shape[1]), BF16)],
        compiler_params=_params(("parallel", "arbitrary"), 48), name="merge",
    )(yc, ys, yw, ysb, ym, P, P, P, wn, wsb, wm)


def _out_kernel(mix_ref, w_ref, x_ref, g_ref, x2_o, h2_o):
    x2 = x_ref[...] + jnp.dot(mix_ref[...], w_ref[...], preferred_element_type=F32)
    x2_o[...] = x2
    h2_o[...] = _rms(x2, g_ref[...]).astype(BF16)


def _out_proj(mixed, w, x, gain):
    T, D = x.shape
    tm = _tile(T, 256)
    return pl.pallas_call(
        _out_kernel, grid=(T // tm,),
        in_specs=[pl.BlockSpec((tm, D), lambda i: (i, 0)),
                  pl.BlockSpec((D, D), lambda i: (0, 0)),
                  pl.BlockSpec((tm, D), lambda i: (i, 0)),
                  pl.BlockSpec((1, D), lambda i: (0, 0))],
        out_specs=[pl.BlockSpec((tm, D), lambda i: (i, 0)), pl.BlockSpec((tm, D), lambda i: (i, 0))],
        out_shape=[jax.ShapeDtypeStruct((T, D), F32), jax.ShapeDtypeStruct((T, D), BF16)],
        compiler_params=_params(("parallel",), 48), name="out_proj",
    )(mixed, w, x, gain)


def _ffn_kernel(h_ref, wgu_ref, wd_ref, x_ref, o_ref, z_sc, *, nf):
    f = pl.program_id(1)
    tf = z_sc.shape[1]

    def up():
        ab = jnp.dot(h_ref[...], wgu_ref[0], preferred_element_type=F32)
        a, b = ab[:, :tf], ab[:, tf:]
        return (a * jax.nn.sigmoid(a) * b).astype(BF16)

    def down():
        return jnp.dot(z_sc[...], wd_ref[...], preferred_element_type=F32)

    @pl.when(f == 0)
    def _():
        z_sc[...] = up()

    @pl.when((f > 0) & (f < nf))
    def _():
        c = down()
        z_new = up()
        o_ref[...] = jnp.where(f == 1, x_ref[...], o_ref[...]) + c
        z_sc[...] = z_new

    @pl.when(f == nf)
    def _():
        o_ref[...] += down()


def _ffn(h2, wg, wu, wd, x2):
    T, D = x2.shape
    F = wg.shape[1]
    tm = _tile(T, 512)
    tf = 512
    assert F % tf == 0
    nf = F // tf
    assert nf >= 2
    slab = lambda w: w.astype(BF16).reshape(D, nf, tf).transpose(1, 0, 2)
    wgu = jnp.concatenate([slab(wg), slab(wu)], axis=2)
    return pl.pallas_call(
        functools.partial(_ffn_kernel, nf=nf), grid=(T // tm, nf + 1),
        in_specs=[pl.BlockSpec((tm, D), lambda i, f: (i, 0)),
                  pl.BlockSpec((1, D, 2 * tf), lambda i, f: (jnp.minimum(f, nf - 1), 0, 0)),
                  pl.BlockSpec((tf, D), lambda i, f: (jnp.maximum(f - 1, 0), 0)),
                  pl.BlockSpec((tm, D), lambda i, f: (i, 0))],
        out_specs=pl.BlockSpec((tm, D), lambda i, f: (i, 0)),
        out_shape=jax.ShapeDtypeStruct((T, D), F32),
        scratch_shapes=[pltpu.VMEM((tm, tf), BF16)],
        compiler_params=_params(("parallel", "arbitrary"), 48), name="ffn",
    )(h2, wgu, wd.astype(BF16), x2)


def _layer(x, mem, pos_col, posc_col, consts, attn_norm, w_in, nsa_q_norm, nsa_kc_norm, nsa_ks_norm, nsa_kw_norm,
           cmp_k_pe, cmp_k_w1, cmp_k_w2, cmp_v_pe, cmp_v_w1, cmp_v_w2, mem_norm, w_mem_kv,
           mem_q_norm, mem_k_norm, w_o_nsa, w_o_sb, w_o_mem, w_out, ffn_norm,
           w_ffn_gate, w_ffn_up, w_ffn_down):
    T, D = x.shape
    inv2, sgn, ovT = consts
    hd = HEAD_DIM
    row = lambda g: g.reshape(1, -1)

    q_w, kv_w, gn_w = NSA_HEADS * hd, 6 * NSA_GROUPS * hd, 3 * NSA_HEADS
    sb_w, mq_w, gm_w = 3 * SB_HEADS * hd, MEM_HEADS * hd, N_BRANCH * D
    o_gn = q_w + kv_w
    o_sb = o_gn + gn_w
    o_mq = o_sb + sb_w
    o_gm = o_mq + mq_w
    assert w_in.shape[1] == o_gm + gm_w
    w_main = jnp.concatenate([w_in[:, :o_gn], w_in[:, o_sb:]], axis=1).astype(BF16)
    w_gate = jnp.pad(w_in[:, o_gn:o_sb], ((0, 0), (0, LANE - gn_w))).astype(BF16)
    gate_col0 = q_w + kv_w + sb_w + mq_w
    P, Pg = _in_proj(x, row(attn_norm), w_main, w_gate)

    (qn, kc_raw, vc_raw, ksn, kwn, vsT, vwT, sbq, sbk, sbvT, memq, gT) = _prep(
        P, Pg, pos_col, inv2, sgn, row(nsa_q_norm), row(nsa_ks_norm), row(nsa_kw_norm), row(mem_q_norm))

    gates = gT[:gn_w].reshape(NSA_GROUPS, NSA_HPG, 3, T).transpose(2, 0, 1, 3)
    gates = jnp.pad(gates, ((0, 0), (0, 0), (0, 8 - NSA_HPG), (0, 0)))

    n_pad = T // CMP_STRIDE
    half = CMP_LEN // 2

    def w1_pack(w1):
        return jnp.concatenate([w1[:half].reshape(half * hd, -1), w1[half:].reshape(half * hd, -1)], axis=1).astype(BF16)

    def pe_pack(pe):
        return jnp.pad(pe.reshape(2, half * hd), ((0, 6), (0, 0))).astype(BF16)

    kc, vcT = _compress(
        kc_raw.reshape(NSA_GROUPS, n_pad, CMP_STRIDE * hd), vc_raw.reshape(NSA_GROUPS, n_pad, CMP_STRIDE * hd),
        w1_pack(cmp_k_w1), w1_pack(cmp_v_w1), pe_pack(cmp_k_pe), pe_pack(cmp_v_pe),
        cmp_k_w2.astype(BF16), cmp_v_w2.astype(BF16), row(nsa_kc_norm), posc_col, inv2, sgn)

    y_cmp, bias = _cmp_select(qn, kc, vcT, ovT, gates)
    y_sel = _nsa_flash("sel", qn, ksn, vsT, gates, bias)
    y_win = _nsa_flash("win", qn, kwn, vwT, gates)
    y_sb = _stick_breaking(sbq, sbk, sbvT)

    mk, mvT = _memkv(mem, row(mem_norm), w_mem_kv.astype(BF16), row(mem_k_norm))
    y_mem = _mem_attention(memq, mk, mvT)

    mixed = _merge(y_cmp, y_sel, y_win, y_sb, y_mem, P,
                   w_o_nsa.astype(BF16), w_o_sb.astype(BF16), w_o_mem.astype(BF16), gate_col0)
    x2, h2 = _out_proj(mixed, w_out.astype(BF16), x, row(ffn_norm))
    return _ffn(h2, w_ffn_gate, w_ffn_up, w_ffn_down, x2)


def kernel(x, mem, positions, attn_norm, w_in, nsa_q_norm, nsa_kc_norm, nsa_ks_norm, nsa_kw_norm, cmp_k_pe, cmp_k_w1, cmp_k_w2, cmp_v_pe, cmp_v_w1, cmp_v_w2, mem_norm, w_mem_kv, mem_q_norm, mem_k_norm, w_o_nsa, w_o_sb, w_o_mem, w_out, ffn_norm, w_ffn_gate, w_ffn_up, w_ffn_down):
    B, T, D = x.shape
    assert T % (4 * LANE) == 0 and T // SEL_BLOCK >= 8
    n_pad = T // CMP_STRIDE
    n_sel = T // SEL_BLOCK
    inv = 1.0 / (ROPE_THETA ** (jnp.arange(0, HEAD_DIM, 2, dtype=F32) / HEAD_DIM))
    inv2 = jnp.concatenate([inv, inv]).reshape(1, HEAD_DIM)
    sgn = jnp.concatenate([-jnp.ones((HEAD_DIM // 2,), F32), jnp.ones((HEAD_DIM // 2,), F32)]).reshape(1, HEAD_DIM)
    cs = np.arange(n_pad)[None, :] * CMP_STRIDE
    ss = np.arange(n_sel)[:, None] * SEL_BLOCK
    ovT = jnp.asarray(((cs < ss + SEL_BLOCK) & (cs + CMP_LEN - 1 >= ss)).astype(np.float32), BF16)
    consts = (inv2, sgn, ovT)
    depth = w_in.shape[0]
    outs = []
    for b in range(B):
        xb = x[b]
        posf = positions[b].astype(F32)
        pos_col = posf.reshape(T, 1)
        posc = jnp.concatenate([posf[CMP_LEN - 1::CMP_STRIDE], posf[-1:]]).reshape(n_pad, 1)
        for l in range(depth):
            xb = _layer(xb, mem[b], pos_col, posc, consts, attn_norm[l], w_in[l], nsa_q_norm[l], nsa_kc_norm[l],
                        nsa_ks_norm[l], nsa_kw_norm[l], cmp_k_pe[l], cmp_k_w1[l], cmp_k_w2[l], cmp_v_pe[l],
                        cmp_v_w1[l], cmp_v_w2[l], mem_norm[l], w_mem_kv[l], mem_q_norm[l], mem_k_norm[l],
                        w_o_nsa[l], w_o_sb[l], w_o_mem[l], w_out[l], ffn_norm[l],
                        w_ffn_gate[l], w_ffn_up[l], w_ffn_down[l])
        outs.append(xb)
    return outs[0][None] if B == 1 else jnp.stack(outs, axis=0)
```

```python
import functools

import numpy as np
import jax
import jax.numpy as jnp
from jax import lax
from jax.experimental import pallas as pl
from jax.experimental.pallas import tpu as pltpu

HEAD_DIM = 128
NSA_HEADS = 8
NSA_GROUPS = 2
NSA_HPG = NSA_HEADS // NSA_GROUPS
SB_HEADS = 4
MEM_HEADS = 4
CMP_LEN = 32
CMP_STRIDE = 16
CMP_HIDDEN = 2 * HEAD_DIM
SEL_BLOCK = 64
SEL_TOPK = 16
WINDOW = 512
ROPE_THETA = 10000.0
NORM_EPS = 1e-6
NEG_BIG = -1e30
N_BRANCH = 3
SCALE = HEAD_DIM ** -0.5
LOG2E = 1.4426950408889634
QSCALE = SCALE * LOG2E
CMP_BUCKETS = 8
SB_NEAR_TILES = 3
SB_DEAD_LOG2 = -160.0
VAUG = HEAD_DIM + 16

LANE = 128
MIB = 1 << 20
BF16 = jnp.bfloat16
F32 = jnp.float32

_NT = (((1,), (1,)), ((), ()))


def _tile(n, pref):
    t = min(n, pref)
    assert n % t == 0, (n, pref)
    return t


def _params(sem, vmem_mib):
    return pltpu.CompilerParams(dimension_semantics=sem, vmem_limit_bytes=vmem_mib * MIB)


def _rms(x, gain):
    return x * lax.rsqrt(jnp.mean(x * x, axis=-1, keepdims=True) + NORM_EPS) * gain


def _rope_tables(pos, inv2, sgn):
    ang = pos * inv2
    return jnp.cos(ang), jnp.sin(ang) * sgn


def _rope(x, c, s):
    return x * c + pltpu.roll(x, HEAD_DIM // 2, 1) * s


def _proj_kernel(x_ref, g_ref, w_ref, wg_ref, o_ref, og_ref, hn_ref):
    @pl.when(pl.program_id(1) == 0)
    def _():
        hn_ref[...] = _rms(x_ref[...], g_ref[...]).astype(BF16)
        og_ref[...] = jnp.dot(hn_ref[...], wg_ref[...], preferred_element_type=F32)

    o_ref[...] = jnp.dot(hn_ref[...], w_ref[...], preferred_element_type=F32).astype(BF16)


def _in_proj(x, gain, w_main, w_gate):
    T, D = x.shape
    N = w_main.shape[1]
    tm = _tile(T, 1024)
    tn = 1536
    assert N % tn == 0
    return pl.pallas_call(
        _proj_kernel,
        grid=(T // tm, N // tn),
        in_specs=[
            pl.BlockSpec((tm, D), lambda i, j: (i, 0)),
            pl.BlockSpec((1, D), lambda i, j: (0, 0)),
            pl.BlockSpec((D, tn), lambda i, j: (0, j)),
            pl.BlockSpec((D, LANE), lambda i, j: (0, 0)),
        ],
        out_specs=[pl.BlockSpec((tm, tn), lambda i, j: (i, j)), pl.BlockSpec((tm, LANE), lambda i, j: (i, 0))],
        out_shape=[jax.ShapeDtypeStruct((T, N), BF16), jax.ShapeDtypeStruct((T, LANE), F32)],
        scratch_shapes=[pltpu.VMEM((tm, D), BF16)],
        compiler_params=_params(("parallel", "arbitrary"), 52),
        name="in_proj",
    )(x, gain, w_main, w_gate)


def _prep_kernel(pos_ref, inv_ref, sgn_ref, gq_ref, gks_ref, gkw_ref, gmq_ref,
                 q_ref, kc_ref, vc_ref, ks_ref, vs_ref, kw_ref, vw_ref,
                 sq_ref, sk_ref, sv_ref, mq_ref, gn_ref,
                 qn_o, kc_o, vc_o, ks_o, kw_o, vsT_o, vwT_o, sq_o, sk_o, svT_o, mq_o, gT_o):
    c, s = _rope_tables(pos_ref[...], inv_ref[...], sgn_ref[...])
    hd = HEAD_DIM
    tp = pos_ref.shape[0]
    f32 = lambda ref, sl: ref[:, sl].astype(F32)
    for h in range(NSA_HEADS):
        sl = slice(h * hd, (h + 1) * hd)
        qn_o[:, sl] = (_rope(_rms(f32(q_ref, sl), gq_ref[...]), c, s) * QSCALE).astype(BF16)
    ones_rows = (lax.broadcasted_iota(jnp.int32, (VAUG - hd, tp), 0) == 0).astype(F32).astype(BF16)
    for g in range(NSA_GROUPS):
        sl = slice(g * hd, (g + 1) * hd)
        kc_o[g] = kc_ref[:, sl]
        vc_o[g] = vc_ref[:, sl]
        ks_o[:, sl] = _rope(_rms(f32(ks_ref, sl), gks_ref[...]), c, s).astype(BF16)
        kw_o[:, sl] = _rope(_rms(f32(kw_ref, sl), gkw_ref[...]), c, s).astype(BF16)
        vsT_o[g * VAUG:g * VAUG + hd, :] = f32(vs_ref, sl).T.astype(BF16)
        vsT_o[g * VAUG + hd:(g + 1) * VAUG, :] = ones_rows
        vwT_o[g * VAUG:g * VAUG + hd, :] = f32(vw_ref, sl).T.astype(BF16)
        vwT_o[g * VAUG + hd:(g + 1) * VAUG, :] = ones_rows
    for h in range(SB_HEADS):
        sl = slice(h * hd, (h + 1) * hd)
        sq_o[:, sl] = (f32(sq_ref, sl) * QSCALE).astype(BF16)
        svT_o[sl, :] = f32(sv_ref, sl).T.astype(BF16)
    sk_o[...] = sk_ref[...]
    for h in range(MEM_HEADS):
        sl = slice(h * hd, (h + 1) * hd)
        mq_o[:, sl] = (_rms(f32(mq_ref, sl), gmq_ref[...]) * QSCALE).astype(BF16)
    gT_o[...] = jax.nn.sigmoid(gn_ref[...]).T


def _prep(P, Pg, pos_col, inv2, sgn, gq, gks, gkw, gmq):
    T = P.shape[0]
    tp = _tile(T, 512)
    hd = HEAD_DIM
    row = lambda w, c: pl.BlockSpec((tp, w), lambda i, c=c: (i, c))
    const = lambda: pl.BlockSpec((1, hd), lambda i: (0, 0))
    in_specs = [pl.BlockSpec((tp, 1), lambda i: (i, 0)), const(), const(), const(), const(), const(), const(),
                row(8 * hd, 0),
                row(2 * hd, 4), row(2 * hd, 5),
                row(2 * hd, 6), row(2 * hd, 7),
                row(2 * hd, 8), row(2 * hd, 9),
                row(4 * hd, 5), row(4 * hd, 6), row(4 * hd, 7),
                row(4 * hd, 8),
                row(hd, 0)]
    colT = lambda w: pl.BlockSpec((w, tp), lambda i: (0, i))
    grp = lambda w: pl.BlockSpec((NSA_GROUPS, tp, w), lambda i: (0, i, 0))
    out_specs = [row(8 * hd, 0), grp(hd), grp(hd), row(2 * hd, 0),
                 row(2 * hd, 0), colT(NSA_GROUPS * VAUG), colT(NSA_GROUPS * VAUG),
                 row(4 * hd, 0), row(4 * hd, 0), colT(4 * hd), row(4 * hd, 0), colT(hd)]
    sds = jax.ShapeDtypeStruct
    out_shape = [sds((T, 8 * hd), BF16),
                 sds((NSA_GROUPS, T, hd), BF16), sds((NSA_GROUPS, T, hd), BF16), sds((T, 2 * hd), BF16),
                 sds((T, 2 * hd), BF16),
                 sds((NSA_GROUPS * VAUG, T), BF16), sds((NSA_GROUPS * VAUG, T), BF16),
                 sds((T, 4 * hd), BF16), sds((T, 4 * hd), BF16), sds((4 * hd, T), BF16), sds((T, 4 * hd), BF16),
                 sds((hd, T), F32)]
    return pl.pallas_call(
        _prep_kernel, grid=(T // tp,), in_specs=in_specs, out_specs=out_specs, out_shape=out_shape,
        compiler_params=_params(("parallel",), 48), name="prep",
    )(pos_col, inv2, sgn, gq, gks, gkw, gmq, *([P] * 11), Pg)


def _gelu_tanh(x):
    return 0.5 * x * (1.0 + jnp.tanh(0.7978845608028654 * (x + 0.044715 * (x * x * x))))


def _compress_one(x, w1, pe, w2):
    n = x.shape[0]
    ab = jnp.dot(x, w1, preferred_element_type=F32)
    pr = jnp.dot(pe, w1, preferred_element_type=F32)
    pec = pr[0:1, :CMP_HIDDEN] + pr[1:2, CMP_HIDDEN:]
    hid = ab[:, :CMP_HIDDEN] + pltpu.roll(ab[:, CMP_HIDDEN:], n - 1, 0) + pec
    return jnp.dot(_gelu_tanh(hid).astype(BF16), w2, preferred_element_type=F32)


def _compress_kernel(xk_ref, xv_ref, w1k_ref, w1v_ref, pek_ref, pev_ref, w2k_ref, w2v_ref,
                     gk_ref, pos_ref, inv_ref, sgn_ref, kc_o, vcT_o):
    c, s = _rope_tables(pos_ref[...], inv_ref[...], sgn_ref[...])
    k = _compress_one(xk_ref[0], w1k_ref[...], pek_ref[...], w2k_ref[...])
    kc_o[0] = _rope(_rms(k, gk_ref[...]), c, s).astype(BF16)
    v = _compress_one(xv_ref[0], w1v_ref[...], pev_ref[...], w2v_ref[...])
    vcT_o[0] = v.T.astype(BF16)


def _compress(xk, xv, w1k, w1v, pek, pev, w2k, w2v, gk, posc, inv2, sgn):
    G, n, W = xk.shape
    full = lambda a: pl.BlockSpec(a.shape, lambda g: (0,) * a.ndim)
    grp = pl.BlockSpec((1, n, W), lambda g: (g, 0, 0))
    return pl.pallas_call(
        _compress_kernel, grid=(G,),
        in_specs=[grp, grp, full(w1k), full(w1v), full(pek), full(pev), full(w2k), full(w2v),
                  full(gk), full(posc), full(inv2), full(sgn)],
        out_specs=[pl.BlockSpec((1, n, HEAD_DIM), lambda g: (g, 0, 0)),
                   pl.BlockSpec((1, HEAD_DIM, n), lambda g: (g, 0, 0))],
        out_shape=[jax.ShapeDtypeStruct((G, n, HEAD_DIM), BF16), jax.ShapeDtypeStruct((G, HEAD_DIM, n), BF16)],
        compiler_params=_params(("parallel",), 48), name="compress",
    )(xk, xv, w1k, w1v, pek, pev, w2k, w2v, gk, posc, inv2, sgn)


def _memkv_kernel(mem_ref, gm_ref, w_ref, gk_ref, k_o, vT_o):
    hn = _rms(mem_ref[...], gm_ref[...]).astype(BF16)
    kv = jnp.dot(hn, w_ref[...], preferred_element_type=F32)
    mw = MEM_HEADS * HEAD_DIM
    for h in range(MEM_HEADS):
        sl = slice(h * HEAD_DIM, (h + 1) * HEAD_DIM)
        k_o[:, sl] = _rms(kv[:, sl], gk_ref[...]).astype(BF16)
        vT_o[sl, :] = kv[:, mw + h * HEAD_DIM: mw + (h + 1) * HEAD_DIM].T.astype(BF16)


def _memkv(mem, gm, w, gk):
    M = mem.shape[0]
    mw = MEM_HEADS * HEAD_DIM
    return pl.pallas_call(
        _memkv_kernel,
        out_shape=[jax.ShapeDtypeStruct((M, mw), BF16), jax.ShapeDtypeStruct((mw, M), BF16)],
        compiler_params=pltpu.CompilerParams(vmem_limit_bytes=32 * MIB), name="mem_kv",
    )(mem, gm, w, gk)


def _cmp_kernel(q_ref, kc_ref, vcT_ref, ovT_ref, g_ref, y_in, b_in, y_ref, b_ref, *, tq, i0, n_pad, n_sel, top_k):
    del y_in, b_in
    i = i0 + pl.program_id(1)
    t = i * tq + lax.broadcasted_iota(jnp.int32, (1, tq), 1)
    n_end = lax.broadcasted_iota(jnp.int32, (n_pad, 1), 0) * CMP_STRIDE + (CMP_LEN - 1)
    valid = n_end <= t
    kc = kc_ref[0]
    vcT = vcT_ref[0]
    has_valid = (t >= CMP_LEN - 1).astype(F32)
    sms = []
    for h in range(NSA_HPG):
        q_h = q_ref[:, h * HEAD_DIM:(h + 1) * HEAD_DIM]
        sms.append(jnp.where(valid, lax.dot_general(kc, q_h, _NT, preferred_element_type=F32), NEG_BIG))
    psum = jnp.zeros((n_pad, tq), F32)
    for h in range(NSA_HPG):
        e = jnp.exp2(sms[h] - jnp.max(sms[h], axis=0, keepdims=True))
        p = e * (has_valid / jnp.sum(e, axis=0, keepdims=True))
        oT = jnp.dot(vcT, p.astype(BF16), preferred_element_type=F32)
        y_ref[:, h * HEAD_DIM:(h + 1) * HEAD_DIM] = (oT * g_ref[0, 0, h:h + 1, :]).T.astype(BF16)
        psum = psum + p
    hi = psum.astype(BF16)
    r1 = psum - hi.astype(F32)
    mid = r1.astype(BF16)
    lo = (r1 - mid.astype(F32)).astype(BF16)
    parts = jnp.dot(ovT_ref[...], jnp.concatenate([hi, mid, lo], axis=1), preferred_element_type=F32)
    imp = parts[:, :tq] + parts[:, tq:2 * tq] + parts[:, 2 * tq:]
    s_i = lax.broadcasted_iota(jnp.int32, (n_sel, 1), 0)
    cur = lax.shift_right_logical(t, 6)
    forced = (s_i == 0) | (s_i == cur) | (s_i == cur - 1)
    future = s_i * SEL_BLOCK > t
    w = jnp.where(forced, jnp.inf, jnp.where(future, -jnp.inf, imp))
    s_f = jnp.broadcast_to(s_i.astype(F32), (n_sel, tq))
    for _ in range(top_k):
        m = jnp.max(w, axis=0, keepdims=True)
        idx = jnp.min(jnp.where(w == m, s_f, float(n_sel)), axis=0, keepdims=True)
        w = jnp.where(s_f == idx, -jnp.inf, w)
    b_ref[0] = jnp.where(future, NEG_BIG, jnp.where(w == -jnp.inf, 0.0, NEG_BIG))


def _round_up(n, m):
    return -(-n // m) * m


def _cmp_select(qn, kc, vcT, ovT, gates):
    T = qn.shape[0]
    G, n_pad, _ = kc.shape
    n_sel = T // SEL_BLOCK
    tq = _tile(T, 256)
    nq = T // tq
    gw = NSA_HPG * HEAD_DIM
    nb = min(CMP_BUCKETS, nq)
    assert nq % nb == 0
    per = nq // nb
    y = jnp.zeros((T, NSA_HEADS * HEAD_DIM), BF16)
    bias = jnp.full((G, n_sel, T), NEG_BIG, F32)
    for b in range(nb):
        i0 = b * per
        t_max = (i0 + per) * tq - 1
        n_len = min(n_pad, _round_up(max(t_max - (CMP_LEN - 1), 0) // CMP_STRIDE + 1, LANE))
        s_len = min(n_sel, _round_up(t_max // SEL_BLOCK + 1, 8))
        kern = functools.partial(_cmp_kernel, tq=tq, i0=i0, n_pad=n_len, n_sel=s_len, top_k=min(SEL_TOPK, n_sel))
        y, bias = pl.pallas_call(
            kern, grid=(G, per),
            in_specs=[pl.BlockSpec((tq, gw), lambda g, i, i0=i0: (i0 + i, g)),
                      pl.BlockSpec((1, n_len, HEAD_DIM), lambda g, i: (g, 0, 0)),
                      pl.BlockSpec((1, HEAD_DIM, n_len), lambda g, i: (g, 0, 0)),
                      pl.BlockSpec((s_len, n_len), lambda g, i: (0, 0)),
                      pl.BlockSpec((1, 1, 8, tq), lambda g, i, i0=i0: (0, g, 0, i0 + i)),
                      pl.BlockSpec(memory_space=pl.ANY), pl.BlockSpec(memory_space=pl.ANY)],
            out_specs=[pl.BlockSpec((tq, gw), lambda g, i, i0=i0: (i0 + i, g)),
                       pl.BlockSpec((1, s_len, tq), lambda g, i, i0=i0: (g, 0, i0 + i))],
            out_shape=[jax.ShapeDtypeStruct(y.shape, y.dtype), jax.ShapeDtypeStruct(bias.shape, bias.dtype)],
            input_output_aliases={5: 0, 6: 1},
            compiler_params=_params(("parallel", "parallel"), 48), name="cmp_select_%d" % b,
        )(qn, kc, vcT, ovT, gates, y, bias)
    return y, bias


def _flash_kernel(qi_ref, kj_ref, fl_ref, q_ref, k_ref, vT_ref, g_ref, *rest, mode, tq, tk):
    if mode == "sel":
        bias_ref, o_ref, m_sc, acc_sc = rest
    else:
        o_ref, m_sc, acc_sc = rest
    p = pl.program_id(1)
    i = qi_ref[p]
    j = kj_ref[p]
    fl = fl_ref[p]

    @pl.when((fl & 1) != 0)
    def _():
        m_sc[...] = jnp.full(m_sc.shape, NEG_BIG, F32)
        acc_sc[...] = jnp.zeros(acc_sc.shape, F32)

    t = i * tq + lax.broadcasted_iota(jnp.int32, (1, tq), 1)
    kpos = j * tk + lax.broadcasted_iota(jnp.int32, (tk, 1), 0)
    if mode == "sel":
        nb = tk // SEL_BLOCK
        base = jnp.concatenate(
            [jnp.broadcast_to(bias_ref[0, b:b + 1, :], (SEL_BLOCK, tq)) for b in range(nb)], axis=0)
        mask_add = jnp.where(kpos <= t, base, NEG_BIG)
    else:
        mask_add = jnp.where((kpos <= t) & (kpos > t - WINDOW), 0.0, NEG_BIG)
    k = k_ref[...]
    vT = vT_ref[...]
    sTs = []
    for h in range(NSA_HPG):
        q_h = q_ref[:, h * HEAD_DIM:(h + 1) * HEAD_DIM]
        sTs.append(lax.dot_general(k, q_h, _NT, preferred_element_type=F32) + mask_add)
    pTs, alphas = [], []
    for h in range(NSA_HPG):
        sT = sTs[h]
        m_old = m_sc[h]
        m_new = jnp.maximum(m_old, jnp.max(sT, axis=0, keepdims=True))
        alphas.append(jnp.exp2(m_old - m_new))
        pTs.append(jnp.exp2((sT - m_new).astype(BF16)))
        m_sc[h] = m_new
    for h in range(NSA_HPG):
        acc_sc[h] = alphas[h] * acc_sc[h] + jnp.dot(vT, pTs[h], preferred_element_type=F32)

    @pl.when((fl & 2) != 0)
    def _():
        for h in range(NSA_HPG):
            acc = acc_sc[h]
            l = acc[HEAD_DIM:HEAD_DIM + 1, :]
            o = acc[:HEAD_DIM, :] * ((1.0 / l) * g_ref[0, 0, h:h + 1, :])
            o_ref[:, h * HEAD_DIM:(h + 1) * HEAD_DIM] = o.T.astype(BF16)


def _steps(nq, lo_fn, hi_fn, reverse=False):
    qi, kj, fl = [], [], []
    for i in range(nq):
        js = list(range(lo_fn(i), hi_fn(i) + 1))
        if reverse:
            js = js[::-1]
        for n, j in enumerate(js):
            qi.append(i)
            kj.append(j)
            fl.append((1 if n == 0 else 0) | (2 if n == len(js) - 1 else 0))
    return (jnp.asarray(np.array(qi, np.int32)), jnp.asarray(np.array(kj, np.int32)),
            jnp.asarray(np.array(fl, np.int32)))


def _nsa_flash(mode, qn, k, vT, gates, bias=None):
    T = qn.shape[0]
    tq = _tile(T, 1024)
    tk = _tile(T, 512)
    nq = T // tq
    gw = NSA_HPG * HEAD_DIM
    hi = lambda i: ((i + 1) * tq - 1) // tk
    if mode == "sel":
        lo = lambda i: 0
        br = 1
    else:
        lo = lambda i: max(0, (i * tq - (WINDOW - 1)) // tk)
        br = 2
    qi, kj, fl = _steps(nq, lo, hi)
    in_specs = [pl.BlockSpec((tq, gw), lambda g, p, qi, kj, fl: (qi[p], g)),
                pl.BlockSpec((tk, HEAD_DIM), lambda g, p, qi, kj, fl: (kj[p], g)),
                pl.BlockSpec((VAUG, tk), lambda g, p, qi, kj, fl: (g, kj[p])),
                pl.BlockSpec((1, 1, 8, tq), lambda g, p, qi, kj, fl, br=br: (br, g, 0, qi[p]))]
    args = [qn, k, vT, gates]
    if mode == "sel":
        in_specs.append(pl.BlockSpec((1, tk // SEL_BLOCK, tq), lambda g, p, qi, kj, fl: (g, kj[p], qi[p])))
        args.append(bias)
    kern = functools.partial(_flash_kernel, mode=mode, tq=tq, tk=tk)
    return pl.pallas_call(
        kern,
        grid_spec=pltpu.PrefetchScalarGridSpec(
            num_scalar_prefetch=3, grid=(NSA_GROUPS, int(qi.shape[0])),
            in_specs=in_specs,
            out_specs=pl.BlockSpec((tq, gw), lambda g, p, qi, kj, fl: (qi[p], g)),
            scratch_shapes=[pltpu.VMEM((NSA_HPG, 1, tq), F32), pltpu.VMEM((NSA_HPG, VAUG, tq), F32)]),
        out_shape=jax.ShapeDtypeStruct((T, NSA_HEADS * HEAD_DIM), BF16),
        compiler_params=_params(("parallel", "arbitrary"), 48), name="nsa_" + mode,
    )(qi, kj, fl, *args)


def _sb_kernel(qi_ref, kj_ref, fl_ref, q_ref, k_ref, vT_ref, lm_ref, *rest, tq, tk, resume):
    if resume:
        acc_in, carry_in, o_ref, carry_sc, acc_sc = rest
    else:
        o_ref, acc_o, carry_o, carry_sc, acc_sc = rest
    p = pl.program_id(0)
    i = qi_ref[p]
    j = kj_ref[p]
    fl = fl_ref[p]

    @pl.when((fl & 1) != 0)
    def _():
        if resume:
            for h in range(SB_HEADS):
                carry_sc[h] = carry_in[h:h + 1, :]
                acc_sc[h] = acc_in[h * HEAD_DIM:(h + 1) * HEAD_DIM, :]
        else:
            carry_sc[...] = jnp.zeros(carry_sc.shape, F32)
            acc_sc[...] = jnp.zeros(acc_sc.shape, F32)

    @pl.when((jnp.max(carry_sc[...]) > SB_DEAD_LOG2) & ((fl & 4) == 0))
    def _():
        t = i * tq + lax.broadcasted_iota(jnp.int32, (1, tq), 1)
        kpos = j * tk + lax.broadcasted_iota(jnp.int32, (tk, 1), 0)
        past = kpos < t
        lm = lm_ref[...]
        hs = [slice(h * HEAD_DIM, (h + 1) * HEAD_DIM) for h in range(SB_HEADS)]
        zs = [lax.dot_general(k_ref[:, sl], q_ref[:, sl], _NT, preferred_element_type=F32) for sl in hs]
        lgs, css = [], []
        for h in range(SB_HEADS):
            z = zs[h]
            sp = jnp.maximum(z, 0.0) + jnp.log2(1.0 + jnp.exp2(-jnp.abs(z)))
            lk = jnp.where(past, -sp, 0.0)
            hi = lk.astype(BF16)
            lo = (lk - hi.astype(F32)).astype(BF16)
            css.append(jnp.dot(lm, jnp.concatenate([hi, lo], axis=1), preferred_element_type=F32))
            lgs.append(z - sp)
            carry_old = carry_sc[h]
            carry_sc[h] = carry_old + jnp.sum(lk, axis=0, keepdims=True)
            css[h] = css[h][:, :tq] + css[h][:, tq:] + carry_old
        for h in range(SB_HEADS):
            wgt = jnp.where(past, jnp.exp2(lgs[h] + css[h]), 0.0)
            acc_sc[h] = acc_sc[h] + jnp.dot(vT_ref[hs[h], :], wgt.astype(BF16), preferred_element_type=F32)

    @pl.when((fl & 2) != 0)
    def _():
        for h in range(SB_HEADS):
            o_ref[:, h * HEAD_DIM:(h + 1) * HEAD_DIM] = acc_sc[h].T.astype(BF16)
            if not resume:
                acc_o[h * HEAD_DIM:(h + 1) * HEAD_DIM, :] = acc_sc[h]
                carry_o[h:h + 1, :] = carry_sc[h]
        if not resume:
            carry_o[SB_HEADS:, :] = jnp.zeros((8 - SB_HEADS, tq), F32)


def _sb_call(q, k, vT, steps, tq, tk, state=None):
    T, W = q.shape
    qi, kj, fl = steps
    lmat = jnp.asarray(np.triu(np.ones((tk, tk), np.float32), 1), BF16)
    resume = state is not None
    qtile = lambda shape: pl.BlockSpec(shape, lambda p, qi, kj, fl: (qi[p], 0))
    qtileT = lambda rows: pl.BlockSpec((rows, tq), lambda p, qi, kj, fl: (0, qi[p]))
    in_specs = [qtile((tq, W)),
                pl.BlockSpec((tk, W), lambda p, qi, kj, fl: (kj[p], 0)),
                pl.BlockSpec((W, tk), lambda p, qi, kj, fl: (0, kj[p])),
                pl.BlockSpec((tk, tk), lambda p, qi, kj, fl: (0, 0))]
    y_sds = jax.ShapeDtypeStruct((T, W), BF16)
    if resume:
        in_specs += [qtileT(W), qtileT(8)]
        out_specs, out_shape = qtile((tq, W)), y_sds
    else:
        out_specs = [qtile((tq, W)), qtileT(W), qtileT(8)]
        out_shape = [y_sds, jax.ShapeDtypeStruct((W, T), F32), jax.ShapeDtypeStruct((8, T), F32)]
    return pl.pallas_call(
        functools.partial(_sb_kernel, tq=tq, tk=tk, resume=resume),
        grid_spec=pltpu.PrefetchScalarGridSpec(
            num_scalar_prefetch=3, grid=(int(qi.shape[0]),),
            in_specs=in_specs, out_specs=out_specs,
            scratch_shapes=[pltpu.VMEM((SB_HEADS, 1, tq), F32), pltpu.VMEM((SB_HEADS, HEAD_DIM, tq), F32)]),
        out_shape=out_shape,
        compiler_params=_params(("arbitrary",), 48), name="sb_far" if resume else "sb_near",
    )(qi, kj, fl, q, k, vT, lmat, *(state or ()))


def _stick_breaking(q, k, vT):
    T, W = q.shape
    tq = _tile(T, 512)
    tk = _tile(T, 256)
    nq = T // tq
    hi = lambda i: ((i + 1) * tq - 2) // tk
    lo_near = lambda i: max(0, hi(i) - SB_NEAR_TILES + 1)
    near = _steps(nq, lo_near, hi, reverse=True)
    y_near, acc, carry = _sb_call(q, k, vT, near, tq, tk)
    qi, kj, fl = [], [], []
    for i in range(nq):
        js = list(range(lo_near(i) - 1, -1, -1))
        for n, j in enumerate(js or [0]):
            qi.append(i)
            kj.append(j)
            fl.append((1 if n == 0 else 0) | (2 if n == max(len(js), 1) - 1 else 0) | (0 if js else 4))
    far = tuple(jnp.asarray(np.array(a, np.int32)) for a in (qi, kj, fl))
    with_far = [i for i in range(nq) if lo_near(i) > 0]
    if not with_far:
        return y_near
    alive = jnp.max(carry[:SB_HEADS, with_far[0] * tq:]) > SB_DEAD_LOG2
    return lax.cond(alive, lambda: _sb_call(q, k, vT, far, tq, tk, state=(acc, carry)), lambda: y_near)


def _memattn_kernel(q_ref, k_ref, vT_ref, o_ref):
    for h in range(MEM_HEADS):
        sl = slice(h * HEAD_DIM, (h + 1) * HEAD_DIM)
        sT = lax.dot_general(k_ref[:, sl], q_ref[:, sl], _NT, preferred_element_type=F32)
        e = jnp.exp2(sT - jnp.max(sT, axis=0, keepdims=True))
        l = jnp.sum(e, axis=0, keepdims=True)
        oT = jnp.dot(vT_ref[sl, :], e.astype(BF16), preferred_element_type=F32) * (1.0 / l)
        o_ref[:, sl] = oT.T.astype(BF16)


def _mem_attention(q, k, vT):
    T, W = q.shape
    M = k.shape[0]
    tq = _tile(T, 512)
    return pl.pallas_call(
        _memattn_kernel, grid=(T // tq,),
        in_specs=[pl.BlockSpec((tq, W), lambda i: (i, 0)),
                  pl.BlockSpec((M, W), lambda i: (0, 0)),
                  pl.BlockSpec((W, M), lambda i: (0, 0))],
        out_specs=pl.BlockSpec((tq, W), lambda i: (i, 0)),
        out_shape=jax.ShapeDtypeStruct((T, W), BF16),
        compiler_params=_params(("parallel",), 32), name="mem_attention",
    )(q, k, vT)


def _merge_kernel(yc_ref, ys_ref, yw_ref, ysb_ref, ym_ref, g0_ref, g1_ref, g2_ref,
                  wn_ref, wsb_ref, wm_ref, o_ref, yn_sc):
    @pl.when(pl.program_id(1) == 0)
    def _():
        yn_sc[...] = (yc_ref[...].astype(F32) + ys_ref[...].astype(F32) + yw_ref[...].astype(F32)).astype(BF16)

    a = jnp.dot(yn_sc[...], wn_ref[...], preferred_element_type=F32)
    b = jnp.dot(ysb_ref[...], wsb_ref[...], preferred_element_type=F32)
    c = jnp.dot(ym_ref[...], wm_ref[...], preferred_element_type=F32)
    sig = lambda ref: jax.nn.sigmoid(ref[...].astype(F32))
    o = sig(g0_ref) * a + sig(g1_ref) * b + sig(g2_ref) * c
    o_ref[...] = o.astype(BF16)


def _merge(yc, ys, yw, ysb, ym, P, wn, wsb, wm, gate_col0):
    T = yc.shape[0]
    D = wn.shape[1]
    tm = _tile(T, 512)
    tn = 512
    assert D % tn == 0 and gate_col0 % tn == 0
    gb = gate_col0 // tn
    nb = D // tn
    rowi = lambda w: pl.BlockSpec((tm, w), lambda i, j: (i, 0))
    gate = lambda b: pl.BlockSpec((tm, tn), lambda i, j, b=b: (i, gb + b * nb + j))
    wcol = lambda k: pl.BlockSpec((k, tn), lambda i, j: (0, j))
    return pl.pallas_call(
        _merge_kernel, grid=(T // tm, nb),
        in_specs=[rowi(yc.shape[1]), rowi(ys.shape[1]), rowi(yw.shape[1]), rowi(ysb.shape[1]), rowi(ym.shape[1]),
                  gate(0), gate(1), gate(2), wcol(wn.shape[0]), wcol(wsb.shape[0]), wcol(wm.shape[0])],
        out_specs=pl.BlockSpec((tm, tn), lambda i, j: (i, j)),
        out_shape=jax.ShapeDtypeStruct((T, D), BF16),
        scratch_shapes=[pltpu.VMEM((tm, yc.shape[1]), BF16)],
        compiler_params=_params(("parallel", "arbitrary"), 48), name="merge",
    )(yc, ys, yw, ysb, ym, P, P, P, wn, wsb, wm)


def _out_kernel(mix_ref, w_ref, x_ref, g_ref, x2_o, h2_o):
    x2 = x_ref[...] + jnp.dot(mix_ref[...], w_ref[...], preferred_element_type=F32)
    x2_o[...] = x2
    h2_o[...] = _rms(x2, g_ref[...]).astype(BF16)


def _out_proj(mixed, w, x, gain):
    T, D = x.shape
    tm = _tile(T, 256)
    return pl.pallas_call(
        _out_kernel, grid=(T // tm,),
        in_specs=[pl.BlockSpec((tm, D), lambda i: (i, 0)),
                  pl.BlockSpec((D, D), lambda i: (0, 0)),
                  pl.BlockSpec((tm, D), lambda i: (i, 0)),
                  pl.BlockSpec((1, D), lambda i: (0, 0))],
        out_specs=[pl.BlockSpec((tm, D), lambda i: (i, 0)), pl.BlockSpec((tm, D), lambda i: (i, 0))],
        out_shape=[jax.ShapeDtypeStruct((T, D), F32), jax.ShapeDtypeStruct((T, D), BF16)],
        compiler_params=_params(("parallel",), 48), name="out_proj",
    )(mixed, w, x, gain)


def _ffn_kernel(h_ref, wg_ref, wu_ref, wd_ref, x_ref, o_ref, z_sc, *, nf):
    f = pl.program_id(1)

    def up():
        a = jnp.dot(h_ref[...], wg_ref[...], preferred_element_type=F32)
        b = jnp.dot(h_ref[...], wu_ref[...], preferred_element_type=F32)
        return (a * jax.nn.sigmoid(a) * b).astype(BF16)

    def down():
        return jnp.dot(z_sc[...], wd_ref[...], preferred_element_type=F32)

    @pl.when(f == 0)
    def _():
        z_sc[...] = up()

    @pl.when((f > 0) & (f < nf))
    def _():
        c = down()
        z_new = up()
        o_ref[...] = jnp.where(f == 1, x_ref[...], o_ref[...]) + c
        z_sc[...] = z_new

    @pl.when(f == nf)
    def _():
        o_ref[...] += down()


def _ffn(h2, wg, wu, wd, x2):
    T, D = x2.shape
    F = wg.shape[1]
    tm = _tile(T, 1024)
    tf = 512
    assert F % tf == 0
    nf = F // tf
    assert nf >= 2
    return pl.pallas_call(
        functools.partial(_ffn_kernel, nf=nf), grid=(T // tm, nf + 1),
        in_specs=[pl.BlockSpec((tm, D), lambda i, f: (i, 0)),
                  pl.BlockSpec((D, tf), lambda i, f: (0, jnp.minimum(f, nf - 1))),
                  pl.BlockSpec((D, tf), lambda i, f: (0, jnp.minimum(f, nf - 1))),
                  pl.BlockSpec((tf, D), lambda i, f: (jnp.maximum(f - 1, 0), 0)),
                  pl.BlockSpec((tm, D), lambda i, f: (i, 0), pipeline_mode=pl.Buffered(1))],
        out_specs=pl.BlockSpec((tm, D), lambda i, f: (i, 0)),
        out_shape=jax.ShapeDtypeStruct((T, D), F32),
        scratch_shapes=[pltpu.VMEM((tm, tf), BF16)],
        compiler_params=_params(("parallel", "arbitrary"), 56), name="ffn",
    )(h2, wg.astype(BF16), wu.astype(BF16), wd.astype(BF16), x2)


def _layer(x, mem, pos_col, posc_col, consts, attn_norm, w_in, nsa_q_norm, nsa_kc_norm, nsa_ks_norm, nsa_kw_norm,
           cmp_k_pe, cmp_k_w1, cmp_k_w2, cmp_v_pe, cmp_v_w1, cmp_v_w2, mem_norm, w_mem_kv,
           mem_q_norm, mem_k_norm, w_o_nsa, w_o_sb, w_o_mem, w_out, ffn_norm,
           w_ffn_gate, w_ffn_up, w_ffn_down):
    T, D = x.shape
    inv2, sgn, ovT = consts
    hd = HEAD_DIM
    row = lambda g: g.reshape(1, -1)

    q_w, kv_w, gn_w = NSA_HEADS * hd, 6 * NSA_GROUPS * hd, 3 * NSA_HEADS
    sb_w, mq_w, gm_w = 3 * SB_HEADS * hd, MEM_HEADS * hd, N_BRANCH * D
    o_gn = q_w + kv_w
    o_sb = o_gn + gn_w
    o_mq = o_sb + sb_w
    o_gm = o_mq + mq_w
    assert w_in.shape[1] == o_gm + gm_w
    w_main = jnp.concatenate([w_in[:, :o_gn], w_in[:, o_sb:]], axis=1).astype(BF16)
    w_gate = jnp.pad(w_in[:, o_gn:o_sb], ((0, 0), (0, LANE - gn_w))).astype(BF16)
    gate_col0 = q_w + kv_w + sb_w + mq_w
    P, Pg = _in_proj(x, row(attn_norm), w_main, w_gate)

    (qn, kc_raw, vc_raw, ksn, kwn, vsT, vwT, sbq, sbk, sbvT, memq, gT) = _prep(
        P, Pg, pos_col, inv2, sgn, row(nsa_q_norm), row(nsa_ks_norm), row(nsa_kw_norm), row(mem_q_norm))

    gates = gT[:gn_w].reshape(NSA_GROUPS, NSA_HPG, 3, T).transpose(2, 0, 1, 3)
    gates = jnp.pad(gates, ((0, 0), (0, 0), (0, 8 - NSA_HPG), (0, 0)))

    n_pad = T // CMP_STRIDE
    half = CMP_LEN // 2

    def w1_pack(w1):
        return jnp.concatenate([w1[:half].reshape(half * hd, -1), w1[half:].reshape(half * hd, -1)], axis=1).astype(BF16)

    def pe_pack(pe):
        return jnp.pad(pe.reshape(2, half * hd), ((0, 6), (0, 0))).astype(BF16)

    kc, vcT = _compress(
        kc_raw.reshape(NSA_GROUPS, n_pad, CMP_STRIDE * hd), vc_raw.reshape(NSA_GROUPS, n_pad, CMP_STRIDE * hd),
        w1_pack(cmp_k_w1), w1_pack(cmp_v_w1), pe_pack(cmp_k_pe), pe_pack(cmp_v_pe),
        cmp_k_w2.astype(BF16), cmp_v_w2.astype(BF16), row(nsa_kc_norm), posc_col, inv2, sgn)

    y_cmp, bias = _cmp_select(qn, kc, vcT, ovT, gates)
    y_sel = _nsa_flash("sel", qn, ksn, vsT, gates, bias)
    y_win = _nsa_flash("win", qn, kwn, vwT, gates)
    y_sb = _stick_breaking(sbq, sbk, sbvT)

    mk, mvT = _memkv(mem, row(mem_norm), w_mem_kv.astype(BF16), row(mem_k_norm))
    y_mem = _mem_attention(memq, mk, mvT)

    mixed = _merge(y_cmp, y_sel, y_win, y_sb, y_mem, P,
                   w_o_nsa.astype(BF16), w_o_sb.astype(BF16), w_o_mem.astype(BF16), gate_col0)
    x2, h2 = _out_proj(mixed, w_out.astype(BF16), x, row(ffn_norm))
    return _ffn(h2, w_ffn_gate, w_ffn_up, w_ffn_down, x2)


def kernel(x, mem, positions, attn_norm, w_in, nsa_q_norm, nsa_kc_norm, nsa_ks_norm, nsa_kw_norm, cmp_k_pe, cmp_k_w1, cmp_k_w2, cmp_v_pe, cmp_v_w1, cmp_v_w2, mem_norm, w_mem_kv, mem_q_norm, mem_k_norm, w_o_nsa, w_o_sb, w_o_mem, w_out, ffn_norm, w_ffn_gate, w_ffn_up, w_ffn_down):
    B, T, D = x.shape
    assert T % (4 * LANE) == 0 and T // SEL_BLOCK >= 8
    n_pad = T // CMP_STRIDE
    n_sel = T // SEL_BLOCK
    inv = 1.0 / (ROPE_THETA ** (jnp.arange(0, HEAD_DIM, 2, dtype=F32) / HEAD_DIM))
    inv2 = jnp.concatenate([inv, inv]).reshape(1, HEAD_DIM)
    sgn = jnp.concatenate([-jnp.ones((HEAD_DIM // 2,), F32), jnp.ones((HEAD_DIM // 2,), F32)]).reshape(1, HEAD_DIM)
    cs = np.arange(n_pad)[None, :] * CMP_STRIDE
    ss = np.arange(n_sel)[:, None] * SEL_BLOCK
    ovT = jnp.asarray(((cs < ss + SEL_BLOCK) & (cs + CMP_LEN - 1 >= ss)).astype(np.float32), BF16)
    consts = (inv2, sgn, ovT)
    depth = w_in.shape[0]
    outs = []
    for b in range(B):
        xb = x[b]
        posf = positions[b].astype(F32)
        pos_col = posf.reshape(T, 1)
        posc = jnp.concatenate([posf[CMP_LEN - 1::CMP_STRIDE], posf[-1:]]).reshape(n_pad, 1)
        for l in range(depth):
            xb = _layer(xb, mem[b], pos_col, posc, consts, attn_norm[l], w_in[l], nsa_q_norm[l], nsa_kc_norm[l],
                        nsa_ks_norm[l], nsa_kw_norm[l], cmp_k_pe[l], cmp_k_w1[l], cmp_k_w2[l], cmp_v_pe[l],
                        cmp_v_w1[l], cmp_v_w2[l], mem_norm[l], w_mem_kv[l], mem_q_norm[l], mem_k_norm[l],
                        w_o_nsa[l], w_o_sb[l], w_o_mem[l], w_out[l], ffn_norm[l],
                        w_ffn_gate[l], w_ffn_up[l], w_ffn_down[l])
        outs.append(xb)
    return outs[0][None] if B == 1 else jnp.stack(outs, axis=0)
```

```python
import functools

import numpy as np
import jax
import jax.numpy as jnp
from jax import lax
from jax.experimental import pallas as pl
from jax.experimental.pallas import tpu as pltpu

HEAD_DIM = 128
NSA_HEADS = 8
NSA_GROUPS = 2
NSA_HPG = NSA_HEADS // NSA_GROUPS
SB_HEADS = 4
MEM_HEADS = 4
CMP_LEN = 32
CMP_STRIDE = 16
CMP_HIDDEN = 2 * HEAD_DIM
SEL_BLOCK = 64
SEL_TOPK = 16
WINDOW = 512
ROPE_THETA = 10000.0
NORM_EPS = 1e-6
NEG_BIG = -1e30
N_BRANCH = 3
SCALE = HEAD_DIM ** -0.5
LOG2E = 1.4426950408889634
QSCALE = SCALE * LOG2E
SOFTMAX_BOUND_MAX = 50.0
CMP_BUCKETS = 8
SB_NEAR_TILES = 3
SB_DEAD_LOG2 = -160.0
VAUG = HEAD_DIM + 16

LANE = 128
MIB = 1 << 20
BF16 = jnp.bfloat16
F32 = jnp.float32

_NT = (((1,), (1,)), ((), ()))


def _tile(n, pref):
    t = min(n, pref)
    assert n % t == 0, (n, pref)
    return t


def _params(sem, vmem_mib):
    return pltpu.CompilerParams(dimension_semantics=sem, vmem_limit_bytes=vmem_mib * MIB)


def _rms(x, gain):
    return x * lax.rsqrt(jnp.mean(x * x, axis=-1, keepdims=True) + NORM_EPS) * gain


def _rope_tables(pos, inv2, sgn):
    ang = pos * inv2
    return jnp.cos(ang), jnp.sin(ang) * sgn


def _rope(x, c, s):
    return x * c + pltpu.roll(x, HEAD_DIM // 2, 1) * s


def _proj_kernel(x_ref, g_ref, w_ref, wg_ref, o_ref, og_ref, hn_ref):
    @pl.when(pl.program_id(1) == 0)
    def _():
        hn_ref[...] = _rms(x_ref[...], g_ref[...]).astype(BF16)
        og_ref[...] = jnp.dot(hn_ref[...], wg_ref[...], preferred_element_type=F32)

    o_ref[...] = jnp.dot(hn_ref[...], w_ref[...], preferred_element_type=F32).astype(BF16)


def _in_proj(x, gain, w_main, w_gate):
    T, D = x.shape
    N = w_main.shape[1]
    tm = _tile(T, 1024)
    tn = 1536
    assert N % tn == 0
    return pl.pallas_call(
        _proj_kernel,
        grid=(T // tm, N // tn),
        in_specs=[
            pl.BlockSpec((tm, D), lambda i, j: (i, 0)),
            pl.BlockSpec((1, D), lambda i, j: (0, 0)),
            pl.BlockSpec((D, tn), lambda i, j: (0, j)),
            pl.BlockSpec((D, LANE), lambda i, j: (0, 0)),
        ],
        out_specs=[pl.BlockSpec((tm, tn), lambda i, j: (i, j)), pl.BlockSpec((tm, LANE), lambda i, j: (i, 0))],
        out_shape=[jax.ShapeDtypeStruct((T, N), BF16), jax.ShapeDtypeStruct((T, LANE), F32)],
        scratch_shapes=[pltpu.VMEM((tm, D), BF16)],
        compiler_params=_params(("parallel", "arbitrary"), 52),
        name="in_proj",
    )(x, gain, w_main, w_gate)


def _prep_kernel(pos_ref, inv_ref, sgn_ref, gq_ref, gks_ref, gkw_ref, gmq_ref,
                 q_ref, kc_ref, vc_ref, ks_ref, vs_ref, kw_ref, vw_ref,
                 sq_ref, sk_ref, sv_ref, mq_ref, gn_ref,
                 qn_o, kc_o, vc_o, ks_o, kw_o, vsT_o, vwT_o, sq_o, sk_o, svT_o, mq_o, gT_o, qn2_o, kn2_o):
    c, s = _rope_tables(pos_ref[...], inv_ref[...], sgn_ref[...])
    hd = HEAD_DIM
    tp = pos_ref.shape[0]
    f32 = lambda ref, sl: ref[:, sl].astype(F32)
    ones8 = jnp.ones((8, hd), F32)

    def sqnorm_rows(xb):
        x = xb.astype(F32)
        return lax.dot_general(ones8, x * x, _NT, preferred_element_type=F32)

    for h in range(NSA_HEADS):
        sl = slice(h * hd, (h + 1) * hd)
        qb = (_rope(_rms(f32(q_ref, sl), gq_ref[...]), c, s) * QSCALE).astype(BF16)
        qn_o[:, sl] = qb
        qn2_o[h * 8:(h + 1) * 8, :] = sqnorm_rows(qb)
    ones_rows = (lax.broadcasted_iota(jnp.int32, (VAUG - hd, tp), 0) == 0).astype(F32).astype(BF16)
    for g in range(NSA_GROUPS):
        sl = slice(g * hd, (g + 1) * hd)
        kc_o[g] = kc_ref[:, sl]
        vc_o[g] = vc_ref[:, sl]
        ksb = _rope(_rms(f32(ks_ref, sl), gks_ref[...]), c, s).astype(BF16)
        kwb = _rope(_rms(f32(kw_ref, sl), gkw_ref[...]), c, s).astype(BF16)
        ks_o[:, sl] = ksb
        kw_o[:, sl] = kwb
        kn2_o[g * 8:(g + 1) * 8, :] = sqnorm_rows(ksb)
        kn2_o[(NSA_GROUPS + g) * 8:(NSA_GROUPS + g + 1) * 8, :] = sqnorm_rows(kwb)
        vsT_o[g * VAUG:g * VAUG + hd, :] = f32(vs_ref, sl).T.astype(BF16)
        vsT_o[g * VAUG + hd:(g + 1) * VAUG, :] = ones_rows
        vwT_o[g * VAUG:g * VAUG + hd, :] = f32(vw_ref, sl).T.astype(BF16)
        vwT_o[g * VAUG + hd:(g + 1) * VAUG, :] = ones_rows
    for h in range(SB_HEADS):
        sl = slice(h * hd, (h + 1) * hd)
        sq_o[:, sl] = (f32(sq_ref, sl) * QSCALE).astype(BF16)
        svT_o[sl, :] = f32(sv_ref, sl).T.astype(BF16)
    sk_o[...] = sk_ref[...]
    for h in range(MEM_HEADS):
        sl = slice(h * hd, (h + 1) * hd)
        mq_o[:, sl] = (_rms(f32(mq_ref, sl), gmq_ref[...]) * QSCALE).astype(BF16)
    gT_o[...] = jax.nn.sigmoid(gn_ref[...]).T


def _prep(P, Pg, pos_col, inv2, sgn, gq, gks, gkw, gmq):
    T = P.shape[0]
    tp = _tile(T, 512)
    hd = HEAD_DIM
    row = lambda w, c: pl.BlockSpec((tp, w), lambda i, c=c: (i, c))
    const = lambda: pl.BlockSpec((1, hd), lambda i: (0, 0))
    in_specs = [pl.BlockSpec((tp, 1), lambda i: (i, 0)), const(), const(), const(), const(), const(), const(),
                row(8 * hd, 0),
                row(2 * hd, 4), row(2 * hd, 5),
                row(2 * hd, 6), row(2 * hd, 7),
                row(2 * hd, 8), row(2 * hd, 9),
                row(4 * hd, 5), row(4 * hd, 6), row(4 * hd, 7),
                row(4 * hd, 8),
                row(hd, 0)]
    colT = lambda w: pl.BlockSpec((w, tp), lambda i: (0, i))
    grp = lambda w: pl.BlockSpec((NSA_GROUPS, tp, w), lambda i: (0, i, 0))
    out_specs = [row(8 * hd, 0), grp(hd), grp(hd), row(2 * hd, 0),
                 row(2 * hd, 0), colT(NSA_GROUPS * VAUG), colT(NSA_GROUPS * VAUG),
                 row(4 * hd, 0), row(4 * hd, 0), colT(4 * hd), row(4 * hd, 0), colT(hd),
                 colT(NSA_HEADS * 8), colT(2 * NSA_GROUPS * 8)]
    sds = jax.ShapeDtypeStruct
    out_shape = [sds((T, 8 * hd), BF16),
                 sds((NSA_GROUPS, T, hd), BF16), sds((NSA_GROUPS, T, hd), BF16), sds((T, 2 * hd), BF16),
                 sds((T, 2 * hd), BF16),
                 sds((NSA_GROUPS * VAUG, T), BF16), sds((NSA_GROUPS * VAUG, T), BF16),
                 sds((T, 4 * hd), BF16), sds((T, 4 * hd), BF16), sds((4 * hd, T), BF16), sds((T, 4 * hd), BF16),
                 sds((hd, T), F32), sds((NSA_HEADS * 8, T), F32), sds((2 * NSA_GROUPS * 8, T), F32)]
    return pl.pallas_call(
        _prep_kernel, grid=(T // tp,), in_specs=in_specs, out_specs=out_specs, out_shape=out_shape,
        compiler_params=_params(("parallel",), 48), name="prep",
    )(pos_col, inv2, sgn, gq, gks, gkw, gmq, *([P] * 11), Pg)


def _gelu_tanh(x):
    return 0.5 * x * (1.0 + jnp.tanh(0.7978845608028654 * (x + 0.044715 * (x * x * x))))


def _compress_one(x, w1, pe, w2):
    n = x.shape[0]
    ab = jnp.dot(x, w1, preferred_element_type=F32)
    pr = jnp.dot(pe, w1, preferred_element_type=F32)
    pec = pr[0:1, :CMP_HIDDEN] + pr[1:2, CMP_HIDDEN:]
    hid = ab[:, :CMP_HIDDEN] + pltpu.roll(ab[:, CMP_HIDDEN:], n - 1, 0) + pec
    return jnp.dot(_gelu_tanh(hid).astype(BF16), w2, preferred_element_type=F32)


def _compress_kernel(xk_ref, xv_ref, w1k_ref, w1v_ref, pek_ref, pev_ref, w2k_ref, w2v_ref,
                     gk_ref, pos_ref, inv_ref, sgn_ref, kc_o, vcT_o):
    c, s = _rope_tables(pos_ref[...], inv_ref[...], sgn_ref[...])
    k = _compress_one(xk_ref[0], w1k_ref[...], pek_ref[...], w2k_ref[...])
    kc_o[0] = _rope(_rms(k, gk_ref[...]), c, s).astype(BF16)
    v = _compress_one(xv_ref[0], w1v_ref[...], pev_ref[...], w2v_ref[...])
    vcT_o[0] = v.T.astype(BF16)


def _compress(xk, xv, w1k, w1v, pek, pev, w2k, w2v, gk, posc, inv2, sgn):
    G, n, W = xk.shape
    full = lambda a: pl.BlockSpec(a.shape, lambda g: (0,) * a.ndim)
    grp = pl.BlockSpec((1, n, W), lambda g: (g, 0, 0))
    return pl.pallas_call(
        _compress_kernel, grid=(G,),
        in_specs=[grp, grp, full(w1k), full(w1v), full(pek), full(pev), full(w2k), full(w2v),
                  full(gk), full(posc), full(inv2), full(sgn)],
        out_specs=[pl.BlockSpec((1, n, HEAD_DIM), lambda g: (g, 0, 0)),
                   pl.BlockSpec((1, HEAD_DIM, n), lambda g: (g, 0, 0))],
        out_shape=[jax.ShapeDtypeStruct((G, n, HEAD_DIM), BF16), jax.ShapeDtypeStruct((G, HEAD_DIM, n), BF16)],
        compiler_params=_params(("parallel",), 48), name="compress",
    )(xk, xv, w1k, w1v, pek, pev, w2k, w2v, gk, posc, inv2, sgn)


def _memkv_kernel(mem_ref, gm_ref, w_ref, gk_ref, k_o, vT_o):
    hn = _rms(mem_ref[...], gm_ref[...]).astype(BF16)
    kv = jnp.dot(hn, w_ref[...], preferred_element_type=F32)
    mw = MEM_HEADS * HEAD_DIM
    for h in range(MEM_HEADS):
        sl = slice(h * HEAD_DIM, (h + 1) * HEAD_DIM)
        k_o[:, sl] = _rms(kv[:, sl], gk_ref[...]).astype(BF16)
        vT_o[sl, :] = kv[:, mw + h * HEAD_DIM: mw + (h + 1) * HEAD_DIM].T.astype(BF16)


def _memkv(mem, gm, w, gk):
    M = mem.shape[0]
    mw = MEM_HEADS * HEAD_DIM
    return pl.pallas_call(
        _memkv_kernel,
        out_shape=[jax.ShapeDtypeStruct((M, mw), BF16), jax.ShapeDtypeStruct((mw, M), BF16)],
        compiler_params=pltpu.CompilerParams(vmem_limit_bytes=32 * MIB), name="mem_kv",
    )(mem, gm, w, gk)


def _cmp_kernel(q_ref, kc_ref, vcT_ref, ovT_ref, g_ref, y_in, b_in, y_ref, b_ref, *, tq, i0, n_pad, n_sel, top_k):
    del y_in, b_in
    i = i0 + pl.program_id(1)
    t = i * tq + lax.broadcasted_iota(jnp.int32, (1, tq), 1)
    n_end = lax.broadcasted_iota(jnp.int32, (n_pad, 1), 0) * CMP_STRIDE + (CMP_LEN - 1)
    valid = n_end <= t
    kc = kc_ref[0]
    vcT = vcT_ref[0]
    has_valid = (t >= CMP_LEN - 1).astype(F32)
    sms = []
    for h in range(NSA_HPG):
        q_h = q_ref[:, h * HEAD_DIM:(h + 1) * HEAD_DIM]
        sms.append(jnp.where(valid, lax.dot_general(kc, q_h, _NT, preferred_element_type=F32), NEG_BIG))
    psum = jnp.zeros((n_pad, tq), F32)
    for h in range(NSA_HPG):
        e = jnp.exp2(sms[h] - jnp.max(sms[h], axis=0, keepdims=True))
        p = e * (has_valid / jnp.sum(e, axis=0, keepdims=True))
        oT = jnp.dot(vcT, p.astype(BF16), preferred_element_type=F32)
        y_ref[:, h * HEAD_DIM:(h + 1) * HEAD_DIM] = (oT * g_ref[0, 0, h:h + 1, :]).T.astype(BF16)
        psum = psum + p
    hi = psum.astype(BF16)
    r1 = psum - hi.astype(F32)
    mid = r1.astype(BF16)
    lo = (r1 - mid.astype(F32)).astype(BF16)
    parts = jnp.dot(ovT_ref[...], jnp.concatenate([hi, mid, lo], axis=1), preferred_element_type=F32)
    imp = parts[:, :tq] + parts[:, tq:2 * tq] + parts[:, 2 * tq:]
    s_i = lax.broadcasted_iota(jnp.int32, (n_sel, 1), 0)
    cur = lax.shift_right_logical(t, 6)
    forced = (s_i == 0) | (s_i == cur) | (s_i == cur - 1)
    future = s_i * SEL_BLOCK > t
    w = jnp.where(forced, jnp.inf, jnp.where(future, -jnp.inf, imp))
    s_f = jnp.broadcast_to(s_i.astype(F32), (n_sel, tq))
    for _ in range(top_k):
        m = jnp.max(w, axis=0, keepdims=True)
        idx = jnp.min(jnp.where(w == m, s_f, float(n_sel)), axis=0, keepdims=True)
        w = jnp.where(s_f == idx, -jnp.inf, w)
    b_ref[0] = jnp.where(future, NEG_BIG, jnp.where(w == -jnp.inf, 0.0, NEG_BIG))


def _round_up(n, m):
    return -(-n // m) * m


def _cmp_select(qn, kc, vcT, ovT, gates):
    T = qn.shape[0]
    G, n_pad, _ = kc.shape
    n_sel = T // SEL_BLOCK
    tq = _tile(T, 256)
    nq = T // tq
    gw = NSA_HPG * HEAD_DIM
    nb = min(CMP_BUCKETS, nq)
    assert nq % nb == 0
    per = nq // nb
    y = jnp.zeros((T, NSA_HEADS * HEAD_DIM), BF16)
    bias = jnp.full((G, n_sel, T), NEG_BIG, F32)
    for b in range(nb):
        i0 = b * per
        t_max = (i0 + per) * tq - 1
        n_len = min(n_pad, _round_up(max(t_max - (CMP_LEN - 1), 0) // CMP_STRIDE + 1, LANE))
        s_len = min(n_sel, _round_up(t_max // SEL_BLOCK + 1, 8))
        kern = functools.partial(_cmp_kernel, tq=tq, i0=i0, n_pad=n_len, n_sel=s_len, top_k=min(SEL_TOPK, n_sel))
        y, bias = pl.pallas_call(
            kern, grid=(G, per),
            in_specs=[pl.BlockSpec((tq, gw), lambda g, i, i0=i0: (i0 + i, g)),
                      pl.BlockSpec((1, n_len, HEAD_DIM), lambda g, i: (g, 0, 0)),
                      pl.BlockSpec((1, HEAD_DIM, n_len), lambda g, i: (g, 0, 0)),
                      pl.BlockSpec((s_len, n_len), lambda g, i: (0, 0)),
                      pl.BlockSpec((1, 1, 8, tq), lambda g, i, i0=i0: (0, g, 0, i0 + i)),
                      pl.BlockSpec(memory_space=pl.ANY), pl.BlockSpec(memory_space=pl.ANY)],
            out_specs=[pl.BlockSpec((tq, gw), lambda g, i, i0=i0: (i0 + i, g)),
                       pl.BlockSpec((1, s_len, tq), lambda g, i, i0=i0: (g, 0, i0 + i))],
            out_shape=[jax.ShapeDtypeStruct(y.shape, y.dtype), jax.ShapeDtypeStruct(bias.shape, bias.dtype)],
            input_output_aliases={5: 0, 6: 1},
            compiler_params=_params(("parallel", "parallel"), 48), name="cmp_select_%d" % b,
        )(qn, kc, vcT, ovT, gates, y, bias)
    return y, bias


def _flash_kernel(qi_ref, kj_ref, fl_ref, q_ref, k_ref, vT_ref, g_ref, sh_ref, *rest, mode, bounded, tq, tk):
    if mode == "sel":
        bias_ref, o_ref, m_sc, acc_sc = rest
    else:
        o_ref, m_sc, acc_sc = rest
    p = pl.program_id(1)
    i = qi_ref[p]
    j = kj_ref[p]
    fl = fl_ref[p]

    @pl.when((fl & 1) != 0)
    def _():
        m_sc[...] = jnp.full(m_sc.shape, NEG_BIG, F32)
        acc_sc[...] = jnp.zeros(acc_sc.shape, F32)

    t = i * tq + lax.broadcasted_iota(jnp.int32, (1, tq), 1)
    kpos = j * tk + lax.broadcasted_iota(jnp.int32, (tk, 1), 0)
    shift = sh_ref[0, 0:1, :] if bounded else 0.0
    if mode == "sel":
        nb = tk // SEL_BLOCK
        rows = bias_ref[0] - shift
        base = jnp.concatenate(
            [jnp.broadcast_to(rows[b:b + 1, :], (SEL_BLOCK, tq)) for b in range(nb)], axis=0)
        mask_add = jnp.where(kpos <= t, base, NEG_BIG)
    else:
        mask_add = jnp.where((kpos <= t) & (kpos > t - WINDOW), 0.0 - shift, NEG_BIG)
    k = k_ref[...]
    vT = vT_ref[...]
    sTs = []
    for h in range(NSA_HPG):
        q_h = q_ref[:, h * HEAD_DIM:(h + 1) * HEAD_DIM]
        sTs.append(lax.dot_general(k, q_h, _NT, preferred_element_type=F32) + mask_add)
    if bounded:
        for h in range(NSA_HPG):
            pT = jnp.exp2(sTs[h]).astype(BF16)
            acc_sc[h] = acc_sc[h] + jnp.dot(vT, pT, preferred_element_type=F32)
    else:
        pTs, alphas = [], []
        for h in range(NSA_HPG):
            sT = sTs[h]
            m_old = m_sc[h]
            m_new = jnp.maximum(m_old, jnp.max(sT, axis=0, keepdims=True))
            alphas.append(jnp.exp2(m_old - m_new))
            pTs.append(jnp.exp2((sT - m_new).astype(BF16)))
            m_sc[h] = m_new
        for h in range(NSA_HPG):
            acc_sc[h] = alphas[h] * acc_sc[h] + jnp.dot(vT, pTs[h], preferred_element_type=F32)

    @pl.when((fl & 2) != 0)
    def _():
        for h in range(NSA_HPG):
            acc = acc_sc[h]
            l = acc[HEAD_DIM:HEAD_DIM + 1, :]
            o = acc[:HEAD_DIM, :] * ((1.0 / l) * g_ref[0, 0, h:h + 1, :])
            o_ref[:, h * HEAD_DIM:(h + 1) * HEAD_DIM] = o.T.astype(BF16)


def _steps(nq, lo_fn, hi_fn, reverse=False):
    qi, kj, fl = [], [], []
    for i in range(nq):
        js = list(range(lo_fn(i), hi_fn(i) + 1))
        if reverse:
            js = js[::-1]
        for n, j in enumerate(js):
            qi.append(i)
            kj.append(j)
            fl.append((1 if n == 0 else 0) | (2 if n == len(js) - 1 else 0))
    return (jnp.asarray(np.array(qi, np.int32)), jnp.asarray(np.array(kj, np.int32)),
            jnp.asarray(np.array(fl, np.int32)))


def _nsa_flash(mode, qn, k, vT, gates, shift, bias=None):
    use_bounded = jnp.max(shift) <= SOFTMAX_BOUND_MAX
    return lax.cond(use_bounded,
                    lambda: _nsa_flash_call(mode, True, qn, k, vT, gates, shift, bias),
                    lambda: _nsa_flash_call(mode, False, qn, k, vT, gates, shift, bias))


def _nsa_flash_call(mode, bounded, qn, k, vT, gates, shift, bias):
    T = qn.shape[0]
    tq = _tile(T, 1024)
    tk = _tile(T, 512)
    nq = T // tq
    gw = NSA_HPG * HEAD_DIM
    hi = lambda i: ((i + 1) * tq - 1) // tk
    if mode == "sel":
        lo = lambda i: 0
        br = 1
    else:
        lo = lambda i: max(0, (i * tq - (WINDOW - 1)) // tk)
        br = 2
    qi, kj, fl = _steps(nq, lo, hi)
    in_specs = [pl.BlockSpec((tq, gw), lambda g, p, qi, kj, fl: (qi[p], g)),
                pl.BlockSpec((tk, HEAD_DIM), lambda g, p, qi, kj, fl: (kj[p], g)),
                pl.BlockSpec((VAUG, tk), lambda g, p, qi, kj, fl: (g, kj[p])),
                pl.BlockSpec((1, 1, 8, tq), lambda g, p, qi, kj, fl, br=br: (br, g, 0, qi[p])),
                pl.BlockSpec((1, 8, tq), lambda g, p, qi, kj, fl: (g, 0, qi[p]))]
    args = [qn, k, vT, gates, shift]
    if mode == "sel":
        in_specs.append(pl.BlockSpec((1, tk // SEL_BLOCK, tq), lambda g, p, qi, kj, fl: (g, kj[p], qi[p])))
        args.append(bias)
    kern = functools.partial(_flash_kernel, mode=mode, bounded=bounded, tq=tq, tk=tk)
    return pl.pallas_call(
        kern,
        grid_spec=pltpu.PrefetchScalarGridSpec(
            num_scalar_prefetch=3, grid=(NSA_GROUPS, int(qi.shape[0])),
            in_specs=in_specs,
            out_specs=pl.BlockSpec((tq, gw), lambda g, p, qi, kj, fl: (qi[p], g)),
            scratch_shapes=[pltpu.VMEM((NSA_HPG, 1, tq), F32), pltpu.VMEM((NSA_HPG, VAUG, tq), F32)]),
        out_shape=jax.ShapeDtypeStruct((T, NSA_HEADS * HEAD_DIM), BF16),
        compiler_params=_params(("parallel", "arbitrary"), 48),
        name="nsa_" + mode + ("_bounded" if bounded else "_online"),
    )(qi, kj, fl, *args)


def _sb_kernel(qi_ref, kj_ref, fl_ref, q_ref, k_ref, vT_ref, lm_ref, *rest, tq, tk, resume):
    if resume:
        acc_in, carry_in, o_ref, carry_sc, acc_sc = rest
    else:
        o_ref, acc_o, carry_o, carry_sc, acc_sc = rest
    p = pl.program_id(0)
    i = qi_ref[p]
    j = kj_ref[p]
    fl = fl_ref[p]

    @pl.when((fl & 1) != 0)
    def _():
        if resume:
            for h in range(SB_HEADS):
                carry_sc[h] = carry_in[h:h + 1, :]
                acc_sc[h] = acc_in[h * HEAD_DIM:(h + 1) * HEAD_DIM, :]
        else:
            carry_sc[...] = jnp.zeros(carry_sc.shape, F32)
            acc_sc[...] = jnp.zeros(acc_sc.shape, F32)

    @pl.when((jnp.max(carry_sc[...]) > SB_DEAD_LOG2) & ((fl & 4) == 0))
    def _():
        t = i * tq + lax.broadcasted_iota(jnp.int32, (1, tq), 1)
        kpos = j * tk + lax.broadcasted_iota(jnp.int32, (tk, 1), 0)
        past = kpos < t
        lm = lm_ref[...]
        hs = [slice(h * HEAD_DIM, (h + 1) * HEAD_DIM) for h in range(SB_HEADS)]
        zs = [lax.dot_general(k_ref[:, sl], q_ref[:, sl], _NT, preferred_element_type=F32) for sl in hs]
        lgs, css = [], []
        for h in range(SB_HEADS):
            z = zs[h]
            sp = jnp.maximum(z, 0.0) + jnp.log2(1.0 + jnp.exp2(-jnp.abs(z)))
            lk = jnp.where(past, -sp, 0.0)
            hi = lk.astype(BF16)
            lo = (lk - hi.astype(F32)).astype(BF16)
            css.append(jnp.dot(lm, jnp.concatenate([hi, lo], axis=1), preferred_element_type=F32))
            lgs.append(z - sp)
            carry_old = carry_sc[h]
            carry_sc[h] = carry_old + jnp.sum(lk, axis=0, keepdims=True)
            css[h] = css[h][:, :tq] + css[h][:, tq:] + carry_old
        for h in range(SB_HEADS):
            wgt = jnp.where(past, jnp.exp2(lgs[h] + css[h]), 0.0)
            acc_sc[h] = acc_sc[h] + jnp.dot(vT_ref[hs[h], :], wgt.astype(BF16), preferred_element_type=F32)

    @pl.when((fl & 2) != 0)
    def _():
        for h in range(SB_HEADS):
            o_ref[:, h * HEAD_DIM:(h + 1) * HEAD_DIM] = acc_sc[h].T.astype(BF16)
            if not resume:
                acc_o[h * HEAD_DIM:(h + 1) * HEAD_DIM, :] = acc_sc[h]
                carry_o[h:h + 1, :] = carry_sc[h]
        if not resume:
            carry_o[SB_HEADS:, :] = jnp.zeros((8 - SB_HEADS, tq), F32)


def _sb_call(q, k, vT, steps, tq, tk, state=None):
    T, W = q.shape
    qi, kj, fl = steps
    lmat = jnp.asarray(np.triu(np.ones((tk, tk), np.float32), 1), BF16)
    resume = state is not None
    qtile = lambda shape: pl.BlockSpec(shape, lambda p, qi, kj, fl: (qi[p], 0))
    qtileT = lambda rows: pl.BlockSpec((rows, tq), lambda p, qi, kj, fl: (0, qi[p]))
    in_specs = [qtile((tq, W)),
                pl.BlockSpec((tk, W), lambda p, qi, kj, fl: (kj[p], 0)),
                pl.BlockSpec((W, tk), lambda p, qi, kj, fl: (0, kj[p])),
                pl.BlockSpec((tk, tk), lambda p, qi, kj, fl: (0, 0))]
    y_sds = jax.ShapeDtypeStruct((T, W), BF16)
    if resume:
        in_specs += [qtileT(W), qtileT(8)]
        out_specs, out_shape = qtile((tq, W)), y_sds
    else:
        out_specs = [qtile((tq, W)), qtileT(W), qtileT(8)]
        out_shape = [y_sds, jax.ShapeDtypeStruct((W, T), F32), jax.ShapeDtypeStruct((8, T), F32)]
    return pl.pallas_call(
        functools.partial(_sb_kernel, tq=tq, tk=tk, resume=resume),
        grid_spec=pltpu.PrefetchScalarGridSpec(
            num_scalar_prefetch=3, grid=(int(qi.shape[0]),),
            in_specs=in_specs, out_specs=out_specs,
            scratch_shapes=[pltpu.VMEM((SB_HEADS, 1, tq), F32), pltpu.VMEM((SB_HEADS, HEAD_DIM, tq), F32)]),
        out_shape=out_shape,
        compiler_params=_params(("arbitrary",), 48), name="sb_far" if resume else "sb_near",
    )(qi, kj, fl, q, k, vT, lmat, *(state or ()))


def _stick_breaking(q, k, vT):
    T, W = q.shape
    tq = _tile(T, 512)
    tk = _tile(T, 256)
    nq = T // tq
    hi = lambda i: ((i + 1) * tq - 2) // tk
    lo_near = lambda i: max(0, hi(i) - SB_NEAR_TILES + 1)
    near = _steps(nq, lo_near, hi, reverse=True)
    y_near, acc, carry = _sb_call(q, k, vT, near, tq, tk)
    qi, kj, fl = [], [], []
    for i in range(nq):
        js = list(range(lo_near(i) - 1, -1, -1))
        for n, j in enumerate(js or [0]):
            qi.append(i)
            kj.append(j)
            fl.append((1 if n == 0 else 0) | (2 if n == max(len(js), 1) - 1 else 0) | (0 if js else 4))
    far = tuple(jnp.asarray(np.array(a, np.int32)) for a in (qi, kj, fl))
    with_far = [i for i in range(nq) if lo_near(i) > 0]
    if not with_far:
        return y_near
    alive = jnp.max(carry[:SB_HEADS, with_far[0] * tq:]) > SB_DEAD_LOG2
    return lax.cond(alive, lambda: _sb_call(q, k, vT, far, tq, tk, state=(acc, carry)), lambda: y_near)


def _memattn_kernel(q_ref, k_ref, vT_ref, o_ref):
    for h in range(MEM_HEADS):
        sl = slice(h * HEAD_DIM, (h + 1) * HEAD_DIM)
        sT = lax.dot_general(k_ref[:, sl], q_ref[:, sl], _NT, preferred_element_type=F32)
        e = jnp.exp2(sT - jnp.max(sT, axis=0, keepdims=True))
        l = jnp.sum(e, axis=0, keepdims=True)
        oT = jnp.dot(vT_ref[sl, :], e.astype(BF16), preferred_element_type=F32) * (1.0 / l)
        o_ref[:, sl] = oT.T.astype(BF16)


def _mem_attention(q, k, vT):
    T, W = q.shape
    M = k.shape[0]
    tq = _tile(T, 512)
    return pl.pallas_call(
        _memattn_kernel, grid=(T // tq,),
        in_specs=[pl.BlockSpec((tq, W), lambda i: (i, 0)),
                  pl.BlockSpec((M, W), lambda i: (0, 0)),
                  pl.BlockSpec((W, M), lambda i: (0, 0))],
        out_specs=pl.BlockSpec((tq, W), lambda i: (i, 0)),
        out_shape=jax.ShapeDtypeStruct((T, W), BF16),
        compiler_params=_params(("parallel",), 32), name="mem_attention",
    )(q, k, vT)


def _merge_kernel(yc_ref, ys_ref, yw_ref, ysb_ref, ym_ref, g0_ref, g1_ref, g2_ref,
                  wn_ref, wsb_ref, wm_ref, o_ref, yn_sc):
    @pl.when(pl.program_id(1) == 0)
    def _():
        yn_sc[...] = (yc_ref[...].astype(F32) + ys_ref[...].astype(F32) + yw_ref[...].astype(F32)).astype(BF16)

    a = jnp.dot(yn_sc[...], wn_ref[...], preferred_element_type=F32)
    b = jnp.dot(ysb_ref[...], wsb_ref[...], preferred_element_type=F32)
    c = jnp.dot(ym_ref[...], wm_ref[...], preferred_element_type=F32)
    sig = lambda ref: jax.nn.sigmoid(ref[...].astype(F32))
    o = sig(g0_ref) * a + sig(g1_ref) * b + sig(g2_ref) * c
    o_ref[...] = o.astype(BF16)


def _merge(yc, ys, yw, ysb, ym, P, wn, wsb, wm, gate_col0):
    T = yc.shape[0]
    D = wn.shape[1]
    tm = _tile(T, 512)
    tn = 512
    assert D % tn == 0 and gate_col0 % tn == 0
    gb = gate_col0 // tn
    nb = D // tn
    rowi = lambda w: pl.BlockSpec((tm, w), lambda i, j: (i, 0))
    gate = lambda b: pl.BlockSpec((tm, tn), lambda i, j, b=b: (i, gb + b * nb + j))
    wcol = lambda k: pl.BlockSpec((k, tn), lambda i, j: (0, j))
    return pl.pallas_call(
        _merge_kernel, grid=(T // tm, nb),
        in_specs=[rowi(yc.shape[1]), rowi(ys.shape[1]), rowi(yw.shape[1]), rowi(ysb.shape[1]), rowi(ym.shape[1]),
                  gate(0), gate(1), gate(2), wcol(wn.shape[0]), wcol(wsb.shape[0]), wcol(wm.shape[0])],
        out_specs=pl.BlockSpec((tm, tn), lambda i, j: (i, j)),
        out_shape=jax.ShapeDtypeStruct((T, D), BF16),
        scratch_shapes=[pltpu.VMEM((tm, yc.shape[1]), BF16)],
        compiler_params=_params(("parallel", "arbitrary"), 48), name="merge",
    )(yc, ys, yw, ysb, ym, P, P, P, wn, wsb, wm)


def _out_kernel(mix_ref, w_ref, x_ref, g_ref, x2_o, h2_o):
    x2 = x_ref[...] + jnp.dot(mix_ref[...], w_ref[...], preferred_element_type=F32)
    x2_o[...] = x2
    h2_o[...] = _rms(x2, g_ref[...]).astype(BF16)


def _out_proj(mixed, w, x, gain):
    T, D = x.shape
    tm = _tile(T, 256)
    return pl.pallas_call(
        _out_kernel, grid=(T // tm,),
        in_specs=[pl.BlockSpec((tm, D), lambda i: (i, 0)),
                  pl.BlockSpec((D, D), lambda i: (0, 0)),
                  pl.BlockSpec((tm, D), lambda i: (i, 0)),
                  pl.BlockSpec((1, D), lambda i: (0, 0))],
        out_specs=[pl.BlockSpec((tm, D), lambda i: (i, 0)), pl.BlockSpec((tm, D), lambda i: (i, 0))],
        out_shape=[jax.ShapeDtypeStruct((T, D), F32), jax.ShapeDtypeStruct((T, D), BF16)],
        compiler_params=_params(("parallel",), 48), name="out_proj",
    )(mixed, w, x, gain)


def _ffn_kernel(h_ref, wg_ref, wu_ref, wd_ref, x_ref, o_ref, z_sc, *, nf):
    f = pl.program_id(1)

    def up():
        a = jnp.dot(h_ref[...], wg_ref[...], preferred_element_type=F32)
        b = jnp.dot(h_ref[...], wu_ref[...], preferred_element_type=F32)
        return (a * jax.nn.sigmoid(a) * b).astype(BF16)

    def down():
        return jnp.dot(z_sc[...], wd_ref[...], preferred_element_type=F32)

    @pl.when(f == 0)
    def _():
        z_sc[...] = up()

    @pl.when((f > 0) & (f < nf))
    def _():
        c = down()
        z_new = up()
        o_ref[...] = jnp.where(f == 1, x_ref[...], o_ref[...]) + c
        z_sc[...] = z_new

    @pl.when(f == nf)
    def _():
        o_ref[...] += down()


def _ffn(h2, wg, wu, wd, x2):
    T, D = x2.shape
    F = wg.shape[1]
    tm = _tile(T, 1024)
    tf = 512
    assert F % tf == 0
    nf = F // tf
    assert nf >= 2
    return pl.pallas_call(
        functools.partial(_ffn_kernel, nf=nf), grid=(T // tm, nf + 1),
        in_specs=[pl.BlockSpec((tm, D), lambda i, f: (i, 0)),
                  pl.BlockSpec((D, tf), lambda i, f: (0, jnp.minimum(f, nf - 1))),
                  pl.BlockSpec((D, tf), lambda i, f: (0, jnp.minimum(f, nf - 1))),
                  pl.BlockSpec((tf, D), lambda i, f: (jnp.maximum(f - 1, 0), 0)),
                  pl.BlockSpec((tm, D), lambda i, f: (i, 0), pipeline_mode=pl.Buffered(1))],
        out_specs=pl.BlockSpec((tm, D), lambda i, f: (i, 0)),
        out_shape=jax.ShapeDtypeStruct((T, D), F32),
        scratch_shapes=[pltpu.VMEM((tm, tf), BF16)],
        compiler_params=_params(("parallel", "arbitrary"), 56), name="ffn",
    )(h2, wg.astype(BF16), wu.astype(BF16), wd.astype(BF16), x2)


def _layer(x, mem, pos_col, posc_col, consts, attn_norm, w_in, nsa_q_norm, nsa_kc_norm, nsa_ks_norm, nsa_kw_norm,
           cmp_k_pe, cmp_k_w1, cmp_k_w2, cmp_v_pe, cmp_v_w1, cmp_v_w2, mem_norm, w_mem_kv,
           mem_q_norm, mem_k_norm, w_o_nsa, w_o_sb, w_o_mem, w_out, ffn_norm,
           w_ffn_gate, w_ffn_up, w_ffn_down):
    T, D = x.shape
    inv2, sgn, ovT = consts
    hd = HEAD_DIM
    row = lambda g: g.reshape(1, -1)

    q_w, kv_w, gn_w = NSA_HEADS * hd, 6 * NSA_GROUPS * hd, 3 * NSA_HEADS
    sb_w, mq_w, gm_w = 3 * SB_HEADS * hd, MEM_HEADS * hd, N_BRANCH * D
    o_gn = q_w + kv_w
    o_sb = o_gn + gn_w
    o_mq = o_sb + sb_w
    o_gm = o_mq + mq_w
    assert w_in.shape[1] == o_gm + gm_w
    w_main = jnp.concatenate([w_in[:, :o_gn], w_in[:, o_sb:]], axis=1).astype(BF16)
    w_gate = jnp.pad(w_in[:, o_gn:o_sb], ((0, 0), (0, LANE - gn_w))).astype(BF16)
    gate_col0 = q_w + kv_w + sb_w + mq_w
    P, Pg = _in_proj(x, row(attn_norm), w_main, w_gate)

    (qn, kc_raw, vc_raw, ksn, kwn, vsT, vwT, sbq, sbk, sbvT, memq, gT, qn2, kn2) = _prep(
        P, Pg, pos_col, inv2, sgn, row(nsa_q_norm), row(nsa_ks_norm), row(nsa_kw_norm), row(mem_q_norm))

    q2 = jnp.max(qn2[::8].reshape(NSA_GROUPS, NSA_HPG, T), axis=1)
    k2 = jnp.max(kn2[::8], axis=1).reshape(2, NSA_GROUPS)
    bound = lambda kk: jnp.broadcast_to((1.02 * jnp.sqrt(q2 * kk[:, None]))[:, None, :], (NSA_GROUPS, 8, T))
    shift_sel, shift_win = bound(k2[0]), bound(k2[1])

    gates = gT[:gn_w].reshape(NSA_GROUPS, NSA_HPG, 3, T).transpose(2, 0, 1, 3)
    gates = jnp.pad(gates, ((0, 0), (0, 0), (0, 8 - NSA_HPG), (0, 0)))

    n_pad = T // CMP_STRIDE
    half = CMP_LEN // 2

    def w1_pack(w1):
        return jnp.concatenate([w1[:half].reshape(half * hd, -1), w1[half:].reshape(half * hd, -1)], axis=1).astype(BF16)

    def pe_pack(pe):
        return jnp.pad(pe.reshape(2, half * hd), ((0, 6), (0, 0))).astype(BF16)

    kc, vcT = _compress(
        kc_raw.reshape(NSA_GROUPS, n_pad, CMP_STRIDE * hd), vc_raw.reshape(NSA_GROUPS, n_pad, CMP_STRIDE * hd),
        w1_pack(cmp_k_w1), w1_pack(cmp_v_w1), pe_pack(cmp_k_pe), pe_pack(cmp_v_pe),
        cmp_k_w2.astype(BF16), cmp_v_w2.astype(BF16), row(nsa_kc_norm), posc_col, inv2, sgn)

    y_cmp, bias = _cmp_select(qn, kc, vcT, ovT, gates)
    y_sel = _nsa_flash("sel", qn, ksn, vsT, gates, shift_sel, bias)
    y_win = _nsa_flash("win", qn, kwn, vwT, gates, shift_win)
    y_sb = _stick_breaking(sbq, sbk, sbvT)

    mk, mvT = _memkv(mem, row(mem_norm), w_mem_kv.astype(BF16), row(mem_k_norm))
    y_mem = _mem_attention(memq, mk, mvT)

    mixed = _merge(y_cmp, y_sel, y_win, y_sb, y_mem, P,
                   w_o_nsa.astype(BF16), w_o_sb.astype(BF16), w_o_mem.astype(BF16), gate_col0)
    x2, h2 = _out_proj(mixed, w_out.astype(BF16), x, row(ffn_norm))
    return _ffn(h2, w_ffn_gate, w_ffn_up, w_ffn_down, x2)


def kernel(x, mem, positions, attn_norm, w_in, nsa_q_norm, nsa_kc_norm, nsa_ks_norm, nsa_kw_norm, cmp_k_pe, cmp_k_w1, cmp_k_w2, cmp_v_pe, cmp_v_w1, cmp_v_w2, mem_norm, w_mem_kv, mem_q_norm, mem_k_norm, w_o_nsa, w_o_sb, w_o_mem, w_out, ffn_norm, w_ffn_gate, w_ffn_up, w_ffn_down):
    B, T, D = x.shape
    assert T % (4 * LANE) == 0 and T // SEL_BLOCK >= 8
    n_pad = T // CMP_STRIDE
    n_sel = T // SEL_BLOCK
    inv = 1.0 / (ROPE_THETA ** (jnp.arange(0, HEAD_DIM, 2, dtype=F32) / HEAD_DIM))
    inv2 = jnp.concatenate([inv, inv]).reshape(1, HEAD_DIM)
    sgn = jnp.concatenate([-jnp.ones((HEAD_DIM // 2,), F32), jnp.ones((HEAD_DIM // 2,), F32)]).reshape(1, HEAD_DIM)
    cs = np.arange(n_pad)[None, :] * CMP_STRIDE
    ss = np.arange(n_sel)[:, None] * SEL_BLOCK
    ovT = jnp.asarray(((cs < ss + SEL_BLOCK) & (cs + CMP_LEN - 1 >= ss)).astype(np.float32), BF16)
    consts = (inv2, sgn, ovT)
    depth = w_in.shape[0]
    outs = []
    for b in range(B):
        xb = x[b]
        posf = positions[b].astype(F32)
        pos_col = posf.reshape(T, 1)
        posc = jnp.concatenate([posf[CMP_LEN - 1::CMP_STRIDE], posf[-1:]]).reshape(n_pad, 1)
        for l in range(depth):
            xb = _layer(xb, mem[b], pos_col, posc, consts, attn_norm[l], w_in[l], nsa_q_norm[l], nsa_kc_norm[l],
                        nsa_ks_norm[l], nsa_kw_norm[l], cmp_k_pe[l], cmp_k_w1[l], cmp_k_w2[l], cmp_v_pe[l],
                        cmp_v_w1[l], cmp_v_w2[l], mem_norm[l], w_mem_kv[l], mem_q_norm[l], mem_k_norm[l],
                        w_o_nsa[l], w_o_sb[l], w_o_mem[l], w_out[l], ffn_norm[l],
                        w_ffn_gate[l], w_ffn_up[l], w_ffn_down[l])
        outs.append(xb)
    return outs[0][None] if B == 1 else jnp.stack(outs, axis=0)
```

```python
import functools

import numpy as np
import jax
import jax.numpy as jnp
from jax import lax
from jax.experimental import pallas as pl
from jax.experimental.pallas import tpu as pltpu

HEAD_DIM = 128
NSA_HEADS = 8
NSA_GROUPS = 2
NSA_HPG = NSA_HEADS // NSA_GROUPS
SB_HEADS = 4
MEM_HEADS = 4
CMP_LEN = 32
CMP_STRIDE = 16
CMP_HIDDEN = 2 * HEAD_DIM
SEL_BLOCK = 64
SEL_TOPK = 16
WINDOW = 512
ROPE_THETA = 10000.0
NORM_EPS = 1e-6
NEG_BIG = -1e30
N_BRANCH = 3
SCALE = HEAD_DIM ** -0.5
LOG2E = 1.4426950408889634
QSCALE = SCALE * LOG2E
SOFTMAX_BOUND_MAX = 50.0
CMP_BUCKETS = 8
SB_NEAR_TILES = 3
SB_DEAD_LOG2 = -160.0
VAUG = HEAD_DIM + 16

LANE = 128
MIB = 1 << 20
BF16 = jnp.bfloat16
F32 = jnp.float32

_NT = (((1,), (1,)), ((), ()))


def _tile(n, pref):
    t = min(n, pref)
    assert n % t == 0, (n, pref)
    return t


def _params(sem, vmem_mib):
    return pltpu.CompilerParams(dimension_semantics=sem, vmem_limit_bytes=vmem_mib * MIB)


def _rms(x, gain):
    return x * lax.rsqrt(jnp.mean(x * x, axis=-1, keepdims=True) + NORM_EPS) * gain


def _rope_tables(pos, inv2, sgn):
    ang = pos * inv2
    return jnp.cos(ang), jnp.sin(ang) * sgn


def _rope(x, c, s):
    return x * c + pltpu.roll(x, HEAD_DIM // 2, 1) * s


def _proj_kernel(x_ref, g_ref, w_ref, wg_ref, o_ref, og_ref, hn_ref):
    @pl.when(pl.program_id(1) == 0)
    def _():
        hn_ref[...] = _rms(x_ref[...], g_ref[...]).astype(BF16)
        og_ref[...] = jnp.dot(hn_ref[...], wg_ref[...], preferred_element_type=F32)

    o_ref[...] = jnp.dot(hn_ref[...], w_ref[...], preferred_element_type=F32).astype(BF16)


def _in_proj(x, gain, w_main, w_gate):
    T, D = x.shape
    N = w_main.shape[1]
    tm = _tile(T, 1024)
    tn = 1536
    assert N % tn == 0
    return pl.pallas_call(
        _proj_kernel,
        grid=(T // tm, N // tn),
        in_specs=[
            pl.BlockSpec((tm, D), lambda i, j: (i, 0)),
            pl.BlockSpec((1, D), lambda i, j: (0, 0)),
            pl.BlockSpec((D, tn), lambda i, j: (0, j)),
            pl.BlockSpec((D, LANE), lambda i, j: (0, 0)),
        ],
        out_specs=[pl.BlockSpec((tm, tn), lambda i, j: (i, j)), pl.BlockSpec((tm, LANE), lambda i, j: (i, 0))],
        out_shape=[jax.ShapeDtypeStruct((T, N), BF16), jax.ShapeDtypeStruct((T, LANE), F32)],
        scratch_shapes=[pltpu.VMEM((tm, D), BF16)],
        compiler_params=_params(("parallel", "arbitrary"), 52),
        name="in_proj",
    )(x, gain, w_main, w_gate)


def _prep_kernel(pos_ref, inv_ref, sgn_ref, gq_ref, gks_ref, gkw_ref, gmq_ref,
                 q_ref, kc_ref, vc_ref, ks_ref, vs_ref, kw_ref, vw_ref,
                 sq_ref, sk_ref, sv_ref, mq_ref, gn_ref,
                 qn_o, kc_o, vc_o, ks_o, kw_o, vsT_o, vwT_o, sq_o, sk_o, svT_o, mq_o, gT_o, qn2_o, kn2_o):
    c, s = _rope_tables(pos_ref[...], inv_ref[...], sgn_ref[...])
    hd = HEAD_DIM
    tp = pos_ref.shape[0]
    f32 = lambda ref, sl: ref[:, sl].astype(F32)
    ones8 = jnp.ones((8, hd), F32)

    def sqnorm_rows(xb):
        x = xb.astype(F32)
        return lax.dot_general(ones8, x * x, _NT, preferred_element_type=F32)

    for h in range(NSA_HEADS):
        sl = slice(h * hd, (h + 1) * hd)
        qb = (_rope(_rms(f32(q_ref, sl), gq_ref[...]), c, s) * QSCALE).astype(BF16)
        qn_o[:, sl] = qb
        qn2_o[h * 8:(h + 1) * 8, :] = sqnorm_rows(qb)
    ones_rows = (lax.broadcasted_iota(jnp.int32, (VAUG - hd, tp), 0) == 0).astype(F32).astype(BF16)
    for g in range(NSA_GROUPS):
        sl = slice(g * hd, (g + 1) * hd)
        kc_o[g] = kc_ref[:, sl]
        vc_o[g] = vc_ref[:, sl]
        ksb = _rope(_rms(f32(ks_ref, sl), gks_ref[...]), c, s).astype(BF16)
        kwb = _rope(_rms(f32(kw_ref, sl), gkw_ref[...]), c, s).astype(BF16)
        ks_o[:, sl] = ksb
        kw_o[:, sl] = kwb
        kn2_o[g * 8:(g + 1) * 8, :] = sqnorm_rows(ksb)
        kn2_o[(NSA_GROUPS + g) * 8:(NSA_GROUPS + g + 1) * 8, :] = sqnorm_rows(kwb)
        vsT_o[g * VAUG:g * VAUG + hd, :] = f32(vs_ref, sl).T.astype(BF16)
        vsT_o[g * VAUG + hd:(g + 1) * VAUG, :] = ones_rows
        vwT_o[g * VAUG:g * VAUG + hd, :] = f32(vw_ref, sl).T.astype(BF16)
        vwT_o[g * VAUG + hd:(g + 1) * VAUG, :] = ones_rows
    for h in range(SB_HEADS):
        sl = slice(h * hd, (h + 1) * hd)
        sq_o[:, sl] = (f32(sq_ref, sl) * QSCALE).astype(BF16)
        svT_o[sl, :] = f32(sv_ref, sl).T.astype(BF16)
    sk_o[...] = sk_ref[...]
    for h in range(MEM_HEADS):
        sl = slice(h * hd, (h + 1) * hd)
        mq_o[:, sl] = (_rms(f32(mq_ref, sl), gmq_ref[...]) * QSCALE).astype(BF16)
    gT_o[...] = jax.nn.sigmoid(gn_ref[...]).T


def _prep(P, Pg, c0, pos_col, inv2, sgn, gq, gks, gkw, gmq):
    T = P.shape[0]
    tp = _tile(T, 512)
    hd = HEAD_DIM
    row = lambda w, c: pl.BlockSpec((tp, w), lambda i, c=c: (i, c))
    const = lambda: pl.BlockSpec((1, hd), lambda i: (0, 0))
    assert c0 % (8 * hd) == 0
    at = lambda w, col: row(w, (c0 + col) // w)
    in_specs = [pl.BlockSpec((tp, 1), lambda i: (i, 0)), const(), const(), const(), const(), const(), const(),
                at(8 * hd, 0),
                at(2 * hd, 8 * hd), at(2 * hd, 10 * hd),
                at(2 * hd, 12 * hd), at(2 * hd, 14 * hd),
                at(2 * hd, 16 * hd), at(2 * hd, 18 * hd),
                at(4 * hd, 20 * hd), at(4 * hd, 24 * hd), at(4 * hd, 28 * hd),
                at(4 * hd, 32 * hd),
                row(hd, 0)]
    colT = lambda w: pl.BlockSpec((w, tp), lambda i: (0, i))
    grp = lambda w: pl.BlockSpec((NSA_GROUPS, tp, w), lambda i: (0, i, 0))
    out_specs = [row(8 * hd, 0), grp(hd), grp(hd), row(2 * hd, 0),
                 row(2 * hd, 0), colT(NSA_GROUPS * VAUG), colT(NSA_GROUPS * VAUG),
                 row(4 * hd, 0), row(4 * hd, 0), colT(4 * hd), row(4 * hd, 0), colT(hd),
                 colT(NSA_HEADS * 8), colT(2 * NSA_GROUPS * 8)]
    sds = jax.ShapeDtypeStruct
    out_shape = [sds((T, 8 * hd), BF16),
                 sds((NSA_GROUPS, T, hd), BF16), sds((NSA_GROUPS, T, hd), BF16), sds((T, 2 * hd), BF16),
                 sds((T, 2 * hd), BF16),
                 sds((NSA_GROUPS * VAUG, T), BF16), sds((NSA_GROUPS * VAUG, T), BF16),
                 sds((T, 4 * hd), BF16), sds((T, 4 * hd), BF16), sds((4 * hd, T), BF16), sds((T, 4 * hd), BF16),
                 sds((hd, T), F32), sds((NSA_HEADS * 8, T), F32), sds((2 * NSA_GROUPS * 8, T), F32)]
    return pl.pallas_call(
        _prep_kernel, grid=(T // tp,), in_specs=in_specs, out_specs=out_specs, out_shape=out_shape,
        compiler_params=_params(("parallel",), 48), name="prep",
    )(pos_col, inv2, sgn, gq, gks, gkw, gmq, *([P] * 11), Pg)


def _gelu_tanh(x):
    return 0.5 * x * (1.0 + jnp.tanh(0.7978845608028654 * (x + 0.044715 * (x * x * x))))


def _compress_one(x, w1, pe, w2):
    n = x.shape[0]
    ab = jnp.dot(x, w1, preferred_element_type=F32)
    pr = jnp.dot(pe, w1, preferred_element_type=F32)
    pec = pr[0:1, :CMP_HIDDEN] + pr[1:2, CMP_HIDDEN:]
    hid = ab[:, :CMP_HIDDEN] + pltpu.roll(ab[:, CMP_HIDDEN:], n - 1, 0) + pec
    return jnp.dot(_gelu_tanh(hid).astype(BF16), w2, preferred_element_type=F32)


def _compress_kernel(xk_ref, xv_ref, w1k_ref, w1v_ref, pek_ref, pev_ref, w2k_ref, w2v_ref,
                     gk_ref, pos_ref, inv_ref, sgn_ref, kc_o, vcT_o):
    c, s = _rope_tables(pos_ref[...], inv_ref[...], sgn_ref[...])
    k = _compress_one(xk_ref[0], w1k_ref[...], pek_ref[...], w2k_ref[...])
    kc_o[0] = _rope(_rms(k, gk_ref[...]), c, s).astype(BF16)
    v = _compress_one(xv_ref[0], w1v_ref[...], pev_ref[...], w2v_ref[...])
    vcT_o[0] = v.T.astype(BF16)


def _compress(xk, xv, w1k, w1v, pek, pev, w2k, w2v, gk, posc, inv2, sgn):
    G, n, W = xk.shape
    full = lambda a: pl.BlockSpec(a.shape, lambda g: (0,) * a.ndim)
    grp = pl.BlockSpec((1, n, W), lambda g: (g, 0, 0))
    return pl.pallas_call(
        _compress_kernel, grid=(G,),
        in_specs=[grp, grp, full(w1k), full(w1v), full(pek), full(pev), full(w2k), full(w2v),
                  full(gk), full(posc), full(inv2), full(sgn)],
        out_specs=[pl.BlockSpec((1, n, HEAD_DIM), lambda g: (g, 0, 0)),
                   pl.BlockSpec((1, HEAD_DIM, n), lambda g: (g, 0, 0))],
        out_shape=[jax.ShapeDtypeStruct((G, n, HEAD_DIM), BF16), jax.ShapeDtypeStruct((G, HEAD_DIM, n), BF16)],
        compiler_params=_params(("parallel",), 48), name="compress",
    )(xk, xv, w1k, w1v, pek, pev, w2k, w2v, gk, posc, inv2, sgn)


def _memkv_kernel(mem_ref, gm_ref, w_ref, gk_ref, k_o, vT_o):
    hn = _rms(mem_ref[...], gm_ref[...]).astype(BF16)
    kv = jnp.dot(hn, w_ref[...], preferred_element_type=F32)
    mw = MEM_HEADS * HEAD_DIM
    for h in range(MEM_HEADS):
        sl = slice(h * HEAD_DIM, (h + 1) * HEAD_DIM)
        k_o[:, sl] = _rms(kv[:, sl], gk_ref[...]).astype(BF16)
        vT_o[sl, :] = kv[:, mw + h * HEAD_DIM: mw + (h + 1) * HEAD_DIM].T.astype(BF16)


def _memkv(mem, gm, w, gk):
    M = mem.shape[0]
    mw = MEM_HEADS * HEAD_DIM
    return pl.pallas_call(
        _memkv_kernel,
        out_shape=[jax.ShapeDtypeStruct((M, mw), BF16), jax.ShapeDtypeStruct((mw, M), BF16)],
        compiler_params=pltpu.CompilerParams(vmem_limit_bytes=32 * MIB), name="mem_kv",
    )(mem, gm, w, gk)


def _cmp_kernel(q_ref, kc_ref, vcT_ref, ovT_ref, g_ref, y_in, b_in, y_ref, b_ref, *, tq, i0, n_pad, n_sel, top_k):
    del y_in, b_in
    i = i0 + pl.program_id(1)
    t = i * tq + lax.broadcasted_iota(jnp.int32, (1, tq), 1)
    n_end = lax.broadcasted_iota(jnp.int32, (n_pad, 1), 0) * CMP_STRIDE + (CMP_LEN - 1)
    valid = n_end <= t
    kc = kc_ref[0]
    vcT = vcT_ref[0]
    has_valid = (t >= CMP_LEN - 1).astype(F32)
    sms = []
    for h in range(NSA_HPG):
        q_h = q_ref[:, h * HEAD_DIM:(h + 1) * HEAD_DIM]
        sms.append(jnp.where(valid, lax.dot_general(kc, q_h, _NT, preferred_element_type=F32), NEG_BIG))
    psum = jnp.zeros((n_pad, tq), F32)
    for h in range(NSA_HPG):
        e = jnp.exp2(sms[h] - jnp.max(sms[h], axis=0, keepdims=True))
        p = e * (has_valid / jnp.sum(e, axis=0, keepdims=True))
        oT = jnp.dot(vcT, p.astype(BF16), preferred_element_type=F32)
        y_ref[:, h * HEAD_DIM:(h + 1) * HEAD_DIM] = (oT * g_ref[0, 0, h:h + 1, :]).T.astype(BF16)
        psum = psum + p
    hi = psum.astype(BF16)
    r1 = psum - hi.astype(F32)
    mid = r1.astype(BF16)
    lo = (r1 - mid.astype(F32)).astype(BF16)
    parts = jnp.dot(ovT_ref[...], jnp.concatenate([hi, mid, lo], axis=1), preferred_element_type=F32)
    imp = parts[:, :tq] + parts[:, tq:2 * tq] + parts[:, 2 * tq:]
    s_i = lax.broadcasted_iota(jnp.int32, (n_sel, 1), 0)
    cur = lax.shift_right_logical(t, 6)
    forced = (s_i == 0) | (s_i == cur) | (s_i == cur - 1)
    future = s_i * SEL_BLOCK > t
    w = jnp.where(forced, jnp.inf, jnp.where(future, -jnp.inf, imp))
    s_f = jnp.broadcast_to(s_i.astype(F32), (n_sel, tq))
    for _ in range(top_k):
        m = jnp.max(w, axis=0, keepdims=True)
        idx = jnp.min(jnp.where(w == m, s_f, float(n_sel)), axis=0, keepdims=True)
        w = jnp.where(s_f == idx, -jnp.inf, w)
    b_ref[0] = jnp.where(future, NEG_BIG, jnp.where(w == -jnp.inf, 0.0, NEG_BIG))


def _round_up(n, m):
    return -(-n // m) * m


def _cmp_select(qn, kc, vcT, ovT, gates):
    T = qn.shape[0]
    G, n_pad, _ = kc.shape
    n_sel = T // SEL_BLOCK
    tq = _tile(T, 256)
    nq = T // tq
    gw = NSA_HPG * HEAD_DIM
    nb = min(CMP_BUCKETS, nq)
    assert nq % nb == 0
    per = nq // nb
    y = jnp.zeros((T, NSA_HEADS * HEAD_DIM), BF16)
    bias = jnp.full((G, n_sel, T), NEG_BIG, F32)
    for b in range(nb):
        i0 = b * per
        t_max = (i0 + per) * tq - 1
        n_len = min(n_pad, _round_up(max(t_max - (CMP_LEN - 1), 0) // CMP_STRIDE + 1, LANE))
        s_len = min(n_sel, _round_up(t_max // SEL_BLOCK + 1, 8))
        kern = functools.partial(_cmp_kernel, tq=tq, i0=i0, n_pad=n_len, n_sel=s_len, top_k=min(SEL_TOPK, n_sel))
        y, bias = pl.pallas_call(
            kern, grid=(G, per),
            in_specs=[pl.BlockSpec((tq, gw), lambda g, i, i0=i0: (i0 + i, g)),
                      pl.BlockSpec((1, n_len, HEAD_DIM), lambda g, i: (g, 0, 0)),
                      pl.BlockSpec((1, HEAD_DIM, n_len), lambda g, i: (g, 0, 0)),
                      pl.BlockSpec((s_len, n_len), lambda g, i: (0, 0)),
                      pl.BlockSpec((1, 1, 8, tq), lambda g, i, i0=i0: (0, g, 0, i0 + i)),
                      pl.BlockSpec(memory_space=pl.ANY), pl.BlockSpec(memory_space=pl.ANY)],
            out_specs=[pl.BlockSpec((tq, gw), lambda g, i, i0=i0: (i0 + i, g)),
                       pl.BlockSpec((1, s_len, tq), lambda g, i, i0=i0: (g, 0, i0 + i))],
            out_shape=[jax.ShapeDtypeStruct(y.shape, y.dtype), jax.ShapeDtypeStruct(bias.shape, bias.dtype)],
            input_output_aliases={5: 0, 6: 1},
            compiler_params=_params(("parallel", "parallel"), 48), name="cmp_select_%d" % b,
        )(qn, kc, vcT, ovT, gates, y, bias)
    return y, bias


def _flash_kernel(qi_ref, kj_ref, fl_ref, q_ref, k_ref, vT_ref, g_ref, sh_ref, *rest, mode, bounded, tq, tk):
    if mode == "sel":
        bias_ref, o_ref, m_sc, acc_sc = rest
    else:
        o_ref, m_sc, acc_sc = rest
    p = pl.program_id(1)
    i = qi_ref[p]
    j = kj_ref[p]
    fl = fl_ref[p]

    @pl.when((fl & 1) != 0)
    def _():
        m_sc[...] = jnp.full(m_sc.shape, NEG_BIG, F32)
        acc_sc[...] = jnp.zeros(acc_sc.shape, F32)

    t = i * tq + lax.broadcasted_iota(jnp.int32, (1, tq), 1)
    kpos = j * tk + lax.broadcasted_iota(jnp.int32, (tk, 1), 0)
    shift = sh_ref[0, 0:1, :] if bounded else 0.0
    if mode == "sel":
        nb = tk // SEL_BLOCK
        rows = bias_ref[0] - shift
        base = jnp.concatenate(
            [jnp.broadcast_to(rows[b:b + 1, :], (SEL_BLOCK, tq)) for b in range(nb)], axis=0)
        mask_add = jnp.where(kpos <= t, base, NEG_BIG)
    else:
        mask_add = jnp.where((kpos <= t) & (kpos > t - WINDOW), 0.0 - shift, NEG_BIG)
    k = k_ref[...]
    vT = vT_ref[...]
    scores = lambda h: lax.dot_general(k, q_ref[:, h * HEAD_DIM:(h + 1) * HEAD_DIM], _NT,
                                       preferred_element_type=F32) + mask_add
    if bounded:
        pTs = [jnp.exp2(scores(h)).astype(BF16) for h in range(NSA_HPG)]
        for h in range(NSA_HPG):
            acc_sc[h] = acc_sc[h] + jnp.dot(vT, pTs[h], preferred_element_type=F32)
    else:
        sTs = [scores(h) for h in range(NSA_HPG)]
        pTs, alphas = [], []
        for h in range(NSA_HPG):
            sT = sTs[h]
            m_old = m_sc[h]
            m_new = jnp.maximum(m_old, jnp.max(sT, axis=0, keepdims=True))
            alphas.append(jnp.exp2(m_old - m_new))
            pTs.append(jnp.exp2((sT - m_new).astype(BF16)))
            m_sc[h] = m_new
        for h in range(NSA_HPG):
            acc_sc[h] = alphas[h] * acc_sc[h] + jnp.dot(vT, pTs[h], preferred_element_type=F32)

    @pl.when((fl & 2) != 0)
    def _():
        for h in range(NSA_HPG):
            acc = acc_sc[h]
            l = acc[HEAD_DIM:HEAD_DIM + 1, :]
            o = acc[:HEAD_DIM, :] * ((1.0 / l) * g_ref[0, 0, h:h + 1, :])
            o_ref[:, h * HEAD_DIM:(h + 1) * HEAD_DIM] = o.T.astype(BF16)


def _steps(nq, lo_fn, hi_fn, reverse=False):
    qi, kj, fl = [], [], []
    for i in range(nq):
        js = list(range(lo_fn(i), hi_fn(i) + 1))
        if reverse:
            js = js[::-1]
        for n, j in enumerate(js):
            qi.append(i)
            kj.append(j)
            fl.append((1 if n == 0 else 0) | (2 if n == len(js) - 1 else 0))
    return (jnp.asarray(np.array(qi, np.int32)), jnp.asarray(np.array(kj, np.int32)),
            jnp.asarray(np.array(fl, np.int32)))


def _nsa_flash(mode, qn, k, vT, gates, shift, bias=None):
    use_bounded = jnp.max(shift) <= SOFTMAX_BOUND_MAX
    return lax.cond(use_bounded,
                    lambda: _nsa_flash_call(mode, True, qn, k, vT, gates, shift, bias),
                    lambda: _nsa_flash_call(mode, False, qn, k, vT, gates, shift, bias))


def _nsa_flash_call(mode, bounded, qn, k, vT, gates, shift, bias):
    T = qn.shape[0]
    tq = _tile(T, 1024)
    tk = _tile(T, 512)
    nq = T // tq
    gw = NSA_HPG * HEAD_DIM
    hi = lambda i: ((i + 1) * tq - 1) // tk
    if mode == "sel":
        lo = lambda i: 0
        br = 1
    else:
        lo = lambda i: max(0, (i * tq - (WINDOW - 1)) // tk)
        br = 2
    qi, kj, fl = _steps(nq, lo, hi)
    in_specs = [pl.BlockSpec((tq, gw), lambda g, p, qi, kj, fl: (qi[p], g)),
                pl.BlockSpec((tk, HEAD_DIM), lambda g, p, qi, kj, fl: (kj[p], g)),
                pl.BlockSpec((VAUG, tk), lambda g, p, qi, kj, fl: (g, kj[p])),
                pl.BlockSpec((1, 1, 8, tq), lambda g, p, qi, kj, fl, br=br: (br, g, 0, qi[p])),
                pl.BlockSpec((1, 8, tq), lambda g, p, qi, kj, fl: (g, 0, qi[p]))]
    args = [qn, k, vT, gates, shift]
    if mode == "sel":
        in_specs.append(pl.BlockSpec((1, tk // SEL_BLOCK, tq), lambda g, p, qi, kj, fl: (g, kj[p], qi[p])))
        args.append(bias)
    kern = functools.partial(_flash_kernel, mode=mode, bounded=bounded, tq=tq, tk=tk)
    return pl.pallas_call(
        kern,
        grid_spec=pltpu.PrefetchScalarGridSpec(
            num_scalar_prefetch=3, grid=(NSA_GROUPS, int(qi.shape[0])),
            in_specs=in_specs,
            out_specs=pl.BlockSpec((tq, gw), lambda g, p, qi, kj, fl: (qi[p], g)),
            scratch_shapes=[pltpu.VMEM((NSA_HPG, 1, tq), F32), pltpu.VMEM((NSA_HPG, VAUG, tq), F32)]),
        out_shape=jax.ShapeDtypeStruct((T, NSA_HEADS * HEAD_DIM), BF16),
        compiler_params=_params(("parallel", "arbitrary"), 48),
        name="nsa_" + mode + ("_bounded" if bounded else "_online"),
    )(qi, kj, fl, *args)


def _sb_kernel(qi_ref, kj_ref, fl_ref, q_ref, k_ref, vT_ref, lm_ref, *rest, tq, tk, resume):
    if resume:
        acc_in, carry_in, o_ref, carry_sc, acc_sc = rest
    else:
        o_ref, acc_o, carry_o, carry_sc, acc_sc = rest
    p = pl.program_id(0)
    i = qi_ref[p]
    j = kj_ref[p]
    fl = fl_ref[p]

    @pl.when((fl & 1) != 0)
    def _():
        if resume:
            for h in range(SB_HEADS):
                carry_sc[h] = carry_in[h:h + 1, :]
                acc_sc[h] = acc_in[h * HEAD_DIM:(h + 1) * HEAD_DIM, :]
        else:
            carry_sc[...] = jnp.zeros(carry_sc.shape, F32)
            acc_sc[...] = jnp.zeros(acc_sc.shape, F32)

    @pl.when((jnp.max(carry_sc[...]) > SB_DEAD_LOG2) & ((fl & 4) == 0))
    def _():
        t = i * tq + lax.broadcasted_iota(jnp.int32, (1, tq), 1)
        kpos = j * tk + lax.broadcasted_iota(jnp.int32, (tk, 1), 0)
        past = kpos < t
        lm = lm_ref[...]
        hs = [slice(h * HEAD_DIM, (h + 1) * HEAD_DIM) for h in range(SB_HEADS)]
        zs = [lax.dot_general(k_ref[:, sl], q_ref[:, sl], _NT, preferred_element_type=F32) for sl in hs]
        lgs, css = [], []
        for h in range(SB_HEADS):
            z = zs[h]
            sp = jnp.maximum(z, 0.0) + jnp.log2(1.0 + jnp.exp2(-jnp.abs(z)))
            lk = jnp.where(past, -sp, 0.0)
            hi = lk.astype(BF16)
            lo = (lk - hi.astype(F32)).astype(BF16)
            css.append(jnp.dot(lm, jnp.concatenate([hi, lo], axis=1), preferred_element_type=F32))
            lgs.append(z - sp)
            carry_old = carry_sc[h]
            carry_sc[h] = carry_old + jnp.sum(lk, axis=0, keepdims=True)
            css[h] = css[h][:, :tq] + css[h][:, tq:] + carry_old
        for h in range(SB_HEADS):
            wgt = jnp.where(past, jnp.exp2(lgs[h] + css[h]), 0.0)
            acc_sc[h] = acc_sc[h] + jnp.dot(vT_ref[hs[h], :], wgt.astype(BF16), preferred_element_type=F32)

    @pl.when((fl & 2) != 0)
    def _():
        for h in range(SB_HEADS):
            o_ref[:, h * HEAD_DIM:(h + 1) * HEAD_DIM] = acc_sc[h].T.astype(BF16)
            if not resume:
                acc_o[h * HEAD_DIM:(h + 1) * HEAD_DIM, :] = acc_sc[h]
                carry_o[h:h + 1, :] = carry_sc[h]
        if not resume:
            carry_o[SB_HEADS:, :] = jnp.zeros((8 - SB_HEADS, tq), F32)


def _sb_call(q, k, vT, steps, tq, tk, state=None):
    T, W = q.shape
    qi, kj, fl = steps
    lmat = jnp.asarray(np.triu(np.ones((tk, tk), np.float32), 1), BF16)
    resume = state is not None
    qtile = lambda shape: pl.BlockSpec(shape, lambda p, qi, kj, fl: (qi[p], 0))
    qtileT = lambda rows: pl.BlockSpec((rows, tq), lambda p, qi, kj, fl: (0, qi[p]))
    in_specs = [qtile((tq, W)),
                pl.BlockSpec((tk, W), lambda p, qi, kj, fl: (kj[p], 0)),
                pl.BlockSpec((W, tk), lambda p, qi, kj, fl: (0, kj[p])),
                pl.BlockSpec((tk, tk), lambda p, qi, kj, fl: (0, 0))]
    y_sds = jax.ShapeDtypeStruct((T, W), BF16)
    if resume:
        in_specs += [qtileT(W), qtileT(8)]
        out_specs, out_shape = qtile((tq, W)), y_sds
    else:
        out_specs = [qtile((tq, W)), qtileT(W), qtileT(8)]
        out_shape = [y_sds, jax.ShapeDtypeStruct((W, T), F32), jax.ShapeDtypeStruct((8, T), F32)]
    return pl.pallas_call(
        functools.partial(_sb_kernel, tq=tq, tk=tk, resume=resume),
        grid_spec=pltpu.PrefetchScalarGridSpec(
            num_scalar_prefetch=3, grid=(int(qi.shape[0]),),
            in_specs=in_specs, out_specs=out_specs,
            scratch_shapes=[pltpu.VMEM((SB_HEADS, 1, tq), F32), pltpu.VMEM((SB_HEADS, HEAD_DIM, tq), F32)]),
        out_shape=out_shape,
        compiler_params=_params(("arbitrary",), 48), name="sb_far" if resume else "sb_near",
    )(qi, kj, fl, q, k, vT, lmat, *(state or ()))


def _stick_breaking(q, k, vT):
    T, W = q.shape
    tq = _tile(T, 512)
    tk = _tile(T, 256)
    nq = T // tq
    hi = lambda i: ((i + 1) * tq - 2) // tk
    lo_near = lambda i: max(0, hi(i) - SB_NEAR_TILES + 1)
    near = _steps(nq, lo_near, hi, reverse=True)
    y_near, acc, carry = _sb_call(q, k, vT, near, tq, tk)
    qi, kj, fl = [], [], []
    for i in range(nq):
        js = list(range(lo_near(i) - 1, -1, -1))
        for n, j in enumerate(js or [0]):
            qi.append(i)
            kj.append(j)
            fl.append((1 if n == 0 else 0) | (2 if n == max(len(js), 1) - 1 else 0) | (0 if js else 4))
    far = tuple(jnp.asarray(np.array(a, np.int32)) for a in (qi, kj, fl))
    with_far = [i for i in range(nq) if lo_near(i) > 0]
    if not with_far:
        return y_near
    alive = jnp.max(carry[:SB_HEADS, with_far[0] * tq:]) > SB_DEAD_LOG2
    return lax.cond(alive, lambda: _sb_call(q, k, vT, far, tq, tk, state=(acc, carry)), lambda: y_near)


def _memattn_kernel(q_ref, k_ref, vT_ref, o_ref):
    for h in range(MEM_HEADS):
        sl = slice(h * HEAD_DIM, (h + 1) * HEAD_DIM)
        sT = lax.dot_general(k_ref[:, sl], q_ref[:, sl], _NT, preferred_element_type=F32)
        e = jnp.exp2(sT - jnp.max(sT, axis=0, keepdims=True))
        l = jnp.sum(e, axis=0, keepdims=True)
        oT = jnp.dot(vT_ref[sl, :], e.astype(BF16), preferred_element_type=F32) * (1.0 / l)
        o_ref[:, sl] = oT.T.astype(BF16)


def _mem_attention(q, k, vT):
    T, W = q.shape
    M = k.shape[0]
    tq = _tile(T, 512)
    return pl.pallas_call(
        _memattn_kernel, grid=(T // tq,),
        in_specs=[pl.BlockSpec((tq, W), lambda i: (i, 0)),
                  pl.BlockSpec((M, W), lambda i: (0, 0)),
                  pl.BlockSpec((W, M), lambda i: (0, 0))],
        out_specs=pl.BlockSpec((tq, W), lambda i: (i, 0)),
        out_shape=jax.ShapeDtypeStruct((T, W), BF16),
        compiler_params=_params(("parallel",), 32), name="mem_attention",
    )(q, k, vT)


def _mixout_kernel(yc_ref, ys_ref, yw_ref, ysb_ref, ym_ref, g_ref, wn_ref, wsb_ref, wm_ref, wo_ref, x_ref, gain_ref,
                   x2_o, h2_o):
    D = x_ref.shape[1]
    yn = (yc_ref[...].astype(F32) + ys_ref[...].astype(F32) + yw_ref[...].astype(F32)).astype(BF16)
    a = jnp.dot(yn, wn_ref[...], preferred_element_type=F32)
    b = jnp.dot(ysb_ref[...], wsb_ref[...], preferred_element_type=F32)
    c = jnp.dot(ym_ref[...], wm_ref[...], preferred_element_type=F32)
    sig = lambda n: jax.nn.sigmoid(g_ref[:, n * D:(n + 1) * D].astype(F32))
    mixed = (sig(0) * a + sig(1) * b + sig(2) * c).astype(BF16)
    x2 = x_ref[...] + jnp.dot(mixed, wo_ref[...], preferred_element_type=F32)
    x2_o[...] = x2
    h2_o[...] = _rms(x2, gain_ref[...]).astype(BF16)


def _mix_out(yc, ys, yw, ysb, ym, P, wn, wsb, wm, wo, x, gain):
    T, D = x.shape
    tm = _tile(T, 256)
    rowi = lambda w: pl.BlockSpec((tm, w), lambda i: (i, 0))
    res = lambda a: pl.BlockSpec(a.shape, lambda i: (0, 0), pipeline_mode=pl.Buffered(1))
    return pl.pallas_call(
        _mixout_kernel, grid=(T // tm,),
        in_specs=[rowi(yc.shape[1]), rowi(ys.shape[1]), rowi(yw.shape[1]), rowi(ysb.shape[1]), rowi(ym.shape[1]),
                  rowi(N_BRANCH * D), res(wn), res(wsb), res(wm), res(wo), rowi(D), res(gain)],
        out_specs=[rowi(D), rowi(D)],
        out_shape=[jax.ShapeDtypeStruct((T, D), F32), jax.ShapeDtypeStruct((T, D), BF16)],
        compiler_params=_params(("parallel",), 48), name="mix_out",
    )(yc, ys, yw, ysb, ym, P, wn, wsb, wm, wo, x, gain)


def _ffn_kernel(h_ref, wg_ref, wu_ref, wd_ref, x_ref, o_ref, z_sc, *, nf):
    f = pl.program_id(1)

    def up():
        a = jnp.dot(h_ref[...], wg_ref[...], preferred_element_type=F32)
        b = jnp.dot(h_ref[...], wu_ref[...], preferred_element_type=F32)
        return (a * jax.nn.sigmoid(a) * b).astype(BF16)

    def down():
        return jnp.dot(z_sc[...], wd_ref[...], preferred_element_type=F32)

    @pl.when(f == 0)
    def _():
        z_sc[...] = up()

    @pl.when((f > 0) & (f < nf))
    def _():
        c = down()
        z_new = up()
        o_ref[...] = jnp.where(f == 1, x_ref[...], o_ref[...]) + c
        z_sc[...] = z_new

    @pl.when(f == nf)
    def _():
        o_ref[...] += down()


def _ffn(h2, wg, wu, wd, x2):
    T, D = x2.shape
    F = wg.shape[1]
    tm = _tile(T, 1024)
    tf = 512
    assert F % tf == 0
    nf = F // tf
    assert nf >= 2
    return pl.pallas_call(
        functools.partial(_ffn_kernel, nf=nf), grid=(T // tm, nf + 1),
        in_specs=[pl.BlockSpec((tm, D), lambda i, f: (i, 0)),
                  pl.BlockSpec((D, tf), lambda i, f: (0, jnp.minimum(f, nf - 1))),
                  pl.BlockSpec((D, tf), lambda i, f: (0, jnp.minimum(f, nf - 1))),
                  pl.BlockSpec((tf, D), lambda i, f: (jnp.maximum(f - 1, 0), 0)),
                  pl.BlockSpec((tm, D), lambda i, f: (i, 0), pipeline_mode=pl.Buffered(1))],
        out_specs=pl.BlockSpec((tm, D), lambda i, f: (i, 0)),
        out_shape=jax.ShapeDtypeStruct((T, D), F32),
        scratch_shapes=[pltpu.VMEM((tm, tf), BF16)],
        compiler_params=_params(("parallel", "arbitrary"), 56), name="ffn",
    )(h2, wg.astype(BF16), wu.astype(BF16), wd.astype(BF16), x2)


def _layer(x, mem, pos_col, posc_col, consts, attn_norm, w_in, nsa_q_norm, nsa_kc_norm, nsa_ks_norm, nsa_kw_norm,
           cmp_k_pe, cmp_k_w1, cmp_k_w2, cmp_v_pe, cmp_v_w1, cmp_v_w2, mem_norm, w_mem_kv,
           mem_q_norm, mem_k_norm, w_o_nsa, w_o_sb, w_o_mem, w_out, ffn_norm,
           w_ffn_gate, w_ffn_up, w_ffn_down):
    T, D = x.shape
    inv2, sgn, ovT = consts
    hd = HEAD_DIM
    row = lambda g: g.reshape(1, -1)

    q_w, kv_w, gn_w = NSA_HEADS * hd, 6 * NSA_GROUPS * hd, 3 * NSA_HEADS
    sb_w, mq_w, gm_w = 3 * SB_HEADS * hd, MEM_HEADS * hd, N_BRANCH * D
    o_gn = q_w + kv_w
    o_sb = o_gn + gn_w
    o_mq = o_sb + sb_w
    o_gm = o_mq + mq_w
    assert w_in.shape[1] == o_gm + gm_w
    w_main = jnp.concatenate([w_in[:, o_gm:], w_in[:, :o_gn], w_in[:, o_sb:o_gm]], axis=1).astype(BF16)
    w_gate = jnp.pad(w_in[:, o_gn:o_sb], ((0, 0), (0, LANE - gn_w))).astype(BF16)
    P, Pg = _in_proj(x, row(attn_norm), w_main, w_gate)

    (qn, kc_raw, vc_raw, ksn, kwn, vsT, vwT, sbq, sbk, sbvT, memq, gT, qn2, kn2) = _prep(
        P, Pg, gm_w, pos_col, inv2, sgn, row(nsa_q_norm), row(nsa_ks_norm), row(nsa_kw_norm), row(mem_q_norm))

    q2 = jnp.max(qn2[::8].reshape(NSA_GROUPS, NSA_HPG, T), axis=1)
    k2 = jnp.max(kn2[::8], axis=1).reshape(2, NSA_GROUPS)
    bound = lambda kk: jnp.broadcast_to((1.02 * jnp.sqrt(q2 * kk[:, None]))[:, None, :], (NSA_GROUPS, 8, T))
    shift_sel, shift_win = bound(k2[0]), bound(k2[1])

    gates = gT[:gn_w].reshape(NSA_GROUPS, NSA_HPG, 3, T).transpose(2, 0, 1, 3)
    gates = jnp.pad(gates, ((0, 0), (0, 0), (0, 8 - NSA_HPG), (0, 0)))

    n_pad = T // CMP_STRIDE
    half = CMP_LEN // 2

    def w1_pack(w1):
        return jnp.concatenate([w1[:half].reshape(half * hd, -1), w1[half:].reshape(half * hd, -1)], axis=1).astype(BF16)

    def pe_pack(pe):
        return jnp.pad(pe.reshape(2, half * hd), ((0, 6), (0, 0))).astype(BF16)

    kc, vcT = _compress(
        kc_raw.reshape(NSA_GROUPS, n_pad, CMP_STRIDE * hd), vc_raw.reshape(NSA_GROUPS, n_pad, CMP_STRIDE * hd),
        w1_pack(cmp_k_w1), w1_pack(cmp_v_w1), pe_pack(cmp_k_pe), pe_pack(cmp_v_pe),
        cmp_k_w2.astype(BF16), cmp_v_w2.astype(BF16), row(nsa_kc_norm), posc_col, inv2, sgn)

    y_cmp, bias = _cmp_select(qn, kc, vcT, ovT, gates)
    y_sel = _nsa_flash("sel", qn, ksn, vsT, gates, shift_sel, bias)
    y_win = _nsa_flash("win", qn, kwn, vwT, gates, shift_win)
    y_sb = _stick_breaking(sbq, sbk, sbvT)

    mk, mvT = _memkv(mem, row(mem_norm), w_mem_kv.astype(BF16), row(mem_k_norm))
    y_mem = _mem_attention(memq, mk, mvT)

    x2, h2 = _mix_out(y_cmp, y_sel, y_win, y_sb, y_mem, P, w_o_nsa.astype(BF16), w_o_sb.astype(BF16),
                      w_o_mem.astype(BF16), w_out.astype(BF16), x, row(ffn_norm))
    return _ffn(h2, w_ffn_gate, w_ffn_up, w_ffn_down, x2)


def kernel(x, mem, positions, attn_norm, w_in, nsa_q_norm, nsa_kc_norm, nsa_ks_norm, nsa_kw_norm, cmp_k_pe, cmp_k_w1, cmp_k_w2, cmp_v_pe, cmp_v_w1, cmp_v_w2, mem_norm, w_mem_kv, mem_q_norm, mem_k_norm, w_o_nsa, w_o_sb, w_o_mem, w_out, ffn_norm, w_ffn_gate, w_ffn_up, w_ffn_down):
    B, T, D = x.shape
    assert T % (4 * LANE) == 0 and T // SEL_BLOCK >= 8
    n_pad = T // CMP_STRIDE
    n_sel = T // SEL_BLOCK
    inv = 1.0 / (ROPE_THETA ** (jnp.arange(0, HEAD_DIM, 2, dtype=F32) / HEAD_DIM))
    inv2 = jnp.concatenate([inv, inv]).reshape(1, HEAD_DIM)
    sgn = jnp.concatenate([-jnp.ones((HEAD_DIM // 2,), F32), jnp.ones((HEAD_DIM // 2,), F32)]).reshape(1, HEAD_DIM)
    cs = np.arange(n_pad)[None, :] * CMP_STRIDE
    ss = np.arange(n_sel)[:, None] * SEL_BLOCK
    ovT = jnp.asarray(((cs < ss + SEL_BLOCK) & (cs + CMP_LEN - 1 >= ss)).astype(np.float32), BF16)
    consts = (inv2, sgn, ovT)
    depth = w_in.shape[0]
    outs = []
    for b in range(B):
        xb = x[b]
        posf = positions[b].astype(F32)
        pos_col = posf.reshape(T, 1)
        posc = jnp.concatenate([posf[CMP_LEN - 1::CMP_STRIDE], posf[-1:]]).reshape(n_pad, 1)
        for l in range(depth):
            xb = _layer(xb, mem[b], pos_col, posc, consts, attn_norm[l], w_in[l], nsa_q_norm[l], nsa_kc_norm[l],
                        nsa_ks_norm[l], nsa_kw_norm[l], cmp_k_pe[l], cmp_k_w1[l], cmp_k_w2[l], cmp_v_pe[l],
                        cmp_v_w1[l], cmp_v_w2[l], mem_norm[l], w_mem_kv[l], mem_q_norm[l], mem_k_norm[l],
                        w_o_nsa[l], w_o_sb[l], w_o_mem[l], w_out[l], ffn_norm[l],
                        w_ffn_gate[l], w_ffn_up[l], w_ffn_down[l])
        outs.append(xb)
    return outs[0][None] if B == 1 else jnp.stack(outs, axis=0)
```

```python
import functools

import numpy as np
import jax
import jax.numpy as jnp
from jax import lax
from jax.experimental import pallas as pl
from jax.experimental.pallas import tpu as pltpu

HEAD_DIM = 128
NSA_HEADS = 8
NSA_GROUPS = 2
NSA_HPG = NSA_HEADS // NSA_GROUPS
SB_HEADS = 4
MEM_HEADS = 4
CMP_LEN = 32
CMP_STRIDE = 16
CMP_HIDDEN = 2 * HEAD_DIM
SEL_BLOCK = 64
SEL_TOPK = 16
WINDOW = 512
ROPE_THETA = 10000.0
NORM_EPS = 1e-6
NEG_BIG = -1e30
N_BRANCH = 3
SCALE = HEAD_DIM ** -0.5
LOG2E = 1.4426950408889634
QSCALE = SCALE * LOG2E
SOFTMAX_BOUND_MAX = 50.0
CMP_BUCKETS = 8
SB_NEAR_TILES = 3
SB_DEAD_LOG2 = -160.0
VAUG = HEAD_DIM + 16

LANE = 128
MIB = 1 << 20
BF16 = jnp.bfloat16
F32 = jnp.float32

_NT = (((1,), (1,)), ((), ()))


def _tile(n, pref):
    t = min(n, pref)
    assert n % t == 0, (n, pref)
    return t


def _params(sem, vmem_mib):
    return pltpu.CompilerParams(dimension_semantics=sem, vmem_limit_bytes=vmem_mib * MIB)


def _rms(x, gain):
    return x * lax.rsqrt(jnp.mean(x * x, axis=-1, keepdims=True) + NORM_EPS) * gain


def _rope_tables(pos, inv2, sgn):
    ang = pos * inv2
    return jnp.cos(ang), jnp.sin(ang) * sgn


def _rope(x, c, s):
    return x * c + pltpu.roll(x, HEAD_DIM // 2, 1) * s


def _proj_kernel(x_ref, g_ref, w_ref, wg_ref, o_ref, og_ref, hn_ref):
    @pl.when(pl.program_id(1) == 0)
    def _():
        hn_ref[...] = _rms(x_ref[...], g_ref[...]).astype(BF16)
        og_ref[...] = jnp.dot(hn_ref[...], wg_ref[...], preferred_element_type=F32)

    o_ref[...] = jnp.dot(hn_ref[...], w_ref[...], preferred_element_type=F32).astype(BF16)


def _in_proj(x, gain, w_main, w_gate):
    T, D = x.shape
    N = w_main.shape[1]
    tm = _tile(T, 1024)
    tn = 1536
    assert N % tn == 0
    return pl.pallas_call(
        _proj_kernel,
        grid=(T // tm, N // tn),
        in_specs=[
            pl.BlockSpec((tm, D), lambda i, j: (i, 0)),
            pl.BlockSpec((1, D), lambda i, j: (0, 0)),
            pl.BlockSpec((D, tn), lambda i, j: (0, j)),
            pl.BlockSpec((D, LANE), lambda i, j: (0, 0)),
        ],
        out_specs=[pl.BlockSpec((tm, tn), lambda i, j: (i, j)), pl.BlockSpec((tm, LANE), lambda i, j: (i, 0))],
        out_shape=[jax.ShapeDtypeStruct((T, N), BF16), jax.ShapeDtypeStruct((T, LANE), F32)],
        scratch_shapes=[pltpu.VMEM((tm, D), BF16)],
        compiler_params=_params(("parallel", "arbitrary"), 52),
        name="in_proj",
    )(x, gain, w_main, w_gate)


def _prep_kernel(pos_ref, inv_ref, sgn_ref, gq_ref, gks_ref, gkw_ref, gmq_ref,
                 q_ref, kc_ref, vc_ref, ks_ref, vs_ref, kw_ref, vw_ref,
                 sq_ref, sk_ref, sv_ref, mq_ref, gn_ref,
                 qn_o, kc_o, vc_o, ks_o, kw_o, vsT_o, vwT_o, sq_o, sk_o, svT_o, mq_o, gT_o, qn2_o, kn2_o):
    c, s = _rope_tables(pos_ref[...], inv_ref[...], sgn_ref[...])
    hd = HEAD_DIM
    tp = pos_ref.shape[0]
    f32 = lambda ref, sl: ref[:, sl].astype(F32)
    ones8 = jnp.ones((8, hd), F32)

    def sqnorm_rows(xb):
        x = xb.astype(F32)
        return lax.dot_general(ones8, x * x, _NT, preferred_element_type=F32)

    for h in range(NSA_HEADS):
        sl = slice(h * hd, (h + 1) * hd)
        qb = (_rope(_rms(f32(q_ref, sl), gq_ref[...]), c, s) * QSCALE).astype(BF16)
        qn_o[:, sl] = qb
        qn2_o[h * 8:(h + 1) * 8, :] = sqnorm_rows(qb)
    ones_rows = (lax.broadcasted_iota(jnp.int32, (VAUG - hd, tp), 0) == 0).astype(F32).astype(BF16)
    for g in range(NSA_GROUPS):
        sl = slice(g * hd, (g + 1) * hd)
        kc_o[g] = kc_ref[:, sl]
        vc_o[g] = vc_ref[:, sl]
        ksb = _rope(_rms(f32(ks_ref, sl), gks_ref[...]), c, s).astype(BF16)
        kwb = _rope(_rms(f32(kw_ref, sl), gkw_ref[...]), c, s).astype(BF16)
        ks_o[:, sl] = ksb
        kw_o[:, sl] = kwb
        kn2_o[g * 8:(g + 1) * 8, :] = sqnorm_rows(ksb)
        kn2_o[(NSA_GROUPS + g) * 8:(NSA_GROUPS + g + 1) * 8, :] = sqnorm_rows(kwb)
        vsT_o[g * VAUG:g * VAUG + hd, :] = f32(vs_ref, sl).T.astype(BF16)
        vsT_o[g * VAUG + hd:(g + 1) * VAUG, :] = ones_rows
        vwT_o[g * VAUG:g * VAUG + hd, :] = f32(vw_ref, sl).T.astype(BF16)
        vwT_o[g * VAUG + hd:(g + 1) * VAUG, :] = ones_rows
    for h in range(SB_HEADS):
        sl = slice(h * hd, (h + 1) * hd)
        sq_o[:, sl] = (f32(sq_ref, sl) * QSCALE).astype(BF16)
        svT_o[sl, :] = f32(sv_ref, sl).T.astype(BF16)
    sk_o[...] = sk_ref[...]
    for h in range(MEM_HEADS):
        sl = slice(h * hd, (h + 1) * hd)
        mq_o[:, sl] = (_rms(f32(mq_ref, sl), gmq_ref[...]) * QSCALE).astype(BF16)
    gT_o[...] = jax.nn.sigmoid(gn_ref[...]).T


def _prep(P, Pg, c0, pos_col, inv2, sgn, gq, gks, gkw, gmq):
    T = P.shape[0]
    tp = _tile(T, 512)
    hd = HEAD_DIM
    row = lambda w, c: pl.BlockSpec((tp, w), lambda i, c=c: (i, c))
    const = lambda: pl.BlockSpec((1, hd), lambda i: (0, 0))
    assert c0 % (8 * hd) == 0
    at = lambda w, col: row(w, (c0 + col) // w)
    in_specs = [pl.BlockSpec((tp, 1), lambda i: (i, 0)), const(), const(), const(), const(), const(), const(),
                at(8 * hd, 0),
                at(2 * hd, 8 * hd), at(2 * hd, 10 * hd),
                at(2 * hd, 12 * hd), at(2 * hd, 14 * hd),
                at(2 * hd, 16 * hd), at(2 * hd, 18 * hd),
                at(4 * hd, 20 * hd), at(4 * hd, 24 * hd), at(4 * hd, 28 * hd),
                at(4 * hd, 32 * hd),
                row(hd, 0)]
    colT = lambda w: pl.BlockSpec((w, tp), lambda i: (0, i))
    grp = lambda w: pl.BlockSpec((NSA_GROUPS, tp, w), lambda i: (0, i, 0))
    out_specs = [row(8 * hd, 0), grp(hd), grp(hd), row(2 * hd, 0),
                 row(2 * hd, 0), colT(NSA_GROUPS * VAUG), colT(NSA_GROUPS * VAUG),
                 row(4 * hd, 0), row(4 * hd, 0), colT(4 * hd), row(4 * hd, 0), colT(hd),
                 colT(NSA_HEADS * 8), colT(2 * NSA_GROUPS * 8)]
    sds = jax.ShapeDtypeStruct
    out_shape = [sds((T, 8 * hd), BF16),
                 sds((NSA_GROUPS, T, hd), BF16), sds((NSA_GROUPS, T, hd), BF16), sds((T, 2 * hd), BF16),
                 sds((T, 2 * hd), BF16),
                 sds((NSA_GROUPS * VAUG, T), BF16), sds((NSA_GROUPS * VAUG, T), BF16),
                 sds((T, 4 * hd), BF16), sds((T, 4 * hd), BF16), sds((4 * hd, T), BF16), sds((T, 4 * hd), BF16),
                 sds((hd, T), F32), sds((NSA_HEADS * 8, T), F32), sds((2 * NSA_GROUPS * 8, T), F32)]
    return pl.pallas_call(
        _prep_kernel, grid=(T // tp,), in_specs=in_specs, out_specs=out_specs, out_shape=out_shape,
        compiler_params=_params(("parallel",), 48), name="prep",
    )(pos_col, inv2, sgn, gq, gks, gkw, gmq, *([P] * 11), Pg)


def _gelu_tanh(x):
    return 0.5 * x * (1.0 + jnp.tanh(0.7978845608028654 * (x + 0.044715 * (x * x * x))))


def _compress_one(x, w1, pe, w2):
    n = x.shape[0]
    ab = jnp.dot(x, w1, preferred_element_type=F32)
    pr = jnp.dot(pe, w1, preferred_element_type=F32)
    pec = pr[0:1, :CMP_HIDDEN] + pr[1:2, CMP_HIDDEN:]
    hid = ab[:, :CMP_HIDDEN] + pltpu.roll(ab[:, CMP_HIDDEN:], n - 1, 0) + pec
    return jnp.dot(_gelu_tanh(hid).astype(BF16), w2, preferred_element_type=F32)


def _compress_kernel(xk_ref, xv_ref, w1k_ref, w1v_ref, pek_ref, pev_ref, w2k_ref, w2v_ref,
                     gk_ref, pos_ref, inv_ref, sgn_ref, kc_o, vcT_o):
    c, s = _rope_tables(pos_ref[...], inv_ref[...], sgn_ref[...])
    k = _compress_one(xk_ref[0], w1k_ref[...], pek_ref[...], w2k_ref[...])
    kc_o[0] = _rope(_rms(k, gk_ref[...]), c, s).astype(BF16)
    v = _compress_one(xv_ref[0], w1v_ref[...], pev_ref[...], w2v_ref[...])
    vcT_o[0] = v.T.astype(BF16)


def _compress(xk, xv, w1k, w1v, pek, pev, w2k, w2v, gk, posc, inv2, sgn):
    G, n, W = xk.shape
    full = lambda a: pl.BlockSpec(a.shape, lambda g: (0,) * a.ndim)
    grp = pl.BlockSpec((1, n, W), lambda g: (g, 0, 0))
    return pl.pallas_call(
        _compress_kernel, grid=(G,),
        in_specs=[grp, grp, full(w1k), full(w1v), full(pek), full(pev), full(w2k), full(w2v),
                  full(gk), full(posc), full(inv2), full(sgn)],
        out_specs=[pl.BlockSpec((1, n, HEAD_DIM), lambda g: (g, 0, 0)),
                   pl.BlockSpec((1, HEAD_DIM, n), lambda g: (g, 0, 0))],
        out_shape=[jax.ShapeDtypeStruct((G, n, HEAD_DIM), BF16), jax.ShapeDtypeStruct((G, HEAD_DIM, n), BF16)],
        compiler_params=_params(("parallel",), 48), name="compress",
    )(xk, xv, w1k, w1v, pek, pev, w2k, w2v, gk, posc, inv2, sgn)


def _memkv_kernel(mem_ref, gm_ref, w_ref, gk_ref, k_o, vT_o):
    hn = _rms(mem_ref[...], gm_ref[...]).astype(BF16)
    kv = jnp.dot(hn, w_ref[...], preferred_element_type=F32)
    mw = MEM_HEADS * HEAD_DIM
    for h in range(MEM_HEADS):
        sl = slice(h * HEAD_DIM, (h + 1) * HEAD_DIM)
        k_o[:, sl] = _rms(kv[:, sl], gk_ref[...]).astype(BF16)
        vT_o[sl, :] = kv[:, mw + h * HEAD_DIM: mw + (h + 1) * HEAD_DIM].T.astype(BF16)


def _memkv(mem, gm, w, gk):
    M = mem.shape[0]
    mw = MEM_HEADS * HEAD_DIM
    return pl.pallas_call(
        _memkv_kernel,
        out_shape=[jax.ShapeDtypeStruct((M, mw), BF16), jax.ShapeDtypeStruct((mw, M), BF16)],
        compiler_params=pltpu.CompilerParams(vmem_limit_bytes=32 * MIB), name="mem_kv",
    )(mem, gm, w, gk)


def _cmp_kernel(q_ref, kc_ref, vcT_ref, ovT_ref, g_ref, *rest, tq, i0, n_pad, n_sel, top_k):
    y_ref, b_ref = rest[-2:]
    i = i0 + pl.program_id(1)
    t = i * tq + lax.broadcasted_iota(jnp.int32, (1, tq), 1)
    n_end = lax.broadcasted_iota(jnp.int32, (n_pad, 1), 0) * CMP_STRIDE + (CMP_LEN - 1)
    valid = n_end <= t
    kc = kc_ref[0]
    vcT = vcT_ref[0]
    has_valid = (t >= CMP_LEN - 1).astype(F32)
    sms = []
    for h in range(NSA_HPG):
        q_h = q_ref[:, h * HEAD_DIM:(h + 1) * HEAD_DIM]
        sms.append(jnp.where(valid, lax.dot_general(kc, q_h, _NT, preferred_element_type=F32), NEG_BIG))
    psum = jnp.zeros((n_pad, tq), F32)
    for h in range(NSA_HPG):
        e = jnp.exp2(sms[h] - jnp.max(sms[h], axis=0, keepdims=True))
        p = e * (has_valid / jnp.sum(e, axis=0, keepdims=True))
        oT = jnp.dot(vcT, p.astype(BF16), preferred_element_type=F32)
        y_ref[:, h * HEAD_DIM:(h + 1) * HEAD_DIM] = (oT * g_ref[0, 0, h:h + 1, :]).T.astype(BF16)
        psum = psum + p
    hi = psum.astype(BF16)
    r1 = psum - hi.astype(F32)
    mid = r1.astype(BF16)
    lo = (r1 - mid.astype(F32)).astype(BF16)
    parts = jnp.dot(ovT_ref[...], jnp.concatenate([hi, mid, lo], axis=1), preferred_element_type=F32)
    imp = parts[:, :tq] + parts[:, tq:2 * tq] + parts[:, 2 * tq:]
    s_i = lax.broadcasted_iota(jnp.int32, (n_sel, 1), 0)
    cur = lax.shift_right_logical(t, 6)
    forced = (s_i == 0) | (s_i == cur) | (s_i == cur - 1)
    future = s_i * SEL_BLOCK > t
    w = jnp.where(forced, jnp.inf, jnp.where(future, -jnp.inf, imp))
    s_f = jnp.broadcast_to(s_i.astype(F32), (n_sel, tq))
    for _ in range(top_k):
        m = jnp.max(w, axis=0, keepdims=True)
        idx = jnp.min(jnp.where(w == m, s_f, float(n_sel)), axis=0, keepdims=True)
        w = jnp.where(s_f == idx, -jnp.inf, w)
    b_ref[0, :n_sel, :] = jnp.where(future, NEG_BIG, jnp.where(w == -jnp.inf, 0.0, NEG_BIG))
    n_all = b_ref.shape[1]
    if n_sel < n_all:
        b_ref[0, n_sel:, :] = jnp.full((n_all - n_sel, tq), NEG_BIG, F32)


def _round_up(n, m):
    return -(-n // m) * m


def _cmp_select(qn, kc, vcT, ovT, gates):
    T = qn.shape[0]
    G, n_pad, _ = kc.shape
    n_sel = T // SEL_BLOCK
    tq = _tile(T, 256)
    nq = T // tq
    gw = NSA_HPG * HEAD_DIM
    nb = min(CMP_BUCKETS, nq)
    assert nq % nb == 0
    per = nq // nb
    out_shape = [jax.ShapeDtypeStruct((T, NSA_HEADS * HEAD_DIM), BF16), jax.ShapeDtypeStruct((G, n_sel, T), F32)]
    buffers = ()
    for b in range(nb):
        i0 = b * per
        t_max = (i0 + per) * tq - 1
        n_len = min(n_pad, _round_up(max(t_max - (CMP_LEN - 1), 0) // CMP_STRIDE + 1, LANE))
        s_len = min(n_sel, _round_up(t_max // SEL_BLOCK + 1, 8))
        kern = functools.partial(_cmp_kernel, tq=tq, i0=i0, n_pad=n_len, n_sel=s_len, top_k=min(SEL_TOPK, n_sel))
        buffers = pl.pallas_call(
            kern, grid=(G, per),
            in_specs=[pl.BlockSpec((tq, gw), lambda g, i, i0=i0: (i0 + i, g)),
                      pl.BlockSpec((1, n_len, HEAD_DIM), lambda g, i: (g, 0, 0)),
                      pl.BlockSpec((1, HEAD_DIM, n_len), lambda g, i: (g, 0, 0)),
                      pl.BlockSpec((s_len, n_len), lambda g, i: (0, 0)),
                      pl.BlockSpec((1, 1, 8, tq), lambda g, i, i0=i0: (0, g, 0, i0 + i))]
            + [pl.BlockSpec(memory_space=pl.ANY)] * len(buffers),
            out_specs=[pl.BlockSpec((tq, gw), lambda g, i, i0=i0: (i0 + i, g)),
                       pl.BlockSpec((1, n_sel, tq), lambda g, i, i0=i0: (g, 0, i0 + i))],
            out_shape=out_shape,
            input_output_aliases={5: 0, 6: 1} if buffers else {},
            compiler_params=_params(("parallel", "parallel"), 48), name="cmp_select_%d" % b,
        )(qn, kc, vcT, ovT, gates, *buffers)
    return buffers


def _flash_kernel(qi_ref, kj_ref, fl_ref, q_ref, k_ref, vT_ref, g_ref, sh_ref, *rest, mode, bounded, tq, tk):
    if mode == "sel":
        bias_ref, o_ref, m_sc, acc_sc = rest
    else:
        o_ref, m_sc, acc_sc = rest
    p = pl.program_id(1)
    i = qi_ref[p]
    j = kj_ref[p]
    fl = fl_ref[p]

    @pl.when((fl & 1) != 0)
    def _():
        m_sc[...] = jnp.full(m_sc.shape, NEG_BIG, F32)
        acc_sc[...] = jnp.zeros(acc_sc.shape, F32)

    t = i * tq + lax.broadcasted_iota(jnp.int32, (1, tq), 1)
    kpos = j * tk + lax.broadcasted_iota(jnp.int32, (tk, 1), 0)
    shift = sh_ref[0, 0:1, :] if bounded else 0.0
    if mode == "sel":
        nb = tk // SEL_BLOCK
        rows = bias_ref[0] - shift
        base = jnp.concatenate(
            [jnp.broadcast_to(rows[b:b + 1, :], (SEL_BLOCK, tq)) for b in range(nb)], axis=0)
        mask_add = jnp.where(kpos <= t, base, NEG_BIG)
    else:
        mask_add = jnp.where((kpos <= t) & (kpos > t - WINDOW), 0.0 - shift, NEG_BIG)
    k = k_ref[...]
    vT = vT_ref[...]
    scores = lambda h: lax.dot_general(k, q_ref[:, h * HEAD_DIM:(h + 1) * HEAD_DIM], _NT,
                                       preferred_element_type=F32) + mask_add
    if bounded:
        pTs = [jnp.exp2(scores(h)).astype(BF16) for h in range(NSA_HPG)]
        for h in range(NSA_HPG):
            acc_sc[h] = acc_sc[h] + jnp.dot(vT, pTs[h], preferred_element_type=F32)
    else:
        sTs = [scores(h) for h in range(NSA_HPG)]
        pTs, alphas = [], []
        for h in range(NSA_HPG):
            sT = sTs[h]
            m_old = m_sc[h]
            m_new = jnp.maximum(m_old, jnp.max(sT, axis=0, keepdims=True))
            alphas.append(jnp.exp2(m_old - m_new))
            pTs.append(jnp.exp2((sT - m_new).astype(BF16)))
            m_sc[h] = m_new
        for h in range(NSA_HPG):
            acc_sc[h] = alphas[h] * acc_sc[h] + jnp.dot(vT, pTs[h], preferred_element_type=F32)

    @pl.when((fl & 2) != 0)
    def _():
        for h in range(NSA_HPG):
            acc = acc_sc[h]
            l = acc[HEAD_DIM:HEAD_DIM + 1, :]
            o = acc[:HEAD_DIM, :] * ((1.0 / l) * g_ref[0, 0, h:h + 1, :])
            o_ref[:, h * HEAD_DIM:(h + 1) * HEAD_DIM] = o.T.astype(BF16)


def _steps(nq, lo_fn, hi_fn, reverse=False):
    qi, kj, fl = [], [], []
    for i in range(nq):
        js = list(range(lo_fn(i), hi_fn(i) + 1))
        if reverse:
            js = js[::-1]
        for n, j in enumerate(js):
            qi.append(i)
            kj.append(j)
            fl.append((1 if n == 0 else 0) | (2 if n == len(js) - 1 else 0))
    return (jnp.asarray(np.array(qi, np.int32)), jnp.asarray(np.array(kj, np.int32)),
            jnp.asarray(np.array(fl, np.int32)))


def _nsa_flash(mode, qn, k, vT, gates, shift, bias=None):
    use_bounded = jnp.max(shift) <= SOFTMAX_BOUND_MAX
    return lax.cond(use_bounded,
                    lambda: _nsa_flash_call(mode, True, qn, k, vT, gates, shift, bias),
                    lambda: _nsa_flash_call(mode, False, qn, k, vT, gates, shift, bias))


def _nsa_flash_call(mode, bounded, qn, k, vT, gates, shift, bias):
    T = qn.shape[0]
    tq = _tile(T, 1024)
    tk = _tile(T, 1024 if mode == "sel" else 512)
    nq = T // tq
    gw = NSA_HPG * HEAD_DIM
    hi = lambda i: ((i + 1) * tq - 1) // tk
    if mode == "sel":
        lo = lambda i: 0
        br = 1
    else:
        lo = lambda i: max(0, (i * tq - (WINDOW - 1)) // tk)
        br = 2
    qi, kj, fl = _steps(nq, lo, hi)
    in_specs = [pl.BlockSpec((tq, gw), lambda g, p, qi, kj, fl: (qi[p], g)),
                pl.BlockSpec((tk, HEAD_DIM), lambda g, p, qi, kj, fl: (kj[p], g)),
                pl.BlockSpec((VAUG, tk), lambda g, p, qi, kj, fl: (g, kj[p])),
                pl.BlockSpec((1, 1, 8, tq), lambda g, p, qi, kj, fl, br=br: (br, g, 0, qi[p])),
                pl.BlockSpec((1, 8, tq), lambda g, p, qi, kj, fl: (g, 0, qi[p]))]
    args = [qn, k, vT, gates, shift]
    if mode == "sel":
        in_specs.append(pl.BlockSpec((1, tk // SEL_BLOCK, tq), lambda g, p, qi, kj, fl: (g, kj[p], qi[p])))
        args.append(bias)
    kern = functools.partial(_flash_kernel, mode=mode, bounded=bounded, tq=tq, tk=tk)
    return pl.pallas_call(
        kern,
        grid_spec=pltpu.PrefetchScalarGridSpec(
            num_scalar_prefetch=3, grid=(NSA_GROUPS, int(qi.shape[0])),
            in_specs=in_specs,
            out_specs=pl.BlockSpec((tq, gw), lambda g, p, qi, kj, fl: (qi[p], g)),
            scratch_shapes=[pltpu.VMEM((NSA_HPG, 1, tq), F32), pltpu.VMEM((NSA_HPG, VAUG, tq), F32)]),
        out_shape=jax.ShapeDtypeStruct((T, NSA_HEADS * HEAD_DIM), BF16),
        compiler_params=_params(("parallel", "arbitrary"), 48),
        name="nsa_" + mode + ("_bounded" if bounded else "_online"),
    )(qi, kj, fl, *args)


def _sb_kernel(qi_ref, kj_ref, fl_ref, q_ref, k_ref, vT_ref, lm_ref, *rest, tq, tk, resume):
    if resume:
        acc_in, carry_in, o_ref, carry_sc, acc_sc = rest
    else:
        o_ref, acc_o, carry_o, carry_sc, acc_sc = rest
    p = pl.program_id(0)
    i = qi_ref[p]
    j = kj_ref[p]
    fl = fl_ref[p]

    @pl.when((fl & 1) != 0)
    def _():
        if resume:
            for h in range(SB_HEADS):
                carry_sc[h] = carry_in[h:h + 1, :]
                acc_sc[h] = acc_in[h * HEAD_DIM:(h + 1) * HEAD_DIM, :]
        else:
            carry_sc[...] = jnp.zeros(carry_sc.shape, F32)
            acc_sc[...] = jnp.zeros(acc_sc.shape, F32)

    @pl.when((jnp.max(carry_sc[...]) > SB_DEAD_LOG2) & ((fl & 4) == 0))
    def _():
        t = i * tq + lax.broadcasted_iota(jnp.int32, (1, tq), 1)
        kpos = j * tk + lax.broadcasted_iota(jnp.int32, (tk, 1), 0)
        past = kpos < t
        lm = lm_ref[...]
        hs = [slice(h * HEAD_DIM, (h + 1) * HEAD_DIM) for h in range(SB_HEADS)]
        zs = [lax.dot_general(k_ref[:, sl], q_ref[:, sl], _NT, preferred_element_type=F32) for sl in hs]
        lgs, css = [], []
        for h in range(SB_HEADS):
            z = zs[h]
            sp = jnp.maximum(z, 0.0) + jnp.log2(1.0 + jnp.exp2(-jnp.abs(z)))
            lk = jnp.where(past, -sp, 0.0)
            hi = lk.astype(BF16)
            lo = (lk - hi.astype(F32)).astype(BF16)
            css.append(jnp.dot(lm, jnp.concatenate([hi, lo], axis=1), preferred_element_type=F32))
            lgs.append(z - sp)
            carry_old = carry_sc[h]
            carry_sc[h] = carry_old + jnp.sum(lk, axis=0, keepdims=True)
            css[h] = css[h][:, :tq] + css[h][:, tq:] + carry_old
        for h in range(SB_HEADS):
            wgt = jnp.where(past, jnp.exp2(lgs[h] + css[h]), 0.0)
            acc_sc[h] = acc_sc[h] + jnp.dot(vT_ref[hs[h], :], wgt.astype(BF16), preferred_element_type=F32)

    @pl.when((fl & 2) != 0)
    def _():
        for h in range(SB_HEADS):
            o_ref[:, h * HEAD_DIM:(h + 1) * HEAD_DIM] = acc_sc[h].T.astype(BF16)
            if not resume:
                acc_o[h * HEAD_DIM:(h + 1) * HEAD_DIM, :] = acc_sc[h]
                carry_o[h:h + 1, :] = carry_sc[h]
        if not resume:
            carry_o[SB_HEADS:, :] = jnp.zeros((8 - SB_HEADS, tq), F32)


def _sb_call(q, k, vT, steps, tq, tk, state=None):
    T, W = q.shape
    qi, kj, fl = steps
    lmat = jnp.asarray(np.triu(np.ones((tk, tk), np.float32), 1), BF16)
    resume = state is not None
    qtile = lambda shape: pl.BlockSpec(shape, lambda p, qi, kj, fl: (qi[p], 0))
    qtileT = lambda rows: pl.BlockSpec((rows, tq), lambda p, qi, kj, fl: (0, qi[p]))
    in_specs = [qtile((tq, W)),
                pl.BlockSpec((tk, W), lambda p, qi, kj, fl: (kj[p], 0)),
                pl.BlockSpec((W, tk), lambda p, qi, kj, fl: (0, kj[p])),
                pl.BlockSpec((tk, tk), lambda p, qi, kj, fl: (0, 0))]
    y_sds = jax.ShapeDtypeStruct((T, W), BF16)
    if resume:
        in_specs += [qtileT(W), qtileT(8)]
        out_specs, out_shape = qtile((tq, W)), y_sds
    else:
        out_specs = [qtile((tq, W)), qtileT(W), qtileT(8)]
        out_shape = [y_sds, jax.ShapeDtypeStruct((W, T), F32), jax.ShapeDtypeStruct((8, T), F32)]
    return pl.pallas_call(
        functools.partial(_sb_kernel, tq=tq, tk=tk, resume=resume),
        grid_spec=pltpu.PrefetchScalarGridSpec(
            num_scalar_prefetch=3, grid=(int(qi.shape[0]),),
            in_specs=in_specs, out_specs=out_specs,
            scratch_shapes=[pltpu.VMEM((SB_HEADS, 1, tq), F32), pltpu.VMEM((SB_HEADS, HEAD_DIM, tq), F32)]),
        out_shape=out_shape,
        compiler_params=_params(("arbitrary",), 48), name="sb_far" if resume else "sb_near",
    )(qi, kj, fl, q, k, vT, lmat, *(state or ()))


def _stick_breaking(q, k, vT):
    T, W = q.shape
    tq = _tile(T, 512)
    tk = _tile(T, 256)
    nq = T // tq
    hi = lambda i: ((i + 1) * tq - 2) // tk
    lo_near = lambda i: max(0, hi(i) - SB_NEAR_TILES + 1)
    near = _steps(nq, lo_near, hi, reverse=True)
    y_near, acc, carry = _sb_call(q, k, vT, near, tq, tk)
    qi, kj, fl = [], [], []
    for i in range(nq):
        js = list(range(lo_near(i) - 1, -1, -1))
        for n, j in enumerate(js or [0]):
            qi.append(i)
            kj.append(j)
            fl.append((1 if n == 0 else 0) | (2 if n == max(len(js), 1) - 1 else 0) | (0 if js else 4))
    far = tuple(jnp.asarray(np.array(a, np.int32)) for a in (qi, kj, fl))
    with_far = [i for i in range(nq) if lo_near(i) > 0]
    if not with_far:
        return y_near
    alive = jnp.max(carry[:SB_HEADS, with_far[0] * tq:]) > SB_DEAD_LOG2
    return lax.cond(alive, lambda: _sb_call(q, k, vT, far, tq, tk, state=(acc, carry)), lambda: y_near)


def _memattn_kernel(q_ref, k_ref, vT_ref, o_ref):
    for h in range(MEM_HEADS):
        sl = slice(h * HEAD_DIM, (h + 1) * HEAD_DIM)
        sT = lax.dot_general(k_ref[:, sl], q_ref[:, sl], _NT, preferred_element_type=F32)
        e = jnp.exp2(sT - jnp.max(sT, axis=0, keepdims=True))
        l = jnp.sum(e, axis=0, keepdims=True)
        oT = jnp.dot(vT_ref[sl, :], e.astype(BF16), preferred_element_type=F32) * (1.0 / l)
        o_ref[:, sl] = oT.T.astype(BF16)


def _mem_attention(q, k, vT):
    T, W = q.shape
    M = k.shape[0]
    tq = _tile(T, 512)
    return pl.pallas_call(
        _memattn_kernel, grid=(T // tq,),
        in_specs=[pl.BlockSpec((tq, W), lambda i: (i, 0)),
                  pl.BlockSpec((M, W), lambda i: (0, 0)),
                  pl.BlockSpec((W, M), lambda i: (0, 0))],
        out_specs=pl.BlockSpec((tq, W), lambda i: (i, 0)),
        out_shape=jax.ShapeDtypeStruct((T, W), BF16),
        compiler_params=_params(("parallel",), 32), name="mem_attention",
    )(q, k, vT)


def _mixout_kernel(yc_ref, ys_ref, yw_ref, ysb_ref, ym_ref, g_ref, wn_ref, wsb_ref, wm_ref, wo_ref, x_ref, gain_ref,
                   x2_o, h2_o):
    D = x_ref.shape[1]
    yn = (yc_ref[...].astype(F32) + ys_ref[...].astype(F32) + yw_ref[...].astype(F32)).astype(BF16)
    a = jnp.dot(yn, wn_ref[...], preferred_element_type=F32)
    b = jnp.dot(ysb_ref[...], wsb_ref[...], preferred_element_type=F32)
    c = jnp.dot(ym_ref[...], wm_ref[...], preferred_element_type=F32)
    sig = lambda n: jax.nn.sigmoid(g_ref[:, n * D:(n + 1) * D].astype(F32))
    mixed = (sig(0) * a + sig(1) * b + sig(2) * c).astype(BF16)
    x2 = x_ref[...] + jnp.dot(mixed, wo_ref[...], preferred_element_type=F32)
    x2_o[...] = x2
    h2_o[...] = _rms(x2, gain_ref[...]).astype(BF16)


def _mix_out(yc, ys, yw, ysb, ym, P, wn, wsb, wm, wo, x, gain):
    T, D = x.shape
    tm = _tile(T, 256)
    rowi = lambda w: pl.BlockSpec((tm, w), lambda i: (i, 0))
    res = lambda a: pl.BlockSpec(a.shape, lambda i: (0, 0), pipeline_mode=pl.Buffered(1))
    return pl.pallas_call(
        _mixout_kernel, grid=(T // tm,),
        in_specs=[rowi(yc.shape[1]), rowi(ys.shape[1]), rowi(yw.shape[1]), rowi(ysb.shape[1]), rowi(ym.shape[1]),
                  rowi(N_BRANCH * D), res(wn), res(wsb), res(wm), res(wo), rowi(D), res(gain)],
        out_specs=[rowi(D), rowi(D)],
        out_shape=[jax.ShapeDtypeStruct((T, D), F32), jax.ShapeDtypeStruct((T, D), BF16)],
        compiler_params=_params(("parallel",), 48), name="mix_out",
    )(yc, ys, yw, ysb, ym, P, wn, wsb, wm, wo, x, gain)


def _ffn_kernel(h_ref, wg_ref, wu_ref, wd_ref, x_ref, o_ref, z_sc, *, nf):
    f = pl.program_id(1)

    def up():
        a = jnp.dot(h_ref[...], wg_ref[...], preferred_element_type=F32)
        b = jnp.dot(h_ref[...], wu_ref[...], preferred_element_type=F32)
        return (a * jax.nn.sigmoid(a) * b).astype(BF16)

    def down():
        return jnp.dot(z_sc[...], wd_ref[...], preferred_element_type=F32)

    @pl.when(f == 0)
    def _():
        z_sc[...] = up()

    @pl.when((f > 0) & (f < nf))
    def _():
        c = down()
        z_new = up()
        o_ref[...] = jnp.where(f == 1, x_ref[...], o_ref[...]) + c
        z_sc[...] = z_new

    @pl.when(f == nf)
    def _():
        o_ref[...] += down()


def _ffn(h2, wg, wu, wd, x2):
    T, D = x2.shape
    F = wg.shape[1]
    tm = _tile(T, 1024)
    tf = 512
    assert F % tf == 0
    nf = F // tf
    assert nf >= 2
    return pl.pallas_call(
        functools.partial(_ffn_kernel, nf=nf), grid=(T // tm, nf + 1),
        in_specs=[pl.BlockSpec((tm, D), lambda i, f: (i, 0)),
                  pl.BlockSpec((D, tf), lambda i, f: (0, jnp.minimum(f, nf - 1))),
                  pl.BlockSpec((D, tf), lambda i, f: (0, jnp.minimum(f, nf - 1))),
                  pl.BlockSpec((tf, D), lambda i, f: (jnp.maximum(f - 1, 0), 0)),
                  pl.BlockSpec((tm, D), lambda i, f: (i, 0), pipeline_mode=pl.Buffered(1))],
        out_specs=pl.BlockSpec((tm, D), lambda i, f: (i, 0)),
        out_shape=jax.ShapeDtypeStruct((T, D), F32),
        scratch_shapes=[pltpu.VMEM((tm, tf), BF16)],
        compiler_params=_params(("parallel", "arbitrary"), 56), name="ffn",
    )(h2, wg.astype(BF16), wu.astype(BF16), wd.astype(BF16), x2)


def _layer(x, mem, pos_col, posc_col, consts, attn_norm, w_in, nsa_q_norm, nsa_kc_norm, nsa_ks_norm, nsa_kw_norm,
           cmp_k_pe, cmp_k_w1, cmp_k_w2, cmp_v_pe, cmp_v_w1, cmp_v_w2, mem_norm, w_mem_kv,
           mem_q_norm, mem_k_norm, w_o_nsa, w_o_sb, w_o_mem, w_out, ffn_norm,
           w_ffn_gate, w_ffn_up, w_ffn_down):
    T, D = x.shape
    inv2, sgn, ovT = consts
    hd = HEAD_DIM
    row = lambda g: g.reshape(1, -1)

    q_w, kv_w, gn_w = NSA_HEADS * hd, 6 * NSA_GROUPS * hd, 3 * NSA_HEADS
    sb_w, mq_w, gm_w = 3 * SB_HEADS * hd, MEM_HEADS * hd, N_BRANCH * D
    o_gn = q_w + kv_w
    o_sb = o_gn + gn_w
    o_mq = o_sb + sb_w
    o_gm = o_mq + mq_w
    assert w_in.shape[1] == o_gm + gm_w
    w_main = jnp.concatenate([w_in[:, o_gm:], w_in[:, :o_gn], w_in[:, o_sb:o_gm]], axis=1).astype(BF16)
    w_gate = jnp.pad(w_in[:, o_gn:o_sb], ((0, 0), (0, LANE - gn_w))).astype(BF16)
    P, Pg = _in_proj(x, row(attn_norm), w_main, w_gate)

    (qn, kc_raw, vc_raw, ksn, kwn, vsT, vwT, sbq, sbk, sbvT, memq, gT, qn2, kn2) = _prep(
        P, Pg, gm_w, pos_col, inv2, sgn, row(nsa_q_norm), row(nsa_ks_norm), row(nsa_kw_norm), row(mem_q_norm))

    q2 = jnp.max(qn2[::8].reshape(NSA_GROUPS, NSA_HPG, T), axis=1)
    k2 = jnp.max(kn2[::8], axis=1).reshape(2, NSA_GROUPS)
    bound = lambda kk: jnp.broadcast_to((1.02 * jnp.sqrt(q2 * kk[:, None]))[:, None, :], (NSA_GROUPS, 8, T))
    shift_sel, shift_win = bound(k2[0]), bound(k2[1])

    gates = gT[:gn_w].reshape(NSA_GROUPS, NSA_HPG, 3, T).transpose(2, 0, 1, 3)
    gates = jnp.pad(gates, ((0, 0), (0, 0), (0, 8 - NSA_HPG), (0, 0)))

    n_pad = T // CMP_STRIDE
    half = CMP_LEN // 2

    def w1_pack(w1):
        return jnp.concatenate([w1[:half].reshape(half * hd, -1), w1[half:].reshape(half * hd, -1)], axis=1).astype(BF16)

    def pe_pack(pe):
        return jnp.pad(pe.reshape(2, half * hd), ((0, 6), (0, 0))).astype(BF16)

    kc, vcT = _compress(
        kc_raw.reshape(NSA_GROUPS, n_pad, CMP_STRIDE * hd), vc_raw.reshape(NSA_GROUPS, n_pad, CMP_STRIDE * hd),
        w1_pack(cmp_k_w1), w1_pack(cmp_v_w1), pe_pack(cmp_k_pe), pe_pack(cmp_v_pe),
        cmp_k_w2.astype(BF16), cmp_v_w2.astype(BF16), row(nsa_kc_norm), posc_col, inv2, sgn)

    y_cmp, bias = _cmp_select(qn, kc, vcT, ovT, gates)
    y_sel = _nsa_flash("sel", qn, ksn, vsT, gates, shift_sel, bias)
    y_win = _nsa_flash("win", qn, kwn, vwT, gates, shift_win)
    y_sb = _stick_breaking(sbq, sbk, sbvT)

    mk, mvT = _memkv(mem, row(mem_norm), w_mem_kv.astype(BF16), row(mem_k_norm))
    y_mem = _mem_attention(memq, mk, mvT)

    x2, h2 = _mix_out(y_cmp, y_sel, y_win, y_sb, y_mem, P, w_o_nsa.astype(BF16), w_o_sb.astype(BF16),
                      w_o_mem.astype(BF16), w_out.astype(BF16), x, row(ffn_norm))
    return _ffn(h2, w_ffn_gate, w_ffn_up, w_ffn_down, x2)


def kernel(x, mem, positions, attn_norm, w_in, nsa_q_norm, nsa_kc_norm, nsa_ks_norm, nsa_kw_norm, cmp_k_pe, cmp_k_w1, cmp_k_w2, cmp_v_pe, cmp_v_w1, cmp_v_w2, mem_norm, w_mem_kv, mem_q_norm, mem_k_norm, w_o_nsa, w_o_sb, w_o_mem, w_out, ffn_norm, w_ffn_gate, w_ffn_up, w_ffn_down):
    B, T, D = x.shape
    assert T % (4 * LANE) == 0 and T // SEL_BLOCK >= 8
    n_pad = T // CMP_STRIDE
    n_sel = T // SEL_BLOCK
    inv = 1.0 / (ROPE_THETA ** (jnp.arange(0, HEAD_DIM, 2, dtype=F32) / HEAD_DIM))
    inv2 = jnp.concatenate([inv, inv]).reshape(1, HEAD_DIM)
    sgn = jnp.concatenate([-jnp.ones((HEAD_DIM // 2,), F32), jnp.ones((HEAD_DIM // 2,), F32)]).reshape(1, HEAD_DIM)
    cs = np.arange(n_pad)[None, :] * CMP_STRIDE
    ss = np.arange(n_sel)[:, None] * SEL_BLOCK
    ovT = jnp.asarray(((cs < ss + SEL_BLOCK) & (cs + CMP_LEN - 1 >= ss)).astype(np.float32), BF16)
    consts = (inv2, sgn, ovT)
    depth = w_in.shape[0]
    outs = []
    for b in range(B):
        xb = x[b]
        posf = positions[b].astype(F32)
        pos_col = posf.reshape(T, 1)
        posc = jnp.concatenate([posf[CMP_LEN - 1::CMP_STRIDE], posf[-1:]]).reshape(n_pad, 1)
        for l in range(depth):
            xb = _layer(xb, mem[b], pos_col, posc, consts, attn_norm[l], w_in[l], nsa_q_norm[l], nsa_kc_norm[l],
                        nsa_ks_norm[l], nsa_kw_norm[l], cmp_k_pe[l], cmp_k_w1[l], cmp_k_w2[l], cmp_v_pe[l],
                        cmp_v_w1[l], cmp_v_w2[l], mem_norm[l], w_mem_kv[l], mem_q_norm[l], mem_k_norm[l],
                        w_o_nsa[l], w_o_sb[l], w_o_mem[l], w_out[l], ffn_norm[l],
                        w_ffn_gate[l], w_ffn_up[l], w_ffn_down[l])
        outs.append(xb)
    return outs[0][None] if B == 1 else jnp.stack(outs, axis=0)
```

```python
import functools

import numpy as np
import jax
import jax.numpy as jnp
from jax import lax
from jax.experimental import pallas as pl
from jax.experimental.pallas import tpu as pltpu

HEAD_DIM = 128
NSA_HEADS = 8
NSA_GROUPS = 2
NSA_HPG = NSA_HEADS // NSA_GROUPS
SB_HEADS = 4
MEM_HEADS = 4
CMP_LEN = 32
CMP_STRIDE = 16
CMP_HIDDEN = 2 * HEAD_DIM
SEL_BLOCK = 64
SEL_TOPK = 16
WINDOW = 512
ROPE_THETA = 10000.0
NORM_EPS = 1e-6
NEG_BIG = -1e30
N_BRANCH = 3
SCALE = HEAD_DIM ** -0.5
LOG2E = 1.4426950408889634
QSCALE = SCALE * LOG2E
SOFTMAX_BOUND_MAX = 50.0
CMP_BUCKETS = 8
SB_NEAR_TILES = 3
SB_DEAD_LOG2 = -160.0
VAUG = HEAD_DIM + 16

LANE = 128
MIB = 1 << 20
BF16 = jnp.bfloat16
F32 = jnp.float32

_NT = (((1,), (1,)), ((), ()))


def _tile(n, pref):
    t = min(n, pref)
    assert n % t == 0, (n, pref)
    return t


def _params(sem, vmem_mib):
    return pltpu.CompilerParams(dimension_semantics=sem, vmem_limit_bytes=vmem_mib * MIB)


def _rms(x, gain):
    return x * lax.rsqrt(jnp.mean(x * x, axis=-1, keepdims=True) + NORM_EPS) * gain


def _rope_tables(pos, inv2, sgn):
    ang = pos * inv2
    return jnp.cos(ang), jnp.sin(ang) * sgn


def _rope(x, c, s):
    return x * c + pltpu.roll(x, HEAD_DIM // 2, 1) * s


def _proj_kernel(x_ref, g_ref, w_ref, wg_ref, o_ref, og_ref, hn_ref):
    @pl.when(pl.program_id(1) == 0)
    def _():
        hn_ref[...] = _rms(x_ref[...], g_ref[...]).astype(BF16)
        og_ref[...] = jnp.dot(hn_ref[...], wg_ref[...], preferred_element_type=F32)

    o_ref[...] = jnp.dot(hn_ref[...], w_ref[...], preferred_element_type=F32).astype(BF16)


def _in_proj(x, gain, w_main, w_gate):
    T, D = x.shape
    N = w_main.shape[1]
    tm = _tile(T, 1024)
    tn = 1536
    assert N % tn == 0
    return pl.pallas_call(
        _proj_kernel,
        grid=(T // tm, N // tn),
        in_specs=[
            pl.BlockSpec((tm, D), lambda i, j: (i, 0)),
            pl.BlockSpec((1, D), lambda i, j: (0, 0)),
            pl.BlockSpec((D, tn), lambda i, j: (0, j)),
            pl.BlockSpec((D, LANE), lambda i, j: (0, 0)),
        ],
        out_specs=[pl.BlockSpec((tm, tn), lambda i, j: (i, j)), pl.BlockSpec((tm, LANE), lambda i, j: (i, 0))],
        out_shape=[jax.ShapeDtypeStruct((T, N), BF16), jax.ShapeDtypeStruct((T, LANE), F32)],
        scratch_shapes=[pltpu.VMEM((tm, D), BF16)],
        compiler_params=_params(("parallel", "arbitrary"), 52),
        name="in_proj",
    )(x, gain, w_main, w_gate)


def _prep_kernel(pos_ref, inv_ref, sgn_ref, gq_ref, gks_ref, gkw_ref, gmq_ref,
                 q_ref, kc_ref, vc_ref, ks_ref, vs_ref, kw_ref, vw_ref,
                 sq_ref, sk_ref, sv_ref, mq_ref, gn_ref,
                 qn_o, kc_o, vc_o, ks_o, kw_o, vsT_o, vwT_o, sq_o, sk_o, svT_o, mq_o, gT_o, qn2_o, kn2_o):
    c, s = _rope_tables(pos_ref[...], inv_ref[...], sgn_ref[...])
    hd = HEAD_DIM
    tp = pos_ref.shape[0]
    f32 = lambda ref, sl: ref[:, sl].astype(F32)
    ones8 = jnp.ones((8, hd), F32)

    def sqnorm_rows(xb):
        x = xb.astype(F32)
        return lax.dot_general(ones8, x * x, _NT, preferred_element_type=F32)

    for h in range(NSA_HEADS):
        sl = slice(h * hd, (h + 1) * hd)
        qb = (_rope(_rms(f32(q_ref, sl), gq_ref[...]), c, s) * QSCALE).astype(BF16)
        qn_o[:, sl] = qb
        qn2_o[h * 8:(h + 1) * 8, :] = sqnorm_rows(qb)
    ones_rows = (lax.broadcasted_iota(jnp.int32, (VAUG - hd, tp), 0) == 0).astype(F32).astype(BF16)
    for g in range(NSA_GROUPS):
        sl = slice(g * hd, (g + 1) * hd)
        kc_o[g] = kc_ref[:, sl]
        vc_o[g] = vc_ref[:, sl]
        ksb = _rope(_rms(f32(ks_ref, sl), gks_ref[...]), c, s).astype(BF16)
        kwb = _rope(_rms(f32(kw_ref, sl), gkw_ref[...]), c, s).astype(BF16)
        ks_o[:, sl] = ksb
        kw_o[:, sl] = kwb
        kn2_o[g * 8:(g + 1) * 8, :] = sqnorm_rows(ksb)
        kn2_o[(NSA_GROUPS + g) * 8:(NSA_GROUPS + g + 1) * 8, :] = sqnorm_rows(kwb)
        vsT_o[g * VAUG:g * VAUG + hd, :] = f32(vs_ref, sl).T.astype(BF16)
        vsT_o[g * VAUG + hd:(g + 1) * VAUG, :] = ones_rows
        vwT_o[g * VAUG:g * VAUG + hd, :] = f32(vw_ref, sl).T.astype(BF16)
        vwT_o[g * VAUG + hd:(g + 1) * VAUG, :] = ones_rows
    for h in range(SB_HEADS):
        sl = slice(h * hd, (h + 1) * hd)
        sq_o[:, sl] = (f32(sq_ref, sl) * QSCALE).astype(BF16)
        svT_o[sl, :] = f32(sv_ref, sl).T.astype(BF16)
    sk_o[...] = sk_ref[...]
    for h in range(MEM_HEADS):
        sl = slice(h * hd, (h + 1) * hd)
        mq_o[:, sl] = (_rms(f32(mq_ref, sl), gmq_ref[...]) * QSCALE).astype(BF16)
    gT_o[...] = jax.nn.sigmoid(gn_ref[...]).T


def _prep(P, Pg, c0, pos_col, inv2, sgn, gq, gks, gkw, gmq):
    T = P.shape[0]
    tp = _tile(T, 512)
    hd = HEAD_DIM
    row = lambda w, c: pl.BlockSpec((tp, w), lambda i, c=c: (i, c))
    const = lambda: pl.BlockSpec((1, hd), lambda i: (0, 0))
    assert c0 % (8 * hd) == 0
    at = lambda w, col: row(w, (c0 + col) // w)
    in_specs = [pl.BlockSpec((tp, 1), lambda i: (i, 0)), const(), const(), const(), const(), const(), const(),
                at(8 * hd, 0),
                at(2 * hd, 8 * hd), at(2 * hd, 10 * hd),
                at(2 * hd, 12 * hd), at(2 * hd, 14 * hd),
                at(2 * hd, 16 * hd), at(2 * hd, 18 * hd),
                at(4 * hd, 20 * hd), at(4 * hd, 24 * hd), at(4 * hd, 28 * hd),
                at(4 * hd, 32 * hd),
                row(hd, 0)]
    colT = lambda w: pl.BlockSpec((w, tp), lambda i: (0, i))
    grp = lambda w: pl.BlockSpec((NSA_GROUPS, tp, w), lambda i: (0, i, 0))
    out_specs = [row(8 * hd, 0), grp(hd), grp(hd), row(2 * hd, 0),
                 row(2 * hd, 0), colT(NSA_GROUPS * VAUG), colT(NSA_GROUPS * VAUG),
                 row(4 * hd, 0), row(4 * hd, 0), colT(4 * hd), row(4 * hd, 0), colT(hd),
                 colT(NSA_HEADS * 8), colT(2 * NSA_GROUPS * 8)]
    sds = jax.ShapeDtypeStruct
    out_shape = [sds((T, 8 * hd), BF16),
                 sds((NSA_GROUPS, T, hd), BF16), sds((NSA_GROUPS, T, hd), BF16), sds((T, 2 * hd), BF16),
                 sds((T, 2 * hd), BF16),
                 sds((NSA_GROUPS * VAUG, T), BF16), sds((NSA_GROUPS * VAUG, T), BF16),
                 sds((T, 4 * hd), BF16), sds((T, 4 * hd), BF16), sds((4 * hd, T), BF16), sds((T, 4 * hd), BF16),
                 sds((hd, T), F32), sds((NSA_HEADS * 8, T), F32), sds((2 * NSA_GROUPS * 8, T), F32)]
    return pl.pallas_call(
        _prep_kernel, grid=(T // tp,), in_specs=in_specs, out_specs=out_specs, out_shape=out_shape,
        compiler_params=_params(("parallel",), 48), name="prep",
    )(pos_col, inv2, sgn, gq, gks, gkw, gmq, *([P] * 11), Pg)


def _gelu_tanh(x):
    return 0.5 * x * (1.0 + jnp.tanh(0.7978845608028654 * (x + 0.044715 * (x * x * x))))


def _compress_one(x, w1, pe, w2):
    n = x.shape[0]
    ab = jnp.dot(x, w1, preferred_element_type=F32)
    pr = jnp.dot(pe, w1, preferred_element_type=F32)
    pec = pr[0:1, :CMP_HIDDEN] + pr[1:2, CMP_HIDDEN:]
    hid = ab[:, :CMP_HIDDEN] + pltpu.roll(ab[:, CMP_HIDDEN:], n - 1, 0) + pec
    return jnp.dot(_gelu_tanh(hid).astype(BF16), w2, preferred_element_type=F32)


def _compress_kernel(xk_ref, xv_ref, w1k_ref, w1v_ref, pek_ref, pev_ref, w2k_ref, w2v_ref,
                     gk_ref, pos_ref, inv_ref, sgn_ref, kc_o, vcT_o):
    c, s = _rope_tables(pos_ref[...], inv_ref[...], sgn_ref[...])
    k = _compress_one(xk_ref[0], w1k_ref[...], pek_ref[...], w2k_ref[...])
    kc_o[0] = _rope(_rms(k, gk_ref[...]), c, s).astype(BF16)
    v = _compress_one(xv_ref[0], w1v_ref[...], pev_ref[...], w2v_ref[...])
    vcT_o[0] = v.T.astype(BF16)


def _compress(xk, xv, w1k, w1v, pek, pev, w2k, w2v, gk, posc, inv2, sgn):
    G, n, W = xk.shape
    full = lambda a: pl.BlockSpec(a.shape, lambda g: (0,) * a.ndim)
    grp = pl.BlockSpec((1, n, W), lambda g: (g, 0, 0))
    return pl.pallas_call(
        _compress_kernel, grid=(G,),
        in_specs=[grp, grp, full(w1k), full(w1v), full(pek), full(pev), full(w2k), full(w2v),
                  full(gk), full(posc), full(inv2), full(sgn)],
        out_specs=[pl.BlockSpec((1, n, HEAD_DIM), lambda g: (g, 0, 0)),
                   pl.BlockSpec((1, HEAD_DIM, n), lambda g: (g, 0, 0))],
        out_shape=[jax.ShapeDtypeStruct((G, n, HEAD_DIM), BF16), jax.ShapeDtypeStruct((G, HEAD_DIM, n), BF16)],
        compiler_params=_params(("parallel",), 48), name="compress",
    )(xk, xv, w1k, w1v, pek, pev, w2k, w2v, gk, posc, inv2, sgn)


def _memkv_kernel(mem_ref, gm_ref, w_ref, gk_ref, k_o, vT_o):
    hn = _rms(mem_ref[...], gm_ref[...]).astype(BF16)
    kv = jnp.dot(hn, w_ref[...], preferred_element_type=F32)
    mw = MEM_HEADS * HEAD_DIM
    for h in range(MEM_HEADS):
        sl = slice(h * HEAD_DIM, (h + 1) * HEAD_DIM)
        k_o[:, sl] = _rms(kv[:, sl], gk_ref[...]).astype(BF16)
        vT_o[sl, :] = kv[:, mw + h * HEAD_DIM: mw + (h + 1) * HEAD_DIM].T.astype(BF16)


def _memkv(mem, gm, w, gk):
    M = mem.shape[0]
    mw = MEM_HEADS * HEAD_DIM
    return pl.pallas_call(
        _memkv_kernel,
        out_shape=[jax.ShapeDtypeStruct((M, mw), BF16), jax.ShapeDtypeStruct((mw, M), BF16)],
        compiler_params=pltpu.CompilerParams(vmem_limit_bytes=32 * MIB), name="mem_kv",
    )(mem, gm, w, gk)


def _cmp_kernel(q_ref, kc_ref, vcT_ref, ovT_ref, g_ref, *rest, tq, i0, n_pad, n_sel, top_k):
    y_ref, b_ref = rest[-2:]
    i = i0 + pl.program_id(1)
    t = i * tq + lax.broadcasted_iota(jnp.int32, (1, tq), 1)
    n_end = lax.broadcasted_iota(jnp.int32, (n_pad, 1), 0) * CMP_STRIDE + (CMP_LEN - 1)
    valid = n_end <= t
    kc = kc_ref[0]
    vcT = vcT_ref[0]
    has_valid = (t >= CMP_LEN - 1).astype(F32)
    sms = []
    for h in range(NSA_HPG):
        q_h = q_ref[:, h * HEAD_DIM:(h + 1) * HEAD_DIM]
        sms.append(jnp.where(valid, lax.dot_general(kc, q_h, _NT, preferred_element_type=F32), NEG_BIG))
    psum = jnp.zeros((n_pad, tq), F32)
    for h in range(NSA_HPG):
        e = jnp.exp2(sms[h] - jnp.max(sms[h], axis=0, keepdims=True))
        p = e * (has_valid / jnp.sum(e, axis=0, keepdims=True))
        oT = jnp.dot(vcT, p.astype(BF16), preferred_element_type=F32)
        y_ref[:, h * HEAD_DIM:(h + 1) * HEAD_DIM] = (oT * g_ref[0, 0, h:h + 1, :]).T.astype(BF16)
        psum = psum + p
    hi = psum.astype(BF16)
    r1 = psum - hi.astype(F32)
    mid = r1.astype(BF16)
    lo = (r1 - mid.astype(F32)).astype(BF16)
    parts = jnp.dot(ovT_ref[...], jnp.concatenate([hi, mid, lo], axis=1), preferred_element_type=F32)
    imp = parts[:, :tq] + parts[:, tq:2 * tq] + parts[:, 2 * tq:]
    s_i = lax.broadcasted_iota(jnp.int32, (n_sel, 1), 0)
    cur = lax.shift_right_logical(t, 6)
    forced = (s_i == 0) | (s_i == cur) | (s_i == cur - 1)
    future = s_i * SEL_BLOCK > t
    w = jnp.where(forced, jnp.inf, jnp.where(future, -jnp.inf, imp))
    s_f = jnp.broadcast_to(s_i.astype(F32), (n_sel, tq))
    for _ in range(top_k):
        m = jnp.max(w, axis=0, keepdims=True)
        idx = jnp.min(jnp.where(w == m, s_f, float(n_sel)), axis=0, keepdims=True)
        w = jnp.where(s_f == idx, -jnp.inf, w)
    b_ref[0, :n_sel, :] = jnp.where(future, NEG_BIG, jnp.where(w == -jnp.inf, 0.0, NEG_BIG))
    n_all = b_ref.shape[1]
    if n_sel < n_all:
        b_ref[0, n_sel:, :] = jnp.full((n_all - n_sel, tq), NEG_BIG, F32)


def _round_up(n, m):
    return -(-n // m) * m


def _cmp_select(qn, kc, vcT, ovT, gates):
    T = qn.shape[0]
    G, n_pad, _ = kc.shape
    n_sel = T // SEL_BLOCK
    tq = _tile(T, 256)
    nq = T // tq
    gw = NSA_HPG * HEAD_DIM
    nb = min(CMP_BUCKETS, nq)
    assert nq % nb == 0
    per = nq // nb
    out_shape = [jax.ShapeDtypeStruct((T, NSA_HEADS * HEAD_DIM), BF16), jax.ShapeDtypeStruct((G, n_sel, T), F32)]
    buffers = ()
    for b in range(nb):
        i0 = b * per
        t_max = (i0 + per) * tq - 1
        n_len = min(n_pad, _round_up(max(t_max - (CMP_LEN - 1), 0) // CMP_STRIDE + 1, LANE))
        s_len = min(n_sel, _round_up(t_max // SEL_BLOCK + 1, 8))
        kern = functools.partial(_cmp_kernel, tq=tq, i0=i0, n_pad=n_len, n_sel=s_len, top_k=min(SEL_TOPK, n_sel))
        buffers = pl.pallas_call(
            kern, grid=(G, per),
            in_specs=[pl.BlockSpec((tq, gw), lambda g, i, i0=i0: (i0 + i, g)),
                      pl.BlockSpec((1, n_len, HEAD_DIM), lambda g, i: (g, 0, 0)),
                      pl.BlockSpec((1, HEAD_DIM, n_len), lambda g, i: (g, 0, 0)),
                      pl.BlockSpec((s_len, n_len), lambda g, i: (0, 0)),
                      pl.BlockSpec((1, 1, 8, tq), lambda g, i, i0=i0: (0, g, 0, i0 + i))]
            + [pl.BlockSpec(memory_space=pl.ANY)] * len(buffers),
            out_specs=[pl.BlockSpec((tq, gw), lambda g, i, i0=i0: (i0 + i, g)),
                       pl.BlockSpec((1, n_sel, tq), lambda g, i, i0=i0: (g, 0, i0 + i))],
            out_shape=out_shape,
            input_output_aliases={5: 0, 6: 1} if buffers else {},
            compiler_params=_params(("parallel", "parallel"), 48), name="cmp_select_%d" % b,
        )(qn, kc, vcT, ovT, gates, *buffers)
    return buffers


def _flash_kernel(qi_ref, kj_ref, fl_ref, q_ref, k_ref, vT_ref, g_ref, sh_ref, *rest, mode, bounded, tq, tk):
    if mode == "sel":
        bias_ref, o_ref, m_sc, acc_sc = rest
    else:
        o_ref, m_sc, acc_sc = rest
    p = pl.program_id(1)
    i = qi_ref[p]
    j = kj_ref[p]
    fl = fl_ref[p]

    @pl.when((fl & 1) != 0)
    def _():
        m_sc[...] = jnp.full(m_sc.shape, NEG_BIG, F32)
        acc_sc[...] = jnp.zeros(acc_sc.shape, F32)

    shift = sh_ref[0, 0:1, :] if bounded else 0.0
    nb = tk // SEL_BLOCK
    if mode == "sel":
        rows = bias_ref[0] - shift

    def key_mask():
        t = i * tq + lax.broadcasted_iota(jnp.int32, (1, tq), 1)
        kpos = j * tk + lax.broadcasted_iota(jnp.int32, (tk, 1), 0)
        if mode == "sel":
            base = jnp.concatenate(
                [jnp.broadcast_to(rows[b:b + 1, :], (SEL_BLOCK, tq)) for b in range(nb)], axis=0)
            return jnp.where(kpos <= t, base, NEG_BIG)
        return jnp.where((kpos <= t) & (kpos > t - WINDOW), 0.0 - shift, NEG_BIG)

    k = k_ref[...]
    vT = vT_ref[...]
    raw = lambda h: lax.dot_general(k, q_ref[:, h * HEAD_DIM:(h + 1) * HEAD_DIM], _NT, preferred_element_type=F32)

    def accumulate(pTs):
        for h in range(NSA_HPG):
            acc_sc[h] = acc_sc[h] + jnp.dot(vT, pTs[h], preferred_element_type=F32)

    if bounded and mode == "sel":
        @pl.when((fl & 4) != 0)
        def _():
            mask_add = key_mask()
            accumulate([jnp.exp2(raw(h) + mask_add).astype(BF16) for h in range(NSA_HPG)])

        @pl.when((fl & 4) == 0)
        def _():
            def probs(h):
                s = raw(h).reshape(nb, SEL_BLOCK, tq) + rows[:, None, :]
                return jnp.exp2(s).astype(BF16).reshape(tk, tq)
            accumulate([probs(h) for h in range(NSA_HPG)])
    elif bounded:
        mask_add = key_mask()
        accumulate([jnp.exp2(raw(h) + mask_add).astype(BF16) for h in range(NSA_HPG)])
    else:
        mask_add = key_mask()
        sTs = [raw(h) + mask_add for h in range(NSA_HPG)]
        pTs, alphas = [], []
        for h in range(NSA_HPG):
            sT = sTs[h]
            m_old = m_sc[h]
            m_new = jnp.maximum(m_old, jnp.max(sT, axis=0, keepdims=True))
            alphas.append(jnp.exp2(m_old - m_new))
            pTs.append(jnp.exp2((sT - m_new).astype(BF16)))
            m_sc[h] = m_new
        for h in range(NSA_HPG):
            acc_sc[h] = alphas[h] * acc_sc[h] + jnp.dot(vT, pTs[h], preferred_element_type=F32)

    @pl.when((fl & 2) != 0)
    def _():
        for h in range(NSA_HPG):
            acc = acc_sc[h]
            l = acc[HEAD_DIM:HEAD_DIM + 1, :]
            o = acc[:HEAD_DIM, :] * ((1.0 / l) * g_ref[0, 0, h:h + 1, :])
            o_ref[:, h * HEAD_DIM:(h + 1) * HEAD_DIM] = o.T.astype(BF16)


def _steps(nq, lo_fn, hi_fn, reverse=False):
    qi, kj, fl = [], [], []
    for i in range(nq):
        js = list(range(lo_fn(i), hi_fn(i) + 1))
        if reverse:
            js = js[::-1]
        for n, j in enumerate(js):
            qi.append(i)
            kj.append(j)
            fl.append((1 if n == 0 else 0) | (2 if n == len(js) - 1 else 0))
    return (jnp.asarray(np.array(qi, np.int32)), jnp.asarray(np.array(kj, np.int32)),
            jnp.asarray(np.array(fl, np.int32)))


def _nsa_flash(mode, qn, k, vT, gates, shift, bias=None):
    use_bounded = jnp.max(shift) <= SOFTMAX_BOUND_MAX
    return lax.cond(use_bounded,
                    lambda: _nsa_flash_call(mode, True, qn, k, vT, gates, shift, bias),
                    lambda: _nsa_flash_call(mode, False, qn, k, vT, gates, shift, bias))


def _nsa_flash_call(mode, bounded, qn, k, vT, gates, shift, bias):
    T = qn.shape[0]
    tq = _tile(T, 1024)
    tk = _tile(T, 1024 if mode == "sel" else 512)
    nq = T // tq
    gw = NSA_HPG * HEAD_DIM
    hi = lambda i: ((i + 1) * tq - 1) // tk
    if mode == "sel":
        lo = lambda i: 0
        br = 1
    else:
        lo = lambda i: max(0, (i * tq - (WINDOW - 1)) // tk)
        br = 2
    qi, kj, fl = _steps(nq, lo, hi)
    fl = fl | 4 * ((kj + 1) * tk - 1 > qi * tq).astype(jnp.int32)
    in_specs = [pl.BlockSpec((tq, gw), lambda g, p, qi, kj, fl: (qi[p], g)),
                pl.BlockSpec((tk, HEAD_DIM), lambda g, p, qi, kj, fl: (kj[p], g)),
                pl.BlockSpec((VAUG, tk), lambda g, p, qi, kj, fl: (g, kj[p])),
                pl.BlockSpec((1, 1, 8, tq), lambda g, p, qi, kj, fl, br=br: (br, g, 0, qi[p])),
                pl.BlockSpec((1, 8, tq), lambda g, p, qi, kj, fl: (g, 0, qi[p]))]
    args = [qn, k, vT, gates, shift]
    if mode == "sel":
        in_specs.append(pl.BlockSpec((1, tk // SEL_BLOCK, tq), lambda g, p, qi, kj, fl: (g, kj[p], qi[p])))
        args.append(bias)
    kern = functools.partial(_flash_kernel, mode=mode, bounded=bounded, tq=tq, tk=tk)
    return pl.pallas_call(
        kern,
        grid_spec=pltpu.PrefetchScalarGridSpec(
            num_scalar_prefetch=3, grid=(NSA_GROUPS, int(qi.shape[0])),
            in_specs=in_specs,
            out_specs=pl.BlockSpec((tq, gw), lambda g, p, qi, kj, fl: (qi[p], g)),
            scratch_shapes=[pltpu.VMEM((NSA_HPG, 1, tq), F32), pltpu.VMEM((NSA_HPG, VAUG, tq), F32)]),
        out_shape=jax.ShapeDtypeStruct((T, NSA_HEADS * HEAD_DIM), BF16),
        compiler_params=_params(("parallel", "arbitrary"), 48),
        name="nsa_" + mode + ("_bounded" if bounded else "_online"),
    )(qi, kj, fl, *args)


def _sb_kernel(qi_ref, kj_ref, fl_ref, q_ref, k_ref, vT_ref, lm_ref, *rest, tq, tk, resume):
    if resume:
        acc_in, carry_in, o_ref, carry_sc, acc_sc = rest
    else:
        o_ref, acc_o, carry_o, carry_sc, acc_sc = rest
    p = pl.program_id(0)
    i = qi_ref[p]
    j = kj_ref[p]
    fl = fl_ref[p]

    @pl.when((fl & 1) != 0)
    def _():
        if resume:
            for h in range(SB_HEADS):
                carry_sc[h] = carry_in[h:h + 1, :]
                acc_sc[h] = acc_in[h * HEAD_DIM:(h + 1) * HEAD_DIM, :]
        else:
            carry_sc[...] = jnp.zeros(carry_sc.shape, F32)
            acc_sc[...] = jnp.zeros(acc_sc.shape, F32)

    @pl.when((jnp.max(carry_sc[...]) > SB_DEAD_LOG2) & ((fl & 4) == 0))
    def _():
        t = i * tq + lax.broadcasted_iota(jnp.int32, (1, tq), 1)
        kpos = j * tk + lax.broadcasted_iota(jnp.int32, (tk, 1), 0)
        past = kpos < t
        lm = lm_ref[...]
        hs = [slice(h * HEAD_DIM, (h + 1) * HEAD_DIM) for h in range(SB_HEADS)]
        zs = [lax.dot_general(k_ref[:, sl], q_ref[:, sl], _NT, preferred_element_type=F32) for sl in hs]
        lgs, css = [], []
        for h in range(SB_HEADS):
            z = zs[h]
            sp = jnp.maximum(z, 0.0) + jnp.log2(1.0 + jnp.exp2(-jnp.abs(z)))
            lk = jnp.where(past, -sp, 0.0)
            hi = lk.astype(BF16)
            lo = (lk - hi.astype(F32)).astype(BF16)
            css.append(jnp.dot(lm, jnp.concatenate([hi, lo], axis=1), preferred_element_type=F32))
            lgs.append(z - sp)
            carry_old = carry_sc[h]
            carry_sc[h] = carry_old + jnp.sum(lk, axis=0, keepdims=True)
            css[h] = css[h][:, :tq] + css[h][:, tq:] + carry_old
        for h in range(SB_HEADS):
            wgt = jnp.where(past, jnp.exp2(lgs[h] + css[h]), 0.0)
            acc_sc[h] = acc_sc[h] + jnp.dot(vT_ref[hs[h], :], wgt.astype(BF16), preferred_element_type=F32)

    @pl.when((fl & 2) != 0)
    def _():
        for h in range(SB_HEADS):
            o_ref[:, h * HEAD_DIM:(h + 1) * HEAD_DIM] = acc_sc[h].T.astype(BF16)
            if not resume:
                acc_o[h * HEAD_DIM:(h + 1) * HEAD_DIM, :] = acc_sc[h]
                carry_o[h:h + 1, :] = carry_sc[h]
        if not resume:
            carry_o[SB_HEADS:, :] = jnp.zeros((8 - SB_HEADS, tq), F32)


def _sb_call(q, k, vT, steps, tq, tk, state=None):
    T, W = q.shape
    qi, kj, fl = steps
    lmat = jnp.asarray(np.triu(np.ones((tk, tk), np.float32), 1), BF16)
    resume = state is not None
    qtile = lambda shape: pl.BlockSpec(shape, lambda p, qi, kj, fl: (qi[p], 0))
    qtileT = lambda rows: pl.BlockSpec((rows, tq), lambda p, qi, kj, fl: (0, qi[p]))
    in_specs = [qtile((tq, W)),
                pl.BlockSpec((tk, W), lambda p, qi, kj, fl: (kj[p], 0)),
                pl.BlockSpec((W, tk), lambda p, qi, kj, fl: (0, kj[p])),
                pl.BlockSpec((tk, tk), lambda p, qi, kj, fl: (0, 0))]
    y_sds = jax.ShapeDtypeStruct((T, W), BF16)
    if resume:
        in_specs += [qtileT(W), qtileT(8)]
        out_specs, out_shape = qtile((tq, W)), y_sds
    else:
        out_specs = [qtile((tq, W)), qtileT(W), qtileT(8)]
        out_shape = [y_sds, jax.ShapeDtypeStruct((W, T), F32), jax.ShapeDtypeStruct((8, T), F32)]
    return pl.pallas_call(
        functools.partial(_sb_kernel, tq=tq, tk=tk, resume=resume),
        grid_spec=pltpu.PrefetchScalarGridSpec(
            num_scalar_prefetch=3, grid=(int(qi.shape[0]),),
            in_specs=in_specs, out_specs=out_specs,
            scratch_shapes=[pltpu.VMEM((SB_HEADS, 1, tq), F32), pltpu.VMEM((SB_HEADS, HEAD_DIM, tq), F32)]),
        out_shape=out_shape,
        compiler_params=_params(("arbitrary",), 48), name="sb_far" if resume else "sb_near",
    )(qi, kj, fl, q, k, vT, lmat, *(state or ()))


def _stick_breaking(q, k, vT):
    T, W = q.shape
    tq = _tile(T, 512)
    tk = _tile(T, 256)
    nq = T // tq
    hi = lambda i: ((i + 1) * tq - 2) // tk
    lo_near = lambda i: max(0, hi(i) - SB_NEAR_TILES + 1)
    near = _steps(nq, lo_near, hi, reverse=True)
    y_near, acc, carry = _sb_call(q, k, vT, near, tq, tk)
    qi, kj, fl = [], [], []
    for i in range(nq):
        js = list(range(lo_near(i) - 1, -1, -1))
        for n, j in enumerate(js or [0]):
            qi.append(i)
            kj.append(j)
            fl.append((1 if n == 0 else 0) | (2 if n == max(len(js), 1) - 1 else 0) | (0 if js else 4))
    far = tuple(jnp.asarray(np.array(a, np.int32)) for a in (qi, kj, fl))
    with_far = [i for i in range(nq) if lo_near(i) > 0]
    if not with_far:
        return y_near
    alive = jnp.max(carry[:SB_HEADS, with_far[0] * tq:]) > SB_DEAD_LOG2
    return lax.cond(alive, lambda: _sb_call(q, k, vT, far, tq, tk, state=(acc, carry)), lambda: y_near)


def _memattn_kernel(q_ref, k_ref, vT_ref, o_ref):
    for h in range(MEM_HEADS):
        sl = slice(h * HEAD_DIM, (h + 1) * HEAD_DIM)
        sT = lax.dot_general(k_ref[:, sl], q_ref[:, sl], _NT, preferred_element_type=F32)
        e = jnp.exp2(sT - jnp.max(sT, axis=0, keepdims=True))
        l = jnp.sum(e, axis=0, keepdims=True)
        oT = jnp.dot(vT_ref[sl, :], e.astype(BF16), preferred_element_type=F32) * (1.0 / l)
        o_ref[:, sl] = oT.T.astype(BF16)


def _mem_attention(q, k, vT):
    T, W = q.shape
    M = k.shape[0]
    tq = _tile(T, 512)
    return pl.pallas_call(
        _memattn_kernel, grid=(T // tq,),
        in_specs=[pl.BlockSpec((tq, W), lambda i: (i, 0)),
                  pl.BlockSpec((M, W), lambda i: (0, 0)),
                  pl.BlockSpec((W, M), lambda i: (0, 0))],
        out_specs=pl.BlockSpec((tq, W), lambda i: (i, 0)),
        out_shape=jax.ShapeDtypeStruct((T, W), BF16),
        compiler_params=_params(("parallel",), 32), name="mem_attention",
    )(q, k, vT)


def _mixout_kernel(yc_ref, ys_ref, yw_ref, ysb_ref, ym_ref, g_ref, wn_ref, wsb_ref, wm_ref, wo_ref, x_ref, gain_ref,
                   x2_o, h2_o):
    D = x_ref.shape[1]
    yn = (yc_ref[...].astype(F32) + ys_ref[...].astype(F32) + yw_ref[...].astype(F32)).astype(BF16)
    a = jnp.dot(yn, wn_ref[...], preferred_element_type=F32)
    b = jnp.dot(ysb_ref[...], wsb_ref[...], preferred_element_type=F32)
    c = jnp.dot(ym_ref[...], wm_ref[...], preferred_element_type=F32)
    sig = lambda n: jax.nn.sigmoid(g_ref[:, n * D:(n + 1) * D].astype(F32))
    mixed = (sig(0) * a + sig(1) * b + sig(2) * c).astype(BF16)
    x2 = x_ref[...] + jnp.dot(mixed, wo_ref[...], preferred_element_type=F32)
    x2_o[...] = x2
    h2_o[...] = _rms(x2, gain_ref[...]).astype(BF16)


def _mix_out(yc, ys, yw, ysb, ym, P, wn, wsb, wm, wo, x, gain):
    T, D = x.shape
    tm = _tile(T, 256)
    rowi = lambda w: pl.BlockSpec((tm, w), lambda i: (i, 0))
    res = lambda a: pl.BlockSpec(a.shape, lambda i: (0, 0), pipeline_mode=pl.Buffered(1))
    return pl.pallas_call(
        _mixout_kernel, grid=(T // tm,),
        in_specs=[rowi(yc.shape[1]), rowi(ys.shape[1]), rowi(yw.shape[1]), rowi(ysb.shape[1]), rowi(ym.shape[1]),
                  rowi(N_BRANCH * D), res(wn), res(wsb), res(wm), res(wo), rowi(D), res(gain)],
        out_specs=[rowi(D), rowi(D)],
        out_shape=[jax.ShapeDtypeStruct((T, D), F32), jax.ShapeDtypeStruct((T, D), BF16)],
        compiler_params=_params(("parallel",), 48), name="mix_out",
    )(yc, ys, yw, ysb, ym, P, wn, wsb, wm, wo, x, gain)


def _ffn_kernel(h_ref, wg_ref, wu_ref, wd_ref, x_ref, o_ref, z_sc, *, nf):
    f = pl.program_id(1)

    def up():
        a = jnp.dot(h_ref[...], wg_ref[...], preferred_element_type=F32)
        b = jnp.dot(h_ref[...], wu_ref[...], preferred_element_type=F32)
        return (a * jax.nn.sigmoid(a) * b).astype(BF16)

    def down():
        return jnp.dot(z_sc[...], wd_ref[...], preferred_element_type=F32)

    @pl.when(f == 0)
    def _():
        z_sc[...] = up()

    @pl.when((f > 0) & (f < nf))
    def _():
        c = down()
        z_new = up()
        o_ref[...] = jnp.where(f == 1, x_ref[...], o_ref[...]) + c
        z_sc[...] = z_new

    @pl.when(f == nf)
    def _():
        o_ref[...] += down()


def _ffn(h2, wg, wu, wd, x2):
    T, D = x2.shape
    F = wg.shape[1]
    tm = _tile(T, 1024)
    tf = 512
    assert F % tf == 0
    nf = F // tf
    assert nf >= 2
    return pl.pallas_call(
        functools.partial(_ffn_kernel, nf=nf), grid=(T // tm, nf + 1),
        in_specs=[pl.BlockSpec((tm, D), lambda i, f: (i, 0)),
                  pl.BlockSpec((D, tf), lambda i, f: (0, jnp.minimum(f, nf - 1))),
                  pl.BlockSpec((D, tf), lambda i, f: (0, jnp.minimum(f, nf - 1))),
                  pl.BlockSpec((tf, D), lambda i, f: (jnp.maximum(f - 1, 0), 0)),
                  pl.BlockSpec((tm, D), lambda i, f: (i, 0), pipeline_mode=pl.Buffered(1))],
        out_specs=pl.BlockSpec((tm, D), lambda i, f: (i, 0)),
        out_shape=jax.ShapeDtypeStruct((T, D), F32),
        scratch_shapes=[pltpu.VMEM((tm, tf), BF16)],
        compiler_params=_params(("parallel", "arbitrary"), 56), name="ffn",
    )(h2, wg.astype(BF16), wu.astype(BF16), wd.astype(BF16), x2)


def _layer(x, mem, pos_col, posc_col, consts, attn_norm, w_in, nsa_q_norm, nsa_kc_norm, nsa_ks_norm, nsa_kw_norm,
           cmp_k_pe, cmp_k_w1, cmp_k_w2, cmp_v_pe, cmp_v_w1, cmp_v_w2, mem_norm, w_mem_kv,
           mem_q_norm, mem_k_norm, w_o_nsa, w_o_sb, w_o_mem, w_out, ffn_norm,
           w_ffn_gate, w_ffn_up, w_ffn_down):
    T, D = x.shape
    inv2, sgn, ovT = consts
    hd = HEAD_DIM
    row = lambda g: g.reshape(1, -1)

    q_w, kv_w, gn_w = NSA_HEADS * hd, 6 * NSA_GROUPS * hd, 3 * NSA_HEADS
    sb_w, mq_w, gm_w = 3 * SB_HEADS * hd, MEM_HEADS * hd, N_BRANCH * D
    o_gn = q_w + kv_w
    o_sb = o_gn + gn_w
    o_mq = o_sb + sb_w
    o_gm = o_mq + mq_w
    assert w_in.shape[1] == o_gm + gm_w
    w_main = jnp.concatenate([w_in[:, o_gm:], w_in[:, :o_gn], w_in[:, o_sb:o_gm]], axis=1).astype(BF16)
    w_gate = jnp.pad(w_in[:, o_gn:o_sb], ((0, 0), (0, LANE - gn_w))).astype(BF16)
    P, Pg = _in_proj(x, row(attn_norm), w_main, w_gate)

    (qn, kc_raw, vc_raw, ksn, kwn, vsT, vwT, sbq, sbk, sbvT, memq, gT, qn2, kn2) = _prep(
        P, Pg, gm_w, pos_col, inv2, sgn, row(nsa_q_norm), row(nsa_ks_norm), row(nsa_kw_norm), row(mem_q_norm))

    q2 = jnp.max(qn2[::8].reshape(NSA_GROUPS, NSA_HPG, T), axis=1)
    k2 = jnp.max(kn2[::8], axis=1).reshape(2, NSA_GROUPS)
    bound = lambda kk: jnp.broadcast_to((1.02 * jnp.sqrt(q2 * kk[:, None]))[:, None, :], (NSA_GROUPS, 8, T))
    shift_sel, shift_win = bound(k2[0]), bound(k2[1])

    gates = gT[:gn_w].reshape(NSA_GROUPS, NSA_HPG, 3, T).transpose(2, 0, 1, 3)
    gates = jnp.pad(gates, ((0, 0), (0, 0), (0, 8 - NSA_HPG), (0, 0)))

    n_pad = T // CMP_STRIDE
    half = CMP_LEN // 2

    def w1_pack(w1):
        return jnp.concatenate([w1[:half].reshape(half * hd, -1), w1[half:].reshape(half * hd, -1)], axis=1).astype(BF16)

    def pe_pack(pe):
        return jnp.pad(pe.reshape(2, half * hd), ((0, 6), (0, 0))).astype(BF16)

    kc, vcT = _compress(
        kc_raw.reshape(NSA_GROUPS, n_pad, CMP_STRIDE * hd), vc_raw.reshape(NSA_GROUPS, n_pad, CMP_STRIDE * hd),
        w1_pack(cmp_k_w1), w1_pack(cmp_v_w1), pe_pack(cmp_k_pe), pe_pack(cmp_v_pe),
        cmp_k_w2.astype(BF16), cmp_v_w2.astype(BF16), row(nsa_kc_norm), posc_col, inv2, sgn)

    y_cmp, bias = _cmp_select(qn, kc, vcT, ovT, gates)
    y_sel = _nsa_flash("sel", qn, ksn, vsT, gates, shift_sel, bias)
    y_win = _nsa_flash("win", qn, kwn, vwT, gates, shift_win)
    y_sb = _stick_breaking(sbq, sbk, sbvT)

    mk, mvT = _memkv(mem, row(mem_norm), w_mem_kv.astype(BF16), row(mem_k_norm))
    y_mem = _mem_attention(memq, mk, mvT)

    x2, h2 = _mix_out(y_cmp, y_sel, y_win, y_sb, y_mem, P, w_o_nsa.astype(BF16), w_o_sb.astype(BF16),
                      w_o_mem.astype(BF16), w_out.astype(BF16), x, row(ffn_norm))
    return _ffn(h2, w_ffn_gate, w_ffn_up, w_ffn_down, x2)


def kernel(x, mem, positions, attn_norm, w_in, nsa_q_norm, nsa_kc_norm, nsa_ks_norm, nsa_kw_norm, cmp_k_pe, cmp_k_w1, cmp_k_w2, cmp_v_pe, cmp_v_w1, cmp_v_w2, mem_norm, w_mem_kv, mem_q_norm, mem_k_norm, w_o_nsa, w_o_sb, w_o_mem, w_out, ffn_norm, w_ffn_gate, w_ffn_up, w_ffn_down):
    B, T, D = x.shape
    assert T % (4 * LANE) == 0 and T // SEL_BLOCK >= 8
    n_pad = T // CMP_STRIDE
    n_sel = T // SEL_BLOCK
    inv = 1.0 / (ROPE_THETA ** (jnp.arange(0, HEAD_DIM, 2, dtype=F32) / HEAD_DIM))
    inv2 = jnp.concatenate([inv, inv]).reshape(1, HEAD_DIM)
    sgn = jnp.concatenate([-jnp.ones((HEAD_DIM // 2,), F32), jnp.ones((HEAD_DIM // 2,), F32)]).reshape(1, HEAD_DIM)
    cs = np.arange(n_pad)[None, :] * CMP_STRIDE
    ss = np.arange(n_sel)[:, None] * SEL_BLOCK
    ovT = jnp.asarray(((cs < ss + SEL_BLOCK) & (cs + CMP_LEN - 1 >= ss)).astype(np.float32), BF16)
    consts = (inv2, sgn, ovT)
    depth = w_in.shape[0]
    outs = []
    for b in range(B):
        xb = x[b]
        posf = positions[b].astype(F32)
        pos_col = posf.reshape(T, 1)
        posc = jnp.concatenate([posf[CMP_LEN - 1::CMP_STRIDE], posf[-1:]]).reshape(n_pad, 1)
        for l in range(depth):
            xb = _layer(xb, mem[b], pos_col, posc, consts, attn_norm[l], w_in[l], nsa_q_norm[l], nsa_kc_norm[l],
                        nsa_ks_norm[l], nsa_kw_norm[l], cmp_k_pe[l], cmp_k_w1[l], cmp_k_w2[l], cmp_v_pe[l],
                        cmp_v_w1[l], cmp_v_w2[l], mem_norm[l], w_mem_kv[l], mem_q_norm[l], mem_k_norm[l],
                        w_o_nsa[l], w_o_sb[l], w_o_mem[l], w_out[l], ffn_norm[l],
                        w_ffn_gate[l], w_ffn_up[l], w_ffn_down[l])
        outs.append(xb)
    return outs[0][None] if B == 1 else jnp.stack(outs, axis=0)
```

```python
import functools

import numpy as np
import jax
import jax.numpy as jnp
from jax import lax
from jax.experimental import pallas as pl
from jax.experimental.pallas import tpu as pltpu

HEAD_DIM = 128
NSA_HEADS = 8
NSA_GROUPS = 2
NSA_HPG = NSA_HEADS // NSA_GROUPS
SB_HEADS = 4
MEM_HEADS = 4
CMP_LEN = 32
CMP_STRIDE = 16
CMP_HIDDEN = 2 * HEAD_DIM
SEL_BLOCK = 64
SEL_TOPK = 16
WINDOW = 512
ROPE_THETA = 10000.0
NORM_EPS = 1e-6
NEG_BIG = -1e30
N_BRANCH = 3
SCALE = HEAD_DIM ** -0.5
LOG2E = 1.4426950408889634
QSCALE = SCALE * LOG2E
SOFTMAX_BOUND_MAX = 50.0
CMP_BUCKETS = 8
SB_NEAR_TILES = 3
SB_DEAD_LOG2 = -160.0
VAUG = HEAD_DIM + 16

LANE = 128
MIB = 1 << 20
BF16 = jnp.bfloat16
F32 = jnp.float32

_NT = (((1,), (1,)), ((), ()))


def _tile(n, pref):
    t = min(n, pref)
    assert n % t == 0, (n, pref)
    return t


def _params(sem, vmem_mib):
    return pltpu.CompilerParams(dimension_semantics=sem, vmem_limit_bytes=vmem_mib * MIB)


def _rms(x, gain):
    return x * lax.rsqrt(jnp.mean(x * x, axis=-1, keepdims=True) + NORM_EPS) * gain


def _rope_tables(pos, inv2, sgn):
    ang = pos * inv2
    return jnp.cos(ang), jnp.sin(ang) * sgn


def _rope(x, c, s):
    return x * c + pltpu.roll(x, HEAD_DIM // 2, 1) * s


def _regroup_kernel(w_ref, o_ref, og_ref, *, segs, gate):
    for dst, src, width in segs:
        o_ref[:, dst:dst + width] = w_ref[:, src:src + width].astype(BF16)
    g0, gw = gate
    lane = lax.broadcasted_iota(jnp.int32, (w_ref.shape[0], LANE), 1)
    og_ref[...] = jnp.where(lane < gw, w_ref[:, g0:g0 + LANE], 0.0).astype(BF16)


def _regroup_w_in(w_in, segs, gate, n_out):
    D, N = w_in.shape
    tr = _tile(D, 64)
    return pl.pallas_call(
        functools.partial(_regroup_kernel, segs=segs, gate=gate), grid=(D // tr,),
        in_specs=[pl.BlockSpec((tr, N), lambda i: (i, 0))],
        out_specs=[pl.BlockSpec((tr, n_out), lambda i: (i, 0)), pl.BlockSpec((tr, LANE), lambda i: (i, 0))],
        out_shape=[jax.ShapeDtypeStruct((D, n_out), BF16), jax.ShapeDtypeStruct((D, LANE), BF16)],
        compiler_params=_params(("parallel",), 32), name="regroup_w_in",
    )(w_in)


def _proj_kernel(x_ref, g_ref, w_ref, wg_ref, o_ref, og_ref, hn_ref):
    @pl.when(pl.program_id(1) == 0)
    def _():
        hn_ref[...] = _rms(x_ref[...], g_ref[...]).astype(BF16)
        og_ref[...] = jnp.dot(hn_ref[...], wg_ref[...], preferred_element_type=F32)

    o_ref[...] = jnp.dot(hn_ref[...], w_ref[...], preferred_element_type=F32).astype(BF16)


def _in_proj(x, gain, w_main, w_gate):
    T, D = x.shape
    N = w_main.shape[1]
    tm = _tile(T, 1024)
    tn = 1536
    assert N % tn == 0
    return pl.pallas_call(
        _proj_kernel,
        grid=(T // tm, N // tn),
        in_specs=[
            pl.BlockSpec((tm, D), lambda i, j: (i, 0)),
            pl.BlockSpec((1, D), lambda i, j: (0, 0)),
            pl.BlockSpec((D, tn), lambda i, j: (0, j)),
            pl.BlockSpec((D, LANE), lambda i, j: (0, 0)),
        ],
        out_specs=[pl.BlockSpec((tm, tn), lambda i, j: (i, j)), pl.BlockSpec((tm, LANE), lambda i, j: (i, 0))],
        out_shape=[jax.ShapeDtypeStruct((T, N), BF16), jax.ShapeDtypeStruct((T, LANE), F32)],
        scratch_shapes=[pltpu.VMEM((tm, D), BF16)],
        compiler_params=_params(("parallel", "arbitrary"), 52),
        name="in_proj",
    )(x, gain, w_main, w_gate)


def _prep_kernel(pos_ref, inv_ref, sgn_ref, gq_ref, gks_ref, gkw_ref, gmq_ref,
                 q_ref, kc_ref, vc_ref, ks_ref, vs_ref, kw_ref, vw_ref,
                 sq_ref, sk_ref, sv_ref, mq_ref, gn_ref,
                 qn_o, kc_o, vc_o, ks_o, kw_o, vsT_o, vwT_o, sq_o, sk_o, svT_o, mq_o, gT_o, qn2_o, kn2_o):
    c, s = _rope_tables(pos_ref[...], inv_ref[...], sgn_ref[...])
    hd = HEAD_DIM
    tp = pos_ref.shape[0]
    f32 = lambda ref, sl: ref[:, sl].astype(F32)
    ones8 = jnp.ones((8, hd), F32)

    def sqnorm_rows(xb):
        x = xb.astype(F32)
        return lax.dot_general(ones8, x * x, _NT, preferred_element_type=F32)

    for h in range(NSA_HEADS):
        sl = slice(h * hd, (h + 1) * hd)
        qb = (_rope(_rms(f32(q_ref, sl), gq_ref[...]), c, s) * QSCALE).astype(BF16)
        qn_o[:, sl] = qb
        qn2_o[h * 8:(h + 1) * 8, :] = sqnorm_rows(qb)
    ones_rows = (lax.broadcasted_iota(jnp.int32, (VAUG - hd, tp), 0) == 0).astype(F32).astype(BF16)
    for g in range(NSA_GROUPS):
        sl = slice(g * hd, (g + 1) * hd)
        kc_o[g] = kc_ref[:, sl]
        vc_o[g] = vc_ref[:, sl]
        ksb = _rope(_rms(f32(ks_ref, sl), gks_ref[...]), c, s).astype(BF16)
        kwb = _rope(_rms(f32(kw_ref, sl), gkw_ref[...]), c, s).astype(BF16)
        ks_o[:, sl] = ksb
        kw_o[:, sl] = kwb
        kn2_o[g * 8:(g + 1) * 8, :] = sqnorm_rows(ksb)
        kn2_o[(NSA_GROUPS + g) * 8:(NSA_GROUPS + g + 1) * 8, :] = sqnorm_rows(kwb)
        vsT_o[g * VAUG:g * VAUG + hd, :] = f32(vs_ref, sl).T.astype(BF16)
        vsT_o[g * VAUG + hd:(g + 1) * VAUG, :] = ones_rows
        vwT_o[g * VAUG:g * VAUG + hd, :] = f32(vw_ref, sl).T.astype(BF16)
        vwT_o[g * VAUG + hd:(g + 1) * VAUG, :] = ones_rows
    for h in range(SB_HEADS):
        sl = slice(h * hd, (h + 1) * hd)
        sq_o[:, sl] = (f32(sq_ref, sl) * QSCALE).astype(BF16)
        svT_o[sl, :] = f32(sv_ref, sl).T.astype(BF16)
    sk_o[...] = sk_ref[...]
    for h in range(MEM_HEADS):
        sl = slice(h * hd, (h + 1) * hd)
        mq_o[:, sl] = (_rms(f32(mq_ref, sl), gmq_ref[...]) * QSCALE).astype(BF16)
    gT_o[...] = jax.nn.sigmoid(gn_ref[...]).T


def _prep(P, Pg, c0, pos_col, inv2, sgn, gq, gks, gkw, gmq):
    T = P.shape[0]
    tp = _tile(T, 512)
    hd = HEAD_DIM
    row = lambda w, c: pl.BlockSpec((tp, w), lambda i, c=c: (i, c))
    const = lambda: pl.BlockSpec((1, hd), lambda i: (0, 0))
    assert c0 % (8 * hd) == 0
    at = lambda w, col: row(w, (c0 + col) // w)
    in_specs = [pl.BlockSpec((tp, 1), lambda i: (i, 0)), const(), const(), const(), const(), const(), const(),
                at(8 * hd, 0),
                at(2 * hd, 8 * hd), at(2 * hd, 10 * hd),
                at(2 * hd, 12 * hd), at(2 * hd, 14 * hd),
                at(2 * hd, 16 * hd), at(2 * hd, 18 * hd),
                at(4 * hd, 20 * hd), at(4 * hd, 24 * hd), at(4 * hd, 28 * hd),
                at(4 * hd, 32 * hd),
                row(hd, 0)]
    colT = lambda w: pl.BlockSpec((w, tp), lambda i: (0, i))
    grp = lambda w: pl.BlockSpec((NSA_GROUPS, tp, w), lambda i: (0, i, 0))
    out_specs = [row(8 * hd, 0), grp(hd), grp(hd), row(2 * hd, 0),
                 row(2 * hd, 0), colT(NSA_GROUPS * VAUG), colT(NSA_GROUPS * VAUG),
                 row(4 * hd, 0), row(4 * hd, 0), colT(4 * hd), row(4 * hd, 0), colT(hd),
                 colT(NSA_HEADS * 8), colT(2 * NSA_GROUPS * 8)]
    sds = jax.ShapeDtypeStruct
    out_shape = [sds((T, 8 * hd), BF16),
                 sds((NSA_GROUPS, T, hd), BF16), sds((NSA_GROUPS, T, hd), BF16), sds((T, 2 * hd), BF16),
                 sds((T, 2 * hd), BF16),
                 sds((NSA_GROUPS * VAUG, T), BF16), sds((NSA_GROUPS * VAUG, T), BF16),
                 sds((T, 4 * hd), BF16), sds((T, 4 * hd), BF16), sds((4 * hd, T), BF16), sds((T, 4 * hd), BF16),
                 sds((hd, T), F32), sds((NSA_HEADS * 8, T), F32), sds((2 * NSA_GROUPS * 8, T), F32)]
    return pl.pallas_call(
        _prep_kernel, grid=(T // tp,), in_specs=in_specs, out_specs=out_specs, out_shape=out_shape,
        compiler_params=_params(("parallel",), 48), name="prep",
    )(pos_col, inv2, sgn, gq, gks, gkw, gmq, *([P] * 11), Pg)


def _gelu_tanh(x):
    return 0.5 * x * (1.0 + jnp.tanh(0.7978845608028654 * (x + 0.044715 * (x * x * x))))


def _compress_one(x, w1, pe, w2):
    n = x.shape[0]
    ab = jnp.dot(x, w1, preferred_element_type=F32)
    pr = jnp.dot(pe, w1, preferred_element_type=F32)
    pec = pr[0:1, :CMP_HIDDEN] + pr[1:2, CMP_HIDDEN:]
    hid = ab[:, :CMP_HIDDEN] + pltpu.roll(ab[:, CMP_HIDDEN:], n - 1, 0) + pec
    return jnp.dot(_gelu_tanh(hid).astype(BF16), w2, preferred_element_type=F32)


def _compress_kernel(xk_ref, xv_ref, w1k_ref, w1v_ref, pek_ref, pev_ref, w2k_ref, w2v_ref,
                     gk_ref, pos_ref, inv_ref, sgn_ref, kc_o, vcT_o):
    c, s = _rope_tables(pos_ref[...], inv_ref[...], sgn_ref[...])
    k = _compress_one(xk_ref[0], w1k_ref[...], pek_ref[...], w2k_ref[...])
    kc_o[0] = _rope(_rms(k, gk_ref[...]), c, s).astype(BF16)
    v = _compress_one(xv_ref[0], w1v_ref[...], pev_ref[...], w2v_ref[...])
    vcT_o[0] = v.T.astype(BF16)


def _compress(xk, xv, w1k, w1v, pek, pev, w2k, w2v, gk, posc, inv2, sgn):
    G, n, W = xk.shape
    full = lambda a: pl.BlockSpec(a.shape, lambda g: (0,) * a.ndim)
    grp = pl.BlockSpec((1, n, W), lambda g: (g, 0, 0))
    return pl.pallas_call(
        _compress_kernel, grid=(G,),
        in_specs=[grp, grp, full(w1k), full(w1v), full(pek), full(pev), full(w2k), full(w2v),
                  full(gk), full(posc), full(inv2), full(sgn)],
        out_specs=[pl.BlockSpec((1, n, HEAD_DIM), lambda g: (g, 0, 0)),
                   pl.BlockSpec((1, HEAD_DIM, n), lambda g: (g, 0, 0))],
        out_shape=[jax.ShapeDtypeStruct((G, n, HEAD_DIM), BF16), jax.ShapeDtypeStruct((G, HEAD_DIM, n), BF16)],
        compiler_params=_params(("parallel",), 48), name="compress",
    )(xk, xv, w1k, w1v, pek, pev, w2k, w2v, gk, posc, inv2, sgn)


def _memkv_kernel(mem_ref, gm_ref, w_ref, gk_ref, k_o, vT_o):
    hn = _rms(mem_ref[...], gm_ref[...]).astype(BF16)
    kv = jnp.dot(hn, w_ref[...], preferred_element_type=F32)
    mw = MEM_HEADS * HEAD_DIM
    for h in range(MEM_HEADS):
        sl = slice(h * HEAD_DIM, (h + 1) * HEAD_DIM)
        k_o[:, sl] = _rms(kv[:, sl], gk_ref[...]).astype(BF16)
        vT_o[sl, :] = kv[:, mw + h * HEAD_DIM: mw + (h + 1) * HEAD_DIM].T.astype(BF16)


def _memkv(mem, gm, w, gk):
    M = mem.shape[0]
    mw = MEM_HEADS * HEAD_DIM
    return pl.pallas_call(
        _memkv_kernel,
        out_shape=[jax.ShapeDtypeStruct((M, mw), BF16), jax.ShapeDtypeStruct((mw, M), BF16)],
        compiler_params=pltpu.CompilerParams(vmem_limit_bytes=32 * MIB), name="mem_kv",
    )(mem, gm, w, gk)


def _cmp_kernel(q_ref, kc_ref, vcT_ref, ovT_ref, g_ref, *rest, tq, i0, n_pad, n_sel, top_k):
    y_ref, b_ref = rest[-2:]
    i = i0 + pl.program_id(1)
    t = i * tq + lax.broadcasted_iota(jnp.int32, (1, tq), 1)
    n_end = lax.broadcasted_iota(jnp.int32, (n_pad, 1), 0) * CMP_STRIDE + (CMP_LEN - 1)
    valid = n_end <= t
    kc = kc_ref[0]
    vcT = vcT_ref[0]
    has_valid = (t >= CMP_LEN - 1).astype(F32)
    sms = []
    for h in range(NSA_HPG):
        q_h = q_ref[:, h * HEAD_DIM:(h + 1) * HEAD_DIM]
        sms.append(jnp.where(valid, lax.dot_general(kc, q_h, _NT, preferred_element_type=F32), NEG_BIG))
    psum = jnp.zeros((n_pad, tq), F32)
    for h in range(NSA_HPG):
        e = jnp.exp2(sms[h] - jnp.max(sms[h], axis=0, keepdims=True))
        p = e * (has_valid / jnp.sum(e, axis=0, keepdims=True))
        oT = jnp.dot(vcT, p.astype(BF16), preferred_element_type=F32)
        y_ref[:, h * HEAD_DIM:(h + 1) * HEAD_DIM] = (oT * g_ref[0, 0, h:h + 1, :]).T.astype(BF16)
        psum = psum + p
    hi = psum.astype(BF16)
    r1 = psum - hi.astype(F32)
    mid = r1.astype(BF16)
    lo = (r1 - mid.astype(F32)).astype(BF16)
    parts = jnp.dot(ovT_ref[...], jnp.concatenate([hi, mid, lo], axis=1), preferred_element_type=F32)
    imp = parts[:, :tq] + parts[:, tq:2 * tq] + parts[:, 2 * tq:]
    s_i = lax.broadcasted_iota(jnp.int32, (n_sel, 1), 0)
    cur = lax.shift_right_logical(t, 6)
    forced = (s_i == 0) | (s_i == cur) | (s_i == cur - 1)
    future = s_i * SEL_BLOCK > t
    w = jnp.where(forced, jnp.inf, jnp.where(future, -jnp.inf, imp))
    s_f = jnp.broadcast_to(s_i.astype(F32), (n_sel, tq))
    for _ in range(top_k):
        m = jnp.max(w, axis=0, keepdims=True)
        idx = jnp.min(jnp.where(w == m, s_f, float(n_sel)), axis=0, keepdims=True)
        w = jnp.where(s_f == idx, -jnp.inf, w)
    b_ref[0, :n_sel, :] = jnp.where(future, NEG_BIG, jnp.where(w == -jnp.inf, 0.0, NEG_BIG))
    n_all = b_ref.shape[1]
    if n_sel < n_all:
        b_ref[0, n_sel:, :] = jnp.full((n_all - n_sel, tq), NEG_BIG, F32)


def _round_up(n, m):
    return -(-n // m) * m


def _cmp_select(qn, kc, vcT, ovT, gates):
    T = qn.shape[0]
    G, n_pad, _ = kc.shape
    n_sel = T // SEL_BLOCK
    tq = _tile(T, 256)
    nq = T // tq
    gw = NSA_HPG * HEAD_DIM
    nb = min(CMP_BUCKETS, nq)
    assert nq % nb == 0
    per = nq // nb
    out_shape = [jax.ShapeDtypeStruct((T, NSA_HEADS * HEAD_DIM), BF16), jax.ShapeDtypeStruct((G, n_sel, T), F32)]
    buffers = ()
    for b in range(nb):
        i0 = b * per
        t_max = (i0 + per) * tq - 1
        n_len = min(n_pad, _round_up(max(t_max - (CMP_LEN - 1), 0) // CMP_STRIDE + 1, LANE))
        s_len = min(n_sel, _round_up(t_max // SEL_BLOCK + 1, 8))
        kern = functools.partial(_cmp_kernel, tq=tq, i0=i0, n_pad=n_len, n_sel=s_len, top_k=min(SEL_TOPK, n_sel))
        buffers = pl.pallas_call(
            kern, grid=(G, per),
            in_specs=[pl.BlockSpec((tq, gw), lambda g, i, i0=i0: (i0 + i, g)),
                      pl.BlockSpec((1, n_len, HEAD_DIM), lambda g, i: (g, 0, 0)),
                      pl.BlockSpec((1, HEAD_DIM, n_len), lambda g, i: (g, 0, 0)),
                      pl.BlockSpec((s_len, n_len), lambda g, i: (0, 0)),
                      pl.BlockSpec((1, 1, 8, tq), lambda g, i, i0=i0: (0, g, 0, i0 + i))]
            + [pl.BlockSpec(memory_space=pl.ANY)] * len(buffers),
            out_specs=[pl.BlockSpec((tq, gw), lambda g, i, i0=i0: (i0 + i, g)),
                       pl.BlockSpec((1, n_sel, tq), lambda g, i, i0=i0: (g, 0, i0 + i))],
            out_shape=out_shape,
            input_output_aliases={5: 0, 6: 1} if buffers else {},
            compiler_params=_params(("parallel", "parallel"), 48), name="cmp_select_%d" % b,
        )(qn, kc, vcT, ovT, gates, *buffers)
    return buffers


def _flash_kernel(qi_ref, kj_ref, fl_ref, q_ref, k_ref, vT_ref, g_ref, sh_ref, *rest, mode, bounded, tq, tk):
    if mode == "sel":
        bias_ref, o_ref, m_sc, acc_sc = rest
    else:
        o_ref, m_sc, acc_sc = rest
    p = pl.program_id(1)
    i = qi_ref[p]
    j = kj_ref[p]
    fl = fl_ref[p]

    @pl.when((fl & 1) != 0)
    def _():
        m_sc[...] = jnp.full(m_sc.shape, NEG_BIG, F32)
        acc_sc[...] = jnp.zeros(acc_sc.shape, F32)

    shift = sh_ref[0, 0:1, :] if bounded else 0.0
    nb = tk // SEL_BLOCK
    if mode == "sel":
        rows = bias_ref[0] - shift

    def key_mask():
        t = i * tq + lax.broadcasted_iota(jnp.int32, (1, tq), 1)
        kpos = j * tk + lax.broadcasted_iota(jnp.int32, (tk, 1), 0)
        if mode == "sel":
            base = jnp.concatenate(
                [jnp.broadcast_to(rows[b:b + 1, :], (SEL_BLOCK, tq)) for b in range(nb)], axis=0)
            return jnp.where(kpos <= t, base, NEG_BIG)
        return jnp.where((kpos <= t) & (kpos > t - WINDOW), 0.0 - shift, NEG_BIG)

    k = k_ref[...]
    vT = vT_ref[...]
    raw = lambda h: lax.dot_general(k, q_ref[:, h * HEAD_DIM:(h + 1) * HEAD_DIM], _NT, preferred_element_type=F32)

    def accumulate(pTs):
        for h in range(NSA_HPG):
            acc_sc[h] = acc_sc[h] + jnp.dot(vT, pTs[h], preferred_element_type=F32)

    if bounded and mode == "sel":
        @pl.when((fl & 4) != 0)
        def _():
            mask_add = key_mask()
            accumulate([jnp.exp2(raw(h) + mask_add).astype(BF16) for h in range(NSA_HPG)])

        @pl.when((fl & 4) == 0)
        def _():
            def probs(h):
                s = raw(h).reshape(nb, SEL_BLOCK, tq) + rows[:, None, :]
                return jnp.exp2(s).astype(BF16).reshape(tk, tq)
            accumulate([probs(h) for h in range(NSA_HPG)])
    elif bounded:
        mask_add = key_mask()
        accumulate([jnp.exp2(raw(h) + mask_add).astype(BF16) for h in range(NSA_HPG)])
    else:
        mask_add = key_mask()
        sTs = [raw(h) + mask_add for h in range(NSA_HPG)]
        pTs, alphas = [], []
        for h in range(NSA_HPG):
            sT = sTs[h]
            m_old = m_sc[h]
            m_new = jnp.maximum(m_old, jnp.max(sT, axis=0, keepdims=True))
            alphas.append(jnp.exp2(m_old - m_new))
            pTs.append(jnp.exp2((sT - m_new).astype(BF16)))
            m_sc[h] = m_new
        for h in range(NSA_HPG):
            acc_sc[h] = alphas[h] * acc_sc[h] + jnp.dot(vT, pTs[h], preferred_element_type=F32)

    @pl.when((fl & 2) != 0)
    def _():
        for h in range(NSA_HPG):
            acc = acc_sc[h]
            l = acc[HEAD_DIM:HEAD_DIM + 1, :]
            o = acc[:HEAD_DIM, :] * ((1.0 / l) * g_ref[0, 0, h:h + 1, :])
            o_ref[:, h * HEAD_DIM:(h + 1) * HEAD_DIM] = o.T.astype(BF16)


def _steps(nq, lo_fn, hi_fn, reverse=False):
    qi, kj, fl = [], [], []
    for i in range(nq):
        js = list(range(lo_fn(i), hi_fn(i) + 1))
        if reverse:
            js = js[::-1]
        for n, j in enumerate(js):
            qi.append(i)
            kj.append(j)
            fl.append((1 if n == 0 else 0) | (2 if n == len(js) - 1 else 0))
    return (jnp.asarray(np.array(qi, np.int32)), jnp.asarray(np.array(kj, np.int32)),
            jnp.asarray(np.array(fl, np.int32)))


def _nsa_flash(mode, qn, k, vT, gates, shift, bias=None):
    use_bounded = jnp.max(shift) <= SOFTMAX_BOUND_MAX
    return lax.cond(use_bounded,
                    lambda: _nsa_flash_call(mode, True, qn, k, vT, gates, shift, bias),
                    lambda: _nsa_flash_call(mode, False, qn, k, vT, gates, shift, bias))


def _nsa_flash_call(mode, bounded, qn, k, vT, gates, shift, bias):
    T = qn.shape[0]
    tq = _tile(T, 1024)
    tk = _tile(T, 1024 if mode == "sel" else 512)
    nq = T // tq
    gw = NSA_HPG * HEAD_DIM
    hi = lambda i: ((i + 1) * tq - 1) // tk
    if mode == "sel":
        lo = lambda i: 0
        br = 1
    else:
        lo = lambda i: max(0, (i * tq - (WINDOW - 1)) // tk)
        br = 2
    qi, kj, fl = _steps(nq, lo, hi)
    fl = fl | 4 * ((kj + 1) * tk - 1 > qi * tq).astype(jnp.int32)
    in_specs = [pl.BlockSpec((tq, gw), lambda g, p, qi, kj, fl: (qi[p], g)),
                pl.BlockSpec((tk, HEAD_DIM), lambda g, p, qi, kj, fl: (kj[p], g)),
                pl.BlockSpec((VAUG, tk), lambda g, p, qi, kj, fl: (g, kj[p])),
                pl.BlockSpec((1, 1, 8, tq), lambda g, p, qi, kj, fl, br=br: (br, g, 0, qi[p])),
                pl.BlockSpec((1, 8, tq), lambda g, p, qi, kj, fl: (g, 0, qi[p]))]
    args = [qn, k, vT, gates, shift]
    if mode == "sel":
        in_specs.append(pl.BlockSpec((1, tk // SEL_BLOCK, tq), lambda g, p, qi, kj, fl: (g, kj[p], qi[p])))
        args.append(bias)
    kern = functools.partial(_flash_kernel, mode=mode, bounded=bounded, tq=tq, tk=tk)
    return pl.pallas_call(
        kern,
        grid_spec=pltpu.PrefetchScalarGridSpec(
            num_scalar_prefetch=3, grid=(NSA_GROUPS, int(qi.shape[0])),
            in_specs=in_specs,
            out_specs=pl.BlockSpec((tq, gw), lambda g, p, qi, kj, fl: (qi[p], g)),
            scratch_shapes=[pltpu.VMEM((NSA_HPG, 1, tq), F32), pltpu.VMEM((NSA_HPG, VAUG, tq), F32)]),
        out_shape=jax.ShapeDtypeStruct((T, NSA_HEADS * HEAD_DIM), BF16),
        compiler_params=_params(("parallel", "arbitrary"), 48),
        name="nsa_" + mode + ("_bounded" if bounded else "_online"),
    )(qi, kj, fl, *args)


def _sb_kernel(qi_ref, kj_ref, fl_ref, q_ref, k_ref, vT_ref, lm_ref, *rest, tq, tk, resume):
    if resume:
        acc_in, carry_in, o_ref, carry_sc, acc_sc = rest
    else:
        o_ref, acc_o, carry_o, carry_sc, acc_sc = rest
    p = pl.program_id(0)
    i = qi_ref[p]
    j = kj_ref[p]
    fl = fl_ref[p]

    @pl.when((fl & 1) != 0)
    def _():
        if resume:
            for h in range(SB_HEADS):
                carry_sc[h] = carry_in[h:h + 1, :]
                acc_sc[h] = acc_in[h * HEAD_DIM:(h + 1) * HEAD_DIM, :]
        else:
            carry_sc[...] = jnp.zeros(carry_sc.shape, F32)
            acc_sc[...] = jnp.zeros(acc_sc.shape, F32)

    @pl.when((jnp.max(carry_sc[...]) > SB_DEAD_LOG2) & ((fl & 4) == 0))
    def _():
        t = i * tq + lax.broadcasted_iota(jnp.int32, (1, tq), 1)
        kpos = j * tk + lax.broadcasted_iota(jnp.int32, (tk, 1), 0)
        past = kpos < t
        lm = lm_ref[...]
        hs = [slice(h * HEAD_DIM, (h + 1) * HEAD_DIM) for h in range(SB_HEADS)]
        zs = [lax.dot_general(k_ref[:, sl], q_ref[:, sl], _NT, preferred_element_type=F32) for sl in hs]
        lgs, css = [], []
        for h in range(SB_HEADS):
            z = zs[h]
            sp = jnp.maximum(z, 0.0) + jnp.log2(1.0 + jnp.exp2(-jnp.abs(z)))
            lk = jnp.where(past, -sp, 0.0)
            hi = lk.astype(BF16)
            lo = (lk - hi.astype(F32)).astype(BF16)
            css.append(jnp.dot(lm, jnp.concatenate([hi, lo], axis=1), preferred_element_type=F32))
            lgs.append(z - sp)
            carry_old = carry_sc[h]
            carry_sc[h] = carry_old + jnp.sum(lk, axis=0, keepdims=True)
            css[h] = css[h][:, :tq] + css[h][:, tq:] + carry_old
        for h in range(SB_HEADS):
            wgt = jnp.where(past, jnp.exp2(lgs[h] + css[h]), 0.0)
            acc_sc[h] = acc_sc[h] + jnp.dot(vT_ref[hs[h], :], wgt.astype(BF16), preferred_element_type=F32)

    @pl.when((fl & 2) != 0)
    def _():
        for h in range(SB_HEADS):
            o_ref[:, h * HEAD_DIM:(h + 1) * HEAD_DIM] = acc_sc[h].T.astype(BF16)
            if not resume:
                acc_o[h * HEAD_DIM:(h + 1) * HEAD_DIM, :] = acc_sc[h]
                carry_o[h:h + 1, :] = carry_sc[h]
        if not resume:
            carry_o[SB_HEADS:, :] = jnp.zeros((8 - SB_HEADS, tq), F32)


def _sb_call(q, k, vT, steps, tq, tk, state=None):
    T, W = q.shape
    qi, kj, fl = steps
    lmat = jnp.asarray(np.triu(np.ones((tk, tk), np.float32), 1), BF16)
    resume = state is not None
    qtile = lambda shape: pl.BlockSpec(shape, lambda p, qi, kj, fl: (qi[p], 0))
    qtileT = lambda rows: pl.BlockSpec((rows, tq), lambda p, qi, kj, fl: (0, qi[p]))
    in_specs = [qtile((tq, W)),
                pl.BlockSpec((tk, W), lambda p, qi, kj, fl: (kj[p], 0)),
                pl.BlockSpec((W, tk), lambda p, qi, kj, fl: (0, kj[p])),
                pl.BlockSpec((tk, tk), lambda p, qi, kj, fl: (0, 0))]
    y_sds = jax.ShapeDtypeStruct((T, W), BF16)
    if resume:
        in_specs += [qtileT(W), qtileT(8)]
        out_specs, out_shape = qtile((tq, W)), y_sds
    else:
        out_specs = [qtile((tq, W)), qtileT(W), qtileT(8)]
        out_shape = [y_sds, jax.ShapeDtypeStruct((W, T), F32), jax.ShapeDtypeStruct((8, T), F32)]
    return pl.pallas_call(
        functools.partial(_sb_kernel, tq=tq, tk=tk, resume=resume),
        grid_spec=pltpu.PrefetchScalarGridSpec(
            num_scalar_prefetch=3, grid=(int(qi.shape[0]),),
            in_specs=in_specs, out_specs=out_specs,
            scratch_shapes=[pltpu.VMEM((SB_HEADS, 1, tq), F32), pltpu.VMEM((SB_HEADS, HEAD_DIM, tq), F32)]),
        out_shape=out_shape,
        compiler_params=_params(("arbitrary",), 48), name="sb_far" if resume else "sb_near",
    )(qi, kj, fl, q, k, vT, lmat, *(state or ()))


def _stick_breaking(q, k, vT):
    T, W = q.shape
    tq = _tile(T, 512)
    tk = _tile(T, 256)
    nq = T // tq
    hi = lambda i: ((i + 1) * tq - 2) // tk
    lo_near = lambda i: max(0, hi(i) - SB_NEAR_TILES + 1)
    near = _steps(nq, lo_near, hi, reverse=True)
    y_near, acc, carry = _sb_call(q, k, vT, near, tq, tk)
    qi, kj, fl = [], [], []
    for i in range(nq):
        js = list(range(lo_near(i) - 1, -1, -1))
        for n, j in enumerate(js or [0]):
            qi.append(i)
            kj.append(j)
            fl.append((1 if n == 0 else 0) | (2 if n == max(len(js), 1) - 1 else 0) | (0 if js else 4))
    far = tuple(jnp.asarray(np.array(a, np.int32)) for a in (qi, kj, fl))
    with_far = [i for i in range(nq) if lo_near(i) > 0]
    if not with_far:
        return y_near
    alive = jnp.max(carry[:SB_HEADS, with_far[0] * tq:]) > SB_DEAD_LOG2
    return lax.cond(alive, lambda: _sb_call(q, k, vT, far, tq, tk, state=(acc, carry)), lambda: y_near)


def _memattn_kernel(q_ref, k_ref, vT_ref, o_ref):
    for h in range(MEM_HEADS):
        sl = slice(h * HEAD_DIM, (h + 1) * HEAD_DIM)
        sT = lax.dot_general(k_ref[:, sl], q_ref[:, sl], _NT, preferred_element_type=F32)
        e = jnp.exp2(sT - jnp.max(sT, axis=0, keepdims=True))
        l = jnp.sum(e, axis=0, keepdims=True)
        oT = jnp.dot(vT_ref[sl, :], e.astype(BF16), preferred_element_type=F32) * (1.0 / l)
        o_ref[:, sl] = oT.T.astype(BF16)


def _mem_attention(q, k, vT):
    T, W = q.shape
    M = k.shape[0]
    tq = _tile(T, 512)
    return pl.pallas_call(
        _memattn_kernel, grid=(T // tq,),
        in_specs=[pl.BlockSpec((tq, W), lambda i: (i, 0)),
                  pl.BlockSpec((M, W), lambda i: (0, 0)),
                  pl.BlockSpec((W, M), lambda i: (0, 0))],
        out_specs=pl.BlockSpec((tq, W), lambda i: (i, 0)),
        out_shape=jax.ShapeDtypeStruct((T, W), BF16),
        compiler_params=_params(("parallel",), 32), name="mem_attention",
    )(q, k, vT)


def _mixout_kernel(yc_ref, ys_ref, yw_ref, ysb_ref, ym_ref, g_ref, wn_ref, wsb_ref, wm_ref, wo_ref, x_ref, gain_ref,
                   x2_o, h2_o):
    D = x_ref.shape[1]
    yn = (yc_ref[...].astype(F32) + ys_ref[...].astype(F32) + yw_ref[...].astype(F32)).astype(BF16)
    a = jnp.dot(yn, wn_ref[...], preferred_element_type=F32)
    b = jnp.dot(ysb_ref[...], wsb_ref[...], preferred_element_type=F32)
    c = jnp.dot(ym_ref[...], wm_ref[...], preferred_element_type=F32)
    sig = lambda n: jax.nn.sigmoid(g_ref[:, n * D:(n + 1) * D].astype(F32))
    mixed = (sig(0) * a + sig(1) * b + sig(2) * c).astype(BF16)
    x2 = x_ref[...] + jnp.dot(mixed, wo_ref[...], preferred_element_type=F32)
    x2_o[...] = x2
    h2_o[...] = _rms(x2, gain_ref[...]).astype(BF16)


def _mix_out(yc, ys, yw, ysb, ym, P, wn, wsb, wm, wo, x, gain):
    T, D = x.shape
    tm = _tile(T, 256)
    rowi = lambda w: pl.BlockSpec((tm, w), lambda i: (i, 0))
    res = lambda a: pl.BlockSpec(a.shape, lambda i: (0, 0), pipeline_mode=pl.Buffered(1))
    return pl.pallas_call(
        _mixout_kernel, grid=(T // tm,),
        in_specs=[rowi(yc.shape[1]), rowi(ys.shape[1]), rowi(yw.shape[1]), rowi(ysb.shape[1]), rowi(ym.shape[1]),
                  rowi(N_BRANCH * D), res(wn), res(wsb), res(wm), res(wo), rowi(D), res(gain)],
        out_specs=[rowi(D), rowi(D)],
        out_shape=[jax.ShapeDtypeStruct((T, D), F32), jax.ShapeDtypeStruct((T, D), BF16)],
        compiler_params=_params(("parallel",), 48), name="mix_out",
    )(yc, ys, yw, ysb, ym, P, wn, wsb, wm, wo, x, gain)


def _ffn_kernel(h_ref, wg_ref, wu_ref, wd_ref, x_ref, o_ref, z_sc, *, nf):
    f = pl.program_id(1)

    def up():
        a = jnp.dot(h_ref[...], wg_ref[...], preferred_element_type=F32)
        b = jnp.dot(h_ref[...], wu_ref[...], preferred_element_type=F32)
        return (a * jax.nn.sigmoid(a) * b).astype(BF16)

    def down():
        return jnp.dot(z_sc[...], wd_ref[...], preferred_element_type=F32)

    @pl.when(f == 0)
    def _():
        z_sc[...] = up()

    @pl.when((f > 0) & (f < nf))
    def _():
        c = down()
        z_new = up()
        o_ref[...] = jnp.where(f == 1, x_ref[...], o_ref[...]) + c
        z_sc[...] = z_new

    @pl.when(f == nf)
    def _():
        o_ref[...] += down()


def _ffn(h2, wg, wu, wd, x2):
    T, D = x2.shape
    F = wg.shape[1]
    tm = _tile(T, 1024)
    tf = 512
    assert F % tf == 0
    nf = F // tf
    assert nf >= 2
    return pl.pallas_call(
        functools.partial(_ffn_kernel, nf=nf), grid=(T // tm, nf + 1),
        in_specs=[pl.BlockSpec((tm, D), lambda i, f: (i, 0)),
                  pl.BlockSpec((D, tf), lambda i, f: (0, jnp.minimum(f, nf - 1))),
                  pl.BlockSpec((D, tf), lambda i, f: (0, jnp.minimum(f, nf - 1))),
                  pl.BlockSpec((tf, D), lambda i, f: (jnp.maximum(f - 1, 0), 0)),
                  pl.BlockSpec((tm, D), lambda i, f: (i, 0), pipeline_mode=pl.Buffered(1))],
        out_specs=pl.BlockSpec((tm, D), lambda i, f: (i, 0)),
        out_shape=jax.ShapeDtypeStruct((T, D), F32),
        scratch_shapes=[pltpu.VMEM((tm, tf), BF16)],
        compiler_params=_params(("parallel", "arbitrary"), 56), name="ffn",
    )(h2, wg.astype(BF16), wu.astype(BF16), wd.astype(BF16), x2)


def _layer(x, mem, pos_col, posc_col, consts, attn_norm, w_in, nsa_q_norm, nsa_kc_norm, nsa_ks_norm, nsa_kw_norm,
           cmp_k_pe, cmp_k_w1, cmp_k_w2, cmp_v_pe, cmp_v_w1, cmp_v_w2, mem_norm, w_mem_kv,
           mem_q_norm, mem_k_norm, w_o_nsa, w_o_sb, w_o_mem, w_out, ffn_norm,
           w_ffn_gate, w_ffn_up, w_ffn_down):
    T, D = x.shape
    inv2, sgn, ovT = consts
    hd = HEAD_DIM
    row = lambda g: g.reshape(1, -1)

    q_w, kv_w, gn_w = NSA_HEADS * hd, 6 * NSA_GROUPS * hd, 3 * NSA_HEADS
    sb_w, mq_w, gm_w = 3 * SB_HEADS * hd, MEM_HEADS * hd, N_BRANCH * D
    o_gn = q_w + kv_w
    o_sb = o_gn + gn_w
    o_mq = o_sb + sb_w
    o_gm = o_mq + mq_w
    assert w_in.shape[1] == o_gm + gm_w
    segs = ((0, o_gm, gm_w), (gm_w, 0, o_gn), (gm_w + o_gn, o_sb, o_gm - o_sb))
    w_main, w_gate = _regroup_w_in(w_in, segs, (o_gn, gn_w), gm_w + o_gn + o_gm - o_sb)
    P, Pg = _in_proj(x, row(attn_norm), w_main, w_gate)

    (qn, kc_raw, vc_raw, ksn, kwn, vsT, vwT, sbq, sbk, sbvT, memq, gT, qn2, kn2) = _prep(
        P, Pg, gm_w, pos_col, inv2, sgn, row(nsa_q_norm), row(nsa_ks_norm), row(nsa_kw_norm), row(mem_q_norm))

    q2 = jnp.max(qn2[::8].reshape(NSA_GROUPS, NSA_HPG, T), axis=1)
    k2 = jnp.max(kn2[::8], axis=1).reshape(2, NSA_GROUPS)
    bound = lambda kk: jnp.broadcast_to((1.02 * jnp.sqrt(q2 * kk[:, None]))[:, None, :], (NSA_GROUPS, 8, T))
    shift_sel, shift_win = bound(k2[0]), bound(k2[1])

    gates = gT[:gn_w].reshape(NSA_GROUPS, NSA_HPG, 3, T).transpose(2, 0, 1, 3)
    gates = jnp.pad(gates, ((0, 0), (0, 0), (0, 8 - NSA_HPG), (0, 0)))

    n_pad = T // CMP_STRIDE
    half = CMP_LEN // 2

    def w1_pack(w1):
        return jnp.concatenate([w1[:half].reshape(half * hd, -1), w1[half:].reshape(half * hd, -1)], axis=1).astype(BF16)

    def pe_pack(pe):
        return jnp.pad(pe.reshape(2, half * hd), ((0, 6), (0, 0))).astype(BF16)

    kc, vcT = _compress(
        kc_raw.reshape(NSA_GROUPS, n_pad, CMP_STRIDE * hd), vc_raw.reshape(NSA_GROUPS, n_pad, CMP_STRIDE * hd),
        w1_pack(cmp_k_w1), w1_pack(cmp_v_w1), pe_pack(cmp_k_pe), pe_pack(cmp_v_pe),
        cmp_k_w2.astype(BF16), cmp_v_w2.astype(BF16), row(nsa_kc_norm), posc_col, inv2, sgn)

    y_cmp, bias = _cmp_select(qn, kc, vcT, ovT, gates)
    y_sel = _nsa_flash("sel", qn, ksn, vsT, gates, shift_sel, bias)
    y_win = _nsa_flash("win", qn, kwn, vwT, gates, shift_win)
    y_sb = _stick_breaking(sbq, sbk, sbvT)

    mk, mvT = _memkv(mem, row(mem_norm), w_mem_kv.astype(BF16), row(mem_k_norm))
    y_mem = _mem_attention(memq, mk, mvT)

    x2, h2 = _mix_out(y_cmp, y_sel, y_win, y_sb, y_mem, P, w_o_nsa.astype(BF16), w_o_sb.astype(BF16),
                      w_o_mem.astype(BF16), w_out.astype(BF16), x, row(ffn_norm))
    return _ffn(h2, w_ffn_gate, w_ffn_up, w_ffn_down, x2)


def kernel(x, mem, positions, attn_norm, w_in, nsa_q_norm, nsa_kc_norm, nsa_ks_norm, nsa_kw_norm, cmp_k_pe, cmp_k_w1, cmp_k_w2, cmp_v_pe, cmp_v_w1, cmp_v_w2, mem_norm, w_mem_kv, mem_q_norm, mem_k_norm, w_o_nsa, w_o_sb, w_o_mem, w_out, ffn_norm, w_ffn_gate, w_ffn_up, w_ffn_down):
    B, T, D = x.shape
    assert T % (4 * LANE) == 0 and T // SEL_BLOCK >= 8
    n_pad = T // CMP_STRIDE
    n_sel = T // SEL_BLOCK
    inv = 1.0 / (ROPE_THETA ** (jnp.arange(0, HEAD_DIM, 2, dtype=F32) / HEAD_DIM))
    inv2 = jnp.concatenate([inv, inv]).reshape(1, HEAD_DIM)
    sgn = jnp.concatenate([-jnp.ones((HEAD_DIM // 2,), F32), jnp.ones((HEAD_DIM // 2,), F32)]).reshape(1, HEAD_DIM)
    cs = np.arange(n_pad)[None, :] * CMP_STRIDE
    ss = np.arange(n_sel)[:, None] * SEL_BLOCK
    ovT = jnp.asarray(((cs < ss + SEL_BLOCK) & (cs + CMP_LEN - 1 >= ss)).astype(np.float32), BF16)
    consts = (inv2, sgn, ovT)
    depth = w_in.shape[0]
    outs = []
    for b in range(B):
        xb = x[b]
        posf = positions[b].astype(F32)
        pos_col = posf.reshape(T, 1)
        posc = jnp.concatenate([posf[CMP_LEN - 1::CMP_STRIDE], posf[-1:]]).reshape(n_pad, 1)
        for l in range(depth):
            xb = _layer(xb, mem[b], pos_col, posc, consts, attn_norm[l], w_in[l], nsa_q_norm[l], nsa_kc_norm[l],
                        nsa_ks_norm[l], nsa_kw_norm[l], cmp_k_pe[l], cmp_k_w1[l], cmp_k_w2[l], cmp_v_pe[l],
                        cmp_v_w1[l], cmp_v_w2[l], mem_norm[l], w_mem_kv[l], mem_q_norm[l], mem_k_norm[l],
                        w_o_nsa[l], w_o_sb[l], w_o_mem[l], w_out[l], ffn_norm[l],
                        w_ffn_gate[l], w_ffn_up[l], w_ffn_down[l])
        outs.append(xb)
    return outs[0][None] if B == 1 else jnp.stack(outs, axis=0)
```

```python
import functools

import numpy as np
import jax
import jax.numpy as jnp
from jax import lax
from jax.experimental import pallas as pl
from jax.experimental.pallas import tpu as pltpu

HEAD_DIM = 128
NSA_HEADS = 8
NSA_GROUPS = 2
NSA_HPG = NSA_HEADS // NSA_GROUPS
SB_HEADS = 4
MEM_HEADS = 4
CMP_LEN = 32
CMP_STRIDE = 16
CMP_HIDDEN = 2 * HEAD_DIM
SEL_BLOCK = 64
SEL_TOPK = 16
WINDOW = 512
ROPE_THETA = 10000.0
NORM_EPS = 1e-6
NEG_BIG = -1e30
N_BRANCH = 3
SCALE = HEAD_DIM ** -0.5
LOG2E = 1.4426950408889634
QSCALE = SCALE * LOG2E
SOFTMAX_BOUND_MAX = 50.0
CMP_BUCKETS = 8
SB_NEAR_TILES = 3
SB_DEAD_LOG2 = -160.0
VAUG = HEAD_DIM + 16

LANE = 128
MIB = 1 << 20
BF16 = jnp.bfloat16
F32 = jnp.float32

_NT = (((1,), (1,)), ((), ()))


def _tile(n, pref):
    t = min(n, pref)
    assert n % t == 0, (n, pref)
    return t


def _params(sem, vmem_mib):
    return pltpu.CompilerParams(dimension_semantics=sem, vmem_limit_bytes=vmem_mib * MIB)


def _rms(x, gain):
    return x * lax.rsqrt(jnp.mean(x * x, axis=-1, keepdims=True) + NORM_EPS) * gain


def _rope_tables(pos, inv2, sgn):
    ang = pos * inv2
    return jnp.cos(ang), jnp.sin(ang) * sgn


def _rope(x, c, s):
    return x * c + pltpu.roll(x, HEAD_DIM // 2, 1) * s


def _regroup_kernel(w_ref, o_ref, og_ref, *, segs, gate):
    for dst, src, width in segs:
        o_ref[:, dst:dst + width] = w_ref[:, src:src + width].astype(BF16)
    g0, gw = gate
    lane = lax.broadcasted_iota(jnp.int32, (w_ref.shape[0], LANE), 1)
    og_ref[...] = jnp.where(lane < gw, w_ref[:, g0:g0 + LANE], 0.0).astype(BF16)


def _regroup_w_in(w_in, segs, gate, n_out):
    D, N = w_in.shape
    tr = _tile(D, 64)
    return pl.pallas_call(
        functools.partial(_regroup_kernel, segs=segs, gate=gate), grid=(D // tr,),
        in_specs=[pl.BlockSpec((tr, N), lambda i: (i, 0))],
        out_specs=[pl.BlockSpec((tr, n_out), lambda i: (i, 0)), pl.BlockSpec((tr, LANE), lambda i: (i, 0))],
        out_shape=[jax.ShapeDtypeStruct((D, n_out), BF16), jax.ShapeDtypeStruct((D, LANE), BF16)],
        compiler_params=_params(("parallel",), 32), name="regroup_w_in",
    )(w_in)


def _proj_kernel(x_ref, g_ref, w_ref, wg_ref, o_ref, og_ref, hn_ref):
    @pl.when(pl.program_id(1) == 0)
    def _():
        hn_ref[...] = _rms(x_ref[...], g_ref[...]).astype(BF16)
        og_ref[...] = jnp.dot(hn_ref[...], wg_ref[...], preferred_element_type=F32)

    o_ref[...] = jnp.dot(hn_ref[...], w_ref[...], preferred_element_type=F32).astype(BF16)


def _in_proj(x, gain, w_main, w_gate):
    T, D = x.shape
    N = w_main.shape[1]
    tm = _tile(T, 1024)
    tn = 1536
    assert N % tn == 0
    return pl.pallas_call(
        _proj_kernel,
        grid=(T // tm, N // tn),
        in_specs=[
            pl.BlockSpec((tm, D), lambda i, j: (i, 0)),
            pl.BlockSpec((1, D), lambda i, j: (0, 0)),
            pl.BlockSpec((D, tn), lambda i, j: (0, j)),
            pl.BlockSpec((D, LANE), lambda i, j: (0, 0)),
        ],
        out_specs=[pl.BlockSpec((tm, tn), lambda i, j: (i, j)), pl.BlockSpec((tm, LANE), lambda i, j: (i, 0))],
        out_shape=[jax.ShapeDtypeStruct((T, N), BF16), jax.ShapeDtypeStruct((T, LANE), F32)],
        scratch_shapes=[pltpu.VMEM((tm, D), BF16)],
        compiler_params=_params(("parallel", "arbitrary"), 52),
        name="in_proj",
    )(x, gain, w_main, w_gate)


def _prep_kernel(pos_ref, inv_ref, sgn_ref, gq_ref, gks_ref, gkw_ref, gmq_ref,
                 q_ref, kc_ref, vc_ref, ks_ref, vs_ref, kw_ref, vw_ref,
                 sq_ref, sk_ref, sv_ref, mq_ref, gn_ref,
                 qn_o, kc_o, vc_o, ks_o, kw_o, vsT_o, vwT_o, sq_o, sk_o, svT_o, mq_o, gT_o, qn2_o, kn2_o):
    c, s = _rope_tables(pos_ref[...], inv_ref[...], sgn_ref[...])
    hd = HEAD_DIM
    tp = pos_ref.shape[0]
    f32 = lambda ref, sl: ref[:, sl].astype(F32)
    ones8 = jnp.ones((8, hd), F32)

    def sqnorm_rows(xb):
        x = xb.astype(F32)
        return lax.dot_general(ones8, x * x, _NT, preferred_element_type=F32)

    for h in range(NSA_HEADS):
        sl = slice(h * hd, (h + 1) * hd)
        qb = (_rope(_rms(f32(q_ref, sl), gq_ref[...]), c, s) * QSCALE).astype(BF16)
        qn_o[:, sl] = qb
        qn2_o[h * 8:(h + 1) * 8, :] = sqnorm_rows(qb)
    ones_rows = (lax.broadcasted_iota(jnp.int32, (VAUG - hd, tp), 0) == 0).astype(F32).astype(BF16)
    for g in range(NSA_GROUPS):
        sl = slice(g * hd, (g + 1) * hd)
        kc_o[g] = kc_ref[:, sl]
        vc_o[g] = vc_ref[:, sl]
        ksb = _rope(_rms(f32(ks_ref, sl), gks_ref[...]), c, s).astype(BF16)
        kwb = _rope(_rms(f32(kw_ref, sl), gkw_ref[...]), c, s).astype(BF16)
        ks_o[:, sl] = ksb
        kw_o[:, sl] = kwb
        kn2_o[g * 8:(g + 1) * 8, :] = sqnorm_rows(ksb)
        kn2_o[(NSA_GROUPS + g) * 8:(NSA_GROUPS + g + 1) * 8, :] = sqnorm_rows(kwb)
        vsT_o[g * VAUG:g * VAUG + hd, :] = f32(vs_ref, sl).T.astype(BF16)
        vsT_o[g * VAUG + hd:(g + 1) * VAUG, :] = ones_rows
        vwT_o[g * VAUG:g * VAUG + hd, :] = f32(vw_ref, sl).T.astype(BF16)
        vwT_o[g * VAUG + hd:(g + 1) * VAUG, :] = ones_rows
    for h in range(SB_HEADS):
        sl = slice(h * hd, (h + 1) * hd)
        sq_o[:, sl] = (f32(sq_ref, sl) * QSCALE).astype(BF16)
        svT_o[sl, :] = f32(sv_ref, sl).T.astype(BF16)
    sk_o[...] = sk_ref[...]
    for h in range(MEM_HEADS):
        sl = slice(h * hd, (h + 1) * hd)
        mq_o[:, sl] = (_rms(f32(mq_ref, sl), gmq_ref[...]) * QSCALE).astype(BF16)
    gT_o[...] = jax.nn.sigmoid(gn_ref[...]).T


def _prep(P, Pg, c0, pos_col, inv2, sgn, gq, gks, gkw, gmq):
    T = P.shape[0]
    tp = _tile(T, 512)
    hd = HEAD_DIM
    row = lambda w, c: pl.BlockSpec((tp, w), lambda i, c=c: (i, c))
    const = lambda: pl.BlockSpec((1, hd), lambda i: (0, 0))
    assert c0 % (8 * hd) == 0
    at = lambda w, col: row(w, (c0 + col) // w)
    in_specs = [pl.BlockSpec((tp, 1), lambda i: (i, 0)), const(), const(), const(), const(), const(), const(),
                at(8 * hd, 0),
                at(2 * hd, 8 * hd), at(2 * hd, 10 * hd),
                at(2 * hd, 12 * hd), at(2 * hd, 14 * hd),
                at(2 * hd, 16 * hd), at(2 * hd, 18 * hd),
                at(4 * hd, 20 * hd), at(4 * hd, 24 * hd), at(4 * hd, 28 * hd),
                at(4 * hd, 32 * hd),
                row(hd, 0)]
    colT = lambda w: pl.BlockSpec((w, tp), lambda i: (0, i))
    grp = lambda w: pl.BlockSpec((NSA_GROUPS, tp, w), lambda i: (0, i, 0))
    out_specs = [row(8 * hd, 0), grp(hd), grp(hd), row(2 * hd, 0),
                 row(2 * hd, 0), colT(NSA_GROUPS * VAUG), colT(NSA_GROUPS * VAUG),
                 row(4 * hd, 0), row(4 * hd, 0), colT(4 * hd), row(4 * hd, 0), colT(hd),
                 colT(NSA_HEADS * 8), colT(2 * NSA_GROUPS * 8)]
    sds = jax.ShapeDtypeStruct
    out_shape = [sds((T, 8 * hd), BF16),
                 sds((NSA_GROUPS, T, hd), BF16), sds((NSA_GROUPS, T, hd), BF16), sds((T, 2 * hd), BF16),
                 sds((T, 2 * hd), BF16),
                 sds((NSA_GROUPS * VAUG, T), BF16), sds((NSA_GROUPS * VAUG, T), BF16),
                 sds((T, 4 * hd), BF16), sds((T, 4 * hd), BF16), sds((4 * hd, T), BF16), sds((T, 4 * hd), BF16),
                 sds((hd, T), F32), sds((NSA_HEADS * 8, T), F32), sds((2 * NSA_GROUPS * 8, T), F32)]
    return pl.pallas_call(
        _prep_kernel, grid=(T // tp,), in_specs=in_specs, out_specs=out_specs, out_shape=out_shape,
        compiler_params=_params(("parallel",), 48), name="prep",
    )(pos_col, inv2, sgn, gq, gks, gkw, gmq, *([P] * 11), Pg)


def _gelu_tanh(x):
    return 0.5 * x * (1.0 + jnp.tanh(0.7978845608028654 * (x + 0.044715 * (x * x * x))))


def _compress_one(x, w1, pe, w2):
    n = x.shape[0]
    ab = jnp.dot(x, w1, preferred_element_type=F32)
    pr = jnp.dot(pe, w1, preferred_element_type=F32)
    pec = pr[0:1, :CMP_HIDDEN] + pr[1:2, CMP_HIDDEN:]
    hid = ab[:, :CMP_HIDDEN] + pltpu.roll(ab[:, CMP_HIDDEN:], n - 1, 0) + pec
    return jnp.dot(_gelu_tanh(hid).astype(BF16), w2, preferred_element_type=F32)


def _compress_kernel(xk_ref, xv_ref, w1k_ref, w1v_ref, pek_ref, pev_ref, w2k_ref, w2v_ref,
                     gk_ref, pos_ref, inv_ref, sgn_ref, kc_o, vcT_o):
    c, s = _rope_tables(pos_ref[...], inv_ref[...], sgn_ref[...])
    k = _compress_one(xk_ref[0], w1k_ref[...], pek_ref[...], w2k_ref[...])
    kc_o[0] = _rope(_rms(k, gk_ref[...]), c, s).astype(BF16)
    v = _compress_one(xv_ref[0], w1v_ref[...], pev_ref[...], w2v_ref[...])
    vcT_o[0] = v.T.astype(BF16)


def _compress(xk, xv, w1k, w1v, pek, pev, w2k, w2v, gk, posc, inv2, sgn):
    G, n, W = xk.shape
    full = lambda a: pl.BlockSpec(a.shape, lambda g: (0,) * a.ndim)
    grp = pl.BlockSpec((1, n, W), lambda g: (g, 0, 0))
    return pl.pallas_call(
        _compress_kernel, grid=(G,),
        in_specs=[grp, grp, full(w1k), full(w1v), full(pek), full(pev), full(w2k), full(w2v),
                  full(gk), full(posc), full(inv2), full(sgn)],
        out_specs=[pl.BlockSpec((1, n, HEAD_DIM), lambda g: (g, 0, 0)),
                   pl.BlockSpec((1, HEAD_DIM, n), lambda g: (g, 0, 0))],
        out_shape=[jax.ShapeDtypeStruct((G, n, HEAD_DIM), BF16), jax.ShapeDtypeStruct((G, HEAD_DIM, n), BF16)],
        compiler_params=_params(("parallel",), 48), name="compress",
    )(xk, xv, w1k, w1v, pek, pev, w2k, w2v, gk, posc, inv2, sgn)


def _memkv_kernel(mem_ref, gm_ref, w_ref, gk_ref, k_o, vT_o):
    hn = _rms(mem_ref[...], gm_ref[...]).astype(BF16)
    kv = jnp.dot(hn, w_ref[...], preferred_element_type=F32)
    mw = MEM_HEADS * HEAD_DIM
    for h in range(MEM_HEADS):
        sl = slice(h * HEAD_DIM, (h + 1) * HEAD_DIM)
        k_o[:, sl] = _rms(kv[:, sl], gk_ref[...]).astype(BF16)
        vT_o[sl, :] = kv[:, mw + h * HEAD_DIM: mw + (h + 1) * HEAD_DIM].T.astype(BF16)


def _memkv(mem, gm, w, gk):
    M = mem.shape[0]
    mw = MEM_HEADS * HEAD_DIM
    return pl.pallas_call(
        _memkv_kernel,
        out_shape=[jax.ShapeDtypeStruct((M, mw), BF16), jax.ShapeDtypeStruct((mw, M), BF16)],
        compiler_params=pltpu.CompilerParams(vmem_limit_bytes=32 * MIB), name="mem_kv",
    )(mem, gm, w, gk)


def _cmp_kernel(q_ref, kc_ref, vcT_ref, ovT_ref, g_ref, *rest, tq, i0, n_pad, n_sel, top_k):
    y_ref, b_ref = rest[-2:]
    i = i0 + pl.program_id(1)
    t = i * tq + lax.broadcasted_iota(jnp.int32, (1, tq), 1)
    n_end = lax.broadcasted_iota(jnp.int32, (n_pad, 1), 0) * CMP_STRIDE + (CMP_LEN - 1)
    valid = n_end <= t
    kc = kc_ref[0]
    vcT = vcT_ref[0]
    has_valid = (t >= CMP_LEN - 1).astype(F32)
    sms = []
    for h in range(NSA_HPG):
        q_h = q_ref[:, h * HEAD_DIM:(h + 1) * HEAD_DIM]
        sms.append(jnp.where(valid, lax.dot_general(kc, q_h, _NT, preferred_element_type=F32), NEG_BIG))
    psum = jnp.zeros((n_pad, tq), F32)
    for h in range(NSA_HPG):
        e = jnp.exp2(sms[h] - jnp.max(sms[h], axis=0, keepdims=True))
        p = e * (has_valid / jnp.sum(e, axis=0, keepdims=True))
        oT = jnp.dot(vcT, p.astype(BF16), preferred_element_type=F32)
        y_ref[:, h * HEAD_DIM:(h + 1) * HEAD_DIM] = (oT * g_ref[0, 0, h:h + 1, :]).T.astype(BF16)
        psum = psum + p
    hi = psum.astype(BF16)
    r1 = psum - hi.astype(F32)
    mid = r1.astype(BF16)
    lo = (r1 - mid.astype(F32)).astype(BF16)
    parts = jnp.dot(ovT_ref[...], jnp.concatenate([hi, mid, lo], axis=1), preferred_element_type=F32)
    imp = parts[:, :tq] + parts[:, tq:2 * tq] + parts[:, 2 * tq:]
    s_i = lax.broadcasted_iota(jnp.int32, (n_sel, 1), 0)
    cur = lax.shift_right_logical(t, 6)
    forced = (s_i == 0) | (s_i == cur) | (s_i == cur - 1)
    future = s_i * SEL_BLOCK > t
    w = jnp.where(forced, jnp.inf, jnp.where(future, -jnp.inf, imp))
    s_f = jnp.broadcast_to(s_i.astype(F32), (n_sel, tq))
    for _ in range(top_k):
        m = jnp.max(w, axis=0, keepdims=True)
        idx = jnp.min(jnp.where(w == m, s_f, float(n_sel)), axis=0, keepdims=True)
        w = jnp.where(s_f == idx, -jnp.inf, w)
    b_ref[0, :n_sel, :] = jnp.where(future, NEG_BIG, jnp.where(w == -jnp.inf, 0.0, NEG_BIG))
    n_all = b_ref.shape[1]
    if n_sel < n_all:
        b_ref[0, n_sel:, :] = jnp.full((n_all - n_sel, tq), NEG_BIG, F32)


def _round_up(n, m):
    return -(-n // m) * m


def _cmp_select(qn, kc, vcT, ovT, gates):
    T = qn.shape[0]
    G, n_pad, _ = kc.shape
    n_sel = T // SEL_BLOCK
    tq = _tile(T, 512)
    nq = T // tq
    gw = NSA_HPG * HEAD_DIM
    nb = min(CMP_BUCKETS, nq)
    assert nq % nb == 0
    per = nq // nb
    out_shape = [jax.ShapeDtypeStruct((T, NSA_HEADS * HEAD_DIM), BF16), jax.ShapeDtypeStruct((G, n_sel, T), F32)]
    buffers = ()
    for b in range(nb):
        i0 = b * per
        t_max = (i0 + per) * tq - 1
        n_len = min(n_pad, _round_up(max(t_max - (CMP_LEN - 1), 0) // CMP_STRIDE + 1, LANE))
        s_len = min(n_sel, _round_up(t_max // SEL_BLOCK + 1, 8))
        kern = functools.partial(_cmp_kernel, tq=tq, i0=i0, n_pad=n_len, n_sel=s_len, top_k=min(SEL_TOPK, n_sel))
        buffers = pl.pallas_call(
            kern, grid=(G, per),
            in_specs=[pl.BlockSpec((tq, gw), lambda g, i, i0=i0: (i0 + i, g)),
                      pl.BlockSpec((1, n_len, HEAD_DIM), lambda g, i: (g, 0, 0)),
                      pl.BlockSpec((1, HEAD_DIM, n_len), lambda g, i: (g, 0, 0)),
                      pl.BlockSpec((s_len, n_len), lambda g, i: (0, 0)),
                      pl.BlockSpec((1, 1, 8, tq), lambda g, i, i0=i0: (0, g, 0, i0 + i))]
            + [pl.BlockSpec(memory_space=pl.ANY)] * len(buffers),
            out_specs=[pl.BlockSpec((tq, gw), lambda g, i, i0=i0: (i0 + i, g)),
                       pl.BlockSpec((1, n_sel, tq), lambda g, i, i0=i0: (g, 0, i0 + i))],
            out_shape=out_shape,
            input_output_aliases={5: 0, 6: 1} if buffers else {},
            compiler_params=_params(("parallel", "parallel"), 48), name="cmp_select_%d" % b,
        )(qn, kc, vcT, ovT, gates, *buffers)
    return buffers


def _flash_kernel(qi_ref, kj_ref, fl_ref, q_ref, k_ref, vT_ref, g_ref, sh_ref, *rest, mode, bounded, tq, tk):
    if mode == "sel":
        bias_ref, o_ref, m_sc, acc_sc = rest
    else:
        o_ref, m_sc, acc_sc = rest
    p = pl.program_id(1)
    i = qi_ref[p]
    j = kj_ref[p]
    fl = fl_ref[p]

    @pl.when((fl & 1) != 0)
    def _():
        m_sc[...] = jnp.full(m_sc.shape, NEG_BIG, F32)
        acc_sc[...] = jnp.zeros(acc_sc.shape, F32)

    shift = sh_ref[0, 0:1, :] if bounded else 0.0
    nb = tk // SEL_BLOCK
    if mode == "sel":
        rows = bias_ref[0] - shift

    def key_mask():
        t = i * tq + lax.broadcasted_iota(jnp.int32, (1, tq), 1)
        kpos = j * tk + lax.broadcasted_iota(jnp.int32, (tk, 1), 0)
        if mode == "sel":
            base = jnp.concatenate(
                [jnp.broadcast_to(rows[b:b + 1, :], (SEL_BLOCK, tq)) for b in range(nb)], axis=0)
            return jnp.where(kpos <= t, base, NEG_BIG)
        return jnp.where((kpos <= t) & (kpos > t - WINDOW), 0.0 - shift, NEG_BIG)

    k = k_ref[...]
    vT = vT_ref[...]
    raw = lambda h: lax.dot_general(k, q_ref[:, h * HEAD_DIM:(h + 1) * HEAD_DIM], _NT, preferred_element_type=F32)

    def accumulate(pTs):
        for h in range(NSA_HPG):
            acc_sc[h] = acc_sc[h] + jnp.dot(vT, pTs[h], preferred_element_type=F32)

    if bounded and mode == "sel":
        @pl.when((fl & 4) != 0)
        def _():
            mask_add = key_mask()
            accumulate([jnp.exp2(raw(h) + mask_add).astype(BF16) for h in range(NSA_HPG)])

        @pl.when((fl & 4) == 0)
        def _():
            def probs(h):
                s = raw(h).reshape(nb, SEL_BLOCK, tq) + rows[:, None, :]
                return jnp.exp2(s).astype(BF16).reshape(tk, tq)
            accumulate([probs(h) for h in range(NSA_HPG)])
    elif bounded:
        mask_add = key_mask()
        accumulate([jnp.exp2(raw(h) + mask_add).astype(BF16) for h in range(NSA_HPG)])
    else:
        mask_add = key_mask()
        sTs = [raw(h) + mask_add for h in range(NSA_HPG)]
        pTs, alphas = [], []
        for h in range(NSA_HPG):
            sT = sTs[h]
            m_old = m_sc[h]
            m_new = jnp.maximum(m_old, jnp.max(sT, axis=0, keepdims=True))
            alphas.append(jnp.exp2(m_old - m_new))
            pTs.append(jnp.exp2((sT - m_new).astype(BF16)))
            m_sc[h] = m_new
        for h in range(NSA_HPG):
            acc_sc[h] = alphas[h] * acc_sc[h] + jnp.dot(vT, pTs[h], preferred_element_type=F32)

    @pl.when((fl & 2) != 0)
    def _():
        for h in range(NSA_HPG):
            acc = acc_sc[h]
            l = acc[HEAD_DIM:HEAD_DIM + 1, :]
            o = acc[:HEAD_DIM, :] * ((1.0 / l) * g_ref[0, 0, h:h + 1, :])
            o_ref[:, h * HEAD_DIM:(h + 1) * HEAD_DIM] = o.T.astype(BF16)


def _steps(nq, lo_fn, hi_fn, reverse=False):
    qi, kj, fl = [], [], []
    for i in range(nq):
        js = list(range(lo_fn(i), hi_fn(i) + 1))
        if reverse:
            js = js[::-1]
        for n, j in enumerate(js):
            qi.append(i)
            kj.append(j)
            fl.append((1 if n == 0 else 0) | (2 if n == len(js) - 1 else 0))
    return (jnp.asarray(np.array(qi, np.int32)), jnp.asarray(np.array(kj, np.int32)),
            jnp.asarray(np.array(fl, np.int32)))


def _nsa_flash(mode, qn, k, vT, gates, shift, bias=None):
    use_bounded = jnp.max(shift) <= SOFTMAX_BOUND_MAX
    return lax.cond(use_bounded,
                    lambda: _nsa_flash_call(mode, True, qn, k, vT, gates, shift, bias),
                    lambda: _nsa_flash_call(mode, False, qn, k, vT, gates, shift, bias))


def _nsa_flash_call(mode, bounded, qn, k, vT, gates, shift, bias):
    T = qn.shape[0]
    tq = _tile(T, 1024)
    tk = _tile(T, 1024 if mode == "sel" else 512)
    nq = T // tq
    gw = NSA_HPG * HEAD_DIM
    hi = lambda i: ((i + 1) * tq - 1) // tk
    if mode == "sel":
        lo = lambda i: 0
        br = 1
    else:
        lo = lambda i: max(0, (i * tq - (WINDOW - 1)) // tk)
        br = 2
    qi, kj, fl = _steps(nq, lo, hi)
    fl = fl | 4 * ((kj + 1) * tk - 1 > qi * tq).astype(jnp.int32)
    in_specs = [pl.BlockSpec((tq, gw), lambda g, p, qi, kj, fl: (qi[p], g)),
                pl.BlockSpec((tk, HEAD_DIM), lambda g, p, qi, kj, fl: (kj[p], g)),
                pl.BlockSpec((VAUG, tk), lambda g, p, qi, kj, fl: (g, kj[p])),
                pl.BlockSpec((1, 1, 8, tq), lambda g, p, qi, kj, fl, br=br: (br, g, 0, qi[p])),
                pl.BlockSpec((1, 8, tq), lambda g, p, qi, kj, fl: (g, 0, qi[p]))]
    args = [qn, k, vT, gates, shift]
    if mode == "sel":
        in_specs.append(pl.BlockSpec((1, tk // SEL_BLOCK, tq), lambda g, p, qi, kj, fl: (g, kj[p], qi[p])))
        args.append(bias)
    kern = functools.partial(_flash_kernel, mode=mode, bounded=bounded, tq=tq, tk=tk)
    return pl.pallas_call(
        kern,
        grid_spec=pltpu.PrefetchScalarGridSpec(
            num_scalar_prefetch=3, grid=(NSA_GROUPS, int(qi.shape[0])),
            in_specs=in_specs,
            out_specs=pl.BlockSpec((tq, gw), lambda g, p, qi, kj, fl: (qi[p], g)),
            scratch_shapes=[pltpu.VMEM((NSA_HPG, 1, tq), F32), pltpu.VMEM((NSA_HPG, VAUG, tq), F32)]),
        out_shape=jax.ShapeDtypeStruct((T, NSA_HEADS * HEAD_DIM), BF16),
        compiler_params=_params(("parallel", "arbitrary"), 48),
        name="nsa_" + mode + ("_bounded" if bounded else "_online"),
    )(qi, kj, fl, *args)


def _sb_kernel(qi_ref, kj_ref, fl_ref, q_ref, k_ref, vT_ref, lm_ref, *rest, tq, tk, resume):
    if resume:
        acc_in, carry_in, o_ref, carry_sc, acc_sc = rest
    else:
        o_ref, acc_o, carry_o, carry_sc, acc_sc = rest
    p = pl.program_id(0)
    i = qi_ref[p]
    j = kj_ref[p]
    fl = fl_ref[p]

    @pl.when((fl & 1) != 0)
    def _():
        if resume:
            for h in range(SB_HEADS):
                carry_sc[h] = carry_in[h:h + 1, :]
                acc_sc[h] = acc_in[h * HEAD_DIM:(h + 1) * HEAD_DIM, :]
        else:
            carry_sc[...] = jnp.zeros(carry_sc.shape, F32)
            acc_sc[...] = jnp.zeros(acc_sc.shape, F32)

    @pl.when((jnp.max(carry_sc[...]) > SB_DEAD_LOG2) & ((fl & 4) == 0))
    def _():
        t = i * tq + lax.broadcasted_iota(jnp.int32, (1, tq), 1)
        kpos = j * tk + lax.broadcasted_iota(jnp.int32, (tk, 1), 0)
        past = kpos < t
        lm = lm_ref[...]
        hs = [slice(h * HEAD_DIM, (h + 1) * HEAD_DIM) for h in range(SB_HEADS)]
        zs = [lax.dot_general(k_ref[:, sl], q_ref[:, sl], _NT, preferred_element_type=F32) for sl in hs]
        lgs, css = [], []
        for h in range(SB_HEADS):
            z = zs[h]
            sp = jnp.maximum(z, 0.0) + jnp.log2(1.0 + jnp.exp2(-jnp.abs(z)))
            lk = jnp.where(past, -sp, 0.0)
            hi = lk.astype(BF16)
            lo = (lk - hi.astype(F32)).astype(BF16)
            css.append(jnp.dot(lm, jnp.concatenate([hi, lo], axis=1), preferred_element_type=F32))
            lgs.append(z - sp)
            carry_old = carry_sc[h]
            carry_sc[h] = carry_old + jnp.sum(lk, axis=0, keepdims=True)
            css[h] = css[h][:, :tq] + css[h][:, tq:] + carry_old
        for h in range(SB_HEADS):
            wgt = jnp.where(past, jnp.exp2(lgs[h] + css[h]), 0.0)
            acc_sc[h] = acc_sc[h] + jnp.dot(vT_ref[hs[h], :], wgt.astype(BF16), preferred_element_type=F32)

    @pl.when((fl & 2) != 0)
    def _():
        for h in range(SB_HEADS):
            o_ref[:, h * HEAD_DIM:(h + 1) * HEAD_DIM] = acc_sc[h].T.astype(BF16)
            if not resume:
                acc_o[h * HEAD_DIM:(h + 1) * HEAD_DIM, :] = acc_sc[h]
                carry_o[h:h + 1, :] = carry_sc[h]
        if not resume:
            carry_o[SB_HEADS:, :] = jnp.zeros((8 - SB_HEADS, tq), F32)


def _sb_call(q, k, vT, steps, tq, tk, state=None):
    T, W = q.shape
    qi, kj, fl = steps
    lmat = jnp.asarray(np.triu(np.ones((tk, tk), np.float32), 1), BF16)
    resume = state is not None
    qtile = lambda shape: pl.BlockSpec(shape, lambda p, qi, kj, fl: (qi[p], 0))
    qtileT = lambda rows: pl.BlockSpec((rows, tq), lambda p, qi, kj, fl: (0, qi[p]))
    in_specs = [qtile((tq, W)),
                pl.BlockSpec((tk, W), lambda p, qi, kj, fl: (kj[p], 0)),
                pl.BlockSpec((W, tk), lambda p, qi, kj, fl: (0, kj[p])),
                pl.BlockSpec((tk, tk), lambda p, qi, kj, fl: (0, 0))]
    y_sds = jax.ShapeDtypeStruct((T, W), BF16)
    if resume:
        in_specs += [qtileT(W), qtileT(8)]
        out_specs, out_shape = qtile((tq, W)), y_sds
    else:
        out_specs = [qtile((tq, W)), qtileT(W), qtileT(8)]
        out_shape = [y_sds, jax.ShapeDtypeStruct((W, T), F32), jax.ShapeDtypeStruct((8, T), F32)]
    return pl.pallas_call(
        functools.partial(_sb_kernel, tq=tq, tk=tk, resume=resume),
        grid_spec=pltpu.PrefetchScalarGridSpec(
            num_scalar_prefetch=3, grid=(int(qi.shape[0]),),
            in_specs=in_specs, out_specs=out_specs,
            scratch_shapes=[pltpu.VMEM((SB_HEADS, 1, tq), F32), pltpu.VMEM((SB_HEADS, HEAD_DIM, tq), F32)]),
        out_shape=out_shape,
        compiler_params=_params(("arbitrary",), 48), name="sb_far" if resume else "sb_near",
    )(qi, kj, fl, q, k, vT, lmat, *(state or ()))


def _stick_breaking(q, k, vT):
    T, W = q.shape
    tq = _tile(T, 512)
    tk = _tile(T, 256)
    nq = T // tq
    hi = lambda i: ((i + 1) * tq - 2) // tk
    lo_near = lambda i: max(0, hi(i) - SB_NEAR_TILES + 1)
    near = _steps(nq, lo_near, hi, reverse=True)
    y_near, acc, carry = _sb_call(q, k, vT, near, tq, tk)
    qi, kj, fl = [], [], []
    for i in range(nq):
        js = list(range(lo_near(i) - 1, -1, -1))
        for n, j in enumerate(js or [0]):
            qi.append(i)
            kj.append(j)
            fl.append((1 if n == 0 else 0) | (2 if n == max(len(js), 1) - 1 else 0) | (0 if js else 4))
    far = tuple(jnp.asarray(np.array(a, np.int32)) for a in (qi, kj, fl))
    with_far = [i for i in range(nq) if lo_near(i) > 0]
    if not with_far:
        return y_near
    alive = jnp.max(carry[:SB_HEADS, with_far[0] * tq:]) > SB_DEAD_LOG2
    return lax.cond(alive, lambda: _sb_call(q, k, vT, far, tq, tk, state=(acc, carry)), lambda: y_near)


def _memattn_kernel(q_ref, k_ref, vT_ref, o_ref):
    for h in range(MEM_HEADS):
        sl = slice(h * HEAD_DIM, (h + 1) * HEAD_DIM)
        sT = lax.dot_general(k_ref[:, sl], q_ref[:, sl], _NT, preferred_element_type=F32)
        e = jnp.exp2(sT - jnp.max(sT, axis=0, keepdims=True))
        l = jnp.sum(e, axis=0, keepdims=True)
        oT = jnp.dot(vT_ref[sl, :], e.astype(BF16), preferred_element_type=F32) * (1.0 / l)
        o_ref[:, sl] = oT.T.astype(BF16)


def _mem_attention(q, k, vT):
    T, W = q.shape
    M = k.shape[0]
    tq = _tile(T, 512)
    return pl.pallas_call(
        _memattn_kernel, grid=(T // tq,),
        in_specs=[pl.BlockSpec((tq, W), lambda i: (i, 0)),
                  pl.BlockSpec((M, W), lambda i: (0, 0)),
                  pl.BlockSpec((W, M), lambda i: (0, 0))],
        out_specs=pl.BlockSpec((tq, W), lambda i: (i, 0)),
        out_shape=jax.ShapeDtypeStruct((T, W), BF16),
        compiler_params=_params(("parallel",), 32), name="mem_attention",
    )(q, k, vT)


def _mixout_kernel(yc_ref, ys_ref, yw_ref, ysb_ref, ym_ref, g_ref, wn_ref, wsb_ref, wm_ref, wo_ref, x_ref, gain_ref,
                   x2_o, h2_o):
    D = x_ref.shape[1]
    yn = (yc_ref[...].astype(F32) + ys_ref[...].astype(F32) + yw_ref[...].astype(F32)).astype(BF16)
    a = jnp.dot(yn, wn_ref[...], preferred_element_type=F32)
    b = jnp.dot(ysb_ref[...], wsb_ref[...], preferred_element_type=F32)
    c = jnp.dot(ym_ref[...], wm_ref[...], preferred_element_type=F32)
    sig = lambda n: jax.nn.sigmoid(g_ref[:, n * D:(n + 1) * D].astype(F32))
    mixed = (sig(0) * a + sig(1) * b + sig(2) * c).astype(BF16)
    x2 = x_ref[...] + jnp.dot(mixed, wo_ref[...], preferred_element_type=F32)
    x2_o[...] = x2
    h2_o[...] = _rms(x2, gain_ref[...]).astype(BF16)


def _mix_out(yc, ys, yw, ysb, ym, P, wn, wsb, wm, wo, x, gain):
    T, D = x.shape
    tm = _tile(T, 256)
    rowi = lambda w: pl.BlockSpec((tm, w), lambda i: (i, 0))
    res = lambda a: pl.BlockSpec(a.shape, lambda i: (0, 0), pipeline_mode=pl.Buffered(1))
    return pl.pallas_call(
        _mixout_kernel, grid=(T // tm,),
        in_specs=[rowi(yc.shape[1]), rowi(ys.shape[1]), rowi(yw.shape[1]), rowi(ysb.shape[1]), rowi(ym.shape[1]),
                  rowi(N_BRANCH * D), res(wn), res(wsb), res(wm), res(wo), rowi(D), res(gain)],
        out_specs=[rowi(D), rowi(D)],
        out_shape=[jax.ShapeDtypeStruct((T, D), F32), jax.ShapeDtypeStruct((T, D), BF16)],
        compiler_params=_params(("parallel",), 48), name="mix_out",
    )(yc, ys, yw, ysb, ym, P, wn, wsb, wm, wo, x, gain)


def _ffn_kernel(h_ref, wg_ref, wu_ref, wd_ref, x_ref, o_ref, z_sc, *, nf):
    f = pl.program_id(1)

    def up():
        a = jnp.dot(h_ref[...], wg_ref[...], preferred_element_type=F32)
        b = jnp.dot(h_ref[...], wu_ref[...], preferred_element_type=F32)
        return (a * jax.nn.sigmoid(a) * b).astype(BF16)

    def down():
        return jnp.dot(z_sc[...], wd_ref[...], preferred_element_type=F32)

    @pl.when(f == 0)
    def _():
        z_sc[...] = up()
        o_ref[...] = x_ref[...]

    @pl.when((f > 0) & (f < nf))
    def _():
        c = down()
        z_new = up()
        o_ref[...] += c
        z_sc[...] = z_new

    @pl.when(f == nf)
    def _():
        o_ref[...] += down()


def _ffn(h2, wg, wu, wd, x2):
    T, D = x2.shape
    F = wg.shape[1]
    tm = _tile(T, 1024)
    tf = 512
    assert F % tf == 0
    nf = F // tf
    assert nf >= 2
    return pl.pallas_call(
        functools.partial(_ffn_kernel, nf=nf), grid=(T // tm, nf + 1),
        in_specs=[pl.BlockSpec((tm, D), lambda i, f: (i, 0)),
                  pl.BlockSpec((D, tf), lambda i, f: (0, jnp.minimum(f, nf - 1))),
                  pl.BlockSpec((D, tf), lambda i, f: (0, jnp.minimum(f, nf - 1))),
                  pl.BlockSpec((tf, D), lambda i, f: (jnp.maximum(f - 1, 0), 0)),
                  pl.BlockSpec((tm, D), lambda i, f: (i, 0), pipeline_mode=pl.Buffered(1))],
        out_specs=pl.BlockSpec((tm, D), lambda i, f: (i, 0)),
        out_shape=jax.ShapeDtypeStruct((T, D), F32),
        scratch_shapes=[pltpu.VMEM((tm, tf), BF16)],
        compiler_params=_params(("parallel", "arbitrary"), 56), name="ffn",
    )(h2, wg.astype(BF16), wu.astype(BF16), wd.astype(BF16), x2)


def _layer(x, mem, pos_col, posc_col, consts, attn_norm, w_in, nsa_q_norm, nsa_kc_norm, nsa_ks_norm, nsa_kw_norm,
           cmp_k_pe, cmp_k_w1, cmp_k_w2, cmp_v_pe, cmp_v_w1, cmp_v_w2, mem_norm, w_mem_kv,
           mem_q_norm, mem_k_norm, w_o_nsa, w_o_sb, w_o_mem, w_out, ffn_norm,
           w_ffn_gate, w_ffn_up, w_ffn_down):
    T, D = x.shape
    inv2, sgn, ovT = consts
    hd = HEAD_DIM
    row = lambda g: g.reshape(1, -1)

    q_w, kv_w, gn_w = NSA_HEADS * hd, 6 * NSA_GROUPS * hd, 3 * NSA_HEADS
    sb_w, mq_w, gm_w = 3 * SB_HEADS * hd, MEM_HEADS * hd, N_BRANCH * D
    o_gn = q_w + kv_w
    o_sb = o_gn + gn_w
    o_mq = o_sb + sb_w
    o_gm = o_mq + mq_w
    assert w_in.shape[1] == o_gm + gm_w
    segs = ((0, o_gm, gm_w), (gm_w, 0, o_gn), (gm_w + o_gn, o_sb, o_gm - o_sb))
    w_main, w_gate = _regroup_w_in(w_in, segs, (o_gn, gn_w), gm_w + o_gn + o_gm - o_sb)
    P, Pg = _in_proj(x, row(attn_norm), w_main, w_gate)

    (qn, kc_raw, vc_raw, ksn, kwn, vsT, vwT, sbq, sbk, sbvT, memq, gT, qn2, kn2) = _prep(
        P, Pg, gm_w, pos_col, inv2, sgn, row(nsa_q_norm), row(nsa_ks_norm), row(nsa_kw_norm), row(mem_q_norm))

    q2 = jnp.max(qn2[::8].reshape(NSA_GROUPS, NSA_HPG, T), axis=1)
    k2 = jnp.max(kn2[::8], axis=1).reshape(2, NSA_GROUPS)
    bound = lambda kk: jnp.broadcast_to((1.02 * jnp.sqrt(q2 * kk[:, None]))[:, None, :], (NSA_GROUPS, 8, T))
    shift_sel, shift_win = bound(k2[0]), bound(k2[1])

    gates = gT[:gn_w].reshape(NSA_GROUPS, NSA_HPG, 3, T).transpose(2, 0, 1, 3)
    gates = jnp.pad(gates, ((0, 0), (0, 0), (0, 8 - NSA_HPG), (0, 0)))

    n_pad = T // CMP_STRIDE
    half = CMP_LEN // 2

    def w1_pack(w1):
        return jnp.concatenate([w1[:half].reshape(half * hd, -1), w1[half:].reshape(half * hd, -1)], axis=1).astype(BF16)

    def pe_pack(pe):
        return jnp.pad(pe.reshape(2, half * hd), ((0, 6), (0, 0))).astype(BF16)

    kc, vcT = _compress(
        kc_raw.reshape(NSA_GROUPS, n_pad, CMP_STRIDE * hd), vc_raw.reshape(NSA_GROUPS, n_pad, CMP_STRIDE * hd),
        w1_pack(cmp_k_w1), w1_pack(cmp_v_w1), pe_pack(cmp_k_pe), pe_pack(cmp_v_pe),
        cmp_k_w2.astype(BF16), cmp_v_w2.astype(BF16), row(nsa_kc_norm), posc_col, inv2, sgn)

    y_cmp, bias = _cmp_select(qn, kc, vcT, ovT, gates)
    y_sel = _nsa_flash("sel", qn, ksn, vsT, gates, shift_sel, bias)
    y_win = _nsa_flash("win", qn, kwn, vwT, gates, shift_win)
    y_sb = _stick_breaking(sbq, sbk, sbvT)

    mk, mvT = _memkv(mem, row(mem_norm), w_mem_kv.astype(BF16), row(mem_k_norm))
    y_mem = _mem_attention(memq, mk, mvT)

    x2, h2 = _mix_out(y_cmp, y_sel, y_win, y_sb, y_mem, P, w_o_nsa.astype(BF16), w_o_sb.astype(BF16),
                      w_o_mem.astype(BF16), w_out.astype(BF16), x, row(ffn_norm))
    return _ffn(h2, w_ffn_gate, w_ffn_up, w_ffn_down, x2)


def kernel(x, mem, positions, attn_norm, w_in, nsa_q_norm, nsa_kc_norm, nsa_ks_norm, nsa_kw_norm, cmp_k_pe, cmp_k_w1, cmp_k_w2, cmp_v_pe, cmp_v_w1, cmp_v_w2, mem_norm, w_mem_kv, mem_q_norm, mem_k_norm, w_o_nsa, w_o_sb, w_o_mem, w_out, ffn_norm, w_ffn_gate, w_ffn_up, w_ffn_down):
    B, T, D = x.shape
    assert T % (4 * LANE) == 0 and T // SEL_BLOCK >= 8
    n_pad = T // CMP_STRIDE
    n_sel = T // SEL_BLOCK
    inv = 1.0 / (ROPE_THETA ** (jnp.arange(0, HEAD_DIM, 2, dtype=F32) / HEAD_DIM))
    inv2 = jnp.concatenate([inv, inv]).reshape(1, HEAD_DIM)
    sgn = jnp.concatenate([-jnp.ones((HEAD_DIM // 2,), F32), jnp.ones((HEAD_DIM // 2,), F32)]).reshape(1, HEAD_DIM)
    cs = np.arange(n_pad)[None, :] * CMP_STRIDE
    ss = np.arange(n_sel)[:, None] * SEL_BLOCK
    ovT = jnp.asarray(((cs < ss + SEL_BLOCK) & (cs + CMP_LEN - 1 >= ss)).astype(np.float32), BF16)
    consts = (inv2, sgn, ovT)
    depth = w_in.shape[0]
    outs = []
    for b in range(B):
        xb = x[b]
        posf = positions[b].astype(F32)
        pos_col = posf.reshape(T, 1)
        posc = jnp.concatenate([posf[CMP_LEN - 1::CMP_STRIDE], posf[-1:]]).reshape(n_pad, 1)
        for l in range(depth):
            xb = _layer(xb, mem[b], pos_col, posc, consts, attn_norm[l], w_in[l], nsa_q_norm[l], nsa_kc_norm[l],
                        nsa_ks_norm[l], nsa_kw_norm[l], cmp_k_pe[l], cmp_k_w1[l], cmp_k_w2[l], cmp_v_pe[l],
                        cmp_v_w1[l], cmp_v_w2[l], mem_norm[l], w_mem_kv[l], mem_q_norm[l], mem_k_norm[l],
                        w_o_nsa[l], w_o_sb[l], w_o_mem[l], w_out[l], ffn_norm[l],
                        w_ffn_gate[l], w_ffn_up[l], w_ffn_down[l])
        outs.append(xb)
    return outs[0][None] if B == 1 else jnp.stack(outs, axis=0)
```

```python
import functools

import numpy as np
import jax
import jax.numpy as jnp
from jax import lax
from jax.experimental import pallas as pl
from jax.experimental.pallas import tpu as pltpu

HEAD_DIM = 128
NSA_HEADS = 8
NSA_GROUPS = 2
NSA_HPG = NSA_HEADS // NSA_GROUPS
SB_HEADS = 4
MEM_HEADS = 4
CMP_LEN = 32
CMP_STRIDE = 16
CMP_HIDDEN = 2 * HEAD_DIM
SEL_BLOCK = 64
SEL_TOPK = 16
WINDOW = 512
ROPE_THETA = 10000.0
NORM_EPS = 1e-6
NEG_BIG = -1e30
N_BRANCH = 3
SCALE = HEAD_DIM ** -0.5
LOG2E = 1.4426950408889634
QSCALE = SCALE * LOG2E
SOFTMAX_BOUND_MAX = 50.0
CMP_BUCKETS = 8
SB_NEAR_TILES = 3
SB_DEAD_LOG2 = -160.0
VAUG = HEAD_DIM + 16

LANE = 128
MIB = 1 << 20
BF16 = jnp.bfloat16
F32 = jnp.float32

_NT = (((1,), (1,)), ((), ()))


def _tile(n, pref):
    t = min(n, pref)
    assert n % t == 0, (n, pref)
    return t


def _params(sem, vmem_mib):
    return pltpu.CompilerParams(dimension_semantics=sem, vmem_limit_bytes=vmem_mib * MIB)


def _rms(x, gain):
    return x * lax.rsqrt(jnp.mean(x * x, axis=-1, keepdims=True) + NORM_EPS) * gain


def _rope_tables(pos, inv2, sgn):
    ang = pos * inv2
    return jnp.cos(ang), jnp.sin(ang) * sgn


def _rope(x, c, s):
    return x * c + pltpu.roll(x, HEAD_DIM // 2, 1) * s


def _regroup_kernel(w_ref, o_ref, og_ref, *, segs, gate):
    for dst, src, width in segs:
        o_ref[:, dst:dst + width] = w_ref[0, :, src:src + width].astype(BF16)
    g0, gw = gate
    lane = lax.broadcasted_iota(jnp.int32, (w_ref.shape[1], LANE), 1)
    og_ref[...] = jnp.where(lane < gw, w_ref[0, :, g0:g0 + LANE], 0.0).astype(BF16)


def _regroup_w_in(w_in_all, layer, segs, gate, n_out):
    _, D, N = w_in_all.shape
    tr = _tile(D, 64)
    return pl.pallas_call(
        functools.partial(_regroup_kernel, segs=segs, gate=gate), grid=(D // tr,),
        in_specs=[pl.BlockSpec((1, tr, N), lambda i: (layer, i, 0))],
        out_specs=[pl.BlockSpec((tr, n_out), lambda i: (i, 0)), pl.BlockSpec((tr, LANE), lambda i: (i, 0))],
        out_shape=[jax.ShapeDtypeStruct((D, n_out), BF16), jax.ShapeDtypeStruct((D, LANE), BF16)],
        compiler_params=_params(("parallel",), 32), name="regroup_w_in",
    )(w_in_all)


def _proj_kernel(x_ref, g_ref, w_ref, wg_ref, o_ref, og_ref, hn_ref):
    @pl.when(pl.program_id(1) == 0)
    def _():
        hn_ref[...] = _rms(x_ref[...], g_ref[...]).astype(BF16)
        og_ref[...] = jnp.dot(hn_ref[...], wg_ref[...], preferred_element_type=F32)

    o_ref[...] = jnp.dot(hn_ref[...], w_ref[...], preferred_element_type=F32).astype(BF16)


def _in_proj(x, gain, w_main, w_gate):
    T, D = x.shape
    N = w_main.shape[1]
    tm = _tile(T, 1024)
    tn = 1536
    assert N % tn == 0
    return pl.pallas_call(
        _proj_kernel,
        grid=(T // tm, N // tn),
        in_specs=[
            pl.BlockSpec((tm, D), lambda i, j: (i, 0)),
            pl.BlockSpec((1, D), lambda i, j: (0, 0)),
            pl.BlockSpec((D, tn), lambda i, j: (0, j)),
            pl.BlockSpec((D, LANE), lambda i, j: (0, 0)),
        ],
        out_specs=[pl.BlockSpec((tm, tn), lambda i, j: (i, j)), pl.BlockSpec((tm, LANE), lambda i, j: (i, 0))],
        out_shape=[jax.ShapeDtypeStruct((T, N), BF16), jax.ShapeDtypeStruct((T, LANE), F32)],
        scratch_shapes=[pltpu.VMEM((tm, D), BF16)],
        compiler_params=_params(("parallel", "arbitrary"), 52),
        name="in_proj",
    )(x, gain, w_main, w_gate)


def _prep_kernel(pos_ref, inv_ref, sgn_ref, gq_ref, gks_ref, gkw_ref, gmq_ref,
                 q_ref, kc_ref, vc_ref, ks_ref, vs_ref, kw_ref, vw_ref,
                 sq_ref, sk_ref, sv_ref, mq_ref, gn_ref,
                 qn_o, kc_o, vc_o, ks_o, kw_o, vsT_o, vwT_o, sq_o, sk_o, svT_o, mq_o, gT_o, qn2_o, kn2_o):
    c, s = _rope_tables(pos_ref[...], inv_ref[...], sgn_ref[...])
    hd = HEAD_DIM
    tp = pos_ref.shape[0]
    f32 = lambda ref, sl: ref[:, sl].astype(F32)
    ones8 = jnp.ones((8, hd), F32)

    def sqnorm_rows(xb):
        x = xb.astype(F32)
        return lax.dot_general(ones8, x * x, _NT, preferred_element_type=F32)

    for h in range(NSA_HEADS):
        sl = slice(h * hd, (h + 1) * hd)
        qb = (_rope(_rms(f32(q_ref, sl), gq_ref[...]), c, s) * QSCALE).astype(BF16)
        qn_o[:, sl] = qb
        qn2_o[h * 8:(h + 1) * 8, :] = sqnorm_rows(qb)
    ones_rows = (lax.broadcasted_iota(jnp.int32, (VAUG - hd, tp), 0) == 0).astype(F32).astype(BF16)
    for g in range(NSA_GROUPS):
        sl = slice(g * hd, (g + 1) * hd)
        kc_o[g] = kc_ref[:, sl]
        vc_o[g] = vc_ref[:, sl]
        ksb = _rope(_rms(f32(ks_ref, sl), gks_ref[...]), c, s).astype(BF16)
        kwb = _rope(_rms(f32(kw_ref, sl), gkw_ref[...]), c, s).astype(BF16)
        ks_o[:, sl] = ksb
        kw_o[:, sl] = kwb
        kn2_o[g * 8:(g + 1) * 8, :] = sqnorm_rows(ksb)
        kn2_o[(NSA_GROUPS + g) * 8:(NSA_GROUPS + g + 1) * 8, :] = sqnorm_rows(kwb)
        vsT_o[g * VAUG:g * VAUG + hd, :] = f32(vs_ref, sl).T.astype(BF16)
        vsT_o[g * VAUG + hd:(g + 1) * VAUG, :] = ones_rows
        vwT_o[g * VAUG:g * VAUG + hd, :] = f32(vw_ref, sl).T.astype(BF16)
        vwT_o[g * VAUG + hd:(g + 1) * VAUG, :] = ones_rows
    for h in range(SB_HEADS):
        sl = slice(h * hd, (h + 1) * hd)
        sq_o[:, sl] = (f32(sq_ref, sl) * QSCALE).astype(BF16)
        svT_o[sl, :] = f32(sv_ref, sl).T.astype(BF16)
    sk_o[...] = sk_ref[...]
    for h in range(MEM_HEADS):
        sl = slice(h * hd, (h + 1) * hd)
        mq_o[:, sl] = (_rms(f32(mq_ref, sl), gmq_ref[...]) * QSCALE).astype(BF16)
    gT_o[...] = jax.nn.sigmoid(gn_ref[...]).T


def _prep(P, Pg, c0, pos_col, inv2, sgn, gq, gks, gkw, gmq):
    T = P.shape[0]
    tp = _tile(T, 512)
    hd = HEAD_DIM
    row = lambda w, c: pl.BlockSpec((tp, w), lambda i, c=c: (i, c))
    const = lambda: pl.BlockSpec((1, hd), lambda i: (0, 0))
    assert c0 % (8 * hd) == 0
    at = lambda w, col: row(w, (c0 + col) // w)
    in_specs = [pl.BlockSpec((tp, 1), lambda i: (i, 0)), const(), const(), const(), const(), const(), const(),
                at(8 * hd, 0),
                at(2 * hd, 8 * hd), at(2 * hd, 10 * hd),
                at(2 * hd, 12 * hd), at(2 * hd, 14 * hd),
                at(2 * hd, 16 * hd), at(2 * hd, 18 * hd),
                at(4 * hd, 20 * hd), at(4 * hd, 24 * hd), at(4 * hd, 28 * hd),
                at(4 * hd, 32 * hd),
                row(hd, 0)]
    colT = lambda w: pl.BlockSpec((w, tp), lambda i: (0, i))
    grp = lambda w: pl.BlockSpec((NSA_GROUPS, tp, w), lambda i: (0, i, 0))
    out_specs = [row(8 * hd, 0), grp(hd), grp(hd), row(2 * hd, 0),
                 row(2 * hd, 0), colT(NSA_GROUPS * VAUG), colT(NSA_GROUPS * VAUG),
                 row(4 * hd, 0), row(4 * hd, 0), colT(4 * hd), row(4 * hd, 0), colT(hd),
                 colT(NSA_HEADS * 8), colT(2 * NSA_GROUPS * 8)]
    sds = jax.ShapeDtypeStruct
    out_shape = [sds((T, 8 * hd), BF16),
                 sds((NSA_GROUPS, T, hd), BF16), sds((NSA_GROUPS, T, hd), BF16), sds((T, 2 * hd), BF16),
                 sds((T, 2 * hd), BF16),
                 sds((NSA_GROUPS * VAUG, T), BF16), sds((NSA_GROUPS * VAUG, T), BF16),
                 sds((T, 4 * hd), BF16), sds((T, 4 * hd), BF16), sds((4 * hd, T), BF16), sds((T, 4 * hd), BF16),
                 sds((hd, T), F32), sds((NSA_HEADS * 8, T), F32), sds((2 * NSA_GROUPS * 8, T), F32)]
    return pl.pallas_call(
        _prep_kernel, grid=(T // tp,), in_specs=in_specs, out_specs=out_specs, out_shape=out_shape,
        compiler_params=_params(("parallel",), 48), name="prep",
    )(pos_col, inv2, sgn, gq, gks, gkw, gmq, *([P] * 11), Pg)


def _gelu_tanh(x):
    return 0.5 * x * (1.0 + jnp.tanh(0.7978845608028654 * (x + 0.044715 * (x * x * x))))


def _compress_one(x, w1, pe, w2):
    n = x.shape[0]
    ab = jnp.dot(x, w1, preferred_element_type=F32)
    pr = jnp.dot(pe, w1, preferred_element_type=F32)
    pec = pr[0:1, :CMP_HIDDEN] + pr[1:2, CMP_HIDDEN:]
    hid = ab[:, :CMP_HIDDEN] + pltpu.roll(ab[:, CMP_HIDDEN:], n - 1, 0) + pec
    return jnp.dot(_gelu_tanh(hid).astype(BF16), w2, preferred_element_type=F32)


def _compress_kernel(xk_ref, xv_ref, w1k_ref, w1v_ref, pek_ref, pev_ref, w2k_ref, w2v_ref,
                     gk_ref, pos_ref, inv_ref, sgn_ref, kc_o, vcT_o):
    c, s = _rope_tables(pos_ref[...], inv_ref[...], sgn_ref[...])
    k = _compress_one(xk_ref[0], w1k_ref[...], pek_ref[...], w2k_ref[...])
    kc_o[0] = _rope(_rms(k, gk_ref[...]), c, s).astype(BF16)
    v = _compress_one(xv_ref[0], w1v_ref[...], pev_ref[...], w2v_ref[...])
    vcT_o[0] = v.T.astype(BF16)


def _compress(xk, xv, w1k, w1v, pek, pev, w2k, w2v, gk, posc, inv2, sgn):
    G, n, W = xk.shape
    full = lambda a: pl.BlockSpec(a.shape, lambda g: (0,) * a.ndim)
    grp = pl.BlockSpec((1, n, W), lambda g: (g, 0, 0))
    return pl.pallas_call(
        _compress_kernel, grid=(G,),
        in_specs=[grp, grp, full(w1k), full(w1v), full(pek), full(pev), full(w2k), full(w2v),
                  full(gk), full(posc), full(inv2), full(sgn)],
        out_specs=[pl.BlockSpec((1, n, HEAD_DIM), lambda g: (g, 0, 0)),
                   pl.BlockSpec((1, HEAD_DIM, n), lambda g: (g, 0, 0))],
        out_shape=[jax.ShapeDtypeStruct((G, n, HEAD_DIM), BF16), jax.ShapeDtypeStruct((G, HEAD_DIM, n), BF16)],
        compiler_params=_params(("parallel",), 48), name="compress",
    )(xk, xv, w1k, w1v, pek, pev, w2k, w2v, gk, posc, inv2, sgn)


def _memkv_kernel(mem_ref, gm_ref, w_ref, gk_ref, k_o, vT_o):
    hn = _rms(mem_ref[...], gm_ref[...]).astype(BF16)
    kv = jnp.dot(hn, w_ref[...], preferred_element_type=F32)
    mw = MEM_HEADS * HEAD_DIM
    for h in range(MEM_HEADS):
        sl = slice(h * HEAD_DIM, (h + 1) * HEAD_DIM)
        k_o[:, sl] = _rms(kv[:, sl], gk_ref[...]).astype(BF16)
        vT_o[sl, :] = kv[:, mw + h * HEAD_DIM: mw + (h + 1) * HEAD_DIM].T.astype(BF16)


def _memkv(mem, gm, w, gk):
    M = mem.shape[0]
    mw = MEM_HEADS * HEAD_DIM
    return pl.pallas_call(
        _memkv_kernel,
        out_shape=[jax.ShapeDtypeStruct((M, mw), BF16), jax.ShapeDtypeStruct((mw, M), BF16)],
        compiler_params=pltpu.CompilerParams(vmem_limit_bytes=32 * MIB), name="mem_kv",
    )(mem, gm, w, gk)


def _cmp_kernel(q_ref, kc_ref, vcT_ref, ovT_ref, g_ref, *rest, tq, i0, n_pad, n_sel, top_k):
    y_ref, b_ref = rest[-2:]
    i = i0 + pl.program_id(1)
    t = i * tq + lax.broadcasted_iota(jnp.int32, (1, tq), 1)
    n_end = lax.broadcasted_iota(jnp.int32, (n_pad, 1), 0) * CMP_STRIDE + (CMP_LEN - 1)
    valid = n_end <= t
    kc = kc_ref[0]
    vcT = vcT_ref[0]
    has_valid = (t >= CMP_LEN - 1).astype(F32)
    sms = []
    for h in range(NSA_HPG):
        q_h = q_ref[:, h * HEAD_DIM:(h + 1) * HEAD_DIM]
        sms.append(jnp.where(valid, lax.dot_general(kc, q_h, _NT, preferred_element_type=F32), NEG_BIG))
    psum = jnp.zeros((n_pad, tq), F32)
    for h in range(NSA_HPG):
        e = jnp.exp2(sms[h] - jnp.max(sms[h], axis=0, keepdims=True))
        p = e * (has_valid / jnp.sum(e, axis=0, keepdims=True))
        oT = jnp.dot(vcT, p.astype(BF16), preferred_element_type=F32)
        y_ref[:, h * HEAD_DIM:(h + 1) * HEAD_DIM] = (oT * g_ref[0, 0, h:h + 1, :]).T.astype(BF16)
        psum = psum + p
    hi = psum.astype(BF16)
    r1 = psum - hi.astype(F32)
    mid = r1.astype(BF16)
    lo = (r1 - mid.astype(F32)).astype(BF16)
    parts = jnp.dot(ovT_ref[...], jnp.concatenate([hi, mid, lo], axis=1), preferred_element_type=F32)
    imp = parts[:, :tq] + parts[:, tq:2 * tq] + parts[:, 2 * tq:]
    s_i = lax.broadcasted_iota(jnp.int32, (n_sel, 1), 0)
    cur = lax.shift_right_logical(t, 6)
    forced = (s_i == 0) | (s_i == cur) | (s_i == cur - 1)
    future = s_i * SEL_BLOCK > t
    w = jnp.where(forced, jnp.inf, jnp.where(future, -jnp.inf, imp))
    s_f = jnp.broadcast_to(s_i.astype(F32), (n_sel, tq))
    for _ in range(top_k):
        m = jnp.max(w, axis=0, keepdims=True)
        idx = jnp.min(jnp.where(w == m, s_f, float(n_sel)), axis=0, keepdims=True)
        w = jnp.where(s_f == idx, -jnp.inf, w)
    b_ref[0, :n_sel, :] = jnp.where(future, NEG_BIG, jnp.where(w == -jnp.inf, 0.0, NEG_BIG))
    n_all = b_ref.shape[1]
    if n_sel < n_all:
        b_ref[0, n_sel:, :] = jnp.full((n_all - n_sel, tq), NEG_BIG, F32)


def _round_up(n, m):
    return -(-n // m) * m


def _cmp_select(qn, kc, vcT, ovT, gates):
    T = qn.shape[0]
    G, n_pad, _ = kc.shape
    n_sel = T // SEL_BLOCK
    tq = _tile(T, 512)
    nq = T // tq
    gw = NSA_HPG * HEAD_DIM
    nb = min(CMP_BUCKETS, nq)
    assert nq % nb == 0
    per = nq // nb
    out_shape = [jax.ShapeDtypeStruct((T, NSA_HEADS * HEAD_DIM), BF16), jax.ShapeDtypeStruct((G, n_sel, T), F32)]
    buffers = ()
    for b in range(nb):
        i0 = b * per
        t_max = (i0 + per) * tq - 1
        n_len = min(n_pad, _round_up(max(t_max - (CMP_LEN - 1), 0) // CMP_STRIDE + 1, LANE))
        s_len = min(n_sel, _round_up(t_max // SEL_BLOCK + 1, 8))
        kern = functools.partial(_cmp_kernel, tq=tq, i0=i0, n_pad=n_len, n_sel=s_len, top_k=min(SEL_TOPK, n_sel))
        buffers = pl.pallas_call(
            kern, grid=(G, per),
            in_specs=[pl.BlockSpec((tq, gw), lambda g, i, i0=i0: (i0 + i, g)),
                      pl.BlockSpec((1, n_len, HEAD_DIM), lambda g, i: (g, 0, 0)),
                      pl.BlockSpec((1, HEAD_DIM, n_len), lambda g, i: (g, 0, 0)),
                      pl.BlockSpec((s_len, n_len), lambda g, i: (0, 0)),
                      pl.BlockSpec((1, 1, 8, tq), lambda g, i, i0=i0: (0, g, 0, i0 + i))]
            + [pl.BlockSpec(memory_space=pl.ANY)] * len(buffers),
            out_specs=[pl.BlockSpec((tq, gw), lambda g, i, i0=i0: (i0 + i, g)),
                       pl.BlockSpec((1, n_sel, tq), lambda g, i, i0=i0: (g, 0, i0 + i))],
            out_shape=out_shape,
            input_output_aliases={5: 0, 6: 1} if buffers else {},
            compiler_params=_params(("parallel", "parallel"), 48), name="cmp_select_%d" % b,
        )(qn, kc, vcT, ovT, gates, *buffers)
    return buffers


def _flash_kernel(qi_ref, kj_ref, fl_ref, q_ref, k_ref, vT_ref, g_ref, sh_ref, *rest, mode, bounded, tq, tk):
    if mode == "sel":
        bias_ref, o_ref, m_sc, acc_sc = rest
    else:
        o_ref, m_sc, acc_sc = rest
    p = pl.program_id(1)
    i = qi_ref[p]
    j = kj_ref[p]
    fl = fl_ref[p]

    @pl.when((fl & 1) != 0)
    def _():
        m_sc[...] = jnp.full(m_sc.shape, NEG_BIG, F32)
        acc_sc[...] = jnp.zeros(acc_sc.shape, F32)

    shift = sh_ref[0, 0:1, :] if bounded else 0.0
    nb = tk // SEL_BLOCK
    if mode == "sel":
        rows = bias_ref[0] - shift

    def key_mask():
        t = i * tq + lax.broadcasted_iota(jnp.int32, (1, tq), 1)
        kpos = j * tk + lax.broadcasted_iota(jnp.int32, (tk, 1), 0)
        if mode == "sel":
            base = jnp.concatenate(
                [jnp.broadcast_to(rows[b:b + 1, :], (SEL_BLOCK, tq)) for b in range(nb)], axis=0)
            return jnp.where(kpos <= t, base, NEG_BIG)
        return jnp.where((kpos <= t) & (kpos > t - WINDOW), 0.0 - shift, NEG_BIG)

    k = k_ref[...]
    vT = vT_ref[...]
    raw = lambda h: lax.dot_general(k, q_ref[:, h * HEAD_DIM:(h + 1) * HEAD_DIM], _NT, preferred_element_type=F32)

    def accumulate(pTs):
        for h in range(NSA_HPG):
            acc_sc[h] = acc_sc[h] + jnp.dot(vT, pTs[h], preferred_element_type=F32)

    if bounded and mode == "sel":
        @pl.when((fl & 4) != 0)
        def _():
            mask_add = key_mask()
            accumulate([jnp.exp2(raw(h) + mask_add).astype(BF16) for h in range(NSA_HPG)])

        @pl.when((fl & 4) == 0)
        def _():
            def probs(h):
                s = raw(h).reshape(nb, SEL_BLOCK, tq) + rows[:, None, :]
                return jnp.exp2(s).astype(BF16).reshape(tk, tq)
            accumulate([probs(h) for h in range(NSA_HPG)])
    elif bounded:
        mask_add = key_mask()
        accumulate([jnp.exp2(raw(h) + mask_add).astype(BF16) for h in range(NSA_HPG)])
    else:
        mask_add = key_mask()
        sTs = [raw(h) + mask_add for h in range(NSA_HPG)]
        pTs, alphas = [], []
        for h in range(NSA_HPG):
            sT = sTs[h]
            m_old = m_sc[h]
            m_new = jnp.maximum(m_old, jnp.max(sT, axis=0, keepdims=True))
            alphas.append(jnp.exp2(m_old - m_new))
            pTs.append(jnp.exp2((sT - m_new).astype(BF16)))
            m_sc[h] = m_new
        for h in range(NSA_HPG):
            acc_sc[h] = alphas[h] * acc_sc[h] + jnp.dot(vT, pTs[h], preferred_element_type=F32)

    @pl.when((fl & 2) != 0)
    def _():
        for h in range(NSA_HPG):
            acc = acc_sc[h]
            l = acc[HEAD_DIM:HEAD_DIM + 1, :]
            o = acc[:HEAD_DIM, :] * ((1.0 / l) * g_ref[0, 0, h:h + 1, :])
            o_ref[:, h * HEAD_DIM:(h + 1) * HEAD_DIM] = o.T.astype(BF16)


def _steps(nq, lo_fn, hi_fn, reverse=False):
    qi, kj, fl = [], [], []
    for i in range(nq):
        js = list(range(lo_fn(i), hi_fn(i) + 1))
        if reverse:
            js = js[::-1]
        for n, j in enumerate(js):
            qi.append(i)
            kj.append(j)
            fl.append((1 if n == 0 else 0) | (2 if n == len(js) - 1 else 0))
    return (jnp.asarray(np.array(qi, np.int32)), jnp.asarray(np.array(kj, np.int32)),
            jnp.asarray(np.array(fl, np.int32)))


def _nsa_flash(mode, qn, k, vT, gates, shift, bias=None):
    use_bounded = jnp.max(shift) <= SOFTMAX_BOUND_MAX
    return lax.cond(use_bounded,
                    lambda: _nsa_flash_call(mode, True, qn, k, vT, gates, shift, bias),
                    lambda: _nsa_flash_call(mode, False, qn, k, vT, gates, shift, bias))


def _nsa_flash_call(mode, bounded, qn, k, vT, gates, shift, bias):
    T = qn.shape[0]
    tq = _tile(T, 1024)
    tk = _tile(T, 1024 if mode == "sel" else 512)
    nq = T // tq
    gw = NSA_HPG * HEAD_DIM
    hi = lambda i: ((i + 1) * tq - 1) // tk
    if mode == "sel":
        lo = lambda i: 0
        br = 1
    else:
        lo = lambda i: max(0, (i * tq - (WINDOW - 1)) // tk)
        br = 2
    qi, kj, fl = _steps(nq, lo, hi)
    fl = fl | 4 * ((kj + 1) * tk - 1 > qi * tq).astype(jnp.int32)
    in_specs = [pl.BlockSpec((tq, gw), lambda g, p, qi, kj, fl: (qi[p], g)),
                pl.BlockSpec((tk, HEAD_DIM), lambda g, p, qi, kj, fl: (kj[p], g)),
                pl.BlockSpec((VAUG, tk), lambda g, p, qi, kj, fl: (g, kj[p])),
                pl.BlockSpec((1, 1, 8, tq), lambda g, p, qi, kj, fl, br=br: (br, g, 0, qi[p])),
                pl.BlockSpec((1, 8, tq), lambda g, p, qi, kj, fl: (g, 0, qi[p]))]
    args = [qn, k, vT, gates, shift]
    if mode == "sel":
        in_specs.append(pl.BlockSpec((1, tk // SEL_BLOCK, tq), lambda g, p, qi, kj, fl: (g, kj[p], qi[p])))
        args.append(bias)
    kern = functools.partial(_flash_kernel, mode=mode, bounded=bounded, tq=tq, tk=tk)
    return pl.pallas_call(
        kern,
        grid_spec=pltpu.PrefetchScalarGridSpec(
            num_scalar_prefetch=3, grid=(NSA_GROUPS, int(qi.shape[0])),
            in_specs=in_specs,
            out_specs=pl.BlockSpec((tq, gw), lambda g, p, qi, kj, fl: (qi[p], g)),
            scratch_shapes=[pltpu.VMEM((NSA_HPG, 1, tq), F32), pltpu.VMEM((NSA_HPG, VAUG, tq), F32)]),
        out_shape=jax.ShapeDtypeStruct((T, NSA_HEADS * HEAD_DIM), BF16),
        compiler_params=_params(("parallel", "arbitrary"), 48),
        name="nsa_" + mode + ("_bounded" if bounded else "_online"),
    )(qi, kj, fl, *args)


def _sb_kernel(qi_ref, kj_ref, fl_ref, q_ref, k_ref, vT_ref, lm_ref, *rest, tq, tk, resume):
    if resume:
        acc_in, carry_in, o_ref, carry_sc, acc_sc = rest
    else:
        o_ref, acc_o, carry_o, carry_sc, acc_sc = rest
    p = pl.program_id(0)
    i = qi_ref[p]
    j = kj_ref[p]
    fl = fl_ref[p]

    @pl.when((fl & 1) != 0)
    def _():
        if resume:
            for h in range(SB_HEADS):
                carry_sc[h] = carry_in[h:h + 1, :]
                acc_sc[h] = acc_in[h * HEAD_DIM:(h + 1) * HEAD_DIM, :]
        else:
            carry_sc[...] = jnp.zeros(carry_sc.shape, F32)
            acc_sc[...] = jnp.zeros(acc_sc.shape, F32)

    @pl.when((jnp.max(carry_sc[...]) > SB_DEAD_LOG2) & ((fl & 4) == 0))
    def _():
        t = i * tq + lax.broadcasted_iota(jnp.int32, (1, tq), 1)
        kpos = j * tk + lax.broadcasted_iota(jnp.int32, (tk, 1), 0)
        past = kpos < t
        lm = lm_ref[...]
        hs = [slice(h * HEAD_DIM, (h + 1) * HEAD_DIM) for h in range(SB_HEADS)]
        zs = [lax.dot_general(k_ref[:, sl], q_ref[:, sl], _NT, preferred_element_type=F32) for sl in hs]
        lgs, css = [], []
        for h in range(SB_HEADS):
            z = zs[h]
            sp = jnp.maximum(z, 0.0) + jnp.log2(1.0 + jnp.exp2(-jnp.abs(z)))
            lk = jnp.where(past, -sp, 0.0)
            hi = lk.astype(BF16)
            lo = (lk - hi.astype(F32)).astype(BF16)
            css.append(jnp.dot(lm, jnp.concatenate([hi, lo], axis=1), preferred_element_type=F32))
            lgs.append(z - sp)
            carry_old = carry_sc[h]
            carry_sc[h] = carry_old + jnp.sum(lk, axis=0, keepdims=True)
            css[h] = css[h][:, :tq] + css[h][:, tq:] + carry_old
        for h in range(SB_HEADS):
            wgt = jnp.where(past, jnp.exp2(lgs[h] + css[h]), 0.0)
            acc_sc[h] = acc_sc[h] + jnp.dot(vT_ref[hs[h], :], wgt.astype(BF16), preferred_element_type=F32)

    @pl.when((fl & 2) != 0)
    def _():
        for h in range(SB_HEADS):
            o_ref[:, h * HEAD_DIM:(h + 1) * HEAD_DIM] = acc_sc[h].T.astype(BF16)
            if not resume:
                acc_o[h * HEAD_DIM:(h + 1) * HEAD_DIM, :] = acc_sc[h]
                carry_o[h:h + 1, :] = carry_sc[h]
        if not resume:
            carry_o[SB_HEADS:, :] = jnp.zeros((8 - SB_HEADS, tq), F32)


def _sb_call(q, k, vT, steps, tq, tk, state=None):
    T, W = q.shape
    qi, kj, fl = steps
    lmat = jnp.asarray(np.triu(np.ones((tk, tk), np.float32), 1), BF16)
    resume = state is not None
    qtile = lambda shape: pl.BlockSpec(shape, lambda p, qi, kj, fl: (qi[p], 0))
    qtileT = lambda rows: pl.BlockSpec((rows, tq), lambda p, qi, kj, fl: (0, qi[p]))
    in_specs = [qtile((tq, W)),
                pl.BlockSpec((tk, W), lambda p, qi, kj, fl: (kj[p], 0)),
                pl.BlockSpec((W, tk), lambda p, qi, kj, fl: (0, kj[p])),
                pl.BlockSpec((tk, tk), lambda p, qi, kj, fl: (0, 0))]
    y_sds = jax.ShapeDtypeStruct((T, W), BF16)
    if resume:
        in_specs += [qtileT(W), qtileT(8)]
        out_specs, out_shape = qtile((tq, W)), y_sds
    else:
        out_specs = [qtile((tq, W)), qtileT(W), qtileT(8)]
        out_shape = [y_sds, jax.ShapeDtypeStruct((W, T), F32), jax.ShapeDtypeStruct((8, T), F32)]
    return pl.pallas_call(
        functools.partial(_sb_kernel, tq=tq, tk=tk, resume=resume),
        grid_spec=pltpu.PrefetchScalarGridSpec(
            num_scalar_prefetch=3, grid=(int(qi.shape[0]),),
            in_specs=in_specs, out_specs=out_specs,
            scratch_shapes=[pltpu.VMEM((SB_HEADS, 1, tq), F32), pltpu.VMEM((SB_HEADS, HEAD_DIM, tq), F32)]),
        out_shape=out_shape,
        compiler_params=_params(("arbitrary",), 48), name="sb_far" if resume else "sb_near",
    )(qi, kj, fl, q, k, vT, lmat, *(state or ()))


def _stick_breaking(q, k, vT):
    T, W = q.shape
    tq = _tile(T, 512)
    tk = _tile(T, 256)
    nq = T // tq
    hi = lambda i: ((i + 1) * tq - 2) // tk
    lo_near = lambda i: max(0, hi(i) - SB_NEAR_TILES + 1)
    near = _steps(nq, lo_near, hi, reverse=True)
    y_near, acc, carry = _sb_call(q, k, vT, near, tq, tk)
    qi, kj, fl = [], [], []
    for i in range(nq):
        js = list(range(lo_near(i) - 1, -1, -1))
        for n, j in enumerate(js or [0]):
            qi.append(i)
            kj.append(j)
            fl.append((1 if n == 0 else 0) | (2 if n == max(len(js), 1) - 1 else 0) | (0 if js else 4))
    far = tuple(jnp.asarray(np.array(a, np.int32)) for a in (qi, kj, fl))
    with_far = [i for i in range(nq) if lo_near(i) > 0]
    if not with_far:
        return y_near
    alive = jnp.max(carry[:SB_HEADS, with_far[0] * tq:]) > SB_DEAD_LOG2
    return lax.cond(alive, lambda: _sb_call(q, k, vT, far, tq, tk, state=(acc, carry)), lambda: y_near)


def _memattn_kernel(q_ref, k_ref, vT_ref, o_ref):
    for h in range(MEM_HEADS):
        sl = slice(h * HEAD_DIM, (h + 1) * HEAD_DIM)
        sT = lax.dot_general(k_ref[:, sl], q_ref[:, sl], _NT, preferred_element_type=F32)
        e = jnp.exp2(sT - jnp.max(sT, axis=0, keepdims=True))
        l = jnp.sum(e, axis=0, keepdims=True)
        oT = jnp.dot(vT_ref[sl, :], e.astype(BF16), preferred_element_type=F32) * (1.0 / l)
        o_ref[:, sl] = oT.T.astype(BF16)


def _mem_attention(q, k, vT):
    T, W = q.shape
    M = k.shape[0]
    tq = _tile(T, 512)
    return pl.pallas_call(
        _memattn_kernel, grid=(T // tq,),
        in_specs=[pl.BlockSpec((tq, W), lambda i: (i, 0)),
                  pl.BlockSpec((M, W), lambda i: (0, 0)),
                  pl.BlockSpec((W, M), lambda i: (0, 0))],
        out_specs=pl.BlockSpec((tq, W), lambda i: (i, 0)),
        out_shape=jax.ShapeDtypeStruct((T, W), BF16),
        compiler_params=_params(("parallel",), 32), name="mem_attention",
    )(q, k, vT)


def _mixout_kernel(yc_ref, ys_ref, yw_ref, ysb_ref, ym_ref, g_ref, wn_ref, wsb_ref, wm_ref, wo_ref, x_ref, gain_ref,
                   x2_o, h2_o):
    D = x_ref.shape[1]
    yn = (yc_ref[...].astype(F32) + ys_ref[...].astype(F32) + yw_ref[...].astype(F32)).astype(BF16)
    a = jnp.dot(yn, wn_ref[...], preferred_element_type=F32)
    b = jnp.dot(ysb_ref[...], wsb_ref[...], preferred_element_type=F32)
    c = jnp.dot(ym_ref[...], wm_ref[...], preferred_element_type=F32)
    sig = lambda n: jax.nn.sigmoid(g_ref[:, n * D:(n + 1) * D].astype(F32))
    mixed = (sig(0) * a + sig(1) * b + sig(2) * c).astype(BF16)
    x2 = x_ref[...] + jnp.dot(mixed, wo_ref[...], preferred_element_type=F32)
    x2_o[...] = x2
    h2_o[...] = _rms(x2, gain_ref[...]).astype(BF16)


def _mix_out(yc, ys, yw, ysb, ym, P, wn, wsb, wm, wo, x, gain):
    T, D = x.shape
    tm = _tile(T, 256)
    rowi = lambda w: pl.BlockSpec((tm, w), lambda i: (i, 0))
    res = lambda a: pl.BlockSpec(a.shape, lambda i: (0, 0), pipeline_mode=pl.Buffered(1))
    return pl.pallas_call(
        _mixout_kernel, grid=(T // tm,),
        in_specs=[rowi(yc.shape[1]), rowi(ys.shape[1]), rowi(yw.shape[1]), rowi(ysb.shape[1]), rowi(ym.shape[1]),
                  rowi(N_BRANCH * D), res(wn), res(wsb), res(wm), res(wo), rowi(D), res(gain)],
        out_specs=[rowi(D), rowi(D)],
        out_shape=[jax.ShapeDtypeStruct((T, D), F32), jax.ShapeDtypeStruct((T, D), BF16)],
        compiler_params=_params(("parallel",), 48), name="mix_out",
    )(yc, ys, yw, ysb, ym, P, wn, wsb, wm, wo, x, gain)


def _ffn_kernel(h_ref, wg_ref, wu_ref, wd_ref, x_ref, o_ref, z_sc, *, nf):
    f = pl.program_id(1)

    def up():
        a = jnp.dot(h_ref[...], wg_ref[...], preferred_element_type=F32)
        b = jnp.dot(h_ref[...], wu_ref[...], preferred_element_type=F32)
        return (a * jax.nn.sigmoid(a) * b).astype(BF16)

    def down():
        return jnp.dot(z_sc[...], wd_ref[...], preferred_element_type=F32)

    @pl.when(f == 0)
    def _():
        z_sc[...] = up()

    @pl.when((f > 0) & (f < nf))
    def _():
        c = down()
        z_new = up()
        o_ref[...] = jnp.where(f == 1, x_ref[...], o_ref[...]) + c
        z_sc[...] = z_new

    @pl.when(f == nf)
    def _():
        o_ref[...] += down()


def _ffn(h2, wg, wu, wd, x2):
    T, D = x2.shape
    F = wg.shape[1]
    tm = _tile(T, 1024)
    tf = 512
    assert F % tf == 0
    nf = F // tf
    assert nf >= 2
    return pl.pallas_call(
        functools.partial(_ffn_kernel, nf=nf), grid=(T // tm, nf + 1),
        in_specs=[pl.BlockSpec((tm, D), lambda i, f: (i, 0)),
                  pl.BlockSpec((D, tf), lambda i, f: (0, jnp.minimum(f, nf - 1))),
                  pl.BlockSpec((D, tf), lambda i, f: (0, jnp.minimum(f, nf - 1))),
                  pl.BlockSpec((tf, D), lambda i, f: (jnp.maximum(f - 1, 0), 0)),
                  pl.BlockSpec((tm, D), lambda i, f: (i, 0), pipeline_mode=pl.Buffered(1))],
        out_specs=pl.BlockSpec((tm, D), lambda i, f: (i, 0)),
        out_shape=jax.ShapeDtypeStruct((T, D), F32),
        scratch_shapes=[pltpu.VMEM((tm, tf), BF16)],
        compiler_params=_params(("parallel", "arbitrary"), 56), name="ffn",
    )(h2, wg.astype(BF16), wu.astype(BF16), wd.astype(BF16), x2)


def _layer(x, mem, pos_col, posc_col, consts, layer, attn_norm, w_in_all, nsa_q_norm, nsa_kc_norm, nsa_ks_norm, nsa_kw_norm,
           cmp_k_pe, cmp_k_w1, cmp_k_w2, cmp_v_pe, cmp_v_w1, cmp_v_w2, mem_norm, w_mem_kv,
           mem_q_norm, mem_k_norm, w_o_nsa, w_o_sb, w_o_mem, w_out, ffn_norm,
           w_ffn_gate, w_ffn_up, w_ffn_down):
    T, D = x.shape
    inv2, sgn, ovT = consts
    hd = HEAD_DIM
    row = lambda g: g.reshape(1, -1)

    q_w, kv_w, gn_w = NSA_HEADS * hd, 6 * NSA_GROUPS * hd, 3 * NSA_HEADS
    sb_w, mq_w, gm_w = 3 * SB_HEADS * hd, MEM_HEADS * hd, N_BRANCH * D
    o_gn = q_w + kv_w
    o_sb = o_gn + gn_w
    o_mq = o_sb + sb_w
    o_gm = o_mq + mq_w
    assert w_in_all.shape[2] == o_gm + gm_w
    segs = ((0, o_gm, gm_w), (gm_w, 0, o_gn), (gm_w + o_gn, o_sb, o_gm - o_sb))
    w_main, w_gate = _regroup_w_in(w_in_all, layer, segs, (o_gn, gn_w), gm_w + o_gn + o_gm - o_sb)
    P, Pg = _in_proj(x, row(attn_norm), w_main, w_gate)

    (qn, kc_raw, vc_raw, ksn, kwn, vsT, vwT, sbq, sbk, sbvT, memq, gT, qn2, kn2) = _prep(
        P, Pg, gm_w, pos_col, inv2, sgn, row(nsa_q_norm), row(nsa_ks_norm), row(nsa_kw_norm), row(mem_q_norm))

    q2 = jnp.max(qn2[::8].reshape(NSA_GROUPS, NSA_HPG, T), axis=1)
    k2 = jnp.max(kn2[::8], axis=1).reshape(2, NSA_GROUPS)
    bound = lambda kk: jnp.broadcast_to((1.02 * jnp.sqrt(q2 * kk[:, None]))[:, None, :], (NSA_GROUPS, 8, T))
    shift_sel, shift_win = bound(k2[0]), bound(k2[1])

    gates = gT[:gn_w].reshape(NSA_GROUPS, NSA_HPG, 3, T).transpose(2, 0, 1, 3)
    gates = jnp.pad(gates, ((0, 0), (0, 0), (0, 8 - NSA_HPG), (0, 0)))

    n_pad = T // CMP_STRIDE
    half = CMP_LEN // 2

    def w1_pack(w1):
        return jnp.concatenate([w1[:half].reshape(half * hd, -1), w1[half:].reshape(half * hd, -1)], axis=1).astype(BF16)

    def pe_pack(pe):
        return jnp.pad(pe.reshape(2, half * hd), ((0, 6), (0, 0))).astype(BF16)

    kc, vcT = _compress(
        kc_raw.reshape(NSA_GROUPS, n_pad, CMP_STRIDE * hd), vc_raw.reshape(NSA_GROUPS, n_pad, CMP_STRIDE * hd),
        w1_pack(cmp_k_w1), w1_pack(cmp_v_w1), pe_pack(cmp_k_pe), pe_pack(cmp_v_pe),
        cmp_k_w2.astype(BF16), cmp_v_w2.astype(BF16), row(nsa_kc_norm), posc_col, inv2, sgn)

    y_cmp, bias = _cmp_select(qn, kc, vcT, ovT, gates)
    y_sel = _nsa_flash("sel", qn, ksn, vsT, gates, shift_sel, bias)
    y_win = _nsa_flash("win", qn, kwn, vwT, gates, shift_win)
    y_sb = _stick_breaking(sbq, sbk, sbvT)

    mk, mvT = _memkv(mem, row(mem_norm), w_mem_kv.astype(BF16), row(mem_k_norm))
    y_mem = _mem_attention(memq, mk, mvT)

    x2, h2 = _mix_out(y_cmp, y_sel, y_win, y_sb, y_mem, P, w_o_nsa.astype(BF16), w_o_sb.astype(BF16),
                      w_o_mem.astype(BF16), w_out.astype(BF16), x, row(ffn_norm))
    return _ffn(h2, w_ffn_gate, w_ffn_up, w_ffn_down, x2)


def kernel(x, mem, positions, attn_norm, w_in, nsa_q_norm, nsa_kc_norm, nsa_ks_norm, nsa_kw_norm, cmp_k_pe, cmp_k_w1, cmp_k_w2, cmp_v_pe, cmp_v_w1, cmp_v_w2, mem_norm, w_mem_kv, mem_q_norm, mem_k_norm, w_o_nsa, w_o_sb, w_o_mem, w_out, ffn_norm, w_ffn_gate, w_ffn_up, w_ffn_down):
    B, T, D = x.shape
    assert T % (4 * LANE) == 0 and T // SEL_BLOCK >= 8
    n_pad = T // CMP_STRIDE
    n_sel = T // SEL_BLOCK
    inv = 1.0 / (ROPE_THETA ** (jnp.arange(0, HEAD_DIM, 2, dtype=F32) / HEAD_DIM))
    inv2 = jnp.concatenate([inv, inv]).reshape(1, HEAD_DIM)
    sgn = jnp.concatenate([-jnp.ones((HEAD_DIM // 2,), F32), jnp.ones((HEAD_DIM // 2,), F32)]).reshape(1, HEAD_DIM)
    cs = np.arange(n_pad)[None, :] * CMP_STRIDE
    ss = np.arange(n_sel)[:, None] * SEL_BLOCK
    ovT = jnp.asarray(((cs < ss + SEL_BLOCK) & (cs + CMP_LEN - 1 >= ss)).astype(np.float32), BF16)
    consts = (inv2, sgn, ovT)
    depth = w_in.shape[0]
    outs = []
    for b in range(B):
        xb = x[b]
        posf = positions[b].astype(F32)
        pos_col = posf.reshape(T, 1)
        posc = jnp.concatenate([posf[CMP_LEN - 1::CMP_STRIDE], posf[-1:]]).reshape(n_pad, 1)
        for l in range(depth):
            xb = _layer(xb, mem[b], pos_col, posc, consts, l, attn_norm[l], w_in, nsa_q_norm[l], nsa_kc_norm[l],
                        nsa_ks_norm[l], nsa_kw_norm[l], cmp_k_pe[l], cmp_k_w1[l], cmp_k_w2[l], cmp_v_pe[l],
                        cmp_v_w1[l], cmp_v_w2[l], mem_norm[l], w_mem_kv[l], mem_q_norm[l], mem_k_norm[l],
                        w_o_nsa[l], w_o_sb[l], w_o_mem[l], w_out[l], ffn_norm[l],
                        w_ffn_gate[l], w_ffn_up[l], w_ffn_down[l])
        outs.append(xb)
    return outs[0][None] if B == 1 else jnp.stack(outs, axis=0)
```

```python
import functools

import numpy as np
import jax
import jax.numpy as jnp
from jax import lax
from jax.experimental import pallas as pl
from jax.experimental.pallas import tpu as pltpu

HEAD_DIM = 128
NSA_HEADS = 8
NSA_GROUPS = 2
NSA_HPG = NSA_HEADS // NSA_GROUPS
SB_HEADS = 4
MEM_HEADS = 4
CMP_LEN = 32
CMP_STRIDE = 16
CMP_HIDDEN = 2 * HEAD_DIM
SEL_BLOCK = 64
SEL_TOPK = 16
WINDOW = 512
ROPE_THETA = 10000.0
NORM_EPS = 1e-6
NEG_BIG = -1e30
N_BRANCH = 3
SCALE = HEAD_DIM ** -0.5
LOG2E = 1.4426950408889634
QSCALE = SCALE * LOG2E
SOFTMAX_BOUND_MAX = 50.0
CMP_BUCKETS = 8
SB_NEAR_TILES = 3
SB_DEAD_LOG2 = -160.0
VAUG = HEAD_DIM + 16

LANE = 128
MIB = 1 << 20
BF16 = jnp.bfloat16
F32 = jnp.float32

_NT = (((1,), (1,)), ((), ()))


def _tile(n, pref):
    t = min(n, pref)
    assert n % t == 0, (n, pref)
    return t


def _params(sem, vmem_mib):
    return pltpu.CompilerParams(dimension_semantics=sem, vmem_limit_bytes=vmem_mib * MIB)


def _rms(x, gain):
    return x * lax.rsqrt(jnp.mean(x * x, axis=-1, keepdims=True) + NORM_EPS) * gain


def _rope_tables(pos, inv2, sgn):
    ang = pos * inv2
    return jnp.cos(ang), jnp.sin(ang) * sgn


def _rope(x, c, s):
    return x * c + pltpu.roll(x, HEAD_DIM // 2, 1) * s


def _regroup_kernel(w_ref, o_ref, og_ref, *, segs, gate):
    for dst, src, width in segs:
        o_ref[:, dst:dst + width] = w_ref[0, :, src:src + width].astype(BF16)
    g0, gw = gate
    lane = lax.broadcasted_iota(jnp.int32, (w_ref.shape[1], LANE), 1)
    og_ref[...] = jnp.where(lane < gw, w_ref[0, :, g0:g0 + LANE], 0.0).astype(BF16)


def _regroup_w_in(w_in_all, layer, segs, gate, n_out):
    _, D, N = w_in_all.shape
    tr = _tile(D, 64)
    return pl.pallas_call(
        functools.partial(_regroup_kernel, segs=segs, gate=gate), grid=(D // tr,),
        in_specs=[pl.BlockSpec((1, tr, N), lambda i: (layer, i, 0))],
        out_specs=[pl.BlockSpec((tr, n_out), lambda i: (i, 0)), pl.BlockSpec((tr, LANE), lambda i: (i, 0))],
        out_shape=[jax.ShapeDtypeStruct((D, n_out), BF16), jax.ShapeDtypeStruct((D, LANE), BF16)],
        compiler_params=_params(("parallel",), 32), name="regroup_w_in",
    )(w_in_all)


def _proj_kernel(x_ref, g_ref, w_ref, wg_ref, o_ref, og_ref, hn_ref):
    @pl.when(pl.program_id(1) == 0)
    def _():
        hn_ref[...] = _rms(x_ref[...], g_ref[...]).astype(BF16)
        og_ref[...] = jnp.dot(hn_ref[...], wg_ref[...], preferred_element_type=F32)

    o_ref[...] = jnp.dot(hn_ref[...], w_ref[...], preferred_element_type=F32).astype(BF16)


def _in_proj(x, gain, w_main, w_gate):
    T, D = x.shape
    N = w_main.shape[1]
    tm = _tile(T, 1024)
    tn = 1536
    assert N % tn == 0
    return pl.pallas_call(
        _proj_kernel,
        grid=(T // tm, N // tn),
        in_specs=[
            pl.BlockSpec((tm, D), lambda i, j: (i, 0)),
            pl.BlockSpec((1, D), lambda i, j: (0, 0)),
            pl.BlockSpec((D, tn), lambda i, j: (0, j)),
            pl.BlockSpec((D, LANE), lambda i, j: (0, 0)),
        ],
        out_specs=[pl.BlockSpec((tm, tn), lambda i, j: (i, j)), pl.BlockSpec((tm, LANE), lambda i, j: (i, 0))],
        out_shape=[jax.ShapeDtypeStruct((T, N), BF16), jax.ShapeDtypeStruct((T, LANE), F32)],
        scratch_shapes=[pltpu.VMEM((tm, D), BF16)],
        compiler_params=_params(("parallel", "arbitrary"), 52),
        name="in_proj",
    )(x, gain, w_main, w_gate)


def _prep_kernel(pos_ref, inv_ref, sgn_ref, gq_ref, gks_ref, gkw_ref, gmq_ref,
                 q_ref, kc_ref, vc_ref, ks_ref, vs_ref, kw_ref, vw_ref,
                 sq_ref, sk_ref, sv_ref, mq_ref, gn_ref,
                 qn_o, kc_o, vc_o, ks_o, kw_o, vsT_o, vwT_o, sq_o, sk_o, svT_o, mq_o, gT_o, qn2_o, kn2_o):
    c, s = _rope_tables(pos_ref[...], inv_ref[...], sgn_ref[...])
    hd = HEAD_DIM
    tp = pos_ref.shape[0]
    f32 = lambda ref, sl: ref[:, sl].astype(F32)
    ones8 = jnp.ones((8, hd), F32)

    def sqnorm_rows(xb):
        x = xb.astype(F32)
        return lax.dot_general(ones8, x * x, _NT, preferred_element_type=F32)

    for h in range(NSA_HEADS):
        sl = slice(h * hd, (h + 1) * hd)
        qb = (_rope(_rms(f32(q_ref, sl), gq_ref[...]), c, s) * QSCALE).astype(BF16)
        qn_o[:, sl] = qb
        qn2_o[h * 8:(h + 1) * 8, :] = sqnorm_rows(qb)
    ones_rows = (lax.broadcasted_iota(jnp.int32, (VAUG - hd, tp), 0) == 0).astype(F32).astype(BF16)
    for g in range(NSA_GROUPS):
        sl = slice(g * hd, (g + 1) * hd)
        kc_o[g] = kc_ref[:, sl]
        vc_o[g] = vc_ref[:, sl]
        ksb = _rope(_rms(f32(ks_ref, sl), gks_ref[...]), c, s).astype(BF16)
        kwb = _rope(_rms(f32(kw_ref, sl), gkw_ref[...]), c, s).astype(BF16)
        ks_o[:, sl] = ksb
        kw_o[:, sl] = kwb
        kn2_o[g * 8:(g + 1) * 8, :] = sqnorm_rows(ksb)
        kn2_o[(NSA_GROUPS + g) * 8:(NSA_GROUPS + g + 1) * 8, :] = sqnorm_rows(kwb)
        vsT_o[g * VAUG:g * VAUG + hd, :] = f32(vs_ref, sl).T.astype(BF16)
        vsT_o[g * VAUG + hd:(g + 1) * VAUG, :] = ones_rows
        vwT_o[g * VAUG:g * VAUG + hd, :] = f32(vw_ref, sl).T.astype(BF16)
        vwT_o[g * VAUG + hd:(g + 1) * VAUG, :] = ones_rows
    for h in range(SB_HEADS):
        sl = slice(h * hd, (h + 1) * hd)
        sq_o[:, sl] = (f32(sq_ref, sl) * QSCALE).astype(BF16)
        svT_o[sl, :] = f32(sv_ref, sl).T.astype(BF16)
    sk_o[...] = sk_ref[...]
    for h in range(MEM_HEADS):
        sl = slice(h * hd, (h + 1) * hd)
        mq_o[:, sl] = (_rms(f32(mq_ref, sl), gmq_ref[...]) * QSCALE).astype(BF16)
    gT_o[...] = jax.nn.sigmoid(gn_ref[...]).T


def _prep(P, Pg, c0, pos_col, inv2, sgn, gq, gks, gkw, gmq):
    T = P.shape[0]
    tp = _tile(T, 512)
    hd = HEAD_DIM
    row = lambda w, c: pl.BlockSpec((tp, w), lambda i, c=c: (i, c))
    const = lambda: pl.BlockSpec((1, hd), lambda i: (0, 0))
    assert c0 % (8 * hd) == 0
    at = lambda w, col: row(w, (c0 + col) // w)
    in_specs = [pl.BlockSpec((tp, 1), lambda i: (i, 0)), const(), const(), const(), const(), const(), const(),
                at(8 * hd, 0),
                at(2 * hd, 8 * hd), at(2 * hd, 10 * hd),
                at(2 * hd, 12 * hd), at(2 * hd, 14 * hd),
                at(2 * hd, 16 * hd), at(2 * hd, 18 * hd),
                at(4 * hd, 20 * hd), at(4 * hd, 24 * hd), at(4 * hd, 28 * hd),
                at(4 * hd, 32 * hd),
                row(hd, 0)]
    colT = lambda w: pl.BlockSpec((w, tp), lambda i: (0, i))
    grp = lambda w: pl.BlockSpec((NSA_GROUPS, tp, w), lambda i: (0, i, 0))
    out_specs = [row(8 * hd, 0), grp(hd), grp(hd), row(2 * hd, 0),
                 row(2 * hd, 0), colT(NSA_GROUPS * VAUG), colT(NSA_GROUPS * VAUG),
                 row(4 * hd, 0), row(4 * hd, 0), colT(4 * hd), row(4 * hd, 0), colT(hd),
                 colT(NSA_HEADS * 8), colT(2 * NSA_GROUPS * 8)]
    sds = jax.ShapeDtypeStruct
    out_shape = [sds((T, 8 * hd), BF16),
                 sds((NSA_GROUPS, T, hd), BF16), sds((NSA_GROUPS, T, hd), BF16), sds((T, 2 * hd), BF16),
                 sds((T, 2 * hd), BF16),
                 sds((NSA_GROUPS * VAUG, T), BF16), sds((NSA_GROUPS * VAUG, T), BF16),
                 sds((T, 4 * hd), BF16), sds((T, 4 * hd), BF16), sds((4 * hd, T), BF16), sds((T, 4 * hd), BF16),
                 sds((hd, T), F32), sds((NSA_HEADS * 8, T), F32), sds((2 * NSA_GROUPS * 8, T), F32)]
    return pl.pallas_call(
        _prep_kernel, grid=(T // tp,), in_specs=in_specs, out_specs=out_specs, out_shape=out_shape,
        compiler_params=_params(("parallel",), 48), name="prep",
    )(pos_col, inv2, sgn, gq, gks, gkw, gmq, *([P] * 11), Pg)


def _gelu_tanh(x):
    return 0.5 * x * (1.0 + jnp.tanh(0.7978845608028654 * (x + 0.044715 * (x * x * x))))


def _compress_one(x, w1, pe, w2):
    n = x.shape[0]
    ab = jnp.dot(x, w1, preferred_element_type=F32)
    pr = jnp.dot(pe, w1, preferred_element_type=F32)
    pec = pr[0:1, :CMP_HIDDEN] + pr[1:2, CMP_HIDDEN:]
    hid = ab[:, :CMP_HIDDEN] + pltpu.roll(ab[:, CMP_HIDDEN:], n - 1, 0) + pec
    return jnp.dot(_gelu_tanh(hid).astype(BF16), w2, preferred_element_type=F32)


def _compress_kernel(xk_ref, xv_ref, w1k_ref, w1v_ref, pek_ref, pev_ref, w2k_ref, w2v_ref,
                     gk_ref, pos_ref, inv_ref, sgn_ref, kc_o, vcT_o):
    c, s = _rope_tables(pos_ref[...], inv_ref[...], sgn_ref[...])
    k = _compress_one(xk_ref[0], w1k_ref[...], pek_ref[...], w2k_ref[...])
    kc_o[0] = _rope(_rms(k, gk_ref[...]), c, s).astype(BF16)
    v = _compress_one(xv_ref[0], w1v_ref[...], pev_ref[...], w2v_ref[...])
    vcT_o[0] = v.T.astype(BF16)


def _compress(xk, xv, w1k, w1v, pek, pev, w2k, w2v, gk, posc, inv2, sgn):
    G, n, W = xk.shape
    full = lambda a: pl.BlockSpec(a.shape, lambda g: (0,) * a.ndim)
    grp = pl.BlockSpec((1, n, W), lambda g: (g, 0, 0))
    return pl.pallas_call(
        _compress_kernel, grid=(G,),
        in_specs=[grp, grp, full(w1k), full(w1v), full(pek), full(pev), full(w2k), full(w2v),
                  full(gk), full(posc), full(inv2), full(sgn)],
        out_specs=[pl.BlockSpec((1, n, HEAD_DIM), lambda g: (g, 0, 0)),
                   pl.BlockSpec((1, HEAD_DIM, n), lambda g: (g, 0, 0))],
        out_shape=[jax.ShapeDtypeStruct((G, n, HEAD_DIM), BF16), jax.ShapeDtypeStruct((G, HEAD_DIM, n), BF16)],
        compiler_params=_params(("parallel",), 48), name="compress",
    )(xk, xv, w1k, w1v, pek, pev, w2k, w2v, gk, posc, inv2, sgn)


def _memkv_kernel(mem_ref, gm_ref, w_ref, gk_ref, k_o, vT_o):
    hn = _rms(mem_ref[...], gm_ref[...]).astype(BF16)
    kv = jnp.dot(hn, w_ref[...], preferred_element_type=F32)
    mw = MEM_HEADS * HEAD_DIM
    for h in range(MEM_HEADS):
        sl = slice(h * HEAD_DIM, (h + 1) * HEAD_DIM)
        k_o[:, sl] = _rms(kv[:, sl], gk_ref[...]).astype(BF16)
        vT_o[sl, :] = kv[:, mw + h * HEAD_DIM: mw + (h + 1) * HEAD_DIM].T.astype(BF16)


def _memkv(mem, gm, w, gk):
    M = mem.shape[0]
    mw = MEM_HEADS * HEAD_DIM
    return pl.pallas_call(
        _memkv_kernel,
        out_shape=[jax.ShapeDtypeStruct((M, mw), BF16), jax.ShapeDtypeStruct((mw, M), BF16)],
        compiler_params=pltpu.CompilerParams(vmem_limit_bytes=32 * MIB), name="mem_kv",
    )(mem, gm, w, gk)


def _cmp_kernel(q_ref, kc_ref, vcT_ref, ovT_ref, g_ref, *rest, tq, i0, n_pad, n_sel, top_k):
    y_ref, b_ref = rest[-2:]
    i = i0 + pl.program_id(1)
    t = i * tq + lax.broadcasted_iota(jnp.int32, (1, tq), 1)
    n_end = lax.broadcasted_iota(jnp.int32, (n_pad, 1), 0) * CMP_STRIDE + (CMP_LEN - 1)
    valid = n_end <= t
    kc = kc_ref[0]
    vcT = vcT_ref[0]
    has_valid = (t >= CMP_LEN - 1).astype(F32)
    sms = []
    for h in range(NSA_HPG):
        q_h = q_ref[:, h * HEAD_DIM:(h + 1) * HEAD_DIM]
        sms.append(jnp.where(valid, lax.dot_general(kc, q_h, _NT, preferred_element_type=F32), NEG_BIG))
    psum = jnp.zeros((n_pad, tq), F32)
    for h in range(NSA_HPG):
        e = jnp.exp2(sms[h] - jnp.max(sms[h], axis=0, keepdims=True))
        p = e * (has_valid / jnp.sum(e, axis=0, keepdims=True))
        oT = jnp.dot(vcT, p.astype(BF16), preferred_element_type=F32)
        y_ref[:, h * HEAD_DIM:(h + 1) * HEAD_DIM] = (oT * g_ref[0, 0, h:h + 1, :]).T.astype(BF16)
        psum = psum + p
    hi = psum.astype(BF16)
    r1 = psum - hi.astype(F32)
    mid = r1.astype(BF16)
    lo = (r1 - mid.astype(F32)).astype(BF16)
    parts = jnp.dot(ovT_ref[...], jnp.concatenate([hi, mid, lo], axis=1), preferred_element_type=F32)
    imp = parts[:, :tq] + parts[:, tq:2 * tq] + parts[:, 2 * tq:]
    s_i = lax.broadcasted_iota(jnp.int32, (n_sel, 1), 0)
    cur = lax.shift_right_logical(t, 6)
    forced = (s_i == 0) | (s_i == cur) | (s_i == cur - 1)
    future = s_i * SEL_BLOCK > t
    w = jnp.where(forced, jnp.inf, jnp.where(future, -jnp.inf, imp))
    s_f = jnp.broadcast_to(s_i.astype(F32), (n_sel, tq))
    for _ in range(top_k):
        m = jnp.max(w, axis=0, keepdims=True)
        idx = jnp.min(jnp.where(w == m, s_f, float(n_sel)), axis=0, keepdims=True)
        w = jnp.where(s_f == idx, -jnp.inf, w)
    b_ref[0, :n_sel, :] = jnp.where(future, NEG_BIG, jnp.where(w == -jnp.inf, 0.0, NEG_BIG))
    n_all = b_ref.shape[1]
    if n_sel < n_all:
        b_ref[0, n_sel:, :] = jnp.full((n_all - n_sel, tq), NEG_BIG, F32)


def _round_up(n, m):
    return -(-n // m) * m


def _cmp_select(qn, kc, vcT, ovT, gates):
    T = qn.shape[0]
    G, n_pad, _ = kc.shape
    n_sel = T // SEL_BLOCK
    tq = _tile(T, 512)
    nq = T // tq
    gw = NSA_HPG * HEAD_DIM
    nb = min(CMP_BUCKETS, nq)
    assert nq % nb == 0
    per = nq // nb
    out_shape = [jax.ShapeDtypeStruct((T, NSA_HEADS * HEAD_DIM), BF16), jax.ShapeDtypeStruct((G, n_sel, T), F32)]
    buffers = ()
    for b in range(nb):
        i0 = b * per
        t_max = (i0 + per) * tq - 1
        n_len = min(n_pad, _round_up(max(t_max - (CMP_LEN - 1), 0) // CMP_STRIDE + 1, LANE))
        s_len = min(n_sel, _round_up(t_max // SEL_BLOCK + 1, 8))
        kern = functools.partial(_cmp_kernel, tq=tq, i0=i0, n_pad=n_len, n_sel=s_len, top_k=min(SEL_TOPK, n_sel))
        buffers = pl.pallas_call(
            kern, grid=(G, per),
            in_specs=[pl.BlockSpec((tq, gw), lambda g, i, i0=i0: (i0 + i, g)),
                      pl.BlockSpec((1, n_len, HEAD_DIM), lambda g, i: (g, 0, 0)),
                      pl.BlockSpec((1, HEAD_DIM, n_len), lambda g, i: (g, 0, 0)),
                      pl.BlockSpec((s_len, n_len), lambda g, i: (0, 0)),
                      pl.BlockSpec((1, 1, 8, tq), lambda g, i, i0=i0: (0, g, 0, i0 + i))]
            + [pl.BlockSpec(memory_space=pl.ANY)] * len(buffers),
            out_specs=[pl.BlockSpec((tq, gw), lambda g, i, i0=i0: (i0 + i, g)),
                       pl.BlockSpec((1, n_sel, tq), lambda g, i, i0=i0: (g, 0, i0 + i))],
            out_shape=out_shape,
            input_output_aliases={5: 0, 6: 1} if buffers else {},
            compiler_params=_params(("parallel", "parallel"), 48), name="cmp_select_%d" % b,
        )(qn, kc, vcT, ovT, gates, *buffers)
    return buffers


def _flash_kernel(qi_ref, kj_ref, fl_ref, q_ref, k_ref, vT_ref, g_ref, sh_ref, *rest, mode, bounded, tq, tk):
    if mode == "sel":
        bias_ref, o_ref, m_sc, acc_sc = rest
    else:
        o_ref, m_sc, acc_sc = rest
    p = pl.program_id(1)
    i = qi_ref[p]
    j = kj_ref[p]
    fl = fl_ref[p]

    @pl.when((fl & 1) != 0)
    def _():
        m_sc[...] = jnp.full(m_sc.shape, NEG_BIG, F32)
        acc_sc[...] = jnp.zeros(acc_sc.shape, F32)

    shift = sh_ref[0, 0:1, :] if bounded else 0.0
    nb = tk // SEL_BLOCK
    if mode == "sel":
        rows = bias_ref[0] - shift

    def key_mask():
        t = i * tq + lax.broadcasted_iota(jnp.int32, (1, tq), 1)
        kpos = j * tk + lax.broadcasted_iota(jnp.int32, (tk, 1), 0)
        if mode == "sel":
            base = jnp.concatenate(
                [jnp.broadcast_to(rows[b:b + 1, :], (SEL_BLOCK, tq)) for b in range(nb)], axis=0)
            return jnp.where(kpos <= t, base, NEG_BIG)
        return jnp.where((kpos <= t) & (kpos > t - WINDOW), 0.0 - shift, NEG_BIG)

    k = k_ref[...]
    vT = vT_ref[...]
    raw = lambda h: lax.dot_general(k, q_ref[:, h * HEAD_DIM:(h + 1) * HEAD_DIM], _NT, preferred_element_type=F32)

    def accumulate(pTs):
        for h in range(NSA_HPG):
            acc_sc[h] = acc_sc[h] + jnp.dot(vT, pTs[h], preferred_element_type=F32)

    if bounded and mode == "sel":
        @pl.when((fl & 4) != 0)
        def _():
            mask_add = key_mask()
            accumulate([jnp.exp2(raw(h) + mask_add).astype(BF16) for h in range(NSA_HPG)])

        @pl.when((fl & 4) == 0)
        def _():
            def probs(h):
                s = raw(h).reshape(nb, SEL_BLOCK, tq) + rows[:, None, :]
                return jnp.exp2(s).astype(BF16).reshape(tk, tq)
            accumulate([probs(h) for h in range(NSA_HPG)])
    elif bounded:
        mask_add = key_mask()
        accumulate([jnp.exp2(raw(h) + mask_add).astype(BF16) for h in range(NSA_HPG)])
    else:
        mask_add = key_mask()
        sTs = [raw(h) + mask_add for h in range(NSA_HPG)]
        pTs, alphas = [], []
        for h in range(NSA_HPG):
            sT = sTs[h]
            m_old = m_sc[h]
            m_new = jnp.maximum(m_old, jnp.max(sT, axis=0, keepdims=True))
            alphas.append(jnp.exp2(m_old - m_new))
            pTs.append(jnp.exp2((sT - m_new).astype(BF16)))
            m_sc[h] = m_new
        for h in range(NSA_HPG):
            acc_sc[h] = alphas[h] * acc_sc[h] + jnp.dot(vT, pTs[h], preferred_element_type=F32)

    @pl.when((fl & 2) != 0)
    def _():
        for h in range(NSA_HPG):
            acc = acc_sc[h]
            l = acc[HEAD_DIM:HEAD_DIM + 1, :]
            o = acc[:HEAD_DIM, :] * ((1.0 / l) * g_ref[0, 0, h:h + 1, :])
            o_ref[:, h * HEAD_DIM:(h + 1) * HEAD_DIM] = o.T.astype(BF16)


def _steps(nq, lo_fn, hi_fn, reverse=False):
    qi, kj, fl = [], [], []
    for i in range(nq):
        js = list(range(lo_fn(i), hi_fn(i) + 1))
        if reverse:
            js = js[::-1]
        for n, j in enumerate(js):
            qi.append(i)
            kj.append(j)
            fl.append((1 if n == 0 else 0) | (2 if n == len(js) - 1 else 0))
    return (jnp.asarray(np.array(qi, np.int32)), jnp.asarray(np.array(kj, np.int32)),
            jnp.asarray(np.array(fl, np.int32)))


def _nsa_flash(mode, qn, k, vT, gates, shift, bias=None):
    use_bounded = jnp.max(shift) <= SOFTMAX_BOUND_MAX
    return lax.cond(use_bounded,
                    lambda: _nsa_flash_call(mode, True, qn, k, vT, gates, shift, bias),
                    lambda: _nsa_flash_call(mode, False, qn, k, vT, gates, shift, bias))


def _nsa_flash_call(mode, bounded, qn, k, vT, gates, shift, bias):
    T = qn.shape[0]
    tq = _tile(T, 1024)
    tk = _tile(T, 1024 if mode == "sel" else 512)
    nq = T // tq
    gw = NSA_HPG * HEAD_DIM
    hi = lambda i: ((i + 1) * tq - 1) // tk
    if mode == "sel":
        lo = lambda i: 0
        br = 1
    else:
        lo = lambda i: max(0, (i * tq - (WINDOW - 1)) // tk)
        br = 2
    qi, kj, fl = _steps(nq, lo, hi)
    fl = fl | 4 * ((kj + 1) * tk - 1 > qi * tq).astype(jnp.int32)
    in_specs = [pl.BlockSpec((tq, gw), lambda g, p, qi, kj, fl: (qi[p], g)),
                pl.BlockSpec((tk, HEAD_DIM), lambda g, p, qi, kj, fl: (kj[p], g)),
                pl.BlockSpec((VAUG, tk), lambda g, p, qi, kj, fl: (g, kj[p])),
                pl.BlockSpec((1, 1, 8, tq), lambda g, p, qi, kj, fl, br=br: (br, g, 0, qi[p])),
                pl.BlockSpec((1, 8, tq), lambda g, p, qi, kj, fl: (g, 0, qi[p]))]
    args = [qn, k, vT, gates, shift]
    if mode == "sel":
        in_specs.append(pl.BlockSpec((1, tk // SEL_BLOCK, tq), lambda g, p, qi, kj, fl: (g, kj[p], qi[p])))
        args.append(bias)
    kern = functools.partial(_flash_kernel, mode=mode, bounded=bounded, tq=tq, tk=tk)
    return pl.pallas_call(
        kern,
        grid_spec=pltpu.PrefetchScalarGridSpec(
            num_scalar_prefetch=3, grid=(NSA_GROUPS, int(qi.shape[0])),
            in_specs=in_specs,
            out_specs=pl.BlockSpec((tq, gw), lambda g, p, qi, kj, fl: (qi[p], g)),
            scratch_shapes=[pltpu.VMEM((NSA_HPG, 1, tq), F32), pltpu.VMEM((NSA_HPG, VAUG, tq), F32)]),
        out_shape=jax.ShapeDtypeStruct((T, NSA_HEADS * HEAD_DIM), BF16),
        compiler_params=_params(("parallel", "arbitrary"), 48),
        name="nsa_" + mode + ("_bounded" if bounded else "_online"),
    )(qi, kj, fl, *args)


def _sb_kernel(qi_ref, kj_ref, fl_ref, q_ref, k_ref, vT_ref, lm_ref, *rest, tq, tk, resume):
    if resume:
        acc_in, carry_in, o_ref, carry_sc, acc_sc = rest
    else:
        o_ref, acc_o, carry_o, carry_sc, acc_sc = rest
    p = pl.program_id(0)
    i = qi_ref[p]
    j = kj_ref[p]
    fl = fl_ref[p]

    @pl.when((fl & 1) != 0)
    def _():
        if resume:
            for h in range(SB_HEADS):
                carry_sc[h] = carry_in[h:h + 1, :]
                acc_sc[h] = acc_in[h * HEAD_DIM:(h + 1) * HEAD_DIM, :]
        else:
            carry_sc[...] = jnp.zeros(carry_sc.shape, F32)
            acc_sc[...] = jnp.zeros(acc_sc.shape, F32)

    @pl.when((jnp.max(carry_sc[...]) > SB_DEAD_LOG2) & ((fl & 4) == 0))
    def _():
        t = i * tq + lax.broadcasted_iota(jnp.int32, (1, tq), 1)
        kpos = j * tk + lax.broadcasted_iota(jnp.int32, (tk, 1), 0)
        past = kpos < t
        lm = lm_ref[...]
        hs = [slice(h * HEAD_DIM, (h + 1) * HEAD_DIM) for h in range(SB_HEADS)]
        zs = [lax.dot_general(k_ref[:, sl], q_ref[:, sl], _NT, preferred_element_type=F32) for sl in hs]
        lgs, css = [], []
        for h in range(SB_HEADS):
            z = zs[h]
            sp = jnp.maximum(z, 0.0) + jnp.log2(1.0 + jnp.exp2(-jnp.abs(z)))
            lk = jnp.where(past, -sp, 0.0)
            hi = lk.astype(BF16)
            lo = (lk - hi.astype(F32)).astype(BF16)
            css.append(jnp.dot(lm, jnp.concatenate([hi, lo], axis=1), preferred_element_type=F32))
            lgs.append(z - sp)
            carry_old = carry_sc[h]
            carry_sc[h] = carry_old + jnp.sum(lk, axis=0, keepdims=True)
            css[h] = css[h][:, :tq] + css[h][:, tq:] + carry_old
        for h in range(SB_HEADS):
            wgt = jnp.where(past, jnp.exp2(lgs[h] + css[h]), 0.0)
            acc_sc[h] = acc_sc[h] + jnp.dot(vT_ref[hs[h], :], wgt.astype(BF16), preferred_element_type=F32)

    @pl.when((fl & 2) != 0)
    def _():
        for h in range(SB_HEADS):
            o_ref[:, h * HEAD_DIM:(h + 1) * HEAD_DIM] = acc_sc[h].T.astype(BF16)
            if not resume:
                acc_o[h * HEAD_DIM:(h + 1) * HEAD_DIM, :] = acc_sc[h]
                carry_o[h:h + 1, :] = carry_sc[h]
        if not resume:
            carry_o[SB_HEADS:, :] = jnp.zeros((8 - SB_HEADS, tq), F32)


def _sb_call(q, k, vT, steps, tq, tk, state=None):
    T, W = q.shape
    qi, kj, fl = steps
    lmat = jnp.asarray(np.triu(np.ones((tk, tk), np.float32), 1), BF16)
    resume = state is not None
    qtile = lambda shape: pl.BlockSpec(shape, lambda p, qi, kj, fl: (qi[p], 0))
    qtileT = lambda rows: pl.BlockSpec((rows, tq), lambda p, qi, kj, fl: (0, qi[p]))
    in_specs = [qtile((tq, W)),
                pl.BlockSpec((tk, W), lambda p, qi, kj, fl: (kj[p], 0)),
                pl.BlockSpec((W, tk), lambda p, qi, kj, fl: (0, kj[p])),
                pl.BlockSpec((tk, tk), lambda p, qi, kj, fl: (0, 0))]
    y_sds = jax.ShapeDtypeStruct((T, W), BF16)
    if resume:
        in_specs += [qtileT(W), qtileT(8)]
        out_specs, out_shape = qtile((tq, W)), y_sds
    else:
        out_specs = [qtile((tq, W)), qtileT(W), qtileT(8)]
        out_shape = [y_sds, jax.ShapeDtypeStruct((W, T), F32), jax.ShapeDtypeStruct((8, T), F32)]
    return pl.pallas_call(
        functools.partial(_sb_kernel, tq=tq, tk=tk, resume=resume),
        grid_spec=pltpu.PrefetchScalarGridSpec(
            num_scalar_prefetch=3, grid=(int(qi.shape[0]),),
            in_specs=in_specs, out_specs=out_specs,
            scratch_shapes=[pltpu.VMEM((SB_HEADS, 1, tq), F32), pltpu.VMEM((SB_HEADS, HEAD_DIM, tq), F32)]),
        out_shape=out_shape,
        compiler_params=_params(("arbitrary",), 48), name="sb_far" if resume else "sb_near",
    )(qi, kj, fl, q, k, vT, lmat, *(state or ()))


def _stick_breaking(q, k, vT):
    T, W = q.shape
    tq = _tile(T, 512)
    tk = _tile(T, 256)
    nq = T // tq
    hi = lambda i: ((i + 1) * tq - 2) // tk
    lo_near = lambda i: max(0, hi(i) - SB_NEAR_TILES + 1)
    near = _steps(nq, lo_near, hi, reverse=True)
    y_near, acc, carry = _sb_call(q, k, vT, near, tq, tk)
    qi, kj, fl = [], [], []
    for i in range(nq):
        js = list(range(lo_near(i) - 1, -1, -1))
        for n, j in enumerate(js or [0]):
            qi.append(i)
            kj.append(j)
            fl.append((1 if n == 0 else 0) | (2 if n == max(len(js), 1) - 1 else 0) | (0 if js else 4))
    far = tuple(jnp.asarray(np.array(a, np.int32)) for a in (qi, kj, fl))
    with_far = [i for i in range(nq) if lo_near(i) > 0]
    if not with_far:
        return y_near
    alive = jnp.max(carry[:SB_HEADS, with_far[0] * tq:]) > SB_DEAD_LOG2
    return lax.cond(alive, lambda: _sb_call(q, k, vT, far, tq, tk, state=(acc, carry)), lambda: y_near)


def _mixout_kernel(yc_ref, ys_ref, yw_ref, ysb_ref, mq_ref, mk_ref, mvT_ref, g_ref, wn_ref, wsb_ref, wm_ref, wo_ref,
                   x_ref, gain_ref, x2_o, h2_o):
    D = x_ref.shape[1]
    yn = (yc_ref[...].astype(F32) + ys_ref[...].astype(F32) + yw_ref[...].astype(F32)).astype(BF16)
    a = jnp.dot(yn, wn_ref[...], preferred_element_type=F32)
    b = jnp.dot(ysb_ref[...], wsb_ref[...], preferred_element_type=F32)
    ym = []
    for h in range(MEM_HEADS):
        sl = slice(h * HEAD_DIM, (h + 1) * HEAD_DIM)
        sT = lax.dot_general(mk_ref[:, sl], mq_ref[:, sl], _NT, preferred_element_type=F32)
        e = jnp.exp2(sT - jnp.max(sT, axis=0, keepdims=True))
        l = jnp.sum(e, axis=0, keepdims=True)
        oT = jnp.dot(mvT_ref[sl, :], e.astype(BF16), preferred_element_type=F32) * (1.0 / l)
        ym.append(oT.T.astype(BF16))
    c = jnp.dot(jnp.concatenate(ym, axis=1), wm_ref[...], preferred_element_type=F32)
    sig = lambda n: jax.nn.sigmoid(g_ref[:, n * D:(n + 1) * D].astype(F32))
    mixed = (sig(0) * a + sig(1) * b + sig(2) * c).astype(BF16)
    x2 = x_ref[...] + jnp.dot(mixed, wo_ref[...], preferred_element_type=F32)
    x2_o[...] = x2
    h2_o[...] = _rms(x2, gain_ref[...]).astype(BF16)


def _mix_out(yc, ys, yw, ysb, mq, mk, mvT, P, wn, wsb, wm, wo, x, gain):
    T, D = x.shape
    tm = _tile(T, 256)
    rowi = lambda w: pl.BlockSpec((tm, w), lambda i: (i, 0))
    res = lambda a: pl.BlockSpec(a.shape, lambda i: (0, 0), pipeline_mode=pl.Buffered(1))
    return pl.pallas_call(
        _mixout_kernel, grid=(T // tm,),
        in_specs=[rowi(yc.shape[1]), rowi(ys.shape[1]), rowi(yw.shape[1]), rowi(ysb.shape[1]), rowi(mq.shape[1]),
                  res(mk), res(mvT), rowi(N_BRANCH * D), res(wn), res(wsb), res(wm), res(wo), rowi(D), res(gain)],
        out_specs=[rowi(D), rowi(D)],
        out_shape=[jax.ShapeDtypeStruct((T, D), F32), jax.ShapeDtypeStruct((T, D), BF16)],
        compiler_params=_params(("parallel",), 48), name="mix_out",
    )(yc, ys, yw, ysb, mq, mk, mvT, P, wn, wsb, wm, wo, x, gain)


def _ffn_kernel(h_ref, wg_ref, wu_ref, wd_ref, x_ref, o_ref, z_sc, *, nf):
    f = pl.program_id(1)

    def up():
        a = jnp.dot(h_ref[...], wg_ref[...], preferred_element_type=F32)
        b = jnp.dot(h_ref[...], wu_ref[...], preferred_element_type=F32)
        return (a * jax.nn.sigmoid(a) * b).astype(BF16)

    def down():
        return jnp.dot(z_sc[...], wd_ref[...], preferred_element_type=F32)

    @pl.when(f == 0)
    def _():
        z_sc[...] = up()

    @pl.when((f > 0) & (f < nf))
    def _():
        c = down()
        z_new = up()
        o_ref[...] = jnp.where(f == 1, x_ref[...], o_ref[...]) + c
        z_sc[...] = z_new

    @pl.when(f == nf)
    def _():
        o_ref[...] += down()


def _ffn(h2, wg, wu, wd, x2):
    T, D = x2.shape
    F = wg.shape[1]
    tm = _tile(T, 1024)
    tf = 512
    assert F % tf == 0
    nf = F // tf
    assert nf >= 2
    return pl.pallas_call(
        functools.partial(_ffn_kernel, nf=nf), grid=(T // tm, nf + 1),
        in_specs=[pl.BlockSpec((tm, D), lambda i, f: (i, 0)),
                  pl.BlockSpec((D, tf), lambda i, f: (0, jnp.minimum(f, nf - 1))),
                  pl.BlockSpec((D, tf), lambda i, f: (0, jnp.minimum(f, nf - 1))),
                  pl.BlockSpec((tf, D), lambda i, f: (jnp.maximum(f - 1, 0), 0)),
                  pl.BlockSpec((tm, D), lambda i, f: (i, 0), pipeline_mode=pl.Buffered(1))],
        out_specs=pl.BlockSpec((tm, D), lambda i, f: (i, 0)),
        out_shape=jax.ShapeDtypeStruct((T, D), F32),
        scratch_shapes=[pltpu.VMEM((tm, tf), BF16)],
        compiler_params=_params(("parallel", "arbitrary"), 56), name="ffn",
    )(h2, wg.astype(BF16), wu.astype(BF16), wd.astype(BF16), x2)


def _layer(x, mem, pos_col, posc_col, consts, layer, attn_norm, w_in_all, nsa_q_norm, nsa_kc_norm, nsa_ks_norm, nsa_kw_norm,
           cmp_k_pe, cmp_k_w1, cmp_k_w2, cmp_v_pe, cmp_v_w1, cmp_v_w2, mem_norm, w_mem_kv,
           mem_q_norm, mem_k_norm, w_o_nsa, w_o_sb, w_o_mem, w_out, ffn_norm,
           w_ffn_gate, w_ffn_up, w_ffn_down):
    T, D = x.shape
    inv2, sgn, ovT = consts
    hd = HEAD_DIM
    row = lambda g: g.reshape(1, -1)

    q_w, kv_w, gn_w = NSA_HEADS * hd, 6 * NSA_GROUPS * hd, 3 * NSA_HEADS
    sb_w, mq_w, gm_w = 3 * SB_HEADS * hd, MEM_HEADS * hd, N_BRANCH * D
    o_gn = q_w + kv_w
    o_sb = o_gn + gn_w
    o_mq = o_sb + sb_w
    o_gm = o_mq + mq_w
    assert w_in_all.shape[2] == o_gm + gm_w
    segs = ((0, o_gm, gm_w), (gm_w, 0, o_gn), (gm_w + o_gn, o_sb, o_gm - o_sb))
    w_main, w_gate = _regroup_w_in(w_in_all, layer, segs, (o_gn, gn_w), gm_w + o_gn + o_gm - o_sb)
    P, Pg = _in_proj(x, row(attn_norm), w_main, w_gate)

    (qn, kc_raw, vc_raw, ksn, kwn, vsT, vwT, sbq, sbk, sbvT, memq, gT, qn2, kn2) = _prep(
        P, Pg, gm_w, pos_col, inv2, sgn, row(nsa_q_norm), row(nsa_ks_norm), row(nsa_kw_norm), row(mem_q_norm))

    q2 = jnp.max(qn2[::8].reshape(NSA_GROUPS, NSA_HPG, T), axis=1)
    k2 = jnp.max(kn2[::8], axis=1).reshape(2, NSA_GROUPS)
    bound = lambda kk: jnp.broadcast_to((1.02 * jnp.sqrt(q2 * kk[:, None]))[:, None, :], (NSA_GROUPS, 8, T))
    shift_sel, shift_win = bound(k2[0]), bound(k2[1])

    gates = gT[:gn_w].reshape(NSA_GROUPS, NSA_HPG, 3, T).transpose(2, 0, 1, 3)
    gates = jnp.pad(gates, ((0, 0), (0, 0), (0, 8 - NSA_HPG), (0, 0)))

    n_pad = T // CMP_STRIDE
    half = CMP_LEN // 2

    def w1_pack(w1):
        return jnp.concatenate([w1[:half].reshape(half * hd, -1), w1[half:].reshape(half * hd, -1)], axis=1).astype(BF16)

    def pe_pack(pe):
        return jnp.pad(pe.reshape(2, half * hd), ((0, 6), (0, 0))).astype(BF16)

    kc, vcT = _compress(
        kc_raw.reshape(NSA_GROUPS, n_pad, CMP_STRIDE * hd), vc_raw.reshape(NSA_GROUPS, n_pad, CMP_STRIDE * hd),
        w1_pack(cmp_k_w1), w1_pack(cmp_v_w1), pe_pack(cmp_k_pe), pe_pack(cmp_v_pe),
        cmp_k_w2.astype(BF16), cmp_v_w2.astype(BF16), row(nsa_kc_norm), posc_col, inv2, sgn)

    y_cmp, bias = _cmp_select(qn, kc, vcT, ovT, gates)
    y_sel = _nsa_flash("sel", qn, ksn, vsT, gates, shift_sel, bias)
    y_win = _nsa_flash("win", qn, kwn, vwT, gates, shift_win)
    y_sb = _stick_breaking(sbq, sbk, sbvT)

    mk, mvT = _memkv(mem, row(mem_norm), w_mem_kv.astype(BF16), row(mem_k_norm))

    x2, h2 = _mix_out(y_cmp, y_sel, y_win, y_sb, memq, mk, mvT, P, w_o_nsa.astype(BF16), w_o_sb.astype(BF16),
                      w_o_mem.astype(BF16), w_out.astype(BF16), x, row(ffn_norm))
    return _ffn(h2, w_ffn_gate, w_ffn_up, w_ffn_down, x2)


def kernel(x, mem, positions, attn_norm, w_in, nsa_q_norm, nsa_kc_norm, nsa_ks_norm, nsa_kw_norm, cmp_k_pe, cmp_k_w1, cmp_k_w2, cmp_v_pe, cmp_v_w1, cmp_v_w2, mem_norm, w_mem_kv, mem_q_norm, mem_k_norm, w_o_nsa, w_o_sb, w_o_mem, w_out, ffn_norm, w_ffn_gate, w_ffn_up, w_ffn_down):
    B, T, D = x.shape
    assert T % (4 * LANE) == 0 and T // SEL_BLOCK >= 8
    n_pad = T // CMP_STRIDE
    n_sel = T // SEL_BLOCK
    inv = 1.0 / (ROPE_THETA ** (jnp.arange(0, HEAD_DIM, 2, dtype=F32) / HEAD_DIM))
    inv2 = jnp.concatenate([inv, inv]).reshape(1, HEAD_DIM)
    sgn = jnp.concatenate([-jnp.ones((HEAD_DIM // 2,), F32), jnp.ones((HEAD_DIM // 2,), F32)]).reshape(1, HEAD_DIM)
    cs = np.arange(n_pad)[None, :] * CMP_STRIDE
    ss = np.arange(n_sel)[:, None] * SEL_BLOCK
    ovT = jnp.asarray(((cs < ss + SEL_BLOCK) & (cs + CMP_LEN - 1 >= ss)).astype(np.float32), BF16)
    consts = (inv2, sgn, ovT)
    depth = w_in.shape[0]
    outs = []
    for b in range(B):
        xb = x[b]
        posf = positions[b].astype(F32)
        pos_col = posf.reshape(T, 1)
        posc = jnp.concatenate([posf[CMP_LEN - 1::CMP_STRIDE], posf[-1:]]).reshape(n_pad, 1)
        for l in range(depth):
            xb = _layer(xb, mem[b], pos_col, posc, consts, l, attn_norm[l], w_in, nsa_q_norm[l], nsa_kc_norm[l],
                        nsa_ks_norm[l], nsa_kw_norm[l], cmp_k_pe[l], cmp_k_w1[l], cmp_k_w2[l], cmp_v_pe[l],
                        cmp_v_w1[l], cmp_v_w2[l], mem_norm[l], w_mem_kv[l], mem_q_norm[l], mem_k_norm[l],
                        w_o_nsa[l], w_o_sb[l], w_o_mem[l], w_out[l], ffn_norm[l],
                        w_ffn_gate[l], w_ffn_up[l], w_ffn_down[l])
        outs.append(xb)
    return outs[0][None] if B == 1 else jnp.stack(outs, axis=0)
```

```python
import functools

import numpy as np
import jax
import jax.numpy as jnp
from jax import lax
from jax.experimental import pallas as pl
from jax.experimental.pallas import tpu as pltpu

HEAD_DIM = 128
NSA_HEADS = 8
NSA_GROUPS = 2
NSA_HPG = NSA_HEADS // NSA_GROUPS
SB_HEADS = 4
MEM_HEADS = 4
CMP_LEN = 32
CMP_STRIDE = 16
CMP_HIDDEN = 2 * HEAD_DIM
SEL_BLOCK = 64
SEL_TOPK = 16
WINDOW = 512
ROPE_THETA = 10000.0
NORM_EPS = 1e-6
NEG_BIG = -1e30
N_BRANCH = 3
SCALE = HEAD_DIM ** -0.5
LOG2E = 1.4426950408889634
QSCALE = SCALE * LOG2E
SOFTMAX_BOUND_MAX = 50.0
CMP_BUCKETS = 8
SB_NEAR_TILES = 2
SB_DEAD_LOG2 = -160.0
VAUG = HEAD_DIM + 16

LANE = 128
MIB = 1 << 20
BF16 = jnp.bfloat16
F32 = jnp.float32

_NT = (((1,), (1,)), ((), ()))


def _tile(n, pref):
    t = min(n, pref)
    assert n % t == 0, (n, pref)
    return t


def _params(sem, vmem_mib):
    return pltpu.CompilerParams(dimension_semantics=sem, vmem_limit_bytes=vmem_mib * MIB)


def _rms(x, gain):
    return x * lax.rsqrt(jnp.mean(x * x, axis=-1, keepdims=True) + NORM_EPS) * gain


def _rope_tables(pos, inv2, sgn):
    ang = pos * inv2
    return jnp.cos(ang), jnp.sin(ang) * sgn


def _rope(x, c, s):
    return x * c + pltpu.roll(x, HEAD_DIM // 2, 1) * s


def _regroup_kernel(w_ref, o_ref, og_ref, *, segs, gate):
    for dst, src, width in segs:
        o_ref[:, dst:dst + width] = w_ref[0, :, src:src + width].astype(BF16)
    g0, gw = gate
    lane = lax.broadcasted_iota(jnp.int32, (w_ref.shape[1], LANE), 1)
    og_ref[...] = jnp.where(lane < gw, w_ref[0, :, g0:g0 + LANE], 0.0).astype(BF16)


def _regroup_w_in(w_in_all, layer, segs, gate, n_out):
    _, D, N = w_in_all.shape
    tr = _tile(D, 64)
    return pl.pallas_call(
        functools.partial(_regroup_kernel, segs=segs, gate=gate), grid=(D // tr,),
        in_specs=[pl.BlockSpec((1, tr, N), lambda i: (layer, i, 0))],
        out_specs=[pl.BlockSpec((tr, n_out), lambda i: (i, 0)), pl.BlockSpec((tr, LANE), lambda i: (i, 0))],
        out_shape=[jax.ShapeDtypeStruct((D, n_out), BF16), jax.ShapeDtypeStruct((D, LANE), BF16)],
        compiler_params=_params(("parallel",), 32), name="regroup_w_in",
    )(w_in_all)


def _proj_kernel(x_ref, g_ref, w_ref, wg_ref, o_ref, og_ref, hn_ref):
    @pl.when(pl.program_id(1) == 0)
    def _():
        hn_ref[...] = _rms(x_ref[...], g_ref[...]).astype(BF16)
        og_ref[...] = jnp.dot(hn_ref[...], wg_ref[...], preferred_element_type=F32)

    o_ref[...] = jnp.dot(hn_ref[...], w_ref[...], preferred_element_type=F32).astype(BF16)


def _in_proj(x, gain, w_main, w_gate):
    T, D = x.shape
    N = w_main.shape[1]
    tm = _tile(T, 1024)
    tn = 1536
    assert N % tn == 0
    return pl.pallas_call(
        _proj_kernel,
        grid=(T // tm, N // tn),
        in_specs=[
            pl.BlockSpec((tm, D), lambda i, j: (i, 0)),
            pl.BlockSpec((1, D), lambda i, j: (0, 0)),
            pl.BlockSpec((D, tn), lambda i, j: (0, j)),
            pl.BlockSpec((D, LANE), lambda i, j: (0, 0)),
        ],
        out_specs=[pl.BlockSpec((tm, tn), lambda i, j: (i, j)), pl.BlockSpec((tm, LANE), lambda i, j: (i, 0))],
        out_shape=[jax.ShapeDtypeStruct((T, N), BF16), jax.ShapeDtypeStruct((T, LANE), F32)],
        scratch_shapes=[pltpu.VMEM((tm, D), BF16)],
        compiler_params=_params(("parallel", "arbitrary"), 52),
        name="in_proj",
    )(x, gain, w_main, w_gate)


def _prep_kernel(pos_ref, inv_ref, sgn_ref, gq_ref, gks_ref, gkw_ref, gmq_ref,
                 q_ref, kc_ref, vc_ref, ks_ref, vs_ref, kw_ref, vw_ref,
                 sq_ref, sk_ref, sv_ref, mq_ref, gn_ref,
                 qn_o, kc_o, vc_o, ks_o, kw_o, vsT_o, vwT_o, sq_o, sk_o, svT_o, mq_o, gT_o, qn2_o, kn2_o):
    c, s = _rope_tables(pos_ref[...], inv_ref[...], sgn_ref[...])
    hd = HEAD_DIM
    tp = pos_ref.shape[0]
    f32 = lambda ref, sl: ref[:, sl].astype(F32)
    ones8 = jnp.ones((8, hd), F32)

    def sqnorm_rows(xb):
        x = xb.astype(F32)
        return lax.dot_general(ones8, x * x, _NT, preferred_element_type=F32)

    for h in range(NSA_HEADS):
        sl = slice(h * hd, (h + 1) * hd)
        qb = (_rope(_rms(f32(q_ref, sl), gq_ref[...]), c, s) * QSCALE).astype(BF16)
        qn_o[:, sl] = qb
        qn2_o[h * 8:(h + 1) * 8, :] = sqnorm_rows(qb)
    ones_rows = (lax.broadcasted_iota(jnp.int32, (VAUG - hd, tp), 0) == 0).astype(F32).astype(BF16)
    for g in range(NSA_GROUPS):
        sl = slice(g * hd, (g + 1) * hd)
        kc_o[g] = kc_ref[:, sl]
        vc_o[g] = vc_ref[:, sl]
        ksb = _rope(_rms(f32(ks_ref, sl), gks_ref[...]), c, s).astype(BF16)
        kwb = _rope(_rms(f32(kw_ref, sl), gkw_ref[...]), c, s).astype(BF16)
        ks_o[:, sl] = ksb
        kw_o[:, sl] = kwb
        kn2_o[g * 8:(g + 1) * 8, :] = sqnorm_rows(ksb)
        kn2_o[(NSA_GROUPS + g) * 8:(NSA_GROUPS + g + 1) * 8, :] = sqnorm_rows(kwb)
        vsT_o[g * VAUG:g * VAUG + hd, :] = f32(vs_ref, sl).T.astype(BF16)
        vsT_o[g * VAUG + hd:(g + 1) * VAUG, :] = ones_rows
        vwT_o[g * VAUG:g * VAUG + hd, :] = f32(vw_ref, sl).T.astype(BF16)
        vwT_o[g * VAUG + hd:(g + 1) * VAUG, :] = ones_rows
    for h in range(SB_HEADS):
        sl = slice(h * hd, (h + 1) * hd)
        sq_o[:, sl] = (f32(sq_ref, sl) * QSCALE).astype(BF16)
        svT_o[sl, :] = f32(sv_ref, sl).T.astype(BF16)
    sk_o[...] = sk_ref[...]
    for h in range(MEM_HEADS):
        sl = slice(h * hd, (h + 1) * hd)
        mq_o[:, sl] = (_rms(f32(mq_ref, sl), gmq_ref[...]) * QSCALE).astype(BF16)
    gT_o[...] = jax.nn.sigmoid(gn_ref[...]).T


def _prep(P, Pg, c0, pos_col, inv2, sgn, gq, gks, gkw, gmq):
    T = P.shape[0]
    tp = _tile(T, 512)
    hd = HEAD_DIM
    row = lambda w, c: pl.BlockSpec((tp, w), lambda i, c=c: (i, c))
    const = lambda: pl.BlockSpec((1, hd), lambda i: (0, 0))
    assert c0 % (8 * hd) == 0
    at = lambda w, col: row(w, (c0 + col) // w)
    in_specs = [pl.BlockSpec((tp, 1), lambda i: (i, 0)), const(), const(), const(), const(), const(), const(),
                at(8 * hd, 0),
                at(2 * hd, 8 * hd), at(2 * hd, 10 * hd),
                at(2 * hd, 12 * hd), at(2 * hd, 14 * hd),
                at(2 * hd, 16 * hd), at(2 * hd, 18 * hd),
                at(4 * hd, 20 * hd), at(4 * hd, 24 * hd), at(4 * hd, 28 * hd),
                at(4 * hd, 32 * hd),
                row(hd, 0)]
    colT = lambda w: pl.BlockSpec((w, tp), lambda i: (0, i))
    grp = lambda w: pl.BlockSpec((NSA_GROUPS, tp, w), lambda i: (0, i, 0))
    out_specs = [row(8 * hd, 0), grp(hd), grp(hd), row(2 * hd, 0),
                 row(2 * hd, 0), colT(NSA_GROUPS * VAUG), colT(NSA_GROUPS * VAUG),
                 row(4 * hd, 0), row(4 * hd, 0), colT(4 * hd), row(4 * hd, 0), colT(hd),
                 colT(NSA_HEADS * 8), colT(2 * NSA_GROUPS * 8)]
    sds = jax.ShapeDtypeStruct
    out_shape = [sds((T, 8 * hd), BF16),
                 sds((NSA_GROUPS, T, hd), BF16), sds((NSA_GROUPS, T, hd), BF16), sds((T, 2 * hd), BF16),
                 sds((T, 2 * hd), BF16),
                 sds((NSA_GROUPS * VAUG, T), BF16), sds((NSA_GROUPS * VAUG, T), BF16),
                 sds((T, 4 * hd), BF16), sds((T, 4 * hd), BF16), sds((4 * hd, T), BF16), sds((T, 4 * hd), BF16),
                 sds((hd, T), F32), sds((NSA_HEADS * 8, T), F32), sds((2 * NSA_GROUPS * 8, T), F32)]
    return pl.pallas_call(
        _prep_kernel, grid=(T // tp,), in_specs=in_specs, out_specs=out_specs, out_shape=out_shape,
        compiler_params=_params(("parallel",), 48), name="prep",
    )(pos_col, inv2, sgn, gq, gks, gkw, gmq, *([P] * 11), Pg)


def _gelu_tanh(x):
    return 0.5 * x * (1.0 + jnp.tanh(0.7978845608028654 * (x + 0.044715 * (x * x * x))))


def _compress_one(x, w1, pe, w2):
    n = x.shape[0]
    ab = jnp.dot(x, w1, preferred_element_type=F32)
    pr = jnp.dot(pe, w1, preferred_element_type=F32)
    pec = pr[0:1, :CMP_HIDDEN] + pr[1:2, CMP_HIDDEN:]
    hid = ab[:, :CMP_HIDDEN] + pltpu.roll(ab[:, CMP_HIDDEN:], n - 1, 0) + pec
    return jnp.dot(_gelu_tanh(hid).astype(BF16), w2, preferred_element_type=F32)


def _compress_kernel(xk_ref, xv_ref, w1k_ref, w1v_ref, pek_ref, pev_ref, w2k_ref, w2v_ref,
                     gk_ref, pos_ref, inv_ref, sgn_ref, kc_o, vcT_o):
    c, s = _rope_tables(pos_ref[...], inv_ref[...], sgn_ref[...])
    k = _compress_one(xk_ref[0], w1k_ref[...], pek_ref[...], w2k_ref[...])
    kc_o[0] = _rope(_rms(k, gk_ref[...]), c, s).astype(BF16)
    v = _compress_one(xv_ref[0], w1v_ref[...], pev_ref[...], w2v_ref[...])
    vcT_o[0] = v.T.astype(BF16)


def _compress(xk, xv, w1k, w1v, pek, pev, w2k, w2v, gk, posc, inv2, sgn):
    G, n, W = xk.shape
    full = lambda a: pl.BlockSpec(a.shape, lambda g: (0,) * a.ndim)
    grp = pl.BlockSpec((1, n, W), lambda g: (g, 0, 0))
    return pl.pallas_call(
        _compress_kernel, grid=(G,),
        in_specs=[grp, grp, full(w1k), full(w1v), full(pek), full(pev), full(w2k), full(w2v),
                  full(gk), full(posc), full(inv2), full(sgn)],
        out_specs=[pl.BlockSpec((1, n, HEAD_DIM), lambda g: (g, 0, 0)),
                   pl.BlockSpec((1, HEAD_DIM, n), lambda g: (g, 0, 0))],
        out_shape=[jax.ShapeDtypeStruct((G, n, HEAD_DIM), BF16), jax.ShapeDtypeStruct((G, HEAD_DIM, n), BF16)],
        compiler_params=_params(("parallel",), 48), name="compress",
    )(xk, xv, w1k, w1v, pek, pev, w2k, w2v, gk, posc, inv2, sgn)


def _memkv_kernel(mem_ref, gm_ref, w_ref, gk_ref, k_o, vT_o):
    hn = _rms(mem_ref[...], gm_ref[...]).astype(BF16)
    kv = jnp.dot(hn, w_ref[...], preferred_element_type=F32)
    mw = MEM_HEADS * HEAD_DIM
    for h in range(MEM_HEADS):
        sl = slice(h * HEAD_DIM, (h + 1) * HEAD_DIM)
        k_o[:, sl] = _rms(kv[:, sl], gk_ref[...]).astype(BF16)
        vT_o[sl, :] = kv[:, mw + h * HEAD_DIM: mw + (h + 1) * HEAD_DIM].T.astype(BF16)


def _memkv(mem, gm, w, gk):
    M = mem.shape[0]
    mw = MEM_HEADS * HEAD_DIM
    return pl.pallas_call(
        _memkv_kernel,
        out_shape=[jax.ShapeDtypeStruct((M, mw), BF16), jax.ShapeDtypeStruct((mw, M), BF16)],
        compiler_params=pltpu.CompilerParams(vmem_limit_bytes=32 * MIB), name="mem_kv",
    )(mem, gm, w, gk)


def _cmp_kernel(q_ref, kc_ref, vcT_ref, ovT_ref, g_ref, *rest, tq, i0, n_pad, n_sel, top_k):
    y_ref, b_ref = rest[-2:]
    i = i0 + pl.program_id(1)
    t = i * tq + lax.broadcasted_iota(jnp.int32, (1, tq), 1)
    n_end = lax.broadcasted_iota(jnp.int32, (n_pad, 1), 0) * CMP_STRIDE + (CMP_LEN - 1)
    valid = n_end <= t
    kc = kc_ref[0]
    vcT = vcT_ref[0]
    has_valid = (t >= CMP_LEN - 1).astype(F32)
    sms = []
    for h in range(NSA_HPG):
        q_h = q_ref[:, h * HEAD_DIM:(h + 1) * HEAD_DIM]
        sms.append(jnp.where(valid, lax.dot_general(kc, q_h, _NT, preferred_element_type=F32), NEG_BIG))
    psum = jnp.zeros((n_pad, tq), F32)
    for h in range(NSA_HPG):
        e = jnp.exp2(sms[h] - jnp.max(sms[h], axis=0, keepdims=True))
        p = e * (has_valid / jnp.sum(e, axis=0, keepdims=True))
        oT = jnp.dot(vcT, p.astype(BF16), preferred_element_type=F32)
        y_ref[:, h * HEAD_DIM:(h + 1) * HEAD_DIM] = (oT * g_ref[0, 0, h:h + 1, :]).T.astype(BF16)
        psum = psum + p
    hi = psum.astype(BF16)
    r1 = psum - hi.astype(F32)
    mid = r1.astype(BF16)
    lo = (r1 - mid.astype(F32)).astype(BF16)
    parts = jnp.dot(ovT_ref[...], jnp.concatenate([hi, mid, lo], axis=1), preferred_element_type=F32)
    imp = parts[:, :tq] + parts[:, tq:2 * tq] + parts[:, 2 * tq:]
    s_i = lax.broadcasted_iota(jnp.int32, (n_sel, 1), 0)
    cur = lax.shift_right_logical(t, 6)
    forced = (s_i == 0) | (s_i == cur) | (s_i == cur - 1)
    future = s_i * SEL_BLOCK > t
    w = jnp.where(forced, jnp.inf, jnp.where(future, -jnp.inf, imp))
    s_f = jnp.broadcast_to(s_i.astype(F32), (n_sel, tq))
    for _ in range(top_k):
        m = jnp.max(w, axis=0, keepdims=True)
        idx = jnp.min(jnp.where(w == m, s_f, float(n_sel)), axis=0, keepdims=True)
        w = jnp.where(s_f == idx, -jnp.inf, w)
    b_ref[0, :n_sel, :] = jnp.where(future, NEG_BIG, jnp.where(w == -jnp.inf, 0.0, NEG_BIG))
    n_all = b_ref.shape[1]
    if n_sel < n_all:
        b_ref[0, n_sel:, :] = jnp.full((n_all - n_sel, tq), NEG_BIG, F32)


def _round_up(n, m):
    return -(-n // m) * m


def _cmp_select(qn, kc, vcT, ovT, gates):
    T = qn.shape[0]
    G, n_pad, _ = kc.shape
    n_sel = T // SEL_BLOCK
    tq = _tile(T, 512)
    nq = T // tq
    gw = NSA_HPG * HEAD_DIM
    nb = min(CMP_BUCKETS, nq)
    assert nq % nb == 0
    per = nq // nb
    out_shape = [jax.ShapeDtypeStruct((T, NSA_HEADS * HEAD_DIM), BF16), jax.ShapeDtypeStruct((G, n_sel, T), F32)]
    buffers = ()
    for b in range(nb):
        i0 = b * per
        t_max = (i0 + per) * tq - 1
        n_len = min(n_pad, _round_up(max(t_max - (CMP_LEN - 1), 0) // CMP_STRIDE + 1, LANE))
        s_len = min(n_sel, _round_up(t_max // SEL_BLOCK + 1, 8))
        kern = functools.partial(_cmp_kernel, tq=tq, i0=i0, n_pad=n_len, n_sel=s_len, top_k=min(SEL_TOPK, n_sel))
        buffers = pl.pallas_call(
            kern, grid=(G, per),
            in_specs=[pl.BlockSpec((tq, gw), lambda g, i, i0=i0: (i0 + i, g)),
                      pl.BlockSpec((1, n_len, HEAD_DIM), lambda g, i: (g, 0, 0)),
                      pl.BlockSpec((1, HEAD_DIM, n_len), lambda g, i: (g, 0, 0)),
                      pl.BlockSpec((s_len, n_len), lambda g, i: (0, 0)),
                      pl.BlockSpec((1, 1, 8, tq), lambda g, i, i0=i0: (0, g, 0, i0 + i))]
            + [pl.BlockSpec(memory_space=pl.ANY)] * len(buffers),
            out_specs=[pl.BlockSpec((tq, gw), lambda g, i, i0=i0: (i0 + i, g)),
                       pl.BlockSpec((1, n_sel, tq), lambda g, i, i0=i0: (g, 0, i0 + i))],
            out_shape=out_shape,
            input_output_aliases={5: 0, 6: 1} if buffers else {},
            compiler_params=_params(("parallel", "parallel"), 48), name="cmp_select_%d" % b,
        )(qn, kc, vcT, ovT, gates, *buffers)
    return buffers


def _flash_kernel(qi_ref, kj_ref, fl_ref, q_ref, k_ref, vT_ref, g_ref, sh_ref, *rest, mode, bounded, tq, tk):
    if mode == "sel":
        bias_ref, o_ref, m_sc, acc_sc = rest
    else:
        o_ref, m_sc, acc_sc = rest
    p = pl.program_id(1)
    i = qi_ref[p]
    j = kj_ref[p]
    fl = fl_ref[p]

    @pl.when((fl & 1) != 0)
    def _():
        m_sc[...] = jnp.full(m_sc.shape, NEG_BIG, F32)
        acc_sc[...] = jnp.zeros(acc_sc.shape, F32)

    shift = sh_ref[0, 0:1, :] if bounded else 0.0
    nb = tk // SEL_BLOCK
    if mode == "sel":
        rows = bias_ref[0] - shift

    def key_mask():
        t = i * tq + lax.broadcasted_iota(jnp.int32, (1, tq), 1)
        kpos = j * tk + lax.broadcasted_iota(jnp.int32, (tk, 1), 0)
        if mode == "sel":
            base = jnp.concatenate(
                [jnp.broadcast_to(rows[b:b + 1, :], (SEL_BLOCK, tq)) for b in range(nb)], axis=0)
            return jnp.where(kpos <= t, base, NEG_BIG)
        return jnp.where((kpos <= t) & (kpos > t - WINDOW), 0.0 - shift, NEG_BIG)

    k = k_ref[...]
    vT = vT_ref[...]
    raw = lambda h: lax.dot_general(k, q_ref[:, h * HEAD_DIM:(h + 1) * HEAD_DIM], _NT, preferred_element_type=F32)

    def accumulate(pTs):
        for h in range(NSA_HPG):
            acc_sc[h] = acc_sc[h] + jnp.dot(vT, pTs[h], preferred_element_type=F32)

    if bounded and mode == "sel":
        @pl.when((fl & 4) != 0)
        def _():
            mask_add = key_mask()
            accumulate([jnp.exp2(raw(h) + mask_add).astype(BF16) for h in range(NSA_HPG)])

        @pl.when((fl & 4) == 0)
        def _():
            def probs(h):
                s = raw(h).reshape(nb, SEL_BLOCK, tq) + rows[:, None, :]
                return jnp.exp2(s).astype(BF16).reshape(tk, tq)
            accumulate([probs(h) for h in range(NSA_HPG)])
    elif bounded:
        mask_add = key_mask()
        accumulate([jnp.exp2(raw(h) + mask_add).astype(BF16) for h in range(NSA_HPG)])
    else:
        mask_add = key_mask()
        sTs = [raw(h) + mask_add for h in range(NSA_HPG)]
        pTs, alphas = [], []
        for h in range(NSA_HPG):
            sT = sTs[h]
            m_old = m_sc[h]
            m_new = jnp.maximum(m_old, jnp.max(sT, axis=0, keepdims=True))
            alphas.append(jnp.exp2(m_old - m_new))
            pTs.append(jnp.exp2((sT - m_new).astype(BF16)))
            m_sc[h] = m_new
        for h in range(NSA_HPG):
            acc_sc[h] = alphas[h] * acc_sc[h] + jnp.dot(vT, pTs[h], preferred_element_type=F32)

    @pl.when((fl & 2) != 0)
    def _():
        for h in range(NSA_HPG):
            acc = acc_sc[h]
            l = acc[HEAD_DIM:HEAD_DIM + 1, :]
            o = acc[:HEAD_DIM, :] * ((1.0 / l) * g_ref[0, 0, h:h + 1, :])
            o_ref[:, h * HEAD_DIM:(h + 1) * HEAD_DIM] = o.T.astype(BF16)


def _steps(nq, lo_fn, hi_fn, reverse=False):
    qi, kj, fl = [], [], []
    for i in range(nq):
        js = list(range(lo_fn(i), hi_fn(i) + 1))
        if reverse:
            js = js[::-1]
        for n, j in enumerate(js):
            qi.append(i)
            kj.append(j)
            fl.append((1 if n == 0 else 0) | (2 if n == len(js) - 1 else 0))
    return (jnp.asarray(np.array(qi, np.int32)), jnp.asarray(np.array(kj, np.int32)),
            jnp.asarray(np.array(fl, np.int32)))


def _nsa_flash(mode, qn, k, vT, gates, shift, bias=None):
    use_bounded = jnp.max(shift) <= SOFTMAX_BOUND_MAX
    return lax.cond(use_bounded,
                    lambda: _nsa_flash_call(mode, True, qn, k, vT, gates, shift, bias),
                    lambda: _nsa_flash_call(mode, False, qn, k, vT, gates, shift, bias))


def _nsa_flash_call(mode, bounded, qn, k, vT, gates, shift, bias):
    T = qn.shape[0]
    tq = _tile(T, 1024 if mode == "sel" else WINDOW)
    tk = _tile(T, 1024 if mode == "sel" else 512)
    nq = T // tq
    gw = NSA_HPG * HEAD_DIM
    hi = lambda i: ((i + 1) * tq - 1) // tk
    if mode == "sel":
        lo = lambda i: 0
        br = 1
    else:
        lo = lambda i: max(0, (i * tq - (WINDOW - 1)) // tk)
        br = 2
    qi, kj, fl = _steps(nq, lo, hi)
    fl = fl | 4 * ((kj + 1) * tk - 1 > qi * tq).astype(jnp.int32)
    in_specs = [pl.BlockSpec((tq, gw), lambda g, p, qi, kj, fl: (qi[p], g)),
                pl.BlockSpec((tk, HEAD_DIM), lambda g, p, qi, kj, fl: (kj[p], g)),
                pl.BlockSpec((VAUG, tk), lambda g, p, qi, kj, fl: (g, kj[p])),
                pl.BlockSpec((1, 1, 8, tq), lambda g, p, qi, kj, fl, br=br: (br, g, 0, qi[p])),
                pl.BlockSpec((1, 8, tq), lambda g, p, qi, kj, fl: (g, 0, qi[p]))]
    args = [qn, k, vT, gates, shift]
    if mode == "sel":
        in_specs.append(pl.BlockSpec((1, tk // SEL_BLOCK, tq), lambda g, p, qi, kj, fl: (g, kj[p], qi[p])))
        args.append(bias)
    kern = functools.partial(_flash_kernel, mode=mode, bounded=bounded, tq=tq, tk=tk)
    return pl.pallas_call(
        kern,
        grid_spec=pltpu.PrefetchScalarGridSpec(
            num_scalar_prefetch=3, grid=(NSA_GROUPS, int(qi.shape[0])),
            in_specs=in_specs,
            out_specs=pl.BlockSpec((tq, gw), lambda g, p, qi, kj, fl: (qi[p], g)),
            scratch_shapes=[pltpu.VMEM((NSA_HPG, 1, tq), F32), pltpu.VMEM((NSA_HPG, VAUG, tq), F32)]),
        out_shape=jax.ShapeDtypeStruct((T, NSA_HEADS * HEAD_DIM), BF16),
        compiler_params=_params(("parallel", "arbitrary"), 48),
        name="nsa_" + mode + ("_bounded" if bounded else "_online"),
    )(qi, kj, fl, *args)


def _sb_kernel(qi_ref, kj_ref, fl_ref, q_ref, k_ref, vT_ref, lm_ref, *rest, tq, tk, resume):
    if resume:
        acc_in, carry_in, o_ref, carry_sc, acc_sc = rest
    else:
        o_ref, acc_o, carry_o, carry_sc, acc_sc = rest
    p = pl.program_id(0)
    i = qi_ref[p]
    j = kj_ref[p]
    fl = fl_ref[p]

    @pl.when((fl & 1) != 0)
    def _():
        if resume:
            for h in range(SB_HEADS):
                carry_sc[h] = carry_in[h:h + 1, :]
                acc_sc[h] = acc_in[h * HEAD_DIM:(h + 1) * HEAD_DIM, :]
        else:
            carry_sc[...] = jnp.zeros(carry_sc.shape, F32)
            acc_sc[...] = jnp.zeros(acc_sc.shape, F32)

    @pl.when((jnp.max(carry_sc[...]) > SB_DEAD_LOG2) & ((fl & 4) == 0))
    def _():
        t = i * tq + lax.broadcasted_iota(jnp.int32, (1, tq), 1)
        kpos = j * tk + lax.broadcasted_iota(jnp.int32, (tk, 1), 0)
        past = kpos < t
        lm = lm_ref[...]
        hs = [slice(h * HEAD_DIM, (h + 1) * HEAD_DIM) for h in range(SB_HEADS)]
        zs = [lax.dot_general(k_ref[:, sl], q_ref[:, sl], _NT, preferred_element_type=F32) for sl in hs]
        lgs, css = [], []
        for h in range(SB_HEADS):
            z = zs[h]
            sp = jnp.maximum(z, 0.0) + jnp.log2(1.0 + jnp.exp2(-jnp.abs(z)))
            lk = jnp.where(past, -sp, 0.0)
            hi = lk.astype(BF16)
            lo = (lk - hi.astype(F32)).astype(BF16)
            css.append(jnp.dot(lm, jnp.concatenate([hi, lo], axis=1), preferred_element_type=F32))
            lgs.append(z - sp)
            carry_old = carry_sc[h]
            carry_sc[h] = carry_old + jnp.sum(lk, axis=0, keepdims=True)
            css[h] = css[h][:, :tq] + css[h][:, tq:] + carry_old
        for h in range(SB_HEADS):
            wgt = jnp.where(past, jnp.exp2(lgs[h] + css[h]), 0.0)
            acc_sc[h] = acc_sc[h] + jnp.dot(vT_ref[hs[h], :], wgt.astype(BF16), preferred_element_type=F32)

    @pl.when((fl & 2) != 0)
    def _():
        for h in range(SB_HEADS):
            o_ref[:, h * HEAD_DIM:(h + 1) * HEAD_DIM] = acc_sc[h].T.astype(BF16)
            if not resume:
                acc_o[h * HEAD_DIM:(h + 1) * HEAD_DIM, :] = acc_sc[h]
                carry_o[h:h + 1, :] = carry_sc[h]
        if not resume:
            carry_o[SB_HEADS:, :] = jnp.zeros((8 - SB_HEADS, tq), F32)


def _sb_call(q, k, vT, steps, tq, tk, state=None):
    T, W = q.shape
    qi, kj, fl = steps
    lmat = jnp.asarray(np.triu(np.ones((tk, tk), np.float32), 1), BF16)
    resume = state is not None
    qtile = lambda shape: pl.BlockSpec(shape, lambda p, qi, kj, fl: (qi[p], 0))
    qtileT = lambda rows: pl.BlockSpec((rows, tq), lambda p, qi, kj, fl: (0, qi[p]))
    in_specs = [qtile((tq, W)),
                pl.BlockSpec((tk, W), lambda p, qi, kj, fl: (kj[p], 0)),
                pl.BlockSpec((W, tk), lambda p, qi, kj, fl: (0, kj[p])),
                pl.BlockSpec((tk, tk), lambda p, qi, kj, fl: (0, 0))]
    y_sds = jax.ShapeDtypeStruct((T, W), BF16)
    if resume:
        in_specs += [qtileT(W), qtileT(8)]
        out_specs, out_shape = qtile((tq, W)), y_sds
    else:
        out_specs = [qtile((tq, W)), qtileT(W), qtileT(8)]
        out_shape = [y_sds, jax.ShapeDtypeStruct((W, T), F32), jax.ShapeDtypeStruct((8, T), F32)]
    return pl.pallas_call(
        functools.partial(_sb_kernel, tq=tq, tk=tk, resume=resume),
        grid_spec=pltpu.PrefetchScalarGridSpec(
            num_scalar_prefetch=3, grid=(int(qi.shape[0]),),
            in_specs=in_specs, out_specs=out_specs,
            scratch_shapes=[pltpu.VMEM((SB_HEADS, 1, tq), F32), pltpu.VMEM((SB_HEADS, HEAD_DIM, tq), F32)]),
        out_shape=out_shape,
        compiler_params=_params(("arbitrary",), 48), name="sb_far" if resume else "sb_near",
    )(qi, kj, fl, q, k, vT, lmat, *(state or ()))


def _stick_breaking(q, k, vT):
    T, W = q.shape
    tq = _tile(T, 256)
    tk = _tile(T, 256)
    nq = T // tq
    hi = lambda i: ((i + 1) * tq - 2) // tk
    lo_near = lambda i: max(0, hi(i) - SB_NEAR_TILES + 1)
    near = _steps(nq, lo_near, hi, reverse=True)
    y_near, acc, carry = _sb_call(q, k, vT, near, tq, tk)
    qi, kj, fl = [], [], []
    for i in range(nq):
        js = list(range(lo_near(i) - 1, -1, -1))
        for n, j in enumerate(js or [0]):
            qi.append(i)
            kj.append(j)
            fl.append((1 if n == 0 else 0) | (2 if n == max(len(js), 1) - 1 else 0) | (0 if js else 4))
    far = tuple(jnp.asarray(np.array(a, np.int32)) for a in (qi, kj, fl))
    with_far = [i for i in range(nq) if lo_near(i) > 0]
    if not with_far:
        return y_near
    alive = jnp.max(carry[:SB_HEADS, with_far[0] * tq:]) > SB_DEAD_LOG2
    return lax.cond(alive, lambda: _sb_call(q, k, vT, far, tq, tk, state=(acc, carry)), lambda: y_near)


def _memattn_kernel(q_ref, k_ref, vT_ref, o_ref):
    for h in range(MEM_HEADS):
        sl = slice(h * HEAD_DIM, (h + 1) * HEAD_DIM)
        sT = lax.dot_general(k_ref[:, sl], q_ref[:, sl], _NT, preferred_element_type=F32)
        e = jnp.exp2(sT - jnp.max(sT, axis=0, keepdims=True))
        l = jnp.sum(e, axis=0, keepdims=True)
        oT = jnp.dot(vT_ref[sl, :], e.astype(BF16), preferred_element_type=F32) * (1.0 / l)
        o_ref[:, sl] = oT.T.astype(BF16)


def _mem_attention(q, k, vT):
    T, W = q.shape
    M = k.shape[0]
    tq = _tile(T, 512)
    return pl.pallas_call(
        _memattn_kernel, grid=(T // tq,),
        in_specs=[pl.BlockSpec((tq, W), lambda i: (i, 0)),
                  pl.BlockSpec((M, W), lambda i: (0, 0)),
                  pl.BlockSpec((W, M), lambda i: (0, 0))],
        out_specs=pl.BlockSpec((tq, W), lambda i: (i, 0)),
        out_shape=jax.ShapeDtypeStruct((T, W), BF16),
        compiler_params=_params(("parallel",), 32), name="mem_attention",
    )(q, k, vT)


def _mixout_kernel(yc_ref, ys_ref, yw_ref, ysb_ref, ym_ref, g_ref, wn_ref, wsb_ref, wm_ref, wo_ref, x_ref, gain_ref,
                   x2_o, h2_o):
    D = x_ref.shape[1]
    yn = (yc_ref[...].astype(F32) + ys_ref[...].astype(F32) + yw_ref[...].astype(F32)).astype(BF16)
    a = jnp.dot(yn, wn_ref[...], preferred_element_type=F32)
    b = jnp.dot(ysb_ref[...], wsb_ref[...], preferred_element_type=F32)
    c = jnp.dot(ym_ref[...], wm_ref[...], preferred_element_type=F32)
    sig = lambda n: jax.nn.sigmoid(g_ref[:, n * D:(n + 1) * D].astype(F32))
    mixed = (sig(0) * a + sig(1) * b + sig(2) * c).astype(BF16)
    x2 = x_ref[...] + jnp.dot(mixed, wo_ref[...], preferred_element_type=F32)
    x2_o[...] = x2
    h2_o[...] = _rms(x2, gain_ref[...]).astype(BF16)


def _mix_out(yc, ys, yw, ysb, ym, P, wn, wsb, wm, wo, x, gain):
    T, D = x.shape
    tm = _tile(T, 256)
    rowi = lambda w: pl.BlockSpec((tm, w), lambda i: (i, 0))
    res = lambda a: pl.BlockSpec(a.shape, lambda i: (0, 0), pipeline_mode=pl.Buffered(1))
    return pl.pallas_call(
        _mixout_kernel, grid=(T // tm,),
        in_specs=[rowi(yc.shape[1]), rowi(ys.shape[1]), rowi(yw.shape[1]), rowi(ysb.shape[1]), rowi(ym.shape[1]),
                  rowi(N_BRANCH * D), res(wn), res(wsb), res(wm), res(wo), rowi(D), res(gain)],
        out_specs=[rowi(D), rowi(D)],
        out_shape=[jax.ShapeDtypeStruct((T, D), F32), jax.ShapeDtypeStruct((T, D), BF16)],
        compiler_params=_params(("parallel",), 48), name="mix_out",
    )(yc, ys, yw, ysb, ym, P, wn, wsb, wm, wo, x, gain)


def _ffn_kernel(h_ref, wg_ref, wu_ref, wd_ref, x_ref, o_ref, z_sc, *, nf):
    f = pl.program_id(1)

    def up():
        a = jnp.dot(h_ref[...], wg_ref[...], preferred_element_type=F32)
        b = jnp.dot(h_ref[...], wu_ref[...], preferred_element_type=F32)
        return (a * jax.nn.sigmoid(a) * b).astype(BF16)

    def down():
        return jnp.dot(z_sc[...], wd_ref[...], preferred_element_type=F32)

    @pl.when(f == 0)
    def _():
        z_sc[...] = up()

    @pl.when((f > 0) & (f < nf))
    def _():
        c = down()
        z_new = up()
        o_ref[...] = jnp.where(f == 1, x_ref[...], o_ref[...]) + c
        z_sc[...] = z_new

    @pl.when(f == nf)
    def _():
        o_ref[...] += down()


def _ffn(h2, wg, wu, wd, x2):
    T, D = x2.shape
    F = wg.shape[1]
    tm = _tile(T, 1024)
    tf = 512
    assert F % tf == 0
    nf = F // tf
    assert nf >= 2
    return pl.pallas_call(
        functools.partial(_ffn_kernel, nf=nf), grid=(T // tm, nf + 1),
        in_specs=[pl.BlockSpec((tm, D), lambda i, f: (i, 0)),
                  pl.BlockSpec((D, tf), lambda i, f: (0, jnp.minimum(f, nf - 1))),
                  pl.BlockSpec((D, tf), lambda i, f: (0, jnp.minimum(f, nf - 1))),
                  pl.BlockSpec((tf, D), lambda i, f: (jnp.maximum(f - 1, 0), 0)),
                  pl.BlockSpec((tm, D), lambda i, f: (i, 0), pipeline_mode=pl.Buffered(1))],
        out_specs=pl.BlockSpec((tm, D), lambda i, f: (i, 0)),
        out_shape=jax.ShapeDtypeStruct((T, D), F32),
        scratch_shapes=[pltpu.VMEM((tm, tf), BF16)],
        compiler_params=_params(("parallel", "arbitrary"), 56), name="ffn",
    )(h2, wg.astype(BF16), wu.astype(BF16), wd.astype(BF16), x2)


def _layer(x, mem, pos_col, posc_col, consts, layer, attn_norm, w_in_all, nsa_q_norm, nsa_kc_norm, nsa_ks_norm, nsa_kw_norm,
           cmp_k_pe, cmp_k_w1, cmp_k_w2, cmp_v_pe, cmp_v_w1, cmp_v_w2, mem_norm, w_mem_kv,
           mem_q_norm, mem_k_norm, w_o_nsa, w_o_sb, w_o_mem, w_out, ffn_norm,
           w_ffn_gate, w_ffn_up, w_ffn_down):
    T, D = x.shape
    inv2, sgn, ovT = consts
    hd = HEAD_DIM
    row = lambda g: g.reshape(1, -1)

    q_w, kv_w, gn_w = NSA_HEADS * hd, 6 * NSA_GROUPS * hd, 3 * NSA_HEADS
    sb_w, mq_w, gm_w = 3 * SB_HEADS * hd, MEM_HEADS * hd, N_BRANCH * D
    o_gn = q_w + kv_w
    o_sb = o_gn + gn_w
    o_mq = o_sb + sb_w
    o_gm = o_mq + mq_w
    assert w_in_all.shape[2] == o_gm + gm_w
    segs = ((0, o_gm, gm_w), (gm_w, 0, o_gn), (gm_w + o_gn, o_sb, o_gm - o_sb))
    w_main, w_gate = _regroup_w_in(w_in_all, layer, segs, (o_gn, gn_w), gm_w + o_gn + o_gm - o_sb)
    P, Pg = _in_proj(x, row(attn_norm), w_main, w_gate)

    (qn, kc_raw, vc_raw, ksn, kwn, vsT, vwT, sbq, sbk, sbvT, memq, gT, qn2, kn2) = _prep(
        P, Pg, gm_w, pos_col, inv2, sgn, row(nsa_q_norm), row(nsa_ks_norm), row(nsa_kw_norm), row(mem_q_norm))

    q2 = jnp.max(qn2[::8].reshape(NSA_GROUPS, NSA_HPG, T), axis=1)
    k2 = jnp.max(kn2[::8], axis=1).reshape(2, NSA_GROUPS)
    bound = lambda kk: jnp.broadcast_to((1.02 * jnp.sqrt(q2 * kk[:, None]))[:, None, :], (NSA_GROUPS, 8, T))
    shift_sel, shift_win = bound(k2[0]), bound(k2[1])

    gates = gT[:gn_w].reshape(NSA_GROUPS, NSA_HPG, 3, T).transpose(2, 0, 1, 3)
    gates = jnp.pad(gates, ((0, 0), (0, 0), (0, 8 - NSA_HPG), (0, 0)))

    n_pad = T // CMP_STRIDE
    half = CMP_LEN // 2

    def w1_pack(w1):
        return jnp.concatenate([w1[:half].reshape(half * hd, -1), w1[half:].reshape(half * hd, -1)], axis=1).astype(BF16)

    def pe_pack(pe):
        return jnp.pad(pe.reshape(2, half * hd), ((0, 6), (0, 0))).astype(BF16)

    kc, vcT = _compress(
        kc_raw.reshape(NSA_GROUPS, n_pad, CMP_STRIDE * hd), vc_raw.reshape(NSA_GROUPS, n_pad, CMP_STRIDE * hd),
        w1_pack(cmp_k_w1), w1_pack(cmp_v_w1), pe_pack(cmp_k_pe), pe_pack(cmp_v_pe),
        cmp_k_w2.astype(BF16), cmp_v_w2.astype(BF16), row(nsa_kc_norm), posc_col, inv2, sgn)

    y_cmp, bias = _cmp_select(qn, kc, vcT, ovT, gates)
    y_sel = _nsa_flash("sel", qn, ksn, vsT, gates, shift_sel, bias)
    y_win = _nsa_flash("win", qn, kwn, vwT, gates, shift_win)
    y_sb = _stick_breaking(sbq, sbk, sbvT)

    mk, mvT = _memkv(mem, row(mem_norm), w_mem_kv.astype(BF16), row(mem_k_norm))
    y_mem = _mem_attention(memq, mk, mvT)

    x2, h2 = _mix_out(y_cmp, y_sel, y_win, y_sb, y_mem, P, w_o_nsa.astype(BF16), w_o_sb.astype(BF16),
                      w_o_mem.astype(BF16), w_out.astype(BF16), x, row(ffn_norm))
    return _ffn(h2, w_ffn_gate, w_ffn_up, w_ffn_down, x2)


def kernel(x, mem, positions, attn_norm, w_in, nsa_q_norm, nsa_kc_norm, nsa_ks_norm, nsa_kw_norm, cmp_k_pe, cmp_k_w1, cmp_k_w2, cmp_v_pe, cmp_v_w1, cmp_v_w2, mem_norm, w_mem_kv, mem_q_norm, mem_k_norm, w_o_nsa, w_o_sb, w_o_mem, w_out, ffn_norm, w_ffn_gate, w_ffn_up, w_ffn_down):
    B, T, D = x.shape
    assert T % (4 * LANE) == 0 and T // SEL_BLOCK >= 8
    n_pad = T // CMP_STRIDE
    n_sel = T // SEL_BLOCK
    inv = 1.0 / (ROPE_THETA ** (jnp.arange(0, HEAD_DIM, 2, dtype=F32) / HEAD_DIM))
    inv2 = jnp.concatenate([inv, inv]).reshape(1, HEAD_DIM)
    sgn = jnp.concatenate([-jnp.ones((HEAD_DIM // 2,), F32), jnp.ones((HEAD_DIM // 2,), F32)]).reshape(1, HEAD_DIM)
    cs = np.arange(n_pad)[None, :] * CMP_STRIDE
    ss = np.arange(n_sel)[:, None] * SEL_BLOCK
    ovT = jnp.asarray(((cs < ss + SEL_BLOCK) & (cs + CMP_LEN - 1 >= ss)).astype(np.float32), BF16)
    consts = (inv2, sgn, ovT)
    depth = w_in.shape[0]
    outs = []
    for b in range(B):
        xb = x[b]
        posf = positions[b].astype(F32)
        pos_col = posf.reshape(T, 1)
        posc = jnp.concatenate([posf[CMP_LEN - 1::CMP_STRIDE], posf[-1:]]).reshape(n_pad, 1)
        for l in range(depth):
            xb = _layer(xb, mem[b], pos_col, posc, consts, l, attn_norm[l], w_in, nsa_q_norm[l], nsa_kc_norm[l],
                        nsa_ks_norm[l], nsa_kw_norm[l], cmp_k_pe[l], cmp_k_w1[l], cmp_k_w2[l], cmp_v_pe[l],
                        cmp_v_w1[l], cmp_v_w2[l], mem_norm[l], w_mem_kv[l], mem_q_norm[l], mem_k_norm[l],
                        w_o_nsa[l], w_o_sb[l], w_o_mem[l], w_out[l], ffn_norm[l],
                        w_ffn_gate[l], w_ffn_up[l], w_ffn_down[l])
        outs.append(xb)
    return outs[0][None] if B == 1 else jnp.stack(outs, axis=0)
```

```python
import functools

import numpy as np
import jax
import jax.numpy as jnp
from jax import lax
from jax.experimental import pallas as pl
from jax.experimental.pallas import tpu as pltpu

HEAD_DIM = 128
NSA_HEADS = 8
NSA_GROUPS = 2
NSA_HPG = NSA_HEADS // NSA_GROUPS
SB_HEADS = 4
MEM_HEADS = 4
CMP_LEN = 32
CMP_STRIDE = 16
CMP_HIDDEN = 2 * HEAD_DIM
SEL_BLOCK = 64
SEL_TOPK = 16
WINDOW = 512
ROPE_THETA = 10000.0
NORM_EPS = 1e-6
NEG_BIG = -1e30
N_BRANCH = 3
SCALE = HEAD_DIM ** -0.5
LOG2E = 1.4426950408889634
QSCALE = SCALE * LOG2E
SOFTMAX_BOUND_MAX = 50.0
CMP_BUCKETS = 8
SB_NEAR_TILES = 2
SB_DEAD_LOG2 = -160.0
VAUG = HEAD_DIM + 16

LANE = 128
MIB = 1 << 20
BF16 = jnp.bfloat16
F32 = jnp.float32

_NT = (((1,), (1,)), ((), ()))


def _tile(n, pref):
    t = min(n, pref)
    assert n % t == 0, (n, pref)
    return t


def _params(sem, vmem_mib):
    return pltpu.CompilerParams(dimension_semantics=sem, vmem_limit_bytes=vmem_mib * MIB)


def _rms(x, gain):
    return x * lax.rsqrt(jnp.mean(x * x, axis=-1, keepdims=True) + NORM_EPS) * gain


def _rope_tables(pos, inv2, sgn):
    ang = pos * inv2
    return jnp.cos(ang), jnp.sin(ang) * sgn


def _rope(x, c, s):
    return x * c + pltpu.roll(x, HEAD_DIM // 2, 1) * s


def _proj_kernel(x_ref, g_ref, w_ref, wg_ref, o_ref, og_ref, hn_ref):
    @pl.when(pl.program_id(1) == 0)
    def _():
        hn_ref[...] = _rms(x_ref[...], g_ref[...]).astype(BF16)
        og_ref[...] = lax.dot_general(hn_ref[...], wg_ref[...], _NT, preferred_element_type=F32)

    o_ref[...] = lax.dot_general(hn_ref[...], w_ref[...], _NT, preferred_element_type=F32).astype(BF16)


def _in_proj(x, gain, wt_main, wt_gate):
    T, D = x.shape
    N = wt_main.shape[0]
    tm = _tile(T, 1024)
    tn = 1536
    assert N % tn == 0
    return pl.pallas_call(
        _proj_kernel,
        grid=(T // tm, N // tn),
        in_specs=[
            pl.BlockSpec((tm, D), lambda i, j: (i, 0)),
            pl.BlockSpec((1, D), lambda i, j: (0, 0)),
            pl.BlockSpec((tn, D), lambda i, j: (j, 0)),
            pl.BlockSpec((LANE, D), lambda i, j: (0, 0)),
        ],
        out_specs=[pl.BlockSpec((tm, tn), lambda i, j: (i, j)), pl.BlockSpec((tm, LANE), lambda i, j: (i, 0))],
        out_shape=[jax.ShapeDtypeStruct((T, N), BF16), jax.ShapeDtypeStruct((T, LANE), F32)],
        scratch_shapes=[pltpu.VMEM((tm, D), BF16)],
        compiler_params=_params(("parallel", "arbitrary"), 52),
        name="in_proj",
    )(x, gain, wt_main, wt_gate)


def _prep_kernel(pos_ref, inv_ref, sgn_ref, gq_ref, gks_ref, gkw_ref, gmq_ref,
                 q_ref, kc_ref, vc_ref, ks_ref, vs_ref, kw_ref, vw_ref,
                 sq_ref, sk_ref, sv_ref, mq_ref, gn_ref,
                 qn_o, kc_o, vc_o, ks_o, kw_o, vsT_o, vwT_o, sq_o, sk_o, svT_o, mq_o, gT_o, qn2_o, kn2_o):
    c, s = _rope_tables(pos_ref[...], inv_ref[...], sgn_ref[...])
    hd = HEAD_DIM
    tp = pos_ref.shape[0]
    f32 = lambda ref, sl: ref[:, sl].astype(F32)
    ones8 = jnp.ones((8, hd), F32)

    def sqnorm_rows(xb):
        x = xb.astype(F32)
        return lax.dot_general(ones8, x * x, _NT, preferred_element_type=F32)

    for h in range(NSA_HEADS):
        sl = slice(h * hd, (h + 1) * hd)
        qb = (_rope(_rms(f32(q_ref, sl), gq_ref[...]), c, s) * QSCALE).astype(BF16)
        qn_o[:, sl] = qb
        qn2_o[h * 8:(h + 1) * 8, :] = sqnorm_rows(qb)
    ones_rows = (lax.broadcasted_iota(jnp.int32, (VAUG - hd, tp), 0) == 0).astype(F32).astype(BF16)
    for g in range(NSA_GROUPS):
        sl = slice(g * hd, (g + 1) * hd)
        kc_o[g] = kc_ref[:, sl]
        vc_o[g] = vc_ref[:, sl]
        ksb = _rope(_rms(f32(ks_ref, sl), gks_ref[...]), c, s).astype(BF16)
        kwb = _rope(_rms(f32(kw_ref, sl), gkw_ref[...]), c, s).astype(BF16)
        ks_o[:, sl] = ksb
        kw_o[:, sl] = kwb
        kn2_o[g * 8:(g + 1) * 8, :] = sqnorm_rows(ksb)
        kn2_o[(NSA_GROUPS + g) * 8:(NSA_GROUPS + g + 1) * 8, :] = sqnorm_rows(kwb)
        vsT_o[g * VAUG:g * VAUG + hd, :] = f32(vs_ref, sl).T.astype(BF16)
        vsT_o[g * VAUG + hd:(g + 1) * VAUG, :] = ones_rows
        vwT_o[g * VAUG:g * VAUG + hd, :] = f32(vw_ref, sl).T.astype(BF16)
        vwT_o[g * VAUG + hd:(g + 1) * VAUG, :] = ones_rows
    for h in range(SB_HEADS):
        sl = slice(h * hd, (h + 1) * hd)
        sq_o[:, sl] = (f32(sq_ref, sl) * QSCALE).astype(BF16)
        svT_o[sl, :] = f32(sv_ref, sl).T.astype(BF16)
    sk_o[...] = sk_ref[...]
    for h in range(MEM_HEADS):
        sl = slice(h * hd, (h + 1) * hd)
        mq_o[:, sl] = (_rms(f32(mq_ref, sl), gmq_ref[...]) * QSCALE).astype(BF16)
    gT_o[...] = jax.nn.sigmoid(gn_ref[...]).T


def _prep(P, Pg, c0, pos_col, inv2, sgn, gq, gks, gkw, gmq):
    T = P.shape[0]
    tp = _tile(T, 512)
    hd = HEAD_DIM
    row = lambda w, c: pl.BlockSpec((tp, w), lambda i, c=c: (i, c))
    const = lambda: pl.BlockSpec((1, hd), lambda i: (0, 0))
    assert c0 % (8 * hd) == 0
    at = lambda w, col: row(w, (c0 + col) // w)
    in_specs = [pl.BlockSpec((tp, 1), lambda i: (i, 0)), const(), const(), const(), const(), const(), const(),
                at(8 * hd, 0),
                at(2 * hd, 8 * hd), at(2 * hd, 10 * hd),
                at(2 * hd, 12 * hd), at(2 * hd, 14 * hd),
                at(2 * hd, 16 * hd), at(2 * hd, 18 * hd),
                at(4 * hd, 20 * hd), at(4 * hd, 24 * hd), at(4 * hd, 28 * hd),
                at(4 * hd, 32 * hd),
                row(hd, 0)]
    colT = lambda w: pl.BlockSpec((w, tp), lambda i: (0, i))
    grp = lambda w: pl.BlockSpec((NSA_GROUPS, tp, w), lambda i: (0, i, 0))
    out_specs = [row(8 * hd, 0), grp(hd), grp(hd), row(2 * hd, 0),
                 row(2 * hd, 0), colT(NSA_GROUPS * VAUG), colT(NSA_GROUPS * VAUG),
                 row(4 * hd, 0), row(4 * hd, 0), colT(4 * hd), row(4 * hd, 0), colT(hd),
                 colT(NSA_HEADS * 8), colT(2 * NSA_GROUPS * 8)]
    sds = jax.ShapeDtypeStruct
    out_shape = [sds((T, 8 * hd), BF16),
                 sds((NSA_GROUPS, T, hd), BF16), sds((NSA_GROUPS, T, hd), BF16), sds((T, 2 * hd), BF16),
                 sds((T, 2 * hd), BF16),
                 sds((NSA_GROUPS * VAUG, T), BF16), sds((NSA_GROUPS * VAUG, T), BF16),
                 sds((T, 4 * hd), BF16), sds((T, 4 * hd), BF16), sds((4 * hd, T), BF16), sds((T, 4 * hd), BF16),
                 sds((hd, T), F32), sds((NSA_HEADS * 8, T), F32), sds((2 * NSA_GROUPS * 8, T), F32)]
    return pl.pallas_call(
        _prep_kernel, grid=(T // tp,), in_specs=in_specs, out_specs=out_specs, out_shape=out_shape,
        compiler_params=_params(("parallel",), 48), name="prep",
    )(pos_col, inv2, sgn, gq, gks, gkw, gmq, *([P] * 11), Pg)


def _gelu_tanh(x):
    return 0.5 * x * (1.0 + jnp.tanh(0.7978845608028654 * (x + 0.044715 * (x * x * x))))


def _compress_one(x, w1, pe, w2):
    n = x.shape[0]
    ab = jnp.dot(x, w1, preferred_element_type=F32)
    pr = jnp.dot(pe, w1, preferred_element_type=F32)
    pec = pr[0:1, :CMP_HIDDEN] + pr[1:2, CMP_HIDDEN:]
    hid = ab[:, :CMP_HIDDEN] + pltpu.roll(ab[:, CMP_HIDDEN:], n - 1, 0) + pec
    return jnp.dot(_gelu_tanh(hid).astype(BF16), w2, preferred_element_type=F32)


def _compress_kernel(xk_ref, xv_ref, w1k_ref, w1v_ref, pek_ref, pev_ref, w2k_ref, w2v_ref,
                     gk_ref, pos_ref, inv_ref, sgn_ref, kc_o, vcT_o):
    c, s = _rope_tables(pos_ref[...], inv_ref[...], sgn_ref[...])
    k = _compress_one(xk_ref[0], w1k_ref[...], pek_ref[...], w2k_ref[...])
    kc_o[0] = _rope(_rms(k, gk_ref[...]), c, s).astype(BF16)
    v = _compress_one(xv_ref[0], w1v_ref[...], pev_ref[...], w2v_ref[...])
    vcT_o[0] = v.T.astype(BF16)


def _compress(xk, xv, w1k, w1v, pek, pev, w2k, w2v, gk, posc, inv2, sgn):
    G, n, W = xk.shape
    full = lambda a: pl.BlockSpec(a.shape, lambda g: (0,) * a.ndim)
    grp = pl.BlockSpec((1, n, W), lambda g: (g, 0, 0))
    return pl.pallas_call(
        _compress_kernel, grid=(G,),
        in_specs=[grp, grp, full(w1k), full(w1v), full(pek), full(pev), full(w2k), full(w2v),
                  full(gk), full(posc), full(inv2), full(sgn)],
        out_specs=[pl.BlockSpec((1, n, HEAD_DIM), lambda g: (g, 0, 0)),
                   pl.BlockSpec((1, HEAD_DIM, n), lambda g: (g, 0, 0))],
        out_shape=[jax.ShapeDtypeStruct((G, n, HEAD_DIM), BF16), jax.ShapeDtypeStruct((G, HEAD_DIM, n), BF16)],
        compiler_params=_params(("parallel",), 48), name="compress",
    )(xk, xv, w1k, w1v, pek, pev, w2k, w2v, gk, posc, inv2, sgn)


def _memkv_kernel(mem_ref, gm_ref, w_ref, gk_ref, k_o, vT_o):
    hn = _rms(mem_ref[...], gm_ref[...]).astype(BF16)
    kv = jnp.dot(hn, w_ref[...], preferred_element_type=F32)
    mw = MEM_HEADS * HEAD_DIM
    for h in range(MEM_HEADS):
        sl = slice(h * HEAD_DIM, (h + 1) * HEAD_DIM)
        k_o[:, sl] = _rms(kv[:, sl], gk_ref[...]).astype(BF16)
        vT_o[sl, :] = kv[:, mw + h * HEAD_DIM: mw + (h + 1) * HEAD_DIM].T.astype(BF16)


def _memkv(mem, gm, w, gk):
    M = mem.shape[0]
    mw = MEM_HEADS * HEAD_DIM
    return pl.pallas_call(
        _memkv_kernel,
        out_shape=[jax.ShapeDtypeStruct((M, mw), BF16), jax.ShapeDtypeStruct((mw, M), BF16)],
        compiler_params=pltpu.CompilerParams(vmem_limit_bytes=32 * MIB), name="mem_kv",
    )(mem, gm, w, gk)


def _cmp_kernel(q_ref, kc_ref, vcT_ref, ovT_ref, g_ref, *rest, tq, i0, n_pad, n_sel, top_k):
    y_ref, b_ref = rest[-2:]
    i = i0 + pl.program_id(1)
    t = i * tq + lax.broadcasted_iota(jnp.int32, (1, tq), 1)
    n_end = lax.broadcasted_iota(jnp.int32, (n_pad, 1), 0) * CMP_STRIDE + (CMP_LEN - 1)
    valid = n_end <= t
    kc = kc_ref[0]
    vcT = vcT_ref[0]
    has_valid = (t >= CMP_LEN - 1).astype(F32)
    sms = []
    for h in range(NSA_HPG):
        q_h = q_ref[:, h * HEAD_DIM:(h + 1) * HEAD_DIM]
        sms.append(jnp.where(valid, lax.dot_general(kc, q_h, _NT, preferred_element_type=F32), NEG_BIG))
    psum = jnp.zeros((n_pad, tq), F32)
    for h in range(NSA_HPG):
        e = jnp.exp2(sms[h] - jnp.max(sms[h], axis=0, keepdims=True))
        p = e * (has_valid / jnp.sum(e, axis=0, keepdims=True))
        oT = jnp.dot(vcT, p.astype(BF16), preferred_element_type=F32)
        y_ref[:, h * HEAD_DIM:(h + 1) * HEAD_DIM] = (oT * g_ref[0, 0, h:h + 1, :]).T.astype(BF16)
        psum = psum + p
    hi = psum.astype(BF16)
    r1 = psum - hi.astype(F32)
    mid = r1.astype(BF16)
    lo = (r1 - mid.astype(F32)).astype(BF16)
    parts = jnp.dot(ovT_ref[...], jnp.concatenate([hi, mid, lo], axis=1), preferred_element_type=F32)
    imp = parts[:, :tq] + parts[:, tq:2 * tq] + parts[:, 2 * tq:]
    s_i = lax.broadcasted_iota(jnp.int32, (n_sel, 1), 0)
    cur = lax.shift_right_logical(t, 6)
    forced = (s_i == 0) | (s_i == cur) | (s_i == cur - 1)
    future = s_i * SEL_BLOCK > t
    w = jnp.where(forced, jnp.inf, jnp.where(future, -jnp.inf, imp))
    s_f = jnp.broadcast_to(s_i.astype(F32), (n_sel, tq))
    for _ in range(top_k):
        m = jnp.max(w, axis=0, keepdims=True)
        idx = jnp.min(jnp.where(w == m, s_f, float(n_sel)), axis=0, keepdims=True)
        w = jnp.where(s_f == idx, -jnp.inf, w)
    b_ref[0, :n_sel, :] = jnp.where(future, NEG_BIG, jnp.where(w == -jnp.inf, 0.0, NEG_BIG))
    n_all = b_ref.shape[1]
    if n_sel < n_all:
        b_ref[0, n_sel:, :] = jnp.full((n_all - n_sel, tq), NEG_BIG, F32)


def _round_up(n, m):
    return -(-n // m) * m


def _cmp_select(qn, kc, vcT, ovT, gates):
    T = qn.shape[0]
    G, n_pad, _ = kc.shape
    n_sel = T // SEL_BLOCK
    tq = _tile(T, 512)
    nq = T // tq
    gw = NSA_HPG * HEAD_DIM
    nb = min(CMP_BUCKETS, nq)
    assert nq % nb == 0
    per = nq // nb
    out_shape = [jax.ShapeDtypeStruct((T, NSA_HEADS * HEAD_DIM), BF16), jax.ShapeDtypeStruct((G, n_sel, T), F32)]
    buffers = ()
    for b in range(nb):
        i0 = b * per
        t_max = (i0 + per) * tq - 1
        n_len = min(n_pad, _round_up(max(t_max - (CMP_LEN - 1), 0) // CMP_STRIDE + 1, LANE))
        s_len = min(n_sel, _round_up(t_max // SEL_BLOCK + 1, 8))
        kern = functools.partial(_cmp_kernel, tq=tq, i0=i0, n_pad=n_len, n_sel=s_len, top_k=min(SEL_TOPK, n_sel))
        buffers = pl.pallas_call(
            kern, grid=(G, per),
            in_specs=[pl.BlockSpec((tq, gw), lambda g, i, i0=i0: (i0 + i, g)),
                      pl.BlockSpec((1, n_len, HEAD_DIM), lambda g, i: (g, 0, 0)),
                      pl.BlockSpec((1, HEAD_DIM, n_len), lambda g, i: (g, 0, 0)),
                      pl.BlockSpec((s_len, n_len), lambda g, i: (0, 0)),
                      pl.BlockSpec((1, 1, 8, tq), lambda g, i, i0=i0: (0, g, 0, i0 + i))]
            + [pl.BlockSpec(memory_space=pl.ANY)] * len(buffers),
            out_specs=[pl.BlockSpec((tq, gw), lambda g, i, i0=i0: (i0 + i, g)),
                       pl.BlockSpec((1, n_sel, tq), lambda g, i, i0=i0: (g, 0, i0 + i))],
            out_shape=out_shape,
            input_output_aliases={5: 0, 6: 1} if buffers else {},
            compiler_params=_params(("parallel", "parallel"), 48), name="cmp_select_%d" % b,
        )(qn, kc, vcT, ovT, gates, *buffers)
    return buffers


def _flash_kernel(qi_ref, kj_ref, fl_ref, q_ref, k_ref, vT_ref, g_ref, sh_ref, *rest, mode, bounded, tq, tk):
    if mode == "sel":
        bias_ref, o_ref, m_sc, acc_sc = rest
    else:
        o_ref, m_sc, acc_sc = rest
    p = pl.program_id(1)
    i = qi_ref[p]
    j = kj_ref[p]
    fl = fl_ref[p]

    @pl.when((fl & 1) != 0)
    def _():
        m_sc[...] = jnp.full(m_sc.shape, NEG_BIG, F32)
        acc_sc[...] = jnp.zeros(acc_sc.shape, F32)

    shift = sh_ref[0, 0:1, :] if bounded else 0.0
    nb = tk // SEL_BLOCK
    if mode == "sel":
        rows = bias_ref[0] - shift

    def key_mask():
        t = i * tq + lax.broadcasted_iota(jnp.int32, (1, tq), 1)
        kpos = j * tk + lax.broadcasted_iota(jnp.int32, (tk, 1), 0)
        if mode == "sel":
            base = jnp.concatenate(
                [jnp.broadcast_to(rows[b:b + 1, :], (SEL_BLOCK, tq)) for b in range(nb)], axis=0)
            return jnp.where(kpos <= t, base, NEG_BIG)
        return jnp.where((kpos <= t) & (kpos > t - WINDOW), 0.0 - shift, NEG_BIG)

    k = k_ref[...]
    vT = vT_ref[...]
    raw = lambda h: lax.dot_general(k, q_ref[:, h * HEAD_DIM:(h + 1) * HEAD_DIM], _NT, preferred_element_type=F32)

    def accumulate(pTs):
        for h in range(NSA_HPG):
            acc_sc[h] = acc_sc[h] + jnp.dot(vT, pTs[h], preferred_element_type=F32)

    if bounded and mode == "sel":
        @pl.when((fl & 4) != 0)
        def _():
            mask_add = key_mask()
            accumulate([jnp.exp2(raw(h) + mask_add).astype(BF16) for h in range(NSA_HPG)])

        @pl.when((fl & 4) == 0)
        def _():
            def probs(h):
                s = raw(h).reshape(nb, SEL_BLOCK, tq) + rows[:, None, :]
                return jnp.exp2(s).astype(BF16).reshape(tk, tq)
            accumulate([probs(h) for h in range(NSA_HPG)])
    elif bounded:
        mask_add = key_mask()
        accumulate([jnp.exp2(raw(h) + mask_add).astype(BF16) for h in range(NSA_HPG)])
    else:
        mask_add = key_mask()
        sTs = [raw(h) + mask_add for h in range(NSA_HPG)]
        pTs, alphas = [], []
        for h in range(NSA_HPG):
            sT = sTs[h]
            m_old = m_sc[h]
            m_new = jnp.maximum(m_old, jnp.max(sT, axis=0, keepdims=True))
            alphas.append(jnp.exp2(m_old - m_new))
            pTs.append(jnp.exp2((sT - m_new).astype(BF16)))
            m_sc[h] = m_new
        for h in range(NSA_HPG):
            acc_sc[h] = alphas[h] * acc_sc[h] + jnp.dot(vT, pTs[h], preferred_element_type=F32)

    @pl.when((fl & 2) != 0)
    def _():
        for h in range(NSA_HPG):
            acc = acc_sc[h]
            l = acc[HEAD_DIM:HEAD_DIM + 1, :]
            o = acc[:HEAD_DIM, :] * ((1.0 / l) * g_ref[0, 0, h:h + 1, :])
            o_ref[:, h * HEAD_DIM:(h + 1) * HEAD_DIM] = o.T.astype(BF16)


def _steps(nq, lo_fn, hi_fn, reverse=False):
    qi, kj, fl = [], [], []
    for i in range(nq):
        js = list(range(lo_fn(i), hi_fn(i) + 1))
        if reverse:
            js = js[::-1]
        for n, j in enumerate(js):
            qi.append(i)
            kj.append(j)
            fl.append((1 if n == 0 else 0) | (2 if n == len(js) - 1 else 0))
    return (jnp.asarray(np.array(qi, np.int32)), jnp.asarray(np.array(kj, np.int32)),
            jnp.asarray(np.array(fl, np.int32)))


def _nsa_flash(mode, qn, k, vT, gates, shift, bias=None):
    use_bounded = jnp.max(shift) <= SOFTMAX_BOUND_MAX
    return lax.cond(use_bounded,
                    lambda: _nsa_flash_call(mode, True, qn, k, vT, gates, shift, bias),
                    lambda: _nsa_flash_call(mode, False, qn, k, vT, gates, shift, bias))


def _nsa_flash_call(mode, bounded, qn, k, vT, gates, shift, bias):
    T = qn.shape[0]
    tq = _tile(T, 1024 if mode == "sel" else WINDOW)
    tk = _tile(T, 1024 if mode == "sel" else 512)
    nq = T // tq
    gw = NSA_HPG * HEAD_DIM
    hi = lambda i: ((i + 1) * tq - 1) // tk
    if mode == "sel":
        lo = lambda i: 0
        br = 1
    else:
        lo = lambda i: max(0, (i * tq - (WINDOW - 1)) // tk)
        br = 2
    qi, kj, fl = _steps(nq, lo, hi)
    fl = fl | 4 * ((kj + 1) * tk - 1 > qi * tq).astype(jnp.int32)
    in_specs = [pl.BlockSpec((tq, gw), lambda g, p, qi, kj, fl: (qi[p], g)),
                pl.BlockSpec((tk, HEAD_DIM), lambda g, p, qi, kj, fl: (kj[p], g)),
                pl.BlockSpec((VAUG, tk), lambda g, p, qi, kj, fl: (g, kj[p])),
                pl.BlockSpec((1, 1, 8, tq), lambda g, p, qi, kj, fl, br=br: (br, g, 0, qi[p])),
                pl.BlockSpec((1, 8, tq), lambda g, p, qi, kj, fl: (g, 0, qi[p]))]
    args = [qn, k, vT, gates, shift]
    if mode == "sel":
        in_specs.append(pl.BlockSpec((1, tk // SEL_BLOCK, tq), lambda g, p, qi, kj, fl: (g, kj[p], qi[p])))
        args.append(bias)
    kern = functools.partial(_flash_kernel, mode=mode, bounded=bounded, tq=tq, tk=tk)
    return pl.pallas_call(
        kern,
        grid_spec=pltpu.PrefetchScalarGridSpec(
            num_scalar_prefetch=3, grid=(NSA_GROUPS, int(qi.shape[0])),
            in_specs=in_specs,
            out_specs=pl.BlockSpec((tq, gw), lambda g, p, qi, kj, fl: (qi[p], g)),
            scratch_shapes=[pltpu.VMEM((NSA_HPG, 1, tq), F32), pltpu.VMEM((NSA_HPG, VAUG, tq), F32)]),
        out_shape=jax.ShapeDtypeStruct((T, NSA_HEADS * HEAD_DIM), BF16),
        compiler_params=_params(("parallel", "arbitrary"), 48),
        name="nsa_" + mode + ("_bounded" if bounded else "_online"),
    )(qi, kj, fl, *args)


def _sb_kernel(qi_ref, kj_ref, fl_ref, q_ref, k_ref, vT_ref, lm_ref, *rest, tq, tk, resume):
    if resume:
        acc_in, carry_in, o_ref, carry_sc, acc_sc = rest
    else:
        o_ref, acc_o, carry_o, carry_sc, acc_sc = rest
    p = pl.program_id(0)
    i = qi_ref[p]
    j = kj_ref[p]
    fl = fl_ref[p]

    @pl.when((fl & 1) != 0)
    def _():
        if resume:
            for h in range(SB_HEADS):
                carry_sc[h] = carry_in[h:h + 1, :]
                acc_sc[h] = acc_in[h * HEAD_DIM:(h + 1) * HEAD_DIM, :]
        else:
            carry_sc[...] = jnp.zeros(carry_sc.shape, F32)
            acc_sc[...] = jnp.zeros(acc_sc.shape, F32)

    @pl.when((jnp.max(carry_sc[...]) > SB_DEAD_LOG2) & ((fl & 4) == 0))
    def _():
        t = i * tq + lax.broadcasted_iota(jnp.int32, (1, tq), 1)
        kpos = j * tk + lax.broadcasted_iota(jnp.int32, (tk, 1), 0)
        past = kpos < t
        lm = lm_ref[...]
        hs = [slice(h * HEAD_DIM, (h + 1) * HEAD_DIM) for h in range(SB_HEADS)]
        zs = [lax.dot_general(k_ref[:, sl], q_ref[:, sl], _NT, preferred_element_type=F32) for sl in hs]
        lgs, css = [], []
        for h in range(SB_HEADS):
            z = zs[h]
            sp = jnp.maximum(z, 0.0) + jnp.log2(1.0 + jnp.exp2(-jnp.abs(z)))
            lk = jnp.where(past, -sp, 0.0)
            hi = lk.astype(BF16)
            lo = (lk - hi.astype(F32)).astype(BF16)
            css.append(jnp.dot(lm, jnp.concatenate([hi, lo], axis=1), preferred_element_type=F32))
            lgs.append(z - sp)
            carry_old = carry_sc[h]
            carry_sc[h] = carry_old + jnp.sum(lk, axis=0, keepdims=True)
            css[h] = css[h][:, :tq] + css[h][:, tq:] + carry_old
        for h in range(SB_HEADS):
            wgt = jnp.where(past, jnp.exp2(lgs[h] + css[h]), 0.0)
            acc_sc[h] = acc_sc[h] + jnp.dot(vT_ref[hs[h], :], wgt.astype(BF16), preferred_element_type=F32)

    @pl.when((fl & 2) != 0)
    def _():
        for h in range(SB_HEADS):
            o_ref[:, h * HEAD_DIM:(h + 1) * HEAD_DIM] = acc_sc[h].T.astype(BF16)
            if not resume:
                acc_o[h * HEAD_DIM:(h + 1) * HEAD_DIM, :] = acc_sc[h]
                carry_o[h:h + 1, :] = carry_sc[h]
        if not resume:
            carry_o[SB_HEADS:, :] = jnp.zeros((8 - SB_HEADS, tq), F32)


def _sb_call(q, k, vT, steps, tq, tk, state=None):
    T, W = q.shape
    qi, kj, fl = steps
    lmat = jnp.asarray(np.triu(np.ones((tk, tk), np.float32), 1), BF16)
    resume = state is not None
    qtile = lambda shape: pl.BlockSpec(shape, lambda p, qi, kj, fl: (qi[p], 0))
    qtileT = lambda rows: pl.BlockSpec((rows, tq), lambda p, qi, kj, fl: (0, qi[p]))
    in_specs = [qtile((tq, W)),
                pl.BlockSpec((tk, W), lambda p, qi, kj, fl: (kj[p], 0)),
                pl.BlockSpec((W, tk), lambda p, qi, kj, fl: (0, kj[p])),
                pl.BlockSpec((tk, tk), lambda p, qi, kj, fl: (0, 0))]
    y_sds = jax.ShapeDtypeStruct((T, W), BF16)
    if resume:
        in_specs += [qtileT(W), qtileT(8)]
        out_specs, out_shape = qtile((tq, W)), y_sds
    else:
        out_specs = [qtile((tq, W)), qtileT(W), qtileT(8)]
        out_shape = [y_sds, jax.ShapeDtypeStruct((W, T), F32), jax.ShapeDtypeStruct((8, T), F32)]
    return pl.pallas_call(
        functools.partial(_sb_kernel, tq=tq, tk=tk, resume=resume),
        grid_spec=pltpu.PrefetchScalarGridSpec(
            num_scalar_prefetch=3, grid=(int(qi.shape[0]),),
            in_specs=in_specs, out_specs=out_specs,
            scratch_shapes=[pltpu.VMEM((SB_HEADS, 1, tq), F32), pltpu.VMEM((SB_HEADS, HEAD_DIM, tq), F32)]),
        out_shape=out_shape,
        compiler_params=_params(("arbitrary",), 48), name="sb_far" if resume else "sb_near",
    )(qi, kj, fl, q, k, vT, lmat, *(state or ()))


def _stick_breaking(q, k, vT):
    T, W = q.shape
    tq = _tile(T, 256)
    tk = _tile(T, 256)
    nq = T // tq
    hi = lambda i: ((i + 1) * tq - 2) // tk
    lo_near = lambda i: max(0, hi(i) - SB_NEAR_TILES + 1)
    near = _steps(nq, lo_near, hi, reverse=True)
    y_near, acc, carry = _sb_call(q, k, vT, near, tq, tk)
    qi, kj, fl = [], [], []
    for i in range(nq):
        js = list(range(lo_near(i) - 1, -1, -1))
        for n, j in enumerate(js or [0]):
            qi.append(i)
            kj.append(j)
            fl.append((1 if n == 0 else 0) | (2 if n == max(len(js), 1) - 1 else 0) | (0 if js else 4))
    far = tuple(jnp.asarray(np.array(a, np.int32)) for a in (qi, kj, fl))
    with_far = [i for i in range(nq) if lo_near(i) > 0]
    if not with_far:
        return y_near
    alive = jnp.max(carry[:SB_HEADS, with_far[0] * tq:]) > SB_DEAD_LOG2
    return lax.cond(alive, lambda: _sb_call(q, k, vT, far, tq, tk, state=(acc, carry)), lambda: y_near)


def _memattn_kernel(q_ref, k_ref, vT_ref, o_ref):
    for h in range(MEM_HEADS):
        sl = slice(h * HEAD_DIM, (h + 1) * HEAD_DIM)
        sT = lax.dot_general(k_ref[:, sl], q_ref[:, sl], _NT, preferred_element_type=F32)
        e = jnp.exp2(sT - jnp.max(sT, axis=0, keepdims=True))
        l = jnp.sum(e, axis=0, keepdims=True)
        oT = jnp.dot(vT_ref[sl, :], e.astype(BF16), preferred_element_type=F32) * (1.0 / l)
        o_ref[:, sl] = oT.T.astype(BF16)


def _mem_attention(q, k, vT):
    T, W = q.shape
    M = k.shape[0]
    tq = _tile(T, 512)
    return pl.pallas_call(
        _memattn_kernel, grid=(T // tq,),
        in_specs=[pl.BlockSpec((tq, W), lambda i: (i, 0)),
                  pl.BlockSpec((M, W), lambda i: (0, 0)),
                  pl.BlockSpec((W, M), lambda i: (0, 0))],
        out_specs=pl.BlockSpec((tq, W), lambda i: (i, 0)),
        out_shape=jax.ShapeDtypeStruct((T, W), BF16),
        compiler_params=_params(("parallel",), 32), name="mem_attention",
    )(q, k, vT)


def _mixout_kernel(yc_ref, ys_ref, yw_ref, ysb_ref, ym_ref, g_ref, wn_ref, wsb_ref, wm_ref, wo_ref, x_ref, gain_ref,
                   x2_o, h2_o):
    D = x_ref.shape[1]
    yn = (yc_ref[...].astype(F32) + ys_ref[...].astype(F32) + yw_ref[...].astype(F32)).astype(BF16)
    a = jnp.dot(yn, wn_ref[...], preferred_element_type=F32)
    b = jnp.dot(ysb_ref[...], wsb_ref[...], preferred_element_type=F32)
    c = jnp.dot(ym_ref[...], wm_ref[...], preferred_element_type=F32)
    sig = lambda n: jax.nn.sigmoid(g_ref[:, n * D:(n + 1) * D].astype(F32))
    mixed = (sig(0) * a + sig(1) * b + sig(2) * c).astype(BF16)
    x2 = x_ref[...] + jnp.dot(mixed, wo_ref[...], preferred_element_type=F32)
    x2_o[...] = x2
    h2_o[...] = _rms(x2, gain_ref[...]).astype(BF16)


def _mix_out(yc, ys, yw, ysb, ym, P, wn, wsb, wm, wo, x, gain):
    T, D = x.shape
    tm = _tile(T, 256)
    rowi = lambda w: pl.BlockSpec((tm, w), lambda i: (i, 0))
    res = lambda a: pl.BlockSpec(a.shape, lambda i: (0, 0), pipeline_mode=pl.Buffered(1))
    return pl.pallas_call(
        _mixout_kernel, grid=(T // tm,),
        in_specs=[rowi(yc.shape[1]), rowi(ys.shape[1]), rowi(yw.shape[1]), rowi(ysb.shape[1]), rowi(ym.shape[1]),
                  rowi(N_BRANCH * D), res(wn), res(wsb), res(wm), res(wo), rowi(D), res(gain)],
        out_specs=[rowi(D), rowi(D)],
        out_shape=[jax.ShapeDtypeStruct((T, D), F32), jax.ShapeDtypeStruct((T, D), BF16)],
        compiler_params=_params(("parallel",), 48), name="mix_out",
    )(yc, ys, yw, ysb, ym, P, wn, wsb, wm, wo, x, gain)


def _ffn_kernel(h_ref, wg_ref, wu_ref, wd_ref, x_ref, o_ref, z_sc, *, nf):
    f = pl.program_id(1)

    def up():
        a = jnp.dot(h_ref[...], wg_ref[...], preferred_element_type=F32)
        b = jnp.dot(h_ref[...], wu_ref[...], preferred_element_type=F32)
        return (a * jax.nn.sigmoid(a) * b).astype(BF16)

    def down():
        return jnp.dot(z_sc[...], wd_ref[...], preferred_element_type=F32)

    @pl.when(f == 0)
    def _():
        z_sc[...] = up()

    @pl.when((f > 0) & (f < nf))
    def _():
        c = down()
        z_new = up()
        o_ref[...] = jnp.where(f == 1, x_ref[...], o_ref[...]) + c
        z_sc[...] = z_new

    @pl.when(f == nf)
    def _():
        o_ref[...] += down()


def _ffn(h2, wg, wu, wd, x2):
    T, D = x2.shape
    F = wg.shape[1]
    tm = _tile(T, 1024)
    tf = 512
    assert F % tf == 0
    nf = F // tf
    assert nf >= 2
    return pl.pallas_call(
        functools.partial(_ffn_kernel, nf=nf), grid=(T // tm, nf + 1),
        in_specs=[pl.BlockSpec((tm, D), lambda i, f: (i, 0)),
                  pl.BlockSpec((D, tf), lambda i, f: (0, jnp.minimum(f, nf - 1))),
                  pl.BlockSpec((D, tf), lambda i, f: (0, jnp.minimum(f, nf - 1))),
                  pl.BlockSpec((tf, D), lambda i, f: (jnp.maximum(f - 1, 0), 0)),
                  pl.BlockSpec((tm, D), lambda i, f: (i, 0), pipeline_mode=pl.Buffered(1))],
        out_specs=pl.BlockSpec((tm, D), lambda i, f: (i, 0)),
        out_shape=jax.ShapeDtypeStruct((T, D), F32),
        scratch_shapes=[pltpu.VMEM((tm, tf), BF16)],
        compiler_params=_params(("parallel", "arbitrary"), 56), name="ffn",
    )(h2, wg.astype(BF16), wu.astype(BF16), wd.astype(BF16), x2)


def _layer(x, mem, pos_col, posc_col, consts, layer, attn_norm, w_in_all, nsa_q_norm, nsa_kc_norm, nsa_ks_norm, nsa_kw_norm,
           cmp_k_pe, cmp_k_w1, cmp_k_w2, cmp_v_pe, cmp_v_w1, cmp_v_w2, mem_norm, w_mem_kv,
           mem_q_norm, mem_k_norm, w_o_nsa, w_o_sb, w_o_mem, w_out, ffn_norm,
           w_ffn_gate, w_ffn_up, w_ffn_down):
    T, D = x.shape
    inv2, sgn, ovT = consts
    hd = HEAD_DIM
    row = lambda g: g.reshape(1, -1)

    q_w, kv_w, gn_w = NSA_HEADS * hd, 6 * NSA_GROUPS * hd, 3 * NSA_HEADS
    sb_w, mq_w, gm_w = 3 * SB_HEADS * hd, MEM_HEADS * hd, N_BRANCH * D
    o_gn = q_w + kv_w
    o_sb = o_gn + gn_w
    o_mq = o_sb + sb_w
    o_gm = o_mq + mq_w
    assert w_in_all.shape[2] == o_gm + gm_w
    wt = jnp.swapaxes(w_in_all[layer], 0, 1)
    wt_main = jnp.concatenate([wt[o_gm:], wt[:o_gn], wt[o_sb:o_gm]], axis=0).astype(BF16)
    wt_gate = jnp.pad(wt[o_gn:o_sb], ((0, LANE - gn_w), (0, 0))).astype(BF16)
    P, Pg = _in_proj(x, row(attn_norm), wt_main, wt_gate)

    (qn, kc_raw, vc_raw, ksn, kwn, vsT, vwT, sbq, sbk, sbvT, memq, gT, qn2, kn2) = _prep(
        P, Pg, gm_w, pos_col, inv2, sgn, row(nsa_q_norm), row(nsa_ks_norm), row(nsa_kw_norm), row(mem_q_norm))

    q2 = jnp.max(qn2[::8].reshape(NSA_GROUPS, NSA_HPG, T), axis=1)
    k2 = jnp.max(kn2[::8], axis=1).reshape(2, NSA_GROUPS)
    bound = lambda kk: jnp.broadcast_to((1.02 * jnp.sqrt(q2 * kk[:, None]))[:, None, :], (NSA_GROUPS, 8, T))
    shift_sel, shift_win = bound(k2[0]), bound(k2[1])

    gates = gT[:gn_w].reshape(NSA_GROUPS, NSA_HPG, 3, T).transpose(2, 0, 1, 3)
    gates = jnp.pad(gates, ((0, 0), (0, 0), (0, 8 - NSA_HPG), (0, 0)))

    n_pad = T // CMP_STRIDE
    half = CMP_LEN // 2

    def w1_pack(w1):
        return jnp.concatenate([w1[:half].reshape(half * hd, -1), w1[half:].reshape(half * hd, -1)], axis=1).astype(BF16)

    def pe_pack(pe):
        return jnp.pad(pe.reshape(2, half * hd), ((0, 6), (0, 0))).astype(BF16)

    kc, vcT = _compress(
        kc_raw.reshape(NSA_GROUPS, n_pad, CMP_STRIDE * hd), vc_raw.reshape(NSA_GROUPS, n_pad, CMP_STRIDE * hd),
        w1_pack(cmp_k_w1), w1_pack(cmp_v_w1), pe_pack(cmp_k_pe), pe_pack(cmp_v_pe),
        cmp_k_w2.astype(BF16), cmp_v_w2.astype(BF16), row(nsa_kc_norm), posc_col, inv2, sgn)

    y_cmp, bias = _cmp_select(qn, kc, vcT, ovT, gates)
    y_sel = _nsa_flash("sel", qn, ksn, vsT, gates, shift_sel, bias)
    y_win = _nsa_flash("win", qn, kwn, vwT, gates, shift_win)
    y_sb = _stick_breaking(sbq, sbk, sbvT)

    mk, mvT = _memkv(mem, row(mem_norm), w_mem_kv.astype(BF16), row(mem_k_norm))
    y_mem = _mem_attention(memq, mk, mvT)

    x2, h2 = _mix_out(y_cmp, y_sel, y_win, y_sb, y_mem, P, w_o_nsa.astype(BF16), w_o_sb.astype(BF16),
                      w_o_mem.astype(BF16), w_out.astype(BF16), x, row(ffn_norm))
    return _ffn(h2, w_ffn_gate, w_ffn_up, w_ffn_down, x2)


def kernel(x, mem, positions, attn_norm, w_in, nsa_q_norm, nsa_kc_norm, nsa_ks_norm, nsa_kw_norm, cmp_k_pe, cmp_k_w1, cmp_k_w2, cmp_v_pe, cmp_v_w1, cmp_v_w2, mem_norm, w_mem_kv, mem_q_norm, mem_k_norm, w_o_nsa, w_o_sb, w_o_mem, w_out, ffn_norm, w_ffn_gate, w_ffn_up, w_ffn_down):
    B, T, D = x.shape
    assert T % (4 * LANE) == 0 and T // SEL_BLOCK >= 8
    n_pad = T // CMP_STRIDE
    n_sel = T // SEL_BLOCK
    inv = 1.0 / (ROPE_THETA ** (jnp.arange(0, HEAD_DIM, 2, dtype=F32) / HEAD_DIM))
    inv2 = jnp.concatenate([inv, inv]).reshape(1, HEAD_DIM)
    sgn = jnp.concatenate([-jnp.ones((HEAD_DIM // 2,), F32), jnp.ones((HEAD_DIM // 2,), F32)]).reshape(1, HEAD_DIM)
    cs = np.arange(n_pad)[None, :] * CMP_STRIDE
    ss = np.arange(n_sel)[:, None] * SEL_BLOCK
    ovT = jnp.asarray(((cs < ss + SEL_BLOCK) & (cs + CMP_LEN - 1 >= ss)).astype(np.float32), BF16)
    consts = (inv2, sgn, ovT)
    depth = w_in.shape[0]
    outs = []
    for b in range(B):
        xb = x[b]
        posf = positions[b].astype(F32)
        pos_col = posf.reshape(T, 1)
        posc = jnp.concatenate([posf[CMP_LEN - 1::CMP_STRIDE], posf[-1:]]).reshape(n_pad, 1)
        for l in range(depth):
            xb = _layer(xb, mem[b], pos_col, posc, consts, l, attn_norm[l], w_in, nsa_q_norm[l], nsa_kc_norm[l],
                        nsa_ks_norm[l], nsa_kw_norm[l], cmp_k_pe[l], cmp_k_w1[l], cmp_k_w2[l], cmp_v_pe[l],
                        cmp_v_w1[l], cmp_v_w2[l], mem_norm[l], w_mem_kv[l], mem_q_norm[l], mem_k_norm[l],
                        w_o_nsa[l], w_o_sb[l], w_o_mem[l], w_out[l], ffn_norm[l],
                        w_ffn_gate[l], w_ffn_up[l], w_ffn_down[l])
        outs.append(xb)
    return outs[0][None] if B == 1 else jnp.stack(outs, axis=0)
```

```python
import functools

import numpy as np
import jax
import jax.numpy as jnp
from jax import lax
from jax.experimental import pallas as pl
from jax.experimental.pallas import tpu as pltpu

HEAD_DIM = 128
NSA_HEADS = 8
NSA_GROUPS = 2
NSA_HPG = NSA_HEADS // NSA_GROUPS
SB_HEADS = 4
MEM_HEADS = 4
CMP_LEN = 32
CMP_STRIDE = 16
CMP_HIDDEN = 2 * HEAD_DIM
SEL_BLOCK = 64
SEL_TOPK = 16
WINDOW = 512
ROPE_THETA = 10000.0
NORM_EPS = 1e-6
NEG_BIG = -1e30
N_BRANCH = 3
SCALE = HEAD_DIM ** -0.5
LOG2E = 1.4426950408889634
QSCALE = SCALE * LOG2E
SOFTMAX_BOUND_MAX = 50.0
CMP_BUCKETS = 8
SB_NEAR_TILES = 2
SB_DEAD_LOG2 = -160.0
VAUG = HEAD_DIM + 16

LANE = 128
MIB = 1 << 20
BF16 = jnp.bfloat16
F32 = jnp.float32

_NT = (((1,), (1,)), ((), ()))


def _tile(n, pref):
    t = min(n, pref)
    assert n % t == 0, (n, pref)
    return t


def _params(sem, vmem_mib):
    return pltpu.CompilerParams(dimension_semantics=sem, vmem_limit_bytes=vmem_mib * MIB)


def _rms(x, gain):
    return x * lax.rsqrt(jnp.mean(x * x, axis=-1, keepdims=True) + NORM_EPS) * gain


def _rope_tables(pos, inv2, sgn):
    ang = pos * inv2
    return jnp.cos(ang), jnp.sin(ang) * sgn


def _rope(x, c, s):
    return x * c + pltpu.roll(x, HEAD_DIM // 2, 1) * s


def _proj_kernel(x_ref, g_ref, w_ref, wg_ref, o_ref, og_ref, hn_ref):
    @pl.when(pl.program_id(1) == 0)
    def _():
        hn_ref[...] = _rms(x_ref[...], g_ref[...]).astype(BF16)
        og_ref[...] = lax.dot_general(hn_ref[...], wg_ref[...], _NT, preferred_element_type=F32)

    o_ref[...] = lax.dot_general(hn_ref[...], w_ref[...], _NT, preferred_element_type=F32).astype(BF16)


def _in_proj(x, gain, wt_main, wt_gate):
    T, D = x.shape
    N = wt_main.shape[0]
    tm = _tile(T, 1024)
    tn = 1536
    assert N % tn == 0
    return pl.pallas_call(
        _proj_kernel,
        grid=(T // tm, N // tn),
        in_specs=[
            pl.BlockSpec((tm, D), lambda i, j: (i, 0)),
            pl.BlockSpec((1, D), lambda i, j: (0, 0)),
            pl.BlockSpec((tn, D), lambda i, j: (j, 0)),
            pl.BlockSpec((LANE, D), lambda i, j: (0, 0)),
        ],
        out_specs=[pl.BlockSpec((tm, tn), lambda i, j: (i, j)), pl.BlockSpec((tm, LANE), lambda i, j: (i, 0))],
        out_shape=[jax.ShapeDtypeStruct((T, N), BF16), jax.ShapeDtypeStruct((T, LANE), F32)],
        scratch_shapes=[pltpu.VMEM((tm, D), BF16)],
        compiler_params=_params(("parallel", "arbitrary"), 52),
        name="in_proj",
    )(x, gain, wt_main, wt_gate)


def _prep_kernel(pos_ref, inv_ref, sgn_ref, gq_ref, gks_ref, gkw_ref, gmq_ref,
                 q_ref, kc_ref, vc_ref, ks_ref, vs_ref, kw_ref, vw_ref,
                 sq_ref, sv_ref, mq_ref, gn_ref,
                 qn_o, kc_o, vc_o, ks_o, kw_o, vsT_o, vwT_o, sq_o, svT_o, mq_o, gT_o, qn2_o, kn2_o):
    c, s = _rope_tables(pos_ref[...], inv_ref[...], sgn_ref[...])
    hd = HEAD_DIM
    tp = pos_ref.shape[0]
    f32 = lambda ref, sl: ref[:, sl].astype(F32)
    ones8 = jnp.ones((8, hd), F32)

    def sqnorm_rows(xb):
        x = xb.astype(F32)
        return lax.dot_general(ones8, x * x, _NT, preferred_element_type=F32)

    for h in range(NSA_HEADS):
        sl = slice(h * hd, (h + 1) * hd)
        qb = (_rope(_rms(f32(q_ref, sl), gq_ref[...]), c, s) * QSCALE).astype(BF16)
        qn_o[:, sl] = qb
        qn2_o[h * 8:(h + 1) * 8, :] = sqnorm_rows(qb)
    ones_rows = (lax.broadcasted_iota(jnp.int32, (VAUG - hd, tp), 0) == 0).astype(F32).astype(BF16)
    for g in range(NSA_GROUPS):
        sl = slice(g * hd, (g + 1) * hd)
        kc_o[g] = kc_ref[:, sl]
        vc_o[g] = vc_ref[:, sl]
        ksb = _rope(_rms(f32(ks_ref, sl), gks_ref[...]), c, s).astype(BF16)
        kwb = _rope(_rms(f32(kw_ref, sl), gkw_ref[...]), c, s).astype(BF16)
        ks_o[:, sl] = ksb
        kw_o[:, sl] = kwb
        kn2_o[g * 8:(g + 1) * 8, :] = sqnorm_rows(ksb)
        kn2_o[(NSA_GROUPS + g) * 8:(NSA_GROUPS + g + 1) * 8, :] = sqnorm_rows(kwb)
        vsT_o[g * VAUG:g * VAUG + hd, :] = f32(vs_ref, sl).T.astype(BF16)
        vsT_o[g * VAUG + hd:(g + 1) * VAUG, :] = ones_rows
        vwT_o[g * VAUG:g * VAUG + hd, :] = f32(vw_ref, sl).T.astype(BF16)
        vwT_o[g * VAUG + hd:(g + 1) * VAUG, :] = ones_rows
    for h in range(SB_HEADS):
        sl = slice(h * hd, (h + 1) * hd)
        sq_o[:, sl] = (f32(sq_ref, sl) * QSCALE).astype(BF16)
        svT_o[sl, :] = f32(sv_ref, sl).T.astype(BF16)
    for h in range(MEM_HEADS):
        sl = slice(h * hd, (h + 1) * hd)
        mq_o[:, sl] = (_rms(f32(mq_ref, sl), gmq_ref[...]) * QSCALE).astype(BF16)
    gT_o[...] = jax.nn.sigmoid(gn_ref[...]).T


def _prep(P, Pg, c0, pos_col, inv2, sgn, gq, gks, gkw, gmq):
    T = P.shape[0]
    tp = _tile(T, 512)
    hd = HEAD_DIM
    row = lambda w, c: pl.BlockSpec((tp, w), lambda i, c=c: (i, c))
    const = lambda: pl.BlockSpec((1, hd), lambda i: (0, 0))
    assert c0 % (8 * hd) == 0
    at = lambda w, col: row(w, (c0 + col) // w)
    in_specs = [pl.BlockSpec((tp, 1), lambda i: (i, 0)), const(), const(), const(), const(), const(), const(),
                at(8 * hd, 0),
                at(2 * hd, 8 * hd), at(2 * hd, 10 * hd),
                at(2 * hd, 12 * hd), at(2 * hd, 14 * hd),
                at(2 * hd, 16 * hd), at(2 * hd, 18 * hd),
                at(4 * hd, 20 * hd), at(4 * hd, 28 * hd),
                at(4 * hd, 32 * hd),
                row(hd, 0)]
    colT = lambda w: pl.BlockSpec((w, tp), lambda i: (0, i))
    grp = lambda w: pl.BlockSpec((NSA_GROUPS, tp, w), lambda i: (0, i, 0))
    out_specs = [row(8 * hd, 0), grp(hd), grp(hd), row(2 * hd, 0),
                 row(2 * hd, 0), colT(NSA_GROUPS * VAUG), colT(NSA_GROUPS * VAUG),
                 row(4 * hd, 0), colT(4 * hd), row(4 * hd, 0), colT(hd),
                 colT(NSA_HEADS * 8), colT(2 * NSA_GROUPS * 8)]
    sds = jax.ShapeDtypeStruct
    out_shape = [sds((T, 8 * hd), BF16),
                 sds((NSA_GROUPS, T, hd), BF16), sds((NSA_GROUPS, T, hd), BF16), sds((T, 2 * hd), BF16),
                 sds((T, 2 * hd), BF16),
                 sds((NSA_GROUPS * VAUG, T), BF16), sds((NSA_GROUPS * VAUG, T), BF16),
                 sds((T, 4 * hd), BF16), sds((4 * hd, T), BF16), sds((T, 4 * hd), BF16),
                 sds((hd, T), F32), sds((NSA_HEADS * 8, T), F32), sds((2 * NSA_GROUPS * 8, T), F32)]
    return pl.pallas_call(
        _prep_kernel, grid=(T // tp,), in_specs=in_specs, out_specs=out_specs, out_shape=out_shape,
        compiler_params=_params(("parallel",), 48), name="prep",
    )(pos_col, inv2, sgn, gq, gks, gkw, gmq, *([P] * 10), Pg)


def _gelu_tanh(x):
    return 0.5 * x * (1.0 + jnp.tanh(0.7978845608028654 * (x + 0.044715 * (x * x * x))))


def _compress_one(x, w1, pe, w2):
    n = x.shape[0]
    ab = jnp.dot(x, w1, preferred_element_type=F32)
    pr = jnp.dot(pe, w1, preferred_element_type=F32)
    pec = pr[0:1, :CMP_HIDDEN] + pr[1:2, CMP_HIDDEN:]
    hid = ab[:, :CMP_HIDDEN] + pltpu.roll(ab[:, CMP_HIDDEN:], n - 1, 0) + pec
    return jnp.dot(_gelu_tanh(hid).astype(BF16), w2, preferred_element_type=F32)


def _compress_kernel(xk_ref, xv_ref, w1k_ref, w1v_ref, pek_ref, pev_ref, w2k_ref, w2v_ref,
                     gk_ref, pos_ref, inv_ref, sgn_ref, kc_o, vcT_o):
    c, s = _rope_tables(pos_ref[...], inv_ref[...], sgn_ref[...])
    k = _compress_one(xk_ref[0], w1k_ref[...], pek_ref[...], w2k_ref[...])
    kc_o[0] = _rope(_rms(k, gk_ref[...]), c, s).astype(BF16)
    v = _compress_one(xv_ref[0], w1v_ref[...], pev_ref[...], w2v_ref[...])
    vcT_o[0] = v.T.astype(BF16)


def _compress(xk, xv, w1k, w1v, pek, pev, w2k, w2v, gk, posc, inv2, sgn):
    G, n, W = xk.shape
    full = lambda a: pl.BlockSpec(a.shape, lambda g: (0,) * a.ndim)
    grp = pl.BlockSpec((1, n, W), lambda g: (g, 0, 0))
    return pl.pallas_call(
        _compress_kernel, grid=(G,),
        in_specs=[grp, grp, full(w1k), full(w1v), full(pek), full(pev), full(w2k), full(w2v),
                  full(gk), full(posc), full(inv2), full(sgn)],
        out_specs=[pl.BlockSpec((1, n, HEAD_DIM), lambda g: (g, 0, 0)),
                   pl.BlockSpec((1, HEAD_DIM, n), lambda g: (g, 0, 0))],
        out_shape=[jax.ShapeDtypeStruct((G, n, HEAD_DIM), BF16), jax.ShapeDtypeStruct((G, HEAD_DIM, n), BF16)],
        compiler_params=_params(("parallel",), 48), name="compress",
    )(xk, xv, w1k, w1v, pek, pev, w2k, w2v, gk, posc, inv2, sgn)


def _memkv_kernel(mem_ref, gm_ref, w_ref, gk_ref, k_o, vT_o):
    hn = _rms(mem_ref[...], gm_ref[...]).astype(BF16)
    kv = jnp.dot(hn, w_ref[...], preferred_element_type=F32)
    mw = MEM_HEADS * HEAD_DIM
    for h in range(MEM_HEADS):
        sl = slice(h * HEAD_DIM, (h + 1) * HEAD_DIM)
        k_o[:, sl] = _rms(kv[:, sl], gk_ref[...]).astype(BF16)
        vT_o[sl, :] = kv[:, mw + h * HEAD_DIM: mw + (h + 1) * HEAD_DIM].T.astype(BF16)


def _memkv(mem, gm, w, gk):
    M = mem.shape[0]
    mw = MEM_HEADS * HEAD_DIM
    return pl.pallas_call(
        _memkv_kernel,
        out_shape=[jax.ShapeDtypeStruct((M, mw), BF16), jax.ShapeDtypeStruct((mw, M), BF16)],
        compiler_params=pltpu.CompilerParams(vmem_limit_bytes=32 * MIB), name="mem_kv",
    )(mem, gm, w, gk)


def _cmp_kernel(q_ref, kc_ref, vcT_ref, ovT_ref, g_ref, *rest, tq, i0, n_pad, n_sel, top_k):
    y_ref, b_ref = rest[-2:]
    i = i0 + pl.program_id(1)
    t = i * tq + lax.broadcasted_iota(jnp.int32, (1, tq), 1)
    n_end = lax.broadcasted_iota(jnp.int32, (n_pad, 1), 0) * CMP_STRIDE + (CMP_LEN - 1)
    valid = n_end <= t
    kc = kc_ref[0]
    vcT = vcT_ref[0]
    has_valid = (t >= CMP_LEN - 1).astype(F32)
    sms = []
    for h in range(NSA_HPG):
        q_h = q_ref[:, h * HEAD_DIM:(h + 1) * HEAD_DIM]
        sms.append(jnp.where(valid, lax.dot_general(kc, q_h, _NT, preferred_element_type=F32), NEG_BIG))
    psum = jnp.zeros((n_pad, tq), F32)
    for h in range(NSA_HPG):
        e = jnp.exp2(sms[h] - jnp.max(sms[h], axis=0, keepdims=True))
        p = e * (has_valid / jnp.sum(e, axis=0, keepdims=True))
        oT = jnp.dot(vcT, p.astype(BF16), preferred_element_type=F32)
        y_ref[:, h * HEAD_DIM:(h + 1) * HEAD_DIM] = (oT * g_ref[0, 0, h:h + 1, :]).T.astype(BF16)
        psum = psum + p
    hi = psum.astype(BF16)
    r1 = psum - hi.astype(F32)
    mid = r1.astype(BF16)
    lo = (r1 - mid.astype(F32)).astype(BF16)
    parts = jnp.dot(ovT_ref[...], jnp.concatenate([hi, mid, lo], axis=1), preferred_element_type=F32)
    imp = parts[:, :tq] + parts[:, tq:2 * tq] + parts[:, 2 * tq:]
    s_i = lax.broadcasted_iota(jnp.int32, (n_sel, 1), 0)
    cur = lax.shift_right_logical(t, 6)
    forced = (s_i == 0) | (s_i == cur) | (s_i == cur - 1)
    future = s_i * SEL_BLOCK > t
    w = jnp.where(forced, jnp.inf, jnp.where(future, -jnp.inf, imp))
    s_f = jnp.broadcast_to(s_i.astype(F32), (n_sel, tq))
    for _ in range(top_k):
        m = jnp.max(w, axis=0, keepdims=True)
        idx = jnp.min(jnp.where(w == m, s_f, float(n_sel)), axis=0, keepdims=True)
        w = jnp.where(s_f == idx, -jnp.inf, w)
    b_ref[0, :n_sel, :] = jnp.where(future, NEG_BIG, jnp.where(w == -jnp.inf, 0.0, NEG_BIG))
    n_all = b_ref.shape[1]
    if n_sel < n_all:
        b_ref[0, n_sel:, :] = jnp.full((n_all - n_sel, tq), NEG_BIG, F32)


def _round_up(n, m):
    return -(-n // m) * m


def _cmp_select(qn, kc, vcT, ovT, gates):
    T = qn.shape[0]
    G, n_pad, _ = kc.shape
    n_sel = T // SEL_BLOCK
    tq = _tile(T, 512)
    nq = T // tq
    gw = NSA_HPG * HEAD_DIM
    nb = min(CMP_BUCKETS, nq)
    assert nq % nb == 0
    per = nq // nb
    out_shape = [jax.ShapeDtypeStruct((T, NSA_HEADS * HEAD_DIM), BF16), jax.ShapeDtypeStruct((G, n_sel, T), F32)]
    buffers = ()
    for b in range(nb):
        i0 = b * per
        t_max = (i0 + per) * tq - 1
        n_len = min(n_pad, _round_up(max(t_max - (CMP_LEN - 1), 0) // CMP_STRIDE + 1, LANE))
        s_len = min(n_sel, _round_up(t_max // SEL_BLOCK + 1, 8))
        kern = functools.partial(_cmp_kernel, tq=tq, i0=i0, n_pad=n_len, n_sel=s_len, top_k=min(SEL_TOPK, n_sel))
        buffers = pl.pallas_call(
            kern, grid=(G, per),
            in_specs=[pl.BlockSpec((tq, gw), lambda g, i, i0=i0: (i0 + i, g)),
                      pl.BlockSpec((1, n_len, HEAD_DIM), lambda g, i: (g, 0, 0)),
                      pl.BlockSpec((1, HEAD_DIM, n_len), lambda g, i: (g, 0, 0)),
                      pl.BlockSpec((s_len, n_len), lambda g, i: (0, 0)),
                      pl.BlockSpec((1, 1, 8, tq), lambda g, i, i0=i0: (0, g, 0, i0 + i))]
            + [pl.BlockSpec(memory_space=pl.ANY)] * len(buffers),
            out_specs=[pl.BlockSpec((tq, gw), lambda g, i, i0=i0: (i0 + i, g)),
                       pl.BlockSpec((1, n_sel, tq), lambda g, i, i0=i0: (g, 0, i0 + i))],
            out_shape=out_shape,
            input_output_aliases={5: 0, 6: 1} if buffers else {},
            compiler_params=_params(("parallel", "parallel"), 48), name="cmp_select_%d" % b,
        )(qn, kc, vcT, ovT, gates, *buffers)
    return buffers


def _flash_kernel(qi_ref, kj_ref, fl_ref, q_ref, k_ref, vT_ref, g_ref, sh_ref, *rest, mode, bounded, tq, tk):
    if mode == "sel":
        bias_ref, o_ref, m_sc, acc_sc = rest
    else:
        o_ref, m_sc, acc_sc = rest
    p = pl.program_id(1)
    i = qi_ref[p]
    j = kj_ref[p]
    fl = fl_ref[p]

    @pl.when((fl & 1) != 0)
    def _():
        m_sc[...] = jnp.full(m_sc.shape, NEG_BIG, F32)
        acc_sc[...] = jnp.zeros(acc_sc.shape, F32)

    shift = sh_ref[0, 0:1, :] if bounded else 0.0
    nb = tk // SEL_BLOCK
    if mode == "sel":
        rows = bias_ref[0] - shift

    def key_mask():
        t = i * tq + lax.broadcasted_iota(jnp.int32, (1, tq), 1)
        kpos = j * tk + lax.broadcasted_iota(jnp.int32, (tk, 1), 0)
        if mode == "sel":
            base = jnp.concatenate(
                [jnp.broadcast_to(rows[b:b + 1, :], (SEL_BLOCK, tq)) for b in range(nb)], axis=0)
            return jnp.where(kpos <= t, base, NEG_BIG)
        return jnp.where((kpos <= t) & (kpos > t - WINDOW), 0.0 - shift, NEG_BIG)

    k = k_ref[...]
    vT = vT_ref[...]
    raw = lambda h: lax.dot_general(k, q_ref[:, h * HEAD_DIM:(h + 1) * HEAD_DIM], _NT, preferred_element_type=F32)

    def accumulate(pTs):
        for h in range(NSA_HPG):
            acc_sc[h] = acc_sc[h] + jnp.dot(vT, pTs[h], preferred_element_type=F32)

    if bounded and mode == "sel":
        @pl.when((fl & 4) != 0)
        def _():
            mask_add = key_mask()
            accumulate([jnp.exp2(raw(h) + mask_add).astype(BF16) for h in range(NSA_HPG)])

        @pl.when((fl & 4) == 0)
        def _():
            def probs(h):
                s = raw(h).reshape(nb, SEL_BLOCK, tq) + rows[:, None, :]
                return jnp.exp2(s).astype(BF16).reshape(tk, tq)
            accumulate([probs(h) for h in range(NSA_HPG)])
    elif bounded:
        mask_add = key_mask()
        accumulate([jnp.exp2(raw(h) + mask_add).astype(BF16) for h in range(NSA_HPG)])
    else:
        mask_add = key_mask()
        sTs = [raw(h) + mask_add for h in range(NSA_HPG)]
        pTs, alphas = [], []
        for h in range(NSA_HPG):
            sT = sTs[h]
            m_old = m_sc[h]
            m_new = jnp.maximum(m_old, jnp.max(sT, axis=0, keepdims=True))
            alphas.append(jnp.exp2(m_old - m_new))
            pTs.append(jnp.exp2((sT - m_new).astype(BF16)))
            m_sc[h] = m_new
        for h in range(NSA_HPG):
            acc_sc[h] = alphas[h] * acc_sc[h] + jnp.dot(vT, pTs[h], preferred_element_type=F32)

    @pl.when((fl & 2) != 0)
    def _():
        for h in range(NSA_HPG):
            acc = acc_sc[h]
            l = acc[HEAD_DIM:HEAD_DIM + 1, :]
            o = acc[:HEAD_DIM, :] * ((1.0 / l) * g_ref[0, 0, h:h + 1, :])
            o_ref[:, h * HEAD_DIM:(h + 1) * HEAD_DIM] = o.T.astype(BF16)


def _steps(nq, lo_fn, hi_fn, reverse=False):
    qi, kj, fl = [], [], []
    for i in range(nq):
        js = list(range(lo_fn(i), hi_fn(i) + 1))
        if reverse:
            js = js[::-1]
        for n, j in enumerate(js):
            qi.append(i)
            kj.append(j)
            fl.append((1 if n == 0 else 0) | (2 if n == len(js) - 1 else 0))
    return (jnp.asarray(np.array(qi, np.int32)), jnp.asarray(np.array(kj, np.int32)),
            jnp.asarray(np.array(fl, np.int32)))


def _nsa_flash(mode, qn, k, vT, gates, shift, bias=None):
    use_bounded = jnp.max(shift) <= SOFTMAX_BOUND_MAX
    return lax.cond(use_bounded,
                    lambda: _nsa_flash_call(mode, True, qn, k, vT, gates, shift, bias),
                    lambda: _nsa_flash_call(mode, False, qn, k, vT, gates, shift, bias))


def _nsa_flash_call(mode, bounded, qn, k, vT, gates, shift, bias):
    T = qn.shape[0]
    tq = _tile(T, 1024 if mode == "sel" else WINDOW)
    tk = _tile(T, 1024 if mode == "sel" else 512)
    nq = T // tq
    gw = NSA_HPG * HEAD_DIM
    hi = lambda i: ((i + 1) * tq - 1) // tk
    if mode == "sel":
        lo = lambda i: 0
        br = 1
    else:
        lo = lambda i: max(0, (i * tq - (WINDOW - 1)) // tk)
        br = 2
    qi, kj, fl = _steps(nq, lo, hi)
    fl = fl | 4 * ((kj + 1) * tk - 1 > qi * tq).astype(jnp.int32)
    in_specs = [pl.BlockSpec((tq, gw), lambda g, p, qi, kj, fl: (qi[p], g)),
                pl.BlockSpec((tk, HEAD_DIM), lambda g, p, qi, kj, fl: (kj[p], g)),
                pl.BlockSpec((VAUG, tk), lambda g, p, qi, kj, fl: (g, kj[p])),
                pl.BlockSpec((1, 1, 8, tq), lambda g, p, qi, kj, fl, br=br: (br, g, 0, qi[p])),
                pl.BlockSpec((1, 8, tq), lambda g, p, qi, kj, fl: (g, 0, qi[p]))]
    args = [qn, k, vT, gates, shift]
    if mode == "sel":
        in_specs.append(pl.BlockSpec((1, tk // SEL_BLOCK, tq), lambda g, p, qi, kj, fl: (g, kj[p], qi[p])))
        args.append(bias)
    kern = functools.partial(_flash_kernel, mode=mode, bounded=bounded, tq=tq, tk=tk)
    return pl.pallas_call(
        kern,
        grid_spec=pltpu.PrefetchScalarGridSpec(
            num_scalar_prefetch=3, grid=(NSA_GROUPS, int(qi.shape[0])),
            in_specs=in_specs,
            out_specs=pl.BlockSpec((tq, gw), lambda g, p, qi, kj, fl: (qi[p], g)),
            scratch_shapes=[pltpu.VMEM((NSA_HPG, 1, tq), F32), pltpu.VMEM((NSA_HPG, VAUG, tq), F32)]),
        out_shape=jax.ShapeDtypeStruct((T, NSA_HEADS * HEAD_DIM), BF16),
        compiler_params=_params(("parallel", "arbitrary"), 48),
        name="nsa_" + mode + ("_bounded" if bounded else "_online"),
    )(qi, kj, fl, *args)


def _sb_kernel(qi_ref, kj_ref, fl_ref, q_ref, k_ref, vT_ref, lm_ref, *rest, tq, tk, resume):
    if resume:
        acc_in, carry_in, o_ref, carry_sc, acc_sc = rest
    else:
        o_ref, acc_o, carry_o, carry_sc, acc_sc = rest
    p = pl.program_id(0)
    i = qi_ref[p]
    j = kj_ref[p]
    fl = fl_ref[p]

    @pl.when((fl & 1) != 0)
    def _():
        if resume:
            for h in range(SB_HEADS):
                carry_sc[h] = carry_in[h:h + 1, :]
                acc_sc[h] = acc_in[h * HEAD_DIM:(h + 1) * HEAD_DIM, :]
        else:
            carry_sc[...] = jnp.zeros(carry_sc.shape, F32)
            acc_sc[...] = jnp.zeros(acc_sc.shape, F32)

    @pl.when((jnp.max(carry_sc[...]) > SB_DEAD_LOG2) & ((fl & 4) == 0))
    def _():
        t = i * tq + lax.broadcasted_iota(jnp.int32, (1, tq), 1)
        kpos = j * tk + lax.broadcasted_iota(jnp.int32, (tk, 1), 0)
        past = kpos < t
        lm = lm_ref[...]
        hs = [slice(h * HEAD_DIM, (h + 1) * HEAD_DIM) for h in range(SB_HEADS)]
        zs = [lax.dot_general(k_ref[:, sl], q_ref[:, sl], _NT, preferred_element_type=F32) for sl in hs]
        lgs, css = [], []
        for h in range(SB_HEADS):
            z = zs[h]
            sp = jnp.maximum(z, 0.0) + jnp.log2(1.0 + jnp.exp2(-jnp.abs(z)))
            lk = jnp.where(past, -sp, 0.0)
            hi = lk.astype(BF16)
            lo = (lk - hi.astype(F32)).astype(BF16)
            css.append(jnp.dot(lm, jnp.concatenate([hi, lo], axis=1), preferred_element_type=F32))
            lgs.append(z - sp)
            carry_old = carry_sc[h]
            carry_sc[h] = carry_old + jnp.sum(lk, axis=0, keepdims=True)
            css[h] = css[h][:, :tq] + css[h][:, tq:] + carry_old
        for h in range(SB_HEADS):
            wgt = jnp.where(past, jnp.exp2(lgs[h] + css[h]), 0.0)
            acc_sc[h] = acc_sc[h] + jnp.dot(vT_ref[hs[h], :], wgt.astype(BF16), preferred_element_type=F32)

    @pl.when((fl & 2) != 0)
    def _():
        for h in range(SB_HEADS):
            o_ref[:, h * HEAD_DIM:(h + 1) * HEAD_DIM] = acc_sc[h].T.astype(BF16)
            if not resume:
                acc_o[h * HEAD_DIM:(h + 1) * HEAD_DIM, :] = acc_sc[h]
                carry_o[h:h + 1, :] = carry_sc[h]
        if not resume:
            carry_o[SB_HEADS:, :] = jnp.zeros((8 - SB_HEADS, tq), F32)


def _sb_call(q, k, kcol, vT, steps, tq, tk, state=None):
    T, W = q.shape
    qi, kj, fl = steps
    lmat = jnp.asarray(np.triu(np.ones((tk, tk), np.float32), 1), BF16)
    resume = state is not None
    qtile = lambda shape: pl.BlockSpec(shape, lambda p, qi, kj, fl: (qi[p], 0))
    qtileT = lambda rows: pl.BlockSpec((rows, tq), lambda p, qi, kj, fl: (0, qi[p]))
    in_specs = [qtile((tq, W)),
                pl.BlockSpec((tk, W), lambda p, qi, kj, fl: (kj[p], kcol)),
                pl.BlockSpec((W, tk), lambda p, qi, kj, fl: (0, kj[p])),
                pl.BlockSpec((tk, tk), lambda p, qi, kj, fl: (0, 0))]
    y_sds = jax.ShapeDtypeStruct((T, W), BF16)
    if resume:
        in_specs += [qtileT(W), qtileT(8)]
        out_specs, out_shape = qtile((tq, W)), y_sds
    else:
        out_specs = [qtile((tq, W)), qtileT(W), qtileT(8)]
        out_shape = [y_sds, jax.ShapeDtypeStruct((W, T), F32), jax.ShapeDtypeStruct((8, T), F32)]
    return pl.pallas_call(
        functools.partial(_sb_kernel, tq=tq, tk=tk, resume=resume),
        grid_spec=pltpu.PrefetchScalarGridSpec(
            num_scalar_prefetch=3, grid=(int(qi.shape[0]),),
            in_specs=in_specs, out_specs=out_specs,
            scratch_shapes=[pltpu.VMEM((SB_HEADS, 1, tq), F32), pltpu.VMEM((SB_HEADS, HEAD_DIM, tq), F32)]),
        out_shape=out_shape,
        compiler_params=_params(("arbitrary",), 48), name="sb_far" if resume else "sb_near",
    )(qi, kj, fl, q, k, vT, lmat, *(state or ()))


def _stick_breaking(q, k, kcol, vT):
    T, W = q.shape
    tq = _tile(T, 256)
    tk = _tile(T, 256)
    nq = T // tq
    hi = lambda i: ((i + 1) * tq - 2) // tk
    lo_near = lambda i: max(0, hi(i) - SB_NEAR_TILES + 1)
    near = _steps(nq, lo_near, hi, reverse=True)
    y_near, acc, carry = _sb_call(q, k, kcol, vT, near, tq, tk)
    qi, kj, fl = [], [], []
    for i in range(nq):
        js = list(range(lo_near(i) - 1, -1, -1))
        for n, j in enumerate(js or [0]):
            qi.append(i)
            kj.append(j)
            fl.append((1 if n == 0 else 0) | (2 if n == max(len(js), 1) - 1 else 0) | (0 if js else 4))
    far = tuple(jnp.asarray(np.array(a, np.int32)) for a in (qi, kj, fl))
    with_far = [i for i in range(nq) if lo_near(i) > 0]
    if not with_far:
        return y_near
    alive = jnp.max(carry[:SB_HEADS, with_far[0] * tq:]) > SB_DEAD_LOG2
    return lax.cond(alive, lambda: _sb_call(q, k, kcol, vT, far, tq, tk, state=(acc, carry)), lambda: y_near)


def _memattn_kernel(q_ref, k_ref, vT_ref, o_ref):
    for h in range(MEM_HEADS):
        sl = slice(h * HEAD_DIM, (h + 1) * HEAD_DIM)
        sT = lax.dot_general(k_ref[:, sl], q_ref[:, sl], _NT, preferred_element_type=F32)
        e = jnp.exp2(sT - jnp.max(sT, axis=0, keepdims=True))
        l = jnp.sum(e, axis=0, keepdims=True)
        oT = jnp.dot(vT_ref[sl, :], e.astype(BF16), preferred_element_type=F32) * (1.0 / l)
        o_ref[:, sl] = oT.T.astype(BF16)


def _mem_attention(q, k, vT):
    T, W = q.shape
    M = k.shape[0]
    tq = _tile(T, 512)
    return pl.pallas_call(
        _memattn_kernel, grid=(T // tq,),
        in_specs=[pl.BlockSpec((tq, W), lambda i: (i, 0)),
                  pl.BlockSpec((M, W), lambda i: (0, 0)),
                  pl.BlockSpec((W, M), lambda i: (0, 0))],
        out_specs=pl.BlockSpec((tq, W), lambda i: (i, 0)),
        out_shape=jax.ShapeDtypeStruct((T, W), BF16),
        compiler_params=_params(("parallel",), 32), name="mem_attention",
    )(q, k, vT)


def _mixout_kernel(yc_ref, ys_ref, yw_ref, ysb_ref, ym_ref, g_ref, wn_ref, wsb_ref, wm_ref, wo_ref, x_ref, gain_ref,
                   x2_o, h2_o):
    D = x_ref.shape[1]
    yn = (yc_ref[...].astype(F32) + ys_ref[...].astype(F32) + yw_ref[...].astype(F32)).astype(BF16)
    a = jnp.dot(yn, wn_ref[...], preferred_element_type=F32)
    b = jnp.dot(ysb_ref[...], wsb_ref[...], preferred_element_type=F32)
    c = jnp.dot(ym_ref[...], wm_ref[...], preferred_element_type=F32)
    sig = lambda n: jax.nn.sigmoid(g_ref[:, n * D:(n + 1) * D].astype(F32))
    mixed = (sig(0) * a + sig(1) * b + sig(2) * c).astype(BF16)
    x2 = x_ref[...] + jnp.dot(mixed, wo_ref[...], preferred_element_type=F32)
    x2_o[...] = x2
    h2_o[...] = _rms(x2, gain_ref[...]).astype(BF16)


def _mix_out(yc, ys, yw, ysb, ym, P, wn, wsb, wm, wo, x, gain):
    T, D = x.shape
    tm = _tile(T, 256)
    rowi = lambda w: pl.BlockSpec((tm, w), lambda i: (i, 0))
    res = lambda a: pl.BlockSpec(a.shape, lambda i: (0, 0), pipeline_mode=pl.Buffered(1))
    return pl.pallas_call(
        _mixout_kernel, grid=(T // tm,),
        in_specs=[rowi(yc.shape[1]), rowi(ys.shape[1]), rowi(yw.shape[1]), rowi(ysb.shape[1]), rowi(ym.shape[1]),
                  rowi(N_BRANCH * D), res(wn), res(wsb), res(wm), res(wo), rowi(D), res(gain)],
        out_specs=[rowi(D), rowi(D)],
        out_shape=[jax.ShapeDtypeStruct((T, D), F32), jax.ShapeDtypeStruct((T, D), BF16)],
        compiler_params=_params(("parallel",), 48), name="mix_out",
    )(yc, ys, yw, ysb, ym, P, wn, wsb, wm, wo, x, gain)


def _ffn_kernel(h_ref, wg_ref, wu_ref, wd_ref, x_ref, o_ref, z_sc, *, nf):
    f = pl.program_id(1)

    def up():
        a = jnp.dot(h_ref[...], wg_ref[...], preferred_element_type=F32)
        b = jnp.dot(h_ref[...], wu_ref[...], preferred_element_type=F32)
        return (a * jax.nn.sigmoid(a) * b).astype(BF16)

    def down():
        return jnp.dot(z_sc[...], wd_ref[...], preferred_element_type=F32)

    @pl.when(f == 0)
    def _():
        z_sc[...] = up()

    @pl.when((f > 0) & (f < nf))
    def _():
        c = down()
        z_new = up()
        o_ref[...] = jnp.where(f == 1, x_ref[...], o_ref[...]) + c
        z_sc[...] = z_new

    @pl.when(f == nf)
    def _():
        o_ref[...] += down()


def _ffn(h2, wg, wu, wd, x2):
    T, D = x2.shape
    F = wg.shape[1]
    tm = _tile(T, 1024)
    tf = 512
    assert F % tf == 0
    nf = F // tf
    assert nf >= 2
    return pl.pallas_call(
        functools.partial(_ffn_kernel, nf=nf), grid=(T // tm, nf + 1),
        in_specs=[pl.BlockSpec((tm, D), lambda i, f: (i, 0)),
                  pl.BlockSpec((D, tf), lambda i, f: (0, jnp.minimum(f, nf - 1))),
                  pl.BlockSpec((D, tf), lambda i, f: (0, jnp.minimum(f, nf - 1))),
                  pl.BlockSpec((tf, D), lambda i, f: (jnp.maximum(f - 1, 0), 0)),
                  pl.BlockSpec((tm, D), lambda i, f: (i, 0), pipeline_mode=pl.Buffered(1))],
        out_specs=pl.BlockSpec((tm, D), lambda i, f: (i, 0)),
        out_shape=jax.ShapeDtypeStruct((T, D), F32),
        scratch_shapes=[pltpu.VMEM((tm, tf), BF16)],
        compiler_params=_params(("parallel", "arbitrary"), 56), name="ffn",
    )(h2, wg.astype(BF16), wu.astype(BF16), wd.astype(BF16), x2)


def _layer(x, mem, pos_col, posc_col, consts, layer, attn_norm, w_in_all, nsa_q_norm, nsa_kc_norm, nsa_ks_norm, nsa_kw_norm,
           cmp_k_pe, cmp_k_w1, cmp_k_w2, cmp_v_pe, cmp_v_w1, cmp_v_w2, mem_norm, w_mem_kv,
           mem_q_norm, mem_k_norm, w_o_nsa, w_o_sb, w_o_mem, w_out, ffn_norm,
           w_ffn_gate, w_ffn_up, w_ffn_down):
    T, D = x.shape
    inv2, sgn, ovT = consts
    hd = HEAD_DIM
    row = lambda g: g.reshape(1, -1)

    q_w, kv_w, gn_w = NSA_HEADS * hd, 6 * NSA_GROUPS * hd, 3 * NSA_HEADS
    sb_w, mq_w, gm_w = 3 * SB_HEADS * hd, MEM_HEADS * hd, N_BRANCH * D
    o_gn = q_w + kv_w
    o_sb = o_gn + gn_w
    o_mq = o_sb + sb_w
    o_gm = o_mq + mq_w
    assert w_in_all.shape[2] == o_gm + gm_w
    wt = jnp.swapaxes(w_in_all[layer], 0, 1)
    wt_main = jnp.concatenate([wt[o_gm:], wt[:o_gn], wt[o_sb:o_gm]], axis=0).astype(BF16)
    wt_gate = jnp.pad(wt[o_gn:o_sb], ((0, LANE - gn_w), (0, 0))).astype(BF16)
    P, Pg = _in_proj(x, row(attn_norm), wt_main, wt_gate)

    (qn, kc_raw, vc_raw, ksn, kwn, vsT, vwT, sbq, sbvT, memq, gT, qn2, kn2) = _prep(
        P, Pg, gm_w, pos_col, inv2, sgn, row(nsa_q_norm), row(nsa_ks_norm), row(nsa_kw_norm), row(mem_q_norm))

    q2 = jnp.max(qn2[::8].reshape(NSA_GROUPS, NSA_HPG, T), axis=1)
    k2 = jnp.max(kn2[::8], axis=1).reshape(2, NSA_GROUPS)
    bound = lambda kk: jnp.broadcast_to((1.02 * jnp.sqrt(q2 * kk[:, None]))[:, None, :], (NSA_GROUPS, 8, T))
    shift_sel, shift_win = bound(k2[0]), bound(k2[1])

    gates = gT[:gn_w].reshape(NSA_GROUPS, NSA_HPG, 3, T).transpose(2, 0, 1, 3)
    gates = jnp.pad(gates, ((0, 0), (0, 0), (0, 8 - NSA_HPG), (0, 0)))

    n_pad = T // CMP_STRIDE
    half = CMP_LEN // 2

    def w1_pack(w1):
        return jnp.concatenate([w1[:half].reshape(half * hd, -1), w1[half:].reshape(half * hd, -1)], axis=1).astype(BF16)

    def pe_pack(pe):
        return jnp.pad(pe.reshape(2, half * hd), ((0, 6), (0, 0))).astype(BF16)

    kc, vcT = _compress(
        kc_raw.reshape(NSA_GROUPS, n_pad, CMP_STRIDE * hd), vc_raw.reshape(NSA_GROUPS, n_pad, CMP_STRIDE * hd),
        w1_pack(cmp_k_w1), w1_pack(cmp_v_w1), pe_pack(cmp_k_pe), pe_pack(cmp_v_pe),
        cmp_k_w2.astype(BF16), cmp_v_w2.astype(BF16), row(nsa_kc_norm), posc_col, inv2, sgn)

    y_cmp, bias = _cmp_select(qn, kc, vcT, ovT, gates)
    y_sel = _nsa_flash("sel", qn, ksn, vsT, gates, shift_sel, bias)
    y_win = _nsa_flash("win", qn, kwn, vwT, gates, shift_win)
    sbk_col = gm_w + q_w + kv_w + SB_HEADS * hd
    assert sbk_col % (SB_HEADS * hd) == 0
    y_sb = _stick_breaking(sbq, P, sbk_col // (SB_HEADS * hd), sbvT)

    mk, mvT = _memkv(mem, row(mem_norm), w_mem_kv.astype(BF16), row(mem_k_norm))
    y_mem = _mem_attention(memq, mk, mvT)

    x2, h2 = _mix_out(y_cmp, y_sel, y_win, y_sb, y_mem, P, w_o_nsa.astype(BF16), w_o_sb.astype(BF16),
                      w_o_mem.astype(BF16), w_out.astype(BF16), x, row(ffn_norm))
    return _ffn(h2, w_ffn_gate, w_ffn_up, w_ffn_down, x2)


def kernel(x, mem, positions, attn_norm, w_in, nsa_q_norm, nsa_kc_norm, nsa_ks_norm, nsa_kw_norm, cmp_k_pe, cmp_k_w1, cmp_k_w2, cmp_v_pe, cmp_v_w1, cmp_v_w2, mem_norm, w_mem_kv, mem_q_norm, mem_k_norm, w_o_nsa, w_o_sb, w_o_mem, w_out, ffn_norm, w_ffn_gate, w_ffn_up, w_ffn_down):
    B, T, D = x.shape
    assert T % (4 * LANE) == 0 and T // SEL_BLOCK >= 8
    n_pad = T // CMP_STRIDE
    n_sel = T // SEL_BLOCK
    inv = 1.0 / (ROPE_THETA ** (jnp.arange(0, HEAD_DIM, 2, dtype=F32) / HEAD_DIM))
    inv2 = jnp.concatenate([inv, inv]).reshape(1, HEAD_DIM)
    sgn = jnp.concatenate([-jnp.ones((HEAD_DIM // 2,), F32), jnp.ones((HEAD_DIM // 2,), F32)]).reshape(1, HEAD_DIM)
    cs = np.arange(n_pad)[None, :] * CMP_STRIDE
    ss = np.arange(n_sel)[:, None] * SEL_BLOCK
    ovT = jnp.asarray(((cs < ss + SEL_BLOCK) & (cs + CMP_LEN - 1 >= ss)).astype(np.float32), BF16)
    consts = (inv2, sgn, ovT)
    depth = w_in.shape[0]
    outs = []
    for b in range(B):
        xb = x[b]
        posf = positions[b].astype(F32)
        pos_col = posf.reshape(T, 1)
        posc = jnp.concatenate([posf[CMP_LEN - 1::CMP_STRIDE], posf[-1:]]).reshape(n_pad, 1)
        for l in range(depth):
            xb = _layer(xb, mem[b], pos_col, posc, consts, l, attn_norm[l], w_in, nsa_q_norm[l], nsa_kc_norm[l],
                        nsa_ks_norm[l], nsa_kw_norm[l], cmp_k_pe[l], cmp_k_w1[l], cmp_k_w2[l], cmp_v_pe[l],
                        cmp_v_w1[l], cmp_v_w2[l], mem_norm[l], w_mem_kv[l], mem_q_norm[l], mem_k_norm[l],
                        w_o_nsa[l], w_o_sb[l], w_o_mem[l], w_out[l], ffn_norm[l],
                        w_ffn_gate[l], w_ffn_up[l], w_ffn_down[l])
        outs.append(xb)
    return outs[0][None] if B == 1 else jnp.stack(outs, axis=0)
```

```python
import functools

import numpy as np
import jax
import jax.numpy as jnp
from jax import lax
from jax.experimental import pallas as pl
from jax.experimental.pallas import tpu as pltpu

HEAD_DIM = 128
NSA_HEADS = 8
NSA_GROUPS = 2
NSA_HPG = NSA_HEADS // NSA_GROUPS
SB_HEADS = 4
MEM_HEADS = 4
CMP_LEN = 32
CMP_STRIDE = 16
CMP_HIDDEN = 2 * HEAD_DIM
SEL_BLOCK = 64
SEL_TOPK = 16
WINDOW = 512
ROPE_THETA = 10000.0
NORM_EPS = 1e-6
NEG_BIG = -1e30
N_BRANCH = 3
SCALE = HEAD_DIM ** -0.5
LOG2E = 1.4426950408889634
QSCALE = SCALE * LOG2E
SOFTMAX_BOUND_MAX = 50.0
CMP_BUCKETS = 8
SB_NEAR_TILES = 2
SB_DEAD_LOG2 = -160.0
VAUG = HEAD_DIM + 16

LANE = 128
MIB = 1 << 20
BF16 = jnp.bfloat16
F32 = jnp.float32

_NT = (((1,), (1,)), ((), ()))


def _tile(n, pref):
    t = min(n, pref)
    assert n % t == 0, (n, pref)
    return t


def _params(sem, vmem_mib):
    return pltpu.CompilerParams(dimension_semantics=sem, vmem_limit_bytes=vmem_mib * MIB)


def _rms(x, gain):
    return x * lax.rsqrt(jnp.mean(x * x, axis=-1, keepdims=True) + NORM_EPS) * gain


def _rope_tables(pos, inv2, sgn):
    ang = pos * inv2
    return jnp.cos(ang), jnp.sin(ang) * sgn


def _rope(x, c, s):
    return x * c + pltpu.roll(x, HEAD_DIM // 2, 1) * s


def _proj_kernel(x_ref, g_ref, w_ref, wg_ref, o_ref, og_ref, hn_ref):
    @pl.when(pl.program_id(1) == 0)
    def _():
        hn_ref[...] = _rms(x_ref[...], g_ref[...]).astype(BF16)
        og_ref[...] = lax.dot_general(hn_ref[...], wg_ref[...], _NT, preferred_element_type=F32)

    o_ref[...] = lax.dot_general(hn_ref[...], w_ref[...], _NT, preferred_element_type=F32).astype(BF16)


def _in_proj(x, gain, wt_main, wt_gate):
    T, D = x.shape
    N = wt_main.shape[0]
    tm = _tile(T, 1024)
    tn = 1536
    assert N % tn == 0
    return pl.pallas_call(
        _proj_kernel,
        grid=(T // tm, N // tn),
        in_specs=[
            pl.BlockSpec((tm, D), lambda i, j: (i, 0)),
            pl.BlockSpec((1, D), lambda i, j: (0, 0)),
            pl.BlockSpec((tn, D), lambda i, j: (j, 0)),
            pl.BlockSpec((LANE, D), lambda i, j: (0, 0)),
        ],
        out_specs=[pl.BlockSpec((tm, tn), lambda i, j: (i, j)), pl.BlockSpec((tm, LANE), lambda i, j: (i, 0))],
        out_shape=[jax.ShapeDtypeStruct((T, N), BF16), jax.ShapeDtypeStruct((T, LANE), F32)],
        scratch_shapes=[pltpu.VMEM((tm, D), BF16)],
        compiler_params=_params(("parallel", "arbitrary"), 52),
        name="in_proj",
    )(x, gain, wt_main, wt_gate)


def _prep_kernel(pos_ref, inv_ref, sgn_ref, gq_ref, gks_ref, gkw_ref, gmq_ref,
                 q_ref, ks_ref, vs_ref, kw_ref, vw_ref,
                 sq_ref, sv_ref, mq_ref, gn_ref,
                 qn_o, ks_o, kw_o, vsT_o, vwT_o, sq_o, svT_o, mq_o, gT_o, qn2_o, kn2_o):
    c, s = _rope_tables(pos_ref[...], inv_ref[...], sgn_ref[...])
    hd = HEAD_DIM
    tp = pos_ref.shape[0]
    f32 = lambda ref, sl: ref[:, sl].astype(F32)
    ones8 = jnp.ones((8, hd), F32)

    def sqnorm_rows(xb):
        x = xb.astype(F32)
        return lax.dot_general(ones8, x * x, _NT, preferred_element_type=F32)

    for h in range(NSA_HEADS):
        sl = slice(h * hd, (h + 1) * hd)
        qb = (_rope(_rms(f32(q_ref, sl), gq_ref[...]), c, s) * QSCALE).astype(BF16)
        qn_o[:, sl] = qb
        qn2_o[h * 8:(h + 1) * 8, :] = sqnorm_rows(qb)
    ones_rows = (lax.broadcasted_iota(jnp.int32, (VAUG - hd, tp), 0) == 0).astype(F32).astype(BF16)
    for g in range(NSA_GROUPS):
        sl = slice(g * hd, (g + 1) * hd)
        ksb = _rope(_rms(f32(ks_ref, sl), gks_ref[...]), c, s).astype(BF16)
        kwb = _rope(_rms(f32(kw_ref, sl), gkw_ref[...]), c, s).astype(BF16)
        ks_o[:, sl] = ksb
        kw_o[:, sl] = kwb
        kn2_o[g * 8:(g + 1) * 8, :] = sqnorm_rows(ksb)
        kn2_o[(NSA_GROUPS + g) * 8:(NSA_GROUPS + g + 1) * 8, :] = sqnorm_rows(kwb)
        vsT_o[g * VAUG:g * VAUG + hd, :] = f32(vs_ref, sl).T.astype(BF16)
        vsT_o[g * VAUG + hd:(g + 1) * VAUG, :] = ones_rows
        vwT_o[g * VAUG:g * VAUG + hd, :] = f32(vw_ref, sl).T.astype(BF16)
        vwT_o[g * VAUG + hd:(g + 1) * VAUG, :] = ones_rows
    for h in range(SB_HEADS):
        sl = slice(h * hd, (h + 1) * hd)
        sq_o[:, sl] = (f32(sq_ref, sl) * QSCALE).astype(BF16)
        svT_o[sl, :] = f32(sv_ref, sl).T.astype(BF16)
    for h in range(MEM_HEADS):
        sl = slice(h * hd, (h + 1) * hd)
        mq_o[:, sl] = (_rms(f32(mq_ref, sl), gmq_ref[...]) * QSCALE).astype(BF16)
    gT_o[...] = jax.nn.sigmoid(gn_ref[...]).T


def _prep(P, Pg, c0, pos_col, inv2, sgn, gq, gks, gkw, gmq):
    T = P.shape[0]
    tp = _tile(T, 512)
    hd = HEAD_DIM
    row = lambda w, c: pl.BlockSpec((tp, w), lambda i, c=c: (i, c))
    const = lambda: pl.BlockSpec((1, hd), lambda i: (0, 0))
    assert c0 % (8 * hd) == 0
    at = lambda w, col: row(w, (c0 + col) // w)
    in_specs = [pl.BlockSpec((tp, 1), lambda i: (i, 0)), const(), const(), const(), const(), const(), const(),
                at(8 * hd, 0),
                at(2 * hd, 12 * hd), at(2 * hd, 14 * hd),
                at(2 * hd, 16 * hd), at(2 * hd, 18 * hd),
                at(4 * hd, 20 * hd), at(4 * hd, 28 * hd),
                at(4 * hd, 32 * hd),
                row(hd, 0)]
    colT = lambda w: pl.BlockSpec((w, tp), lambda i: (0, i))
    out_specs = [row(8 * hd, 0), row(2 * hd, 0),
                 row(2 * hd, 0), colT(NSA_GROUPS * VAUG), colT(NSA_GROUPS * VAUG),
                 row(4 * hd, 0), colT(4 * hd), row(4 * hd, 0), colT(hd),
                 colT(NSA_HEADS * 8), colT(2 * NSA_GROUPS * 8)]
    sds = jax.ShapeDtypeStruct
    out_shape = [sds((T, 8 * hd), BF16), sds((T, 2 * hd), BF16),
                 sds((T, 2 * hd), BF16),
                 sds((NSA_GROUPS * VAUG, T), BF16), sds((NSA_GROUPS * VAUG, T), BF16),
                 sds((T, 4 * hd), BF16), sds((4 * hd, T), BF16), sds((T, 4 * hd), BF16),
                 sds((hd, T), F32), sds((NSA_HEADS * 8, T), F32), sds((2 * NSA_GROUPS * 8, T), F32)]
    return pl.pallas_call(
        _prep_kernel, grid=(T // tp,), in_specs=in_specs, out_specs=out_specs, out_shape=out_shape,
        compiler_params=_params(("parallel",), 48), name="prep",
    )(pos_col, inv2, sgn, gq, gks, gkw, gmq, *([P] * 8), Pg)


def _gelu_tanh(x):
    return 0.5 * x * (1.0 + jnp.tanh(0.7978845608028654 * (x + 0.044715 * (x * x * x))))


def _compress_one(x, w1, pe, w2):
    n = x.shape[0]
    ab = jnp.dot(x, w1, preferred_element_type=F32)
    pr = jnp.dot(pe, w1, preferred_element_type=F32)
    pec = pr[0:1, :CMP_HIDDEN] + pr[1:2, CMP_HIDDEN:]
    hid = ab[:, :CMP_HIDDEN] + pltpu.roll(ab[:, CMP_HIDDEN:], n - 1, 0) + pec
    return jnp.dot(_gelu_tanh(hid).astype(BF16), w2, preferred_element_type=F32)


def _compress_kernel(xk_ref, xv_ref, w1k_ref, w1v_ref, pek_ref, pev_ref, w2k_ref, w2v_ref,
                     gk_ref, pos_ref, inv_ref, sgn_ref, kc_o, vcT_o):
    c, s = _rope_tables(pos_ref[...], inv_ref[...], sgn_ref[...])
    k = _compress_one(xk_ref[0], w1k_ref[...], pek_ref[...], w2k_ref[...])
    kc_o[0] = _rope(_rms(k, gk_ref[...]), c, s).astype(BF16)
    v = _compress_one(xv_ref[0], w1v_ref[...], pev_ref[...], w2v_ref[...])
    vcT_o[0] = v.T.astype(BF16)


def _compress(xk, xv, w1k, w1v, pek, pev, w2k, w2v, gk, posc, inv2, sgn):
    G, n, W = xk.shape
    full = lambda a: pl.BlockSpec(a.shape, lambda g: (0,) * a.ndim)
    grp = pl.BlockSpec((1, n, W), lambda g: (g, 0, 0))
    return pl.pallas_call(
        _compress_kernel, grid=(G,),
        in_specs=[grp, grp, full(w1k), full(w1v), full(pek), full(pev), full(w2k), full(w2v),
                  full(gk), full(posc), full(inv2), full(sgn)],
        out_specs=[pl.BlockSpec((1, n, HEAD_DIM), lambda g: (g, 0, 0)),
                   pl.BlockSpec((1, HEAD_DIM, n), lambda g: (g, 0, 0))],
        out_shape=[jax.ShapeDtypeStruct((G, n, HEAD_DIM), BF16), jax.ShapeDtypeStruct((G, HEAD_DIM, n), BF16)],
        compiler_params=_params(("parallel",), 48), name="compress",
    )(xk, xv, w1k, w1v, pek, pev, w2k, w2v, gk, posc, inv2, sgn)


def _memkv_kernel(mem_ref, gm_ref, w_ref, gk_ref, k_o, vT_o):
    hn = _rms(mem_ref[...], gm_ref[...]).astype(BF16)
    kv = jnp.dot(hn, w_ref[...], preferred_element_type=F32)
    mw = MEM_HEADS * HEAD_DIM
    for h in range(MEM_HEADS):
        sl = slice(h * HEAD_DIM, (h + 1) * HEAD_DIM)
        k_o[:, sl] = _rms(kv[:, sl], gk_ref[...]).astype(BF16)
        vT_o[sl, :] = kv[:, mw + h * HEAD_DIM: mw + (h + 1) * HEAD_DIM].T.astype(BF16)


def _memkv(mem, gm, w, gk):
    M = mem.shape[0]
    mw = MEM_HEADS * HEAD_DIM
    return pl.pallas_call(
        _memkv_kernel,
        out_shape=[jax.ShapeDtypeStruct((M, mw), BF16), jax.ShapeDtypeStruct((mw, M), BF16)],
        compiler_params=pltpu.CompilerParams(vmem_limit_bytes=32 * MIB), name="mem_kv",
    )(mem, gm, w, gk)


def _cmp_kernel(q_ref, kc_ref, vcT_ref, ovT_ref, g_ref, *rest, tq, i0, n_pad, n_sel, top_k):
    y_ref, b_ref = rest[-2:]
    i = i0 + pl.program_id(1)
    t = i * tq + lax.broadcasted_iota(jnp.int32, (1, tq), 1)
    n_end = lax.broadcasted_iota(jnp.int32, (n_pad, 1), 0) * CMP_STRIDE + (CMP_LEN - 1)
    valid = n_end <= t
    kc = kc_ref[0]
    vcT = vcT_ref[0]
    has_valid = (t >= CMP_LEN - 1).astype(F32)
    sms = []
    for h in range(NSA_HPG):
        q_h = q_ref[:, h * HEAD_DIM:(h + 1) * HEAD_DIM]
        sms.append(jnp.where(valid, lax.dot_general(kc, q_h, _NT, preferred_element_type=F32), NEG_BIG))
    psum = jnp.zeros((n_pad, tq), F32)
    for h in range(NSA_HPG):
        e = jnp.exp2(sms[h] - jnp.max(sms[h], axis=0, keepdims=True))
        p = e * (has_valid / jnp.sum(e, axis=0, keepdims=True))
        oT = jnp.dot(vcT, p.astype(BF16), preferred_element_type=F32)
        y_ref[:, h * HEAD_DIM:(h + 1) * HEAD_DIM] = (oT * g_ref[0, 0, h:h + 1, :]).T.astype(BF16)
        psum = psum + p
    hi = psum.astype(BF16)
    r1 = psum - hi.astype(F32)
    mid = r1.astype(BF16)
    lo = (r1 - mid.astype(F32)).astype(BF16)
    parts = jnp.dot(ovT_ref[...], jnp.concatenate([hi, mid, lo], axis=1), preferred_element_type=F32)
    imp = parts[:, :tq] + parts[:, tq:2 * tq] + parts[:, 2 * tq:]
    s_i = lax.broadcasted_iota(jnp.int32, (n_sel, 1), 0)
    cur = lax.shift_right_logical(t, 6)
    forced = (s_i == 0) | (s_i == cur) | (s_i == cur - 1)
    future = s_i * SEL_BLOCK > t
    w = jnp.where(forced, jnp.inf, jnp.where(future, -jnp.inf, imp))
    s_f = jnp.broadcast_to(s_i.astype(F32), (n_sel, tq))
    for _ in range(top_k):
        m = jnp.max(w, axis=0, keepdims=True)
        idx = jnp.min(jnp.where(w == m, s_f, float(n_sel)), axis=0, keepdims=True)
        w = jnp.where(s_f == idx, -jnp.inf, w)
    b_ref[0, :n_sel, :] = jnp.where(future, NEG_BIG, jnp.where(w == -jnp.inf, 0.0, NEG_BIG))
    n_all = b_ref.shape[1]
    if n_sel < n_all:
        b_ref[0, n_sel:, :] = jnp.full((n_all - n_sel, tq), NEG_BIG, F32)


def _round_up(n, m):
    return -(-n // m) * m


def _cmp_select(qn, kc, vcT, ovT, gates):
    T = qn.shape[0]
    G, n_pad, _ = kc.shape
    n_sel = T // SEL_BLOCK
    tq = _tile(T, 512)
    nq = T // tq
    gw = NSA_HPG * HEAD_DIM
    nb = min(CMP_BUCKETS, nq)
    assert nq % nb == 0
    per = nq // nb
    out_shape = [jax.ShapeDtypeStruct((T, NSA_HEADS * HEAD_DIM), BF16), jax.ShapeDtypeStruct((G, n_sel, T), F32)]
    buffers = ()
    for b in range(nb):
        i0 = b * per
        t_max = (i0 + per) * tq - 1
        n_len = min(n_pad, _round_up(max(t_max - (CMP_LEN - 1), 0) // CMP_STRIDE + 1, LANE))
        s_len = min(n_sel, _round_up(t_max // SEL_BLOCK + 1, 8))
        kern = functools.partial(_cmp_kernel, tq=tq, i0=i0, n_pad=n_len, n_sel=s_len, top_k=min(SEL_TOPK, n_sel))
        buffers = pl.pallas_call(
            kern, grid=(G, per),
            in_specs=[pl.BlockSpec((tq, gw), lambda g, i, i0=i0: (i0 + i, g)),
                      pl.BlockSpec((1, n_len, HEAD_DIM), lambda g, i: (g, 0, 0)),
                      pl.BlockSpec((1, HEAD_DIM, n_len), lambda g, i: (g, 0, 0)),
                      pl.BlockSpec((s_len, n_len), lambda g, i: (0, 0)),
                      pl.BlockSpec((1, 1, 8, tq), lambda g, i, i0=i0: (0, g, 0, i0 + i))]
            + [pl.BlockSpec(memory_space=pl.ANY)] * len(buffers),
            out_specs=[pl.BlockSpec((tq, gw), lambda g, i, i0=i0: (i0 + i, g)),
                       pl.BlockSpec((1, n_sel, tq), lambda g, i, i0=i0: (g, 0, i0 + i))],
            out_shape=out_shape,
            input_output_aliases={5: 0, 6: 1} if buffers else {},
            compiler_params=_params(("parallel", "parallel"), 48), name="cmp_select_%d" % b,
        )(qn, kc, vcT, ovT, gates, *buffers)
    return buffers


def _flash_kernel(qi_ref, kj_ref, fl_ref, q_ref, k_ref, vT_ref, g_ref, sh_ref, *rest, mode, bounded, tq, tk):
    if mode == "sel":
        bias_ref, o_ref, m_sc, acc_sc = rest
    else:
        o_ref, m_sc, acc_sc = rest
    p = pl.program_id(1)
    i = qi_ref[p]
    j = kj_ref[p]
    fl = fl_ref[p]

    @pl.when((fl & 1) != 0)
    def _():
        m_sc[...] = jnp.full(m_sc.shape, NEG_BIG, F32)
        acc_sc[...] = jnp.zeros(acc_sc.shape, F32)

    shift = sh_ref[0, 0:1, :] if bounded else 0.0
    nb = tk // SEL_BLOCK
    if mode == "sel":
        rows = bias_ref[0] - shift

    def key_mask():
        t = i * tq + lax.broadcasted_iota(jnp.int32, (1, tq), 1)
        kpos = j * tk + lax.broadcasted_iota(jnp.int32, (tk, 1), 0)
        if mode == "sel":
            base = jnp.concatenate(
                [jnp.broadcast_to(rows[b:b + 1, :], (SEL_BLOCK, tq)) for b in range(nb)], axis=0)
            return jnp.where(kpos <= t, base, NEG_BIG)
        return jnp.where((kpos <= t) & (kpos > t - WINDOW), 0.0 - shift, NEG_BIG)

    k = k_ref[...]
    vT = vT_ref[...]
    raw = lambda h: lax.dot_general(k, q_ref[:, h * HEAD_DIM:(h + 1) * HEAD_DIM], _NT, preferred_element_type=F32)

    def accumulate(pTs):
        for h in range(NSA_HPG):
            acc_sc[h] = acc_sc[h] + jnp.dot(vT, pTs[h], preferred_element_type=F32)

    if bounded and mode == "sel":
        @pl.when((fl & 4) != 0)
        def _():
            mask_add = key_mask()
            accumulate([jnp.exp2(raw(h) + mask_add).astype(BF16) for h in range(NSA_HPG)])

        @pl.when((fl & 4) == 0)
        def _():
            def probs(h):
                s = raw(h).reshape(nb, SEL_BLOCK, tq) + rows[:, None, :]
                return jnp.exp2(s).astype(BF16).reshape(tk, tq)
            accumulate([probs(h) for h in range(NSA_HPG)])
    elif bounded:
        mask_add = key_mask()
        accumulate([jnp.exp2(raw(h) + mask_add).astype(BF16) for h in range(NSA_HPG)])
    else:
        mask_add = key_mask()
        sTs = [raw(h) + mask_add for h in range(NSA_HPG)]
        pTs, alphas = [], []
        for h in range(NSA_HPG):
            sT = sTs[h]
            m_old = m_sc[h]
            m_new = jnp.maximum(m_old, jnp.max(sT, axis=0, keepdims=True))
            alphas.append(jnp.exp2(m_old - m_new))
            pTs.append(jnp.exp2((sT - m_new).astype(BF16)))
            m_sc[h] = m_new
        for h in range(NSA_HPG):
            acc_sc[h] = alphas[h] * acc_sc[h] + jnp.dot(vT, pTs[h], preferred_element_type=F32)

    @pl.when((fl & 2) != 0)
    def _():
        for h in range(NSA_HPG):
            acc = acc_sc[h]
            l = acc[HEAD_DIM:HEAD_DIM + 1, :]
            o = acc[:HEAD_DIM, :] * ((1.0 / l) * g_ref[0, 0, h:h + 1, :])
            o_ref[:, h * HEAD_DIM:(h + 1) * HEAD_DIM] = o.T.astype(BF16)


def _steps(nq, lo_fn, hi_fn, reverse=False):
    qi, kj, fl = [], [], []
    for i in range(nq):
        js = list(range(lo_fn(i), hi_fn(i) + 1))
        if reverse:
            js = js[::-1]
        for n, j in enumerate(js):
            qi.append(i)
            kj.append(j)
            fl.append((1 if n == 0 else 0) | (2 if n == len(js) - 1 else 0))
    return (jnp.asarray(np.array(qi, np.int32)), jnp.asarray(np.array(kj, np.int32)),
            jnp.asarray(np.array(fl, np.int32)))


def _nsa_flash(mode, qn, k, vT, gates, shift, bias=None):
    use_bounded = jnp.max(shift) <= SOFTMAX_BOUND_MAX
    return lax.cond(use_bounded,
                    lambda: _nsa_flash_call(mode, True, qn, k, vT, gates, shift, bias),
                    lambda: _nsa_flash_call(mode, False, qn, k, vT, gates, shift, bias))


def _nsa_flash_call(mode, bounded, qn, k, vT, gates, shift, bias):
    T = qn.shape[0]
    tq = _tile(T, 1024 if mode == "sel" else WINDOW)
    tk = _tile(T, 1024 if mode == "sel" else 512)
    nq = T // tq
    gw = NSA_HPG * HEAD_DIM
    hi = lambda i: ((i + 1) * tq - 1) // tk
    if mode == "sel":
        lo = lambda i: 0
        br = 1
    else:
        lo = lambda i: max(0, (i * tq - (WINDOW - 1)) // tk)
        br = 2
    qi, kj, fl = _steps(nq, lo, hi)
    fl = fl | 4 * ((kj + 1) * tk - 1 > qi * tq).astype(jnp.int32)
    in_specs = [pl.BlockSpec((tq, gw), lambda g, p, qi, kj, fl: (qi[p], g)),
                pl.BlockSpec((tk, HEAD_DIM), lambda g, p, qi, kj, fl: (kj[p], g)),
                pl.BlockSpec((VAUG, tk), lambda g, p, qi, kj, fl: (g, kj[p])),
                pl.BlockSpec((1, 1, 8, tq), lambda g, p, qi, kj, fl, br=br: (br, g, 0, qi[p])),
                pl.BlockSpec((1, 8, tq), lambda g, p, qi, kj, fl: (g, 0, qi[p]))]
    args = [qn, k, vT, gates, shift]
    if mode == "sel":
        in_specs.append(pl.BlockSpec((1, tk // SEL_BLOCK, tq), lambda g, p, qi, kj, fl: (g, kj[p], qi[p])))
        args.append(bias)
    kern = functools.partial(_flash_kernel, mode=mode, bounded=bounded, tq=tq, tk=tk)
    return pl.pallas_call(
        kern,
        grid_spec=pltpu.PrefetchScalarGridSpec(
            num_scalar_prefetch=3, grid=(NSA_GROUPS, int(qi.shape[0])),
            in_specs=in_specs,
            out_specs=pl.BlockSpec((tq, gw), lambda g, p, qi, kj, fl: (qi[p], g)),
            scratch_shapes=[pltpu.VMEM((NSA_HPG, 1, tq), F32), pltpu.VMEM((NSA_HPG, VAUG, tq), F32)]),
        out_shape=jax.ShapeDtypeStruct((T, NSA_HEADS * HEAD_DIM), BF16),
        compiler_params=_params(("parallel", "arbitrary"), 48),
        name="nsa_" + mode + ("_bounded" if bounded else "_online"),
    )(qi, kj, fl, *args)


def _sb_kernel(qi_ref, kj_ref, fl_ref, q_ref, k_ref, vT_ref, lm_ref, *rest, tq, tk, resume):
    if resume:
        acc_in, carry_in, o_ref, carry_sc, acc_sc = rest
    else:
        o_ref, acc_o, carry_o, carry_sc, acc_sc = rest
    p = pl.program_id(0)
    i = qi_ref[p]
    j = kj_ref[p]
    fl = fl_ref[p]

    @pl.when((fl & 1) != 0)
    def _():
        if resume:
            for h in range(SB_HEADS):
                carry_sc[h] = carry_in[h:h + 1, :]
                acc_sc[h] = acc_in[h * HEAD_DIM:(h + 1) * HEAD_DIM, :]
        else:
            carry_sc[...] = jnp.zeros(carry_sc.shape, F32)
            acc_sc[...] = jnp.zeros(acc_sc.shape, F32)

    @pl.when((jnp.max(carry_sc[...]) > SB_DEAD_LOG2) & ((fl & 4) == 0))
    def _():
        t = i * tq + lax.broadcasted_iota(jnp.int32, (1, tq), 1)
        kpos = j * tk + lax.broadcasted_iota(jnp.int32, (tk, 1), 0)
        past = kpos < t
        lm = lm_ref[...]
        hs = [slice(h * HEAD_DIM, (h + 1) * HEAD_DIM) for h in range(SB_HEADS)]
        zs = [lax.dot_general(k_ref[:, sl], q_ref[:, sl], _NT, preferred_element_type=F32) for sl in hs]
        lgs, css = [], []
        for h in range(SB_HEADS):
            z = zs[h]
            sp = jnp.maximum(z, 0.0) + jnp.log2(1.0 + jnp.exp2(-jnp.abs(z)))
            lk = jnp.where(past, -sp, 0.0)
            hi = lk.astype(BF16)
            lo = (lk - hi.astype(F32)).astype(BF16)
            css.append(jnp.dot(lm, jnp.concatenate([hi, lo], axis=1), preferred_element_type=F32))
            lgs.append(z - sp)
            carry_old = carry_sc[h]
            carry_sc[h] = carry_old + jnp.sum(lk, axis=0, keepdims=True)
            css[h] = css[h][:, :tq] + css[h][:, tq:] + carry_old
        for h in range(SB_HEADS):
            wgt = jnp.where(past, jnp.exp2(lgs[h] + css[h]), 0.0)
            acc_sc[h] = acc_sc[h] + jnp.dot(vT_ref[hs[h], :], wgt.astype(BF16), preferred_element_type=F32)

    @pl.when((fl & 2) != 0)
    def _():
        for h in range(SB_HEADS):
            o_ref[:, h * HEAD_DIM:(h + 1) * HEAD_DIM] = acc_sc[h].T.astype(BF16)
            if not resume:
                acc_o[h * HEAD_DIM:(h + 1) * HEAD_DIM, :] = acc_sc[h]
                carry_o[h:h + 1, :] = carry_sc[h]
        if not resume:
            carry_o[SB_HEADS:, :] = jnp.zeros((8 - SB_HEADS, tq), F32)


def _sb_call(q, k, kcol, vT, steps, tq, tk, state=None):
    T, W = q.shape
    qi, kj, fl = steps
    lmat = jnp.asarray(np.triu(np.ones((tk, tk), np.float32), 1), BF16)
    resume = state is not None
    qtile = lambda shape: pl.BlockSpec(shape, lambda p, qi, kj, fl: (qi[p], 0))
    qtileT = lambda rows: pl.BlockSpec((rows, tq), lambda p, qi, kj, fl: (0, qi[p]))
    in_specs = [qtile((tq, W)),
                pl.BlockSpec((tk, W), lambda p, qi, kj, fl: (kj[p], kcol)),
                pl.BlockSpec((W, tk), lambda p, qi, kj, fl: (0, kj[p])),
                pl.BlockSpec((tk, tk), lambda p, qi, kj, fl: (0, 0))]
    y_sds = jax.ShapeDtypeStruct((T, W), BF16)
    if resume:
        in_specs += [qtileT(W), qtileT(8)]
        out_specs, out_shape = qtile((tq, W)), y_sds
    else:
        out_specs = [qtile((tq, W)), qtileT(W), qtileT(8)]
        out_shape = [y_sds, jax.ShapeDtypeStruct((W, T), F32), jax.ShapeDtypeStruct((8, T), F32)]
    return pl.pallas_call(
        functools.partial(_sb_kernel, tq=tq, tk=tk, resume=resume),
        grid_spec=pltpu.PrefetchScalarGridSpec(
            num_scalar_prefetch=3, grid=(int(qi.shape[0]),),
            in_specs=in_specs, out_specs=out_specs,
            scratch_shapes=[pltpu.VMEM((SB_HEADS, 1, tq), F32), pltpu.VMEM((SB_HEADS, HEAD_DIM, tq), F32)]),
        out_shape=out_shape,
        compiler_params=_params(("arbitrary",), 48), name="sb_far" if resume else "sb_near",
    )(qi, kj, fl, q, k, vT, lmat, *(state or ()))


def _stick_breaking(q, k, kcol, vT):
    T, W = q.shape
    tq = _tile(T, 256)
    tk = _tile(T, 256)
    nq = T // tq
    hi = lambda i: ((i + 1) * tq - 2) // tk
    lo_near = lambda i: max(0, hi(i) - SB_NEAR_TILES + 1)
    near = _steps(nq, lo_near, hi, reverse=True)
    y_near, acc, carry = _sb_call(q, k, kcol, vT, near, tq, tk)
    qi, kj, fl = [], [], []
    for i in range(nq):
        js = list(range(lo_near(i) - 1, -1, -1))
        for n, j in enumerate(js or [0]):
            qi.append(i)
            kj.append(j)
            fl.append((1 if n == 0 else 0) | (2 if n == max(len(js), 1) - 1 else 0) | (0 if js else 4))
    far = tuple(jnp.asarray(np.array(a, np.int32)) for a in (qi, kj, fl))
    with_far = [i for i in range(nq) if lo_near(i) > 0]
    if not with_far:
        return y_near
    alive = jnp.max(carry[:SB_HEADS, with_far[0] * tq:]) > SB_DEAD_LOG2
    return lax.cond(alive, lambda: _sb_call(q, k, kcol, vT, far, tq, tk, state=(acc, carry)), lambda: y_near)


def _memattn_kernel(q_ref, k_ref, vT_ref, o_ref):
    for h in range(MEM_HEADS):
        sl = slice(h * HEAD_DIM, (h + 1) * HEAD_DIM)
        sT = lax.dot_general(k_ref[:, sl], q_ref[:, sl], _NT, preferred_element_type=F32)
        e = jnp.exp2(sT - jnp.max(sT, axis=0, keepdims=True))
        l = jnp.sum(e, axis=0, keepdims=True)
        oT = jnp.dot(vT_ref[sl, :], e.astype(BF16), preferred_element_type=F32) * (1.0 / l)
        o_ref[:, sl] = oT.T.astype(BF16)


def _mem_attention(q, k, vT):
    T, W = q.shape
    M = k.shape[0]
    tq = _tile(T, 1024)
    return pl.pallas_call(
        _memattn_kernel, grid=(T // tq,),
        in_specs=[pl.BlockSpec((tq, W), lambda i: (i, 0)),
                  pl.BlockSpec((M, W), lambda i: (0, 0)),
                  pl.BlockSpec((W, M), lambda i: (0, 0))],
        out_specs=pl.BlockSpec((tq, W), lambda i: (i, 0)),
        out_shape=jax.ShapeDtypeStruct((T, W), BF16),
        compiler_params=_params(("parallel",), 32), name="mem_attention",
    )(q, k, vT)


def _mixout_kernel(yc_ref, ys_ref, yw_ref, ysb_ref, ym_ref, g_ref, wn_ref, wsb_ref, wm_ref, wo_ref, x_ref, gain_ref,
                   x2_o, h2_o):
    D = x_ref.shape[1]
    yn = (yc_ref[...].astype(F32) + ys_ref[...].astype(F32) + yw_ref[...].astype(F32)).astype(BF16)
    a = jnp.dot(yn, wn_ref[...], preferred_element_type=F32)
    b = jnp.dot(ysb_ref[...], wsb_ref[...], preferred_element_type=F32)
    c = jnp.dot(ym_ref[...], wm_ref[...], preferred_element_type=F32)
    sig = lambda n: jax.nn.sigmoid(g_ref[:, n * D:(n + 1) * D].astype(F32))
    mixed = (sig(0) * a + sig(1) * b + sig(2) * c).astype(BF16)
    x2 = x_ref[...] + jnp.dot(mixed, wo_ref[...], preferred_element_type=F32)
    x2_o[...] = x2
    h2_o[...] = _rms(x2, gain_ref[...]).astype(BF16)


def _mix_out(yc, ys, yw, ysb, ym, P, wn, wsb, wm, wo, x, gain):
    T, D = x.shape
    tm = _tile(T, 256)
    rowi = lambda w: pl.BlockSpec((tm, w), lambda i: (i, 0))
    res = lambda a: pl.BlockSpec(a.shape, lambda i: (0, 0), pipeline_mode=pl.Buffered(1))
    return pl.pallas_call(
        _mixout_kernel, grid=(T // tm,),
        in_specs=[rowi(yc.shape[1]), rowi(ys.shape[1]), rowi(yw.shape[1]), rowi(ysb.shape[1]), rowi(ym.shape[1]),
                  rowi(N_BRANCH * D), res(wn), res(wsb), res(wm), res(wo), rowi(D), res(gain)],
        out_specs=[rowi(D), rowi(D)],
        out_shape=[jax.ShapeDtypeStruct((T, D), F32), jax.ShapeDtypeStruct((T, D), BF16)],
        compiler_params=_params(("parallel",), 48), name="mix_out",
    )(yc, ys, yw, ysb, ym, P, wn, wsb, wm, wo, x, gain)


def _ffn_kernel(h_ref, wg_ref, wu_ref, wd_ref, x_ref, o_ref, z_sc, *, nf):
    f = pl.program_id(1)

    def up():
        a = jnp.dot(h_ref[...], wg_ref[...], preferred_element_type=F32)
        b = jnp.dot(h_ref[...], wu_ref[...], preferred_element_type=F32)
        return (a * jax.nn.sigmoid(a) * b).astype(BF16)

    def down():
        return jnp.dot(z_sc[...], wd_ref[...], preferred_element_type=F32)

    @pl.when(f == 0)
    def _():
        z_sc[...] = up()

    @pl.when((f > 0) & (f < nf))
    def _():
        c = down()
        z_new = up()
        o_ref[...] = jnp.where(f == 1, x_ref[...], o_ref[...]) + c
        z_sc[...] = z_new

    @pl.when(f == nf)
    def _():
        o_ref[...] += down()


def _ffn(h2, wg, wu, wd, x2):
    T, D = x2.shape
    F = wg.shape[1]
    tm = _tile(T, 1024)
    tf = 512
    assert F % tf == 0
    nf = F // tf
    assert nf >= 2
    return pl.pallas_call(
        functools.partial(_ffn_kernel, nf=nf), grid=(T // tm, nf + 1),
        in_specs=[pl.BlockSpec((tm, D), lambda i, f: (i, 0)),
                  pl.BlockSpec((D, tf), lambda i, f: (0, jnp.minimum(f, nf - 1))),
                  pl.BlockSpec((D, tf), lambda i, f: (0, jnp.minimum(f, nf - 1))),
                  pl.BlockSpec((tf, D), lambda i, f: (jnp.maximum(f - 1, 0), 0)),
                  pl.BlockSpec((tm, D), lambda i, f: (i, 0), pipeline_mode=pl.Buffered(1))],
        out_specs=pl.BlockSpec((tm, D), lambda i, f: (i, 0)),
        out_shape=jax.ShapeDtypeStruct((T, D), F32),
        scratch_shapes=[pltpu.VMEM((tm, tf), BF16)],
        compiler_params=_params(("parallel", "arbitrary"), 56), name="ffn",
    )(h2, wg.astype(BF16), wu.astype(BF16), wd.astype(BF16), x2)


def _layer(x, mem, pos_col, posc_col, consts, layer, attn_norm, w_in_all, nsa_q_norm, nsa_kc_norm, nsa_ks_norm, nsa_kw_norm,
           cmp_k_pe, cmp_k_w1, cmp_k_w2, cmp_v_pe, cmp_v_w1, cmp_v_w2, mem_norm, w_mem_kv,
           mem_q_norm, mem_k_norm, w_o_nsa, w_o_sb, w_o_mem, w_out, ffn_norm,
           w_ffn_gate, w_ffn_up, w_ffn_down):
    T, D = x.shape
    inv2, sgn, ovT = consts
    hd = HEAD_DIM
    row = lambda g: g.reshape(1, -1)

    q_w, kv_w, gn_w = NSA_HEADS * hd, 6 * NSA_GROUPS * hd, 3 * NSA_HEADS
    sb_w, mq_w, gm_w = 3 * SB_HEADS * hd, MEM_HEADS * hd, N_BRANCH * D
    o_gn = q_w + kv_w
    o_sb = o_gn + gn_w
    o_mq = o_sb + sb_w
    o_gm = o_mq + mq_w
    assert w_in_all.shape[2] == o_gm + gm_w
    wt = jnp.swapaxes(w_in_all[layer], 0, 1)
    wt_main = jnp.concatenate([wt[o_gm:], wt[:o_gn], wt[o_sb:o_gm]], axis=0).astype(BF16)
    wt_gate = jnp.pad(wt[o_gn:o_sb], ((0, LANE - gn_w), (0, 0))).astype(BF16)
    P, Pg = _in_proj(x, row(attn_norm), wt_main, wt_gate)

    (qn, ksn, kwn, vsT, vwT, sbq, sbvT, memq, gT, qn2, kn2) = _prep(
        P, Pg, gm_w, pos_col, inv2, sgn, row(nsa_q_norm), row(nsa_ks_norm), row(nsa_kw_norm), row(mem_q_norm))

    q2 = jnp.max(qn2[::8].reshape(NSA_GROUPS, NSA_HPG, T), axis=1)
    k2 = jnp.max(kn2[::8], axis=1).reshape(2, NSA_GROUPS)
    bound = lambda kk: jnp.broadcast_to((1.02 * jnp.sqrt(q2 * kk[:, None]))[:, None, :], (NSA_GROUPS, 8, T))
    shift_sel, shift_win = bound(k2[0]), bound(k2[1])

    gates = gT[:gn_w].reshape(NSA_GROUPS, NSA_HPG, 3, T).transpose(2, 0, 1, 3)
    gates = jnp.pad(gates, ((0, 0), (0, 0), (0, 8 - NSA_HPG), (0, 0)))

    n_pad = T // CMP_STRIDE
    half = CMP_LEN // 2

    def w1_pack(w1):
        return jnp.concatenate([w1[:half].reshape(half * hd, -1), w1[half:].reshape(half * hd, -1)], axis=1).astype(BF16)

    def pe_pack(pe):
        return jnp.pad(pe.reshape(2, half * hd), ((0, 6), (0, 0))).astype(BF16)

    def chunks(col):
        return (P[:, col:col + NSA_GROUPS * hd].reshape(T, NSA_GROUPS, hd).transpose(1, 0, 2)
                .reshape(NSA_GROUPS, n_pad, CMP_STRIDE * hd))

    kc, vcT = _compress(
        chunks(gm_w + q_w), chunks(gm_w + q_w + NSA_GROUPS * hd),
        w1_pack(cmp_k_w1), w1_pack(cmp_v_w1), pe_pack(cmp_k_pe), pe_pack(cmp_v_pe),
        cmp_k_w2.astype(BF16), cmp_v_w2.astype(BF16), row(nsa_kc_norm), posc_col, inv2, sgn)

    y_cmp, bias = _cmp_select(qn, kc, vcT, ovT, gates)
    y_sel = _nsa_flash("sel", qn, ksn, vsT, gates, shift_sel, bias)
    y_win = _nsa_flash("win", qn, kwn, vwT, gates, shift_win)
    sbk_col = gm_w + q_w + kv_w + SB_HEADS * hd
    assert sbk_col % (SB_HEADS * hd) == 0
    y_sb = _stick_breaking(sbq, P, sbk_col // (SB_HEADS * hd), sbvT)

    mk, mvT = _memkv(mem, row(mem_norm), w_mem_kv.astype(BF16), row(mem_k_norm))
    y_mem = _mem_attention(memq, mk, mvT)

    x2, h2 = _mix_out(y_cmp, y_sel, y_win, y_sb, y_mem, P, w_o_nsa.astype(BF16), w_o_sb.astype(BF16),
                      w_o_mem.astype(BF16), w_out.astype(BF16), x, row(ffn_norm))
    return _ffn(h2, w_ffn_gate, w_ffn_up, w_ffn_down, x2)


def kernel(x, mem, positions, attn_norm, w_in, nsa_q_norm, nsa_kc_norm, nsa_ks_norm, nsa_kw_norm, cmp_k_pe, cmp_k_w1, cmp_k_w2, cmp_v_pe, cmp_v_w1, cmp_v_w2, mem_norm, w_mem_kv, mem_q_norm, mem_k_norm, w_o_nsa, w_o_sb, w_o_mem, w_out, ffn_norm, w_ffn_gate, w_ffn_up, w_ffn_down):
    B, T, D = x.shape
    assert T % (4 * LANE) == 0 and T // SEL_BLOCK >= 8
    n_pad = T // CMP_STRIDE
    n_sel = T // SEL_BLOCK
    inv = 1.0 / (ROPE_THETA ** (jnp.arange(0, HEAD_DIM, 2, dtype=F32) / HEAD_DIM))
    inv2 = jnp.concatenate([inv, inv]).reshape(1, HEAD_DIM)
    sgn = jnp.concatenate([-jnp.ones((HEAD_DIM // 2,), F32), jnp.ones((HEAD_DIM // 2,), F32)]).reshape(1, HEAD_DIM)
    cs = np.arange(n_pad)[None, :] * CMP_STRIDE
    ss = np.arange(n_sel)[:, None] * SEL_BLOCK
    ovT = jnp.asarray(((cs < ss + SEL_BLOCK) & (cs + CMP_LEN - 1 >= ss)).astype(np.float32), BF16)
    consts = (inv2, sgn, ovT)
    depth = w_in.shape[0]
    outs = []
    for b in range(B):
        xb = x[b]
        posf = positions[b].astype(F32)
        pos_col = posf.reshape(T, 1)
        posc = jnp.concatenate([posf[CMP_LEN - 1::CMP_STRIDE], posf[-1:]]).reshape(n_pad, 1)
        for l in range(depth):
            xb = _layer(xb, mem[b], pos_col, posc, consts, l, attn_norm[l], w_in, nsa_q_norm[l], nsa_kc_norm[l],
                        nsa_ks_norm[l], nsa_kw_norm[l], cmp_k_pe[l], cmp_k_w1[l], cmp_k_w2[l], cmp_v_pe[l],
                        cmp_v_w1[l], cmp_v_w2[l], mem_norm[l], w_mem_kv[l], mem_q_norm[l], mem_k_norm[l],
                        w_o_nsa[l], w_o_sb[l], w_o_mem[l], w_out[l], ffn_norm[l],
                        w_ffn_gate[l], w_ffn_up[l], w_ffn_down[l])
        outs.append(xb)
    return outs[0][None] if B == 1 else jnp.stack(outs, axis=0)
```

```python
import functools

import numpy as np
import jax
import jax.numpy as jnp
from jax import lax
from jax.experimental import pallas as pl
from jax.experimental.pallas import tpu as pltpu

HEAD_DIM = 128
NSA_HEADS = 8
NSA_GROUPS = 2
NSA_HPG = NSA_HEADS // NSA_GROUPS
SB_HEADS = 4
MEM_HEADS = 4
CMP_LEN = 32
CMP_STRIDE = 16
CMP_HIDDEN = 2 * HEAD_DIM
SEL_BLOCK = 64
SEL_TOPK = 16
WINDOW = 512
ROPE_THETA = 10000.0
NORM_EPS = 1e-6
NEG_BIG = -1e30
N_BRANCH = 3
SCALE = HEAD_DIM ** -0.5
LOG2E = 1.4426950408889634
QSCALE = SCALE * LOG2E
SOFTMAX_BOUND_MAX = 50.0
CMP_BUCKETS = 8
SB_NEAR_TILES = 2
SB_DEAD_LOG2 = -160.0
VAUG = HEAD_DIM + 16

LANE = 128
MIB = 1 << 20
BF16 = jnp.bfloat16
F32 = jnp.float32

_NT = (((1,), (1,)), ((), ()))


def _tile(n, pref):
    t = min(n, pref)
    assert n % t == 0, (n, pref)
    return t


def _params(sem, vmem_mib):
    return pltpu.CompilerParams(dimension_semantics=sem, vmem_limit_bytes=vmem_mib * MIB)


def _rms(x, gain):
    return x * lax.rsqrt(jnp.mean(x * x, axis=-1, keepdims=True) + NORM_EPS) * gain


def _rope_tables(pos, inv2, sgn):
    ang = pos * inv2
    return jnp.cos(ang), jnp.sin(ang) * sgn


def _rope(x, c, s):
    return x * c + pltpu.roll(x, HEAD_DIM // 2, 1) * s


def _proj_kernel(x_ref, g_ref, w_ref, wg_ref, o_ref, og_ref, hn_ref):
    @pl.when(pl.program_id(1) == 0)
    def _():
        hn_ref[...] = _rms(x_ref[...], g_ref[...]).astype(BF16)
        og_ref[...] = lax.dot_general(hn_ref[...], wg_ref[...], _NT, preferred_element_type=F32)

    o_ref[...] = lax.dot_general(hn_ref[...], w_ref[...], _NT, preferred_element_type=F32).astype(BF16)


def _in_proj(x, gain, wt_main, wt_gate):
    T, D = x.shape
    N = wt_main.shape[0]
    tm = _tile(T, 1024)
    tn = 1792
    assert N % tn == 0
    return pl.pallas_call(
        _proj_kernel,
        grid=(T // tm, N // tn),
        in_specs=[
            pl.BlockSpec((tm, D), lambda i, j: (i, 0)),
            pl.BlockSpec((1, D), lambda i, j: (0, 0)),
            pl.BlockSpec((tn, D), lambda i, j: (j, 0)),
            pl.BlockSpec((LANE, D), lambda i, j: (0, 0)),
        ],
        out_specs=[pl.BlockSpec((tm, tn), lambda i, j: (i, j)), pl.BlockSpec((tm, LANE), lambda i, j: (i, 0))],
        out_shape=[jax.ShapeDtypeStruct((T, N), BF16), jax.ShapeDtypeStruct((T, LANE), F32)],
        scratch_shapes=[pltpu.VMEM((tm, D), BF16)],
        compiler_params=_params(("parallel", "arbitrary"), 52),
        name="in_proj",
    )(x, gain, wt_main, wt_gate)


def _prep_kernel(pos_ref, inv_ref, sgn_ref, gq_ref, gks_ref, gkw_ref, gmq_ref,
                 q_ref, ks_ref, vs_ref, kw_ref, vw_ref,
                 sq_ref, sv_ref, mq_ref, gn_ref,
                 qn_o, ks_o, kw_o, vsT_o, vwT_o, sq_o, svT_o, mq_o, gT_o, qn2_o, kn2_o):
    c, s = _rope_tables(pos_ref[...], inv_ref[...], sgn_ref[...])
    hd = HEAD_DIM
    tp = pos_ref.shape[0]
    f32 = lambda ref, sl: ref[:, sl].astype(F32)
    ones8 = jnp.ones((8, hd), F32)

    def sqnorm_rows(xb):
        x = xb.astype(F32)
        return lax.dot_general(ones8, x * x, _NT, preferred_element_type=F32)

    for h in range(NSA_HEADS):
        sl = slice(h * hd, (h + 1) * hd)
        qb = (_rope(_rms(f32(q_ref, sl), gq_ref[...]), c, s) * QSCALE).astype(BF16)
        qn_o[:, sl] = qb
        qn2_o[h * 8:(h + 1) * 8, :] = sqnorm_rows(qb)
    ones_rows = (lax.broadcasted_iota(jnp.int32, (VAUG - hd, tp), 0) == 0).astype(F32).astype(BF16)
    for g in range(NSA_GROUPS):
        sl = slice(g * hd, (g + 1) * hd)
        ksb = _rope(_rms(f32(ks_ref, sl), gks_ref[...]), c, s).astype(BF16)
        kwb = _rope(_rms(f32(kw_ref, sl), gkw_ref[...]), c, s).astype(BF16)
        ks_o[:, sl] = ksb
        kw_o[:, sl] = kwb
        kn2_o[g * 8:(g + 1) * 8, :] = sqnorm_rows(ksb)
        kn2_o[(NSA_GROUPS + g) * 8:(NSA_GROUPS + g + 1) * 8, :] = sqnorm_rows(kwb)
        vsT_o[g * VAUG:g * VAUG + hd, :] = f32(vs_ref, sl).T.astype(BF16)
        vsT_o[g * VAUG + hd:(g + 1) * VAUG, :] = ones_rows
        vwT_o[g * VAUG:g * VAUG + hd, :] = f32(vw_ref, sl).T.astype(BF16)
        vwT_o[g * VAUG + hd:(g + 1) * VAUG, :] = ones_rows
    for h in range(SB_HEADS):
        sl = slice(h * hd, (h + 1) * hd)
        sq_o[:, sl] = (f32(sq_ref, sl) * QSCALE).astype(BF16)
        svT_o[sl, :] = f32(sv_ref, sl).T.astype(BF16)
    for h in range(MEM_HEADS):
        sl = slice(h * hd, (h + 1) * hd)
        mq_o[:, sl] = (_rms(f32(mq_ref, sl), gmq_ref[...]) * QSCALE).astype(BF16)
    gT_o[...] = jax.nn.sigmoid(gn_ref[...]).T


def _prep(P, Pg, c0, pos_col, inv2, sgn, gq, gks, gkw, gmq):
    T = P.shape[0]
    tp = _tile(T, 512)
    hd = HEAD_DIM
    row = lambda w, c: pl.BlockSpec((tp, w), lambda i, c=c: (i, c))
    const = lambda: pl.BlockSpec((1, hd), lambda i: (0, 0))
    assert c0 % (8 * hd) == 0
    at = lambda w, col: row(w, (c0 + col) // w)
    in_specs = [pl.BlockSpec((tp, 1), lambda i: (i, 0)), const(), const(), const(), const(), const(), const(),
                at(8 * hd, 0),
                at(2 * hd, 12 * hd), at(2 * hd, 14 * hd),
                at(2 * hd, 16 * hd), at(2 * hd, 18 * hd),
                at(4 * hd, 20 * hd), at(4 * hd, 28 * hd),
                at(4 * hd, 32 * hd),
                row(hd, 0)]
    colT = lambda w: pl.BlockSpec((w, tp), lambda i: (0, i))
    out_specs = [row(8 * hd, 0), row(2 * hd, 0),
                 row(2 * hd, 0), colT(NSA_GROUPS * VAUG), colT(NSA_GROUPS * VAUG),
                 row(4 * hd, 0), colT(4 * hd), row(4 * hd, 0), colT(hd),
                 colT(NSA_HEADS * 8), colT(2 * NSA_GROUPS * 8)]
    sds = jax.ShapeDtypeStruct
    out_shape = [sds((T, 8 * hd), BF16), sds((T, 2 * hd), BF16),
                 sds((T, 2 * hd), BF16),
                 sds((NSA_GROUPS * VAUG, T), BF16), sds((NSA_GROUPS * VAUG, T), BF16),
                 sds((T, 4 * hd), BF16), sds((4 * hd, T), BF16), sds((T, 4 * hd), BF16),
                 sds((hd, T), F32), sds((NSA_HEADS * 8, T), F32), sds((2 * NSA_GROUPS * 8, T), F32)]
    return pl.pallas_call(
        _prep_kernel, grid=(T // tp,), in_specs=in_specs, out_specs=out_specs, out_shape=out_shape,
        compiler_params=_params(("parallel",), 48), name="prep",
    )(pos_col, inv2, sgn, gq, gks, gkw, gmq, *([P] * 8), Pg)


def _gelu_tanh(x):
    return 0.5 * x * (1.0 + jnp.tanh(0.7978845608028654 * (x + 0.044715 * (x * x * x))))


def _compress_one(x, w1, pe, w2):
    n = x.shape[0]
    ab = jnp.dot(x, w1, preferred_element_type=F32)
    pr = jnp.dot(pe, w1, preferred_element_type=F32)
    pec = pr[0:1, :CMP_HIDDEN] + pr[1:2, CMP_HIDDEN:]
    hid = ab[:, :CMP_HIDDEN] + pltpu.roll(ab[:, CMP_HIDDEN:], n - 1, 0) + pec
    return jnp.dot(_gelu_tanh(hid).astype(BF16), w2, preferred_element_type=F32)


def _compress_kernel(xk_ref, xv_ref, w1k_ref, w1v_ref, pek_ref, pev_ref, w2k_ref, w2v_ref,
                     gk_ref, pos_ref, inv_ref, sgn_ref, kc_o, vcT_o):
    c, s = _rope_tables(pos_ref[...], inv_ref[...], sgn_ref[...])
    k = _compress_one(xk_ref[0], w1k_ref[...], pek_ref[...], w2k_ref[...])
    kc_o[0] = _rope(_rms(k, gk_ref[...]), c, s).astype(BF16)
    v = _compress_one(xv_ref[0], w1v_ref[...], pev_ref[...], w2v_ref[...])
    vcT_o[0] = v.T.astype(BF16)


def _compress(xk, xv, w1k, w1v, pek, pev, w2k, w2v, gk, posc, inv2, sgn):
    G, n, W = xk.shape
    full = lambda a: pl.BlockSpec(a.shape, lambda g: (0,) * a.ndim)
    grp = pl.BlockSpec((1, n, W), lambda g: (g, 0, 0))
    return pl.pallas_call(
        _compress_kernel, grid=(G,),
        in_specs=[grp, grp, full(w1k), full(w1v), full(pek), full(pev), full(w2k), full(w2v),
                  full(gk), full(posc), full(inv2), full(sgn)],
        out_specs=[pl.BlockSpec((1, n, HEAD_DIM), lambda g: (g, 0, 0)),
                   pl.BlockSpec((1, HEAD_DIM, n), lambda g: (g, 0, 0))],
        out_shape=[jax.ShapeDtypeStruct((G, n, HEAD_DIM), BF16), jax.ShapeDtypeStruct((G, HEAD_DIM, n), BF16)],
        compiler_params=_params(("parallel",), 48), name="compress",
    )(xk, xv, w1k, w1v, pek, pev, w2k, w2v, gk, posc, inv2, sgn)


def _memkv_kernel(mem_ref, gm_ref, w_ref, gk_ref, k_o, vT_o):
    hn = _rms(mem_ref[...], gm_ref[...]).astype(BF16)
    kv = jnp.dot(hn, w_ref[...], preferred_element_type=F32)
    mw = MEM_HEADS * HEAD_DIM
    for h in range(MEM_HEADS):
        sl = slice(h * HEAD_DIM, (h + 1) * HEAD_DIM)
        k_o[:, sl] = _rms(kv[:, sl], gk_ref[...]).astype(BF16)
        vT_o[sl, :] = kv[:, mw + h * HEAD_DIM: mw + (h + 1) * HEAD_DIM].T.astype(BF16)


def _memkv(mem, gm, w, gk):
    M = mem.shape[0]
    mw = MEM_HEADS * HEAD_DIM
    return pl.pallas_call(
        _memkv_kernel,
        out_shape=[jax.ShapeDtypeStruct((M, mw), BF16), jax.ShapeDtypeStruct((mw, M), BF16)],
        compiler_params=pltpu.CompilerParams(vmem_limit_bytes=32 * MIB), name="mem_kv",
    )(mem, gm, w, gk)


def _cmp_kernel(q_ref, kc_ref, vcT_ref, ovT_ref, g_ref, *rest, tq, i0, n_pad, n_sel, top_k):
    y_ref, b_ref = rest[-2:]
    i = i0 + pl.program_id(1)
    t = i * tq + lax.broadcasted_iota(jnp.int32, (1, tq), 1)
    n_end = lax.broadcasted_iota(jnp.int32, (n_pad, 1), 0) * CMP_STRIDE + (CMP_LEN - 1)
    valid = n_end <= t
    kc = kc_ref[0]
    vcT = vcT_ref[0]
    has_valid = (t >= CMP_LEN - 1).astype(F32)
    sms = []
    for h in range(NSA_HPG):
        q_h = q_ref[:, h * HEAD_DIM:(h + 1) * HEAD_DIM]
        sms.append(jnp.where(valid, lax.dot_general(kc, q_h, _NT, preferred_element_type=F32), NEG_BIG))
    psum = jnp.zeros((n_pad, tq), F32)
    for h in range(NSA_HPG):
        e = jnp.exp2(sms[h] - jnp.max(sms[h], axis=0, keepdims=True))
        p = e * (has_valid / jnp.sum(e, axis=0, keepdims=True))
        oT = jnp.dot(vcT, p.astype(BF16), preferred_element_type=F32)
        y_ref[:, h * HEAD_DIM:(h + 1) * HEAD_DIM] = (oT * g_ref[0, 0, h:h + 1, :]).T.astype(BF16)
        psum = psum + p
    hi = psum.astype(BF16)
    r1 = psum - hi.astype(F32)
    mid = r1.astype(BF16)
    lo = (r1 - mid.astype(F32)).astype(BF16)
    parts = jnp.dot(ovT_ref[...], jnp.concatenate([hi, mid, lo], axis=1), preferred_element_type=F32)
    imp = parts[:, :tq] + parts[:, tq:2 * tq] + parts[:, 2 * tq:]
    s_i = lax.broadcasted_iota(jnp.int32, (n_sel, 1), 0)
    cur = lax.shift_right_logical(t, 6)
    forced = (s_i == 0) | (s_i == cur) | (s_i == cur - 1)
    future = s_i * SEL_BLOCK > t
    w = jnp.where(forced, jnp.inf, jnp.where(future, -jnp.inf, imp))
    s_f = jnp.broadcast_to(s_i.astype(F32), (n_sel, tq))
    for _ in range(top_k):
        m = jnp.max(w, axis=0, keepdims=True)
        idx = jnp.min(jnp.where(w == m, s_f, float(n_sel)), axis=0, keepdims=True)
        w = jnp.where(s_f == idx, -jnp.inf, w)
    b_ref[0, :n_sel, :] = jnp.where(future, NEG_BIG, jnp.where(w == -jnp.inf, 0.0, NEG_BIG))
    n_all = b_ref.shape[1]
    if n_sel < n_all:
        b_ref[0, n_sel:, :] = jnp.full((n_all - n_sel, tq), NEG_BIG, F32)


def _round_up(n, m):
    return -(-n // m) * m


def _cmp_select(qn, kc, vcT, ovT, gates):
    T = qn.shape[0]
    G, n_pad, _ = kc.shape
    n_sel = T // SEL_BLOCK
    tq = _tile(T, 512)
    nq = T // tq
    gw = NSA_HPG * HEAD_DIM
    nb = min(CMP_BUCKETS, nq)
    assert nq % nb == 0
    per = nq // nb
    out_shape = [jax.ShapeDtypeStruct((T, NSA_HEADS * HEAD_DIM), BF16), jax.ShapeDtypeStruct((G, n_sel, T), F32)]
    buffers = ()
    for b in range(nb):
        i0 = b * per
        t_max = (i0 + per) * tq - 1
        n_len = min(n_pad, _round_up(max(t_max - (CMP_LEN - 1), 0) // CMP_STRIDE + 1, LANE))
        s_len = min(n_sel, _round_up(t_max // SEL_BLOCK + 1, 8))
        kern = functools.partial(_cmp_kernel, tq=tq, i0=i0, n_pad=n_len, n_sel=s_len, top_k=min(SEL_TOPK, n_sel))
        buffers = pl.pallas_call(
            kern, grid=(G, per),
            in_specs=[pl.BlockSpec((tq, gw), lambda g, i, i0=i0: (i0 + i, g)),
                      pl.BlockSpec((1, n_len, HEAD_DIM), lambda g, i: (g, 0, 0)),
                      pl.BlockSpec((1, HEAD_DIM, n_len), lambda g, i: (g, 0, 0)),
                      pl.BlockSpec((s_len, n_len), lambda g, i: (0, 0)),
                      pl.BlockSpec((1, 1, 8, tq), lambda g, i, i0=i0: (0, g, 0, i0 + i))]
            + [pl.BlockSpec(memory_space=pl.ANY)] * len(buffers),
            out_specs=[pl.BlockSpec((tq, gw), lambda g, i, i0=i0: (i0 + i, g)),
                       pl.BlockSpec((1, n_sel, tq), lambda g, i, i0=i0: (g, 0, i0 + i))],
            out_shape=out_shape,
            input_output_aliases={5: 0, 6: 1} if buffers else {},
            compiler_params=_params(("parallel", "parallel"), 48), name="cmp_select_%d" % b,
        )(qn, kc, vcT, ovT, gates, *buffers)
    return buffers


def _flash_kernel(qi_ref, kj_ref, fl_ref, q_ref, k_ref, vT_ref, g_ref, sh_ref, *rest, mode, bounded, tq, tk):
    if mode == "sel":
        bias_ref, o_ref, m_sc, acc_sc = rest
    else:
        o_ref, m_sc, acc_sc = rest
    p = pl.program_id(1)
    i = qi_ref[p]
    j = kj_ref[p]
    fl = fl_ref[p]

    @pl.when((fl & 1) != 0)
    def _():
        m_sc[...] = jnp.full(m_sc.shape, NEG_BIG, F32)
        acc_sc[...] = jnp.zeros(acc_sc.shape, F32)

    shift = sh_ref[0, 0:1, :] if bounded else 0.0
    nb = tk // SEL_BLOCK
    if mode == "sel":
        rows = bias_ref[0] - shift

    def key_mask():
        t = i * tq + lax.broadcasted_iota(jnp.int32, (1, tq), 1)
        kpos = j * tk + lax.broadcasted_iota(jnp.int32, (tk, 1), 0)
        if mode == "sel":
            base = jnp.concatenate(
                [jnp.broadcast_to(rows[b:b + 1, :], (SEL_BLOCK, tq)) for b in range(nb)], axis=0)
            return jnp.where(kpos <= t, base, NEG_BIG)
        return jnp.where((kpos <= t) & (kpos > t - WINDOW), 0.0 - shift, NEG_BIG)

    k = k_ref[...]
    vT = vT_ref[...]
    raw = lambda h: lax.dot_general(k, q_ref[:, h * HEAD_DIM:(h + 1) * HEAD_DIM], _NT, preferred_element_type=F32)

    def accumulate(pTs):
        for h in range(NSA_HPG):
            acc_sc[h] = acc_sc[h] + jnp.dot(vT, pTs[h], preferred_element_type=F32)

    if bounded and mode == "sel":
        @pl.when((fl & 4) != 0)
        def _():
            mask_add = key_mask()
            accumulate([jnp.exp2(raw(h) + mask_add).astype(BF16) for h in range(NSA_HPG)])

        @pl.when((fl & 4) == 0)
        def _():
            def probs(h):
                s = raw(h).reshape(nb, SEL_BLOCK, tq) + rows[:, None, :]
                return jnp.exp2(s).astype(BF16).reshape(tk, tq)
            accumulate([probs(h) for h in range(NSA_HPG)])
    elif bounded:
        mask_add = key_mask()
        accumulate([jnp.exp2(raw(h) + mask_add).astype(BF16) for h in range(NSA_HPG)])
    else:
        mask_add = key_mask()
        sTs = [raw(h) + mask_add for h in range(NSA_HPG)]
        pTs, alphas = [], []
        for h in range(NSA_HPG):
            sT = sTs[h]
            m_old = m_sc[h]
            m_new = jnp.maximum(m_old, jnp.max(sT, axis=0, keepdims=True))
            alphas.append(jnp.exp2(m_old - m_new))
            pTs.append(jnp.exp2((sT - m_new).astype(BF16)))
            m_sc[h] = m_new
        for h in range(NSA_HPG):
            acc_sc[h] = alphas[h] * acc_sc[h] + jnp.dot(vT, pTs[h], preferred_element_type=F32)

    @pl.when((fl & 2) != 0)
    def _():
        for h in range(NSA_HPG):
            acc = acc_sc[h]
            l = acc[HEAD_DIM:HEAD_DIM + 1, :]
            o = acc[:HEAD_DIM, :] * ((1.0 / l) * g_ref[0, 0, h:h + 1, :])
            o_ref[:, h * HEAD_DIM:(h + 1) * HEAD_DIM] = o.T.astype(BF16)


def _steps(nq, lo_fn, hi_fn, reverse=False):
    qi, kj, fl = [], [], []
    for i in range(nq):
        js = list(range(lo_fn(i), hi_fn(i) + 1))
        if reverse:
            js = js[::-1]
        for n, j in enumerate(js):
            qi.append(i)
            kj.append(j)
            fl.append((1 if n == 0 else 0) | (2 if n == len(js) - 1 else 0))
    return (jnp.asarray(np.array(qi, np.int32)), jnp.asarray(np.array(kj, np.int32)),
            jnp.asarray(np.array(fl, np.int32)))


def _nsa_flash(mode, qn, k, vT, gates, shift, bias=None):
    use_bounded = jnp.max(shift) <= SOFTMAX_BOUND_MAX
    return lax.cond(use_bounded,
                    lambda: _nsa_flash_call(mode, True, qn, k, vT, gates, shift, bias),
                    lambda: _nsa_flash_call(mode, False, qn, k, vT, gates, shift, bias))


def _nsa_flash_call(mode, bounded, qn, k, vT, gates, shift, bias):
    T = qn.shape[0]
    tq = _tile(T, 1024 if mode == "sel" else WINDOW)
    tk = _tile(T, 1024 if mode == "sel" else 512)
    nq = T // tq
    gw = NSA_HPG * HEAD_DIM
    hi = lambda i: ((i + 1) * tq - 1) // tk
    if mode == "sel":
        lo = lambda i: 0
        br = 1
    else:
        lo = lambda i: max(0, (i * tq - (WINDOW - 1)) // tk)
        br = 2
    qi, kj, fl = _steps(nq, lo, hi)
    fl = fl | 4 * ((kj + 1) * tk - 1 > qi * tq).astype(jnp.int32)
    in_specs = [pl.BlockSpec((tq, gw), lambda g, p, qi, kj, fl: (qi[p], g)),
                pl.BlockSpec((tk, HEAD_DIM), lambda g, p, qi, kj, fl: (kj[p], g)),
                pl.BlockSpec((VAUG, tk), lambda g, p, qi, kj, fl: (g, kj[p])),
                pl.BlockSpec((1, 1, 8, tq), lambda g, p, qi, kj, fl, br=br: (br, g, 0, qi[p])),
                pl.BlockSpec((1, 8, tq), lambda g, p, qi, kj, fl: (g, 0, qi[p]))]
    args = [qn, k, vT, gates, shift]
    if mode == "sel":
        in_specs.append(pl.BlockSpec((1, tk // SEL_BLOCK, tq), lambda g, p, qi, kj, fl: (g, kj[p], qi[p])))
        args.append(bias)
    kern = functools.partial(_flash_kernel, mode=mode, bounded=bounded, tq=tq, tk=tk)
    return pl.pallas_call(
        kern,
        grid_spec=pltpu.PrefetchScalarGridSpec(
            num_scalar_prefetch=3, grid=(NSA_GROUPS, int(qi.shape[0])),
            in_specs=in_specs,
            out_specs=pl.BlockSpec((tq, gw), lambda g, p, qi, kj, fl: (qi[p], g)),
            scratch_shapes=[pltpu.VMEM((NSA_HPG, 1, tq), F32), pltpu.VMEM((NSA_HPG, VAUG, tq), F32)]),
        out_shape=jax.ShapeDtypeStruct((T, NSA_HEADS * HEAD_DIM), BF16),
        compiler_params=_params(("parallel", "arbitrary"), 48),
        name="nsa_" + mode + ("_bounded" if bounded else "_online"),
    )(qi, kj, fl, *args)


def _sb_kernel(qi_ref, kj_ref, fl_ref, q_ref, k_ref, vT_ref, lm_ref, *rest, tq, tk, resume):
    if resume:
        acc_in, carry_in, o_ref, carry_sc, acc_sc = rest
    else:
        o_ref, acc_o, carry_o, carry_sc, acc_sc = rest
    p = pl.program_id(0)
    i = qi_ref[p]
    j = kj_ref[p]
    fl = fl_ref[p]

    @pl.when((fl & 1) != 0)
    def _():
        if resume:
            for h in range(SB_HEADS):
                carry_sc[h] = carry_in[h:h + 1, :]
                acc_sc[h] = acc_in[h * HEAD_DIM:(h + 1) * HEAD_DIM, :]
        else:
            carry_sc[...] = jnp.zeros(carry_sc.shape, F32)
            acc_sc[...] = jnp.zeros(acc_sc.shape, F32)

    @pl.when((jnp.max(carry_sc[...]) > SB_DEAD_LOG2) & ((fl & 4) == 0))
    def _():
        t = i * tq + lax.broadcasted_iota(jnp.int32, (1, tq), 1)
        kpos = j * tk + lax.broadcasted_iota(jnp.int32, (tk, 1), 0)
        past = kpos < t
        lm = lm_ref[...]
        hs = [slice(h * HEAD_DIM, (h + 1) * HEAD_DIM) for h in range(SB_HEADS)]
        zs = [lax.dot_general(k_ref[:, sl], q_ref[:, sl], _NT, preferred_element_type=F32) for sl in hs]
        lgs, css = [], []
        for h in range(SB_HEADS):
            z = zs[h]
            sp = jnp.maximum(z, 0.0) + jnp.log2(1.0 + jnp.exp2(-jnp.abs(z)))
            lk = jnp.where(past, -sp, 0.0)
            hi = lk.astype(BF16)
            lo = (lk - hi.astype(F32)).astype(BF16)
            css.append(jnp.dot(lm, jnp.concatenate([hi, lo], axis=1), preferred_element_type=F32))
            lgs.append(z - sp)
            carry_old = carry_sc[h]
            carry_sc[h] = carry_old + jnp.sum(lk, axis=0, keepdims=True)
            css[h] = css[h][:, :tq] + css[h][:, tq:] + carry_old
        for h in range(SB_HEADS):
            wgt = jnp.where(past, jnp.exp2(lgs[h] + css[h]), 0.0)
            acc_sc[h] = acc_sc[h] + jnp.dot(vT_ref[hs[h], :], wgt.astype(BF16), preferred_element_type=F32)

    @pl.when((fl & 2) != 0)
    def _():
        for h in range(SB_HEADS):
            o_ref[:, h * HEAD_DIM:(h + 1) * HEAD_DIM] = acc_sc[h].T.astype(BF16)
            if not resume:
                acc_o[h * HEAD_DIM:(h + 1) * HEAD_DIM, :] = acc_sc[h]
                carry_o[h:h + 1, :] = carry_sc[h]
        if not resume:
            carry_o[SB_HEADS:, :] = jnp.zeros((8 - SB_HEADS, tq), F32)


def _sb_call(q, k, kcol, vT, steps, tq, tk, state=None):
    T, W = q.shape
    qi, kj, fl = steps
    lmat = jnp.asarray(np.triu(np.ones((tk, tk), np.float32), 1), BF16)
    resume = state is not None
    qtile = lambda shape: pl.BlockSpec(shape, lambda p, qi, kj, fl: (qi[p], 0))
    qtileT = lambda rows: pl.BlockSpec((rows, tq), lambda p, qi, kj, fl: (0, qi[p]))
    in_specs = [qtile((tq, W)),
                pl.BlockSpec((tk, W), lambda p, qi, kj, fl: (kj[p], kcol)),
                pl.BlockSpec((W, tk), lambda p, qi, kj, fl: (0, kj[p])),
                pl.BlockSpec((tk, tk), lambda p, qi, kj, fl: (0, 0))]
    y_sds = jax.ShapeDtypeStruct((T, W), BF16)
    if resume:
        in_specs += [qtileT(W), qtileT(8)]
        out_specs, out_shape = qtile((tq, W)), y_sds
    else:
        out_specs = [qtile((tq, W)), qtileT(W), qtileT(8)]
        out_shape = [y_sds, jax.ShapeDtypeStruct((W, T), F32), jax.ShapeDtypeStruct((8, T), F32)]
    return pl.pallas_call(
        functools.partial(_sb_kernel, tq=tq, tk=tk, resume=resume),
        grid_spec=pltpu.PrefetchScalarGridSpec(
            num_scalar_prefetch=3, grid=(int(qi.shape[0]),),
            in_specs=in_specs, out_specs=out_specs,
            scratch_shapes=[pltpu.VMEM((SB_HEADS, 1, tq), F32), pltpu.VMEM((SB_HEADS, HEAD_DIM, tq), F32)]),
        out_shape=out_shape,
        compiler_params=_params(("arbitrary",), 48), name="sb_far" if resume else "sb_near",
    )(qi, kj, fl, q, k, vT, lmat, *(state or ()))


def _stick_breaking(q, k, kcol, vT):
    T, W = q.shape
    tq = _tile(T, 256)
    tk = _tile(T, 256)
    nq = T // tq
    hi = lambda i: ((i + 1) * tq - 2) // tk
    lo_near = lambda i: max(0, hi(i) - SB_NEAR_TILES + 1)
    near = _steps(nq, lo_near, hi, reverse=True)
    y_near, acc, carry = _sb_call(q, k, kcol, vT, near, tq, tk)
    qi, kj, fl = [], [], []
    for i in range(nq):
        js = list(range(lo_near(i) - 1, -1, -1))
        for n, j in enumerate(js or [0]):
            qi.append(i)
            kj.append(j)
            fl.append((1 if n == 0 else 0) | (2 if n == max(len(js), 1) - 1 else 0) | (0 if js else 4))
    far = tuple(jnp.asarray(np.array(a, np.int32)) for a in (qi, kj, fl))
    with_far = [i for i in range(nq) if lo_near(i) > 0]
    if not with_far:
        return y_near
    alive = jnp.max(carry[:SB_HEADS, with_far[0] * tq:]) > SB_DEAD_LOG2
    return lax.cond(alive, lambda: _sb_call(q, k, kcol, vT, far, tq, tk, state=(acc, carry)), lambda: y_near)


def _memattn_kernel(q_ref, k_ref, vT_ref, o_ref):
    for h in range(MEM_HEADS):
        sl = slice(h * HEAD_DIM, (h + 1) * HEAD_DIM)
        sT = lax.dot_general(k_ref[:, sl], q_ref[:, sl], _NT, preferred_element_type=F32)
        e = jnp.exp2(sT - jnp.max(sT, axis=0, keepdims=True))
        l = jnp.sum(e, axis=0, keepdims=True)
        oT = jnp.dot(vT_ref[sl, :], e.astype(BF16), preferred_element_type=F32) * (1.0 / l)
        o_ref[:, sl] = oT.T.astype(BF16)


def _mem_attention(q, k, vT):
    T, W = q.shape
    M = k.shape[0]
    tq = _tile(T, 1024)
    return pl.pallas_call(
        _memattn_kernel, grid=(T // tq,),
        in_specs=[pl.BlockSpec((tq, W), lambda i: (i, 0)),
                  pl.BlockSpec((M, W), lambda i: (0, 0)),
                  pl.BlockSpec((W, M), lambda i: (0, 0))],
        out_specs=pl.BlockSpec((tq, W), lambda i: (i, 0)),
        out_shape=jax.ShapeDtypeStruct((T, W), BF16),
        compiler_params=_params(("parallel",), 32), name="mem_attention",
    )(q, k, vT)


def _mixout_kernel(yc_ref, ys_ref, yw_ref, ysb_ref, ym_ref, g_ref, wn_ref, wsb_ref, wm_ref, wo_ref, x_ref, gain_ref,
                   x2_o, h2_o):
    D = x_ref.shape[1]
    yn = (yc_ref[...].astype(F32) + ys_ref[...].astype(F32) + yw_ref[...].astype(F32)).astype(BF16)
    a = jnp.dot(yn, wn_ref[...], preferred_element_type=F32)
    b = jnp.dot(ysb_ref[...], wsb_ref[...], preferred_element_type=F32)
    c = jnp.dot(ym_ref[...], wm_ref[...], preferred_element_type=F32)
    sig = lambda n: jax.nn.sigmoid(g_ref[:, n * D:(n + 1) * D].astype(F32))
    mixed = (sig(0) * a + sig(1) * b + sig(2) * c).astype(BF16)
    x2 = x_ref[...] + jnp.dot(mixed, wo_ref[...], preferred_element_type=F32)
    x2_o[...] = x2
    h2_o[...] = _rms(x2, gain_ref[...]).astype(BF16)


def _mix_out(yc, ys, yw, ysb, ym, P, wn, wsb, wm, wo, x, gain):
    T, D = x.shape
    tm = _tile(T, 256)
    rowi = lambda w: pl.BlockSpec((tm, w), lambda i: (i, 0))
    res = lambda a: pl.BlockSpec(a.shape, lambda i: (0, 0), pipeline_mode=pl.Buffered(1))
    return pl.pallas_call(
        _mixout_kernel, grid=(T // tm,),
        in_specs=[rowi(yc.shape[1]), rowi(ys.shape[1]), rowi(yw.shape[1]), rowi(ysb.shape[1]), rowi(ym.shape[1]),
                  rowi(N_BRANCH * D), res(wn), res(wsb), res(wm), res(wo), rowi(D), res(gain)],
        out_specs=[rowi(D), rowi(D)],
        out_shape=[jax.ShapeDtypeStruct((T, D), F32), jax.ShapeDtypeStruct((T, D), BF16)],
        compiler_params=_params(("parallel",), 48), name="mix_out",
    )(yc, ys, yw, ysb, ym, P, wn, wsb, wm, wo, x, gain)


def _ffn_kernel(h_ref, wg_ref, wu_ref, wd_ref, x_ref, o_ref, z_sc, *, nf):
    f = pl.program_id(1)

    def up():
        a = jnp.dot(h_ref[...], wg_ref[...], preferred_element_type=F32)
        b = jnp.dot(h_ref[...], wu_ref[...], preferred_element_type=F32)
        return (a * jax.nn.sigmoid(a) * b).astype(BF16)

    def down():
        return jnp.dot(z_sc[...], wd_ref[...], preferred_element_type=F32)

    @pl.when(f == 0)
    def _():
        z_sc[...] = up()

    @pl.when((f > 0) & (f < nf))
    def _():
        c = down()
        z_new = up()
        o_ref[...] = jnp.where(f == 1, x_ref[...], o_ref[...]) + c
        z_sc[...] = z_new

    @pl.when(f == nf)
    def _():
        o_ref[...] += down()


def _ffn(h2, wg, wu, wd, x2):
    T, D = x2.shape
    F = wg.shape[1]
    tm = _tile(T, 1024)
    tf = 512
    assert F % tf == 0
    nf = F // tf
    assert nf >= 2
    return pl.pallas_call(
        functools.partial(_ffn_kernel, nf=nf), grid=(T // tm, nf + 1),
        in_specs=[pl.BlockSpec((tm, D), lambda i, f: (i, 0)),
                  pl.BlockSpec((D, tf), lambda i, f: (0, jnp.minimum(f, nf - 1))),
                  pl.BlockSpec((D, tf), lambda i, f: (0, jnp.minimum(f, nf - 1))),
                  pl.BlockSpec((tf, D), lambda i, f: (jnp.maximum(f - 1, 0), 0)),
                  pl.BlockSpec((tm, D), lambda i, f: (i, 0), pipeline_mode=pl.Buffered(1))],
        out_specs=pl.BlockSpec((tm, D), lambda i, f: (i, 0)),
        out_shape=jax.ShapeDtypeStruct((T, D), F32),
        scratch_shapes=[pltpu.VMEM((tm, tf), BF16)],
        compiler_params=_params(("parallel", "arbitrary"), 56), name="ffn",
    )(h2, wg.astype(BF16), wu.astype(BF16), wd.astype(BF16), x2)


def _layer(x, mem, pos_col, posc_col, consts, layer, attn_norm, w_in_all, nsa_q_norm, nsa_kc_norm, nsa_ks_norm, nsa_kw_norm,
           cmp_k_pe, cmp_k_w1, cmp_k_w2, cmp_v_pe, cmp_v_w1, cmp_v_w2, mem_norm, w_mem_kv,
           mem_q_norm, mem_k_norm, w_o_nsa, w_o_sb, w_o_mem, w_out, ffn_norm,
           w_ffn_gate, w_ffn_up, w_ffn_down):
    T, D = x.shape
    inv2, sgn, ovT = consts
    hd = HEAD_DIM
    row = lambda g: g.reshape(1, -1)

    q_w, kv_w, gn_w = NSA_HEADS * hd, 6 * NSA_GROUPS * hd, 3 * NSA_HEADS
    sb_w, mq_w, gm_w = 3 * SB_HEADS * hd, MEM_HEADS * hd, N_BRANCH * D
    o_gn = q_w + kv_w
    o_sb = o_gn + gn_w
    o_mq = o_sb + sb_w
    o_gm = o_mq + mq_w
    assert w_in_all.shape[2] == o_gm + gm_w
    wt = jnp.swapaxes(w_in_all[layer], 0, 1)
    wt_main = jnp.concatenate([wt[o_gm:], wt[:o_gn], wt[o_sb:o_gm]], axis=0).astype(BF16)
    wt_gate = jnp.pad(wt[o_gn:o_sb], ((0, LANE - gn_w), (0, 0))).astype(BF16)
    P, Pg = _in_proj(x, row(attn_norm), wt_main, wt_gate)

    (qn, ksn, kwn, vsT, vwT, sbq, sbvT, memq, gT, qn2, kn2) = _prep(
        P, Pg, gm_w, pos_col, inv2, sgn, row(nsa_q_norm), row(nsa_ks_norm), row(nsa_kw_norm), row(mem_q_norm))

    q2 = jnp.max(qn2[::8].reshape(NSA_GROUPS, NSA_HPG, T), axis=1)
    k2 = jnp.max(kn2[::8], axis=1).reshape(2, NSA_GROUPS)
    bound = lambda kk: jnp.broadcast_to((1.02 * jnp.sqrt(q2 * kk[:, None]))[:, None, :], (NSA_GROUPS, 8, T))
    shift_sel, shift_win = bound(k2[0]), bound(k2[1])

    gates = gT[:gn_w].reshape(NSA_GROUPS, NSA_HPG, 3, T).transpose(2, 0, 1, 3)
    gates = jnp.pad(gates, ((0, 0), (0, 0), (0, 8 - NSA_HPG), (0, 0)))

    n_pad = T // CMP_STRIDE
    half = CMP_LEN // 2

    def w1_pack(w1):
        return jnp.concatenate([w1[:half].reshape(half * hd, -1), w1[half:].reshape(half * hd, -1)], axis=1).astype(BF16)

    def pe_pack(pe):
        return jnp.pad(pe.reshape(2, half * hd), ((0, 6), (0, 0))).astype(BF16)

    def chunks(col):
        return (P[:, col:col + NSA_GROUPS * hd].reshape(T, NSA_GROUPS, hd).transpose(1, 0, 2)
                .reshape(NSA_GROUPS, n_pad, CMP_STRIDE * hd))

    kc, vcT = _compress(
        chunks(gm_w + q_w), chunks(gm_w + q_w + NSA_GROUPS * hd),
        w1_pack(cmp_k_w1), w1_pack(cmp_v_w1), pe_pack(cmp_k_pe), pe_pack(cmp_v_pe),
        cmp_k_w2.astype(BF16), cmp_v_w2.astype(BF16), row(nsa_kc_norm), posc_col, inv2, sgn)

    y_cmp, bias = _cmp_select(qn, kc, vcT, ovT, gates)
    y_sel = _nsa_flash("sel", qn, ksn, vsT, gates, shift_sel, bias)
    y_win = _nsa_flash("win", qn, kwn, vwT, gates, shift_win)
    sbk_col = gm_w + q_w + kv_w + SB_HEADS * hd
    assert sbk_col % (SB_HEADS * hd) == 0
    y_sb = _stick_breaking(sbq, P, sbk_col // (SB_HEADS * hd), sbvT)

    mk, mvT = _memkv(mem, row(mem_norm), w_mem_kv.astype(BF16), row(mem_k_norm))
    y_mem = _mem_attention(memq, mk, mvT)

    x2, h2 = _mix_out(y_cmp, y_sel, y_win, y_sb, y_mem, P, w_o_nsa.astype(BF16), w_o_sb.astype(BF16),
                      w_o_mem.astype(BF16), w_out.astype(BF16), x, row(ffn_norm))
    return _ffn(h2, w_ffn_gate, w_ffn_up, w_ffn_down, x2)


def kernel(x, mem, positions, attn_norm, w_in, nsa_q_norm, nsa_kc_norm, nsa_ks_norm, nsa_kw_norm, cmp_k_pe, cmp_k_w1, cmp_k_w2, cmp_v_pe, cmp_v_w1, cmp_v_w2, mem_norm, w_mem_kv, mem_q_norm, mem_k_norm, w_o_nsa, w_o_sb, w_o_mem, w_out, ffn_norm, w_ffn_gate, w_ffn_up, w_ffn_down):
    B, T, D = x.shape
    assert T % (4 * LANE) == 0 and T // SEL_BLOCK >= 8
    n_pad = T // CMP_STRIDE
    n_sel = T // SEL_BLOCK
    inv = 1.0 / (ROPE_THETA ** (jnp.arange(0, HEAD_DIM, 2, dtype=F32) / HEAD_DIM))
    inv2 = jnp.concatenate([inv, inv]).reshape(1, HEAD_DIM)
    sgn = jnp.concatenate([-jnp.ones((HEAD_DIM // 2,), F32), jnp.ones((HEAD_DIM // 2,), F32)]).reshape(1, HEAD_DIM)
    cs = np.arange(n_pad)[None, :] * CMP_STRIDE
    ss = np.arange(n_sel)[:, None] * SEL_BLOCK
    ovT = jnp.asarray(((cs < ss + SEL_BLOCK) & (cs + CMP_LEN - 1 >= ss)).astype(np.float32), BF16)
    consts = (inv2, sgn, ovT)
    depth = w_in.shape[0]
    outs = []
    for b in range(B):
        xb = x[b]
        posf = positions[b].astype(F32)
        pos_col = posf.reshape(T, 1)
        posc = jnp.concatenate([posf[CMP_LEN - 1::CMP_STRIDE], posf[-1:]]).reshape(n_pad, 1)
        for l in range(depth):
            xb = _layer(xb, mem[b], pos_col, posc, consts, l, attn_norm[l], w_in, nsa_q_norm[l], nsa_kc_norm[l],
                        nsa_ks_norm[l], nsa_kw_norm[l], cmp_k_pe[l], cmp_k_w1[l], cmp_k_w2[l], cmp_v_pe[l],
                        cmp_v_w1[l], cmp_v_w2[l], mem_norm[l], w_mem_kv[l], mem_q_norm[l], mem_k_norm[l],
                        w_o_nsa[l], w_o_sb[l], w_o_mem[l], w_out[l], ffn_norm[l],
                        w_ffn_gate[l], w_ffn_up[l], w_ffn_down[l])
        outs.append(xb)
    return outs[0][None] if B == 1 else jnp.stack(outs, axis=0)
```

```python
import functools

import numpy as np
import jax
import jax.numpy as jnp
from jax import lax
from jax.experimental import pallas as pl
from jax.experimental.pallas import tpu as pltpu

HEAD_DIM = 128
NSA_HEADS = 8
NSA_GROUPS = 2
NSA_HPG = NSA_HEADS // NSA_GROUPS
SB_HEADS = 4
MEM_HEADS = 4
CMP_LEN = 32
CMP_STRIDE = 16
CMP_HIDDEN = 2 * HEAD_DIM
SEL_BLOCK = 64
SEL_TOPK = 16
WINDOW = 512
ROPE_THETA = 10000.0
NORM_EPS = 1e-6
NEG_BIG = -1e30
N_BRANCH = 3
SCALE = HEAD_DIM ** -0.5
LOG2E = 1.4426950408889634
QSCALE = SCALE * LOG2E
SOFTMAX_BOUND_MAX = 50.0
CMP_BUCKETS = 8
SB_NEAR_TILES = 2
SB_DEAD_LOG2 = -160.0
VAUG = HEAD_DIM + 16

LANE = 128
MIB = 1 << 20
BF16 = jnp.bfloat16
F32 = jnp.float32

_NT = (((1,), (1,)), ((), ()))


def _tile(n, pref):
    t = min(n, pref)
    assert n % t == 0, (n, pref)
    return t


def _params(sem, vmem_mib):
    return pltpu.CompilerParams(dimension_semantics=sem, vmem_limit_bytes=vmem_mib * MIB)


def _rms(x, gain):
    return x * lax.rsqrt(jnp.mean(x * x, axis=-1, keepdims=True) + NORM_EPS) * gain


def _rope_tables(pos, inv2, sgn):
    ang = pos * inv2
    return jnp.cos(ang), jnp.sin(ang) * sgn


def _rope(x, c, s):
    return x * c + pltpu.roll(x, HEAD_DIM // 2, 1) * s


def _proj_kernel(x_ref, g_ref, w_ref, wg_ref, o_ref, og_ref, hn_ref):
    @pl.when(pl.program_id(1) == 0)
    def _():
        hn_ref[...] = _rms(x_ref[...], g_ref[...]).astype(BF16)
        og_ref[...] = lax.dot_general(hn_ref[...], wg_ref[...], _NT, preferred_element_type=F32)

    o_ref[...] = lax.dot_general(hn_ref[...], w_ref[...], _NT, preferred_element_type=F32).astype(BF16)


def _in_proj(x, gain, wt_main, wt_gate):
    T, D = x.shape
    N = wt_main.shape[0]
    tm = _tile(T, 1024)
    tn = 1792
    assert N % tn == 0
    return pl.pallas_call(
        _proj_kernel,
        grid=(T // tm, N // tn),
        in_specs=[
            pl.BlockSpec((tm, D), lambda i, j: (i, 0)),
            pl.BlockSpec((1, D), lambda i, j: (0, 0)),
            pl.BlockSpec((tn, D), lambda i, j: (j, 0)),
            pl.BlockSpec((LANE, D), lambda i, j: (0, 0)),
        ],
        out_specs=[pl.BlockSpec((tm, tn), lambda i, j: (i, j)), pl.BlockSpec((tm, LANE), lambda i, j: (i, 0))],
        out_shape=[jax.ShapeDtypeStruct((T, N), BF16), jax.ShapeDtypeStruct((T, LANE), F32)],
        scratch_shapes=[pltpu.VMEM((tm, D), BF16)],
        compiler_params=_params(("parallel", "arbitrary"), 52),
        name="in_proj",
    )(x, gain, wt_main, wt_gate)


def _prep_kernel(pos_ref, inv_ref, sgn_ref, gq_ref, gks_ref, gkw_ref, gmq_ref,
                 q_ref, ks_ref, vs_ref, kw_ref, vw_ref,
                 sq_ref, sv_ref, mq_ref, gn_ref,
                 qn_o, ks_o, kw_o, vsT_o, vwT_o, sq_o, svT_o, mq_o, gT_o, qn2_o, kn2_o):
    c, s = _rope_tables(pos_ref[...], inv_ref[...], sgn_ref[...])
    hd = HEAD_DIM
    tp = pos_ref.shape[0]
    f32 = lambda ref, sl: ref[:, sl].astype(F32)
    ones8 = jnp.ones((8, hd), F32)

    def sqnorm_rows(xb):
        x = xb.astype(F32)
        return lax.dot_general(ones8, x * x, _NT, preferred_element_type=F32)

    for h in range(NSA_HEADS):
        sl = slice(h * hd, (h + 1) * hd)
        qb = (_rope(_rms(f32(q_ref, sl), gq_ref[...]), c, s) * QSCALE).astype(BF16)
        qn_o[:, sl] = qb
        qn2_o[h * 8:(h + 1) * 8, :] = sqnorm_rows(qb)
    ones_rows = (lax.broadcasted_iota(jnp.int32, (VAUG - hd, tp), 0) == 0).astype(F32).astype(BF16)
    for g in range(NSA_GROUPS):
        sl = slice(g * hd, (g + 1) * hd)
        ksb = _rope(_rms(f32(ks_ref, sl), gks_ref[...]), c, s).astype(BF16)
        kwb = _rope(_rms(f32(kw_ref, sl), gkw_ref[...]), c, s).astype(BF16)
        ks_o[:, sl] = ksb
        kw_o[:, sl] = kwb
        kn2_o[g * 8:(g + 1) * 8, :] = sqnorm_rows(ksb)
        kn2_o[(NSA_GROUPS + g) * 8:(NSA_GROUPS + g + 1) * 8, :] = sqnorm_rows(kwb)
        vsT_o[g * VAUG:g * VAUG + hd, :] = f32(vs_ref, sl).T.astype(BF16)
        vsT_o[g * VAUG + hd:(g + 1) * VAUG, :] = ones_rows
        vwT_o[g * VAUG:g * VAUG + hd, :] = f32(vw_ref, sl).T.astype(BF16)
        vwT_o[g * VAUG + hd:(g + 1) * VAUG, :] = ones_rows
    for h in range(SB_HEADS):
        sl = slice(h * hd, (h + 1) * hd)
        sq_o[:, sl] = (f32(sq_ref, sl) * QSCALE).astype(BF16)
        svT_o[sl, :] = f32(sv_ref, sl).T.astype(BF16)
    for h in range(MEM_HEADS):
        sl = slice(h * hd, (h + 1) * hd)
        mq_o[:, sl] = (_rms(f32(mq_ref, sl), gmq_ref[...]) * QSCALE).astype(BF16)
    gT_o[...] = jax.nn.sigmoid(gn_ref[...]).T


def _prep(P, Pg, c0, pos_col, inv2, sgn, gq, gks, gkw, gmq):
    T = P.shape[0]
    tp = _tile(T, 1024)
    hd = HEAD_DIM
    row = lambda w, c: pl.BlockSpec((tp, w), lambda i, c=c: (i, c))
    const = lambda: pl.BlockSpec((1, hd), lambda i: (0, 0))
    assert c0 % (8 * hd) == 0
    at = lambda w, col: row(w, (c0 + col) // w)
    in_specs = [pl.BlockSpec((tp, 1), lambda i: (i, 0)), const(), const(), const(), const(), const(), const(),
                at(8 * hd, 0),
                at(2 * hd, 12 * hd), at(2 * hd, 14 * hd),
                at(2 * hd, 16 * hd), at(2 * hd, 18 * hd),
                at(4 * hd, 20 * hd), at(4 * hd, 28 * hd),
                at(4 * hd, 32 * hd),
                row(hd, 0)]
    colT = lambda w: pl.BlockSpec((w, tp), lambda i: (0, i))
    out_specs = [row(8 * hd, 0), row(2 * hd, 0),
                 row(2 * hd, 0), colT(NSA_GROUPS * VAUG), colT(NSA_GROUPS * VAUG),
                 row(4 * hd, 0), colT(4 * hd), row(4 * hd, 0), colT(hd),
                 colT(NSA_HEADS * 8), colT(2 * NSA_GROUPS * 8)]
    sds = jax.ShapeDtypeStruct
    out_shape = [sds((T, 8 * hd), BF16), sds((T, 2 * hd), BF16),
                 sds((T, 2 * hd), BF16),
                 sds((NSA_GROUPS * VAUG, T), BF16), sds((NSA_GROUPS * VAUG, T), BF16),
                 sds((T, 4 * hd), BF16), sds((4 * hd, T), BF16), sds((T, 4 * hd), BF16),
                 sds((hd, T), F32), sds((NSA_HEADS * 8, T), F32), sds((2 * NSA_GROUPS * 8, T), F32)]
    return pl.pallas_call(
        _prep_kernel, grid=(T // tp,), in_specs=in_specs, out_specs=out_specs, out_shape=out_shape,
        compiler_params=_params(("parallel",), 48), name="prep",
    )(pos_col, inv2, sgn, gq, gks, gkw, gmq, *([P] * 8), Pg)


def _gelu_tanh(x):
    return 0.5 * x * (1.0 + jnp.tanh(0.7978845608028654 * (x + 0.044715 * (x * x * x))))


def _compress_one(x, w1, pe, w2):
    n = x.shape[0]
    ab = jnp.dot(x, w1, preferred_element_type=F32)
    pr = jnp.dot(pe, w1, preferred_element_type=F32)
    pec = pr[0:1, :CMP_HIDDEN] + pr[1:2, CMP_HIDDEN:]
    hid = ab[:, :CMP_HIDDEN] + pltpu.roll(ab[:, CMP_HIDDEN:], n - 1, 0) + pec
    return jnp.dot(_gelu_tanh(hid).astype(BF16), w2, preferred_element_type=F32)


def _compress_kernel(xk_ref, xv_ref, w1k_ref, w1v_ref, pek_ref, pev_ref, w2k_ref, w2v_ref,
                     gk_ref, pos_ref, inv_ref, sgn_ref, kc_o, vcT_o):
    c, s = _rope_tables(pos_ref[...], inv_ref[...], sgn_ref[...])
    k = _compress_one(xk_ref[0], w1k_ref[...], pek_ref[...], w2k_ref[...])
    kc_o[0] = _rope(_rms(k, gk_ref[...]), c, s).astype(BF16)
    v = _compress_one(xv_ref[0], w1v_ref[...], pev_ref[...], w2v_ref[...])
    vcT_o[0] = v.T.astype(BF16)


def _compress(xk, xv, w1k, w1v, pek, pev, w2k, w2v, gk, posc, inv2, sgn):
    G, n, W = xk.shape
    full = lambda a: pl.BlockSpec(a.shape, lambda g: (0,) * a.ndim)
    grp = pl.BlockSpec((1, n, W), lambda g: (g, 0, 0))
    return pl.pallas_call(
        _compress_kernel, grid=(G,),
        in_specs=[grp, grp, full(w1k), full(w1v), full(pek), full(pev), full(w2k), full(w2v),
                  full(gk), full(posc), full(inv2), full(sgn)],
        out_specs=[pl.BlockSpec((1, n, HEAD_DIM), lambda g: (g, 0, 0)),
                   pl.BlockSpec((1, HEAD_DIM, n), lambda g: (g, 0, 0))],
        out_shape=[jax.ShapeDtypeStruct((G, n, HEAD_DIM), BF16), jax.ShapeDtypeStruct((G, HEAD_DIM, n), BF16)],
        compiler_params=_params(("parallel",), 48), name="compress",
    )(xk, xv, w1k, w1v, pek, pev, w2k, w2v, gk, posc, inv2, sgn)


def _memkv_kernel(mem_ref, gm_ref, w_ref, gk_ref, k_o, vT_o):
    hn = _rms(mem_ref[...], gm_ref[...]).astype(BF16)
    kv = jnp.dot(hn, w_ref[...], preferred_element_type=F32)
    mw = MEM_HEADS * HEAD_DIM
    for h in range(MEM_HEADS):
        sl = slice(h * HEAD_DIM, (h + 1) * HEAD_DIM)
        k_o[:, sl] = _rms(kv[:, sl], gk_ref[...]).astype(BF16)
        vT_o[sl, :] = kv[:, mw + h * HEAD_DIM: mw + (h + 1) * HEAD_DIM].T.astype(BF16)


def _memkv(mem, gm, w, gk):
    M = mem.shape[0]
    mw = MEM_HEADS * HEAD_DIM
    return pl.pallas_call(
        _memkv_kernel,
        out_shape=[jax.ShapeDtypeStruct((M, mw), BF16), jax.ShapeDtypeStruct((mw, M), BF16)],
        compiler_params=pltpu.CompilerParams(vmem_limit_bytes=32 * MIB), name="mem_kv",
    )(mem, gm, w, gk)


def _cmp_kernel(q_ref, kc_ref, vcT_ref, ovT_ref, g_ref, *rest, tq, i0, n_pad, n_sel, top_k):
    y_ref, b_ref = rest[-2:]
    i = i0 + pl.program_id(1)
    t = i * tq + lax.broadcasted_iota(jnp.int32, (1, tq), 1)
    n_end = lax.broadcasted_iota(jnp.int32, (n_pad, 1), 0) * CMP_STRIDE + (CMP_LEN - 1)
    valid = n_end <= t
    kc = kc_ref[0]
    vcT = vcT_ref[0]
    has_valid = (t >= CMP_LEN - 1).astype(F32)
    sms = []
    for h in range(NSA_HPG):
        q_h = q_ref[:, h * HEAD_DIM:(h + 1) * HEAD_DIM]
        sms.append(jnp.where(valid, lax.dot_general(kc, q_h, _NT, preferred_element_type=F32), NEG_BIG))
    psum = jnp.zeros((n_pad, tq), F32)
    for h in range(NSA_HPG):
        e = jnp.exp2(sms[h] - jnp.max(sms[h], axis=0, keepdims=True))
        p = e * (has_valid / jnp.sum(e, axis=0, keepdims=True))
        oT = jnp.dot(vcT, p.astype(BF16), preferred_element_type=F32)
        y_ref[:, h * HEAD_DIM:(h + 1) * HEAD_DIM] = (oT * g_ref[0, 0, h:h + 1, :]).T.astype(BF16)
        psum = psum + p
    hi = psum.astype(BF16)
    r1 = psum - hi.astype(F32)
    mid = r1.astype(BF16)
    lo = (r1 - mid.astype(F32)).astype(BF16)
    parts = jnp.dot(ovT_ref[...], jnp.concatenate([hi, mid, lo], axis=1), preferred_element_type=F32)
    imp = parts[:, :tq] + parts[:, tq:2 * tq] + parts[:, 2 * tq:]
    s_i = lax.broadcasted_iota(jnp.int32, (n_sel, 1), 0)
    cur = lax.shift_right_logical(t, 6)
    forced = (s_i == 0) | (s_i == cur) | (s_i == cur - 1)
    future = s_i * SEL_BLOCK > t
    w = jnp.where(forced, jnp.inf, jnp.where(future, -jnp.inf, imp))
    s_f = jnp.broadcast_to(s_i.astype(F32), (n_sel, tq))
    for _ in range(top_k):
        m = jnp.max(w, axis=0, keepdims=True)
        idx = jnp.min(jnp.where(w == m, s_f, float(n_sel)), axis=0, keepdims=True)
        w = jnp.where(s_f == idx, -jnp.inf, w)
    b_ref[0, :n_sel, :] = jnp.where(future, NEG_BIG, jnp.where(w == -jnp.inf, 0.0, NEG_BIG))
    n_all = b_ref.shape[1]
    if n_sel < n_all:
        b_ref[0, n_sel:, :] = jnp.full((n_all - n_sel, tq), NEG_BIG, F32)


def _round_up(n, m):
    return -(-n // m) * m


def _cmp_select(qn, kc, vcT, ovT, gates):
    T = qn.shape[0]
    G, n_pad, _ = kc.shape
    n_sel = T // SEL_BLOCK
    tq = _tile(T, 512)
    nq = T // tq
    gw = NSA_HPG * HEAD_DIM
    nb = min(CMP_BUCKETS, nq)
    assert nq % nb == 0
    per = nq // nb
    out_shape = [jax.ShapeDtypeStruct((T, NSA_HEADS * HEAD_DIM), BF16), jax.ShapeDtypeStruct((G, n_sel, T), F32)]
    buffers = ()
    for b in range(nb):
        i0 = b * per
        t_max = (i0 + per) * tq - 1
        n_len = min(n_pad, _round_up(max(t_max - (CMP_LEN - 1), 0) // CMP_STRIDE + 1, LANE))
        s_len = min(n_sel, _round_up(t_max // SEL_BLOCK + 1, 8))
        kern = functools.partial(_cmp_kernel, tq=tq, i0=i0, n_pad=n_len, n_sel=s_len, top_k=min(SEL_TOPK, n_sel))
        buffers = pl.pallas_call(
            kern, grid=(G, per),
            in_specs=[pl.BlockSpec((tq, gw), lambda g, i, i0=i0: (i0 + i, g)),
                      pl.BlockSpec((1, n_len, HEAD_DIM), lambda g, i: (g, 0, 0)),
                      pl.BlockSpec((1, HEAD_DIM, n_len), lambda g, i: (g, 0, 0)),
                      pl.BlockSpec((s_len, n_len), lambda g, i: (0, 0)),
                      pl.BlockSpec((1, 1, 8, tq), lambda g, i, i0=i0: (0, g, 0, i0 + i))]
            + [pl.BlockSpec(memory_space=pl.ANY)] * len(buffers),
            out_specs=[pl.BlockSpec((tq, gw), lambda g, i, i0=i0: (i0 + i, g)),
                       pl.BlockSpec((1, n_sel, tq), lambda g, i, i0=i0: (g, 0, i0 + i))],
            out_shape=out_shape,
            input_output_aliases={5: 0, 6: 1} if buffers else {},
            compiler_params=_params(("parallel", "parallel"), 48), name="cmp_select_%d" % b,
        )(qn, kc, vcT, ovT, gates, *buffers)
    return buffers


def _flash_kernel(qi_ref, kj_ref, fl_ref, q_ref, k_ref, vT_ref, g_ref, sh_ref, *rest, mode, bounded, tq, tk):
    if mode == "sel":
        bias_ref, o_ref, m_sc, acc_sc = rest
    else:
        o_ref, m_sc, acc_sc = rest
    p = pl.program_id(1)
    i = qi_ref[p]
    j = kj_ref[p]
    fl = fl_ref[p]

    @pl.when((fl & 1) != 0)
    def _():
        m_sc[...] = jnp.full(m_sc.shape, NEG_BIG, F32)
        acc_sc[...] = jnp.zeros(acc_sc.shape, F32)

    shift = sh_ref[0, 0:1, :] if bounded else 0.0
    nb = tk // SEL_BLOCK
    if mode == "sel":
        rows = bias_ref[0] - shift

    def key_mask():
        t = i * tq + lax.broadcasted_iota(jnp.int32, (1, tq), 1)
        kpos = j * tk + lax.broadcasted_iota(jnp.int32, (tk, 1), 0)
        if mode == "sel":
            base = jnp.concatenate(
                [jnp.broadcast_to(rows[b:b + 1, :], (SEL_BLOCK, tq)) for b in range(nb)], axis=0)
            return jnp.where(kpos <= t, base, NEG_BIG)
        return jnp.where((kpos <= t) & (kpos > t - WINDOW), 0.0 - shift, NEG_BIG)

    k = k_ref[...]
    vT = vT_ref[...]
    raw = lambda h: lax.dot_general(k, q_ref[:, h * HEAD_DIM:(h + 1) * HEAD_DIM], _NT, preferred_element_type=F32)

    def accumulate(pTs):
        for h in range(NSA_HPG):
            acc_sc[h] = acc_sc[h] + jnp.dot(vT, pTs[h], preferred_element_type=F32)

    if bounded and mode == "sel":
        @pl.when((fl & 4) != 0)
        def _():
            mask_add = key_mask()
            accumulate([jnp.exp2(raw(h) + mask_add).astype(BF16) for h in range(NSA_HPG)])

        @pl.when((fl & 4) == 0)
        def _():
            def probs(h):
                s = raw(h).reshape(nb, SEL_BLOCK, tq) + rows[:, None, :]
                return jnp.exp2(s).astype(BF16).reshape(tk, tq)
            accumulate([probs(h) for h in range(NSA_HPG)])
    elif bounded:
        mask_add = key_mask()
        accumulate([jnp.exp2(raw(h) + mask_add).astype(BF16) for h in range(NSA_HPG)])
    else:
        mask_add = key_mask()
        sTs = [raw(h) + mask_add for h in range(NSA_HPG)]
        pTs, alphas = [], []
        for h in range(NSA_HPG):
            sT = sTs[h]
            m_old = m_sc[h]
            m_new = jnp.maximum(m_old, jnp.max(sT, axis=0, keepdims=True))
            alphas.append(jnp.exp2(m_old - m_new))
            pTs.append(jnp.exp2((sT - m_new).astype(BF16)))
            m_sc[h] = m_new
        for h in range(NSA_HPG):
            acc_sc[h] = alphas[h] * acc_sc[h] + jnp.dot(vT, pTs[h], preferred_element_type=F32)

    @pl.when((fl & 2) != 0)
    def _():
        for h in range(NSA_HPG):
            acc = acc_sc[h]
            l = acc[HEAD_DIM:HEAD_DIM + 1, :]
            o = acc[:HEAD_DIM, :] * ((1.0 / l) * g_ref[0, 0, h:h + 1, :])
            o_ref[:, h * HEAD_DIM:(h + 1) * HEAD_DIM] = o.T.astype(BF16)


def _steps(nq, lo_fn, hi_fn, reverse=False):
    qi, kj, fl = [], [], []
    for i in range(nq):
        js = list(range(lo_fn(i), hi_fn(i) + 1))
        if reverse:
            js = js[::-1]
        for n, j in enumerate(js):
            qi.append(i)
            kj.append(j)
            fl.append((1 if n == 0 else 0) | (2 if n == len(js) - 1 else 0))
    return (jnp.asarray(np.array(qi, np.int32)), jnp.asarray(np.array(kj, np.int32)),
            jnp.asarray(np.array(fl, np.int32)))


def _nsa_flash(mode, qn, k, vT, gates, shift, bias=None):
    use_bounded = jnp.max(shift) <= SOFTMAX_BOUND_MAX
    return lax.cond(use_bounded,
                    lambda: _nsa_flash_call(mode, True, qn, k, vT, gates, shift, bias),
                    lambda: _nsa_flash_call(mode, False, qn, k, vT, gates, shift, bias))


def _nsa_flash_call(mode, bounded, qn, k, vT, gates, shift, bias):
    T = qn.shape[0]
    tq = _tile(T, 1024 if mode == "sel" else WINDOW)
    tk = _tile(T, 1024 if mode == "sel" else 512)
    nq = T // tq
    gw = NSA_HPG * HEAD_DIM
    hi = lambda i: ((i + 1) * tq - 1) // tk
    if mode == "sel":
        lo = lambda i: 0
        br = 1
    else:
        lo = lambda i: max(0, (i * tq - (WINDOW - 1)) // tk)
        br = 2
    qi, kj, fl = _steps(nq, lo, hi)
    fl = fl | 4 * ((kj + 1) * tk - 1 > qi * tq).astype(jnp.int32)
    in_specs = [pl.BlockSpec((tq, gw), lambda g, p, qi, kj, fl: (qi[p], g)),
                pl.BlockSpec((tk, HEAD_DIM), lambda g, p, qi, kj, fl: (kj[p], g)),
                pl.BlockSpec((VAUG, tk), lambda g, p, qi, kj, fl: (g, kj[p])),
                pl.BlockSpec((1, 1, 8, tq), lambda g, p, qi, kj, fl, br=br: (br, g, 0, qi[p])),
                pl.BlockSpec((1, 8, tq), lambda g, p, qi, kj, fl: (g, 0, qi[p]))]
    args = [qn, k, vT, gates, shift]
    if mode == "sel":
        in_specs.append(pl.BlockSpec((1, tk // SEL_BLOCK, tq), lambda g, p, qi, kj, fl: (g, kj[p], qi[p])))
        args.append(bias)
    kern = functools.partial(_flash_kernel, mode=mode, bounded=bounded, tq=tq, tk=tk)
    return pl.pallas_call(
        kern,
        grid_spec=pltpu.PrefetchScalarGridSpec(
            num_scalar_prefetch=3, grid=(NSA_GROUPS, int(qi.shape[0])),
            in_specs=in_specs,
            out_specs=pl.BlockSpec((tq, gw), lambda g, p, qi, kj, fl: (qi[p], g)),
            scratch_shapes=[pltpu.VMEM((NSA_HPG, 1, tq), F32), pltpu.VMEM((NSA_HPG, VAUG, tq), F32)]),
        out_shape=jax.ShapeDtypeStruct((T, NSA_HEADS * HEAD_DIM), BF16),
        compiler_params=_params(("parallel", "arbitrary"), 48),
        name="nsa_" + mode + ("_bounded" if bounded else "_online"),
    )(qi, kj, fl, *args)


def _sb_kernel(qi_ref, kj_ref, fl_ref, q_ref, k_ref, vT_ref, lm_ref, *rest, tq, tk, resume):
    if resume:
        acc_in, carry_in, o_ref, carry_sc, acc_sc = rest
    else:
        o_ref, acc_o, carry_o, carry_sc, acc_sc = rest
    p = pl.program_id(0)
    i = qi_ref[p]
    j = kj_ref[p]
    fl = fl_ref[p]

    @pl.when((fl & 1) != 0)
    def _():
        if resume:
            for h in range(SB_HEADS):
                carry_sc[h] = carry_in[h:h + 1, :]
                acc_sc[h] = acc_in[h * HEAD_DIM:(h + 1) * HEAD_DIM, :]
        else:
            carry_sc[...] = jnp.zeros(carry_sc.shape, F32)
            acc_sc[...] = jnp.zeros(acc_sc.shape, F32)

    @pl.when((jnp.max(carry_sc[...]) > SB_DEAD_LOG2) & ((fl & 4) == 0))
    def _():
        t = i * tq + lax.broadcasted_iota(jnp.int32, (1, tq), 1)
        kpos = j * tk + lax.broadcasted_iota(jnp.int32, (tk, 1), 0)
        past = kpos < t
        lm = lm_ref[...]
        hs = [slice(h * HEAD_DIM, (h + 1) * HEAD_DIM) for h in range(SB_HEADS)]
        zs = [lax.dot_general(k_ref[:, sl], q_ref[:, sl], _NT, preferred_element_type=F32) for sl in hs]
        lgs, css = [], []
        for h in range(SB_HEADS):
            z = zs[h]
            sp = jnp.maximum(z, 0.0) + jnp.log2(1.0 + jnp.exp2(-jnp.abs(z)))
            lk = jnp.where(past, -sp, 0.0)
            hi = lk.astype(BF16)
            lo = (lk - hi.astype(F32)).astype(BF16)
            css.append(jnp.dot(lm, jnp.concatenate([hi, lo], axis=1), preferred_element_type=F32))
            lgs.append(z - sp)
            carry_old = carry_sc[h]
            carry_sc[h] = carry_old + jnp.sum(lk, axis=0, keepdims=True)
            css[h] = css[h][:, :tq] + css[h][:, tq:] + carry_old
        for h in range(SB_HEADS):
            wgt = jnp.where(past, jnp.exp2(lgs[h] + css[h]), 0.0)
            acc_sc[h] = acc_sc[h] + jnp.dot(vT_ref[hs[h], :], wgt.astype(BF16), preferred_element_type=F32)

    @pl.when((fl & 2) != 0)
    def _():
        for h in range(SB_HEADS):
            o_ref[:, h * HEAD_DIM:(h + 1) * HEAD_DIM] = acc_sc[h].T.astype(BF16)
            if not resume:
                acc_o[h * HEAD_DIM:(h + 1) * HEAD_DIM, :] = acc_sc[h]
                carry_o[h:h + 1, :] = carry_sc[h]
        if not resume:
            carry_o[SB_HEADS:, :] = jnp.zeros((8 - SB_HEADS, tq), F32)


def _sb_call(q, k, kcol, vT, steps, tq, tk, state=None):
    T, W = q.shape
    qi, kj, fl = steps
    lmat = jnp.asarray(np.triu(np.ones((tk, tk), np.float32), 1), BF16)
    resume = state is not None
    qtile = lambda shape: pl.BlockSpec(shape, lambda p, qi, kj, fl: (qi[p], 0))
    qtileT = lambda rows: pl.BlockSpec((rows, tq), lambda p, qi, kj, fl: (0, qi[p]))
    in_specs = [qtile((tq, W)),
                pl.BlockSpec((tk, W), lambda p, qi, kj, fl: (kj[p], kcol)),
                pl.BlockSpec((W, tk), lambda p, qi, kj, fl: (0, kj[p])),
                pl.BlockSpec((tk, tk), lambda p, qi, kj, fl: (0, 0))]
    y_sds = jax.ShapeDtypeStruct((T, W), BF16)
    if resume:
        in_specs += [qtileT(W), qtileT(8)]
        out_specs, out_shape = qtile((tq, W)), y_sds
    else:
        out_specs = [qtile((tq, W)), qtileT(W), qtileT(8)]
        out_shape = [y_sds, jax.ShapeDtypeStruct((W, T), F32), jax.ShapeDtypeStruct((8, T), F32)]
    return pl.pallas_call(
        functools.partial(_sb_kernel, tq=tq, tk=tk, resume=resume),
        grid_spec=pltpu.PrefetchScalarGridSpec(
            num_scalar_prefetch=3, grid=(int(qi.shape[0]),),
            in_specs=in_specs, out_specs=out_specs,
            scratch_shapes=[pltpu.VMEM((SB_HEADS, 1, tq), F32), pltpu.VMEM((SB_HEADS, HEAD_DIM, tq), F32)]),
        out_shape=out_shape,
        compiler_params=_params(("arbitrary",), 48), name="sb_far" if resume else "sb_near",
    )(qi, kj, fl, q, k, vT, lmat, *(state or ()))


def _stick_breaking(q, k, kcol, vT):
    T, W = q.shape
    tq = _tile(T, 256)
    tk = _tile(T, 256)
    nq = T // tq
    hi = lambda i: ((i + 1) * tq - 2) // tk
    lo_near = lambda i: max(0, hi(i) - SB_NEAR_TILES + 1)
    near = _steps(nq, lo_near, hi, reverse=True)
    y_near, acc, carry = _sb_call(q, k, kcol, vT, near, tq, tk)
    qi, kj, fl = [], [], []
    for i in range(nq):
        js = list(range(lo_near(i) - 1, -1, -1))
        for n, j in enumerate(js or [0]):
            qi.append(i)
            kj.append(j)
            fl.append((1 if n == 0 else 0) | (2 if n == max(len(js), 1) - 1 else 0) | (0 if js else 4))
    far = tuple(jnp.asarray(np.array(a, np.int32)) for a in (qi, kj, fl))
    with_far = [i for i in range(nq) if lo_near(i) > 0]
    if not with_far:
        return y_near
    alive = jnp.max(carry[:SB_HEADS, with_far[0] * tq:]) > SB_DEAD_LOG2
    return lax.cond(alive, lambda: _sb_call(q, k, kcol, vT, far, tq, tk, state=(acc, carry)), lambda: y_near)


def _memattn_kernel(q_ref, k_ref, vT_ref, o_ref):
    for h in range(MEM_HEADS):
        sl = slice(h * HEAD_DIM, (h + 1) * HEAD_DIM)
        sT = lax.dot_general(k_ref[:, sl], q_ref[:, sl], _NT, preferred_element_type=F32)
        e = jnp.exp2(sT - jnp.max(sT, axis=0, keepdims=True))
        l = jnp.sum(e, axis=0, keepdims=True)
        oT = jnp.dot(vT_ref[sl, :], e.astype(BF16), preferred_element_type=F32) * (1.0 / l)
        o_ref[:, sl] = oT.T.astype(BF16)


def _mem_attention(q, k, vT):
    T, W = q.shape
    M = k.shape[0]
    tq = _tile(T, 1024)
    return pl.pallas_call(
        _memattn_kernel, grid=(T // tq,),
        in_specs=[pl.BlockSpec((tq, W), lambda i: (i, 0)),
                  pl.BlockSpec((M, W), lambda i: (0, 0)),
                  pl.BlockSpec((W, M), lambda i: (0, 0))],
        out_specs=pl.BlockSpec((tq, W), lambda i: (i, 0)),
        out_shape=jax.ShapeDtypeStruct((T, W), BF16),
        compiler_params=_params(("parallel",), 32), name="mem_attention",
    )(q, k, vT)


def _mixout_kernel(yc_ref, ys_ref, yw_ref, ysb_ref, ym_ref, g_ref, wn_ref, wsb_ref, wm_ref, wo_ref, x_ref, gain_ref,
                   x2_o, h2_o):
    D = x_ref.shape[1]
    yn = (yc_ref[...].astype(F32) + ys_ref[...].astype(F32) + yw_ref[...].astype(F32)).astype(BF16)
    a = jnp.dot(yn, wn_ref[...], preferred_element_type=F32)
    b = jnp.dot(ysb_ref[...], wsb_ref[...], preferred_element_type=F32)
    c = jnp.dot(ym_ref[...], wm_ref[...], preferred_element_type=F32)
    sig = lambda n: jax.nn.sigmoid(g_ref[:, n * D:(n + 1) * D].astype(F32))
    mixed = (sig(0) * a + sig(1) * b + sig(2) * c).astype(BF16)
    x2 = x_ref[...] + jnp.dot(mixed, wo_ref[...], preferred_element_type=F32)
    x2_o[...] = x2
    h2_o[...] = _rms(x2, gain_ref[...]).astype(BF16)


def _mix_out(yc, ys, yw, ysb, ym, P, wn, wsb, wm, wo, x, gain):
    T, D = x.shape
    tm = _tile(T, 256)
    rowi = lambda w: pl.BlockSpec((tm, w), lambda i: (i, 0))
    res = lambda a: pl.BlockSpec(a.shape, lambda i: (0, 0), pipeline_mode=pl.Buffered(1))
    return pl.pallas_call(
        _mixout_kernel, grid=(T // tm,),
        in_specs=[rowi(yc.shape[1]), rowi(ys.shape[1]), rowi(yw.shape[1]), rowi(ysb.shape[1]), rowi(ym.shape[1]),
                  rowi(N_BRANCH * D), res(wn), res(wsb), res(wm), res(wo), rowi(D), res(gain)],
        out_specs=[rowi(D), rowi(D)],
        out_shape=[jax.ShapeDtypeStruct((T, D), F32), jax.ShapeDtypeStruct((T, D), BF16)],
        compiler_params=_params(("parallel",), 48), name="mix_out",
    )(yc, ys, yw, ysb, ym, P, wn, wsb, wm, wo, x, gain)


def _ffn_kernel(h_ref, wg_ref, wu_ref, wd_ref, x_ref, o_ref, z_sc, *, nf):
    f = pl.program_id(1)

    def up():
        a = jnp.dot(h_ref[...], wg_ref[...], preferred_element_type=F32)
        b = jnp.dot(h_ref[...], wu_ref[...], preferred_element_type=F32)
        return (a * jax.nn.sigmoid(a) * b).astype(BF16)

    def down():
        return jnp.dot(z_sc[...], wd_ref[...], preferred_element_type=F32)

    @pl.when(f == 0)
    def _():
        z_sc[...] = up()

    @pl.when((f > 0) & (f < nf))
    def _():
        c = down()
        z_new = up()
        o_ref[...] = jnp.where(f == 1, x_ref[...], o_ref[...]) + c
        z_sc[...] = z_new

    @pl.when(f == nf)
    def _():
        o_ref[...] += down()


def _ffn(h2, wg, wu, wd, x2):
    T, D = x2.shape
    F = wg.shape[1]
    tm = _tile(T, 1024)
    tf = 512
    assert F % tf == 0
    nf = F // tf
    assert nf >= 2
    return pl.pallas_call(
        functools.partial(_ffn_kernel, nf=nf), grid=(T // tm, nf + 1),
        in_specs=[pl.BlockSpec((tm, D), lambda i, f: (i, 0)),
                  pl.BlockSpec((D, tf), lambda i, f: (0, jnp.minimum(f, nf - 1))),
                  pl.BlockSpec((D, tf), lambda i, f: (0, jnp.minimum(f, nf - 1))),
                  pl.BlockSpec((tf, D), lambda i, f: (jnp.maximum(f - 1, 0), 0)),
                  pl.BlockSpec((tm, D), lambda i, f: (i, 0), pipeline_mode=pl.Buffered(1))],
        out_specs=pl.BlockSpec((tm, D), lambda i, f: (i, 0)),
        out_shape=jax.ShapeDtypeStruct((T, D), F32),
        scratch_shapes=[pltpu.VMEM((tm, tf), BF16)],
        compiler_params=_params(("parallel", "arbitrary"), 56), name="ffn",
    )(h2, wg.astype(BF16), wu.astype(BF16), wd.astype(BF16), x2)


def _layer(x, mem, pos_col, posc_col, consts, layer, attn_norm, w_in_all, nsa_q_norm, nsa_kc_norm, nsa_ks_norm, nsa_kw_norm,
           cmp_k_pe, cmp_k_w1, cmp_k_w2, cmp_v_pe, cmp_v_w1, cmp_v_w2, mem_norm, w_mem_kv,
           mem_q_norm, mem_k_norm, w_o_nsa, w_o_sb, w_o_mem, w_out, ffn_norm,
           w_ffn_gate, w_ffn_up, w_ffn_down):
    T, D = x.shape
    inv2, sgn, ovT = consts
    hd = HEAD_DIM
    row = lambda g: g.reshape(1, -1)

    q_w, kv_w, gn_w = NSA_HEADS * hd, 6 * NSA_GROUPS * hd, 3 * NSA_HEADS
    sb_w, mq_w, gm_w = 3 * SB_HEADS * hd, MEM_HEADS * hd, N_BRANCH * D
    o_gn = q_w + kv_w
    o_sb = o_gn + gn_w
    o_mq = o_sb + sb_w
    o_gm = o_mq + mq_w
    assert w_in_all.shape[2] == o_gm + gm_w
    wt = jnp.swapaxes(w_in_all[layer], 0, 1)
    wt_main = jnp.concatenate([wt[o_gm:], wt[:o_gn], wt[o_sb:o_gm]], axis=0).astype(BF16)
    wt_gate = jnp.pad(wt[o_gn:o_sb], ((0, LANE - gn_w), (0, 0))).astype(BF16)
    P, Pg = _in_proj(x, row(attn_norm), wt_main, wt_gate)

    (qn, ksn, kwn, vsT, vwT, sbq, sbvT, memq, gT, qn2, kn2) = _prep(
        P, Pg, gm_w, pos_col, inv2, sgn, row(nsa_q_norm), row(nsa_ks_norm), row(nsa_kw_norm), row(mem_q_norm))

    q2 = jnp.max(qn2[::8].reshape(NSA_GROUPS, NSA_HPG, T), axis=1)
    k2 = jnp.max(kn2[::8], axis=1).reshape(2, NSA_GROUPS)
    bound = lambda kk: jnp.broadcast_to((1.02 * jnp.sqrt(q2 * kk[:, None]))[:, None, :], (NSA_GROUPS, 8, T))
    shift_sel, shift_win = bound(k2[0]), bound(k2[1])

    gates = gT[:gn_w].reshape(NSA_GROUPS, NSA_HPG, 3, T).transpose(2, 0, 1, 3)
    gates = jnp.pad(gates, ((0, 0), (0, 0), (0, 8 - NSA_HPG), (0, 0)))

    n_pad = T // CMP_STRIDE
    half = CMP_LEN // 2

    def w1_pack(w1):
        return jnp.concatenate([w1[:half].reshape(half * hd, -1), w1[half:].reshape(half * hd, -1)], axis=1).astype(BF16)

    def pe_pack(pe):
        return jnp.pad(pe.reshape(2, half * hd), ((0, 6), (0, 0))).astype(BF16)

    def chunks(col):
        return (P[:, col:col + NSA_GROUPS * hd].reshape(T, NSA_GROUPS, hd).transpose(1, 0, 2)
                .reshape(NSA_GROUPS, n_pad, CMP_STRIDE * hd))

    kc, vcT = _compress(
        chunks(gm_w + q_w), chunks(gm_w + q_w + NSA_GROUPS * hd),
        w1_pack(cmp_k_w1), w1_pack(cmp_v_w1), pe_pack(cmp_k_pe), pe_pack(cmp_v_pe),
        cmp_k_w2.astype(BF16), cmp_v_w2.astype(BF16), row(nsa_kc_norm), posc_col, inv2, sgn)

    y_cmp, bias = _cmp_select(qn, kc, vcT, ovT, gates)
    y_sel = _nsa_flash("sel", qn, ksn, vsT, gates, shift_sel, bias)
    y_win = _nsa_flash("win", qn, kwn, vwT, gates, shift_win)
    sbk_col = gm_w + q_w + kv_w + SB_HEADS * hd
    assert sbk_col % (SB_HEADS * hd) == 0
    y_sb = _stick_breaking(sbq, P, sbk_col // (SB_HEADS * hd), sbvT)

    mk, mvT = _memkv(mem, row(mem_norm), w_mem_kv.astype(BF16), row(mem_k_norm))
    y_mem = _mem_attention(memq, mk, mvT)

    x2, h2 = _mix_out(y_cmp, y_sel, y_win, y_sb, y_mem, P, w_o_nsa.astype(BF16), w_o_sb.astype(BF16),
                      w_o_mem.astype(BF16), w_out.astype(BF16), x, row(ffn_norm))
    return _ffn(h2, w_ffn_gate, w_ffn_up, w_ffn_down, x2)


def kernel(x, mem, positions, attn_norm, w_in, nsa_q_norm, nsa_kc_norm, nsa_ks_norm, nsa_kw_norm, cmp_k_pe, cmp_k_w1, cmp_k_w2, cmp_v_pe, cmp_v_w1, cmp_v_w2, mem_norm, w_mem_kv, mem_q_norm, mem_k_norm, w_o_nsa, w_o_sb, w_o_mem, w_out, ffn_norm, w_ffn_gate, w_ffn_up, w_ffn_down):
    B, T, D = x.shape
    assert T % (4 * LANE) == 0 and T // SEL_BLOCK >= 8
    n_pad = T // CMP_STRIDE
    n_sel = T // SEL_BLOCK
    inv = 1.0 / (ROPE_THETA ** (jnp.arange(0, HEAD_DIM, 2, dtype=F32) / HEAD_DIM))
    inv2 = jnp.concatenate([inv, inv]).reshape(1, HEAD_DIM)
    sgn = jnp.concatenate([-jnp.ones((HEAD_DIM // 2,), F32), jnp.ones((HEAD_DIM // 2,), F32)]).reshape(1, HEAD_DIM)
    cs = np.arange(n_pad)[None, :] * CMP_STRIDE
    ss = np.arange(n_sel)[:, None] * SEL_BLOCK
    ovT = jnp.asarray(((cs < ss + SEL_BLOCK) & (cs + CMP_LEN - 1 >= ss)).astype(np.float32), BF16)
    consts = (inv2, sgn, ovT)
    depth = w_in.shape[0]
    outs = []
    for b in range(B):
        xb = x[b]
        posf = positions[b].astype(F32)
        pos_col = posf.reshape(T, 1)
        posc = jnp.concatenate([posf[CMP_LEN - 1::CMP_STRIDE], posf[-1:]]).reshape(n_pad, 1)
        for l in range(depth):
            xb = _layer(xb, mem[b], pos_col, posc, consts, l, attn_norm[l], w_in, nsa_q_norm[l], nsa_kc_norm[l],
                        nsa_ks_norm[l], nsa_kw_norm[l], cmp_k_pe[l], cmp_k_w1[l], cmp_k_w2[l], cmp_v_pe[l],
                        cmp_v_w1[l], cmp_v_w2[l], mem_norm[l], w_mem_kv[l], mem_q_norm[l], mem_k_norm[l],
                        w_o_nsa[l], w_o_sb[l], w_o_mem[l], w_out[l], ffn_norm[l],
                        w_ffn_gate[l], w_ffn_up[l], w_ffn_down[l])
        outs.append(xb)
    return outs[0][None] if B == 1 else jnp.stack(outs, axis=0)
```
